```python
import jax, jax.numpy as jnp
from jax import lax
import numpy as np

D_MODEL = 1024
BATCH = 8
SEQ = 2048
DEPTH = 2
DEC_BATCH = 128
DEC_SEQ = 1
PAST_LEN = 16384
PAGE_SIZE = 128

D_MIX = D_MODEL
POOL_W = D_MIX // 2
POOL_WINDOWS = (2, 4, 8, 16)
N_POOL_GROUPS = len(POOL_WINDOWS)
POOL_GC = POOL_W // N_POOL_GROUPS
POOL_STATE = max(POOL_WINDOWS) - 1
HEAD_DIM = 64
N_HEADS = (D_MIX - POOL_W) // HEAD_DIM
N_KV_HEADS = 2
GQA_GROUP = N_HEADS // N_KV_HEADS
Q_W = N_HEADS * HEAD_DIM
KV_W = N_KV_HEADS * HEAD_DIM
D_IN = POOL_W + Q_W + 2 * KV_W
WINDOW = 128
ATTN_BLOCK = WINDOW
ATTN_SCALE = HEAD_DIM ** -0.5
N_EXPERT_GROUPS = 4
EXPERTS_PER_GROUP = 8
N_EXPERTS = N_EXPERT_GROUPS * EXPERTS_PER_GROUP
TOP_K = 2
EXPERT_FF = D_MODEL // 2
MOE_BLOCK = 128
EPS = 1e-6

kernel_name = 'hymba_pool_swa_hier_moe_step'


def _rmsnorm(x, g):
    xf = x.astype(jnp.float32)
    y = xf * lax.rsqrt(jnp.mean(xf * xf, axis=-1, keepdims=True) + EPS)
    return (y * g.astype(jnp.float32)).astype(x.dtype)


def _alibi_slopes():
    h = jnp.arange(1, N_HEADS + 1, dtype=jnp.float32)
    return jnp.exp2(-8.0 * h / N_HEADS).reshape(N_KV_HEADS, GQA_GROUP)


def _project(h, w_in, q_g, k_g):
    n, t, _ = h.shape
    z = jnp.einsum('ntd,de->nte', h, w_in)
    u = z[..., :POOL_W]
    q = z[..., POOL_W:POOL_W + Q_W].reshape(n, t, N_KV_HEADS, GQA_GROUP, HEAD_DIM)
    k = z[..., POOL_W + Q_W:POOL_W + Q_W + KV_W].reshape(n, t, N_KV_HEADS, HEAD_DIM)
    v = z[..., POOL_W + Q_W + KV_W:].reshape(n, t, N_KV_HEADS, HEAD_DIM)
    return u, _rmsnorm(q, q_g), _rmsnorm(k, k_g), v


def _pool_mixer(u, prefix, pos0, w_pool, scale):
    t_len = u.shape[1]
    full = jnp.concatenate([prefix, u], axis=1)
    csum = jnp.pad(jnp.cumsum(full.astype(jnp.float32), axis=1), ((0, 0), (1, 0), (0, 0)))
    pos = pos0 + jnp.arange(t_len, dtype=jnp.int32)
    uf = u.astype(jnp.float32)
    diffs = []
    for g, w in enumerate(POOL_WINDOWS):
        sl = slice(g * POOL_GC, (g + 1) * POOL_GC)
        hi = csum[:, POOL_STATE + 1:POOL_STATE + 1 + t_len, sl]
        lo = csum[:, POOL_STATE + 1 - w:POOL_STATE + 1 - w + t_len, sl]
        count = jnp.minimum(pos + 1, w).astype(jnp.float32)[None, :, None]
        diffs.append((hi - lo) / count - uf[..., sl])
    d = jnp.stack(diffs, axis=2).astype(u.dtype)
    y = jnp.einsum('ntgc,gce->ntge', d, w_pool).reshape(u.shape)
    return y * scale, full[:, -POOL_STATE:]


def _window_attn_core(q, k, v, qpos, kpos, slopes, sinks):
    s = jnp.einsum('nqkgd,nskd->nkgqs', q.astype(jnp.float32), k.astype(jnp.float32)) * ATTN_SCALE
    dist = qpos[:, :, None] - kpos[:, None, :]
    valid = (dist >= 0) & (dist < WINDOW) & (kpos[:, None, :] >= 0)
    s = s - slopes[None, :, :, None, None] * dist[:, None, None].astype(jnp.float32)
    s = jnp.where(valid[:, None, None], s, -jnp.inf)
    sink = sinks.astype(jnp.float32).reshape(N_KV_HEADS, GQA_GROUP)[None, :, :, None, None]
    m = jnp.maximum(jnp.max(s, axis=-1, keepdims=True), sink)
    p = jnp.exp(s - m)
    denom = jnp.sum(p, axis=-1, keepdims=True) + jnp.exp(sink - m)
    o = jnp.einsum('nkgqs,nskd->nqkgd', p / denom, v.astype(jnp.float32))
    return o.astype(q.dtype)


def _prompt_window_attn(q, k, v, slopes, sinks):
    n, t_len = q.shape[:2]
    nb = t_len // ATTN_BLOCK

    def cur(a):
        return a.reshape((n * nb, ATTN_BLOCK) + a.shape[2:])

    def prev(a):
        pad = ((0, 0), (ATTN_BLOCK, 0)) + ((0, 0),) * (a.ndim - 2)
        return cur(jnp.pad(a, pad)[:, :t_len])

    kb = jnp.concatenate([prev(k), cur(k)], axis=1)
    vb = jnp.concatenate([prev(v), cur(v)], axis=1)
    pos = jnp.arange(t_len, dtype=jnp.int32).reshape(nb, ATTN_BLOCK)
    qpos = jnp.tile(pos, (n, 1))
    kpos = jnp.tile(jnp.concatenate([pos - ATTN_BLOCK, pos], axis=1), (n, 1))
    o = _window_attn_core(cur(q), kb, vb, qpos, kpos, slopes, sinks)
    return o.reshape(q.shape)


def _merge(pool_out, attn_out, w_out):
    n, t = pool_out.shape[:2]
    cat = jnp.concatenate([pool_out, attn_out.reshape(n, t, Q_W)], axis=-1)
    return jnp.einsum('nte,ed->ntd', cat, w_out)


def _hier_moe(h, rg_w, rg_b, re_w, re_b, w_gate, w_up, w_down):
    n, t, d = h.shape
    xt = h.reshape(-1, d)
    ntok = xt.shape[0]
    rows = jnp.arange(ntok)
    gl = (xt @ rg_w).astype(jnp.float32) + rg_b.astype(jnp.float32)
    gp = jax.nn.softmax(gl, axis=-1)
    grp = jnp.argmax(gl, axis=-1)
    g_w = gp[rows, grp]
    el = ((xt @ re_w).astype(jnp.float32) + re_b.astype(jnp.float32)).reshape(ntok, N_EXPERT_GROUPS, EXPERTS_PER_GROUP)
    el_sel = el[rows, grp]
    top_v, top_i = lax.top_k(el_sel, TOP_K)
    e_w = jax.nn.softmax(top_v, axis=-1) * g_w[:, None]
    eid = grp[:, None] * EXPERTS_PER_GROUP + top_i
    n_assign = ntok * TOP_K
    flat_e = eid.reshape(-1)
    flat_w = e_w.reshape(-1)
    flat_tok = jnp.repeat(rows, TOP_K)
    order = jnp.argsort(flat_e)
    se, sw, stok = flat_e[order], flat_w[order], flat_tok[order]
    counts = jax.ops.segment_sum(jnp.ones_like(se), se, num_segments=N_EXPERTS)
    offsets = jnp.cumsum(counts) - counts
    pcounts = (counts + MOE_BLOCK - 1) // MOE_BLOCK * MOE_BLOCK
    pends = jnp.cumsum(pcounts)
    poffsets = pends - pcounts
    dest = poffsets[se] + (jnp.arange(n_assign) - offsets[se])
    n_blocks = -(-n_assign // MOE_BLOCK) + N_EXPERTS
    xd = jnp.zeros((n_blocks * MOE_BLOCK, d), xt.dtype).at[dest].set(xt[stok])
    block_e = jnp.minimum(jnp.searchsorted(pends, jnp.arange(n_blocks) * MOE_BLOCK, side='right'), N_EXPERTS - 1)

    def expert_block(args):
        xb, e = args
        gte = xb @ w_gate[e]
        upe = xb @ w_up[e]
        return (jax.nn.silu(gte) * upe) @ w_down[e]

    yd = lax.map(expert_block, (xd.reshape(n_blocks, MOE_BLOCK, d), block_e)).reshape(-1, d)
    ys = yd[dest] * sw[:, None].astype(yd.dtype)
    y = jax.ops.segment_sum(ys, stok, num_segments=ntok)
    return y.reshape(n, t, d)


def setup_inputs(seed: int = 0) -> dict:
    key = jax.random.key(seed)
    ks = jax.random.split(key, 24)
    f32 = jnp.float32

    def nrm(k, shape, s):
        return jax.random.normal(k, shape, f32) * s

    lw = min(WINDOW, PAST_LEN)
    return {
        'x_prompt': nrm(ks[0], (BATCH, SEQ, D_MODEL), 1.0),
        'x_sample': nrm(ks[1], (DEC_BATCH, DEC_SEQ, D_MODEL), 1.0),
        'state_pool': nrm(ks[2], (DEPTH, DEC_BATCH, POOL_STATE, POOL_W), 1.0),
        'cache_k_win': nrm(ks[3], (DEPTH, DEC_BATCH, lw, N_KV_HEADS, HEAD_DIM), 1.0),
        'cache_v_win': nrm(ks[4], (DEPTH, DEC_BATCH, lw, N_KV_HEADS, HEAD_DIM), 1.0),
        'norm_attn_g': 1.0 + nrm(ks[5], (DEPTH, D_MODEL), 0.05),
        'w_in': nrm(ks[6], (DEPTH, D_MODEL, D_IN), D_MODEL ** -0.5),
        'pool_w': nrm(ks[7], (DEPTH, N_POOL_GROUPS, POOL_GC, POOL_GC), POOL_GC ** -0.5),
        'pool_scale': 1.0 + nrm(ks[8], (DEPTH, POOL_W), 0.1),
        'q_norm_g': 1.0 + nrm(ks[9], (DEPTH, HEAD_DIM), 0.05),
        'k_norm_g': 1.0 + nrm(ks[10], (DEPTH, HEAD_DIM), 0.05),
        'attn_sinks': nrm(ks[11], (DEPTH, N_HEADS), 0.5),
        'w_out': nrm(ks[12], (DEPTH, D_MIX, D_MODEL), D_MIX ** -0.5),
        'norm_ffn_g': 1.0 + nrm(ks[13], (DEPTH, D_MODEL), 0.05),
        'router_group_w': nrm(ks[14], (DEPTH, D_MODEL, N_EXPERT_GROUPS), D_MODEL ** -0.5),
        'router_group_b': nrm(ks[15], (DEPTH, N_EXPERT_GROUPS), 0.01),
        'router_expert_w': nrm(ks[16], (DEPTH, D_MODEL, N_EXPERTS), D_MODEL ** -0.5),
        'router_expert_b': nrm(ks[17], (DEPTH, N_EXPERTS), 0.01),
        'w_gate': nrm(ks[18], (DEPTH, N_EXPERTS, D_MODEL, EXPERT_FF), D_MODEL ** -0.5),
        'w_up': nrm(ks[19], (DEPTH, N_EXPERTS, D_MODEL, EXPERT_FF), D_MODEL ** -0.5),
        'w_down': nrm(ks[20], (DEPTH, N_EXPERTS, EXPERT_FF, D_MODEL), EXPERT_FF ** -0.5),
    }


def reference(x_prompt, x_sample, state_pool, cache_k_win, cache_v_win,
              norm_attn_g, w_in, pool_w, pool_scale, q_norm_g, k_norm_g, attn_sinks,
              w_out, norm_ffn_g, router_group_w, router_group_b, router_expert_w,
              router_expert_b, w_gate, w_up, w_down):
    slopes = _alibi_slopes()
    xp, xs = x_prompt, x_sample
    n_p, t_p = xp.shape[:2]
    t_s = xs.shape[1]
    lw_p = min(WINDOW, t_p)
    lw_s = cache_k_win.shape[2]
    pool_p, kp_new, vp_new = [], [], []
    pool_s, ks_new, vs_new = [], [], []
    for l in range(DEPTH):
        h = _rmsnorm(xp, norm_attn_g[l])
        u, q, k, v = _project(h, w_in[l], q_norm_g[l], k_norm_g[l])
        prefix = jnp.zeros((n_p, POOL_STATE, POOL_W), u.dtype)
        p_out, p_state = _pool_mixer(u, prefix, 0, pool_w[l], pool_scale[l])
        a_out = _prompt_window_attn(q, k, v, slopes, attn_sinks[l])
        xp = xp + _merge(p_out, a_out, w_out[l])
        xp = xp + _hier_moe(_rmsnorm(xp, norm_ffn_g[l]), router_group_w[l], router_group_b[l],
                            router_expert_w[l], router_expert_b[l], w_gate[l], w_up[l], w_down[l])
        pool_p.append(p_state)
        kp_new.append(k[:, t_p - lw_p:])
        vp_new.append(v[:, t_p - lw_p:])
        h = _rmsnorm(xs, norm_attn_g[l])
        u, q, k, v = _project(h, w_in[l], q_norm_g[l], k_norm_g[l])
        p_out, p_state = _pool_mixer(u, state_pool[l], PAST_LEN, pool_w[l], pool_scale[l])
        kk = jnp.concatenate([cache_k_win[l], k], axis=1)
        vv = jnp.concatenate([cache_v_win[l], v], axis=1)
        qpos = (PAST_LEN + jnp.arange(t_s, dtype=jnp.int32))[None]
        kpos = jnp.concatenate([PAST_LEN - lw_s + jnp.arange(lw_s, dtype=jnp.int32),
                                PAST_LEN + jnp.arange(t_s, dtype=jnp.int32)])[None]
        a_out = _window_attn_core(q, kk, vv, qpos, kpos, slopes, attn_sinks[l])
        xs = xs + _merge(p_out, a_out, w_out[l])
        xs = xs + _hier_moe(_rmsnorm(xs, norm_ffn_g[l]), router_group_w[l], router_group_b[l],
                            router_expert_w[l], router_expert_b[l], w_gate[l], w_up[l], w_down[l])
        pool_s.append(p_state)
        ks_new.append(kk[:, -lw_s:])
        vs_new.append(vv[:, -lw_s:])
    return (xp, xs, jnp.stack(pool_p), jnp.stack(kp_new), jnp.stack(vp_new),
            jnp.stack(pool_s), jnp.stack(ks_new), jnp.stack(vs_new))
```

```python
import functools

import jax
import jax.numpy as jnp
from jax import lax
from jax.experimental import pallas as pl
from jax.experimental.pallas import tpu as pltpu

D_MODEL = 1024
POOL_W = 512
POOL_WINDOWS = (2, 4, 8, 16)
POOL_GC = 128
POOL_STATE = 15
HEAD_DIM = 64
N_HEADS = 8
N_KV_HEADS = 2
GQA_GROUP = 4
Q_W = 512
KV_W = 128
D_IN = POOL_W + Q_W + 2 * KV_W
WINDOW = 128
ATTN_SCALE = HEAD_DIM ** -0.5
N_EXPERT_GROUPS = 4
EXPERTS_PER_GROUP = 8
N_EXPERTS = 32
EXPERT_FF = 512
EPS = 1e-6

LANES = 128
HALO = 32
TM_PROJ = 512
MOE_BM = 256
GROUP_LANE0 = 32
VMEM_LIMIT = 48 * 1024 * 1024

BF16 = jnp.bfloat16
F32 = jnp.float32


def _segment_mean_sq(a, bd):
    w = a.shape[1]
    return jnp.dot((a * a).astype(BF16), bd[:w, :w], preferred_element_type=F32)


def _project(x, g, w_in, qg, kg, bd):
    ms = jnp.mean(x * x, axis=-1, keepdims=True)
    h = (x * lax.rsqrt(ms + EPS) * g).astype(BF16)
    z = jnp.dot(h, w_in, preferred_element_type=F32)
    u = z[:, :POOL_W]
    q = z[:, POOL_W:POOL_W + Q_W]
    k = z[:, POOL_W + Q_W:POOL_W + Q_W + KV_W]
    v = z[:, POOL_W + Q_W + KV_W:]
    qn = []
    for c in range(Q_W // 256):
        qc = q[:, c * 256:(c + 1) * 256]
        qn.append(qc * lax.rsqrt(_segment_mean_sq(qc, bd) + EPS))
    qn = jnp.concatenate(qn, axis=-1) * qg
    kn = k * lax.rsqrt(_segment_mean_sq(k, bd) + EPS) * kg
    return u, qn, kn, v


def _pool_project(d_groups, wp_ref, ps):
    outs = []
    for p in range(2):
        dp = jnp.concatenate([d_groups[2 * p], d_groups[2 * p + 1]], axis=-1).astype(BF16)
        y = jnp.dot(dp, wp_ref[p], preferred_element_type=F32)
        outs.append(y * ps[:, p * 256:(p + 1) * 256])
    return jnp.concatenate(outs, axis=-1)


def _proj_pool_kernel(x_ref, g_ref, win_ref, qg_ref, kg_ref, bd_ref, wp_ref, ps_ref,
                      pool_ref, q_ref, k_ref, v_ref, utail_ref,
                      ext_ref, sa_ref, sb_ref, *, tm, n_j):
    j = pl.program_id(1)
    u, qn, kn, v = _project(x_ref[...], g_ref[...], win_ref[...], qg_ref[...], kg_ref[...], bd_ref[...])
    q_ref[...] = qn.astype(BF16)
    k_ref[...] = kn
    v_ref[...] = v

    @pl.when(j == 0)
    def _():
        ext_ref[0:HALO, :] = jnp.zeros((HALO, POOL_W), F32)

    r = tm + HALO
    ext_ref[HALO:r, :] = u
    sa_ref[8:r, :] = ext_ref[8:r, :] + ext_ref[7:r - 1, :]
    sb_ref[16:r, 128:] = sa_ref[16:r, 128:] + sa_ref[14:r - 2, 128:]
    sa_ref[24:r, 256:] = sb_ref[24:r, 256:] + sb_ref[20:r - 4, 256:]
    sb_ref[32:r, 384:] = sa_ref[32:r, 384:] + sa_ref[24:r - 8, 384:]
    pos1 = j * tm + lax.broadcasted_iota(jnp.int32, (tm, POOL_GC), 0) + 1
    sums = (sa_ref, sb_ref, sa_ref, sb_ref)
    d_groups = []
    for gi, w in enumerate(POOL_WINDOWS):
        sl = slice(gi * POOL_GC, (gi + 1) * POOL_GC)
        cnt = jnp.minimum(pos1, w).astype(F32)
        d_groups.append(sums[gi][HALO:r, sl] / cnt - u[:, sl])
    pool_ref[...] = _pool_project(d_groups, wp_ref, ps_ref[...]).astype(BF16)
    ext_ref[16:HALO, :] = ext_ref[tm + 16:r, :]

    @pl.when(j == n_j - 1)
    def _():
        utail_ref[...] = u[tm - 16:, :]


def _proj_pool_prompt(l, x2d, n_seq, seq, g_attn, w_in, qg, kg, bd, wp, ps):
    tm = TM_PROJ
    n_j = seq // tm
    t = n_seq * seq
    row = lambda b, j: (b * n_j + j, 0)
    lay = lambda b, j: (l, 0, 0)
    return pl.pallas_call(
        functools.partial(_proj_pool_kernel, tm=tm, n_j=n_j),
        grid=(n_seq, n_j),
        in_specs=[
            pl.BlockSpec((tm, D_MODEL), row),
            pl.BlockSpec((None, 1, D_MODEL), lay),
            pl.BlockSpec((None, D_MODEL, D_IN), lay),
            pl.BlockSpec((None, 1, Q_W), lay),
            pl.BlockSpec((None, 1, KV_W), lay),
            pl.BlockSpec((256, 256), lambda b, j: (0, 0)),
            pl.BlockSpec((None, 2, 256, 256), lambda b, j: (l, 0, 0, 0)),
            pl.BlockSpec((None, 1, POOL_W), lay),
        ],
        out_specs=[
            pl.BlockSpec((tm, POOL_W), row),
            pl.BlockSpec((tm, Q_W), row),
            pl.BlockSpec((tm, KV_W), row),
            pl.BlockSpec((tm, KV_W), row),
            pl.BlockSpec((None, 16, POOL_W), lambda b, j: (b, 0, 0)),
        ],
        out_shape=[
            jax.ShapeDtypeStruct((t, POOL_W), BF16),
            jax.ShapeDtypeStruct((t, Q_W), BF16),
            jax.ShapeDtypeStruct((t, KV_W), F32),
            jax.ShapeDtypeStruct((t, KV_W), F32),
            jax.ShapeDtypeStruct((n_seq, 16, POOL_W), F32),
        ],
        scratch_shapes=[pltpu.VMEM((tm + HALO, POOL_W), F32)] * 3,
        compiler_params=pltpu.CompilerParams(
            dimension_semantics=("arbitrary", "arbitrary"), vmem_limit_bytes=VMEM_LIMIT),
        name="proj_pool_prompt",
    )(x2d, g_attn, w_in, qg, kg, bd, wp, ps)


def _attn_kernel(sink_ref, q_ref, kp_ref, kc_ref, vp_ref, vc_ref, bias_ref, o_ref):
    q = q_ref[...]
    kk = jnp.concatenate([kp_ref[...], kc_ref[...]], axis=0).astype(BF16)
    vv = jnp.concatenate([vp_ref[...], vc_ref[...]], axis=0).astype(BF16)
    outs = []
    for h in range(N_HEADS):
        kv = h // GQA_GROUP
        qh = q[:, h * HEAD_DIM:(h + 1) * HEAD_DIM]
        kh = kk[:, kv * HEAD_DIM:(kv + 1) * HEAD_DIM]
        s = lax.dot_general(qh, kh, (((1,), (1,)), ((), ())), preferred_element_type=F32)
        s = s + bias_ref[h]
        sink = sink_ref[h]
        m = jnp.maximum(jnp.max(s, axis=-1, keepdims=True), sink)
        p = jnp.exp(s - m)
        denom = jnp.sum(p, axis=-1, keepdims=True) + jnp.exp(sink - m)
        o = jnp.dot(p.astype(BF16), vv[:, kv * HEAD_DIM:(kv + 1) * HEAD_DIM], preferred_element_type=F32)
        outs.append(o / denom)
    o_ref[...] = jnp.concatenate(outs, axis=-1).astype(BF16)


def _attn_prompt(q, k, v, bias, sinks, n_seq, seq):
    nb = seq // WINDOW
    t = n_seq * seq
    cur = lambda b, i: (b * nb + i, 0)
    prev = lambda b, i: (b * nb + jnp.maximum(i - 1, 0), 0)
    return pl.pallas_call(
        _attn_kernel,
        grid=(n_seq, nb),
        in_specs=[
            pl.BlockSpec(memory_space=pltpu.SMEM),
            pl.BlockSpec((WINDOW, Q_W), cur),
            pl.BlockSpec((WINDOW, KV_W), prev),
            pl.BlockSpec((WINDOW, KV_W), cur),
            pl.BlockSpec((WINDOW, KV_W), prev),
            pl.BlockSpec((WINDOW, KV_W), cur),
            pl.BlockSpec((None, N_HEADS, WINDOW, 2 * WINDOW), lambda b, i: (jnp.minimum(i, 1), 0, 0, 0)),
        ],
        out_specs=pl.BlockSpec((WINDOW, Q_W), cur),
        out_shape=jax.ShapeDtypeStruct((t, Q_W), BF16),
        compiler_params=pltpu.CompilerParams(
            dimension_semantics=("arbitrary", "arbitrary"), vmem_limit_bytes=VMEM_LIMIT),
        name="attn_prompt",
    )(sinks, q, k, k, v, v, bias)


def _prompt_bias():
    r = jnp.arange(WINDOW, dtype=jnp.int32)[:, None]
    c = jnp.arange(2 * WINDOW, dtype=jnp.int32)[None, :]
    dist = WINDOW + r - c
    valid = (dist >= 0) & (dist < WINDOW)
    slopes = jnp.exp2(-8.0 * jnp.arange(1, N_HEADS + 1, dtype=F32) / N_HEADS)
    pen = -slopes[:, None, None] * dist.astype(F32)[None]
    later = jnp.where(valid[None], pen, -jnp.inf)
    first = jnp.where((valid & (c >= WINDOW))[None], pen, -jnp.inf)
    return jnp.stack([first, later])


def _sample_kernel(x_ref, g_ref, win_ref, qg_ref, kg_ref, bd_ref, wp_ref, ps_ref,
                   st_ref, ck_ref, cv_ref, sink_ref, bias_ref,
                   pool_ref, attn_ref, u_ref, k_ref, v_ref,
                   qf_ref, kn_ref, vn_ref, ao_ref, *, ns, pos0):
    u, qn, kn, v = _project(x_ref[...], g_ref[...], win_ref[...], qg_ref[...], kg_ref[...], bd_ref[...])
    u_ref[...] = u
    k_ref[...] = kn
    v_ref[...] = v
    kn_ref[...] = kn
    vn_ref[...] = v

    d_groups = []
    for gi, w in enumerate(POOL_WINDOWS):
        lo = gi * POOL_GC
        acc = u[:, lo:lo + POOL_GC]
        for back in range(1, w):
            off = (POOL_STATE - back) * POOL_W + lo
            acc = acc + st_ref[:, off:off + POOL_GC]
        d_groups.append(acc / float(min(pos0 + 1, w)) - u[:, lo:lo + POOL_GC])
    pool_ref[...] = _pool_project(d_groups, wp_ref, ps_ref[...]).astype(BF16)

    zeros = jnp.zeros((ns, HEAD_DIM), F32)
    for h in range(N_HEADS):
        piece = qn[:, h * HEAD_DIM:(h + 1) * HEAD_DIM]
        pair = [piece, zeros] if h < GQA_GROUP else [zeros, piece]
        qf_ref[h] = jnp.concatenate(pair, axis=-1)

    sub = lax.broadcasted_iota(jnp.int32, (N_HEADS, LANES), 0)
    row = lax.broadcasted_iota(jnp.int32, (WINDOW, LANES), 0)
    sink = sink_ref[...]
    bias = bias_ref[...]

    def body(n, carry):
        q3 = jnp.zeros((N_HEADS, LANES), F32)
        for h in range(N_HEADS):
            q3 = jnp.where(sub == h, qf_ref[h, pl.ds(n, 1), :], q3)
        kmat = jnp.where(row == 0, kn_ref[pl.ds(n, 1), :], ck_ref[n]).astype(BF16)
        vmat = jnp.where(row == 0, vn_ref[pl.ds(n, 1), :], cv_ref[n]).astype(BF16)
        s = lax.dot_general(q3.astype(BF16), kmat, (((1,), (1,)), ((), ())), preferred_element_type=F32)
        s = s + bias
        m = jnp.maximum(jnp.max(s, axis=-1, keepdims=True), sink)
        p = jnp.exp(s - m)
        denom = jnp.sum(p, axis=-1, keepdims=True) + jnp.exp(sink - m)
        o = jnp.dot(p.astype(BF16), vmat, preferred_element_type=F32) / denom
        pieces = []
        for h in range(N_HEADS):
            kv = h // GQA_GROUP
            pieces.append(o[h:h + 1, kv * HEAD_DIM:(kv + 1) * HEAD_DIM])
        ao_ref[pl.ds(n, 1), :] = jnp.concatenate(pieces, axis=-1)
        return carry

    lax.fori_loop(0, ns, body, 0)
    attn_ref[...] = ao_ref[...].astype(BF16)


def _sample_mixer(l, xs, g_attn, w_in, qg, kg, bd, wp, ps, state2d, ck, cv, sink8, bias_s, pos0):
    n = xs.shape[0]
    ns = 32
    row = lambda i: (i, 0)
    lay = lambda i: (l, 0, 0)
    return pl.pallas_call(
        functools.partial(_sample_kernel, ns=ns, pos0=pos0),
        grid=(n // ns,),
        in_specs=[
            pl.BlockSpec((ns, D_MODEL), row),
            pl.BlockSpec((None, 1, D_MODEL), lay),
            pl.BlockSpec((None, D_MODEL, D_IN), lay),
            pl.BlockSpec((None, 1, Q_W), lay),
            pl.BlockSpec((None, 1, KV_W), lay),
            pl.BlockSpec((256, 256), lambda i: (0, 0)),
            pl.BlockSpec((None, 2, 256, 256), lambda i: (l, 0, 0, 0)),
            pl.BlockSpec((None, 1, POOL_W), lay),
            pl.BlockSpec((None, ns, POOL_STATE * POOL_W), lambda i: (l, i, 0)),
            pl.BlockSpec((None, ns, WINDOW, KV_W), lambda i: (l, i, 0, 0)),
            pl.BlockSpec((None, ns, WINDOW, KV_W), lambda i: (l, i, 0, 0)),
            pl.BlockSpec((N_HEADS, 1), lambda i: (0, 0)),
            pl.BlockSpec((N_HEADS, WINDOW), lambda i: (0, 0)),
        ],
        out_specs=[
            pl.BlockSpec((ns, POOL_W), row),
            pl.BlockSpec((ns, Q_W), row),
            pl.BlockSpec((ns, POOL_W), row),
            pl.BlockSpec((ns, KV_W), row),
            pl.BlockSpec((ns, KV_W), row),
        ],
        out_shape=[
            jax.ShapeDtypeStruct((n, POOL_W), BF16),
            jax.ShapeDtypeStruct((n, Q_W), BF16),
            jax.ShapeDtypeStruct((n, POOL_W), F32),
            jax.ShapeDtypeStruct((n, KV_W), F32),
            jax.ShapeDtypeStruct((n, KV_W), F32),
        ],
        scratch_shapes=[
            pltpu.VMEM((N_HEADS, ns, LANES), F32),
            pltpu.VMEM((ns, KV_W), F32),
            pltpu.VMEM((ns, KV_W), F32),
            pltpu.VMEM((ns, Q_W), F32),
        ],
        compiler_params=pltpu.CompilerParams(
            dimension_semantics=("arbitrary",), vmem_limit_bytes=VMEM_LIMIT),
        name="sample_mixer",
    )(xs, g_attn, w_in, qg, kg, bd, wp, ps, state2d, ck, cv, sink8, bias_s)


def _merge_router_kernel(pool_ref, attn_ref, x_ref, wout_ref, g_ref, wr_ref, br_ref, ltri_ref, cin_ref,
                         x1_ref, h2_ref, route_ref, cnt_ref):
    i = pl.program_id(0)

    @pl.when(i == 0)
    def _():
        cnt_ref[...] = cin_ref[...]

    y = jnp.dot(pool_ref[...], wout_ref[0:POOL_W, :], preferred_element_type=F32)
    y = y + jnp.dot(attn_ref[...], wout_ref[POOL_W:, :], preferred_element_type=F32)
    x1 = x_ref[...] + y
    x1_ref[...] = x1
    ms = jnp.mean(x1 * x1, axis=-1, keepdims=True)
    h2 = (x1 * lax.rsqrt(ms + EPS) * g_ref[...]).astype(BF16)
    h2_ref[...] = h2
    logits = jnp.dot(h2, wr_ref[...], preferred_element_type=F32) + br_ref[...]

    tm = logits.shape[0]
    lane = lax.broadcasted_iota(jnp.int32, (tm, LANES), 1)
    big = jnp.int32(2 * LANES)
    neg = -jnp.inf
    is_group = (lane >= GROUP_LANE0) & (lane < GROUP_LANE0 + N_EXPERT_GROUPS)
    gl = jnp.where(is_group, logits, neg)
    gmax = jnp.max(gl, axis=-1, keepdims=True)
    grp = jnp.min(jnp.where(gl == gmax, lane, big), axis=-1, keepdims=True) - GROUP_LANE0
    g_w = 1.0 / jnp.sum(jnp.exp(gl - gmax), axis=-1, keepdims=True)
    in_grp = (lane >= grp * EXPERTS_PER_GROUP) & (lane < (grp + 1) * EXPERTS_PER_GROUP)
    el = jnp.where(in_grp, logits, neg)
    v1 = jnp.max(el, axis=-1, keepdims=True)
    i1 = jnp.min(jnp.where(el == v1, lane, big), axis=-1, keepdims=True)
    el2 = jnp.where(lane == i1, neg, el)
    v2 = jnp.max(el2, axis=-1, keepdims=True)
    i2 = jnp.min(jnp.where(el2 == v2, lane, big), axis=-1, keepdims=True)
    e21 = jnp.exp(v2 - v1)
    w1 = g_w / (1.0 + e21)
    w2 = g_w * e21 / (1.0 + e21)

    oh1 = lane == i1
    oh2 = lane == i2
    c = jnp.where(oh1 | oh2, 1.0, 0.0)
    prefix = jnp.dot(ltri_ref[...], c.astype(BF16), preferred_element_type=F32) + cnt_ref[...]
    r1 = jnp.sum(jnp.where(oh1, prefix, 0.0), axis=-1, keepdims=True)
    r2 = jnp.sum(jnp.where(oh2, prefix, 0.0), axis=-1, keepdims=True)
    cnt_ref[...] = cnt_ref[...] + jnp.sum(c, axis=0, keepdims=True)

    out = jnp.zeros((tm, LANES), F32)
    for idx, val in enumerate((i1.astype(F32), i2.astype(F32), w1, w2, r1, r2)):
        out = jnp.where(lane == idx, val, out)
    route_ref[...] = out


def _merge_router(l, pool, attn, x2d, w_out, g_ffn, wr, br, cnt_in, tm):
    t = x2d.shape[0]
    ltri = (jnp.arange(tm)[:, None] > jnp.arange(tm)[None, :]).astype(BF16)
    row = lambda i: (i, 0)
    lay = lambda i: (l, 0, 0)
    return pl.pallas_call(
        _merge_router_kernel,
        grid=(t // tm,),
        in_specs=[
            pl.BlockSpec((tm, POOL_W), row),
            pl.BlockSpec((tm, Q_W), row),
            pl.BlockSpec((tm, D_MODEL), row),
            pl.BlockSpec((None, D_MODEL, D_MODEL), lay),
            pl.BlockSpec((None, 1, D_MODEL), lay),
            pl.BlockSpec((None, D_MODEL, LANES), lay),
            pl.BlockSpec((None, 1, LANES), lay),
            pl.BlockSpec((tm, tm), lambda i: (0, 0)),
            pl.BlockSpec((1, LANES), lambda i: (0, 0)),
        ],
        out_specs=[
            pl.BlockSpec((tm, D_MODEL), row),
            pl.BlockSpec((tm, D_MODEL), row),
            pl.BlockSpec((tm, LANES), row),
            pl.BlockSpec((1, LANES), lambda i: (0, 0)),
        ],
        out_shape=[
            jax.ShapeDtypeStruct((t, D_MODEL), F32),
            jax.ShapeDtypeStruct((t, D_MODEL), BF16),
            jax.ShapeDtypeStruct((t, LANES), F32),
            jax.ShapeDtypeStruct((1, LANES), F32),
        ],
        compiler_params=pltpu.CompilerParams(
            dimension_semantics=("arbitrary",), vmem_limit_bytes=VMEM_LIMIT),
        name="merge_router",
    )(pool, attn, x2d, w_out, g_ffn, wr, br, ltri, cnt_in)


def _moe_kernel(be_ref, nb_ref, xd_ref, wg_ref, wu_ref, wd_ref, yd_ref, wg_s, wu_s, wd_s):
    i = pl.program_id(0)
    changed = (i == 0) | (be_ref[i] != be_ref[jnp.maximum(i - 1, 0)])

    @pl.when(changed)
    def _():
        wg_s[...] = wg_ref[...].astype(BF16)
        wu_s[...] = wu_ref[...].astype(BF16)
        wd_s[...] = wd_ref[...].astype(BF16)

    @pl.when(i < nb_ref[0])
    def _():
        x = xd_ref[...]
        gate = jnp.dot(x, wg_s[...], preferred_element_type=F32)
        up = jnp.dot(x, wu_s[...], preferred_element_type=F32)
        act = (gate * jax.nn.sigmoid(gate) * up).astype(BF16)
        yd_ref[...] = jnp.dot(act, wd_s[...], preferred_element_type=F32)

    @pl.when(i >= nb_ref[0])
    def _():
        yd_ref[...] = jnp.zeros(yd_ref.shape, F32)


def _moe_experts(l, block_e, n_valid, xd, w_gate, w_up, w_down):
    n_blocks = xd.shape[0] // MOE_BM
    row = lambda i, be, nb: (i, 0)
    wsel = lambda i, be, nb: (l, be[i], 0, 0)
    return pl.pallas_call(
        _moe_kernel,
        grid_spec=pltpu.PrefetchScalarGridSpec(
            num_scalar_prefetch=2,
            grid=(n_blocks,),
            in_specs=[
                pl.BlockSpec((MOE_BM, D_MODEL), row),
                pl.BlockSpec((None, None, D_MODEL, EXPERT_FF), wsel),
                pl.BlockSpec((None, None, D_MODEL, EXPERT_FF), wsel),
                pl.BlockSpec((None, None, EXPERT_FF, D_MODEL), wsel),
            ],
            out_specs=pl.BlockSpec((MOE_BM, D_MODEL), row),
            scratch_shapes=[
                pltpu.VMEM((D_MODEL, EXPERT_FF), BF16),
                pltpu.VMEM((D_MODEL, EXPERT_FF), BF16),
                pltpu.VMEM((EXPERT_FF, D_MODEL), BF16),
            ],
        ),
        out_shape=jax.ShapeDtypeStruct((n_blocks * MOE_BM, D_MODEL), F32),
        compiler_params=pltpu.CompilerParams(
            dimension_semantics=("arbitrary",), vmem_limit_bytes=VMEM_LIMIT),
        name="moe_experts",
    )(block_e, n_valid, xd, w_gate, w_up, w_down)


def _hier_moe(l, h2_all, route_all, counts, w_gate, w_up, w_down):
    t_all = h2_all.shape[0]
    n_assign = 2 * t_all
    eid = route_all[:, 0:2].astype(jnp.int32)
    ew = route_all[:, 2:4]
    rank = route_all[:, 4:6].astype(jnp.int32)
    pcounts = (counts + MOE_BM - 1) // MOE_BM * MOE_BM
    pends = jnp.cumsum(pcounts)
    poffsets = pends - pcounts
    dest = poffsets[eid] + rank
    n_blocks = -(-n_assign // MOE_BM) + N_EXPERTS
    tok = jnp.broadcast_to(jnp.arange(t_all, dtype=jnp.int32)[:, None], (t_all, 2))
    src = jnp.zeros((n_blocks * MOE_BM,), jnp.int32).at[dest.reshape(-1)].set(tok.reshape(-1))
    block_e = jnp.minimum(
        jnp.searchsorted(pends, jnp.arange(n_blocks, dtype=jnp.int32) * MOE_BM, side='right'),
        N_EXPERTS - 1).astype(jnp.int32)
    n_valid = (pends[-1:] // MOE_BM).astype(jnp.int32)
    xd = h2_all[src]
    yd = _moe_experts(l, block_e, n_valid, xd, w_gate, w_up, w_down)
    return yd[dest[:, 0]] * ew[:, 0:1] + yd[dest[:, 1]] * ew[:, 1:2]


def kernel(x_prompt, x_sample, state_pool, cache_k_win, cache_v_win, norm_attn_g, w_in, pool_w, pool_scale, q_norm_g, k_norm_g, attn_sinks, w_out, norm_ffn_g, router_group_w, router_group_b, router_expert_w, router_expert_b, w_gate, w_up, w_down):
    n_p, t_p, d = x_prompt.shape
    n_s, t_s, _ = x_sample.shape
    depth = w_in.shape[0]
    lw_s = cache_k_win.shape[2]
    assert t_s == 1 and lw_s == WINDOW and d == D_MODEL
    assert t_p % TM_PROJ == 0 and t_p >= WINDOW
    past_len = 16384

    w_in_b = w_in.astype(BF16)
    w_out_b = w_out.astype(BF16)
    g_attn = norm_attn_g.reshape(depth, 1, D_MODEL)
    g_ffn = norm_ffn_g.reshape(depth, 1, D_MODEL)
    qg = (jnp.tile(q_norm_g, (1, N_HEADS)) * ATTN_SCALE).reshape(depth, 1, Q_W)
    kg = jnp.tile(k_norm_g, (1, N_KV_HEADS)).reshape(depth, 1, KV_W)
    seg = jnp.arange(256) // HEAD_DIM
    bd = jnp.where(seg[:, None] == seg[None, :], 1.0 / HEAD_DIM, 0.0).astype(BF16)
    wp = jnp.zeros((depth, 2, 256, 256), F32)
    for p in range(2):
        wp = wp.at[:, p, :POOL_GC, :POOL_GC].set(pool_w[:, 2 * p])
        wp = wp.at[:, p, POOL_GC:, POOL_GC:].set(pool_w[:, 2 * p + 1])
    wp = wp.astype(BF16)
    ps = pool_scale.reshape(depth, 1, POOL_W)
    wr = jnp.zeros((depth, D_MODEL, LANES), F32)
    wr = wr.at[:, :, :N_EXPERTS].set(router_expert_w)
    wr = wr.at[:, :, GROUP_LANE0:GROUP_LANE0 + N_EXPERT_GROUPS].set(router_group_w)
    wr = wr.astype(BF16)
    br = jnp.zeros((depth, 1, LANES), F32)
    br = br.at[:, 0, :N_EXPERTS].set(router_expert_b)
    br = br.at[:, 0, GROUP_LANE0:GROUP_LANE0 + N_EXPERT_GROUPS].set(router_group_b)

    slopes = jnp.exp2(-8.0 * jnp.arange(1, N_HEADS + 1, dtype=F32) / N_HEADS)
    bias_p = _prompt_bias()
    jpos = jnp.arange(WINDOW, dtype=F32)
    dist_s = jnp.where(jpos == 0, 0.0, WINDOW - jpos)
    bias_s = -slopes[:, None] * dist_s[None, :]
    state2d = state_pool.reshape(depth, n_s, POOL_STATE * POOL_W)
    ck_all = cache_k_win.reshape(depth, n_s, lw_s, KV_W)
    cv_all = cache_v_win.reshape(depth, n_s, lw_s, KV_W)

    xp = x_prompt.reshape(n_p * t_p, D_MODEL)
    xs = x_sample.reshape(n_s, D_MODEL)
    lw_p = min(WINDOW, t_p)
    pool_p, kp_new, vp_new, pool_s, ks_new, vs_new = [], [], [], [], [], []
    zero_cnt = jnp.zeros((1, LANES), F32)
    for l in range(depth):
        sinks = attn_sinks[l]
        pool_o, q, k, v, utail = _proj_pool_prompt(l, xp, n_p, t_p, g_attn, w_in_b, qg, kg, bd, wp, ps)
        attn_o = _attn_prompt(q, k, v, bias_p, sinks, n_p, t_p)
        x1p, h2p, route_p, cnt_p = _merge_router(l, pool_o, attn_o, xp, w_out_b, g_ffn, wr, br, zero_cnt, TM_PROJ)
        pool_p.append(utail[:, 16 - POOL_STATE:, :])
        kp_new.append(k.reshape(n_p, t_p, N_KV_HEADS, HEAD_DIM)[:, t_p - lw_p:])
        vp_new.append(v.reshape(n_p, t_p, N_KV_HEADS, HEAD_DIM)[:, t_p - lw_p:])
        pool_so, attn_so, u_s, k_s, v_s = _sample_mixer(
            l, xs, g_attn, w_in_b, qg, kg, bd, wp, ps, state2d, ck_all, cv_all,
            sinks.reshape(N_HEADS, 1), bias_s, past_len)
        x1s, h2s, route_s, cnt_all = _merge_router(l, pool_so, attn_so, xs, w_out_b, g_ffn, wr, br, cnt_p, n_s)
        pool_s.append(jnp.concatenate([state_pool[l][:, 1:], u_s[:, None, :]], axis=1))
        ks_new.append(jnp.concatenate(
            [cache_k_win[l][:, 1:], k_s.reshape(n_s, 1, N_KV_HEADS, HEAD_DIM)], axis=1))
        vs_new.append(jnp.concatenate(
            [cache_v_win[l][:, 1:], v_s.reshape(n_s, 1, N_KV_HEADS, HEAD_DIM)], axis=1))
        h2_all = jnp.concatenate([h2p, h2s], axis=0)
        route_all = jnp.concatenate([route_p, route_s], axis=0)
        counts = cnt_all[0, :N_EXPERTS].astype(jnp.int32)
        y_all = _hier_moe(l, h2_all, route_all, counts, w_gate, w_up, w_down)
        xp = x1p + y_all[:n_p * t_p]
        xs = x1s + y_all[n_p * t_p:]
    return (xp.reshape(n_p, t_p, D_MODEL), xs.reshape(n_s, t_s, D_MODEL),
            jnp.stack(pool_p), jnp.stack(kp_new), jnp.stack(vp_new),
            jnp.stack(pool_s), jnp.stack(ks_new), jnp.stack(vs_new))
```

```python
import functools

import jax
import jax.numpy as jnp
from jax import lax
from jax.experimental import pallas as pl
from jax.experimental.pallas import tpu as pltpu
from jax.experimental.pallas import tpu_sc as plsc

D_MODEL = 1024
POOL_W = 512
POOL_WINDOWS = (2, 4, 8, 16)
POOL_GC = 128
POOL_STATE = 15
HEAD_DIM = 64
N_HEADS = 8
N_KV_HEADS = 2
GQA_GROUP = 4
Q_W = 512
KV_W = 128
D_IN = POOL_W + Q_W + 2 * KV_W
WINDOW = 128
ATTN_SCALE = HEAD_DIM ** -0.5
N_EXPERT_GROUPS = 4
EXPERTS_PER_GROUP = 8
N_EXPERTS = 32
EXPERT_FF = 512
EPS = 1e-6

LANES = 128
HALO = 32
TM_PROJ = 512
MOE_BM = 256
GROUP_LANE0 = 32
SC_CORES = 2
SC_SUBCORES = 16
SC_WORKERS = SC_CORES * SC_SUBCORES
DISP_CH = 64
COMB_CH = 32
SAMPLE_CH = 32
VMEM_LIMIT = 48 * 1024 * 1024

BF16 = jnp.bfloat16
F32 = jnp.float32


def _pack_bf16_pairs(h):
    w = h.shape[1] // 2
    hi = lax.bitcast_convert_type(h[:, :w].astype(F32), jnp.uint32)
    lo = lax.bitcast_convert_type(h[:, w:].astype(F32), jnp.uint32)
    return lax.bitcast_convert_type(hi | (lo >> 16), jnp.int32)


def _unpack_bf16_pairs(words):
    u = lax.bitcast_convert_type(words, jnp.uint32)
    hi = lax.bitcast_convert_type(u & jnp.uint32(0xFFFF0000), F32)
    lo = lax.bitcast_convert_type(u << 16, F32)
    return jnp.concatenate([hi, lo], axis=-1)


def _segment_mean_sq(a, bd):
    w = a.shape[1]
    return jnp.dot((a * a).astype(BF16), bd[:w, :w], preferred_element_type=F32)


def _project(x, g, w_in, qg, kg, bd):
    ms = jnp.mean(x * x, axis=-1, keepdims=True)
    h = (x * lax.rsqrt(ms + EPS) * g).astype(BF16)
    z = jnp.dot(h, w_in, preferred_element_type=F32)
    u = z[:, :POOL_W]
    q = z[:, POOL_W:POOL_W + Q_W]
    k = z[:, POOL_W + Q_W:POOL_W + Q_W + KV_W]
    v = z[:, POOL_W + Q_W + KV_W:]
    qn = []
    for c in range(Q_W // 256):
        qc = q[:, c * 256:(c + 1) * 256]
        qn.append(qc * lax.rsqrt(_segment_mean_sq(qc, bd) + EPS))
    qn = jnp.concatenate(qn, axis=-1) * qg
    kn = k * lax.rsqrt(_segment_mean_sq(k, bd) + EPS) * kg
    return u, qn, kn, v


def _pool_project(d_groups, wp_ref, ps):
    outs = []
    for p in range(2):
        dp = jnp.concatenate([d_groups[2 * p], d_groups[2 * p + 1]], axis=-1).astype(BF16)
        y = jnp.dot(dp, wp_ref[p], preferred_element_type=F32)
        outs.append(y * ps[:, p * 256:(p + 1) * 256])
    return jnp.concatenate(outs, axis=-1)


def _proj_pool_kernel(x_ref, g_ref, win_ref, qg_ref, kg_ref, bd_ref, wp_ref, ps_ref,
                      pool_ref, q_ref, k_ref, v_ref, utail_ref,
                      ext_ref, sa_ref, sb_ref, *, tm, n_j):
    j = pl.program_id(1)
    u, qn, kn, v = _project(x_ref[...], g_ref[...], win_ref[...], qg_ref[...], kg_ref[...], bd_ref[...])
    q_ref[...] = qn.astype(BF16)
    k_ref[...] = kn
    v_ref[...] = v

    @pl.when(j == 0)
    def _():
        ext_ref[0:HALO, :] = jnp.zeros((HALO, POOL_W), F32)

    r = tm + HALO
    ext_ref[HALO:r, :] = u
    sa_ref[8:r, :] = ext_ref[8:r, :] + ext_ref[7:r - 1, :]
    sb_ref[16:r, 128:] = sa_ref[16:r, 128:] + sa_ref[14:r - 2, 128:]
    sa_ref[24:r, 256:] = sb_ref[24:r, 256:] + sb_ref[20:r - 4, 256:]
    sb_ref[32:r, 384:] = sa_ref[32:r, 384:] + sa_ref[24:r - 8, 384:]
    pos1 = j * tm + lax.broadcasted_iota(jnp.int32, (tm, POOL_GC), 0) + 1
    sums = (sa_ref, sb_ref, sa_ref, sb_ref)
    d_groups = []
    for gi, w in enumerate(POOL_WINDOWS):
        sl = slice(gi * POOL_GC, (gi + 1) * POOL_GC)
        cnt = jnp.minimum(pos1, w).astype(F32)
        d_groups.append(sums[gi][HALO:r, sl] / cnt - u[:, sl])
    pool_ref[...] = _pool_project(d_groups, wp_ref, ps_ref[...]).astype(BF16)
    ext_ref[16:HALO, :] = ext_ref[tm + 16:r, :]

    @pl.when(j == n_j - 1)
    def _():
        utail_ref[...] = u[tm - 16:, :]


def _proj_pool_prompt(l, x2d, n_seq, seq, g_attn, w_in, qg, kg, bd, wp, ps):
    tm = TM_PROJ
    n_j = seq // tm
    t = n_seq * seq
    row = lambda b, j: (b * n_j + j, 0)
    lay = lambda b, j: (l, 0, 0)
    return pl.pallas_call(
        functools.partial(_proj_pool_kernel, tm=tm, n_j=n_j),
        grid=(n_seq, n_j),
        in_specs=[
            pl.BlockSpec((tm, D_MODEL), row),
            pl.BlockSpec((None, 1, D_MODEL), lay),
            pl.BlockSpec((None, D_MODEL, D_IN), lay),
            pl.BlockSpec((None, 1, Q_W), lay),
            pl.BlockSpec((None, 1, KV_W), lay),
            pl.BlockSpec((256, 256), lambda b, j: (0, 0)),
            pl.BlockSpec((None, 2, 256, 256), lambda b, j: (l, 0, 0, 0)),
            pl.BlockSpec((None, 1, POOL_W), lay),
        ],
        out_specs=[
            pl.BlockSpec((tm, POOL_W), row),
            pl.BlockSpec((tm, Q_W), row),
            pl.BlockSpec((tm, KV_W), row),
            pl.BlockSpec((tm, KV_W), row),
            pl.BlockSpec((None, 16, POOL_W), lambda b, j: (b, 0, 0)),
        ],
        out_shape=[
            jax.ShapeDtypeStruct((t, POOL_W), BF16),
            jax.ShapeDtypeStruct((t, Q_W), BF16),
            jax.ShapeDtypeStruct((t, KV_W), F32),
            jax.ShapeDtypeStruct((t, KV_W), F32),
            jax.ShapeDtypeStruct((n_seq, 16, POOL_W), F32),
        ],
        scratch_shapes=[pltpu.VMEM((tm + HALO, POOL_W), F32)] * 3,
        compiler_params=pltpu.CompilerParams(
            dimension_semantics=("arbitrary", "arbitrary"), vmem_limit_bytes=VMEM_LIMIT),
        name="proj_pool_prompt",
    )(x2d, g_attn, w_in, qg, kg, bd, wp, ps)


def _attn_kernel(sink_ref, q_ref, kp_ref, kc_ref, vp_ref, vc_ref, bias_ref, o_ref):
    q = q_ref[...]
    kk = jnp.concatenate([kp_ref[...], kc_ref[...]], axis=0).astype(BF16)
    vv = jnp.concatenate([vp_ref[...], vc_ref[...]], axis=0).astype(BF16)
    outs = []
    for h in range(N_HEADS):
        kv = h // GQA_GROUP
        qh = q[:, h * HEAD_DIM:(h + 1) * HEAD_DIM]
        kh = kk[:, kv * HEAD_DIM:(kv + 1) * HEAD_DIM]
        s = lax.dot_general(qh, kh, (((1,), (1,)), ((), ())), preferred_element_type=F32)
        s = s + bias_ref[h]
        sink = sink_ref[h]
        m = jnp.maximum(jnp.max(s, axis=-1, keepdims=True), sink)
        p = jnp.exp(s - m)
        denom = jnp.sum(p, axis=-1, keepdims=True) + jnp.exp(sink - m)
        o = jnp.dot(p.astype(BF16), vv[:, kv * HEAD_DIM:(kv + 1) * HEAD_DIM], preferred_element_type=F32)
        outs.append(o / denom)
    o_ref[...] = jnp.concatenate(outs, axis=-1).astype(BF16)


def _attn_prompt(q, k, v, bias, sinks, n_seq, seq):
    nb = seq // WINDOW
    t = n_seq * seq
    cur = lambda b, i: (b * nb + i, 0)
    prev = lambda b, i: (b * nb + jnp.maximum(i - 1, 0), 0)
    return pl.pallas_call(
        _attn_kernel,
        grid=(n_seq, nb),
        in_specs=[
            pl.BlockSpec(memory_space=pltpu.SMEM),
            pl.BlockSpec((WINDOW, Q_W), cur),
            pl.BlockSpec((WINDOW, KV_W), prev),
            pl.BlockSpec((WINDOW, KV_W), cur),
            pl.BlockSpec((WINDOW, KV_W), prev),
            pl.BlockSpec((WINDOW, KV_W), cur),
            pl.BlockSpec((None, N_HEADS, WINDOW, 2 * WINDOW), lambda b, i: (jnp.minimum(i, 1), 0, 0, 0)),
        ],
        out_specs=pl.BlockSpec((WINDOW, Q_W), cur),
        out_shape=jax.ShapeDtypeStruct((t, Q_W), BF16),
        compiler_params=pltpu.CompilerParams(
            dimension_semantics=("arbitrary", "arbitrary"), vmem_limit_bytes=VMEM_LIMIT),
        name="attn_prompt",
    )(sinks, q, k, k, v, v, bias)


def _prompt_bias():
    r = jnp.arange(WINDOW, dtype=jnp.int32)[:, None]
    c = jnp.arange(2 * WINDOW, dtype=jnp.int32)[None, :]
    dist = WINDOW + r - c
    valid = (dist >= 0) & (dist < WINDOW)
    slopes = jnp.exp2(-8.0 * jnp.arange(1, N_HEADS + 1, dtype=F32) / N_HEADS)
    pen = -slopes[:, None, None] * dist.astype(F32)[None]
    later = jnp.where(valid[None], pen, -jnp.inf)
    first = jnp.where((valid & (c >= WINDOW))[None], pen, -jnp.inf)
    return jnp.stack([first, later])


def _sample_kernel(x_ref, g_ref, win_ref, qg_ref, kg_ref, bd_ref, wp_ref, ps_ref,
                   st_ref, ck_ref, cv_ref, sink_ref, bias_ref,
                   pool_ref, attn_ref, u_ref, k_ref, v_ref,
                   qf_ref, kn_ref, vn_ref, ao_ref, *, ns, pos0):
    u, qn, kn, v = _project(x_ref[...], g_ref[...], win_ref[...], qg_ref[...], kg_ref[...], bd_ref[...])
    u_ref[...] = u
    k_ref[...] = kn
    v_ref[...] = v
    kn_ref[...] = kn
    vn_ref[...] = v

    d_groups = []
    for gi, w in enumerate(POOL_WINDOWS):
        lo = gi * POOL_GC
        acc = u[:, lo:lo + POOL_GC]
        for back in range(1, w):
            off = (POOL_STATE - back) * POOL_W + lo
            acc = acc + st_ref[:, off:off + POOL_GC]
        d_groups.append(acc / float(min(pos0 + 1, w)) - u[:, lo:lo + POOL_GC])
    pool_ref[...] = _pool_project(d_groups, wp_ref, ps_ref[...]).astype(BF16)

    zeros = jnp.zeros((ns, HEAD_DIM), F32)
    for h in range(N_HEADS):
        piece = qn[:, h * HEAD_DIM:(h + 1) * HEAD_DIM]
        pair = [piece, zeros] if h < GQA_GROUP else [zeros, piece]
        qf_ref[h] = jnp.concatenate(pair, axis=-1)

    sub = lax.broadcasted_iota(jnp.int32, (N_HEADS, LANES), 0)
    row = lax.broadcasted_iota(jnp.int32, (WINDOW, LANES), 0)
    sink = sink_ref[...]
    bias = bias_ref[...]

    def body(n, carry):
        q3 = jnp.zeros((N_HEADS, LANES), F32)
        for h in range(N_HEADS):
            q3 = jnp.where(sub == h, qf_ref[h, pl.ds(n, 1), :], q3)
        kmat = jnp.where(row == 0, kn_ref[pl.ds(n, 1), :], ck_ref[n]).astype(BF16)
        vmat = jnp.where(row == 0, vn_ref[pl.ds(n, 1), :], cv_ref[n]).astype(BF16)
        s = lax.dot_general(q3.astype(BF16), kmat, (((1,), (1,)), ((), ())), preferred_element_type=F32)
        s = s + bias
        m = jnp.maximum(jnp.max(s, axis=-1, keepdims=True), sink)
        p = jnp.exp(s - m)
        denom = jnp.sum(p, axis=-1, keepdims=True) + jnp.exp(sink - m)
        o = jnp.dot(p.astype(BF16), vmat, preferred_element_type=F32) / denom
        pieces = []
        for h in range(N_HEADS):
            kv = h // GQA_GROUP
            pieces.append(o[h:h + 1, kv * HEAD_DIM:(kv + 1) * HEAD_DIM])
        ao_ref[pl.ds(n, 1), :] = jnp.concatenate(pieces, axis=-1)
        return carry

    lax.fori_loop(0, ns, body, 0)
    attn_ref[...] = ao_ref[...].astype(BF16)


def _sample_mixer(l, xs, g_attn, w_in, qg, kg, bd, wp, ps, state2d, ck, cv, sink8, bias_s, pos0):
    n = xs.shape[0]
    ns = 32
    row = lambda i: (i, 0)
    lay = lambda i: (l, 0, 0)
    return pl.pallas_call(
        functools.partial(_sample_kernel, ns=ns, pos0=pos0),
        grid=(n // ns,),
        in_specs=[
            pl.BlockSpec((ns, D_MODEL), row),
            pl.BlockSpec((None, 1, D_MODEL), lay),
            pl.BlockSpec((None, D_MODEL, D_IN), lay),
            pl.BlockSpec((None, 1, Q_W), lay),
            pl.BlockSpec((None, 1, KV_W), lay),
            pl.BlockSpec((256, 256), lambda i: (0, 0)),
            pl.BlockSpec((None, 2, 256, 256), lambda i: (l, 0, 0, 0)),
            pl.BlockSpec((None, 1, POOL_W), lay),
            pl.BlockSpec((None, ns, POOL_STATE * POOL_W), lambda i: (l, i, 0)),
            pl.BlockSpec((None, ns, WINDOW, KV_W), lambda i: (l, i, 0, 0)),
            pl.BlockSpec((None, ns, WINDOW, KV_W), lambda i: (l, i, 0, 0)),
            pl.BlockSpec((N_HEADS, 1), lambda i: (0, 0)),
            pl.BlockSpec((N_HEADS, WINDOW), lambda i: (0, 0)),
        ],
        out_specs=[
            pl.BlockSpec((ns, POOL_W), row),
            pl.BlockSpec((ns, Q_W), row),
            pl.BlockSpec((ns, POOL_W), row),
            pl.BlockSpec((ns, KV_W), row),
            pl.BlockSpec((ns, KV_W), row),
        ],
        out_shape=[
            jax.ShapeDtypeStruct((n, POOL_W), BF16),
            jax.ShapeDtypeStruct((n, Q_W), BF16),
            jax.ShapeDtypeStruct((n, POOL_W), F32),
            jax.ShapeDtypeStruct((n, KV_W), F32),
            jax.ShapeDtypeStruct((n, KV_W), F32),
        ],
        scratch_shapes=[
            pltpu.VMEM((N_HEADS, ns, LANES), F32),
            pltpu.VMEM((ns, KV_W), F32),
            pltpu.VMEM((ns, KV_W), F32),
            pltpu.VMEM((ns, Q_W), F32),
        ],
        compiler_params=pltpu.CompilerParams(
            dimension_semantics=("arbitrary",), vmem_limit_bytes=VMEM_LIMIT),
        name="sample_mixer",
    )(xs, g_attn, w_in, qg, kg, bd, wp, ps, state2d, ck, cv, sink8, bias_s)


def _merge_router_kernel(pool_ref, attn_ref, x_ref, wout_ref, g_ref, wr_ref, br_ref, ltri_ref, cin_ref,
                         x1_ref, h2_ref, route_ref, cnt_ref):
    i = pl.program_id(0)

    @pl.when(i == 0)
    def _():
        cnt_ref[...] = cin_ref[...]

    y = jnp.dot(pool_ref[...], wout_ref[0:POOL_W, :], preferred_element_type=F32)
    y = y + jnp.dot(attn_ref[...], wout_ref[POOL_W:, :], preferred_element_type=F32)
    x1 = x_ref[...] + y
    x1_ref[...] = x1
    ms = jnp.mean(x1 * x1, axis=-1, keepdims=True)
    h2 = (x1 * lax.rsqrt(ms + EPS) * g_ref[...]).astype(BF16)
    h2_ref[...] = _pack_bf16_pairs(h2)
    logits = jnp.dot(h2, wr_ref[...], preferred_element_type=F32) + br_ref[...]

    tm = logits.shape[0]
    lane = lax.broadcasted_iota(jnp.int32, (tm, LANES), 1)
    big = jnp.int32(2 * LANES)
    neg = -jnp.inf
    is_group = (lane >= GROUP_LANE0) & (lane < GROUP_LANE0 + N_EXPERT_GROUPS)
    gl = jnp.where(is_group, logits, neg)
    gmax = jnp.max(gl, axis=-1, keepdims=True)
    grp = jnp.min(jnp.where(gl == gmax, lane, big), axis=-1, keepdims=True) - GROUP_LANE0
    g_w = 1.0 / jnp.sum(jnp.exp(gl - gmax), axis=-1, keepdims=True)
    in_grp = (lane >= grp * EXPERTS_PER_GROUP) & (lane < (grp + 1) * EXPERTS_PER_GROUP)
    el = jnp.where(in_grp, logits, neg)
    v1 = jnp.max(el, axis=-1, keepdims=True)
    i1 = jnp.min(jnp.where(el == v1, lane, big), axis=-1, keepdims=True)
    el2 = jnp.where(lane == i1, neg, el)
    v2 = jnp.max(el2, axis=-1, keepdims=True)
    i2 = jnp.min(jnp.where(el2 == v2, lane, big), axis=-1, keepdims=True)
    e21 = jnp.exp(v2 - v1)
    w1 = g_w / (1.0 + e21)
    w2 = g_w * e21 / (1.0 + e21)

    oh1 = lane == i1
    oh2 = lane == i2
    c = jnp.where(oh1 | oh2, 1.0, 0.0)
    prefix = jnp.dot(ltri_ref[...], c.astype(BF16), preferred_element_type=F32) + cnt_ref[...]
    r1 = jnp.sum(jnp.where(oh1, prefix, 0.0), axis=-1, keepdims=True)
    r2 = jnp.sum(jnp.where(oh2, prefix, 0.0), axis=-1, keepdims=True)
    cnt_ref[...] = cnt_ref[...] + jnp.sum(c, axis=0, keepdims=True)

    out = jnp.zeros((tm, LANES), F32)
    for idx, val in enumerate((i1.astype(F32), i2.astype(F32), w1, w2, r1, r2)):
        out = jnp.where(lane == idx, val, out)
    route_ref[...] = out


def _merge_router(l, pool, attn, x2d, w_out, g_ffn, wr, br, cnt_in, tm):
    t = x2d.shape[0]
    ltri = (jnp.arange(tm)[:, None] > jnp.arange(tm)[None, :]).astype(BF16)
    row = lambda i: (i, 0)
    lay = lambda i: (l, 0, 0)
    return pl.pallas_call(
        _merge_router_kernel,
        grid=(t // tm,),
        in_specs=[
            pl.BlockSpec((tm, POOL_W), row),
            pl.BlockSpec((tm, Q_W), row),
            pl.BlockSpec((tm, D_MODEL), row),
            pl.BlockSpec((None, D_MODEL, D_MODEL), lay),
            pl.BlockSpec((None, 1, D_MODEL), lay),
            pl.BlockSpec((None, D_MODEL, LANES), lay),
            pl.BlockSpec((None, 1, LANES), lay),
            pl.BlockSpec((tm, tm), lambda i: (0, 0)),
            pl.BlockSpec((1, LANES), lambda i: (0, 0)),
        ],
        out_specs=[
            pl.BlockSpec((tm, D_MODEL), row),
            pl.BlockSpec((tm, D_MODEL // 2), row),
            pl.BlockSpec((tm, LANES), row),
            pl.BlockSpec((1, LANES), lambda i: (0, 0)),
        ],
        out_shape=[
            jax.ShapeDtypeStruct((t, D_MODEL), F32),
            jax.ShapeDtypeStruct((t, D_MODEL // 2), jnp.int32),
            jax.ShapeDtypeStruct((t, LANES), F32),
            jax.ShapeDtypeStruct((1, LANES), F32),
        ],
        compiler_params=pltpu.CompilerParams(
            dimension_semantics=("arbitrary",), vmem_limit_bytes=VMEM_LIMIT),
        name="merge_router",
    )(pool, attn, x2d, w_out, g_ffn, wr, br, ltri, cnt_in)


def _moe_kernel(be_ref, rv_ref, xd_ref, wg_ref, wu_ref, wd_ref, yd_ref, wg_s, wu_s, wd_s):
    i = pl.program_id(0)
    changed = (i == 0) | (be_ref[i] != be_ref[jnp.maximum(i - 1, 0)])
    rows_valid = rv_ref[i]

    @pl.when(changed)
    def _():
        wg_s[...] = wg_ref[...].astype(BF16)
        wu_s[...] = wu_ref[...].astype(BF16)
        wd_s[...] = wd_ref[...].astype(BF16)

    @pl.when(rows_valid > 0)
    def _():
        row = lax.broadcasted_iota(jnp.int32, (MOE_BM, D_MODEL), 0)
        x = jnp.where(row < rows_valid, _unpack_bf16_pairs(xd_ref[...]), 0.0).astype(BF16)
        gate = jnp.dot(x, wg_s[...], preferred_element_type=F32)
        up = jnp.dot(x, wu_s[...], preferred_element_type=F32)
        act = (gate * jax.nn.sigmoid(gate) * up).astype(BF16)
        yd_ref[...] = jnp.dot(act, wd_s[...], preferred_element_type=F32)

    @pl.when(rows_valid <= 0)
    def _():
        yd_ref[...] = jnp.zeros(yd_ref.shape, F32)


def _moe_experts(l, block_e, rows_valid, xd, w_gate, w_up, w_down):
    n_blocks = xd.shape[0] // MOE_BM
    row = lambda i, be, rv: (i, 0)
    wsel = lambda i, be, rv: (l, be[i], 0, 0)
    return pl.pallas_call(
        _moe_kernel,
        grid_spec=pltpu.PrefetchScalarGridSpec(
            num_scalar_prefetch=2,
            grid=(n_blocks,),
            in_specs=[
                pl.BlockSpec((MOE_BM, D_MODEL // 2), row),
                pl.BlockSpec((None, None, D_MODEL, EXPERT_FF), wsel),
                pl.BlockSpec((None, None, D_MODEL, EXPERT_FF), wsel),
                pl.BlockSpec((None, None, EXPERT_FF, D_MODEL), wsel),
            ],
            out_specs=pl.BlockSpec((MOE_BM, D_MODEL), row),
            scratch_shapes=[
                pltpu.VMEM((D_MODEL, EXPERT_FF), BF16),
                pltpu.VMEM((D_MODEL, EXPERT_FF), BF16),
                pltpu.VMEM((EXPERT_FF, D_MODEL), BF16),
            ],
        ),
        out_shape=jax.ShapeDtypeStruct((n_blocks * MOE_BM, D_MODEL), F32),
        compiler_params=pltpu.CompilerParams(
            dimension_semantics=("arbitrary",), vmem_limit_bytes=VMEM_LIMIT),
        name="moe_experts",
    )(block_e, rows_valid, xd, w_gate, w_up, w_down)


def _sc_worker_id():
    return lax.axis_index("s") * SC_CORES + lax.axis_index("c")


def _sc_dispatch(hp, hs, dest_p, dest_s, n_rows):
    tp, width = hp.shape
    per_w = tp // SC_WORKERS
    n_ch = per_w // DISP_CH
    n_sw = hs.shape[0] // SAMPLE_CH
    mesh = plsc.VectorSubcoreMesh(core_axis_name="c", subcore_axis_name="s")

    @functools.partial(
        pl.kernel, mesh=mesh,
        out_type=jax.ShapeDtypeStruct((n_rows, width), jnp.int32),
        scratch_types=[
            pltpu.VMEM((2, n_ch, DISP_CH), jnp.int32),
            pltpu.VMEM((2, 1, SAMPLE_CH), jnp.int32),
            pltpu.VMEM((2, DISP_CH, width), jnp.int32),
            pltpu.SemaphoreType.DMA((2,)),
            pltpu.SemaphoreType.DMA((2,)),
        ],
        name="sc_dispatch",
    )
    def k(hp_hbm, hs_hbm, dp_hbm, ds_hbm, xd_hbm, idx_v, idxs_v, bufs, rsem, wsem):
        wid = _sc_worker_id()
        base = wid * per_w
        for kk in range(2):
            pltpu.sync_copy(dp_hbm.at[kk, wid], idx_v.at[kk])
        reads = [pltpu.make_async_copy(hp_hbm.at[pl.ds(base + j * DISP_CH, DISP_CH)],
                                       bufs.at[j % 2], rsem.at[j % 2]) for j in range(n_ch)]
        reads[0].start()
        for j in range(n_ch):
            if j + 1 < n_ch:
                reads[j + 1].start()
            reads[j].wait()
            writes = [pltpu.make_async_copy(bufs.at[j % 2], xd_hbm.at[idx_v.at[kk, j]], wsem.at[kk])
                      for kk in range(2)]
            for w in writes:
                w.start()
            for w in writes:
                w.wait()

        @pl.when(wid < n_sw)
        def _():
            rows = bufs.at[0, pl.ds(0, SAMPLE_CH)]
            for kk in range(2):
                pltpu.sync_copy(ds_hbm.at[kk, wid], idxs_v.at[kk])
            pltpu.sync_copy(hs_hbm.at[pl.ds(wid * SAMPLE_CH, SAMPLE_CH)], rows)
            for kk in range(2):
                pltpu.sync_copy(rows, xd_hbm.at[idxs_v.at[kk, 0]])

    return k(hp, hs, dest_p, dest_s)


def _sc_combine_gather(yd, dest_p, dest_s, tp, ts):
    width = yd.shape[1]
    per_w = tp // SC_WORKERS
    n_ch = per_w // COMB_CH
    n_sw = ts // SAMPLE_CH
    mesh = plsc.VectorSubcoreMesh(core_axis_name="c", subcore_axis_name="s")

    @functools.partial(
        pl.kernel, mesh=mesh,
        out_type=jax.ShapeDtypeStruct((2, tp + ts, width), F32),
        scratch_types=[
            pltpu.VMEM((2, n_ch, COMB_CH), jnp.int32),
            pltpu.VMEM((2, 1, SAMPLE_CH), jnp.int32),
            pltpu.VMEM((2, COMB_CH, width), F32),
            pltpu.SemaphoreType.DMA((2,)),
            pltpu.SemaphoreType.DMA((2,)),
        ],
        name="sc_combine_gather",
    )
    def k(yd_hbm, dp_hbm, ds_hbm, g_hbm, idx_v, idxs_v, bufs, gsem, wsem):
        wid = _sc_worker_id()
        base = wid * per_w
        for kk in range(2):
            pltpu.sync_copy(dp_hbm.at[kk, wid], idx_v.at[kk])
        items = [(kk, j) for kk in range(2) for j in range(n_ch)]
        gathers = [pltpu.make_async_copy(yd_hbm.at[idx_v.at[kk, j]], bufs.at[n % 2], gsem.at[n % 2])
                   for n, (kk, j) in enumerate(items)]
        gathers[0].start()
        for n, (kk, j) in enumerate(items):
            if n + 1 < len(items):
                gathers[n + 1].start()
            gathers[n].wait()
            w = pltpu.make_async_copy(bufs.at[n % 2], g_hbm.at[kk, pl.ds(base + j * COMB_CH, COMB_CH)],
                                      wsem.at[n % 2])
            w.start()
            w.wait()

        @pl.when(wid < n_sw)
        def _():
            for kk in range(2):
                pltpu.sync_copy(ds_hbm.at[kk, wid], idxs_v.at[kk])
            for kk in range(2):
                pltpu.sync_copy(yd_hbm.at[idxs_v.at[kk, 0]], bufs.at[kk])
                pltpu.sync_copy(bufs.at[kk], g_hbm.at[kk, pl.ds(tp + wid * SAMPLE_CH, SAMPLE_CH)])

    return k(yd, dest_p, dest_s)


def _combine_kernel(x1_ref, g_ref, route_ref, x2_ref):
    w1 = route_ref[:, 2:3]
    w2 = route_ref[:, 3:4]
    x2_ref[...] = x1_ref[...] + g_ref[0] * w1 + g_ref[1] * w2


def _combine(x1, g, route, row0, tm):
    t = x1.shape[0]
    blk0 = row0 // tm
    row = lambda i: (i, 0)
    return pl.pallas_call(
        _combine_kernel,
        grid=(t // tm,),
        in_specs=[
            pl.BlockSpec((tm, D_MODEL), row),
            pl.BlockSpec((2, tm, D_MODEL), lambda i: (0, blk0 + i, 0)),
            pl.BlockSpec((tm, LANES), row),
        ],
        out_specs=pl.BlockSpec((tm, D_MODEL), row),
        out_shape=jax.ShapeDtypeStruct((t, D_MODEL), F32),
        compiler_params=pltpu.CompilerParams(
            dimension_semantics=("arbitrary",), vmem_limit_bytes=VMEM_LIMIT),
        name="combine",
    )(x1, g, route)


def _dest_layout(dest, workers, chunk):
    t = dest.shape[0]
    return dest.T.reshape(2, workers, t // (workers * chunk), chunk)


def _hier_moe(l, h2p, h2s, route_p, route_s, counts, x1p, x1s, w_gate, w_up, w_down):
    tp, ts = h2p.shape[0], h2s.shape[0]
    n_assign = 2 * (tp + ts)
    n_blocks = -(-n_assign // MOE_BM) + N_EXPERTS
    pcounts = (counts + MOE_BM - 1) // MOE_BM * MOE_BM
    pends = jnp.cumsum(pcounts)
    poffsets = pends - pcounts
    starts = jnp.arange(n_blocks, dtype=jnp.int32) * MOE_BM
    block_e = jnp.minimum(jnp.sum((pends[None, :] <= starts[:, None]).astype(jnp.int32), axis=1),
                          N_EXPERTS - 1)
    rows_valid = jnp.clip(poffsets[block_e] + counts[block_e] - starts, 0, MOE_BM).astype(jnp.int32)

    def dest_of(route):
        return poffsets[route[:, 0:2].astype(jnp.int32)] + route[:, 4:6].astype(jnp.int32)

    dest_p, dest_s = dest_of(route_p), dest_of(route_s)
    n_sw = ts // SAMPLE_CH
    xd = _sc_dispatch(h2p, h2s, _dest_layout(dest_p, SC_WORKERS, DISP_CH),
                      _dest_layout(dest_s, n_sw, SAMPLE_CH), n_blocks * MOE_BM)
    yd = _moe_experts(l, block_e, rows_valid, xd, w_gate, w_up, w_down)
    g = _sc_combine_gather(yd, _dest_layout(dest_p, SC_WORKERS, COMB_CH),
                           _dest_layout(dest_s, n_sw, SAMPLE_CH), tp, ts)
    return _combine(x1p, g, route_p, 0, TM_PROJ), _combine(x1s, g, route_s, tp, ts)


def kernel(x_prompt, x_sample, state_pool, cache_k_win, cache_v_win, norm_attn_g, w_in, pool_w, pool_scale, q_norm_g, k_norm_g, attn_sinks, w_out, norm_ffn_g, router_group_w, router_group_b, router_expert_w, router_expert_b, w_gate, w_up, w_down):
    n_p, t_p, d = x_prompt.shape
    n_s, t_s, _ = x_sample.shape
    depth = w_in.shape[0]
    lw_s = cache_k_win.shape[2]
    assert t_s == 1 and lw_s == WINDOW and d == D_MODEL
    assert t_p % TM_PROJ == 0 and t_p >= WINDOW
    past_len = 16384

    w_in_b = w_in.astype(BF16)
    w_out_b = w_out.astype(BF16)
    g_attn = norm_attn_g.reshape(depth, 1, D_MODEL)
    g_ffn = norm_ffn_g.reshape(depth, 1, D_MODEL)
    qg = (jnp.tile(q_norm_g, (1, N_HEADS)) * ATTN_SCALE).reshape(depth, 1, Q_W)
    kg = jnp.tile(k_norm_g, (1, N_KV_HEADS)).reshape(depth, 1, KV_W)
    seg = jnp.arange(256) // HEAD_DIM
    bd = jnp.where(seg[:, None] == seg[None, :], 1.0 / HEAD_DIM, 0.0).astype(BF16)
    wp = jnp.zeros((depth, 2, 256, 256), F32)
    for p in range(2):
        wp = wp.at[:, p, :POOL_GC, :POOL_GC].set(pool_w[:, 2 * p])
        wp = wp.at[:, p, POOL_GC:, POOL_GC:].set(pool_w[:, 2 * p + 1])
    wp = wp.astype(BF16)
    ps = pool_scale.reshape(depth, 1, POOL_W)
    wr = jnp.zeros((depth, D_MODEL, LANES), F32)
    wr = wr.at[:, :, :N_EXPERTS].set(router_expert_w)
    wr = wr.at[:, :, GROUP_LANE0:GROUP_LANE0 + N_EXPERT_GROUPS].set(router_group_w)
    wr = wr.astype(BF16)
    br = jnp.zeros((depth, 1, LANES), F32)
    br = br.at[:, 0, :N_EXPERTS].set(router_expert_b)
    br = br.at[:, 0, GROUP_LANE0:GROUP_LANE0 + N_EXPERT_GROUPS].set(router_group_b)

    slopes = jnp.exp2(-8.0 * jnp.arange(1, N_HEADS + 1, dtype=F32) / N_HEADS)
    bias_p = _prompt_bias()
    jpos = jnp.arange(WINDOW, dtype=F32)
    dist_s = jnp.where(jpos == 0, 0.0, WINDOW - jpos)
    bias_s = -slopes[:, None] * dist_s[None, :]
    state2d = state_pool.reshape(depth, n_s, POOL_STATE * POOL_W)
    ck_all = cache_k_win.reshape(depth, n_s, lw_s, KV_W)
    cv_all = cache_v_win.reshape(depth, n_s, lw_s, KV_W)

    xp = x_prompt.reshape(n_p * t_p, D_MODEL)
    xs = x_sample.reshape(n_s, D_MODEL)
    lw_p = min(WINDOW, t_p)
    pool_p, kp_new, vp_new, pool_s, ks_new, vs_new = [], [], [], [], [], []
    zero_cnt = jnp.zeros((1, LANES), F32)
    for l in range(depth):
        sinks = attn_sinks[l]
        pool_o, q, k, v, utail = _proj_pool_prompt(l, xp, n_p, t_p, g_attn, w_in_b, qg, kg, bd, wp, ps)
        attn_o = _attn_prompt(q, k, v, bias_p, sinks, n_p, t_p)
        x1p, h2p, route_p, cnt_p = _merge_router(l, pool_o, attn_o, xp, w_out_b, g_ffn, wr, br, zero_cnt, TM_PROJ)
        pool_p.append(utail[:, 16 - POOL_STATE:, :])
        kp_new.append(k.reshape(n_p, t_p, N_KV_HEADS, HEAD_DIM)[:, t_p - lw_p:])
        vp_new.append(v.reshape(n_p, t_p, N_KV_HEADS, HEAD_DIM)[:, t_p - lw_p:])
        pool_so, attn_so, u_s, k_s, v_s = _sample_mixer(
            l, xs, g_attn, w_in_b, qg, kg, bd, wp, ps, state2d, ck_all, cv_all,
            sinks.reshape(N_HEADS, 1), bias_s, past_len)
        x1s, h2s, route_s, cnt_all = _merge_router(l, pool_so, attn_so, xs, w_out_b, g_ffn, wr, br, cnt_p, n_s)
        pool_s.append(jnp.concatenate([state_pool[l][:, 1:], u_s[:, None, :]], axis=1))
        ks_new.append(jnp.concatenate(
            [cache_k_win[l][:, 1:], k_s.reshape(n_s, 1, N_KV_HEADS, HEAD_DIM)], axis=1))
        vs_new.append(jnp.concatenate(
            [cache_v_win[l][:, 1:], v_s.reshape(n_s, 1, N_KV_HEADS, HEAD_DIM)], axis=1))
        counts = cnt_all[0, :N_EXPERTS].astype(jnp.int32)
        xp, xs = _hier_moe(l, h2p, h2s, route_p, route_s, counts, x1p, x1s, w_gate, w_up, w_down)
    return (xp.reshape(n_p, t_p, D_MODEL), xs.reshape(n_s, t_s, D_MODEL),
            jnp.stack(pool_p), jnp.stack(kp_new), jnp.stack(vp_new),
            jnp.stack(pool_s), jnp.stack(ks_new), jnp.stack(vs_new))
```

```python
import functools

import jax
import jax.numpy as jnp
from jax import lax
from jax.experimental import pallas as pl
from jax.experimental.pallas import tpu as pltpu
from jax.experimental.pallas import tpu_sc as plsc

D_MODEL = 1024
POOL_W = 512
POOL_WINDOWS = (2, 4, 8, 16)
POOL_GC = 128
POOL_STATE = 15
HEAD_DIM = 64
N_HEADS = 8
N_KV_HEADS = 2
GQA_GROUP = 4
Q_W = 512
KV_W = 128
D_IN = POOL_W + Q_W + 2 * KV_W
WINDOW = 128
ATTN_SCALE = HEAD_DIM ** -0.5
N_EXPERT_GROUPS = 4
EXPERTS_PER_GROUP = 8
N_EXPERTS = 32
EXPERT_FF = 512
EPS = 1e-6

LANES = 128
HALO = 32
TM_PROJ = 512
MOE_BM = 256
GROUP_LANE0 = 32
ROUTE_FIELDS = 8
SC_CORES = 2
SC_SUBCORES = 16
SC_WORKERS = SC_CORES * SC_SUBCORES
DISP_CH = 64
COMB_CH = 32
SAMPLE_CH = 32
VMEM_LIMIT = 48 * 1024 * 1024

BF16 = jnp.bfloat16
F32 = jnp.float32


def _pack_bf16_pairs(h):
    w = h.shape[1] // 2
    hi = lax.bitcast_convert_type(h[:, :w].astype(F32), jnp.uint32)
    lo = lax.bitcast_convert_type(h[:, w:].astype(F32), jnp.uint32)
    return lax.bitcast_convert_type(hi | (lo >> 16), jnp.int32)


def _unpack_bf16_pairs(words):
    u = lax.bitcast_convert_type(words, jnp.uint32)
    hi = lax.bitcast_convert_type(u & jnp.uint32(0xFFFF0000), F32)
    lo = lax.bitcast_convert_type(u << 16, F32)
    return jnp.concatenate([hi, lo], axis=-1)


def _segment_mean_sq(a, bd):
    w = a.shape[1]
    return jnp.dot((a * a).astype(BF16), bd[:w, :w], preferred_element_type=F32)


def _project(x, g, w_in, qg, kg, bd):
    ms = jnp.mean(x * x, axis=-1, keepdims=True)
    h = (x * lax.rsqrt(ms + EPS) * g).astype(BF16)
    z = jnp.dot(h, w_in, preferred_element_type=F32)
    u = z[:, :POOL_W]
    q = z[:, POOL_W:POOL_W + Q_W]
    k = z[:, POOL_W + Q_W:POOL_W + Q_W + KV_W]
    v = z[:, POOL_W + Q_W + KV_W:]
    qn = []
    for c in range(Q_W // 256):
        qc = q[:, c * 256:(c + 1) * 256]
        qn.append(qc * lax.rsqrt(_segment_mean_sq(qc, bd) + EPS))
    qn = jnp.concatenate(qn, axis=-1) * qg
    kn = k * lax.rsqrt(_segment_mean_sq(k, bd) + EPS) * kg
    return u, qn, kn, v


def _pool_project(d_groups, wp_ref, ps):
    outs = []
    for p in range(2):
        dp = jnp.concatenate([d_groups[2 * p], d_groups[2 * p + 1]], axis=-1).astype(BF16)
        y = jnp.dot(dp, wp_ref[p], preferred_element_type=F32)
        outs.append(y * ps[:, p * 256:(p + 1) * 256])
    return jnp.concatenate(outs, axis=-1)


def _proj_pool_kernel(x_ref, g_ref, win_ref, qg_ref, kg_ref, bd_ref, wp_ref, ps_ref,
                      pool_ref, q_ref, k_ref, v_ref, utail_ref,
                      ext_ref, sa_ref, sb_ref, *, tm, n_j):
    j = pl.program_id(1)
    u, qn, kn, v = _project(x_ref[...], g_ref[...], win_ref[...], qg_ref[...], kg_ref[...], bd_ref[...])
    q_ref[...] = qn.astype(BF16)
    k_ref[...] = kn
    v_ref[...] = v

    @pl.when(j == 0)
    def _():
        ext_ref[0:HALO, :] = jnp.zeros((HALO, POOL_W), F32)

    r = tm + HALO
    ext_ref[HALO:r, :] = u
    sa_ref[8:r, :] = ext_ref[8:r, :] + ext_ref[7:r - 1, :]
    sb_ref[16:r, 128:] = sa_ref[16:r, 128:] + sa_ref[14:r - 2, 128:]
    sa_ref[24:r, 256:] = sb_ref[24:r, 256:] + sb_ref[20:r - 4, 256:]
    sb_ref[32:r, 384:] = sa_ref[32:r, 384:] + sa_ref[24:r - 8, 384:]
    pos1 = j * tm + lax.broadcasted_iota(jnp.int32, (tm, POOL_GC), 0) + 1
    sums = (sa_ref, sb_ref, sa_ref, sb_ref)
    d_groups = []
    for gi, w in enumerate(POOL_WINDOWS):
        sl = slice(gi * POOL_GC, (gi + 1) * POOL_GC)
        cnt = jnp.minimum(pos1, w).astype(F32)
        d_groups.append(sums[gi][HALO:r, sl] / cnt - u[:, sl])
    pool_ref[...] = _pool_project(d_groups, wp_ref, ps_ref[...]).astype(BF16)
    ext_ref[16:HALO, :] = ext_ref[tm + 16:r, :]

    @pl.when(j == n_j - 1)
    def _():
        utail_ref[...] = u[tm - 16:, :]


def _proj_pool_prompt(l, x2d, n_seq, seq, g_attn, w_in, qg, kg, bd, wp, ps):
    tm = TM_PROJ
    n_j = seq // tm
    t = n_seq * seq
    row = lambda b, j: (b * n_j + j, 0)
    lay = lambda b, j: (l, 0, 0)
    return pl.pallas_call(
        functools.partial(_proj_pool_kernel, tm=tm, n_j=n_j),
        grid=(n_seq, n_j),
        in_specs=[
            pl.BlockSpec((tm, D_MODEL), row),
            pl.BlockSpec((None, 1, D_MODEL), lay),
            pl.BlockSpec((None, D_MODEL, D_IN), lay),
            pl.BlockSpec((None, 1, Q_W), lay),
            pl.BlockSpec((None, 1, KV_W), lay),
            pl.BlockSpec((256, 256), lambda b, j: (0, 0)),
            pl.BlockSpec((None, 2, 256, 256), lambda b, j: (l, 0, 0, 0)),
            pl.BlockSpec((None, 1, POOL_W), lay),
        ],
        out_specs=[
            pl.BlockSpec((tm, POOL_W), row),
            pl.BlockSpec((tm, Q_W), row),
            pl.BlockSpec((tm, KV_W), row),
            pl.BlockSpec((tm, KV_W), row),
            pl.BlockSpec((None, 16, POOL_W), lambda b, j: (b, 0, 0)),
        ],
        out_shape=[
            jax.ShapeDtypeStruct((t, POOL_W), BF16),
            jax.ShapeDtypeStruct((t, Q_W), BF16),
            jax.ShapeDtypeStruct((t, KV_W), F32),
            jax.ShapeDtypeStruct((t, KV_W), F32),
            jax.ShapeDtypeStruct((n_seq, 16, POOL_W), F32),
        ],
        scratch_shapes=[pltpu.VMEM((tm + HALO, POOL_W), F32)] * 3,
        compiler_params=pltpu.CompilerParams(
            dimension_semantics=("arbitrary", "arbitrary"), vmem_limit_bytes=VMEM_LIMIT),
        name="proj_pool_prompt",
    )(x2d, g_attn, w_in, qg, kg, bd, wp, ps)


def _attn_kernel(sink_ref, q_ref, kp_ref, kc_ref, vp_ref, vc_ref, bias_ref, o_ref):
    q = q_ref[...]
    kk = jnp.concatenate([kp_ref[...], kc_ref[...]], axis=0).astype(BF16)
    vv = jnp.concatenate([vp_ref[...], vc_ref[...]], axis=0).astype(BF16)
    outs = []
    for h in range(N_HEADS):
        kv = h // GQA_GROUP
        qh = q[:, h * HEAD_DIM:(h + 1) * HEAD_DIM]
        kh = kk[:, kv * HEAD_DIM:(kv + 1) * HEAD_DIM]
        s = lax.dot_general(qh, kh, (((1,), (1,)), ((), ())), preferred_element_type=F32)
        s = s + bias_ref[h]
        sink = sink_ref[h]
        m = jnp.maximum(jnp.max(s, axis=-1, keepdims=True), sink)
        p = jnp.exp(s - m)
        denom = jnp.sum(p, axis=-1, keepdims=True) + jnp.exp(sink - m)
        o = jnp.dot(p.astype(BF16), vv[:, kv * HEAD_DIM:(kv + 1) * HEAD_DIM], preferred_element_type=F32)
        outs.append(o / denom)
    o_ref[...] = jnp.concatenate(outs, axis=-1).astype(BF16)


def _attn_prompt(q, k, v, bias, sinks, n_seq, seq):
    nb = seq // WINDOW
    t = n_seq * seq
    cur = lambda b, i: (b * nb + i, 0)
    prev = lambda b, i: (b * nb + jnp.maximum(i - 1, 0), 0)
    return pl.pallas_call(
        _attn_kernel,
        grid=(n_seq, nb),
        in_specs=[
            pl.BlockSpec(memory_space=pltpu.SMEM),
            pl.BlockSpec((WINDOW, Q_W), cur),
            pl.BlockSpec((WINDOW, KV_W), prev),
            pl.BlockSpec((WINDOW, KV_W), cur),
            pl.BlockSpec((WINDOW, KV_W), prev),
            pl.BlockSpec((WINDOW, KV_W), cur),
            pl.BlockSpec((None, N_HEADS, WINDOW, 2 * WINDOW), lambda b, i: (jnp.minimum(i, 1), 0, 0, 0)),
        ],
        out_specs=pl.BlockSpec((WINDOW, Q_W), cur),
        out_shape=jax.ShapeDtypeStruct((t, Q_W), BF16),
        compiler_params=pltpu.CompilerParams(
            dimension_semantics=("arbitrary", "arbitrary"), vmem_limit_bytes=VMEM_LIMIT),
        name="attn_prompt",
    )(sinks, q, k, k, v, v, bias)


def _prompt_bias():
    r = jnp.arange(WINDOW, dtype=jnp.int32)[:, None]
    c = jnp.arange(2 * WINDOW, dtype=jnp.int32)[None, :]
    dist = WINDOW + r - c
    valid = (dist >= 0) & (dist < WINDOW)
    slopes = jnp.exp2(-8.0 * jnp.arange(1, N_HEADS + 1, dtype=F32) / N_HEADS)
    pen = -slopes[:, None, None] * dist.astype(F32)[None]
    later = jnp.where(valid[None], pen, -jnp.inf)
    first = jnp.where((valid & (c >= WINDOW))[None], pen, -jnp.inf)
    return jnp.stack([first, later])


def _sample_kernel(x_ref, g_ref, win_ref, qg_ref, kg_ref, bd_ref, wp_ref, ps_ref,
                   st_ref, ck_ref, cv_ref, sink_ref, bias_ref,
                   pool_ref, attn_ref, u_ref, k_ref, v_ref,
                   qf_ref, kn_ref, vn_ref, ao_ref, *, ns, pos0):
    u, qn, kn, v = _project(x_ref[...], g_ref[...], win_ref[...], qg_ref[...], kg_ref[...], bd_ref[...])
    u_ref[...] = u
    k_ref[...] = kn
    v_ref[...] = v
    kn_ref[...] = kn
    vn_ref[...] = v

    d_groups = []
    for gi, w in enumerate(POOL_WINDOWS):
        lo = gi * POOL_GC
        acc = u[:, lo:lo + POOL_GC]
        for back in range(1, w):
            off = (POOL_STATE - back) * POOL_W + lo
            acc = acc + st_ref[:, off:off + POOL_GC]
        d_groups.append(acc / float(min(pos0 + 1, w)) - u[:, lo:lo + POOL_GC])
    pool_ref[...] = _pool_project(d_groups, wp_ref, ps_ref[...]).astype(BF16)

    zeros = jnp.zeros((ns, HEAD_DIM), F32)
    for h in range(N_HEADS):
        piece = qn[:, h * HEAD_DIM:(h + 1) * HEAD_DIM]
        pair = [piece, zeros] if h < GQA_GROUP else [zeros, piece]
        qf_ref[h] = jnp.concatenate(pair, axis=-1)

    sub = lax.broadcasted_iota(jnp.int32, (N_HEADS, LANES), 0)
    row = lax.broadcasted_iota(jnp.int32, (WINDOW, LANES), 0)
    sink = sink_ref[...]
    bias = bias_ref[...]

    def body(n, carry):
        q3 = jnp.zeros((N_HEADS, LANES), F32)
        for h in range(N_HEADS):
            q3 = jnp.where(sub == h, qf_ref[h, pl.ds(n, 1), :], q3)
        kmat = jnp.where(row == 0, kn_ref[pl.ds(n, 1), :], ck_ref[n]).astype(BF16)
        vmat = jnp.where(row == 0, vn_ref[pl.ds(n, 1), :], cv_ref[n]).astype(BF16)
        s = lax.dot_general(q3.astype(BF16), kmat, (((1,), (1,)), ((), ())), preferred_element_type=F32)
        s = s + bias
        m = jnp.maximum(jnp.max(s, axis=-1, keepdims=True), sink)
        p = jnp.exp(s - m)
        denom = jnp.sum(p, axis=-1, keepdims=True) + jnp.exp(sink - m)
        o = jnp.dot(p.astype(BF16), vmat, preferred_element_type=F32) / denom
        pieces = []
        for h in range(N_HEADS):
            kv = h // GQA_GROUP
            pieces.append(o[h:h + 1, kv * HEAD_DIM:(kv + 1) * HEAD_DIM])
        ao_ref[pl.ds(n, 1), :] = jnp.concatenate(pieces, axis=-1)
        return carry

    lax.fori_loop(0, ns, body, 0)
    attn_ref[...] = ao_ref[...].astype(BF16)


def _sample_mixer(l, xs, g_attn, w_in, qg, kg, bd, wp, ps, state2d, ck, cv, sink8, bias_s, pos0):
    n = xs.shape[0]
    ns = 32
    row = lambda i: (i, 0)
    lay = lambda i: (l, 0, 0)
    return pl.pallas_call(
        functools.partial(_sample_kernel, ns=ns, pos0=pos0),
        grid=(n // ns,),
        in_specs=[
            pl.BlockSpec((ns, D_MODEL), row),
            pl.BlockSpec((None, 1, D_MODEL), lay),
            pl.BlockSpec((None, D_MODEL, D_IN), lay),
            pl.BlockSpec((None, 1, Q_W), lay),
            pl.BlockSpec((None, 1, KV_W), lay),
            pl.BlockSpec((256, 256), lambda i: (0, 0)),
            pl.BlockSpec((None, 2, 256, 256), lambda i: (l, 0, 0, 0)),
            pl.BlockSpec((None, 1, POOL_W), lay),
            pl.BlockSpec((None, ns, POOL_STATE * POOL_W), lambda i: (l, i, 0)),
            pl.BlockSpec((None, ns, WINDOW, KV_W), lambda i: (l, i, 0, 0)),
            pl.BlockSpec((None, ns, WINDOW, KV_W), lambda i: (l, i, 0, 0)),
            pl.BlockSpec((N_HEADS, 1), lambda i: (0, 0)),
            pl.BlockSpec((N_HEADS, WINDOW), lambda i: (0, 0)),
        ],
        out_specs=[
            pl.BlockSpec((ns, POOL_W), row),
            pl.BlockSpec((ns, Q_W), row),
            pl.BlockSpec((ns, POOL_W), row),
            pl.BlockSpec((ns, KV_W), row),
            pl.BlockSpec((ns, KV_W), row),
        ],
        out_shape=[
            jax.ShapeDtypeStruct((n, POOL_W), BF16),
            jax.ShapeDtypeStruct((n, Q_W), BF16),
            jax.ShapeDtypeStruct((n, POOL_W), F32),
            jax.ShapeDtypeStruct((n, KV_W), F32),
            jax.ShapeDtypeStruct((n, KV_W), F32),
        ],
        scratch_shapes=[
            pltpu.VMEM((N_HEADS, ns, LANES), F32),
            pltpu.VMEM((ns, KV_W), F32),
            pltpu.VMEM((ns, KV_W), F32),
            pltpu.VMEM((ns, Q_W), F32),
        ],
        compiler_params=pltpu.CompilerParams(
            dimension_semantics=("arbitrary",), vmem_limit_bytes=VMEM_LIMIT),
        name="sample_mixer",
    )(xs, g_attn, w_in, qg, kg, bd, wp, ps, state2d, ck, cv, sink8, bias_s)


def _merge_router_kernel(pool_ref, attn_ref, x_ref, wout_ref, g_ref, wr_ref, br_ref, ltri_ref, cin_ref,
                         x1_ref, h2_ref, route_ref, route_t_ref, cnt_ref):
    i = pl.program_id(0)

    @pl.when(i == 0)
    def _():
        cnt_ref[...] = cin_ref[...]

    y = jnp.dot(pool_ref[...], wout_ref[0:POOL_W, :], preferred_element_type=F32)
    y = y + jnp.dot(attn_ref[...], wout_ref[POOL_W:, :], preferred_element_type=F32)
    x1 = x_ref[...] + y
    x1_ref[...] = x1
    ms = jnp.mean(x1 * x1, axis=-1, keepdims=True)
    h2 = (x1 * lax.rsqrt(ms + EPS) * g_ref[...]).astype(BF16)
    h2_ref[...] = _pack_bf16_pairs(h2)
    logits = jnp.dot(h2, wr_ref[...], preferred_element_type=F32) + br_ref[...]

    tm = logits.shape[0]
    lane = lax.broadcasted_iota(jnp.int32, (tm, LANES), 1)
    big = jnp.int32(2 * LANES)
    neg = -jnp.inf
    is_group = (lane >= GROUP_LANE0) & (lane < GROUP_LANE0 + N_EXPERT_GROUPS)
    gl = jnp.where(is_group, logits, neg)
    gmax = jnp.max(gl, axis=-1, keepdims=True)
    grp = jnp.min(jnp.where(gl == gmax, lane, big), axis=-1, keepdims=True) - GROUP_LANE0
    g_w = 1.0 / jnp.sum(jnp.exp(gl - gmax), axis=-1, keepdims=True)
    in_grp = (lane >= grp * EXPERTS_PER_GROUP) & (lane < (grp + 1) * EXPERTS_PER_GROUP)
    el = jnp.where(in_grp, logits, neg)
    v1 = jnp.max(el, axis=-1, keepdims=True)
    i1 = jnp.min(jnp.where(el == v1, lane, big), axis=-1, keepdims=True)
    el2 = jnp.where(lane == i1, neg, el)
    v2 = jnp.max(el2, axis=-1, keepdims=True)
    i2 = jnp.min(jnp.where(el2 == v2, lane, big), axis=-1, keepdims=True)
    e21 = jnp.exp(v2 - v1)
    w1 = g_w / (1.0 + e21)
    w2 = g_w * e21 / (1.0 + e21)

    oh1 = lane == i1
    oh2 = lane == i2
    c = jnp.where(oh1 | oh2, 1.0, 0.0)
    prefix = jnp.dot(ltri_ref[...], c.astype(BF16), preferred_element_type=F32) + cnt_ref[...]
    r1 = jnp.sum(jnp.where(oh1, prefix, 0.0), axis=-1, keepdims=True)
    r2 = jnp.sum(jnp.where(oh2, prefix, 0.0), axis=-1, keepdims=True)
    cnt_ref[...] = cnt_ref[...] + jnp.sum(c, axis=0, keepdims=True)

    out = jnp.zeros((tm, LANES), F32)
    for idx, val in enumerate((i1.astype(F32), i2.astype(F32), w1, w2, r1, r2)):
        out = jnp.where(lane == idx, val, out)
    route_ref[...] = out
    route_t_ref[...] = jnp.transpose(out)[0:ROUTE_FIELDS, :]


def _merge_router(l, pool, attn, x2d, w_out, g_ffn, wr, br, cnt_in, tm):
    t = x2d.shape[0]
    ltri = (jnp.arange(tm)[:, None] > jnp.arange(tm)[None, :]).astype(BF16)
    row = lambda i: (i, 0)
    lay = lambda i: (l, 0, 0)
    return pl.pallas_call(
        _merge_router_kernel,
        grid=(t // tm,),
        in_specs=[
            pl.BlockSpec((tm, POOL_W), row),
            pl.BlockSpec((tm, Q_W), row),
            pl.BlockSpec((tm, D_MODEL), row),
            pl.BlockSpec((None, D_MODEL, D_MODEL), lay),
            pl.BlockSpec((None, 1, D_MODEL), lay),
            pl.BlockSpec((None, D_MODEL, LANES), lay),
            pl.BlockSpec((None, 1, LANES), lay),
            pl.BlockSpec((tm, tm), lambda i: (0, 0)),
            pl.BlockSpec((1, LANES), lambda i: (0, 0)),
        ],
        out_specs=[
            pl.BlockSpec((tm, D_MODEL), row),
            pl.BlockSpec((tm, D_MODEL // 2), row),
            pl.BlockSpec((tm, LANES), row),
            pl.BlockSpec((ROUTE_FIELDS, tm), lambda i: (0, i)),
            pl.BlockSpec((1, LANES), lambda i: (0, 0)),
        ],
        out_shape=[
            jax.ShapeDtypeStruct((t, D_MODEL), F32),
            jax.ShapeDtypeStruct((t, D_MODEL // 2), jnp.int32),
            jax.ShapeDtypeStruct((t, LANES), F32),
            jax.ShapeDtypeStruct((ROUTE_FIELDS, t), F32),
            jax.ShapeDtypeStruct((1, LANES), F32),
        ],
        compiler_params=pltpu.CompilerParams(
            dimension_semantics=("arbitrary",), vmem_limit_bytes=VMEM_LIMIT),
        name="merge_router",
    )(pool, attn, x2d, w_out, g_ffn, wr, br, ltri, cnt_in)


def _moe_kernel(be_ref, rv_ref, nx_ref, sl_ref, xd_ref, wg_hbm, wu_hbm, wd_hbm, yd_ref,
                wg_f, wu_f, wd_f, wg_s, wu_s, wd_s, sem, *, layer):
    i = pl.program_id(0)
    expert = be_ref[i]
    changed = (i == 0) | (expert != be_ref[jnp.maximum(i - 1, 0)])
    rows_valid = rv_ref[i]
    slot = sl_ref[i]

    def weight_copies(e, s):
        return [pltpu.make_async_copy(w_hbm.at[layer, e], w_f.at[s], sem.at[s, n])
                for n, (w_hbm, w_f) in enumerate(((wg_hbm, wg_f), (wu_hbm, wu_f), (wd_hbm, wd_f)))]

    @pl.when(i == 0)
    def _():
        for c in weight_copies(expert, slot):
            c.start()

    @pl.when(changed)
    def _():
        for c in weight_copies(expert, slot):
            c.wait()

        @pl.when(nx_ref[i] >= 0)
        def _():
            for c in weight_copies(nx_ref[i], 1 - slot):
                c.start()

        wg_s[...] = wg_f[slot].astype(BF16)
        wu_s[...] = wu_f[slot].astype(BF16)
        wd_s[...] = wd_f[slot].astype(BF16)

    @pl.when(rows_valid > 0)
    def _():
        row = lax.broadcasted_iota(jnp.int32, (MOE_BM, D_MODEL), 0)
        x = jnp.where(row < rows_valid, _unpack_bf16_pairs(xd_ref[...]), 0.0).astype(BF16)
        gate = jnp.dot(x, wg_s[...], preferred_element_type=F32)
        up = jnp.dot(x, wu_s[...], preferred_element_type=F32)
        act = (gate * jax.nn.sigmoid(gate) * up).astype(BF16)
        yd_ref[...] = jnp.dot(act, wd_s[...], preferred_element_type=F32)

    @pl.when(rows_valid <= 0)
    def _():
        yd_ref[...] = jnp.zeros(yd_ref.shape, F32)


def _moe_experts(l, block_e, rows_valid, next_e, slot, xd, w_gate, w_up, w_down):
    n_blocks = xd.shape[0] // MOE_BM
    row = lambda i, *_: (i, 0)
    return pl.pallas_call(
        functools.partial(_moe_kernel, layer=l),
        grid_spec=pltpu.PrefetchScalarGridSpec(
            num_scalar_prefetch=4,
            grid=(n_blocks,),
            in_specs=[
                pl.BlockSpec((MOE_BM, D_MODEL // 2), row),
                pl.BlockSpec(memory_space=pl.ANY),
                pl.BlockSpec(memory_space=pl.ANY),
                pl.BlockSpec(memory_space=pl.ANY),
            ],
            out_specs=pl.BlockSpec((MOE_BM, D_MODEL), row),
            scratch_shapes=[
                pltpu.VMEM((2, D_MODEL, EXPERT_FF), F32),
                pltpu.VMEM((2, D_MODEL, EXPERT_FF), F32),
                pltpu.VMEM((2, EXPERT_FF, D_MODEL), F32),
                pltpu.VMEM((D_MODEL, EXPERT_FF), BF16),
                pltpu.VMEM((D_MODEL, EXPERT_FF), BF16),
                pltpu.VMEM((EXPERT_FF, D_MODEL), BF16),
                pltpu.SemaphoreType.DMA((2, 3)),
            ],
        ),
        out_shape=jax.ShapeDtypeStruct((n_blocks * MOE_BM, D_MODEL), F32),
        compiler_params=pltpu.CompilerParams(
            dimension_semantics=("arbitrary",), vmem_limit_bytes=VMEM_LIMIT),
        name="moe_experts",
    )(block_e, rows_valid, next_e, slot, xd, w_gate, w_up, w_down)


def _sc_worker_id():
    return lax.axis_index("s") * SC_CORES + lax.axis_index("c")


def _sc_dispatch(hp, hs, dest_p, dest_s, n_rows):
    tp, width = hp.shape
    per_w = tp // SC_WORKERS
    n_ch = per_w // DISP_CH
    n_sw = hs.shape[0] // SAMPLE_CH
    mesh = plsc.VectorSubcoreMesh(core_axis_name="c", subcore_axis_name="s")

    @functools.partial(
        pl.kernel, mesh=mesh,
        out_type=jax.ShapeDtypeStruct((n_rows, width), jnp.int32),
        scratch_types=[
            pltpu.VMEM((2, n_ch, DISP_CH), jnp.int32),
            pltpu.VMEM((2, 1, SAMPLE_CH), jnp.int32),
            pltpu.VMEM((2, DISP_CH, width), jnp.int32),
            pltpu.SemaphoreType.DMA((2,)),
            pltpu.SemaphoreType.DMA((2,)),
        ],
        name="sc_dispatch",
    )
    def k(hp_hbm, hs_hbm, dp_hbm, ds_hbm, xd_hbm, idx_v, idxs_v, bufs, rsem, wsem):
        wid = _sc_worker_id()
        base = wid * per_w
        for kk in range(2):
            pltpu.sync_copy(dp_hbm.at[kk, wid], idx_v.at[kk])
        reads = [pltpu.make_async_copy(hp_hbm.at[pl.ds(base + j * DISP_CH, DISP_CH)],
                                       bufs.at[j % 2], rsem.at[j % 2]) for j in range(n_ch)]
        reads[0].start()
        for j in range(n_ch):
            if j + 1 < n_ch:
                reads[j + 1].start()
            reads[j].wait()
            writes = [pltpu.make_async_copy(bufs.at[j % 2], xd_hbm.at[idx_v.at[kk, j]], wsem.at[kk])
                      for kk in range(2)]
            for w in writes:
                w.start()
            for w in writes:
                w.wait()

        @pl.when(wid < n_sw)
        def _():
            rows = bufs.at[0, pl.ds(0, SAMPLE_CH)]
            for kk in range(2):
                pltpu.sync_copy(ds_hbm.at[kk, wid], idxs_v.at[kk])
            pltpu.sync_copy(hs_hbm.at[pl.ds(wid * SAMPLE_CH, SAMPLE_CH)], rows)
            for kk in range(2):
                pltpu.sync_copy(rows, xd_hbm.at[idxs_v.at[kk, 0]])

    return k(hp, hs, dest_p, dest_s)


def _sc_combine_gather(yd, dest_p, dest_s, tp, ts):
    width = yd.shape[1]
    per_w = tp // SC_WORKERS
    n_ch = per_w // COMB_CH
    n_sw = ts // SAMPLE_CH
    mesh = plsc.VectorSubcoreMesh(core_axis_name="c", subcore_axis_name="s")

    @functools.partial(
        pl.kernel, mesh=mesh,
        out_type=jax.ShapeDtypeStruct((2, tp + ts, width), F32),
        scratch_types=[
            pltpu.VMEM((2, n_ch, COMB_CH), jnp.int32),
            pltpu.VMEM((2, 1, SAMPLE_CH), jnp.int32),
            pltpu.VMEM((2, COMB_CH, width), F32),
            pltpu.SemaphoreType.DMA((2,)),
            pltpu.SemaphoreType.DMA((2,)),
        ],
        name="sc_combine_gather",
    )
    def k(yd_hbm, dp_hbm, ds_hbm, g_hbm, idx_v, idxs_v, bufs, gsem, wsem):
        wid = _sc_worker_id()
        base = wid * per_w
        for kk in range(2):
            pltpu.sync_copy(dp_hbm.at[kk, wid], idx_v.at[kk])
        items = [(kk, j) for kk in range(2) for j in range(n_ch)]
        gathers = [pltpu.make_async_copy(yd_hbm.at[idx_v.at[kk, j]], bufs.at[n % 2], gsem.at[n % 2])
                   for n, (kk, j) in enumerate(items)]
        gathers[0].start()
        for n, (kk, j) in enumerate(items):
            if n + 1 < len(items):
                gathers[n + 1].start()
            gathers[n].wait()
            w = pltpu.make_async_copy(bufs.at[n % 2], g_hbm.at[kk, pl.ds(base + j * COMB_CH, COMB_CH)],
                                      wsem.at[n % 2])
            w.start()
            w.wait()

        @pl.when(wid < n_sw)
        def _():
            for kk in range(2):
                pltpu.sync_copy(ds_hbm.at[kk, wid], idxs_v.at[kk])
            for kk in range(2):
                pltpu.sync_copy(yd_hbm.at[idxs_v.at[kk, 0]], bufs.at[kk])
                pltpu.sync_copy(bufs.at[kk], g_hbm.at[kk, pl.ds(tp + wid * SAMPLE_CH, SAMPLE_CH)])

    return k(yd, dest_p, dest_s)


def _combine_kernel(x1_ref, g_ref, route_ref, x2_ref):
    w1 = route_ref[:, 2:3]
    w2 = route_ref[:, 3:4]
    x2_ref[...] = x1_ref[...] + g_ref[0] * w1 + g_ref[1] * w2


def _combine(x1, g, route, row0, tm):
    t = x1.shape[0]
    blk0 = row0 // tm
    row = lambda i: (i, 0)
    return pl.pallas_call(
        _combine_kernel,
        grid=(t // tm,),
        in_specs=[
            pl.BlockSpec((tm, D_MODEL), row),
            pl.BlockSpec((2, tm, D_MODEL), lambda i: (0, blk0 + i, 0)),
            pl.BlockSpec((tm, LANES), row),
        ],
        out_specs=pl.BlockSpec((tm, D_MODEL), row),
        out_shape=jax.ShapeDtypeStruct((t, D_MODEL), F32),
        compiler_params=pltpu.CompilerParams(
            dimension_semantics=("arbitrary",), vmem_limit_bytes=VMEM_LIMIT),
        name="combine",
    )(x1, g, route)


def _dest_layout(dest, workers, chunk):
    t = dest.shape[1]
    return dest.reshape(2, workers, t // (workers * chunk), chunk)


def _hier_moe(l, h2p, h2s, route_p, route_s, route_tp, route_ts, counts, x1p, x1s, w_gate, w_up, w_down):
    tp, ts = h2p.shape[0], h2s.shape[0]
    n_assign = 2 * (tp + ts)
    n_blocks = -(-n_assign // MOE_BM) + N_EXPERTS
    pcounts = (counts + MOE_BM - 1) // MOE_BM * MOE_BM
    pends = jnp.cumsum(pcounts)
    poffsets = pends - pcounts
    starts = jnp.arange(n_blocks, dtype=jnp.int32) * MOE_BM
    block_e = jnp.minimum(jnp.sum((pends[None, :] <= starts[:, None]).astype(jnp.int32), axis=1),
                          N_EXPERTS - 1)
    rows_valid = jnp.clip(poffsets[block_e] + counts[block_e] - starts, 0, MOE_BM).astype(jnp.int32)
    used = counts > 0
    last_e = jnp.max(jnp.where(used, jnp.arange(N_EXPERTS, dtype=jnp.int32), 0))
    block_e = jnp.where(rows_valid > 0, block_e, last_e).astype(jnp.int32)
    later = used[None, :] & (jnp.arange(N_EXPERTS)[None, :] > jnp.arange(N_EXPERTS)[:, None])
    next_used = jnp.min(jnp.where(later, jnp.arange(N_EXPERTS, dtype=jnp.int32)[None, :], N_EXPERTS), axis=1)
    next_used = jnp.where(next_used >= N_EXPERTS, -1, next_used).astype(jnp.int32)
    slot_of = ((jnp.cumsum(used.astype(jnp.int32)) - 1) & 1).astype(jnp.int32)
    next_e = next_used[block_e]
    slot = slot_of[block_e]

    def dest_of(route_t):
        return poffsets[route_t[0:2].astype(jnp.int32)] + route_t[4:6].astype(jnp.int32)

    dest_p, dest_s = dest_of(route_tp), dest_of(route_ts)
    n_sw = ts // SAMPLE_CH
    xd = _sc_dispatch(h2p, h2s, _dest_layout(dest_p, SC_WORKERS, DISP_CH),
                      _dest_layout(dest_s, n_sw, SAMPLE_CH), n_blocks * MOE_BM)
    yd = _moe_experts(l, block_e, rows_valid, next_e, slot, xd, w_gate, w_up, w_down)
    g = _sc_combine_gather(yd, _dest_layout(dest_p, SC_WORKERS, COMB_CH),
                           _dest_layout(dest_s, n_sw, SAMPLE_CH), tp, ts)
    return _combine(x1p, g, route_p, 0, TM_PROJ), _combine(x1s, g, route_s, tp, ts)


def kernel(x_prompt, x_sample, state_pool, cache_k_win, cache_v_win, norm_attn_g, w_in, pool_w, pool_scale, q_norm_g, k_norm_g, attn_sinks, w_out, norm_ffn_g, router_group_w, router_group_b, router_expert_w, router_expert_b, w_gate, w_up, w_down):
    n_p, t_p, d = x_prompt.shape
    n_s, t_s, _ = x_sample.shape
    depth = w_in.shape[0]
    lw_s = cache_k_win.shape[2]
    assert t_s == 1 and lw_s == WINDOW and d == D_MODEL
    assert t_p % TM_PROJ == 0 and t_p >= WINDOW
    past_len = 16384

    w_in_b = w_in.astype(BF16)
    w_out_b = w_out.astype(BF16)
    g_attn = norm_attn_g.reshape(depth, 1, D_MODEL)
    g_ffn = norm_ffn_g.reshape(depth, 1, D_MODEL)
    qg = (jnp.tile(q_norm_g, (1, N_HEADS)) * ATTN_SCALE).reshape(depth, 1, Q_W)
    kg = jnp.tile(k_norm_g, (1, N_KV_HEADS)).reshape(depth, 1, KV_W)
    seg = jnp.arange(256) // HEAD_DIM
    bd = jnp.where(seg[:, None] == seg[None, :], 1.0 / HEAD_DIM, 0.0).astype(BF16)
    wp = jnp.zeros((depth, 2, 256, 256), F32)
    for p in range(2):
        wp = wp.at[:, p, :POOL_GC, :POOL_GC].set(pool_w[:, 2 * p])
        wp = wp.at[:, p, POOL_GC:, POOL_GC:].set(pool_w[:, 2 * p + 1])
    wp = wp.astype(BF16)
    ps = pool_scale.reshape(depth, 1, POOL_W)
    wr = jnp.zeros((depth, D_MODEL, LANES), F32)
    wr = wr.at[:, :, :N_EXPERTS].set(router_expert_w)
    wr = wr.at[:, :, GROUP_LANE0:GROUP_LANE0 + N_EXPERT_GROUPS].set(router_group_w)
    wr = wr.astype(BF16)
    br = jnp.zeros((depth, 1, LANES), F32)
    br = br.at[:, 0, :N_EXPERTS].set(router_expert_b)
    br = br.at[:, 0, GROUP_LANE0:GROUP_LANE0 + N_EXPERT_GROUPS].set(router_group_b)

    slopes = jnp.exp2(-8.0 * jnp.arange(1, N_HEADS + 1, dtype=F32) / N_HEADS)
    bias_p = _prompt_bias()
    jpos = jnp.arange(WINDOW, dtype=F32)
    dist_s = jnp.where(jpos == 0, 0.0, WINDOW - jpos)
    bias_s = -slopes[:, None] * dist_s[None, :]
    state2d = state_pool.reshape(depth, n_s, POOL_STATE * POOL_W)
    ck_all = cache_k_win.reshape(depth, n_s, lw_s, KV_W)
    cv_all = cache_v_win.reshape(depth, n_s, lw_s, KV_W)

    xp = x_prompt.reshape(n_p * t_p, D_MODEL)
    xs = x_sample.reshape(n_s, D_MODEL)
    lw_p = min(WINDOW, t_p)
    pool_p, kp_new, vp_new, pool_s, ks_new, vs_new = [], [], [], [], [], []
    zero_cnt = jnp.zeros((1, LANES), F32)
    for l in range(depth):
        sinks = attn_sinks[l]
        pool_o, q, k, v, utail = _proj_pool_prompt(l, xp, n_p, t_p, g_attn, w_in_b, qg, kg, bd, wp, ps)
        attn_o = _attn_prompt(q, k, v, bias_p, sinks, n_p, t_p)
        x1p, h2p, route_p, route_tp, cnt_p = _merge_router(
            l, pool_o, attn_o, xp, w_out_b, g_ffn, wr, br, zero_cnt, TM_PROJ)
        pool_p.append(utail[:, 16 - POOL_STATE:, :])
        kp_new.append(k.reshape(n_p, t_p, N_KV_HEADS, HEAD_DIM)[:, t_p - lw_p:])
        vp_new.append(v.reshape(n_p, t_p, N_KV_HEADS, HEAD_DIM)[:, t_p - lw_p:])
        pool_so, attn_so, u_s, k_s, v_s = _sample_mixer(
            l, xs, g_attn, w_in_b, qg, kg, bd, wp, ps, state2d, ck_all, cv_all,
            sinks.reshape(N_HEADS, 1), bias_s, past_len)
        x1s, h2s, route_s, route_ts, cnt_all = _merge_router(
            l, pool_so, attn_so, xs, w_out_b, g_ffn, wr, br, cnt_p, n_s)
        pool_s.append(jnp.concatenate([state_pool[l][:, 1:], u_s[:, None, :]], axis=1))
        ks_new.append(jnp.concatenate(
            [cache_k_win[l][:, 1:], k_s.reshape(n_s, 1, N_KV_HEADS, HEAD_DIM)], axis=1))
        vs_new.append(jnp.concatenate(
            [cache_v_win[l][:, 1:], v_s.reshape(n_s, 1, N_KV_HEADS, HEAD_DIM)], axis=1))
        counts = cnt_all[0, :N_EXPERTS].astype(jnp.int32)
        xp, xs = _hier_moe(l, h2p, h2s, route_p, route_s, route_tp, route_ts, counts, x1p, x1s,
                           w_gate, w_up, w_down)
    return (xp.reshape(n_p, t_p, D_MODEL), xs.reshape(n_s, t_s, D_MODEL),
            jnp.stack(pool_p), jnp.stack(kp_new), jnp.stack(vp_new),
            jnp.stack(pool_s), jnp.stack(ks_new), jnp.stack(vs_new))
```

```python
import functools

import jax
import jax.numpy as jnp
from jax import lax
from jax.experimental import pallas as pl
from jax.experimental.pallas import tpu as pltpu
from jax.experimental.pallas import tpu_sc as plsc

D_MODEL = 1024
POOL_W = 512
POOL_WINDOWS = (2, 4, 8, 16)
POOL_GC = 128
POOL_STATE = 15
HEAD_DIM = 64
N_HEADS = 8
N_KV_HEADS = 2
GQA_GROUP = 4
Q_W = 512
KV_W = 128
D_IN = POOL_W + Q_W + 2 * KV_W
WINDOW = 128
ATTN_SCALE = HEAD_DIM ** -0.5
N_EXPERT_GROUPS = 4
EXPERTS_PER_GROUP = 8
N_EXPERTS = 32
EXPERT_FF = 512
EPS = 1e-6

LANES = 128
HALO = 32
TM_PROJ = 512
MOE_BM = 256
GROUP_LANE0 = 32
ROUTE_FIELDS = 8
SC_CORES = 2
SC_SUBCORES = 16
SC_WORKERS = SC_CORES * SC_SUBCORES
DISP_CH = 64
COMB_CH = 32
SAMPLE_CH = 32
VMEM_LIMIT = 48 * 1024 * 1024

BF16 = jnp.bfloat16
F32 = jnp.float32


def _pack_bf16_pairs(h):
    w = h.shape[1] // 2
    hi = lax.bitcast_convert_type(h[:, :w].astype(F32), jnp.uint32)
    lo = lax.bitcast_convert_type(h[:, w:].astype(F32), jnp.uint32)
    return lax.bitcast_convert_type(hi | (lo >> 16), jnp.int32)


def _unpack_bf16_pairs(words):
    u = lax.bitcast_convert_type(words, jnp.uint32)
    hi = lax.bitcast_convert_type(u & jnp.uint32(0xFFFF0000), F32)
    lo = lax.bitcast_convert_type(u << 16, F32)
    return jnp.concatenate([hi, lo], axis=-1)


def _segment_mean_sq(a, bd):
    w = a.shape[1]
    return jnp.dot((a * a).astype(BF16), bd[:w, :w], preferred_element_type=F32)


def _project(x, g, w_in, qg, kg, bd):
    ms = jnp.mean(x * x, axis=-1, keepdims=True)
    h = (x * lax.rsqrt(ms + EPS) * g).astype(BF16)
    z = jnp.dot(h, w_in, preferred_element_type=F32)
    u = z[:, :POOL_W]
    q = z[:, POOL_W:POOL_W + Q_W]
    k = z[:, POOL_W + Q_W:POOL_W + Q_W + KV_W]
    v = z[:, POOL_W + Q_W + KV_W:]
    qn = []
    for c in range(Q_W // 256):
        qc = q[:, c * 256:(c + 1) * 256]
        qn.append(qc * lax.rsqrt(_segment_mean_sq(qc, bd) + EPS))
    qn = jnp.concatenate(qn, axis=-1) * qg
    kn = k * lax.rsqrt(_segment_mean_sq(k, bd) + EPS) * kg
    return u, qn, kn, v


def _pool_project(d_groups, wp_ref, ps):
    outs = []
    for p in range(2):
        dp = jnp.concatenate([d_groups[2 * p], d_groups[2 * p + 1]], axis=-1).astype(BF16)
        y = jnp.dot(dp, wp_ref[p], preferred_element_type=F32)
        outs.append(y * ps[:, p * 256:(p + 1) * 256])
    return jnp.concatenate(outs, axis=-1)


def _proj_pool_kernel(x_ref, g_ref, win_ref, qg_ref, kg_ref, bd_ref, wp_ref, ps_ref,
                      pool_ref, q_ref, k_ref, v_ref, utail_ref,
                      ext_ref, sa_ref, sb_ref, *, tm, n_j):
    j = pl.program_id(1)
    u, qn, kn, v = _project(x_ref[...], g_ref[...], win_ref[...], qg_ref[...], kg_ref[...], bd_ref[...])
    q_ref[...] = qn.astype(BF16)
    k_ref[...] = kn
    v_ref[...] = v

    @pl.when(j == 0)
    def _():
        ext_ref[0:HALO, :] = jnp.zeros((HALO, POOL_W), F32)

    r = tm + HALO
    ext_ref[HALO:r, :] = u
    sa_ref[8:r, :] = ext_ref[8:r, :] + ext_ref[7:r - 1, :]
    sb_ref[16:r, 128:] = sa_ref[16:r, 128:] + sa_ref[14:r - 2, 128:]
    sa_ref[24:r, 256:] = sb_ref[24:r, 256:] + sb_ref[20:r - 4, 256:]
    sb_ref[32:r, 384:] = sa_ref[32:r, 384:] + sa_ref[24:r - 8, 384:]
    pos1 = j * tm + lax.broadcasted_iota(jnp.int32, (tm, POOL_GC), 0) + 1
    sums = (sa_ref, sb_ref, sa_ref, sb_ref)
    d_groups = []
    for gi, w in enumerate(POOL_WINDOWS):
        sl = slice(gi * POOL_GC, (gi + 1) * POOL_GC)
        cnt = jnp.minimum(pos1, w).astype(F32)
        d_groups.append(sums[gi][HALO:r, sl] / cnt - u[:, sl])
    pool_ref[...] = _pool_project(d_groups, wp_ref, ps_ref[...]).astype(BF16)
    ext_ref[16:HALO, :] = ext_ref[tm + 16:r, :]

    @pl.when(j == n_j - 1)
    def _():
        utail_ref[...] = u[tm - 16:, :]


def _proj_pool_prompt(l, x2d, n_seq, seq, g_attn, w_in, qg, kg, bd, wp, ps):
    tm = TM_PROJ
    n_j = seq // tm
    t = n_seq * seq
    row = lambda b, j: (b * n_j + j, 0)
    lay = lambda b, j: (l, 0, 0)
    return pl.pallas_call(
        functools.partial(_proj_pool_kernel, tm=tm, n_j=n_j),
        grid=(n_seq, n_j),
        in_specs=[
            pl.BlockSpec((tm, D_MODEL), row),
            pl.BlockSpec((None, 1, D_MODEL), lay),
            pl.BlockSpec((None, D_MODEL, D_IN), lay),
            pl.BlockSpec((None, 1, Q_W), lay),
            pl.BlockSpec((None, 1, KV_W), lay),
            pl.BlockSpec((256, 256), lambda b, j: (0, 0)),
            pl.BlockSpec((None, 2, 256, 256), lambda b, j: (l, 0, 0, 0)),
            pl.BlockSpec((None, 1, POOL_W), lay),
        ],
        out_specs=[
            pl.BlockSpec((tm, POOL_W), row),
            pl.BlockSpec((tm, Q_W), row),
            pl.BlockSpec((tm, KV_W), row),
            pl.BlockSpec((tm, KV_W), row),
            pl.BlockSpec((None, 16, POOL_W), lambda b, j: (b, 0, 0)),
        ],
        out_shape=[
            jax.ShapeDtypeStruct((t, POOL_W), BF16),
            jax.ShapeDtypeStruct((t, Q_W), BF16),
            jax.ShapeDtypeStruct((t, KV_W), F32),
            jax.ShapeDtypeStruct((t, KV_W), F32),
            jax.ShapeDtypeStruct((n_seq, 16, POOL_W), F32),
        ],
        scratch_shapes=[pltpu.VMEM((tm + HALO, POOL_W), F32)] * 3,
        compiler_params=pltpu.CompilerParams(
            dimension_semantics=("arbitrary", "arbitrary"), vmem_limit_bytes=VMEM_LIMIT),
        name="proj_pool_prompt",
    )(x2d, g_attn, w_in, qg, kg, bd, wp, ps)


def _attn_kernel(sink_ref, q_ref, kp_ref, kc_ref, vp_ref, vc_ref, bias_ref, o_ref):
    q = q_ref[...]
    kk = jnp.concatenate([kp_ref[...], kc_ref[...]], axis=0).astype(BF16)
    vv = jnp.concatenate([vp_ref[...], vc_ref[...]], axis=0).astype(BF16)
    outs = []
    for h in range(N_HEADS):
        kv = h // GQA_GROUP
        qh = q[:, h * HEAD_DIM:(h + 1) * HEAD_DIM]
        kh = kk[:, kv * HEAD_DIM:(kv + 1) * HEAD_DIM]
        s = lax.dot_general(qh, kh, (((1,), (1,)), ((), ())), preferred_element_type=F32)
        s = s + bias_ref[h]
        sink = sink_ref[h]
        m = jnp.maximum(jnp.max(s, axis=-1, keepdims=True), sink)
        p = jnp.exp(s - m)
        denom = jnp.sum(p, axis=-1, keepdims=True) + jnp.exp(sink - m)
        o = jnp.dot(p.astype(BF16), vv[:, kv * HEAD_DIM:(kv + 1) * HEAD_DIM], preferred_element_type=F32)
        outs.append(o / denom)
    o_ref[...] = jnp.concatenate(outs, axis=-1).astype(BF16)


def _attn_prompt(q, k, v, bias, sinks, n_seq, seq):
    nb = seq // WINDOW
    t = n_seq * seq
    cur = lambda b, i: (b * nb + i, 0)
    prev = lambda b, i: (b * nb + jnp.maximum(i - 1, 0), 0)
    return pl.pallas_call(
        _attn_kernel,
        grid=(n_seq, nb),
        in_specs=[
            pl.BlockSpec(memory_space=pltpu.SMEM),
            pl.BlockSpec((WINDOW, Q_W), cur),
            pl.BlockSpec((WINDOW, KV_W), prev),
            pl.BlockSpec((WINDOW, KV_W), cur),
            pl.BlockSpec((WINDOW, KV_W), prev),
            pl.BlockSpec((WINDOW, KV_W), cur),
            pl.BlockSpec((None, N_HEADS, WINDOW, 2 * WINDOW), lambda b, i: (jnp.minimum(i, 1), 0, 0, 0)),
        ],
        out_specs=pl.BlockSpec((WINDOW, Q_W), cur),
        out_shape=jax.ShapeDtypeStruct((t, Q_W), BF16),
        compiler_params=pltpu.CompilerParams(
            dimension_semantics=("arbitrary", "arbitrary"), vmem_limit_bytes=VMEM_LIMIT),
        name="attn_prompt",
    )(sinks, q, k, k, v, v, bias)


def _prompt_bias():
    r = jnp.arange(WINDOW, dtype=jnp.int32)[:, None]
    c = jnp.arange(2 * WINDOW, dtype=jnp.int32)[None, :]
    dist = WINDOW + r - c
    valid = (dist >= 0) & (dist < WINDOW)
    slopes = jnp.exp2(-8.0 * jnp.arange(1, N_HEADS + 1, dtype=F32) / N_HEADS)
    pen = -slopes[:, None, None] * dist.astype(F32)[None]
    later = jnp.where(valid[None], pen, -jnp.inf)
    first = jnp.where((valid & (c >= WINDOW))[None], pen, -jnp.inf)
    return jnp.stack([first, later])


def _sample_kernel(x_ref, g_ref, win_ref, qg_ref, kg_ref, bd_ref, wp_ref, ps_ref,
                   st_ref, ck_ref, cv_ref, sink_ref, bias_ref, perm_ref,
                   pool_ref, attn_ref, u_ref, kc_ref, vc_ref, *, ns, pos0):
    u, qn, kn, v = _project(x_ref[...], g_ref[...], win_ref[...], qg_ref[...], kg_ref[...], bd_ref[...])
    u_ref[...] = u
    kc_ref[:, 0:WINDOW - 1, :] = ck_ref[:, 1:WINDOW, :]
    vc_ref[:, 0:WINDOW - 1, :] = cv_ref[:, 1:WINDOW, :]
    for n in range(ns):
        kc_ref[n, WINDOW - 1:WINDOW, :] = kn[n:n + 1, :]
        vc_ref[n, WINDOW - 1:WINDOW, :] = v[n:n + 1, :]

    d_groups = []
    for gi, w in enumerate(POOL_WINDOWS):
        lo = gi * POOL_GC
        acc = u[:, lo:lo + POOL_GC]
        for back in range(1, w):
            off = (POOL_STATE - back) * POOL_W + lo
            acc = acc + st_ref[:, off:off + POOL_GC]
        d_groups.append(acc / float(min(pos0 + 1, w)) - u[:, lo:lo + POOL_GC])
    pool_ref[...] = _pool_project(d_groups, wp_ref, ps_ref[...]).astype(BF16)

    zeros = jnp.zeros((ns, HEAD_DIM), F32)
    stacked = []
    for h in range(N_HEADS):
        piece = qn[:, h * HEAD_DIM:(h + 1) * HEAD_DIM]
        pair = [piece, zeros] if h < GQA_GROUP else [zeros, piece]
        stacked.append(jnp.concatenate(pair, axis=-1))
    q_hn = jnp.concatenate(stacked, axis=0).astype(BF16)
    q_nh = jnp.dot(perm_ref[0], q_hn, preferred_element_type=F32).astype(BF16)

    keys = kc_ref[...].reshape(ns * WINDOW, KV_W).astype(BF16)
    vals = vc_ref[...].reshape(ns * WINDOW, KV_W).astype(BF16)
    s_all = lax.dot_general(q_nh, keys, (((1,), (1,)), ((), ())), preferred_element_type=F32)
    sink = sink_ref[...]
    bias = bias_ref[...]
    zero_blk = jnp.zeros((N_HEADS, WINDOW), F32)
    p_rows = []
    for n in range(ns):
        s = s_all[n * N_HEADS:(n + 1) * N_HEADS, n * WINDOW:(n + 1) * WINDOW] + bias
        m = jnp.maximum(jnp.max(s, axis=-1, keepdims=True), sink)
        p = jnp.exp(s - m)
        denom = jnp.sum(p, axis=-1, keepdims=True) + jnp.exp(sink - m)
        p_rows.append(jnp.concatenate([zero_blk] * n + [p / denom] + [zero_blk] * (ns - 1 - n), axis=-1))
    p_blockdiag = jnp.concatenate(p_rows, axis=0).astype(BF16)
    o_nh = jnp.dot(p_blockdiag, vals, preferred_element_type=F32).astype(BF16)
    o_hn = jnp.dot(perm_ref[1], o_nh, preferred_element_type=F32)
    pieces = []
    for h in range(N_HEADS):
        kv = h // GQA_GROUP
        pieces.append(o_hn[h * ns:(h + 1) * ns, kv * HEAD_DIM:(kv + 1) * HEAD_DIM])
    attn_ref[...] = jnp.concatenate(pieces, axis=-1).astype(BF16)


def _sample_mixer(l, xs, g_attn, w_in, qg, kg, bd, wp, ps, state2d, ck, cv, sink8, bias_s, pos0):
    n = xs.shape[0]
    ns = 32
    row = lambda i: (i, 0)
    lay = lambda i: (l, 0, 0)
    src = jnp.arange(ns * N_HEADS)
    perm = (((src % N_HEADS) * ns + src // N_HEADS)[:, None] == src[None, :]).astype(BF16)
    perms = jnp.stack([perm, perm.T])
    return pl.pallas_call(
        functools.partial(_sample_kernel, ns=ns, pos0=pos0),
        grid=(n // ns,),
        in_specs=[
            pl.BlockSpec((ns, D_MODEL), row),
            pl.BlockSpec((None, 1, D_MODEL), lay),
            pl.BlockSpec((None, D_MODEL, D_IN), lay),
            pl.BlockSpec((None, 1, Q_W), lay),
            pl.BlockSpec((None, 1, KV_W), lay),
            pl.BlockSpec((256, 256), lambda i: (0, 0)),
            pl.BlockSpec((None, 2, 256, 256), lambda i: (l, 0, 0, 0)),
            pl.BlockSpec((None, 1, POOL_W), lay),
            pl.BlockSpec((None, ns, POOL_STATE * POOL_W), lambda i: (l, i, 0)),
            pl.BlockSpec((None, ns, WINDOW, KV_W), lambda i: (l, i, 0, 0)),
            pl.BlockSpec((None, ns, WINDOW, KV_W), lambda i: (l, i, 0, 0)),
            pl.BlockSpec((N_HEADS, 1), lambda i: (0, 0)),
            pl.BlockSpec((N_HEADS, WINDOW), lambda i: (0, 0)),
            pl.BlockSpec((2, ns * N_HEADS, ns * N_HEADS), lambda i: (0, 0, 0)),
        ],
        out_specs=[
            pl.BlockSpec((ns, POOL_W), row),
            pl.BlockSpec((ns, Q_W), row),
            pl.BlockSpec((ns, POOL_W), row),
            pl.BlockSpec((ns, WINDOW, KV_W), lambda i: (i, 0, 0)),
            pl.BlockSpec((ns, WINDOW, KV_W), lambda i: (i, 0, 0)),
        ],
        out_shape=[
            jax.ShapeDtypeStruct((n, POOL_W), BF16),
            jax.ShapeDtypeStruct((n, Q_W), BF16),
            jax.ShapeDtypeStruct((n, POOL_W), F32),
            jax.ShapeDtypeStruct((n, WINDOW, KV_W), F32),
            jax.ShapeDtypeStruct((n, WINDOW, KV_W), F32),
        ],
        compiler_params=pltpu.CompilerParams(
            dimension_semantics=("arbitrary",), vmem_limit_bytes=VMEM_LIMIT),
        name="sample_mixer",
    )(xs, g_attn, w_in, qg, kg, bd, wp, ps, state2d, ck, cv, sink8, bias_s, perms)


def _merge_router_kernel(pool_ref, attn_ref, x_ref, wout_ref, g_ref, wr_ref, br_ref, ltri_ref, cin_ref,
                         x1_ref, h2_ref, route_ref, route_t_ref, cnt_ref):
    i = pl.program_id(0)

    @pl.when(i == 0)
    def _():
        cnt_ref[...] = cin_ref[...]

    y = jnp.dot(pool_ref[...], wout_ref[0:POOL_W, :], preferred_element_type=F32)
    y = y + jnp.dot(attn_ref[...], wout_ref[POOL_W:, :], preferred_element_type=F32)
    x1 = x_ref[...] + y
    x1_ref[...] = x1
    ms = jnp.mean(x1 * x1, axis=-1, keepdims=True)
    h2 = (x1 * lax.rsqrt(ms + EPS) * g_ref[...]).astype(BF16)
    h2_ref[...] = _pack_bf16_pairs(h2)
    logits = jnp.dot(h2, wr_ref[...], preferred_element_type=F32) + br_ref[...]

    tm = logits.shape[0]
    lane = lax.broadcasted_iota(jnp.int32, (tm, LANES), 1)
    big = jnp.int32(2 * LANES)
    neg = -jnp.inf
    is_group = (lane >= GROUP_LANE0) & (lane < GROUP_LANE0 + N_EXPERT_GROUPS)
    gl = jnp.where(is_group, logits, neg)
    gmax = jnp.max(gl, axis=-1, keepdims=True)
    grp = jnp.min(jnp.where(gl == gmax, lane, big), axis=-1, keepdims=True) - GROUP_LANE0
    g_w = 1.0 / jnp.sum(jnp.exp(gl - gmax), axis=-1, keepdims=True)
    in_grp = (lane >= grp * EXPERTS_PER_GROUP) & (lane < (grp + 1) * EXPERTS_PER_GROUP)
    el = jnp.where(in_grp, logits, neg)
    v1 = jnp.max(el, axis=-1, keepdims=True)
    i1 = jnp.min(jnp.where(el == v1, lane, big), axis=-1, keepdims=True)
    el2 = jnp.where(lane == i1, neg, el)
    v2 = jnp.max(el2, axis=-1, keepdims=True)
    i2 = jnp.min(jnp.where(el2 == v2, lane, big), axis=-1, keepdims=True)
    e21 = jnp.exp(v2 - v1)
    w1 = g_w / (1.0 + e21)
    w2 = g_w * e21 / (1.0 + e21)

    oh1 = lane == i1
    oh2 = lane == i2
    c = jnp.where(oh1 | oh2, 1.0, 0.0)
    prefix = jnp.dot(ltri_ref[...], c.astype(BF16), preferred_element_type=F32) + cnt_ref[...]
    r1 = jnp.sum(jnp.where(oh1, prefix, 0.0), axis=-1, keepdims=True)
    r2 = jnp.sum(jnp.where(oh2, prefix, 0.0), axis=-1, keepdims=True)
    cnt_ref[...] = cnt_ref[...] + jnp.sum(c, axis=0, keepdims=True)

    out = jnp.zeros((tm, LANES), F32)
    for idx, val in enumerate((i1.astype(F32), i2.astype(F32), w1, w2, r1, r2)):
        out = jnp.where(lane == idx, val, out)
    route_ref[...] = out
    route_t_ref[...] = jnp.transpose(out)[0:ROUTE_FIELDS, :]


def _merge_router(l, pool, attn, x2d, w_out, g_ffn, wr, br, cnt_in, tm):
    t = x2d.shape[0]
    ltri = (jnp.arange(tm)[:, None] > jnp.arange(tm)[None, :]).astype(BF16)
    row = lambda i: (i, 0)
    lay = lambda i: (l, 0, 0)
    return pl.pallas_call(
        _merge_router_kernel,
        grid=(t // tm,),
        in_specs=[
            pl.BlockSpec((tm, POOL_W), row),
            pl.BlockSpec((tm, Q_W), row),
            pl.BlockSpec((tm, D_MODEL), row),
            pl.BlockSpec((None, D_MODEL, D_MODEL), lay),
            pl.BlockSpec((None, 1, D_MODEL), lay),
            pl.BlockSpec((None, D_MODEL, LANES), lay),
            pl.BlockSpec((None, 1, LANES), lay),
            pl.BlockSpec((tm, tm), lambda i: (0, 0)),
            pl.BlockSpec((1, LANES), lambda i: (0, 0)),
        ],
        out_specs=[
            pl.BlockSpec((tm, D_MODEL), row),
            pl.BlockSpec((tm, D_MODEL // 2), row),
            pl.BlockSpec((tm, LANES), row),
            pl.BlockSpec((ROUTE_FIELDS, tm), lambda i: (0, i)),
            pl.BlockSpec((1, LANES), lambda i: (0, 0)),
        ],
        out_shape=[
            jax.ShapeDtypeStruct((t, D_MODEL), F32),
            jax.ShapeDtypeStruct((t, D_MODEL // 2), jnp.int32),
            jax.ShapeDtypeStruct((t, LANES), F32),
            jax.ShapeDtypeStruct((ROUTE_FIELDS, t), F32),
            jax.ShapeDtypeStruct((1, LANES), F32),
        ],
        compiler_params=pltpu.CompilerParams(
            dimension_semantics=("arbitrary",), vmem_limit_bytes=VMEM_LIMIT),
        name="merge_router",
    )(pool, attn, x2d, w_out, g_ffn, wr, br, ltri, cnt_in)


def _moe_kernel(be_ref, rv_ref, nx_ref, sl_ref, xd_ref, wg_hbm, wu_hbm, wd_hbm, yd_ref,
                wg_f, wu_f, wd_f, wg_s, wu_s, wd_s, sem, *, layer):
    i = pl.program_id(0)
    expert = be_ref[i]
    changed = (i == 0) | (expert != be_ref[jnp.maximum(i - 1, 0)])
    rows_valid = rv_ref[i]
    slot = sl_ref[i]

    def weight_copies(e, s):
        return [pltpu.make_async_copy(w_hbm.at[layer, e], w_f.at[s], sem.at[s, n])
                for n, (w_hbm, w_f) in enumerate(((wg_hbm, wg_f), (wu_hbm, wu_f), (wd_hbm, wd_f)))]

    @pl.when(i == 0)
    def _():
        for c in weight_copies(expert, slot):
            c.start()

    @pl.when(changed)
    def _():
        for c in weight_copies(expert, slot):
            c.wait()

        @pl.when(nx_ref[i] >= 0)
        def _():
            for c in weight_copies(nx_ref[i], 1 - slot):
                c.start()

        wg_s[...] = wg_f[slot].astype(BF16)
        wu_s[...] = wu_f[slot].astype(BF16)
        wd_s[...] = wd_f[slot].astype(BF16)

    @pl.when(rows_valid > 0)
    def _():
        row = lax.broadcasted_iota(jnp.int32, (MOE_BM, D_MODEL), 0)
        x = jnp.where(row < rows_valid, _unpack_bf16_pairs(xd_ref[...]), 0.0).astype(BF16)
        gate = jnp.dot(x, wg_s[...], preferred_element_type=F32)
        up = jnp.dot(x, wu_s[...], preferred_element_type=F32)
        act = (gate * jax.nn.sigmoid(gate) * up).astype(BF16)
        yd_ref[...] = jnp.dot(act, wd_s[...], preferred_element_type=F32)

    @pl.when(rows_valid <= 0)
    def _():
        yd_ref[...] = jnp.zeros(yd_ref.shape, F32)


def _moe_experts(l, block_e, rows_valid, next_e, slot, xd, w_gate, w_up, w_down):
    n_blocks = xd.shape[0] // MOE_BM
    row = lambda i, *_: (i, 0)
    return pl.pallas_call(
        functools.partial(_moe_kernel, layer=l),
        grid_spec=pltpu.PrefetchScalarGridSpec(
            num_scalar_prefetch=4,
            grid=(n_blocks,),
            in_specs=[
                pl.BlockSpec((MOE_BM, D_MODEL // 2), row),
                pl.BlockSpec(memory_space=pl.ANY),
                pl.BlockSpec(memory_space=pl.ANY),
                pl.BlockSpec(memory_space=pl.ANY),
            ],
            out_specs=pl.BlockSpec((MOE_BM, D_MODEL), row),
            scratch_shapes=[
                pltpu.VMEM((2, D_MODEL, EXPERT_FF), F32),
                pltpu.VMEM((2, D_MODEL, EXPERT_FF), F32),
                pltpu.VMEM((2, EXPERT_FF, D_MODEL), F32),
                pltpu.VMEM((D_MODEL, EXPERT_FF), BF16),
                pltpu.VMEM((D_MODEL, EXPERT_FF), BF16),
                pltpu.VMEM((EXPERT_FF, D_MODEL), BF16),
                pltpu.SemaphoreType.DMA((2, 3)),
            ],
        ),
        out_shape=jax.ShapeDtypeStruct((n_blocks * MOE_BM, D_MODEL), F32),
        compiler_params=pltpu.CompilerParams(
            dimension_semantics=("arbitrary",), vmem_limit_bytes=VMEM_LIMIT),
        name="moe_experts",
    )(block_e, rows_valid, next_e, slot, xd, w_gate, w_up, w_down)


def _sc_worker_id():
    return lax.axis_index("s") * SC_CORES + lax.axis_index("c")


def _sc_dispatch(hp, hs, dest_p, dest_s, n_rows):
    tp, width = hp.shape
    per_w = tp // SC_WORKERS
    n_ch = per_w // DISP_CH
    n_sw = hs.shape[0] // SAMPLE_CH
    mesh = plsc.VectorSubcoreMesh(core_axis_name="c", subcore_axis_name="s")

    @functools.partial(
        pl.kernel, mesh=mesh,
        out_type=jax.ShapeDtypeStruct((n_rows, width), jnp.int32),
        scratch_types=[
            pltpu.VMEM((2, n_ch, DISP_CH), jnp.int32),
            pltpu.VMEM((2, 1, SAMPLE_CH), jnp.int32),
            pltpu.VMEM((2, DISP_CH, width), jnp.int32),
            pltpu.SemaphoreType.DMA((2,)),
            pltpu.SemaphoreType.DMA((2,)),
        ],
        name="sc_dispatch",
    )
    def k(hp_hbm, hs_hbm, dp_hbm, ds_hbm, xd_hbm, idx_v, idxs_v, bufs, rsem, wsem):
        wid = _sc_worker_id()
        base = wid * per_w
        for kk in range(2):
            pltpu.sync_copy(dp_hbm.at[kk, wid], idx_v.at[kk])
        reads = [pltpu.make_async_copy(hp_hbm.at[pl.ds(base + j * DISP_CH, DISP_CH)],
                                       bufs.at[j % 2], rsem.at[j % 2]) for j in range(n_ch)]
        reads[0].start()
        for j in range(n_ch):
            if j + 1 < n_ch:
                reads[j + 1].start()
            reads[j].wait()
            writes = [pltpu.make_async_copy(bufs.at[j % 2], xd_hbm.at[idx_v.at[kk, j]], wsem.at[kk])
                      for kk in range(2)]
            for w in writes:
                w.start()
            for w in writes:
                w.wait()

        @pl.when(wid < n_sw)
        def _():
            rows = bufs.at[0, pl.ds(0, SAMPLE_CH)]
            for kk in range(2):
                pltpu.sync_copy(ds_hbm.at[kk, wid], idxs_v.at[kk])
            pltpu.sync_copy(hs_hbm.at[pl.ds(wid * SAMPLE_CH, SAMPLE_CH)], rows)
            for kk in range(2):
                pltpu.sync_copy(rows, xd_hbm.at[idxs_v.at[kk, 0]])

    return k(hp, hs, dest_p, dest_s)


def _sc_combine_gather(yd, dest_p, dest_s, tp, ts):
    width = yd.shape[1]
    per_w = tp // SC_WORKERS
    n_ch = per_w // COMB_CH
    n_sw = ts // SAMPLE_CH
    mesh = plsc.VectorSubcoreMesh(core_axis_name="c", subcore_axis_name="s")

    @functools.partial(
        pl.kernel, mesh=mesh,
        out_type=jax.ShapeDtypeStruct((2, tp + ts, width), F32),
        scratch_types=[
            pltpu.VMEM((2, n_ch, COMB_CH), jnp.int32),
            pltpu.VMEM((2, 1, SAMPLE_CH), jnp.int32),
            pltpu.VMEM((2, COMB_CH, width), F32),
            pltpu.SemaphoreType.DMA((2,)),
            pltpu.SemaphoreType.DMA((2,)),
        ],
        name="sc_combine_gather",
    )
    def k(yd_hbm, dp_hbm, ds_hbm, g_hbm, idx_v, idxs_v, bufs, gsem, wsem):
        wid = _sc_worker_id()
        base = wid * per_w
        for kk in range(2):
            pltpu.sync_copy(dp_hbm.at[kk, wid], idx_v.at[kk])
        items = [(kk, j) for kk in range(2) for j in range(n_ch)]
        gathers = [pltpu.make_async_copy(yd_hbm.at[idx_v.at[kk, j]], bufs.at[n % 2], gsem.at[n % 2])
                   for n, (kk, j) in enumerate(items)]
        gathers[0].start()
        for n, (kk, j) in enumerate(items):
            if n + 1 < len(items):
                gathers[n + 1].start()
            gathers[n].wait()
            w = pltpu.make_async_copy(bufs.at[n % 2], g_hbm.at[kk, pl.ds(base + j * COMB_CH, COMB_CH)],
                                      wsem.at[n % 2])
            w.start()
            w.wait()

        @pl.when(wid < n_sw)
        def _():
            for kk in range(2):
                pltpu.sync_copy(ds_hbm.at[kk, wid], idxs_v.at[kk])
            for kk in range(2):
                pltpu.sync_copy(yd_hbm.at[idxs_v.at[kk, 0]], bufs.at[kk])
                pltpu.sync_copy(bufs.at[kk], g_hbm.at[kk, pl.ds(tp + wid * SAMPLE_CH, SAMPLE_CH)])

    return k(yd, dest_p, dest_s)


def _combine_kernel(x1_ref, g_ref, route_ref, x2_ref):
    w1 = route_ref[:, 2:3]
    w2 = route_ref[:, 3:4]
    x2_ref[...] = x1_ref[...] + g_ref[0] * w1 + g_ref[1] * w2


def _combine(x1, g, route, row0, tm):
    t = x1.shape[0]
    blk0 = row0 // tm
    row = lambda i: (i, 0)
    return pl.pallas_call(
        _combine_kernel,
        grid=(t // tm,),
        in_specs=[
            pl.BlockSpec((tm, D_MODEL), row),
            pl.BlockSpec((2, tm, D_MODEL), lambda i: (0, blk0 + i, 0)),
            pl.BlockSpec((tm, LANES), row),
        ],
        out_specs=pl.BlockSpec((tm, D_MODEL), row),
        out_shape=jax.ShapeDtypeStruct((t, D_MODEL), F32),
        compiler_params=pltpu.CompilerParams(
            dimension_semantics=("arbitrary",), vmem_limit_bytes=VMEM_LIMIT),
        name="combine",
    )(x1, g, route)


def _dest_layout(dest, workers, chunk):
    t = dest.shape[1]
    return dest.reshape(2, workers, t // (workers * chunk), chunk)


def _hier_moe(l, h2p, h2s, route_p, route_s, route_tp, route_ts, counts, x1p, x1s, w_gate, w_up, w_down):
    tp, ts = h2p.shape[0], h2s.shape[0]
    n_assign = 2 * (tp + ts)
    n_blocks = -(-n_assign // MOE_BM) + N_EXPERTS
    pcounts = (counts + MOE_BM - 1) // MOE_BM * MOE_BM
    pends = jnp.cumsum(pcounts)
    poffsets = pends - pcounts
    starts = jnp.arange(n_blocks, dtype=jnp.int32) * MOE_BM
    block_e = jnp.minimum(jnp.sum((pends[None, :] <= starts[:, None]).astype(jnp.int32), axis=1),
                          N_EXPERTS - 1)
    experts = jnp.arange(N_EXPERTS, dtype=jnp.int32)

    def lookup(table, idx):
        return jnp.sum(jnp.where(idx[..., None] == experts, table, 0), axis=-1)

    rows_valid = jnp.clip(lookup(poffsets + counts, block_e) - starts, 0, MOE_BM).astype(jnp.int32)
    used = counts > 0
    last_e = jnp.max(jnp.where(used, jnp.arange(N_EXPERTS, dtype=jnp.int32), 0))
    block_e = jnp.where(rows_valid > 0, block_e, last_e).astype(jnp.int32)
    later = used[None, :] & (jnp.arange(N_EXPERTS)[None, :] > jnp.arange(N_EXPERTS)[:, None])
    next_used = jnp.min(jnp.where(later, jnp.arange(N_EXPERTS, dtype=jnp.int32)[None, :], N_EXPERTS), axis=1)
    next_used = jnp.where(next_used >= N_EXPERTS, -1, next_used).astype(jnp.int32)
    slot_of = ((jnp.cumsum(used.astype(jnp.int32)) - 1) & 1).astype(jnp.int32)
    next_e = lookup(next_used, block_e)
    slot = lookup(slot_of, block_e)

    def dest_of(route_t):
        return lookup(poffsets, route_t[0:2].astype(jnp.int32)) + route_t[4:6].astype(jnp.int32)

    dest_p, dest_s = dest_of(route_tp), dest_of(route_ts)
    n_sw = ts // SAMPLE_CH
    xd = _sc_dispatch(h2p, h2s, _dest_layout(dest_p, SC_WORKERS, DISP_CH),
                      _dest_layout(dest_s, n_sw, SAMPLE_CH), n_blocks * MOE_BM)
    yd = _moe_experts(l, block_e, rows_valid, next_e, slot, xd, w_gate, w_up, w_down)
    g = _sc_combine_gather(yd, _dest_layout(dest_p, SC_WORKERS, COMB_CH),
                           _dest_layout(dest_s, n_sw, SAMPLE_CH), tp, ts)
    return _combine(x1p, g, route_p, 0, TM_PROJ), _combine(x1s, g, route_s, tp, ts)


def kernel(x_prompt, x_sample, state_pool, cache_k_win, cache_v_win, norm_attn_g, w_in, pool_w, pool_scale, q_norm_g, k_norm_g, attn_sinks, w_out, norm_ffn_g, router_group_w, router_group_b, router_expert_w, router_expert_b, w_gate, w_up, w_down):
    n_p, t_p, d = x_prompt.shape
    n_s, t_s, _ = x_sample.shape
    depth = w_in.shape[0]
    lw_s = cache_k_win.shape[2]
    assert t_s == 1 and lw_s == WINDOW and d == D_MODEL
    assert t_p % TM_PROJ == 0 and t_p >= WINDOW
    past_len = 16384

    w_in_b = w_in.astype(BF16)
    w_out_b = w_out.astype(BF16)
    g_attn = norm_attn_g.reshape(depth, 1, D_MODEL)
    g_ffn = norm_ffn_g.reshape(depth, 1, D_MODEL)
    qg = (jnp.tile(q_norm_g, (1, N_HEADS)) * ATTN_SCALE).reshape(depth, 1, Q_W)
    kg = jnp.tile(k_norm_g, (1, N_KV_HEADS)).reshape(depth, 1, KV_W)
    seg = jnp.arange(256) // HEAD_DIM
    bd = jnp.where(seg[:, None] == seg[None, :], 1.0 / HEAD_DIM, 0.0).astype(BF16)
    wp = jnp.zeros((depth, 2, 256, 256), F32)
    for p in range(2):
        wp = wp.at[:, p, :POOL_GC, :POOL_GC].set(pool_w[:, 2 * p])
        wp = wp.at[:, p, POOL_GC:, POOL_GC:].set(pool_w[:, 2 * p + 1])
    wp = wp.astype(BF16)
    ps = pool_scale.reshape(depth, 1, POOL_W)
    wr = jnp.zeros((depth, D_MODEL, LANES), F32)
    wr = wr.at[:, :, :N_EXPERTS].set(router_expert_w)
    wr = wr.at[:, :, GROUP_LANE0:GROUP_LANE0 + N_EXPERT_GROUPS].set(router_group_w)
    wr = wr.astype(BF16)
    br = jnp.zeros((depth, 1, LANES), F32)
    br = br.at[:, 0, :N_EXPERTS].set(router_expert_b)
    br = br.at[:, 0, GROUP_LANE0:GROUP_LANE0 + N_EXPERT_GROUPS].set(router_group_b)

    slopes = jnp.exp2(-8.0 * jnp.arange(1, N_HEADS + 1, dtype=F32) / N_HEADS)
    bias_p = _prompt_bias()
    dist_s = (WINDOW - 1) - jnp.arange(WINDOW, dtype=F32)
    bias_s = -slopes[:, None] * dist_s[None, :]
    state2d = state_pool.reshape(depth, n_s, POOL_STATE * POOL_W)
    ck_all = cache_k_win.reshape(depth, n_s, lw_s, KV_W)
    cv_all = cache_v_win.reshape(depth, n_s, lw_s, KV_W)

    xp = x_prompt.reshape(n_p * t_p, D_MODEL)
    xs = x_sample.reshape(n_s, D_MODEL)
    lw_p = min(WINDOW, t_p)
    pool_p, kp_new, vp_new, pool_s, ks_new, vs_new = [], [], [], [], [], []
    zero_cnt = jnp.zeros((1, LANES), F32)
    for l in range(depth):
        sinks = attn_sinks[l]
        pool_o, q, k, v, utail = _proj_pool_prompt(l, xp, n_p, t_p, g_attn, w_in_b, qg, kg, bd, wp, ps)
        attn_o = _attn_prompt(q, k, v, bias_p, sinks, n_p, t_p)
        x1p, h2p, route_p, route_tp, cnt_p = _merge_router(
            l, pool_o, attn_o, xp, w_out_b, g_ffn, wr, br, zero_cnt, TM_PROJ)
        pool_p.append(utail[:, 16 - POOL_STATE:, :])
        kp_new.append(k.reshape(n_p, t_p, N_KV_HEADS, HEAD_DIM)[:, t_p - lw_p:])
        vp_new.append(v.reshape(n_p, t_p, N_KV_HEADS, HEAD_DIM)[:, t_p - lw_p:])
        pool_so, attn_so, u_s, kc_s, vc_s = _sample_mixer(
            l, xs, g_attn, w_in_b, qg, kg, bd, wp, ps, state2d, ck_all, cv_all,
            sinks.reshape(N_HEADS, 1), bias_s, past_len)
        x1s, h2s, route_s, route_ts, cnt_all = _merge_router(
            l, pool_so, attn_so, xs, w_out_b, g_ffn, wr, br, cnt_p, n_s)
        pool_s.append(jnp.concatenate([state_pool[l][:, 1:], u_s[:, None, :]], axis=1))
        ks_new.append(kc_s)
        vs_new.append(vc_s)
        counts = cnt_all[0, :N_EXPERTS].astype(jnp.int32)
        xp, xs = _hier_moe(l, h2p, h2s, route_p, route_s, route_tp, route_ts, counts, x1p, x1s,
                           w_gate, w_up, w_down)
    return (xp.reshape(n_p, t_p, D_MODEL), xs.reshape(n_s, t_s, D_MODEL),
            jnp.stack(pool_p), jnp.stack(kp_new), jnp.stack(vp_new),
            jnp.stack(pool_s),
            jnp.stack(ks_new).reshape(depth, n_s, lw_s, N_KV_HEADS, HEAD_DIM),
            jnp.stack(vs_new).reshape(depth, n_s, lw_s, N_KV_HEADS, HEAD_DIM))
```

```python
import functools

import jax
import jax.numpy as jnp
from jax import lax
from jax.experimental import pallas as pl
from jax.experimental.pallas import tpu as pltpu
from jax.experimental.pallas import tpu_sc as plsc

D_MODEL = 1024
POOL_W = 512
POOL_WINDOWS = (2, 4, 8, 16)
POOL_GC = 128
POOL_STATE = 15
HEAD_DIM = 64
N_HEADS = 8
N_KV_HEADS = 2
GQA_GROUP = 4
Q_W = 512
KV_W = 128
D_IN = POOL_W + Q_W + 2 * KV_W
WINDOW = 128
ATTN_SCALE = HEAD_DIM ** -0.5
N_EXPERT_GROUPS = 4
EXPERTS_PER_GROUP = 8
N_EXPERTS = 32
EXPERT_FF = 512
EPS = 1e-6

LANES = 128
HALO = 32
TM_PROJ = 512
ATTN_QB = 4
MOE_BM = 256
GROUP_LANE0 = 32
ROUTE_FIELDS = 8
SC_CORES = 2
SC_SUBCORES = 16
SC_WORKERS = SC_CORES * SC_SUBCORES
DISP_CH = 64
COMB_CH = 64
SAMPLE_CH = 32
VMEM_LIMIT = 48 * 1024 * 1024

BF16 = jnp.bfloat16
F32 = jnp.float32


def _pack_bf16_pairs(h):
    w = h.shape[1] // 2
    hi = lax.bitcast_convert_type(h[:, :w].astype(F32), jnp.uint32)
    lo = lax.bitcast_convert_type(h[:, w:].astype(F32), jnp.uint32)
    return lax.bitcast_convert_type(hi | (lo >> 16), jnp.int32)


def _unpack_bf16_pairs(words):
    u = lax.bitcast_convert_type(words, jnp.uint32)
    hi = lax.bitcast_convert_type(u & jnp.uint32(0xFFFF0000), F32)
    lo = lax.bitcast_convert_type(u << 16, F32)
    return jnp.concatenate([hi, lo], axis=-1)


def _segment_mean_sq(a, bd):
    w = a.shape[1]
    return jnp.dot((a * a).astype(BF16), bd[:w, :w], preferred_element_type=F32)


def _project(x, g, w_in, qg, kg, bd):
    ms = jnp.mean(x * x, axis=-1, keepdims=True)
    h = (x * lax.rsqrt(ms + EPS) * g).astype(BF16)
    z = jnp.dot(h, w_in, preferred_element_type=F32)
    u = z[:, :POOL_W]
    q = z[:, POOL_W:POOL_W + Q_W]
    k = z[:, POOL_W + Q_W:POOL_W + Q_W + KV_W]
    v = z[:, POOL_W + Q_W + KV_W:]
    qn = []
    for c in range(Q_W // 256):
        qc = q[:, c * 256:(c + 1) * 256]
        qn.append(qc * lax.rsqrt(_segment_mean_sq(qc, bd) + EPS))
    qn = jnp.concatenate(qn, axis=-1) * qg
    kn = k * lax.rsqrt(_segment_mean_sq(k, bd) + EPS) * kg
    return u, qn, kn, v


def _pool_project(d_groups, wp_ref, ps):
    outs = []
    for p in range(2):
        dp = jnp.concatenate([d_groups[2 * p], d_groups[2 * p + 1]], axis=-1).astype(BF16)
        y = jnp.dot(dp, wp_ref[p], preferred_element_type=F32)
        outs.append(y * ps[:, p * 256:(p + 1) * 256])
    return jnp.concatenate(outs, axis=-1)


def _proj_pool_kernel(x_ref, g_ref, win_ref, qg_ref, kg_ref, bd_ref, wp_ref, ps_ref,
                      pool_ref, q_ref, k_ref, vt_ref, utail_ref, ktail_ref, vtail_ref,
                      ext_ref, sa_ref, sb_ref, *, tm, n_j):
    j = pl.program_id(1)
    u, qn, kn, v = _project(x_ref[...], g_ref[...], win_ref[...], qg_ref[...], kg_ref[...], bd_ref[...])
    q_ref[...] = qn.astype(BF16)
    k_ref[...] = kn.astype(BF16)
    vt_ref[...] = jnp.transpose(v).astype(BF16)

    @pl.when(j == 0)
    def _():
        ext_ref[0:HALO, :] = jnp.zeros((HALO, POOL_W), F32)

    r = tm + HALO
    ext_ref[HALO:r, :] = u
    sa_ref[8:r, :] = ext_ref[8:r, :] + ext_ref[7:r - 1, :]
    sb_ref[16:r, 128:] = sa_ref[16:r, 128:] + sa_ref[14:r - 2, 128:]
    sa_ref[24:r, 256:] = sb_ref[24:r, 256:] + sb_ref[20:r - 4, 256:]
    sb_ref[32:r, 384:] = sa_ref[32:r, 384:] + sa_ref[24:r - 8, 384:]
    pos1 = j * tm + lax.broadcasted_iota(jnp.int32, (tm, POOL_GC), 0) + 1
    sums = (sa_ref, sb_ref, sa_ref, sb_ref)
    d_groups = []
    for gi, w in enumerate(POOL_WINDOWS):
        sl = slice(gi * POOL_GC, (gi + 1) * POOL_GC)
        cnt = jnp.minimum(pos1, w).astype(F32)
        d_groups.append(sums[gi][HALO:r, sl] / cnt - u[:, sl])
    pool_ref[...] = _pool_project(d_groups, wp_ref, ps_ref[...]).astype(BF16)
    ext_ref[16:HALO, :] = ext_ref[tm + 16:r, :]

    @pl.when(j == n_j - 1)
    def _():
        utail_ref[...] = u[tm - 16:, :]
        ktail_ref[...] = kn[tm - WINDOW:, :]
        vtail_ref[...] = v[tm - WINDOW:, :]


def _proj_pool_prompt(l, x2d, n_seq, seq, g_attn, w_in, qg, kg, bd, wp, ps):
    tm = TM_PROJ
    n_j = seq // tm
    t = n_seq * seq
    row = lambda b, j: (b * n_j + j, 0)
    lay = lambda b, j: (l, 0, 0)
    return pl.pallas_call(
        functools.partial(_proj_pool_kernel, tm=tm, n_j=n_j),
        grid=(n_seq, n_j),
        in_specs=[
            pl.BlockSpec((tm, D_MODEL), row),
            pl.BlockSpec((None, 1, D_MODEL), lay),
            pl.BlockSpec((None, D_MODEL, D_IN), lay),
            pl.BlockSpec((None, 1, Q_W), lay),
            pl.BlockSpec((None, 1, KV_W), lay),
            pl.BlockSpec((256, 256), lambda b, j: (0, 0)),
            pl.BlockSpec((None, 2, 256, 256), lambda b, j: (l, 0, 0, 0)),
            pl.BlockSpec((None, 1, POOL_W), lay),
        ],
        out_specs=[
            pl.BlockSpec((tm, POOL_W), row),
            pl.BlockSpec((tm, Q_W), row),
            pl.BlockSpec((tm, KV_W), row),
            pl.BlockSpec((KV_W, tm), lambda b, j: (0, b * n_j + j)),
            pl.BlockSpec((None, 16, POOL_W), lambda b, j: (b, 0, 0)),
            pl.BlockSpec((None, WINDOW, KV_W), lambda b, j: (b, 0, 0)),
            pl.BlockSpec((None, WINDOW, KV_W), lambda b, j: (b, 0, 0)),
        ],
        out_shape=[
            jax.ShapeDtypeStruct((t, POOL_W), BF16),
            jax.ShapeDtypeStruct((t, Q_W), BF16),
            jax.ShapeDtypeStruct((t, KV_W), BF16),
            jax.ShapeDtypeStruct((KV_W, t), BF16),
            jax.ShapeDtypeStruct((n_seq, 16, POOL_W), F32),
            jax.ShapeDtypeStruct((n_seq, WINDOW, KV_W), F32),
            jax.ShapeDtypeStruct((n_seq, WINDOW, KV_W), F32),
        ],
        scratch_shapes=[pltpu.VMEM((tm + HALO, POOL_W), F32)] * 3,
        compiler_params=pltpu.CompilerParams(
            dimension_semantics=("arbitrary", "arbitrary"), vmem_limit_bytes=VMEM_LIMIT),
        name="proj_pool_prompt",
    )(x2d, g_attn, w_in, qg, kg, bd, wp, ps)


def _attn_kernel(sink_ref, q_ref, kp_ref, kc_ref, vtp_ref, vtc_ref, bias_ref, o_ref):
    j = pl.program_id(1)
    kk_all = jnp.concatenate([kp_ref[...], kc_ref[...]], axis=0)
    vt_all = jnp.concatenate([vtp_ref[...], vtc_ref[...]], axis=1)
    for blk in range(ATTN_QB):
        q = q_ref[blk * WINDOW:(blk + 1) * WINDOW, :]
        kk = kk_all[blk * WINDOW:(blk + 2) * WINDOW, :]
        vt = vt_all[:, blk * WINDOW:(blk + 2) * WINDOW]
        variant = jnp.minimum(j, 1) if blk == 0 else 1
        outs = []
        for kv in range(N_KV_HEADS):
            heads = range(kv * GQA_GROUP, (kv + 1) * GQA_GROUP)
            q_rows = jnp.concatenate([q[:, h * HEAD_DIM:(h + 1) * HEAD_DIM] for h in heads], axis=0)
            s_all = lax.dot_general(kk[:, kv * HEAD_DIM:(kv + 1) * HEAD_DIM], q_rows,
                                    (((1,), (1,)), ((), ())), preferred_element_type=F32)
            vt_kv = vt[kv * HEAD_DIM:(kv + 1) * HEAD_DIM, :]
            for g, h in enumerate(heads):
                s = s_all[:, g * WINDOW:(g + 1) * WINDOW] + bias_ref[variant, h]
                sink = sink_ref[h]
                m = jnp.maximum(jnp.max(s, axis=0, keepdims=True), sink)
                p = jnp.exp(s - m)
                denom = jnp.sum(p, axis=0, keepdims=True) + jnp.exp(sink - m)
                o_t = jnp.dot(vt_kv, p.astype(BF16), preferred_element_type=F32)
                outs.append(o_t / denom)
        o_ref[blk * WINDOW:(blk + 1) * WINDOW, :] = jnp.transpose(jnp.concatenate(outs, axis=0)).astype(BF16)


def _attn_prompt(q, k, vt, bias_t, sinks, n_seq, seq):
    tq = ATTN_QB * WINDOW
    nj = seq // tq
    t = n_seq * seq
    cur = lambda b, j: (b * nj + j, 0)
    prev = lambda b, j: (jnp.maximum((b * nj + j) * ATTN_QB - 1, 0), 0)
    cur_t = lambda b, j: (0, b * nj + j)
    prev_t = lambda b, j: (0, jnp.maximum((b * nj + j) * ATTN_QB - 1, 0))
    return pl.pallas_call(
        _attn_kernel,
        grid=(n_seq, nj),
        in_specs=[
            pl.BlockSpec(memory_space=pltpu.SMEM),
            pl.BlockSpec((tq, Q_W), cur),
            pl.BlockSpec((WINDOW, KV_W), prev),
            pl.BlockSpec((tq, KV_W), cur),
            pl.BlockSpec((KV_W, WINDOW), prev_t),
            pl.BlockSpec((KV_W, tq), cur_t),
            pl.BlockSpec((2, N_HEADS, 2 * WINDOW, WINDOW), lambda b, j: (0, 0, 0, 0)),
        ],
        out_specs=pl.BlockSpec((tq, Q_W), cur),
        out_shape=jax.ShapeDtypeStruct((t, Q_W), BF16),
        compiler_params=pltpu.CompilerParams(
            dimension_semantics=("arbitrary", "arbitrary"), vmem_limit_bytes=VMEM_LIMIT),
        name="attn_prompt",
    )(sinks, q, k, k, vt, vt, bias_t)


def _prompt_bias_t():
    r = jnp.arange(WINDOW, dtype=jnp.int32)[None, :]
    c = jnp.arange(2 * WINDOW, dtype=jnp.int32)[:, None]
    dist = WINDOW + r - c
    valid = (dist >= 0) & (dist < WINDOW)
    slopes = jnp.exp2(-8.0 * jnp.arange(1, N_HEADS + 1, dtype=F32) / N_HEADS)
    pen = -slopes[:, None, None] * dist.astype(F32)[None]
    later = jnp.where(valid[None], pen, -jnp.inf)
    first = jnp.where((valid & (c >= WINDOW))[None], pen, -jnp.inf)
    return jnp.stack([first, later])


def _sample_kernel(x_ref, g_ref, win_ref, qg_ref, kg_ref, bd_ref, wp_ref, ps_ref,
                   st_ref, ck_ref, cv_ref, sink_ref, bias_ref, perm_ref,
                   pool_ref, attn_ref, u_ref, kc_ref, vc_ref, *, ns, pos0):
    u, qn, kn, v = _project(x_ref[...], g_ref[...], win_ref[...], qg_ref[...], kg_ref[...], bd_ref[...])
    u_ref[...] = u
    kc_ref[:, 0:WINDOW - 1, :] = ck_ref[:, 1:WINDOW, :]
    vc_ref[:, 0:WINDOW - 1, :] = cv_ref[:, 1:WINDOW, :]
    for n in range(ns):
        kc_ref[n, WINDOW - 1:WINDOW, :] = kn[n:n + 1, :]
        vc_ref[n, WINDOW - 1:WINDOW, :] = v[n:n + 1, :]

    d_groups = []
    for gi, w in enumerate(POOL_WINDOWS):
        lo = gi * POOL_GC
        acc = u[:, lo:lo + POOL_GC]
        for back in range(1, w):
            off = (POOL_STATE - back) * POOL_W + lo
            acc = acc + st_ref[:, off:off + POOL_GC]
        d_groups.append(acc / float(min(pos0 + 1, w)) - u[:, lo:lo + POOL_GC])
    pool_ref[...] = _pool_project(d_groups, wp_ref, ps_ref[...]).astype(BF16)

    zeros = jnp.zeros((ns, HEAD_DIM), F32)
    stacked = []
    for h in range(N_HEADS):
        piece = qn[:, h * HEAD_DIM:(h + 1) * HEAD_DIM]
        pair = [piece, zeros] if h < GQA_GROUP else [zeros, piece]
        stacked.append(jnp.concatenate(pair, axis=-1))
    q_hn = jnp.concatenate(stacked, axis=0).astype(BF16)
    q_nh = jnp.dot(perm_ref[0], q_hn, preferred_element_type=F32).astype(BF16)

    keys = kc_ref[...].reshape(ns * WINDOW, KV_W).astype(BF16)
    vals = vc_ref[...].reshape(ns * WINDOW, KV_W).astype(BF16)
    s_all = lax.dot_general(q_nh, keys, (((1,), (1,)), ((), ())), preferred_element_type=F32)
    sink = sink_ref[...]
    bias = bias_ref[...]
    zero_blk = jnp.zeros((N_HEADS, WINDOW), F32)
    p_rows = []
    for n in range(ns):
        s = s_all[n * N_HEADS:(n + 1) * N_HEADS, n * WINDOW:(n + 1) * WINDOW] + bias
        m = jnp.maximum(jnp.max(s, axis=-1, keepdims=True), sink)
        p = jnp.exp(s - m)
        denom = jnp.sum(p, axis=-1, keepdims=True) + jnp.exp(sink - m)
        p_rows.append(jnp.concatenate([zero_blk] * n + [p / denom] + [zero_blk] * (ns - 1 - n), axis=-1))
    p_blockdiag = jnp.concatenate(p_rows, axis=0).astype(BF16)
    o_nh = jnp.dot(p_blockdiag, vals, preferred_element_type=F32).astype(BF16)
    o_hn = jnp.dot(perm_ref[1], o_nh, preferred_element_type=F32)
    pieces = []
    for h in range(N_HEADS):
        kv = h // GQA_GROUP
        pieces.append(o_hn[h * ns:(h + 1) * ns, kv * HEAD_DIM:(kv + 1) * HEAD_DIM])
    attn_ref[...] = jnp.concatenate(pieces, axis=-1).astype(BF16)


def _sample_mixer(l, xs, g_attn, w_in, qg, kg, bd, wp, ps, state2d, ck, cv, sink8, bias_s, pos0):
    n = xs.shape[0]
    ns = 32
    row = lambda i: (i, 0)
    lay = lambda i: (l, 0, 0)
    src = jnp.arange(ns * N_HEADS)
    perm = (((src % N_HEADS) * ns + src // N_HEADS)[:, None] == src[None, :]).astype(BF16)
    perms = jnp.stack([perm, perm.T])
    return pl.pallas_call(
        functools.partial(_sample_kernel, ns=ns, pos0=pos0),
        grid=(n // ns,),
        in_specs=[
            pl.BlockSpec((ns, D_MODEL), row),
            pl.BlockSpec((None, 1, D_MODEL), lay),
            pl.BlockSpec((None, D_MODEL, D_IN), lay),
            pl.BlockSpec((None, 1, Q_W), lay),
            pl.BlockSpec((None, 1, KV_W), lay),
            pl.BlockSpec((256, 256), lambda i: (0, 0)),
            pl.BlockSpec((None, 2, 256, 256), lambda i: (l, 0, 0, 0)),
            pl.BlockSpec((None, 1, POOL_W), lay),
            pl.BlockSpec((None, ns, POOL_STATE * POOL_W), lambda i: (l, i, 0)),
            pl.BlockSpec((None, ns, WINDOW, KV_W), lambda i: (l, i, 0, 0)),
            pl.BlockSpec((None, ns, WINDOW, KV_W), lambda i: (l, i, 0, 0)),
            pl.BlockSpec((N_HEADS, 1), lambda i: (0, 0)),
            pl.BlockSpec((N_HEADS, WINDOW), lambda i: (0, 0)),
            pl.BlockSpec((2, ns * N_HEADS, ns * N_HEADS), lambda i: (0, 0, 0)),
        ],
        out_specs=[
            pl.BlockSpec((ns, POOL_W), row),
            pl.BlockSpec((ns, Q_W), row),
            pl.BlockSpec((ns, POOL_W), row),
            pl.BlockSpec((ns, WINDOW, KV_W), lambda i: (i, 0, 0)),
            pl.BlockSpec((ns, WINDOW, KV_W), lambda i: (i, 0, 0)),
        ],
        out_shape=[
            jax.ShapeDtypeStruct((n, POOL_W), BF16),
            jax.ShapeDtypeStruct((n, Q_W), BF16),
            jax.ShapeDtypeStruct((n, POOL_W), F32),
            jax.ShapeDtypeStruct((n, WINDOW, KV_W), F32),
            jax.ShapeDtypeStruct((n, WINDOW, KV_W), F32),
        ],
        compiler_params=pltpu.CompilerParams(
            dimension_semantics=("arbitrary",), vmem_limit_bytes=VMEM_LIMIT),
        name="sample_mixer",
    )(xs, g_attn, w_in, qg, kg, bd, wp, ps, state2d, ck, cv, sink8, bias_s, perms)


def _merge_router_kernel(pool_ref, attn_ref, x_ref, wout_ref, g_ref, wr_ref, br_ref, ltri_ref, cin_ref,
                         x1_ref, h2_ref, route_ref, route_t_ref, cnt_ref):
    i = pl.program_id(0)

    @pl.when(i == 0)
    def _():
        cnt_ref[...] = cin_ref[...]

    y = jnp.dot(pool_ref[...], wout_ref[0:POOL_W, :], preferred_element_type=F32)
    y = y + jnp.dot(attn_ref[...], wout_ref[POOL_W:, :], preferred_element_type=F32)
    x1 = x_ref[...] + y
    x1_ref[...] = x1
    ms = jnp.mean(x1 * x1, axis=-1, keepdims=True)
    h2 = (x1 * lax.rsqrt(ms + EPS) * g_ref[...]).astype(BF16)
    h2_ref[...] = _pack_bf16_pairs(h2)
    logits = jnp.dot(h2, wr_ref[...], preferred_element_type=F32) + br_ref[...]

    tm = logits.shape[0]
    lane = lax.broadcasted_iota(jnp.int32, (tm, LANES), 1)
    big = jnp.int32(2 * LANES)
    neg = -jnp.inf
    is_group = (lane >= GROUP_LANE0) & (lane < GROUP_LANE0 + N_EXPERT_GROUPS)
    gl = jnp.where(is_group, logits, neg)
    gmax = jnp.max(gl, axis=-1, keepdims=True)
    grp = jnp.min(jnp.where(gl == gmax, lane, big), axis=-1, keepdims=True) - GROUP_LANE0
    g_w = 1.0 / jnp.sum(jnp.exp(gl - gmax), axis=-1, keepdims=True)
    in_grp = (lane >= grp * EXPERTS_PER_GROUP) & (lane < (grp + 1) * EXPERTS_PER_GROUP)
    el = jnp.where(in_grp, logits, neg)
    v1 = jnp.max(el, axis=-1, keepdims=True)
    i1 = jnp.min(jnp.where(el == v1, lane, big), axis=-1, keepdims=True)
    el2 = jnp.where(lane == i1, neg, el)
    v2 = jnp.max(el2, axis=-1, keepdims=True)
    i2 = jnp.min(jnp.where(el2 == v2, lane, big), axis=-1, keepdims=True)
    e21 = jnp.exp(v2 - v1)
    w1 = g_w / (1.0 + e21)
    w2 = g_w * e21 / (1.0 + e21)

    oh1 = lane == i1
    oh2 = lane == i2
    c = jnp.where(oh1 | oh2, 1.0, 0.0)
    prefix = jnp.dot(ltri_ref[...], c.astype(BF16), preferred_element_type=F32) + cnt_ref[...]
    r1 = jnp.sum(jnp.where(oh1, prefix, 0.0), axis=-1, keepdims=True)
    r2 = jnp.sum(jnp.where(oh2, prefix, 0.0), axis=-1, keepdims=True)
    cnt_ref[...] = cnt_ref[...] + jnp.sum(c, axis=0, keepdims=True)

    out = jnp.zeros((tm, LANES), F32)
    for idx, val in enumerate((i1.astype(F32), i2.astype(F32), w1, w2, r1, r2)):
        out = jnp.where(lane == idx, val, out)
    route_ref[...] = out
    route_t_ref[...] = jnp.transpose(out)[0:ROUTE_FIELDS, :]


def _merge_router(l, pool, attn, x2d, w_out, g_ffn, wr, br, cnt_in, tm):
    t = x2d.shape[0]
    ltri = (jnp.arange(tm)[:, None] > jnp.arange(tm)[None, :]).astype(BF16)
    row = lambda i: (i, 0)
    lay = lambda i: (l, 0, 0)
    return pl.pallas_call(
        _merge_router_kernel,
        grid=(t // tm,),
        in_specs=[
            pl.BlockSpec((tm, POOL_W), row),
            pl.BlockSpec((tm, Q_W), row),
            pl.BlockSpec((tm, D_MODEL), row),
            pl.BlockSpec((None, D_MODEL, D_MODEL), lay),
            pl.BlockSpec((None, 1, D_MODEL), lay),
            pl.BlockSpec((None, D_MODEL, LANES), lay),
            pl.BlockSpec((None, 1, LANES), lay),
            pl.BlockSpec((tm, tm), lambda i: (0, 0)),
            pl.BlockSpec((1, LANES), lambda i: (0, 0)),
        ],
        out_specs=[
            pl.BlockSpec((tm, D_MODEL), row),
            pl.BlockSpec((tm, D_MODEL // 2), row),
            pl.BlockSpec((tm, LANES), row),
            pl.BlockSpec((ROUTE_FIELDS, tm), lambda i: (0, i)),
            pl.BlockSpec((1, LANES), lambda i: (0, 0)),
        ],
        out_shape=[
            jax.ShapeDtypeStruct((t, D_MODEL), F32),
            jax.ShapeDtypeStruct((t, D_MODEL // 2), jnp.int32),
            jax.ShapeDtypeStruct((t, LANES), F32),
            jax.ShapeDtypeStruct((ROUTE_FIELDS, t), F32),
            jax.ShapeDtypeStruct((1, LANES), F32),
        ],
        compiler_params=pltpu.CompilerParams(
            dimension_semantics=("arbitrary",), vmem_limit_bytes=VMEM_LIMIT),
        name="merge_router",
    )(pool, attn, x2d, w_out, g_ffn, wr, br, ltri, cnt_in)


def _moe_kernel(be_ref, rv_ref, nx_ref, sl_ref, xd_ref, wg_hbm, wu_hbm, wd_hbm, yd_ref,
                wg_f, wu_f, wd_f, wg_s, wu_s, wd_s, sem, *, layer):
    i = pl.program_id(0)
    expert = be_ref[i]
    changed = (i == 0) | (expert != be_ref[jnp.maximum(i - 1, 0)])
    rows_valid = rv_ref[i]
    slot = sl_ref[i]

    def weight_copies(e, s):
        return [pltpu.make_async_copy(w_hbm.at[layer, e], w_f.at[s], sem.at[s, n])
                for n, (w_hbm, w_f) in enumerate(((wg_hbm, wg_f), (wu_hbm, wu_f), (wd_hbm, wd_f)))]

    @pl.when(i == 0)
    def _():
        for c in weight_copies(expert, slot):
            c.start()

    @pl.when(changed)
    def _():
        for c in weight_copies(expert, slot):
            c.wait()

        @pl.when(nx_ref[i] >= 0)
        def _():
            for c in weight_copies(nx_ref[i], 1 - slot):
                c.start()

        wg_s[...] = wg_f[slot].astype(BF16)
        wu_s[...] = wu_f[slot].astype(BF16)
        wd_s[...] = wd_f[slot].astype(BF16)

    @pl.when(rows_valid > 0)
    def _():
        row = lax.broadcasted_iota(jnp.int32, (MOE_BM, D_MODEL), 0)
        x = jnp.where(row < rows_valid, _unpack_bf16_pairs(xd_ref[...]), 0.0).astype(BF16)
        gate = jnp.dot(x, wg_s[...], preferred_element_type=F32)
        up = jnp.dot(x, wu_s[...], preferred_element_type=F32)
        act = (gate * jax.nn.sigmoid(gate) * up).astype(BF16)
        y = jnp.dot(act, wd_s[...], preferred_element_type=F32)
        yd_ref[...] = _pack_bf16_pairs(y.astype(BF16))

    @pl.when(rows_valid <= 0)
    def _():
        yd_ref[...] = jnp.zeros(yd_ref.shape, jnp.int32)


def _moe_experts(l, block_e, rows_valid, next_e, slot, xd, w_gate, w_up, w_down):
    n_blocks = xd.shape[0] // MOE_BM
    row = lambda i, *_: (i, 0)
    return pl.pallas_call(
        functools.partial(_moe_kernel, layer=l),
        grid_spec=pltpu.PrefetchScalarGridSpec(
            num_scalar_prefetch=4,
            grid=(n_blocks,),
            in_specs=[
                pl.BlockSpec((MOE_BM, D_MODEL // 2), row),
                pl.BlockSpec(memory_space=pl.ANY),
                pl.BlockSpec(memory_space=pl.ANY),
                pl.BlockSpec(memory_space=pl.ANY),
            ],
            out_specs=pl.BlockSpec((MOE_BM, D_MODEL // 2), row),
            scratch_shapes=[
                pltpu.VMEM((2, D_MODEL, EXPERT_FF), F32),
                pltpu.VMEM((2, D_MODEL, EXPERT_FF), F32),
                pltpu.VMEM((2, EXPERT_FF, D_MODEL), F32),
                pltpu.VMEM((D_MODEL, EXPERT_FF), BF16),
                pltpu.VMEM((D_MODEL, EXPERT_FF), BF16),
                pltpu.VMEM((EXPERT_FF, D_MODEL), BF16),
                pltpu.SemaphoreType.DMA((2, 3)),
            ],
        ),
        out_shape=jax.ShapeDtypeStruct((n_blocks * MOE_BM, D_MODEL // 2), jnp.int32),
        compiler_params=pltpu.CompilerParams(
            dimension_semantics=("arbitrary",), vmem_limit_bytes=VMEM_LIMIT),
        name="moe_experts",
    )(block_e, rows_valid, next_e, slot, xd, w_gate, w_up, w_down)


def _sc_worker_id():
    return lax.axis_index("s") * SC_CORES + lax.axis_index("c")


def _sc_dispatch(hp, hs, dest_p, dest_s, n_rows):
    tp, width = hp.shape
    per_w = tp // SC_WORKERS
    n_ch = per_w // DISP_CH
    n_sw = hs.shape[0] // SAMPLE_CH
    mesh = plsc.VectorSubcoreMesh(core_axis_name="c", subcore_axis_name="s")

    @functools.partial(
        pl.kernel, mesh=mesh,
        out_type=jax.ShapeDtypeStruct((n_rows, width), jnp.int32),
        scratch_types=[
            pltpu.VMEM((2, n_ch, DISP_CH), jnp.int32),
            pltpu.VMEM((2, 1, SAMPLE_CH), jnp.int32),
            pltpu.VMEM((2, DISP_CH, width), jnp.int32),
            pltpu.SemaphoreType.DMA((2,)),
            pltpu.SemaphoreType.DMA((2,)),
        ],
        name="sc_dispatch",
    )
    def k(hp_hbm, hs_hbm, dp_hbm, ds_hbm, xd_hbm, idx_v, idxs_v, bufs, rsem, wsem):
        wid = _sc_worker_id()
        base = wid * per_w
        for kk in range(2):
            pltpu.sync_copy(dp_hbm.at[kk, wid], idx_v.at[kk])
        reads = [pltpu.make_async_copy(hp_hbm.at[pl.ds(base + j * DISP_CH, DISP_CH)],
                                       bufs.at[j % 2], rsem.at[j % 2]) for j in range(n_ch)]
        reads[0].start()
        for j in range(n_ch):
            if j + 1 < n_ch:
                reads[j + 1].start()
            reads[j].wait()
            writes = [pltpu.make_async_copy(bufs.at[j % 2], xd_hbm.at[idx_v.at[kk, j]], wsem.at[kk])
                      for kk in range(2)]
            for w in writes:
                w.start()
            for w in writes:
                w.wait()

        @pl.when(wid < n_sw)
        def _():
            rows = bufs.at[0, pl.ds(0, SAMPLE_CH)]
            for kk in range(2):
                pltpu.sync_copy(ds_hbm.at[kk, wid], idxs_v.at[kk])
            pltpu.sync_copy(hs_hbm.at[pl.ds(wid * SAMPLE_CH, SAMPLE_CH)], rows)
            for kk in range(2):
                pltpu.sync_copy(rows, xd_hbm.at[idxs_v.at[kk, 0]])

    return k(hp, hs, dest_p, dest_s)


def _sc_combine_gather(yd, dest_p, dest_s, tp, ts):
    width = yd.shape[1]
    per_w = tp // SC_WORKERS
    n_ch = per_w // COMB_CH
    n_sw = ts // SAMPLE_CH
    mesh = plsc.VectorSubcoreMesh(core_axis_name="c", subcore_axis_name="s")

    @functools.partial(
        pl.kernel, mesh=mesh,
        out_type=jax.ShapeDtypeStruct((2, tp + ts, width), yd.dtype),
        scratch_types=[
            pltpu.VMEM((2, n_ch, COMB_CH), jnp.int32),
            pltpu.VMEM((2, 1, SAMPLE_CH), jnp.int32),
            pltpu.VMEM((2, COMB_CH, width), yd.dtype),
            pltpu.SemaphoreType.DMA((2,)),
            pltpu.SemaphoreType.DMA((2,)),
        ],
        name="sc_combine_gather",
    )
    def k(yd_hbm, dp_hbm, ds_hbm, g_hbm, idx_v, idxs_v, bufs, gsem, wsem):
        wid = _sc_worker_id()
        base = wid * per_w
        for kk in range(2):
            pltpu.sync_copy(dp_hbm.at[kk, wid], idx_v.at[kk])
        items = [(kk, j) for kk in range(2) for j in range(n_ch)]
        gathers = [pltpu.make_async_copy(yd_hbm.at[idx_v.at[kk, j]], bufs.at[n % 2], gsem.at[n % 2])
                   for n, (kk, j) in enumerate(items)]
        gathers[0].start()
        for n, (kk, j) in enumerate(items):
            if n + 1 < len(items):
                gathers[n + 1].start()
            gathers[n].wait()
            w = pltpu.make_async_copy(bufs.at[n % 2], g_hbm.at[kk, pl.ds(base + j * COMB_CH, COMB_CH)],
                                      wsem.at[n % 2])
            w.start()
            w.wait()

        @pl.when(wid < n_sw)
        def _():
            for kk in range(2):
                pltpu.sync_copy(ds_hbm.at[kk, wid], idxs_v.at[kk])
            for kk in range(2):
                rows = bufs.at[kk, pl.ds(0, SAMPLE_CH)]
                pltpu.sync_copy(yd_hbm.at[idxs_v.at[kk, 0]], rows)
                pltpu.sync_copy(rows, g_hbm.at[kk, pl.ds(tp + wid * SAMPLE_CH, SAMPLE_CH)])

    return k(yd, dest_p, dest_s)


def _combine_kernel(x1_ref, g_ref, route_ref, x2_ref):
    w1 = route_ref[:, 2:3]
    w2 = route_ref[:, 3:4]
    x2_ref[...] = x1_ref[...] + _unpack_bf16_pairs(g_ref[0]) * w1 + _unpack_bf16_pairs(g_ref[1]) * w2


def _combine(x1, g, route, row0, tm):
    t = x1.shape[0]
    blk0 = row0 // tm
    row = lambda i: (i, 0)
    return pl.pallas_call(
        _combine_kernel,
        grid=(t // tm,),
        in_specs=[
            pl.BlockSpec((tm, D_MODEL), row),
            pl.BlockSpec((2, tm, D_MODEL // 2), lambda i: (0, blk0 + i, 0)),
            pl.BlockSpec((tm, LANES), row),
        ],
        out_specs=pl.BlockSpec((tm, D_MODEL), row),
        out_shape=jax.ShapeDtypeStruct((t, D_MODEL), F32),
        compiler_params=pltpu.CompilerParams(
            dimension_semantics=("arbitrary",), vmem_limit_bytes=VMEM_LIMIT),
        name="combine",
    )(x1, g, route)


def _dest_layout(dest, workers, chunk):
    t = dest.shape[1]
    return dest.reshape(2, workers, t // (workers * chunk), chunk)


def _hier_moe(l, h2p, h2s, route_p, route_s, route_tp, route_ts, counts, x1p, x1s, w_gate, w_up, w_down):
    tp, ts = h2p.shape[0], h2s.shape[0]
    n_assign = 2 * (tp + ts)
    n_blocks = -(-n_assign // MOE_BM) + N_EXPERTS
    pcounts = (counts + MOE_BM - 1) // MOE_BM * MOE_BM
    pends = jnp.cumsum(pcounts)
    poffsets = pends - pcounts
    starts = jnp.arange(n_blocks, dtype=jnp.int32) * MOE_BM
    block_e = jnp.minimum(jnp.sum((pends[None, :] <= starts[:, None]).astype(jnp.int32), axis=1),
                          N_EXPERTS - 1)
    experts = jnp.arange(N_EXPERTS, dtype=jnp.int32)

    def lookup(table, idx):
        return jnp.sum(jnp.where(idx[..., None] == experts, table, 0), axis=-1)

    rows_valid = jnp.clip(lookup(poffsets + counts, block_e) - starts, 0, MOE_BM).astype(jnp.int32)
    used = counts > 0
    last_e = jnp.max(jnp.where(used, jnp.arange(N_EXPERTS, dtype=jnp.int32), 0))
    block_e = jnp.where(rows_valid > 0, block_e, last_e).astype(jnp.int32)
    later = used[None, :] & (jnp.arange(N_EXPERTS)[None, :] > jnp.arange(N_EXPERTS)[:, None])
    next_used = jnp.min(jnp.where(later, jnp.arange(N_EXPERTS, dtype=jnp.int32)[None, :], N_EXPERTS), axis=1)
    next_used = jnp.where(next_used >= N_EXPERTS, -1, next_used).astype(jnp.int32)
    slot_of = ((jnp.cumsum(used.astype(jnp.int32)) - 1) & 1).astype(jnp.int32)
    next_e = lookup(next_used, block_e)
    slot = lookup(slot_of, block_e)

    def dest_of(route_t):
        return lookup(poffsets, route_t[0:2].astype(jnp.int32)) + route_t[4:6].astype(jnp.int32)

    dest_p, dest_s = dest_of(route_tp), dest_of(route_ts)
    n_sw = ts // SAMPLE_CH
    xd = _sc_dispatch(h2p, h2s, _dest_layout(dest_p, SC_WORKERS, DISP_CH),
                      _dest_layout(dest_s, n_sw, SAMPLE_CH), n_blocks * MOE_BM)
    yd = _moe_experts(l, block_e, rows_valid, next_e, slot, xd, w_gate, w_up, w_down)
    g = _sc_combine_gather(yd, _dest_layout(dest_p, SC_WORKERS, COMB_CH),
                           _dest_layout(dest_s, n_sw, SAMPLE_CH), tp, ts)
    return _combine(x1p, g, route_p, 0, TM_PROJ), _combine(x1s, g, route_s, tp, ts)


def kernel(x_prompt, x_sample, state_pool, cache_k_win, cache_v_win, norm_attn_g, w_in, pool_w, pool_scale, q_norm_g, k_norm_g, attn_sinks, w_out, norm_ffn_g, router_group_w, router_group_b, router_expert_w, router_expert_b, w_gate, w_up, w_down):
    n_p, t_p, d = x_prompt.shape
    n_s, t_s, _ = x_sample.shape
    depth = w_in.shape[0]
    lw_s = cache_k_win.shape[2]
    assert t_s == 1 and lw_s == WINDOW and d == D_MODEL
    assert t_p % TM_PROJ == 0 and t_p >= WINDOW
    past_len = 16384

    w_in_b = w_in.astype(BF16)
    w_out_b = w_out.astype(BF16)
    g_attn = norm_attn_g.reshape(depth, 1, D_MODEL)
    g_ffn = norm_ffn_g.reshape(depth, 1, D_MODEL)
    qg = (jnp.tile(q_norm_g, (1, N_HEADS)) * ATTN_SCALE).reshape(depth, 1, Q_W)
    kg = jnp.tile(k_norm_g, (1, N_KV_HEADS)).reshape(depth, 1, KV_W)
    seg = jnp.arange(256) // HEAD_DIM
    bd = jnp.where(seg[:, None] == seg[None, :], 1.0 / HEAD_DIM, 0.0).astype(BF16)
    wp = jnp.zeros((depth, 2, 256, 256), F32)
    for p in range(2):
        wp = wp.at[:, p, :POOL_GC, :POOL_GC].set(pool_w[:, 2 * p])
        wp = wp.at[:, p, POOL_GC:, POOL_GC:].set(pool_w[:, 2 * p + 1])
    wp = wp.astype(BF16)
    ps = pool_scale.reshape(depth, 1, POOL_W)
    wr = jnp.zeros((depth, D_MODEL, LANES), F32)
    wr = wr.at[:, :, :N_EXPERTS].set(router_expert_w)
    wr = wr.at[:, :, GROUP_LANE0:GROUP_LANE0 + N_EXPERT_GROUPS].set(router_group_w)
    wr = wr.astype(BF16)
    br = jnp.zeros((depth, 1, LANES), F32)
    br = br.at[:, 0, :N_EXPERTS].set(router_expert_b)
    br = br.at[:, 0, GROUP_LANE0:GROUP_LANE0 + N_EXPERT_GROUPS].set(router_group_b)

    slopes = jnp.exp2(-8.0 * jnp.arange(1, N_HEADS + 1, dtype=F32) / N_HEADS)
    bias_p = _prompt_bias_t()
    dist_s = (WINDOW - 1) - jnp.arange(WINDOW, dtype=F32)
    bias_s = -slopes[:, None] * dist_s[None, :]
    state2d = state_pool.reshape(depth, n_s, POOL_STATE * POOL_W)
    ck_all = cache_k_win.reshape(depth, n_s, lw_s, KV_W)
    cv_all = cache_v_win.reshape(depth, n_s, lw_s, KV_W)

    xp = x_prompt.reshape(n_p * t_p, D_MODEL)
    xs = x_sample.reshape(n_s, D_MODEL)
    lw_p = min(WINDOW, t_p)
    pool_p, kp_new, vp_new, pool_s, ks_new, vs_new = [], [], [], [], [], []
    zero_cnt = jnp.zeros((1, LANES), F32)
    for l in range(depth):
        sinks = attn_sinks[l]
        pool_o, q, k, vt, utail, ktail, vtail = _proj_pool_prompt(
            l, xp, n_p, t_p, g_attn, w_in_b, qg, kg, bd, wp, ps)
        attn_o = _attn_prompt(q, k, vt, bias_p, sinks, n_p, t_p)
        x1p, h2p, route_p, route_tp, cnt_p = _merge_router(
            l, pool_o, attn_o, xp, w_out_b, g_ffn, wr, br, zero_cnt, TM_PROJ)
        pool_p.append(utail[:, 16 - POOL_STATE:, :])
        kp_new.append(ktail)
        vp_new.append(vtail)
        pool_so, attn_so, u_s, kc_s, vc_s = _sample_mixer(
            l, xs, g_attn, w_in_b, qg, kg, bd, wp, ps, state2d, ck_all, cv_all,
            sinks.reshape(N_HEADS, 1), bias_s, past_len)
        x1s, h2s, route_s, route_ts, cnt_all = _merge_router(
            l, pool_so, attn_so, xs, w_out_b, g_ffn, wr, br, cnt_p, n_s)
        pool_s.append(jnp.concatenate([state_pool[l][:, 1:], u_s[:, None, :]], axis=1))
        ks_new.append(kc_s)
        vs_new.append(vc_s)
        counts = cnt_all[0, :N_EXPERTS].astype(jnp.int32)
        xp, xs = _hier_moe(l, h2p, h2s, route_p, route_s, route_tp, route_ts, counts, x1p, x1s,
                           w_gate, w_up, w_down)
    return (xp.reshape(n_p, t_p, D_MODEL), xs.reshape(n_s, t_s, D_MODEL),
            jnp.stack(pool_p),
            jnp.stack(kp_new).reshape(depth, n_p, lw_p, N_KV_HEADS, HEAD_DIM),
            jnp.stack(vp_new).reshape(depth, n_p, lw_p, N_KV_HEADS, HEAD_DIM),
            jnp.stack(pool_s),
            jnp.stack(ks_new).reshape(depth, n_s, lw_s, N_KV_HEADS, HEAD_DIM),
            jnp.stack(vs_new).reshape(depth, n_s, lw_s, N_KV_HEADS, HEAD_DIM))
```

```python
import functools

import jax
import jax.numpy as jnp
from jax import lax
from jax.experimental import pallas as pl
from jax.experimental.pallas import tpu as pltpu
from jax.experimental.pallas import tpu_sc as plsc

D_MODEL = 1024
POOL_W = 512
POOL_WINDOWS = (2, 4, 8, 16)
POOL_GC = 128
POOL_STATE = 15
HEAD_DIM = 64
N_HEADS = 8
N_KV_HEADS = 2
GQA_GROUP = 4
Q_W = 512
KV_W = 128
D_IN = POOL_W + Q_W + 2 * KV_W
WINDOW = 128
ATTN_SCALE = HEAD_DIM ** -0.5
N_EXPERT_GROUPS = 4
EXPERTS_PER_GROUP = 8
N_EXPERTS = 32
EXPERT_FF = 512
EPS = 1e-6

LANES = 128
HALO = 32
TM_PROJ = 512
ATTN_QB = 4
MOE_BM = 256
GROUP_LANE0 = 32
ROUTE_FIELDS = 8
SC_CORES = 2
SC_SUBCORES = 16
SC_WORKERS = SC_CORES * SC_SUBCORES
DISP_CH = 64
COMB_CH = 64
SAMPLE_CH = 32
VMEM_LIMIT = 48 * 1024 * 1024

BF16 = jnp.bfloat16
F32 = jnp.float32


def _pack_bf16_pairs(h):
    w = h.shape[1] // 2
    hi = lax.bitcast_convert_type(h[:, :w].astype(F32), jnp.uint32)
    lo = lax.bitcast_convert_type(h[:, w:].astype(F32), jnp.uint32)
    return lax.bitcast_convert_type(hi | (lo >> 16), jnp.int32)


def _unpack_bf16_pairs(words):
    u = lax.bitcast_convert_type(words, jnp.uint32)
    hi = lax.bitcast_convert_type(u & jnp.uint32(0xFFFF0000), F32)
    lo = lax.bitcast_convert_type(u << 16, F32)
    return jnp.concatenate([hi, lo], axis=-1)


def _segment_mean_sq(a, bd):
    w = a.shape[1]
    return jnp.dot((a * a).astype(BF16), bd[:w, :w], preferred_element_type=F32)


def _project(x, g, w_in, qg, kg, bd):
    ms = jnp.mean(x * x, axis=-1, keepdims=True)
    h = (x * lax.rsqrt(ms + EPS) * g).astype(BF16)
    z = jnp.dot(h, w_in, preferred_element_type=F32)
    u = z[:, :POOL_W]
    q = z[:, POOL_W:POOL_W + Q_W]
    k = z[:, POOL_W + Q_W:POOL_W + Q_W + KV_W]
    v = z[:, POOL_W + Q_W + KV_W:]
    qn = []
    for c in range(Q_W // 256):
        qc = q[:, c * 256:(c + 1) * 256]
        qn.append(qc * lax.rsqrt(_segment_mean_sq(qc, bd) + EPS))
    qn = jnp.concatenate(qn, axis=-1) * qg
    kn = k * lax.rsqrt(_segment_mean_sq(k, bd) + EPS) * kg
    return u, qn, kn, v


def _pool_project(d_groups, wp_ref, ps):
    outs = []
    for p in range(2):
        dp = jnp.concatenate([d_groups[2 * p], d_groups[2 * p + 1]], axis=-1).astype(BF16)
        y = jnp.dot(dp, wp_ref[p], preferred_element_type=F32)
        outs.append(y * ps[:, p * 256:(p + 1) * 256])
    return jnp.concatenate(outs, axis=-1)


def _proj_pool_kernel(x_ref, g_ref, win_ref, qg_ref, kg_ref, bd_ref, wp_ref, ps_ref,
                      pool_ref, q_ref, k_ref, vt_ref, utail_ref, ktail_ref, vtail_ref,
                      ext_ref, sa_ref, sb_ref, *, tm, n_j):
    j = pl.program_id(1)
    u, qn, kn, v = _project(x_ref[...], g_ref[...], win_ref[...], qg_ref[...], kg_ref[...], bd_ref[...])
    q_ref[...] = qn.astype(BF16)
    k_ref[...] = kn.astype(BF16)
    vt_ref[...] = jnp.transpose(v).astype(BF16)

    @pl.when(j == 0)
    def _():
        ext_ref[0:HALO, :] = jnp.zeros((HALO, POOL_W), F32)

    r = tm + HALO
    ext_ref[HALO:r, :] = u
    sa_ref[8:r, :] = ext_ref[8:r, :] + ext_ref[7:r - 1, :]
    sb_ref[16:r, 128:] = sa_ref[16:r, 128:] + sa_ref[14:r - 2, 128:]
    sa_ref[24:r, 256:] = sb_ref[24:r, 256:] + sb_ref[20:r - 4, 256:]
    sb_ref[32:r, 384:] = sa_ref[32:r, 384:] + sa_ref[24:r - 8, 384:]
    pos1 = j * tm + lax.broadcasted_iota(jnp.int32, (tm, POOL_GC), 0) + 1
    sums = (sa_ref, sb_ref, sa_ref, sb_ref)
    d_groups = []
    for gi, w in enumerate(POOL_WINDOWS):
        sl = slice(gi * POOL_GC, (gi + 1) * POOL_GC)
        cnt = jnp.minimum(pos1, w).astype(F32)
        d_groups.append(sums[gi][HALO:r, sl] / cnt - u[:, sl])
    pool_ref[...] = _pool_project(d_groups, wp_ref, ps_ref[...]).astype(BF16)
    ext_ref[16:HALO, :] = ext_ref[tm + 16:r, :]

    @pl.when(j == n_j - 1)
    def _():
        utail_ref[...] = u[tm - 16:, :]
        ktail_ref[...] = kn[tm - WINDOW:, :]
        vtail_ref[...] = v[tm - WINDOW:, :]


def _proj_pool_prompt(l, x2d, n_seq, seq, g_attn, w_in, qg, kg, bd, wp, ps):
    tm = TM_PROJ
    n_j = seq // tm
    t = n_seq * seq
    row = lambda b, j: (b * n_j + j, 0)
    lay = lambda b, j: (l, 0, 0)
    return pl.pallas_call(
        functools.partial(_proj_pool_kernel, tm=tm, n_j=n_j),
        grid=(n_seq, n_j),
        in_specs=[
            pl.BlockSpec((tm, D_MODEL), row),
            pl.BlockSpec((None, 1, D_MODEL), lay),
            pl.BlockSpec((None, D_MODEL, D_IN), lay),
            pl.BlockSpec((None, 1, Q_W), lay),
            pl.BlockSpec((None, 1, KV_W), lay),
            pl.BlockSpec((256, 256), lambda b, j: (0, 0)),
            pl.BlockSpec((None, 2, 256, 256), lambda b, j: (l, 0, 0, 0)),
            pl.BlockSpec((None, 1, POOL_W), lay),
        ],
        out_specs=[
            pl.BlockSpec((tm, POOL_W), row),
            pl.BlockSpec((tm, Q_W), row),
            pl.BlockSpec((tm, KV_W), row),
            pl.BlockSpec((KV_W, tm), lambda b, j: (0, b * n_j + j)),
            pl.BlockSpec((None, 16, POOL_W), lambda b, j: (b, 0, 0)),
            pl.BlockSpec((None, WINDOW, KV_W), lambda b, j: (b, 0, 0)),
            pl.BlockSpec((None, WINDOW, KV_W), lambda b, j: (b, 0, 0)),
        ],
        out_shape=[
            jax.ShapeDtypeStruct((t, POOL_W), BF16),
            jax.ShapeDtypeStruct((t, Q_W), BF16),
            jax.ShapeDtypeStruct((t, KV_W), BF16),
            jax.ShapeDtypeStruct((KV_W, t), BF16),
            jax.ShapeDtypeStruct((n_seq, 16, POOL_W), F32),
            jax.ShapeDtypeStruct((n_seq, WINDOW, KV_W), F32),
            jax.ShapeDtypeStruct((n_seq, WINDOW, KV_W), F32),
        ],
        scratch_shapes=[pltpu.VMEM((tm + HALO, POOL_W), F32)] * 3,
        compiler_params=pltpu.CompilerParams(
            dimension_semantics=("arbitrary", "arbitrary"), vmem_limit_bytes=VMEM_LIMIT),
        name="proj_pool_prompt",
    )(x2d, g_attn, w_in, qg, kg, bd, wp, ps)


def _attn_kernel(sink_ref, q_ref, kp_ref, kc_ref, vtp_ref, vtc_ref, bias_ref, o_ref):
    j = pl.program_id(1)
    kk_all = jnp.concatenate([kp_ref[...], kc_ref[...]], axis=0)
    vt_all = jnp.concatenate([vtp_ref[...], vtc_ref[...]], axis=1)
    from_prev = (lax.broadcasted_iota(jnp.int32, (WINDOW, WINDOW), 0)
                 > lax.broadcasted_iota(jnp.int32, (WINDOW, WINDOW), 1))
    for blk in range(ATTN_QB):
        q = q_ref[blk * WINDOW:(blk + 1) * WINDOW, :]
        kk = kk_all[blk * WINDOW:(blk + 2) * WINDOW, :]
        vt = vt_all[:, blk * WINDOW:(blk + 2) * WINDOW]
        variant = jnp.minimum(j, 1) if blk == 0 else 1
        outs = []
        for kv in range(N_KV_HEADS):
            heads = range(kv * GQA_GROUP, (kv + 1) * GQA_GROUP)
            q_rows = jnp.concatenate([q[:, h * HEAD_DIM:(h + 1) * HEAD_DIM] for h in heads], axis=0)
            s_all = lax.dot_general(kk[:, kv * HEAD_DIM:(kv + 1) * HEAD_DIM], q_rows,
                                    (((1,), (1,)), ((), ())), preferred_element_type=F32)
            vt_kv = vt[kv * HEAD_DIM:(kv + 1) * HEAD_DIM, :]
            for g, h in enumerate(heads):
                sg = s_all[:, g * WINDOW:(g + 1) * WINDOW]
                s = jnp.where(from_prev, sg[0:WINDOW, :], sg[WINDOW:, :]) + bias_ref[variant, h]
                sink = sink_ref[h]
                m = jnp.maximum(jnp.max(s, axis=0, keepdims=True), sink)
                p = jnp.exp(s - m)
                denom = jnp.sum(p, axis=0, keepdims=True) + jnp.exp(sink - m)
                p_keys = jnp.concatenate([jnp.where(from_prev, p, 0.0), jnp.where(from_prev, 0.0, p)], axis=0)
                o_t = jnp.dot(vt_kv, p_keys.astype(BF16), preferred_element_type=F32)
                outs.append(o_t / denom)
        o_ref[blk * WINDOW:(blk + 1) * WINDOW, :] = jnp.transpose(jnp.concatenate(outs, axis=0)).astype(BF16)


def _attn_prompt(q, k, vt, bias_t, sinks, n_seq, seq):
    tq = ATTN_QB * WINDOW
    nj = seq // tq
    t = n_seq * seq
    cur = lambda b, j: (b * nj + j, 0)
    prev = lambda b, j: (jnp.maximum((b * nj + j) * ATTN_QB - 1, 0), 0)
    cur_t = lambda b, j: (0, b * nj + j)
    prev_t = lambda b, j: (0, jnp.maximum((b * nj + j) * ATTN_QB - 1, 0))
    return pl.pallas_call(
        _attn_kernel,
        grid=(n_seq, nj),
        in_specs=[
            pl.BlockSpec(memory_space=pltpu.SMEM),
            pl.BlockSpec((tq, Q_W), cur),
            pl.BlockSpec((WINDOW, KV_W), prev),
            pl.BlockSpec((tq, KV_W), cur),
            pl.BlockSpec((KV_W, WINDOW), prev_t),
            pl.BlockSpec((KV_W, tq), cur_t),
            pl.BlockSpec((2, N_HEADS, WINDOW, WINDOW), lambda b, j: (0, 0, 0, 0)),
        ],
        out_specs=pl.BlockSpec((tq, Q_W), cur),
        out_shape=jax.ShapeDtypeStruct((t, Q_W), BF16),
        compiler_params=pltpu.CompilerParams(
            dimension_semantics=("arbitrary", "arbitrary"), vmem_limit_bytes=VMEM_LIMIT),
        name="attn_prompt",
    )(sinks, q, k, k, vt, vt, bias_t)


def _prompt_bias_t():
    r = jnp.arange(WINDOW, dtype=jnp.int32)[None, :]
    c = jnp.arange(WINDOW, dtype=jnp.int32)[:, None]
    from_prev = c > r
    dist = r - c + jnp.where(from_prev, WINDOW, 0)
    slopes = jnp.exp2(-8.0 * jnp.arange(1, N_HEADS + 1, dtype=F32) / N_HEADS)
    later = -slopes[:, None, None] * dist.astype(F32)[None]
    first = jnp.where(from_prev[None], -jnp.inf, later)
    return jnp.stack([first, later])


def _sample_kernel(x_ref, g_ref, win_ref, qg_ref, kg_ref, bd_ref, wp_ref, ps_ref,
                   st_ref, ck_ref, cv_ref, sink_ref, bias_ref, perm_ref,
                   pool_ref, attn_ref, u_ref, kc_ref, vc_ref, *, ns, pos0):
    u, qn, kn, v = _project(x_ref[...], g_ref[...], win_ref[...], qg_ref[...], kg_ref[...], bd_ref[...])
    u_ref[...] = u
    kc_ref[:, 0:WINDOW - 1, :] = ck_ref[:, 1:WINDOW, :]
    vc_ref[:, 0:WINDOW - 1, :] = cv_ref[:, 1:WINDOW, :]
    for n in range(ns):
        kc_ref[n, WINDOW - 1:WINDOW, :] = kn[n:n + 1, :]
        vc_ref[n, WINDOW - 1:WINDOW, :] = v[n:n + 1, :]

    d_groups = []
    for gi, w in enumerate(POOL_WINDOWS):
        lo = gi * POOL_GC
        acc = u[:, lo:lo + POOL_GC]
        for back in range(1, w):
            off = (POOL_STATE - back) * POOL_W + lo
            acc = acc + st_ref[:, off:off + POOL_GC]
        d_groups.append(acc / float(min(pos0 + 1, w)) - u[:, lo:lo + POOL_GC])
    pool_ref[...] = _pool_project(d_groups, wp_ref, ps_ref[...]).astype(BF16)

    zeros = jnp.zeros((ns, HEAD_DIM), F32)
    stacked = []
    for h in range(N_HEADS):
        piece = qn[:, h * HEAD_DIM:(h + 1) * HEAD_DIM]
        pair = [piece, zeros] if h < GQA_GROUP else [zeros, piece]
        stacked.append(jnp.concatenate(pair, axis=-1))
    q_hn = jnp.concatenate(stacked, axis=0).astype(BF16)
    q_nh = jnp.dot(perm_ref[0], q_hn, preferred_element_type=F32).astype(BF16)

    keys = kc_ref[...].reshape(ns * WINDOW, KV_W).astype(BF16)
    vals = vc_ref[...].reshape(ns * WINDOW, KV_W).astype(BF16)
    s_all = lax.dot_general(q_nh, keys, (((1,), (1,)), ((), ())), preferred_element_type=F32)
    sink = sink_ref[...]
    bias = bias_ref[...]
    zero_blk = jnp.zeros((N_HEADS, WINDOW), F32)
    p_rows = []
    for n in range(ns):
        s = s_all[n * N_HEADS:(n + 1) * N_HEADS, n * WINDOW:(n + 1) * WINDOW] + bias
        m = jnp.maximum(jnp.max(s, axis=-1, keepdims=True), sink)
        p = jnp.exp(s - m)
        denom = jnp.sum(p, axis=-1, keepdims=True) + jnp.exp(sink - m)
        p_rows.append(jnp.concatenate([zero_blk] * n + [p / denom] + [zero_blk] * (ns - 1 - n), axis=-1))
    p_blockdiag = jnp.concatenate(p_rows, axis=0).astype(BF16)
    o_nh = jnp.dot(p_blockdiag, vals, preferred_element_type=F32).astype(BF16)
    o_hn = jnp.dot(perm_ref[1], o_nh, preferred_element_type=F32)
    pieces = []
    for h in range(N_HEADS):
        kv = h // GQA_GROUP
        pieces.append(o_hn[h * ns:(h + 1) * ns, kv * HEAD_DIM:(kv + 1) * HEAD_DIM])
    attn_ref[...] = jnp.concatenate(pieces, axis=-1).astype(BF16)


def _sample_mixer(l, xs, g_attn, w_in, qg, kg, bd, wp, ps, state2d, ck, cv, sink8, bias_s, pos0):
    n = xs.shape[0]
    ns = 32
    row = lambda i: (i, 0)
    lay = lambda i: (l, 0, 0)
    src = jnp.arange(ns * N_HEADS)
    perm = (((src % N_HEADS) * ns + src // N_HEADS)[:, None] == src[None, :]).astype(BF16)
    perms = jnp.stack([perm, perm.T])
    return pl.pallas_call(
        functools.partial(_sample_kernel, ns=ns, pos0=pos0),
        grid=(n // ns,),
        in_specs=[
            pl.BlockSpec((ns, D_MODEL), row),
            pl.BlockSpec((None, 1, D_MODEL), lay),
            pl.BlockSpec((None, D_MODEL, D_IN), lay),
            pl.BlockSpec((None, 1, Q_W), lay),
            pl.BlockSpec((None, 1, KV_W), lay),
            pl.BlockSpec((256, 256), lambda i: (0, 0)),
            pl.BlockSpec((None, 2, 256, 256), lambda i: (l, 0, 0, 0)),
            pl.BlockSpec((None, 1, POOL_W), lay),
            pl.BlockSpec((None, ns, POOL_STATE * POOL_W), lambda i: (l, i, 0)),
            pl.BlockSpec((None, ns, WINDOW, KV_W), lambda i: (l, i, 0, 0)),
            pl.BlockSpec((None, ns, WINDOW, KV_W), lambda i: (l, i, 0, 0)),
            pl.BlockSpec((N_HEADS, 1), lambda i: (0, 0)),
            pl.BlockSpec((N_HEADS, WINDOW), lambda i: (0, 0)),
            pl.BlockSpec((2, ns * N_HEADS, ns * N_HEADS), lambda i: (0, 0, 0)),
        ],
        out_specs=[
            pl.BlockSpec((ns, POOL_W), row),
            pl.BlockSpec((ns, Q_W), row),
            pl.BlockSpec((ns, POOL_W), row),
            pl.BlockSpec((ns, WINDOW, KV_W), lambda i: (i, 0, 0)),
            pl.BlockSpec((ns, WINDOW, KV_W), lambda i: (i, 0, 0)),
        ],
        out_shape=[
            jax.ShapeDtypeStruct((n, POOL_W), BF16),
            jax.ShapeDtypeStruct((n, Q_W), BF16),
            jax.ShapeDtypeStruct((n, POOL_W), F32),
            jax.ShapeDtypeStruct((n, WINDOW, KV_W), F32),
            jax.ShapeDtypeStruct((n, WINDOW, KV_W), F32),
        ],
        compiler_params=pltpu.CompilerParams(
            dimension_semantics=("arbitrary",), vmem_limit_bytes=VMEM_LIMIT),
        name="sample_mixer",
    )(xs, g_attn, w_in, qg, kg, bd, wp, ps, state2d, ck, cv, sink8, bias_s, perms)


def _merge_router_kernel(pool_ref, attn_ref, x_ref, wout_ref, g_ref, wr_ref, br_ref, utri_ref, cin_ref,
                         x1_ref, h2_ref, route_ref, route_t_ref, cnt_ref):
    i = pl.program_id(0)

    @pl.when(i == 0)
    def _():
        cnt_ref[...] = cin_ref[...]

    y = jnp.dot(pool_ref[...], wout_ref[0:POOL_W, :], preferred_element_type=F32)
    y = y + jnp.dot(attn_ref[...], wout_ref[POOL_W:, :], preferred_element_type=F32)
    x1 = x_ref[...] + y
    x1_ref[...] = x1
    ms = jnp.mean(x1 * x1, axis=-1, keepdims=True)
    h2 = (x1 * lax.rsqrt(ms + EPS) * g_ref[...]).astype(BF16)
    h2_ref[...] = _pack_bf16_pairs(h2)
    logits = jnp.dot(h2, wr_ref[...], preferred_element_type=F32) + br_ref[...]

    tm = logits.shape[0]
    lt = jnp.transpose(logits)
    sub = lax.broadcasted_iota(jnp.int32, (EXPERTS_PER_GROUP, tm), 0)
    neg = -jnp.inf
    big = jnp.int32(EXPERTS_PER_GROUP)
    gl = jnp.where(sub < N_EXPERT_GROUPS, lt[GROUP_LANE0:GROUP_LANE0 + EXPERTS_PER_GROUP, :], neg)
    gmax = jnp.max(gl, axis=0, keepdims=True)
    grp = jnp.min(jnp.where(gl == gmax, sub, big), axis=0, keepdims=True)
    g_w = 1.0 / jnp.sum(jnp.exp(gl - gmax), axis=0, keepdims=True)
    el = lt[(N_EXPERT_GROUPS - 1) * EXPERTS_PER_GROUP:N_EXPERT_GROUPS * EXPERTS_PER_GROUP, :]
    for gi in range(N_EXPERT_GROUPS - 2, -1, -1):
        el = jnp.where(grp == gi, lt[gi * EXPERTS_PER_GROUP:(gi + 1) * EXPERTS_PER_GROUP, :], el)
    v1 = jnp.max(el, axis=0, keepdims=True)
    i1 = jnp.min(jnp.where(el == v1, sub, big), axis=0, keepdims=True)
    el2 = jnp.where(sub == i1, neg, el)
    v2 = jnp.max(el2, axis=0, keepdims=True)
    i2 = jnp.min(jnp.where(el2 == v2, sub, big), axis=0, keepdims=True)
    e21 = jnp.exp(v2 - v1)
    w1 = g_w / (1.0 + e21)
    w2 = g_w * e21 / (1.0 + e21)
    e1 = grp * EXPERTS_PER_GROUP + i1
    e2 = grp * EXPERTS_PER_GROUP + i2

    esub = lax.broadcasted_iota(jnp.int32, (N_EXPERTS, tm), 0)
    oh1 = esub == e1
    oh2 = esub == e2
    c = jnp.where(oh1 | oh2, 1.0, 0.0)
    prefix = jnp.dot(c.astype(BF16), utri_ref[...], preferred_element_type=F32) + cnt_ref[...]
    r1 = jnp.sum(jnp.where(oh1, prefix, 0.0), axis=0, keepdims=True)
    r2 = jnp.sum(jnp.where(oh2, prefix, 0.0), axis=0, keepdims=True)
    cnt_ref[...] = cnt_ref[...] + jnp.sum(c, axis=1, keepdims=True)

    fields = jnp.zeros((ROUTE_FIELDS, tm), F32)
    for idx, val in enumerate((e1.astype(F32), e2.astype(F32), w1, w2, r1, r2)):
        fields = jnp.where(sub == idx, val, fields)
    route_t_ref[...] = fields
    padded = jnp.concatenate([fields, jnp.zeros((LANES - ROUTE_FIELDS, tm), F32)], axis=0)
    route_ref[...] = jnp.transpose(padded)


def _merge_router(l, pool, attn, x2d, w_out, g_ffn, wr, br, cnt_in, tm):
    t = x2d.shape[0]
    utri = (jnp.arange(tm)[:, None] < jnp.arange(tm)[None, :]).astype(BF16)
    row = lambda i: (i, 0)
    lay = lambda i: (l, 0, 0)
    return pl.pallas_call(
        _merge_router_kernel,
        grid=(t // tm,),
        in_specs=[
            pl.BlockSpec((tm, POOL_W), row),
            pl.BlockSpec((tm, Q_W), row),
            pl.BlockSpec((tm, D_MODEL), row),
            pl.BlockSpec((None, D_MODEL, D_MODEL), lay),
            pl.BlockSpec((None, 1, D_MODEL), lay),
            pl.BlockSpec((None, D_MODEL, LANES), lay),
            pl.BlockSpec((None, 1, LANES), lay),
            pl.BlockSpec((tm, tm), lambda i: (0, 0)),
            pl.BlockSpec((N_EXPERTS, 1), lambda i: (0, 0)),
        ],
        out_specs=[
            pl.BlockSpec((tm, D_MODEL), row),
            pl.BlockSpec((tm, D_MODEL // 2), row),
            pl.BlockSpec((tm, LANES), row),
            pl.BlockSpec((ROUTE_FIELDS, tm), lambda i: (0, i)),
            pl.BlockSpec((N_EXPERTS, 1), lambda i: (0, 0)),
        ],
        out_shape=[
            jax.ShapeDtypeStruct((t, D_MODEL), F32),
            jax.ShapeDtypeStruct((t, D_MODEL // 2), jnp.int32),
            jax.ShapeDtypeStruct((t, LANES), F32),
            jax.ShapeDtypeStruct((ROUTE_FIELDS, t), F32),
            jax.ShapeDtypeStruct((N_EXPERTS, 1), F32),
        ],
        compiler_params=pltpu.CompilerParams(
            dimension_semantics=("arbitrary",), vmem_limit_bytes=VMEM_LIMIT),
        name="merge_router",
    )(pool, attn, x2d, w_out, g_ffn, wr, br, utri, cnt_in)


def _moe_kernel(be_ref, rv_ref, nx_ref, sl_ref, xd_ref, wg_hbm, wu_hbm, wd_hbm, yd_ref,
                wg_f, wu_f, wd_f, wg_s, wu_s, wd_s, sem, *, layer):
    i = pl.program_id(0)
    expert = be_ref[i]
    changed = (i == 0) | (expert != be_ref[jnp.maximum(i - 1, 0)])
    rows_valid = rv_ref[i]
    slot = sl_ref[i]

    def weight_copies(e, s):
        return [pltpu.make_async_copy(w_hbm.at[layer, e], w_f.at[s], sem.at[s, n])
                for n, (w_hbm, w_f) in enumerate(((wg_hbm, wg_f), (wu_hbm, wu_f), (wd_hbm, wd_f)))]

    @pl.when(i == 0)
    def _():
        for c in weight_copies(expert, slot):
            c.start()

    @pl.when(changed)
    def _():
        for c in weight_copies(expert, slot):
            c.wait()

        @pl.when(nx_ref[i] >= 0)
        def _():
            for c in weight_copies(nx_ref[i], 1 - slot):
                c.start(priority=1)

        wg_s[...] = wg_f[slot].astype(BF16)
        wu_s[...] = wu_f[slot].astype(BF16)
        wd_s[...] = wd_f[slot].astype(BF16)

    @pl.when(rows_valid > 0)
    def _():
        row = lax.broadcasted_iota(jnp.int32, (MOE_BM, D_MODEL), 0)
        x = jnp.where(row < rows_valid, _unpack_bf16_pairs(xd_ref[...]), 0.0).astype(BF16)
        gate = jnp.dot(x, wg_s[...], preferred_element_type=F32)
        up = jnp.dot(x, wu_s[...], preferred_element_type=F32)
        act = (gate * jax.nn.sigmoid(gate) * up).astype(BF16)
        y = jnp.dot(act, wd_s[...], preferred_element_type=F32)
        yd_ref[...] = _pack_bf16_pairs(y.astype(BF16))

    @pl.when(rows_valid <= 0)
    def _():
        yd_ref[...] = jnp.zeros(yd_ref.shape, jnp.int32)


def _moe_experts(l, block_e, rows_valid, next_e, slot, xd, w_gate, w_up, w_down):
    n_blocks = xd.shape[0] // MOE_BM
    row = lambda i, *_: (i, 0)
    return pl.pallas_call(
        functools.partial(_moe_kernel, layer=l),
        grid_spec=pltpu.PrefetchScalarGridSpec(
            num_scalar_prefetch=4,
            grid=(n_blocks,),
            in_specs=[
                pl.BlockSpec((MOE_BM, D_MODEL // 2), row),
                pl.BlockSpec(memory_space=pl.ANY),
                pl.BlockSpec(memory_space=pl.ANY),
                pl.BlockSpec(memory_space=pl.ANY),
            ],
            out_specs=pl.BlockSpec((MOE_BM, D_MODEL // 2), row),
            scratch_shapes=[
                pltpu.VMEM((2, D_MODEL, EXPERT_FF), F32),
                pltpu.VMEM((2, D_MODEL, EXPERT_FF), F32),
                pltpu.VMEM((2, EXPERT_FF, D_MODEL), F32),
                pltpu.VMEM((D_MODEL, EXPERT_FF), BF16),
                pltpu.VMEM((D_MODEL, EXPERT_FF), BF16),
                pltpu.VMEM((EXPERT_FF, D_MODEL), BF16),
                pltpu.SemaphoreType.DMA((2, 3)),
            ],
        ),
        out_shape=jax.ShapeDtypeStruct((n_blocks * MOE_BM, D_MODEL // 2), jnp.int32),
        compiler_params=pltpu.CompilerParams(
            dimension_semantics=("arbitrary",), vmem_limit_bytes=VMEM_LIMIT),
        name="moe_experts",
    )(block_e, rows_valid, next_e, slot, xd, w_gate, w_up, w_down)


def _sc_worker_id():
    return lax.axis_index("s") * SC_CORES + lax.axis_index("c")


def _sc_dispatch(hp, hs, dest_p, dest_s, n_rows):
    tp, width = hp.shape
    per_w = tp // SC_WORKERS
    n_ch = per_w // DISP_CH
    n_sw = hs.shape[0] // SAMPLE_CH
    mesh = plsc.VectorSubcoreMesh(core_axis_name="c", subcore_axis_name="s")

    @functools.partial(
        pl.kernel, mesh=mesh,
        out_type=jax.ShapeDtypeStruct((n_rows, width), jnp.int32),
        scratch_types=[
            pltpu.VMEM((2, n_ch, DISP_CH), jnp.int32),
            pltpu.VMEM((2, 1, SAMPLE_CH), jnp.int32),
            pltpu.VMEM((2, DISP_CH, width), jnp.int32),
            pltpu.SemaphoreType.DMA((2,)),
            pltpu.SemaphoreType.DMA((2,)),
        ],
        name="sc_dispatch",
    )
    def k(hp_hbm, hs_hbm, dp_hbm, ds_hbm, xd_hbm, idx_v, idxs_v, bufs, rsem, wsem):
        wid = _sc_worker_id()
        base = wid * per_w
        for kk in range(2):
            pltpu.sync_copy(dp_hbm.at[kk, wid], idx_v.at[kk])
        reads = [pltpu.make_async_copy(hp_hbm.at[pl.ds(base + j * DISP_CH, DISP_CH)],
                                       bufs.at[j % 2], rsem.at[j % 2]) for j in range(n_ch)]
        reads[0].start()
        for j in range(n_ch):
            if j + 1 < n_ch:
                reads[j + 1].start()
            reads[j].wait()
            writes = [pltpu.make_async_copy(bufs.at[j % 2], xd_hbm.at[idx_v.at[kk, j]], wsem.at[kk])
                      for kk in range(2)]
            for w in writes:
                w.start()
            for w in writes:
                w.wait()

        @pl.when(wid < n_sw)
        def _():
            rows = bufs.at[0, pl.ds(0, SAMPLE_CH)]
            for kk in range(2):
                pltpu.sync_copy(ds_hbm.at[kk, wid], idxs_v.at[kk])
            pltpu.sync_copy(hs_hbm.at[pl.ds(wid * SAMPLE_CH, SAMPLE_CH)], rows)
            for kk in range(2):
                pltpu.sync_copy(rows, xd_hbm.at[idxs_v.at[kk, 0]])

    return k(hp, hs, dest_p, dest_s)


def _sc_combine_gather(yd, dest_p, dest_s, tp, ts):
    width = yd.shape[1]
    per_w = tp // SC_WORKERS
    n_ch = per_w // COMB_CH
    n_sw = ts // SAMPLE_CH
    mesh = plsc.VectorSubcoreMesh(core_axis_name="c", subcore_axis_name="s")

    @functools.partial(
        pl.kernel, mesh=mesh,
        out_type=jax.ShapeDtypeStruct((2, tp + ts, width), yd.dtype),
        scratch_types=[
            pltpu.VMEM((2, n_ch, COMB_CH), jnp.int32),
            pltpu.VMEM((2, 1, SAMPLE_CH), jnp.int32),
            pltpu.VMEM((2, COMB_CH, width), yd.dtype),
            pltpu.SemaphoreType.DMA((2,)),
            pltpu.SemaphoreType.DMA((2,)),
        ],
        name="sc_combine_gather",
    )
    def k(yd_hbm, dp_hbm, ds_hbm, g_hbm, idx_v, idxs_v, bufs, gsem, wsem):
        wid = _sc_worker_id()
        base = wid * per_w
        for kk in range(2):
            pltpu.sync_copy(dp_hbm.at[kk, wid], idx_v.at[kk])
        items = [(kk, j) for kk in range(2) for j in range(n_ch)]
        gathers = [pltpu.make_async_copy(yd_hbm.at[idx_v.at[kk, j]], bufs.at[n % 2], gsem.at[n % 2])
                   for n, (kk, j) in enumerate(items)]
        gathers[0].start()
        for n, (kk, j) in enumerate(items):
            if n + 1 < len(items):
                gathers[n + 1].start()
            gathers[n].wait()
            w = pltpu.make_async_copy(bufs.at[n % 2], g_hbm.at[kk, pl.ds(base + j * COMB_CH, COMB_CH)],
                                      wsem.at[n % 2])
            w.start()
            w.wait()

        @pl.when(wid < n_sw)
        def _():
            for kk in range(2):
                pltpu.sync_copy(ds_hbm.at[kk, wid], idxs_v.at[kk])
            for kk in range(2):
                rows = bufs.at[kk, pl.ds(0, SAMPLE_CH)]
                pltpu.sync_copy(yd_hbm.at[idxs_v.at[kk, 0]], rows)
                pltpu.sync_copy(rows, g_hbm.at[kk, pl.ds(tp + wid * SAMPLE_CH, SAMPLE_CH)])

    return k(yd, dest_p, dest_s)


def _combine_kernel(x1_ref, g_ref, route_ref, x2_ref):
    w1 = route_ref[:, 2:3]
    w2 = route_ref[:, 3:4]
    x2_ref[...] = x1_ref[...] + _unpack_bf16_pairs(g_ref[0]) * w1 + _unpack_bf16_pairs(g_ref[1]) * w2


def _combine(x1, g, route, row0, tm):
    t = x1.shape[0]
    blk0 = row0 // tm
    row = lambda i: (i, 0)
    return pl.pallas_call(
        _combine_kernel,
        grid=(t // tm,),
        in_specs=[
            pl.BlockSpec((tm, D_MODEL), row),
            pl.BlockSpec((2, tm, D_MODEL // 2), lambda i: (0, blk0 + i, 0)),
            pl.BlockSpec((tm, LANES), row),
        ],
        out_specs=pl.BlockSpec((tm, D_MODEL), row),
        out_shape=jax.ShapeDtypeStruct((t, D_MODEL), F32),
        compiler_params=pltpu.CompilerParams(
            dimension_semantics=("arbitrary",), vmem_limit_bytes=VMEM_LIMIT),
        name="combine",
    )(x1, g, route)


def _dest_layout(dest, workers, chunk):
    t = dest.shape[1]
    return dest.reshape(2, workers, t // (workers * chunk), chunk)


def _hier_moe(l, h2p, h2s, route_p, route_s, route_tp, route_ts, counts, x1p, x1s, w_gate, w_up, w_down):
    tp, ts = h2p.shape[0], h2s.shape[0]
    n_assign = 2 * (tp + ts)
    n_blocks = -(-n_assign // MOE_BM) + N_EXPERTS
    pcounts = (counts + MOE_BM - 1) // MOE_BM * MOE_BM
    pends = jnp.cumsum(pcounts)
    poffsets = pends - pcounts
    starts = jnp.arange(n_blocks, dtype=jnp.int32) * MOE_BM
    block_e = jnp.minimum(jnp.sum((pends[None, :] <= starts[:, None]).astype(jnp.int32), axis=1),
                          N_EXPERTS - 1)
    experts = jnp.arange(N_EXPERTS, dtype=jnp.int32)

    def lookup(table, idx):
        return jnp.sum(jnp.where(idx[..., None] == experts, table, 0), axis=-1)

    rows_valid = jnp.clip(lookup(poffsets + counts, block_e) - starts, 0, MOE_BM).astype(jnp.int32)
    used = counts > 0
    last_e = jnp.max(jnp.where(used, jnp.arange(N_EXPERTS, dtype=jnp.int32), 0))
    block_e = jnp.where(rows_valid > 0, block_e, last_e).astype(jnp.int32)
    later = used[None, :] & (jnp.arange(N_EXPERTS)[None, :] > jnp.arange(N_EXPERTS)[:, None])
    next_used = jnp.min(jnp.where(later, jnp.arange(N_EXPERTS, dtype=jnp.int32)[None, :], N_EXPERTS), axis=1)
    next_used = jnp.where(next_used >= N_EXPERTS, -1, next_used).astype(jnp.int32)
    slot_of = ((jnp.cumsum(used.astype(jnp.int32)) - 1) & 1).astype(jnp.int32)
    next_e = lookup(next_used, block_e)
    slot = lookup(slot_of, block_e)

    def dest_of(route_t):
        return lookup(poffsets, route_t[0:2].astype(jnp.int32)) + route_t[4:6].astype(jnp.int32)

    dest_p, dest_s = dest_of(route_tp), dest_of(route_ts)
    n_sw = ts // SAMPLE_CH
    xd = _sc_dispatch(h2p, h2s, _dest_layout(dest_p, SC_WORKERS, DISP_CH),
                      _dest_layout(dest_s, n_sw, SAMPLE_CH), n_blocks * MOE_BM)
    yd = _moe_experts(l, block_e, rows_valid, next_e, slot, xd, w_gate, w_up, w_down)
    g = _sc_combine_gather(yd, _dest_layout(dest_p, SC_WORKERS, COMB_CH),
                           _dest_layout(dest_s, n_sw, SAMPLE_CH), tp, ts)
    return _combine(x1p, g, route_p, 0, TM_PROJ), _combine(x1s, g, route_s, tp, ts)


def kernel(x_prompt, x_sample, state_pool, cache_k_win, cache_v_win, norm_attn_g, w_in, pool_w, pool_scale, q_norm_g, k_norm_g, attn_sinks, w_out, norm_ffn_g, router_group_w, router_group_b, router_expert_w, router_expert_b, w_gate, w_up, w_down):
    n_p, t_p, d = x_prompt.shape
    n_s, t_s, _ = x_sample.shape
    depth = w_in.shape[0]
    lw_s = cache_k_win.shape[2]
    assert t_s == 1 and lw_s == WINDOW and d == D_MODEL
    assert t_p % TM_PROJ == 0 and t_p >= WINDOW
    past_len = 16384

    w_in_b = w_in.astype(BF16)
    w_out_b = w_out.astype(BF16)
    g_attn = norm_attn_g.reshape(depth, 1, D_MODEL)
    g_ffn = norm_ffn_g.reshape(depth, 1, D_MODEL)
    qg = (jnp.tile(q_norm_g, (1, N_HEADS)) * ATTN_SCALE).reshape(depth, 1, Q_W)
    kg = jnp.tile(k_norm_g, (1, N_KV_HEADS)).reshape(depth, 1, KV_W)
    seg = jnp.arange(256) // HEAD_DIM
    bd = jnp.where(seg[:, None] == seg[None, :], 1.0 / HEAD_DIM, 0.0).astype(BF16)
    wp = jnp.zeros((depth, 2, 256, 256), F32)
    for p in range(2):
        wp = wp.at[:, p, :POOL_GC, :POOL_GC].set(pool_w[:, 2 * p])
        wp = wp.at[:, p, POOL_GC:, POOL_GC:].set(pool_w[:, 2 * p + 1])
    wp = wp.astype(BF16)
    ps = pool_scale.reshape(depth, 1, POOL_W)
    wr = jnp.zeros((depth, D_MODEL, LANES), F32)
    wr = wr.at[:, :, :N_EXPERTS].set(router_expert_w)
    wr = wr.at[:, :, GROUP_LANE0:GROUP_LANE0 + N_EXPERT_GROUPS].set(router_group_w)
    wr = wr.astype(BF16)
    br = jnp.zeros((depth, 1, LANES), F32)
    br = br.at[:, 0, :N_EXPERTS].set(router_expert_b)
    br = br.at[:, 0, GROUP_LANE0:GROUP_LANE0 + N_EXPERT_GROUPS].set(router_group_b)

    slopes = jnp.exp2(-8.0 * jnp.arange(1, N_HEADS + 1, dtype=F32) / N_HEADS)
    bias_p = _prompt_bias_t()
    dist_s = (WINDOW - 1) - jnp.arange(WINDOW, dtype=F32)
    bias_s = -slopes[:, None] * dist_s[None, :]
    state2d = state_pool.reshape(depth, n_s, POOL_STATE * POOL_W)
    ck_all = cache_k_win.reshape(depth, n_s, lw_s, KV_W)
    cv_all = cache_v_win.reshape(depth, n_s, lw_s, KV_W)

    xp = x_prompt.reshape(n_p * t_p, D_MODEL)
    xs = x_sample.reshape(n_s, D_MODEL)
    lw_p = min(WINDOW, t_p)
    pool_p, kp_new, vp_new, pool_s, ks_new, vs_new = [], [], [], [], [], []
    zero_cnt = jnp.zeros((N_EXPERTS, 1), F32)
    for l in range(depth):
        sinks = attn_sinks[l]
        pool_o, q, k, vt, utail, ktail, vtail = _proj_pool_prompt(
            l, xp, n_p, t_p, g_attn, w_in_b, qg, kg, bd, wp, ps)
        attn_o = _attn_prompt(q, k, vt, bias_p, sinks, n_p, t_p)
        x1p, h2p, route_p, route_tp, cnt_p = _merge_router(
            l, pool_o, attn_o, xp, w_out_b, g_ffn, wr, br, zero_cnt, TM_PROJ)
        pool_p.append(utail[:, 16 - POOL_STATE:, :])
        kp_new.append(ktail)
        vp_new.append(vtail)
        pool_so, attn_so, u_s, kc_s, vc_s = _sample_mixer(
            l, xs, g_attn, w_in_b, qg, kg, bd, wp, ps, state2d, ck_all, cv_all,
            sinks.reshape(N_HEADS, 1), bias_s, past_len)
        x1s, h2s, route_s, route_ts, cnt_all = _merge_router(
            l, pool_so, attn_so, xs, w_out_b, g_ffn, wr, br, cnt_p, n_s)
        pool_s.append(jnp.concatenate([state_pool[l][:, 1:], u_s[:, None, :]], axis=1))
        ks_new.append(kc_s)
        vs_new.append(vc_s)
        counts = cnt_all[:, 0].astype(jnp.int32)
        xp, xs = _hier_moe(l, h2p, h2s, route_p, route_s, route_tp, route_ts, counts, x1p, x1s,
                           w_gate, w_up, w_down)
    return (xp.reshape(n_p, t_p, D_MODEL), xs.reshape(n_s, t_s, D_MODEL),
            jnp.stack(pool_p),
            jnp.stack(kp_new).reshape(depth, n_p, lw_p, N_KV_HEADS, HEAD_DIM),
            jnp.stack(vp_new).reshape(depth, n_p, lw_p, N_KV_HEADS, HEAD_DIM),
            jnp.stack(pool_s),
            jnp.stack(ks_new).reshape(depth, n_s, lw_s, N_KV_HEADS, HEAD_DIM),
            jnp.stack(vs_new).reshape(depth, n_s, lw_s, N_KV_HEADS, HEAD_DIM))
```

```python
import functools

import jax
import jax.numpy as jnp
from jax import lax
from jax.experimental import pallas as pl
from jax.experimental.pallas import tpu as pltpu
from jax.experimental.pallas import tpu_sc as plsc

D_MODEL = 1024
POOL_W = 512
POOL_WINDOWS = (2, 4, 8, 16)
POOL_GC = 128
POOL_STATE = 15
HEAD_DIM = 64
N_HEADS = 8
N_KV_HEADS = 2
GQA_GROUP = 4
Q_W = 512
KV_W = 128
D_IN = POOL_W + Q_W + 2 * KV_W
WINDOW = 128
ATTN_SCALE = HEAD_DIM ** -0.5
N_EXPERT_GROUPS = 4
EXPERTS_PER_GROUP = 8
N_EXPERTS = 32
EXPERT_FF = 512
EPS = 1e-6

LANES = 128
HALO = 32
TM_PROJ = 512
ATTN_QB = 4
MOE_BM = 256
MOE_STEP_BLOCKS = 4
GROUP_LANE0 = 32
ROUTE_FIELDS = 8
SC_CORES = 2
SC_SUBCORES = 16
SC_WORKERS = SC_CORES * SC_SUBCORES
DISP_CH = 64
COMB_CH = 64
SAMPLE_CH = 32
VMEM_LIMIT = 48 * 1024 * 1024

BF16 = jnp.bfloat16
F32 = jnp.float32


def _pack_bf16_pairs(h):
    w = h.shape[1] // 2
    hi = lax.bitcast_convert_type(h[:, :w].astype(F32), jnp.uint32)
    lo = lax.bitcast_convert_type(h[:, w:].astype(F32), jnp.uint32)
    return lax.bitcast_convert_type(hi | (lo >> 16), jnp.int32)


def _unpack_bf16_pairs(words):
    u = lax.bitcast_convert_type(words, jnp.uint32)
    hi = lax.bitcast_convert_type(u & jnp.uint32(0xFFFF0000), F32)
    lo = lax.bitcast_convert_type(u << 16, F32)
    return jnp.concatenate([hi, lo], axis=-1)


def _segment_mean_sq(a, bd):
    w = a.shape[1]
    return jnp.dot((a * a).astype(BF16), bd[:w, :w], preferred_element_type=F32)


def _project(x, g, w_in, qg, kg, bd):
    ms = jnp.mean(x * x, axis=-1, keepdims=True)
    h = (x * lax.rsqrt(ms + EPS) * g).astype(BF16)
    z = jnp.dot(h, w_in, preferred_element_type=F32)
    u = z[:, :POOL_W]
    q = z[:, POOL_W:POOL_W + Q_W]
    k = z[:, POOL_W + Q_W:POOL_W + Q_W + KV_W]
    v = z[:, POOL_W + Q_W + KV_W:]
    qn = []
    for c in range(Q_W // 256):
        qc = q[:, c * 256:(c + 1) * 256]
        qn.append(qc * lax.rsqrt(_segment_mean_sq(qc, bd) + EPS))
    qn = jnp.concatenate(qn, axis=-1) * qg
    kn = k * lax.rsqrt(_segment_mean_sq(k, bd) + EPS) * kg
    return u, qn, kn, v


def _pool_project(d_groups, wp_ref, ps):
    outs = []
    for p in range(2):
        dp = jnp.concatenate([d_groups[2 * p], d_groups[2 * p + 1]], axis=-1).astype(BF16)
        y = jnp.dot(dp, wp_ref[p], preferred_element_type=F32)
        outs.append(y * ps[:, p * 256:(p + 1) * 256])
    return jnp.concatenate(outs, axis=-1)


def _proj_pool_kernel(x_ref, g_ref, win_ref, qg_ref, kg_ref, bd_ref, wp_ref, ps_ref,
                      pool_ref, q_ref, k_ref, vt_ref, utail_ref, ktail_ref, vtail_ref,
                      ext_ref, sa_ref, sb_ref, *, tm, n_j):
    j = pl.program_id(1)
    u, qn, kn, v = _project(x_ref[...], g_ref[...], win_ref[...], qg_ref[...], kg_ref[...], bd_ref[...])
    q_ref[...] = qn.astype(BF16)
    k_ref[...] = kn.astype(BF16)
    vt_ref[...] = jnp.transpose(v).astype(BF16)

    @pl.when(j == 0)
    def _():
        ext_ref[0:HALO, :] = jnp.zeros((HALO, POOL_W), F32)

    r = tm + HALO
    ext_ref[HALO:r, :] = u
    sa_ref[8:r, :] = ext_ref[8:r, :] + ext_ref[7:r - 1, :]
    sb_ref[16:r, 128:] = sa_ref[16:r, 128:] + sa_ref[14:r - 2, 128:]
    sa_ref[24:r, 256:] = sb_ref[24:r, 256:] + sb_ref[20:r - 4, 256:]
    sb_ref[32:r, 384:] = sa_ref[32:r, 384:] + sa_ref[24:r - 8, 384:]
    pos1 = j * tm + lax.broadcasted_iota(jnp.int32, (tm, POOL_GC), 0) + 1
    sums = (sa_ref, sb_ref, sa_ref, sb_ref)
    d_groups = []
    for gi, w in enumerate(POOL_WINDOWS):
        sl = slice(gi * POOL_GC, (gi + 1) * POOL_GC)
        cnt = jnp.minimum(pos1, w).astype(F32)
        d_groups.append(sums[gi][HALO:r, sl] / cnt - u[:, sl])
    pool_ref[...] = _pool_project(d_groups, wp_ref, ps_ref[...]).astype(BF16)
    ext_ref[16:HALO, :] = ext_ref[tm + 16:r, :]

    @pl.when(j == n_j - 1)
    def _():
        utail_ref[...] = u[tm - 16:, :]
        ktail_ref[...] = kn[tm - WINDOW:, :]
        vtail_ref[...] = v[tm - WINDOW:, :]


def _proj_pool_prompt(l, x2d, n_seq, seq, g_attn, w_in, qg, kg, bd, wp, ps):
    tm = TM_PROJ
    n_j = seq // tm
    t = n_seq * seq
    row = lambda b, j: (b * n_j + j, 0)
    lay = lambda b, j: (l, 0, 0)
    return pl.pallas_call(
        functools.partial(_proj_pool_kernel, tm=tm, n_j=n_j),
        grid=(n_seq, n_j),
        in_specs=[
            pl.BlockSpec((tm, D_MODEL), row),
            pl.BlockSpec((None, 1, D_MODEL), lay),
            pl.BlockSpec((None, D_MODEL, D_IN), lay),
            pl.BlockSpec((None, 1, Q_W), lay),
            pl.BlockSpec((None, 1, KV_W), lay),
            pl.BlockSpec((256, 256), lambda b, j: (0, 0)),
            pl.BlockSpec((None, 2, 256, 256), lambda b, j: (l, 0, 0, 0)),
            pl.BlockSpec((None, 1, POOL_W), lay),
        ],
        out_specs=[
            pl.BlockSpec((tm, POOL_W), row),
            pl.BlockSpec((tm, Q_W), row),
            pl.BlockSpec((tm, KV_W), row),
            pl.BlockSpec((KV_W, tm), lambda b, j: (0, b * n_j + j)),
            pl.BlockSpec((None, 16, POOL_W), lambda b, j: (b, 0, 0)),
            pl.BlockSpec((None, WINDOW, KV_W), lambda b, j: (b, 0, 0)),
            pl.BlockSpec((None, WINDOW, KV_W), lambda b, j: (b, 0, 0)),
        ],
        out_shape=[
            jax.ShapeDtypeStruct((t, POOL_W), BF16),
            jax.ShapeDtypeStruct((t, Q_W), BF16),
            jax.ShapeDtypeStruct((t, KV_W), BF16),
            jax.ShapeDtypeStruct((KV_W, t), BF16),
            jax.ShapeDtypeStruct((n_seq, 16, POOL_W), F32),
            jax.ShapeDtypeStruct((n_seq, WINDOW, KV_W), F32),
            jax.ShapeDtypeStruct((n_seq, WINDOW, KV_W), F32),
        ],
        scratch_shapes=[pltpu.VMEM((tm + HALO, POOL_W), F32)] * 3,
        compiler_params=pltpu.CompilerParams(
            dimension_semantics=("arbitrary", "arbitrary"), vmem_limit_bytes=VMEM_LIMIT),
        name="proj_pool_prompt",
    )(x2d, g_attn, w_in, qg, kg, bd, wp, ps)


def _attn_kernel(sink_ref, q_ref, kp_ref, kc_ref, vtp_ref, vtc_ref, bias_ref, o_ref):
    j = pl.program_id(1)
    kk_all = jnp.concatenate([kp_ref[...], kc_ref[...]], axis=0)
    vt_all = jnp.concatenate([vtp_ref[...], vtc_ref[...]], axis=1)
    from_prev = (lax.broadcasted_iota(jnp.int32, (WINDOW, WINDOW), 0)
                 > lax.broadcasted_iota(jnp.int32, (WINDOW, WINDOW), 1))
    for blk in range(ATTN_QB):
        q = q_ref[blk * WINDOW:(blk + 1) * WINDOW, :]
        kk = kk_all[blk * WINDOW:(blk + 2) * WINDOW, :]
        vt = vt_all[:, blk * WINDOW:(blk + 2) * WINDOW]
        variant = jnp.minimum(j, 1) if blk == 0 else 1
        outs = []
        for kv in range(N_KV_HEADS):
            heads = range(kv * GQA_GROUP, (kv + 1) * GQA_GROUP)
            q_rows = jnp.concatenate([q[:, h * HEAD_DIM:(h + 1) * HEAD_DIM] for h in heads], axis=0)
            s_all = lax.dot_general(kk[:, kv * HEAD_DIM:(kv + 1) * HEAD_DIM], q_rows,
                                    (((1,), (1,)), ((), ())), preferred_element_type=F32)
            vt_kv = vt[kv * HEAD_DIM:(kv + 1) * HEAD_DIM, :]
            for g, h in enumerate(heads):
                sg = s_all[:, g * WINDOW:(g + 1) * WINDOW]
                s = jnp.where(from_prev, sg[0:WINDOW, :], sg[WINDOW:, :]) + bias_ref[variant, h]
                sink = sink_ref[h]
                m = jnp.maximum(jnp.max(s, axis=0, keepdims=True), sink)
                p = jnp.exp(s - m)
                denom = jnp.sum(p, axis=0, keepdims=True) + jnp.exp(sink - m)
                p_keys = jnp.concatenate([jnp.where(from_prev, p, 0.0), jnp.where(from_prev, 0.0, p)], axis=0)
                o_t = jnp.dot(vt_kv, p_keys.astype(BF16), preferred_element_type=F32)
                outs.append(o_t / denom)
        o_ref[blk * WINDOW:(blk + 1) * WINDOW, :] = jnp.transpose(jnp.concatenate(outs, axis=0)).astype(BF16)


def _attn_prompt(q, k, vt, bias_t, sinks, n_seq, seq):
    tq = ATTN_QB * WINDOW
    nj = seq // tq
    t = n_seq * seq
    cur = lambda b, j: (b * nj + j, 0)
    prev = lambda b, j: (jnp.maximum((b * nj + j) * ATTN_QB - 1, 0), 0)
    cur_t = lambda b, j: (0, b * nj + j)
    prev_t = lambda b, j: (0, jnp.maximum((b * nj + j) * ATTN_QB - 1, 0))
    return pl.pallas_call(
        _attn_kernel,
        grid=(n_seq, nj),
        in_specs=[
            pl.BlockSpec(memory_space=pltpu.SMEM),
            pl.BlockSpec((tq, Q_W), cur),
            pl.BlockSpec((WINDOW, KV_W), prev),
            pl.BlockSpec((tq, KV_W), cur),
            pl.BlockSpec((KV_W, WINDOW), prev_t),
            pl.BlockSpec((KV_W, tq), cur_t),
            pl.BlockSpec((2, N_HEADS, WINDOW, WINDOW), lambda b, j: (0, 0, 0, 0)),
        ],
        out_specs=pl.BlockSpec((tq, Q_W), cur),
        out_shape=jax.ShapeDtypeStruct((t, Q_W), BF16),
        compiler_params=pltpu.CompilerParams(
            dimension_semantics=("arbitrary", "arbitrary"), vmem_limit_bytes=VMEM_LIMIT),
        name="attn_prompt",
    )(sinks, q, k, k, vt, vt, bias_t)


def _prompt_bias_t():
    r = jnp.arange(WINDOW, dtype=jnp.int32)[None, :]
    c = jnp.arange(WINDOW, dtype=jnp.int32)[:, None]
    from_prev = c > r
    dist = r - c + jnp.where(from_prev, WINDOW, 0)
    slopes = jnp.exp2(-8.0 * jnp.arange(1, N_HEADS + 1, dtype=F32) / N_HEADS)
    later = -slopes[:, None, None] * dist.astype(F32)[None]
    first = jnp.where(from_prev[None], -jnp.inf, later)
    return jnp.stack([first, later])


def _sample_kernel(x_ref, g_ref, win_ref, qg_ref, kg_ref, bd_ref, wp_ref, ps_ref,
                   st_ref, ck_ref, cv_ref, sink_ref, bias_ref, perm_ref,
                   pool_ref, attn_ref, u_ref, kc_ref, vc_ref, *, ns, pos0):
    u, qn, kn, v = _project(x_ref[...], g_ref[...], win_ref[...], qg_ref[...], kg_ref[...], bd_ref[...])
    u_ref[...] = u
    kc_ref[:, 0:WINDOW - 1, :] = ck_ref[:, 1:WINDOW, :]
    vc_ref[:, 0:WINDOW - 1, :] = cv_ref[:, 1:WINDOW, :]
    for n in range(ns):
        kc_ref[n, WINDOW - 1:WINDOW, :] = kn[n:n + 1, :]
        vc_ref[n, WINDOW - 1:WINDOW, :] = v[n:n + 1, :]

    d_groups = []
    for gi, w in enumerate(POOL_WINDOWS):
        lo = gi * POOL_GC
        acc = u[:, lo:lo + POOL_GC]
        for back in range(1, w):
            off = (POOL_STATE - back) * POOL_W + lo
            acc = acc + st_ref[:, off:off + POOL_GC]
        d_groups.append(acc / float(min(pos0 + 1, w)) - u[:, lo:lo + POOL_GC])
    pool_ref[...] = _pool_project(d_groups, wp_ref, ps_ref[...]).astype(BF16)

    zeros = jnp.zeros((ns, HEAD_DIM), F32)
    stacked = []
    for h in range(N_HEADS):
        piece = qn[:, h * HEAD_DIM:(h + 1) * HEAD_DIM]
        pair = [piece, zeros] if h < GQA_GROUP else [zeros, piece]
        stacked.append(jnp.concatenate(pair, axis=-1))
    q_hn = jnp.concatenate(stacked, axis=0).astype(BF16)
    q_nh = jnp.dot(perm_ref[0], q_hn, preferred_element_type=F32).astype(BF16)

    keys = kc_ref[...].reshape(ns * WINDOW, KV_W).astype(BF16)
    vals = vc_ref[...].reshape(ns * WINDOW, KV_W).astype(BF16)
    s_all = lax.dot_general(q_nh, keys, (((1,), (1,)), ((), ())), preferred_element_type=F32)
    sink = sink_ref[...]
    bias = bias_ref[...]
    zero_blk = jnp.zeros((N_HEADS, WINDOW), F32)
    p_rows = []
    for n in range(ns):
        s = s_all[n * N_HEADS:(n + 1) * N_HEADS, n * WINDOW:(n + 1) * WINDOW] + bias
        m = jnp.maximum(jnp.max(s, axis=-1, keepdims=True), sink)
        p = jnp.exp(s - m)
        denom = jnp.sum(p, axis=-1, keepdims=True) + jnp.exp(sink - m)
        p_rows.append(jnp.concatenate([zero_blk] * n + [p / denom] + [zero_blk] * (ns - 1 - n), axis=-1))
    p_blockdiag = jnp.concatenate(p_rows, axis=0).astype(BF16)
    o_nh = jnp.dot(p_blockdiag, vals, preferred_element_type=F32).astype(BF16)
    o_hn = jnp.dot(perm_ref[1], o_nh, preferred_element_type=F32)
    pieces = []
    for h in range(N_HEADS):
        kv = h // GQA_GROUP
        pieces.append(o_hn[h * ns:(h + 1) * ns, kv * HEAD_DIM:(kv + 1) * HEAD_DIM])
    attn_ref[...] = jnp.concatenate(pieces, axis=-1).astype(BF16)


def _sample_mixer(l, xs, g_attn, w_in, qg, kg, bd, wp, ps, state2d, ck, cv, sink8, bias_s, pos0):
    n = xs.shape[0]
    ns = 32
    row = lambda i: (i, 0)
    lay = lambda i: (l, 0, 0)
    src = jnp.arange(ns * N_HEADS)
    perm = (((src % N_HEADS) * ns + src // N_HEADS)[:, None] == src[None, :]).astype(BF16)
    perms = jnp.stack([perm, perm.T])
    return pl.pallas_call(
        functools.partial(_sample_kernel, ns=ns, pos0=pos0),
        grid=(n // ns,),
        in_specs=[
            pl.BlockSpec((ns, D_MODEL), row),
            pl.BlockSpec((None, 1, D_MODEL), lay),
            pl.BlockSpec((None, D_MODEL, D_IN), lay),
            pl.BlockSpec((None, 1, Q_W), lay),
            pl.BlockSpec((None, 1, KV_W), lay),
            pl.BlockSpec((256, 256), lambda i: (0, 0)),
            pl.BlockSpec((None, 2, 256, 256), lambda i: (l, 0, 0, 0)),
            pl.BlockSpec((None, 1, POOL_W), lay),
            pl.BlockSpec((None, ns, POOL_STATE * POOL_W), lambda i: (l, i, 0)),
            pl.BlockSpec((None, ns, WINDOW, KV_W), lambda i: (l, i, 0, 0)),
            pl.BlockSpec((None, ns, WINDOW, KV_W), lambda i: (l, i, 0, 0)),
            pl.BlockSpec((N_HEADS, 1), lambda i: (0, 0)),
            pl.BlockSpec((N_HEADS, WINDOW), lambda i: (0, 0)),
            pl.BlockSpec((2, ns * N_HEADS, ns * N_HEADS), lambda i: (0, 0, 0)),
        ],
        out_specs=[
            pl.BlockSpec((ns, POOL_W), row),
            pl.BlockSpec((ns, Q_W), row),
            pl.BlockSpec((ns, POOL_W), row),
            pl.BlockSpec((ns, WINDOW, KV_W), lambda i: (i, 0, 0)),
            pl.BlockSpec((ns, WINDOW, KV_W), lambda i: (i, 0, 0)),
        ],
        out_shape=[
            jax.ShapeDtypeStruct((n, POOL_W), BF16),
            jax.ShapeDtypeStruct((n, Q_W), BF16),
            jax.ShapeDtypeStruct((n, POOL_W), F32),
            jax.ShapeDtypeStruct((n, WINDOW, KV_W), F32),
            jax.ShapeDtypeStruct((n, WINDOW, KV_W), F32),
        ],
        compiler_params=pltpu.CompilerParams(
            dimension_semantics=("arbitrary",), vmem_limit_bytes=VMEM_LIMIT),
        name="sample_mixer",
    )(xs, g_attn, w_in, qg, kg, bd, wp, ps, state2d, ck, cv, sink8, bias_s, perms)


def _merge_router_kernel(pool_ref, attn_ref, x_ref, wout_ref, g_ref, wr_ref, br_ref, utri_ref, cin_ref,
                         x1_ref, h2_ref, route_ref, route_t_ref, cnt_ref):
    i = pl.program_id(0)

    @pl.when(i == 0)
    def _():
        cnt_ref[...] = cin_ref[...]

    y = jnp.dot(pool_ref[...], wout_ref[0:POOL_W, :], preferred_element_type=F32)
    y = y + jnp.dot(attn_ref[...], wout_ref[POOL_W:, :], preferred_element_type=F32)
    x1 = x_ref[...] + y
    x1_ref[...] = x1
    ms = jnp.mean(x1 * x1, axis=-1, keepdims=True)
    h2 = (x1 * lax.rsqrt(ms + EPS) * g_ref[...]).astype(BF16)
    h2_ref[...] = _pack_bf16_pairs(h2)
    logits = jnp.dot(h2, wr_ref[...], preferred_element_type=F32) + br_ref[...]

    tm = logits.shape[0]
    lt = jnp.transpose(logits)
    sub = lax.broadcasted_iota(jnp.int32, (EXPERTS_PER_GROUP, tm), 0)
    neg = -jnp.inf
    big = jnp.int32(EXPERTS_PER_GROUP)
    gl = jnp.where(sub < N_EXPERT_GROUPS, lt[GROUP_LANE0:GROUP_LANE0 + EXPERTS_PER_GROUP, :], neg)
    gmax = jnp.max(gl, axis=0, keepdims=True)
    grp = jnp.min(jnp.where(gl == gmax, sub, big), axis=0, keepdims=True)
    g_w = 1.0 / jnp.sum(jnp.exp(gl - gmax), axis=0, keepdims=True)
    el = lt[(N_EXPERT_GROUPS - 1) * EXPERTS_PER_GROUP:N_EXPERT_GROUPS * EXPERTS_PER_GROUP, :]
    for gi in range(N_EXPERT_GROUPS - 2, -1, -1):
        el = jnp.where(grp == gi, lt[gi * EXPERTS_PER_GROUP:(gi + 1) * EXPERTS_PER_GROUP, :], el)
    v1 = jnp.max(el, axis=0, keepdims=True)
    i1 = jnp.min(jnp.where(el == v1, sub, big), axis=0, keepdims=True)
    el2 = jnp.where(sub == i1, neg, el)
    v2 = jnp.max(el2, axis=0, keepdims=True)
    i2 = jnp.min(jnp.where(el2 == v2, sub, big), axis=0, keepdims=True)
    e21 = jnp.exp(v2 - v1)
    w1 = g_w / (1.0 + e21)
    w2 = g_w * e21 / (1.0 + e21)
    e1 = grp * EXPERTS_PER_GROUP + i1
    e2 = grp * EXPERTS_PER_GROUP + i2

    esub = lax.broadcasted_iota(jnp.int32, (N_EXPERTS, tm), 0)
    oh1 = esub == e1
    oh2 = esub == e2
    c = jnp.where(oh1 | oh2, 1.0, 0.0)
    prefix = jnp.dot(c.astype(BF16), utri_ref[...], preferred_element_type=F32) + cnt_ref[...]
    r1 = jnp.sum(jnp.where(oh1, prefix, 0.0), axis=0, keepdims=True)
    r2 = jnp.sum(jnp.where(oh2, prefix, 0.0), axis=0, keepdims=True)
    cnt_ref[...] = cnt_ref[...] + jnp.sum(c, axis=1, keepdims=True)

    fields = jnp.zeros((ROUTE_FIELDS, tm), F32)
    for idx, val in enumerate((e1.astype(F32), e2.astype(F32), w1, w2, r1, r2)):
        fields = jnp.where(sub == idx, val, fields)
    route_t_ref[...] = fields
    padded = jnp.concatenate([fields, jnp.zeros((LANES - ROUTE_FIELDS, tm), F32)], axis=0)
    route_ref[...] = jnp.transpose(padded)


def _merge_router(l, pool, attn, x2d, w_out, g_ffn, wr, br, cnt_in, tm):
    t = x2d.shape[0]
    utri = (jnp.arange(tm)[:, None] < jnp.arange(tm)[None, :]).astype(BF16)
    row = lambda i: (i, 0)
    lay = lambda i: (l, 0, 0)
    return pl.pallas_call(
        _merge_router_kernel,
        grid=(t // tm,),
        in_specs=[
            pl.BlockSpec((tm, POOL_W), row),
            pl.BlockSpec((tm, Q_W), row),
            pl.BlockSpec((tm, D_MODEL), row),
            pl.BlockSpec((None, D_MODEL, D_MODEL), lay),
            pl.BlockSpec((None, 1, D_MODEL), lay),
            pl.BlockSpec((None, D_MODEL, LANES), lay),
            pl.BlockSpec((None, 1, LANES), lay),
            pl.BlockSpec((tm, tm), lambda i: (0, 0)),
            pl.BlockSpec((N_EXPERTS, 1), lambda i: (0, 0)),
        ],
        out_specs=[
            pl.BlockSpec((tm, D_MODEL), row),
            pl.BlockSpec((tm, D_MODEL // 2), row),
            pl.BlockSpec((tm, LANES), row),
            pl.BlockSpec((ROUTE_FIELDS, tm), lambda i: (0, i)),
            pl.BlockSpec((N_EXPERTS, 1), lambda i: (0, 0)),
        ],
        out_shape=[
            jax.ShapeDtypeStruct((t, D_MODEL), F32),
            jax.ShapeDtypeStruct((t, D_MODEL // 2), jnp.int32),
            jax.ShapeDtypeStruct((t, LANES), F32),
            jax.ShapeDtypeStruct((ROUTE_FIELDS, t), F32),
            jax.ShapeDtypeStruct((N_EXPERTS, 1), F32),
        ],
        compiler_params=pltpu.CompilerParams(
            dimension_semantics=("arbitrary",), vmem_limit_bytes=VMEM_LIMIT),
        name="merge_router",
    )(pool, attn, x2d, w_out, g_ffn, wr, br, utri, cnt_in)


def _moe_kernel(be_ref, rv_ref, nx_ref, sl_ref, xd_ref, wg_hbm, wu_hbm, wd_hbm, yd_ref,
                wg_f, wu_f, wd_f, wg_s, wu_s, wd_s, sem, *, layer):
    step = pl.program_id(0)

    def weight_copies(e, s):
        return [pltpu.make_async_copy(w_hbm.at[layer, e], w_f.at[s], sem.at[s, n])
                for n, (w_hbm, w_f) in enumerate(((wg_hbm, wg_f), (wu_hbm, wu_f), (wd_hbm, wd_f)))]

    @pl.when(step == 0)
    def _():
        for c in weight_copies(be_ref[0], sl_ref[0]):
            c.start()

    for sub_blk in range(MOE_STEP_BLOCKS):
        i = step * MOE_STEP_BLOCKS + sub_blk
        rows = pl.ds(sub_blk * MOE_BM, MOE_BM)
        expert = be_ref[i]
        changed = (i == 0) | (expert != be_ref[jnp.maximum(i - 1, 0)])
        rows_valid = rv_ref[i]
        slot = sl_ref[i]

        @pl.when(changed)
        def _():
            for c in weight_copies(expert, slot):
                c.wait()

            @pl.when(nx_ref[i] >= 0)
            def _():
                for c in weight_copies(nx_ref[i], 1 - slot):
                    c.start(priority=1)

            wg_s[...] = wg_f[slot].astype(BF16)
            wu_s[...] = wu_f[slot].astype(BF16)
            wd_s[...] = wd_f[slot].astype(BF16)

        @pl.when(rows_valid > 0)
        def _():
            row = lax.broadcasted_iota(jnp.int32, (MOE_BM, D_MODEL // 2), 0)
            x = _unpack_bf16_pairs(jnp.where(row < rows_valid, xd_ref[rows, :], 0)).astype(BF16)
            gate = jnp.dot(x, wg_s[...], preferred_element_type=F32)
            up = jnp.dot(x, wu_s[...], preferred_element_type=F32)
            act = (gate * jax.nn.sigmoid(gate) * up).astype(BF16)
            y = jnp.dot(act, wd_s[...], preferred_element_type=F32)
            yd_ref[rows, :] = _pack_bf16_pairs(y.astype(BF16))

        @pl.when(rows_valid <= 0)
        def _():
            yd_ref[rows, :] = jnp.zeros((MOE_BM, D_MODEL // 2), jnp.int32)


def _moe_experts(l, block_e, rows_valid, next_e, slot, xd, w_gate, w_up, w_down):
    n_blocks = xd.shape[0] // MOE_BM
    step_rows = MOE_STEP_BLOCKS * MOE_BM
    row = lambda i, *_: (i, 0)
    return pl.pallas_call(
        functools.partial(_moe_kernel, layer=l),
        grid_spec=pltpu.PrefetchScalarGridSpec(
            num_scalar_prefetch=4,
            grid=(n_blocks // MOE_STEP_BLOCKS,),
            in_specs=[
                pl.BlockSpec((step_rows, D_MODEL // 2), row),
                pl.BlockSpec(memory_space=pl.ANY),
                pl.BlockSpec(memory_space=pl.ANY),
                pl.BlockSpec(memory_space=pl.ANY),
            ],
            out_specs=pl.BlockSpec((step_rows, D_MODEL // 2), row),
            scratch_shapes=[
                pltpu.VMEM((2, D_MODEL, EXPERT_FF), F32),
                pltpu.VMEM((2, D_MODEL, EXPERT_FF), F32),
                pltpu.VMEM((2, EXPERT_FF, D_MODEL), F32),
                pltpu.VMEM((D_MODEL, EXPERT_FF), BF16),
                pltpu.VMEM((D_MODEL, EXPERT_FF), BF16),
                pltpu.VMEM((EXPERT_FF, D_MODEL), BF16),
                pltpu.SemaphoreType.DMA((2, 3)),
            ],
        ),
        out_shape=jax.ShapeDtypeStruct((n_blocks * MOE_BM, D_MODEL // 2), jnp.int32),
        compiler_params=pltpu.CompilerParams(
            dimension_semantics=("arbitrary",), vmem_limit_bytes=VMEM_LIMIT),
        name="moe_experts",
    )(block_e, rows_valid, next_e, slot, xd, w_gate, w_up, w_down)


def _sc_worker_id():
    return lax.axis_index("s") * SC_CORES + lax.axis_index("c")


def _sc_dispatch(hp, hs, dest_p, dest_s, n_rows):
    tp, width = hp.shape
    per_w = tp // SC_WORKERS
    n_ch = per_w // DISP_CH
    n_sw = hs.shape[0] // SAMPLE_CH
    mesh = plsc.VectorSubcoreMesh(core_axis_name="c", subcore_axis_name="s")

    @functools.partial(
        pl.kernel, mesh=mesh,
        out_type=jax.ShapeDtypeStruct((n_rows, width), jnp.int32),
        scratch_types=[
            pltpu.VMEM((2, n_ch, DISP_CH), jnp.int32),
            pltpu.VMEM((2, 1, SAMPLE_CH), jnp.int32),
            pltpu.VMEM((2, DISP_CH, width), jnp.int32),
            pltpu.SemaphoreType.DMA((2,)),
            pltpu.SemaphoreType.DMA((2,)),
        ],
        name="sc_dispatch",
    )
    def k(hp_hbm, hs_hbm, dp_hbm, ds_hbm, xd_hbm, idx_v, idxs_v, bufs, rsem, wsem):
        wid = _sc_worker_id()
        base = wid * per_w
        for kk in range(2):
            pltpu.sync_copy(dp_hbm.at[kk, wid], idx_v.at[kk])
        reads = [pltpu.make_async_copy(hp_hbm.at[pl.ds(base + j * DISP_CH, DISP_CH)],
                                       bufs.at[j % 2], rsem.at[j % 2]) for j in range(n_ch)]
        reads[0].start()
        for j in range(n_ch):
            if j + 1 < n_ch:
                reads[j + 1].start()
            reads[j].wait()
            writes = [pltpu.make_async_copy(bufs.at[j % 2], xd_hbm.at[idx_v.at[kk, j]], wsem.at[kk])
                      for kk in range(2)]
            for w in writes:
                w.start()
            for w in writes:
                w.wait()

        @pl.when(wid < n_sw)
        def _():
            rows = bufs.at[0, pl.ds(0, SAMPLE_CH)]
            for kk in range(2):
                pltpu.sync_copy(ds_hbm.at[kk, wid], idxs_v.at[kk])
            pltpu.sync_copy(hs_hbm.at[pl.ds(wid * SAMPLE_CH, SAMPLE_CH)], rows)
            for kk in range(2):
                pltpu.sync_copy(rows, xd_hbm.at[idxs_v.at[kk, 0]])

    return k(hp, hs, dest_p, dest_s)


def _sc_combine_gather(yd, dest_p, dest_s, tp, ts):
    width = yd.shape[1]
    per_w = tp // SC_WORKERS
    n_ch = per_w // COMB_CH
    n_sw = ts // SAMPLE_CH
    mesh = plsc.VectorSubcoreMesh(core_axis_name="c", subcore_axis_name="s")

    @functools.partial(
        pl.kernel, mesh=mesh,
        out_type=jax.ShapeDtypeStruct((2, tp + ts, width), yd.dtype),
        scratch_types=[
            pltpu.VMEM((2, n_ch, COMB_CH), jnp.int32),
            pltpu.VMEM((2, 1, SAMPLE_CH), jnp.int32),
            pltpu.VMEM((2, COMB_CH, width), yd.dtype),
            pltpu.SemaphoreType.DMA((2,)),
            pltpu.SemaphoreType.DMA((2,)),
        ],
        name="sc_combine_gather",
    )
    def k(yd_hbm, dp_hbm, ds_hbm, g_hbm, idx_v, idxs_v, bufs, gsem, wsem):
        wid = _sc_worker_id()
        base = wid * per_w
        for kk in range(2):
            pltpu.sync_copy(dp_hbm.at[kk, wid], idx_v.at[kk])
        items = [(kk, j) for kk in range(2) for j in range(n_ch)]
        gathers = [pltpu.make_async_copy(yd_hbm.at[idx_v.at[kk, j]], bufs.at[n % 2], gsem.at[n % 2])
                   for n, (kk, j) in enumerate(items)]
        gathers[0].start()
        for n, (kk, j) in enumerate(items):
            if n + 1 < len(items):
                gathers[n + 1].start()
            gathers[n].wait()
            w = pltpu.make_async_copy(bufs.at[n % 2], g_hbm.at[kk, pl.ds(base + j * COMB_CH, COMB_CH)],
                                      wsem.at[n % 2])
            w.start()
            w.wait()

        @pl.when(wid < n_sw)
        def _():
            for kk in range(2):
                pltpu.sync_copy(ds_hbm.at[kk, wid], idxs_v.at[kk])
            for kk in range(2):
                rows = bufs.at[kk, pl.ds(0, SAMPLE_CH)]
                pltpu.sync_copy(yd_hbm.at[idxs_v.at[kk, 0]], rows)
                pltpu.sync_copy(rows, g_hbm.at[kk, pl.ds(tp + wid * SAMPLE_CH, SAMPLE_CH)])

    return k(yd, dest_p, dest_s)


def _combine_kernel(x1_ref, g_ref, route_ref, x2_ref):
    w1 = route_ref[:, 2:3]
    w2 = route_ref[:, 3:4]
    x2_ref[...] = x1_ref[...] + _unpack_bf16_pairs(g_ref[0]) * w1 + _unpack_bf16_pairs(g_ref[1]) * w2


def _combine(x1, g, route, row0, tm):
    t = x1.shape[0]
    blk0 = row0 // tm
    row = lambda i: (i, 0)
    return pl.pallas_call(
        _combine_kernel,
        grid=(t // tm,),
        in_specs=[
            pl.BlockSpec((tm, D_MODEL), row),
            pl.BlockSpec((2, tm, D_MODEL // 2), lambda i: (0, blk0 + i, 0)),
            pl.BlockSpec((tm, LANES), row),
        ],
        out_specs=pl.BlockSpec((tm, D_MODEL), row),
        out_shape=jax.ShapeDtypeStruct((t, D_MODEL), F32),
        compiler_params=pltpu.CompilerParams(
            dimension_semantics=("arbitrary",), vmem_limit_bytes=VMEM_LIMIT),
        name="combine",
    )(x1, g, route)


def _dest_layout(dest, workers, chunk):
    t = dest.shape[1]
    return dest.reshape(2, workers, t // (workers * chunk), chunk)


def _hier_moe(l, h2p, h2s, route_p, route_s, route_tp, route_ts, counts, x1p, x1s, w_gate, w_up, w_down):
    tp, ts = h2p.shape[0], h2s.shape[0]
    n_assign = 2 * (tp + ts)
    n_blocks = -(-n_assign // MOE_BM) + N_EXPERTS
    n_blocks = -(-n_blocks // MOE_STEP_BLOCKS) * MOE_STEP_BLOCKS
    pcounts = (counts + MOE_BM - 1) // MOE_BM * MOE_BM
    pends = jnp.cumsum(pcounts)
    poffsets = pends - pcounts
    starts = jnp.arange(n_blocks, dtype=jnp.int32) * MOE_BM
    block_e = jnp.minimum(jnp.sum((pends[None, :] <= starts[:, None]).astype(jnp.int32), axis=1),
                          N_EXPERTS - 1)
    experts = jnp.arange(N_EXPERTS, dtype=jnp.int32)

    def lookup(table, idx):
        return jnp.sum(jnp.where(idx[..., None] == experts, table, 0), axis=-1)

    rows_valid = jnp.clip(lookup(poffsets + counts, block_e) - starts, 0, MOE_BM).astype(jnp.int32)
    used = counts > 0
    last_e = jnp.max(jnp.where(used, jnp.arange(N_EXPERTS, dtype=jnp.int32), 0))
    block_e = jnp.where(rows_valid > 0, block_e, last_e).astype(jnp.int32)
    later = used[None, :] & (jnp.arange(N_EXPERTS)[None, :] > jnp.arange(N_EXPERTS)[:, None])
    next_used = jnp.min(jnp.where(later, jnp.arange(N_EXPERTS, dtype=jnp.int32)[None, :], N_EXPERTS), axis=1)
    next_used = jnp.where(next_used >= N_EXPERTS, -1, next_used).astype(jnp.int32)
    slot_of = ((jnp.cumsum(used.astype(jnp.int32)) - 1) & 1).astype(jnp.int32)
    next_e = lookup(next_used, block_e)
    slot = lookup(slot_of, block_e)

    def dest_of(route_t):
        return lookup(poffsets, route_t[0:2].astype(jnp.int32)) + route_t[4:6].astype(jnp.int32)

    dest_p, dest_s = dest_of(route_tp), dest_of(route_ts)
    n_sw = ts // SAMPLE_CH
    xd = _sc_dispatch(h2p, h2s, _dest_layout(dest_p, SC_WORKERS, DISP_CH),
                      _dest_layout(dest_s, n_sw, SAMPLE_CH), n_blocks * MOE_BM)
    yd = _moe_experts(l, block_e, rows_valid, next_e, slot, xd, w_gate, w_up, w_down)
    g = _sc_combine_gather(yd, _dest_layout(dest_p, SC_WORKERS, COMB_CH),
                           _dest_layout(dest_s, n_sw, SAMPLE_CH), tp, ts)
    return _combine(x1p, g, route_p, 0, TM_PROJ), _combine(x1s, g, route_s, tp, ts)


def kernel(x_prompt, x_sample, state_pool, cache_k_win, cache_v_win, norm_attn_g, w_in, pool_w, pool_scale, q_norm_g, k_norm_g, attn_sinks, w_out, norm_ffn_g, router_group_w, router_group_b, router_expert_w, router_expert_b, w_gate, w_up, w_down):
    n_p, t_p, d = x_prompt.shape
    n_s, t_s, _ = x_sample.shape
    depth = w_in.shape[0]
    lw_s = cache_k_win.shape[2]
    assert t_s == 1 and lw_s == WINDOW and d == D_MODEL
    assert t_p % TM_PROJ == 0 and t_p >= WINDOW
    past_len = 16384

    w_in_b = w_in.astype(BF16)
    w_out_b = w_out.astype(BF16)
    g_attn = norm_attn_g.reshape(depth, 1, D_MODEL)
    g_ffn = norm_ffn_g.reshape(depth, 1, D_MODEL)
    qg = (jnp.tile(q_norm_g, (1, N_HEADS)) * ATTN_SCALE).reshape(depth, 1, Q_W)
    kg = jnp.tile(k_norm_g, (1, N_KV_HEADS)).reshape(depth, 1, KV_W)
    seg = jnp.arange(256) // HEAD_DIM
    bd = jnp.where(seg[:, None] == seg[None, :], 1.0 / HEAD_DIM, 0.0).astype(BF16)
    wp = jnp.zeros((depth, 2, 256, 256), F32)
    for p in range(2):
        wp = wp.at[:, p, :POOL_GC, :POOL_GC].set(pool_w[:, 2 * p])
        wp = wp.at[:, p, POOL_GC:, POOL_GC:].set(pool_w[:, 2 * p + 1])
    wp = wp.astype(BF16)
    ps = pool_scale.reshape(depth, 1, POOL_W)
    wr = jnp.zeros((depth, D_MODEL, LANES), F32)
    wr = wr.at[:, :, :N_EXPERTS].set(router_expert_w)
    wr = wr.at[:, :, GROUP_LANE0:GROUP_LANE0 + N_EXPERT_GROUPS].set(router_group_w)
    wr = wr.astype(BF16)
    br = jnp.zeros((depth, 1, LANES), F32)
    br = br.at[:, 0, :N_EXPERTS].set(router_expert_b)
    br = br.at[:, 0, GROUP_LANE0:GROUP_LANE0 + N_EXPERT_GROUPS].set(router_group_b)

    slopes = jnp.exp2(-8.0 * jnp.arange(1, N_HEADS + 1, dtype=F32) / N_HEADS)
    bias_p = _prompt_bias_t()
    dist_s = (WINDOW - 1) - jnp.arange(WINDOW, dtype=F32)
    bias_s = -slopes[:, None] * dist_s[None, :]
    state2d = state_pool.reshape(depth, n_s, POOL_STATE * POOL_W)
    ck_all = cache_k_win.reshape(depth, n_s, lw_s, KV_W)
    cv_all = cache_v_win.reshape(depth, n_s, lw_s, KV_W)

    xp = x_prompt.reshape(n_p * t_p, D_MODEL)
    xs = x_sample.reshape(n_s, D_MODEL)
    lw_p = min(WINDOW, t_p)
    pool_p, kp_new, vp_new, pool_s, ks_new, vs_new = [], [], [], [], [], []
    zero_cnt = jnp.zeros((N_EXPERTS, 1), F32)
    for l in range(depth):
        sinks = attn_sinks[l]
        pool_o, q, k, vt, utail, ktail, vtail = _proj_pool_prompt(
            l, xp, n_p, t_p, g_attn, w_in_b, qg, kg, bd, wp, ps)
        attn_o = _attn_prompt(q, k, vt, bias_p, sinks, n_p, t_p)
        x1p, h2p, route_p, route_tp, cnt_p = _merge_router(
            l, pool_o, attn_o, xp, w_out_b, g_ffn, wr, br, zero_cnt, TM_PROJ)
        pool_p.append(utail[:, 16 - POOL_STATE:, :])
        kp_new.append(ktail)
        vp_new.append(vtail)
        pool_so, attn_so, u_s, kc_s, vc_s = _sample_mixer(
            l, xs, g_attn, w_in_b, qg, kg, bd, wp, ps, state2d, ck_all, cv_all,
            sinks.reshape(N_HEADS, 1), bias_s, past_len)
        x1s, h2s, route_s, route_ts, cnt_all = _merge_router(
            l, pool_so, attn_so, xs, w_out_b, g_ffn, wr, br, cnt_p, n_s)
        pool_s.append(jnp.concatenate([state_pool[l][:, 1:], u_s[:, None, :]], axis=1))
        ks_new.append(kc_s)
        vs_new.append(vc_s)
        counts = cnt_all[:, 0].astype(jnp.int32)
        xp, xs = _hier_moe(l, h2p, h2s, route_p, route_s, route_tp, route_ts, counts, x1p, x1s,
                           w_gate, w_up, w_down)
    return (xp.reshape(n_p, t_p, D_MODEL), xs.reshape(n_s, t_s, D_MODEL),
            jnp.stack(pool_p),
            jnp.stack(kp_new).reshape(depth, n_p, lw_p, N_KV_HEADS, HEAD_DIM),
            jnp.stack(vp_new).reshape(depth, n_p, lw_p, N_KV_HEADS, HEAD_DIM),
            jnp.stack(pool_s),
            jnp.stack(ks_new).reshape(depth, n_s, lw_s, N_KV_HEADS, HEAD_DIM),
            jnp.stack(vs_new).reshape(depth, n_s, lw_s, N_KV_HEADS, HEAD_DIM))
```

```python
import functools

import jax
import jax.numpy as jnp
from jax import lax
from jax.experimental import pallas as pl
from jax.experimental.pallas import tpu as pltpu
from jax.experimental.pallas import tpu_sc as plsc

D_MODEL = 1024
POOL_W = 512
POOL_WINDOWS = (2, 4, 8, 16)
POOL_GC = 128
POOL_STATE = 15
HEAD_DIM = 64
N_HEADS = 8
N_KV_HEADS = 2
GQA_GROUP = 4
Q_W = 512
KV_W = 128
D_IN = POOL_W + Q_W + 2 * KV_W
WINDOW = 128
ATTN_SCALE = HEAD_DIM ** -0.5
N_EXPERT_GROUPS = 4
EXPERTS_PER_GROUP = 8
N_EXPERTS = 32
EXPERT_FF = 512
EPS = 1e-6

LANES = 128
HALO = 32
TM_PROJ = 512
ATTN_QB = 4
MOE_BM = 256
MOE_STEP_BLOCKS = 4
GROUP_LANE0 = 32
ROUTE_FIELDS = 8
SC_CORES = 2
SC_SUBCORES = 16
SC_WORKERS = SC_CORES * SC_SUBCORES
DISP_CH = 64
COMB_CH = 64
SAMPLE_CH = 32
VMEM_LIMIT = 48 * 1024 * 1024

BF16 = jnp.bfloat16
F32 = jnp.float32


def _pack_bf16_pairs(h):
    w = h.shape[1] // 2
    hi = lax.bitcast_convert_type(h[:, :w].astype(F32), jnp.uint32)
    lo = lax.bitcast_convert_type(h[:, w:].astype(F32), jnp.uint32)
    return lax.bitcast_convert_type(hi | (lo >> 16), jnp.int32)


def _unpack_bf16_pairs(words):
    u = lax.bitcast_convert_type(words, jnp.uint32)
    hi = lax.bitcast_convert_type(u & jnp.uint32(0xFFFF0000), F32)
    lo = lax.bitcast_convert_type(u << 16, F32)
    return jnp.concatenate([hi, lo], axis=-1)


def _segment_mean_sq(a, bd):
    w = a.shape[1]
    return jnp.dot((a * a).astype(BF16), bd[:w, :w], preferred_element_type=F32)


def _project(x, g, w_in, qg, kg, bd):
    ms = jnp.mean(x * x, axis=-1, keepdims=True)
    h = (x * lax.rsqrt(ms + EPS) * g).astype(BF16)
    z = jnp.dot(h, w_in, preferred_element_type=F32)
    u = z[:, :POOL_W]
    q = z[:, POOL_W:POOL_W + Q_W]
    k = z[:, POOL_W + Q_W:POOL_W + Q_W + KV_W]
    v = z[:, POOL_W + Q_W + KV_W:]
    qn = []
    for c in range(Q_W // 256):
        qc = q[:, c * 256:(c + 1) * 256]
        qn.append(qc * lax.rsqrt(_segment_mean_sq(qc, bd) + EPS))
    qn = jnp.concatenate(qn, axis=-1) * qg
    kn = k * lax.rsqrt(_segment_mean_sq(k, bd) + EPS) * kg
    return u, qn, kn, v


def _pool_project(d_groups, wp_ref, ps):
    outs = []
    for p in range(2):
        dp = jnp.concatenate([d_groups[2 * p], d_groups[2 * p + 1]], axis=-1).astype(BF16)
        y = jnp.dot(dp, wp_ref[p], preferred_element_type=F32)
        outs.append(y * ps[:, p * 256:(p + 1) * 256])
    return jnp.concatenate(outs, axis=-1)


def _proj_pool_kernel(x_ref, g_ref, win_ref, qg_ref, kg_ref, bd_ref, wp_ref, ps_ref,
                      pool_ref, q_ref, k_ref, vt_ref, utail_ref, ktail_ref, vtail_ref,
                      ext_ref, sa_ref, sb_ref, *, tm, n_j):
    j = pl.program_id(1)
    u, qn, kn, v = _project(x_ref[...], g_ref[...], win_ref[...], qg_ref[...], kg_ref[...], bd_ref[...])
    q_ref[...] = qn.astype(BF16)
    k_ref[...] = kn.astype(BF16)
    vt_ref[...] = jnp.transpose(v).astype(BF16)

    @pl.when(j == 0)
    def _():
        ext_ref[0:HALO, :] = jnp.zeros((HALO, POOL_W), F32)

    r = tm + HALO
    ext_ref[HALO:r, :] = u
    sa_ref[8:r, :] = ext_ref[8:r, :] + ext_ref[7:r - 1, :]
    sb_ref[16:r, 128:] = sa_ref[16:r, 128:] + sa_ref[14:r - 2, 128:]
    sa_ref[24:r, 256:] = sb_ref[24:r, 256:] + sb_ref[20:r - 4, 256:]
    sb_ref[32:r, 384:] = sa_ref[32:r, 384:] + sa_ref[24:r - 8, 384:]
    pos1 = j * tm + lax.broadcasted_iota(jnp.int32, (tm, POOL_GC), 0) + 1
    sums = (sa_ref, sb_ref, sa_ref, sb_ref)
    d_groups = []
    for gi, w in enumerate(POOL_WINDOWS):
        sl = slice(gi * POOL_GC, (gi + 1) * POOL_GC)
        cnt = jnp.minimum(pos1, w).astype(F32)
        d_groups.append(sums[gi][HALO:r, sl] / cnt - u[:, sl])
    pool_ref[...] = _pool_project(d_groups, wp_ref, ps_ref[...]).astype(BF16)
    ext_ref[16:HALO, :] = ext_ref[tm + 16:r, :]

    @pl.when(j == n_j - 1)
    def _():
        utail_ref[...] = u[tm - 16:, :]
        ktail_ref[...] = kn[tm - WINDOW:, :]
        vtail_ref[...] = v[tm - WINDOW:, :]


def _proj_pool_prompt(l, x2d, seq0, n_seq, seq, g_attn, w_in, qg, kg, bd, wp, ps):
    tm = TM_PROJ
    n_j = seq // tm
    t = n_seq * seq
    row = lambda b, j: (b * n_j + j, 0)
    lay = lambda b, j: (l, 0, 0)
    return pl.pallas_call(
        functools.partial(_proj_pool_kernel, tm=tm, n_j=n_j),
        grid=(n_seq, n_j),
        in_specs=[
            pl.BlockSpec((tm, D_MODEL), lambda b, j: ((seq0 + b) * n_j + j, 0)),
            pl.BlockSpec((None, 1, D_MODEL), lay),
            pl.BlockSpec((None, D_MODEL, D_IN), lay),
            pl.BlockSpec((None, 1, Q_W), lay),
            pl.BlockSpec((None, 1, KV_W), lay),
            pl.BlockSpec((256, 256), lambda b, j: (0, 0)),
            pl.BlockSpec((None, 2, 256, 256), lambda b, j: (l, 0, 0, 0)),
            pl.BlockSpec((None, 1, POOL_W), lay),
        ],
        out_specs=[
            pl.BlockSpec((tm, POOL_W), row),
            pl.BlockSpec((tm, Q_W), row),
            pl.BlockSpec((tm, KV_W), row),
            pl.BlockSpec((KV_W, tm), lambda b, j: (0, b * n_j + j)),
            pl.BlockSpec((None, 16, POOL_W), lambda b, j: (b, 0, 0)),
            pl.BlockSpec((None, WINDOW, KV_W), lambda b, j: (b, 0, 0)),
            pl.BlockSpec((None, WINDOW, KV_W), lambda b, j: (b, 0, 0)),
        ],
        out_shape=[
            jax.ShapeDtypeStruct((t, POOL_W), BF16),
            jax.ShapeDtypeStruct((t, Q_W), BF16),
            jax.ShapeDtypeStruct((t, KV_W), BF16),
            jax.ShapeDtypeStruct((KV_W, t), BF16),
            jax.ShapeDtypeStruct((n_seq, 16, POOL_W), F32),
            jax.ShapeDtypeStruct((n_seq, WINDOW, KV_W), F32),
            jax.ShapeDtypeStruct((n_seq, WINDOW, KV_W), F32),
        ],
        scratch_shapes=[pltpu.VMEM((tm + HALO, POOL_W), F32)] * 3,
        compiler_params=pltpu.CompilerParams(
            dimension_semantics=("arbitrary", "arbitrary"), vmem_limit_bytes=VMEM_LIMIT),
        name="proj_pool_prompt",
    )(x2d, g_attn, w_in, qg, kg, bd, wp, ps)


def _attn_kernel(sink_ref, q_ref, kp_ref, kc_ref, vtp_ref, vtc_ref, bias_ref, o_ref):
    j = pl.program_id(1)
    kk_all = jnp.concatenate([kp_ref[...], kc_ref[...]], axis=0)
    vt_all = jnp.concatenate([vtp_ref[...], vtc_ref[...]], axis=1)
    from_prev = (lax.broadcasted_iota(jnp.int32, (WINDOW, WINDOW), 0)
                 > lax.broadcasted_iota(jnp.int32, (WINDOW, WINDOW), 1))
    for blk in range(ATTN_QB):
        q = q_ref[blk * WINDOW:(blk + 1) * WINDOW, :]
        kk = kk_all[blk * WINDOW:(blk + 2) * WINDOW, :]
        vt = vt_all[:, blk * WINDOW:(blk + 2) * WINDOW]
        variant = jnp.minimum(j, 1) if blk == 0 else 1
        outs = []
        for kv in range(N_KV_HEADS):
            heads = range(kv * GQA_GROUP, (kv + 1) * GQA_GROUP)
            q_rows = jnp.concatenate([q[:, h * HEAD_DIM:(h + 1) * HEAD_DIM] for h in heads], axis=0)
            s_all = lax.dot_general(kk[:, kv * HEAD_DIM:(kv + 1) * HEAD_DIM], q_rows,
                                    (((1,), (1,)), ((), ())), preferred_element_type=F32)
            vt_kv = vt[kv * HEAD_DIM:(kv + 1) * HEAD_DIM, :]
            for g, h in enumerate(heads):
                sg = s_all[:, g * WINDOW:(g + 1) * WINDOW]
                s = jnp.where(from_prev, sg[0:WINDOW, :], sg[WINDOW:, :]) + bias_ref[variant, h]
                sink = sink_ref[h]
                m = jnp.maximum(jnp.max(s, axis=0, keepdims=True), sink)
                p = jnp.exp(s - m)
                denom = jnp.sum(p, axis=0, keepdims=True) + jnp.exp(sink - m)
                p_keys = jnp.concatenate([jnp.where(from_prev, p, 0.0), jnp.where(from_prev, 0.0, p)], axis=0)
                o_t = jnp.dot(vt_kv, p_keys.astype(BF16), preferred_element_type=F32)
                outs.append(o_t / denom)
        o_ref[blk * WINDOW:(blk + 1) * WINDOW, :] = jnp.transpose(jnp.concatenate(outs, axis=0)).astype(BF16)


def _attn_prompt(q, k, vt, bias_t, sinks, n_seq, seq):
    tq = ATTN_QB * WINDOW
    nj = seq // tq
    t = n_seq * seq
    cur = lambda b, j: (b * nj + j, 0)
    prev = lambda b, j: (jnp.maximum((b * nj + j) * ATTN_QB - 1, 0), 0)
    cur_t = lambda b, j: (0, b * nj + j)
    prev_t = lambda b, j: (0, jnp.maximum((b * nj + j) * ATTN_QB - 1, 0))
    return pl.pallas_call(
        _attn_kernel,
        grid=(n_seq, nj),
        in_specs=[
            pl.BlockSpec(memory_space=pltpu.SMEM),
            pl.BlockSpec((tq, Q_W), cur),
            pl.BlockSpec((WINDOW, KV_W), prev),
            pl.BlockSpec((tq, KV_W), cur),
            pl.BlockSpec((KV_W, WINDOW), prev_t),
            pl.BlockSpec((KV_W, tq), cur_t),
            pl.BlockSpec((2, N_HEADS, WINDOW, WINDOW), lambda b, j: (0, 0, 0, 0)),
        ],
        out_specs=pl.BlockSpec((tq, Q_W), cur),
        out_shape=jax.ShapeDtypeStruct((t, Q_W), BF16),
        compiler_params=pltpu.CompilerParams(
            dimension_semantics=("arbitrary", "arbitrary"), vmem_limit_bytes=VMEM_LIMIT),
        name="attn_prompt",
    )(sinks, q, k, k, vt, vt, bias_t)


def _prompt_bias_t():
    r = jnp.arange(WINDOW, dtype=jnp.int32)[None, :]
    c = jnp.arange(WINDOW, dtype=jnp.int32)[:, None]
    from_prev = c > r
    dist = r - c + jnp.where(from_prev, WINDOW, 0)
    slopes = jnp.exp2(-8.0 * jnp.arange(1, N_HEADS + 1, dtype=F32) / N_HEADS)
    later = -slopes[:, None, None] * dist.astype(F32)[None]
    first = jnp.where(from_prev[None], -jnp.inf, later)
    return jnp.stack([first, later])


def _sample_kernel(x_ref, g_ref, win_ref, qg_ref, kg_ref, bd_ref, wp_ref, ps_ref,
                   st_ref, ck_ref, cv_ref, sink_ref, bias_ref, perm_ref,
                   pool_ref, attn_ref, u_ref, kc_ref, vc_ref, *, ns, pos0):
    u, qn, kn, v = _project(x_ref[...], g_ref[...], win_ref[...], qg_ref[...], kg_ref[...], bd_ref[...])
    u_ref[...] = u
    kc_ref[:, 0:WINDOW - 1, :] = ck_ref[:, 1:WINDOW, :]
    vc_ref[:, 0:WINDOW - 1, :] = cv_ref[:, 1:WINDOW, :]
    for n in range(ns):
        kc_ref[n, WINDOW - 1:WINDOW, :] = kn[n:n + 1, :]
        vc_ref[n, WINDOW - 1:WINDOW, :] = v[n:n + 1, :]

    d_groups = []
    for gi, w in enumerate(POOL_WINDOWS):
        lo = gi * POOL_GC
        acc = u[:, lo:lo + POOL_GC]
        for back in range(1, w):
            off = (POOL_STATE - back) * POOL_W + lo
            acc = acc + st_ref[:, off:off + POOL_GC]
        d_groups.append(acc / float(min(pos0 + 1, w)) - u[:, lo:lo + POOL_GC])
    pool_ref[...] = _pool_project(d_groups, wp_ref, ps_ref[...]).astype(BF16)

    zeros = jnp.zeros((ns, HEAD_DIM), F32)
    stacked = []
    for h in range(N_HEADS):
        piece = qn[:, h * HEAD_DIM:(h + 1) * HEAD_DIM]
        pair = [piece, zeros] if h < GQA_GROUP else [zeros, piece]
        stacked.append(jnp.concatenate(pair, axis=-1))
    q_hn = jnp.concatenate(stacked, axis=0).astype(BF16)
    q_nh = jnp.dot(perm_ref[0], q_hn, preferred_element_type=F32).astype(BF16)

    keys = kc_ref[...].reshape(ns * WINDOW, KV_W).astype(BF16)
    vals = vc_ref[...].reshape(ns * WINDOW, KV_W).astype(BF16)
    s_all = lax.dot_general(q_nh, keys, (((1,), (1,)), ((), ())), preferred_element_type=F32)
    sink = sink_ref[...]
    bias = bias_ref[...]
    zero_blk = jnp.zeros((N_HEADS, WINDOW), F32)
    p_rows = []
    for n in range(ns):
        s = s_all[n * N_HEADS:(n + 1) * N_HEADS, n * WINDOW:(n + 1) * WINDOW] + bias
        m = jnp.maximum(jnp.max(s, axis=-1, keepdims=True), sink)
        p = jnp.exp(s - m)
        denom = jnp.sum(p, axis=-1, keepdims=True) + jnp.exp(sink - m)
        p_rows.append(jnp.concatenate([zero_blk] * n + [p / denom] + [zero_blk] * (ns - 1 - n), axis=-1))
    p_blockdiag = jnp.concatenate(p_rows, axis=0).astype(BF16)
    o_nh = jnp.dot(p_blockdiag, vals, preferred_element_type=F32).astype(BF16)
    o_hn = jnp.dot(perm_ref[1], o_nh, preferred_element_type=F32)
    pieces = []
    for h in range(N_HEADS):
        kv = h // GQA_GROUP
        pieces.append(o_hn[h * ns:(h + 1) * ns, kv * HEAD_DIM:(kv + 1) * HEAD_DIM])
    attn_ref[...] = jnp.concatenate(pieces, axis=-1).astype(BF16)


def _sample_mixer(l, xs, g_attn, w_in, qg, kg, bd, wp, ps, state2d, ck, cv, sink8, bias_s, pos0):
    n = xs.shape[0]
    ns = 32
    row = lambda i: (i, 0)
    lay = lambda i: (l, 0, 0)
    src = jnp.arange(ns * N_HEADS)
    perm = (((src % N_HEADS) * ns + src // N_HEADS)[:, None] == src[None, :]).astype(BF16)
    perms = jnp.stack([perm, perm.T])
    return pl.pallas_call(
        functools.partial(_sample_kernel, ns=ns, pos0=pos0),
        grid=(n // ns,),
        in_specs=[
            pl.BlockSpec((ns, D_MODEL), row),
            pl.BlockSpec((None, 1, D_MODEL), lay),
            pl.BlockSpec((None, D_MODEL, D_IN), lay),
            pl.BlockSpec((None, 1, Q_W), lay),
            pl.BlockSpec((None, 1, KV_W), lay),
            pl.BlockSpec((256, 256), lambda i: (0, 0)),
            pl.BlockSpec((None, 2, 256, 256), lambda i: (l, 0, 0, 0)),
            pl.BlockSpec((None, 1, POOL_W), lay),
            pl.BlockSpec((None, ns, POOL_STATE * POOL_W), lambda i: (l, i, 0)),
            pl.BlockSpec((None, ns, WINDOW, KV_W), lambda i: (l, i, 0, 0)),
            pl.BlockSpec((None, ns, WINDOW, KV_W), lambda i: (l, i, 0, 0)),
            pl.BlockSpec((N_HEADS, 1), lambda i: (0, 0)),
            pl.BlockSpec((N_HEADS, WINDOW), lambda i: (0, 0)),
            pl.BlockSpec((2, ns * N_HEADS, ns * N_HEADS), lambda i: (0, 0, 0)),
        ],
        out_specs=[
            pl.BlockSpec((ns, POOL_W), row),
            pl.BlockSpec((ns, Q_W), row),
            pl.BlockSpec((ns, POOL_W), row),
            pl.BlockSpec((ns, WINDOW, KV_W), lambda i: (i, 0, 0)),
            pl.BlockSpec((ns, WINDOW, KV_W), lambda i: (i, 0, 0)),
        ],
        out_shape=[
            jax.ShapeDtypeStruct((n, POOL_W), BF16),
            jax.ShapeDtypeStruct((n, Q_W), BF16),
            jax.ShapeDtypeStruct((n, POOL_W), F32),
            jax.ShapeDtypeStruct((n, WINDOW, KV_W), F32),
            jax.ShapeDtypeStruct((n, WINDOW, KV_W), F32),
        ],
        compiler_params=pltpu.CompilerParams(
            dimension_semantics=("arbitrary",), vmem_limit_bytes=VMEM_LIMIT),
        name="sample_mixer",
    )(xs, g_attn, w_in, qg, kg, bd, wp, ps, state2d, ck, cv, sink8, bias_s, perms)


def _merge_router_kernel(pool_ref, attn_ref, x_ref, wout_ref, g_ref, wr_ref, br_ref, utri_ref, cin_ref,
                         x1_ref, h2_ref, route_ref, route_t_ref, cnt_ref):
    i = pl.program_id(0)

    @pl.when(i == 0)
    def _():
        cnt_ref[...] = cin_ref[...]

    y = jnp.dot(pool_ref[...], wout_ref[0:POOL_W, :], preferred_element_type=F32)
    y = y + jnp.dot(attn_ref[...], wout_ref[POOL_W:, :], preferred_element_type=F32)
    x1 = x_ref[...] + y
    x1_ref[...] = x1
    ms = jnp.mean(x1 * x1, axis=-1, keepdims=True)
    h2 = (x1 * lax.rsqrt(ms + EPS) * g_ref[...]).astype(BF16)
    h2_ref[...] = _pack_bf16_pairs(h2)
    logits = jnp.dot(h2, wr_ref[...], preferred_element_type=F32) + br_ref[...]

    tm = logits.shape[0]
    lt = jnp.transpose(logits)
    sub = lax.broadcasted_iota(jnp.int32, (EXPERTS_PER_GROUP, tm), 0)
    neg = -jnp.inf
    big = jnp.int32(EXPERTS_PER_GROUP)
    gl = jnp.where(sub < N_EXPERT_GROUPS, lt[GROUP_LANE0:GROUP_LANE0 + EXPERTS_PER_GROUP, :], neg)
    gmax = jnp.max(gl, axis=0, keepdims=True)
    grp = jnp.min(jnp.where(gl == gmax, sub, big), axis=0, keepdims=True)
    g_w = 1.0 / jnp.sum(jnp.exp(gl - gmax), axis=0, keepdims=True)
    el = lt[(N_EXPERT_GROUPS - 1) * EXPERTS_PER_GROUP:N_EXPERT_GROUPS * EXPERTS_PER_GROUP, :]
    for gi in range(N_EXPERT_GROUPS - 2, -1, -1):
        el = jnp.where(grp == gi, lt[gi * EXPERTS_PER_GROUP:(gi + 1) * EXPERTS_PER_GROUP, :], el)
    v1 = jnp.max(el, axis=0, keepdims=True)
    i1 = jnp.min(jnp.where(el == v1, sub, big), axis=0, keepdims=True)
    el2 = jnp.where(sub == i1, neg, el)
    v2 = jnp.max(el2, axis=0, keepdims=True)
    i2 = jnp.min(jnp.where(el2 == v2, sub, big), axis=0, keepdims=True)
    e21 = jnp.exp(v2 - v1)
    w1 = g_w / (1.0 + e21)
    w2 = g_w * e21 / (1.0 + e21)
    e1 = grp * EXPERTS_PER_GROUP + i1
    e2 = grp * EXPERTS_PER_GROUP + i2

    esub = lax.broadcasted_iota(jnp.int32, (N_EXPERTS, tm), 0)
    oh1 = esub == e1
    oh2 = esub == e2
    c = jnp.where(oh1 | oh2, 1.0, 0.0)
    prefix = jnp.dot(c.astype(BF16), utri_ref[...], preferred_element_type=F32) + cnt_ref[...]
    r1 = jnp.sum(jnp.where(oh1, prefix, 0.0), axis=0, keepdims=True)
    r2 = jnp.sum(jnp.where(oh2, prefix, 0.0), axis=0, keepdims=True)
    cnt_ref[...] = cnt_ref[...] + jnp.sum(c, axis=1, keepdims=True)

    fields = jnp.zeros((ROUTE_FIELDS, tm), F32)
    for idx, val in enumerate((e1.astype(F32), e2.astype(F32), w1, w2, r1, r2)):
        fields = jnp.where(sub == idx, val, fields)
    route_t_ref[...] = fields
    padded = jnp.concatenate([fields, jnp.zeros((LANES - ROUTE_FIELDS, tm), F32)], axis=0)
    route_ref[...] = jnp.transpose(padded)


def _merge_router(l, pool, attn, x2d, x_row0, w_out, g_ffn, wr, br, cnt_in, tm):
    t = pool.shape[0]
    x_blk0 = x_row0 // tm
    utri = (jnp.arange(tm)[:, None] < jnp.arange(tm)[None, :]).astype(BF16)
    row = lambda i: (i, 0)
    lay = lambda i: (l, 0, 0)
    return pl.pallas_call(
        _merge_router_kernel,
        grid=(t // tm,),
        in_specs=[
            pl.BlockSpec((tm, POOL_W), row),
            pl.BlockSpec((tm, Q_W), row),
            pl.BlockSpec((tm, D_MODEL), lambda i: (x_blk0 + i, 0)),
            pl.BlockSpec((None, D_MODEL, D_MODEL), lay),
            pl.BlockSpec((None, 1, D_MODEL), lay),
            pl.BlockSpec((None, D_MODEL, LANES), lay),
            pl.BlockSpec((None, 1, LANES), lay),
            pl.BlockSpec((tm, tm), lambda i: (0, 0)),
            pl.BlockSpec((N_EXPERTS, 1), lambda i: (0, 0)),
        ],
        out_specs=[
            pl.BlockSpec((tm, D_MODEL), row),
            pl.BlockSpec((tm, D_MODEL // 2), row),
            pl.BlockSpec((tm, LANES), row),
            pl.BlockSpec((ROUTE_FIELDS, tm), lambda i: (0, i)),
            pl.BlockSpec((N_EXPERTS, 1), lambda i: (0, 0)),
        ],
        out_shape=[
            jax.ShapeDtypeStruct((t, D_MODEL), F32),
            jax.ShapeDtypeStruct((t, D_MODEL // 2), jnp.int32),
            jax.ShapeDtypeStruct((t, LANES), F32),
            jax.ShapeDtypeStruct((ROUTE_FIELDS, t), F32),
            jax.ShapeDtypeStruct((N_EXPERTS, 1), F32),
        ],
        compiler_params=pltpu.CompilerParams(
            dimension_semantics=("arbitrary",), vmem_limit_bytes=VMEM_LIMIT),
        name="merge_router",
    )(pool, attn, x2d, w_out, g_ffn, wr, br, utri, cnt_in)


def _moe_kernel(be_ref, rv_ref, nx_ref, sl_ref, xd_ref, wg_hbm, wu_hbm, wd_hbm, yd_ref,
                wg_f, wu_f, wd_f, wg_s, wu_s, wd_s, sem, *, layer):
    step = pl.program_id(0)

    def weight_copies(e, s):
        return [pltpu.make_async_copy(w_hbm.at[layer, e], w_f.at[s], sem.at[s, n])
                for n, (w_hbm, w_f) in enumerate(((wg_hbm, wg_f), (wu_hbm, wu_f), (wd_hbm, wd_f)))]

    @pl.when(step == 0)
    def _():
        for c in weight_copies(be_ref[0], sl_ref[0]):
            c.start()

    for sub_blk in range(MOE_STEP_BLOCKS):
        i = step * MOE_STEP_BLOCKS + sub_blk
        rows = pl.ds(sub_blk * MOE_BM, MOE_BM)
        expert = be_ref[i]
        changed = (i == 0) | (expert != be_ref[jnp.maximum(i - 1, 0)])
        rows_valid = rv_ref[i]
        slot = sl_ref[i]

        @pl.when(changed)
        def _():
            for c in weight_copies(expert, slot):
                c.wait()

            @pl.when(nx_ref[i] >= 0)
            def _():
                for c in weight_copies(nx_ref[i], 1 - slot):
                    c.start(priority=1)

            wg_s[...] = wg_f[slot].astype(BF16)
            wu_s[...] = wu_f[slot].astype(BF16)
            wd_s[...] = wd_f[slot].astype(BF16)

        @pl.when(rows_valid > 0)
        def _():
            row = lax.broadcasted_iota(jnp.int32, (MOE_BM, D_MODEL // 2), 0)
            x = _unpack_bf16_pairs(jnp.where(row < rows_valid, xd_ref[rows, :], 0)).astype(BF16)
            gate = jnp.dot(x, wg_s[...], preferred_element_type=F32)
            up = jnp.dot(x, wu_s[...], preferred_element_type=F32)
            act = (gate * jax.nn.sigmoid(gate) * up).astype(BF16)
            y = jnp.dot(act, wd_s[...], preferred_element_type=F32)
            yd_ref[rows, :] = _pack_bf16_pairs(y.astype(BF16))

        @pl.when(rows_valid <= 0)
        def _():
            yd_ref[rows, :] = jnp.zeros((MOE_BM, D_MODEL // 2), jnp.int32)


def _moe_experts(l, block_e, rows_valid, next_e, slot, xd, w_gate, w_up, w_down):
    n_blocks = xd.shape[0] // MOE_BM
    step_rows = MOE_STEP_BLOCKS * MOE_BM
    row = lambda i, *_: (i, 0)
    return pl.pallas_call(
        functools.partial(_moe_kernel, layer=l),
        grid_spec=pltpu.PrefetchScalarGridSpec(
            num_scalar_prefetch=4,
            grid=(n_blocks // MOE_STEP_BLOCKS,),
            in_specs=[
                pl.BlockSpec((step_rows, D_MODEL // 2), row),
                pl.BlockSpec(memory_space=pl.ANY),
                pl.BlockSpec(memory_space=pl.ANY),
                pl.BlockSpec(memory_space=pl.ANY),
            ],
            out_specs=pl.BlockSpec((step_rows, D_MODEL // 2), row),
            scratch_shapes=[
                pltpu.VMEM((2, D_MODEL, EXPERT_FF), F32),
                pltpu.VMEM((2, D_MODEL, EXPERT_FF), F32),
                pltpu.VMEM((2, EXPERT_FF, D_MODEL), F32),
                pltpu.VMEM((D_MODEL, EXPERT_FF), BF16),
                pltpu.VMEM((D_MODEL, EXPERT_FF), BF16),
                pltpu.VMEM((EXPERT_FF, D_MODEL), BF16),
                pltpu.SemaphoreType.DMA((2, 3)),
            ],
        ),
        out_shape=jax.ShapeDtypeStruct((n_blocks * MOE_BM, D_MODEL // 2), jnp.int32),
        compiler_params=pltpu.CompilerParams(
            dimension_semantics=("arbitrary",), vmem_limit_bytes=VMEM_LIMIT),
        name="moe_experts",
    )(block_e, rows_valid, next_e, slot, xd, w_gate, w_up, w_down)


def _sc_worker_id():
    return lax.axis_index("s") * SC_CORES + lax.axis_index("c")


def _sc_dispatch(hp, hs, dest_p, dest_s, n_rows):
    tp, width = hp.shape
    per_w = tp // SC_WORKERS
    n_ch = per_w // DISP_CH
    with_sample = hs is not None
    n_sw = hs.shape[0] // SAMPLE_CH if with_sample else 0
    mesh = plsc.VectorSubcoreMesh(core_axis_name="c", subcore_axis_name="s")

    @functools.partial(
        pl.kernel, mesh=mesh,
        out_type=jax.ShapeDtypeStruct((n_rows, width), jnp.int32),
        scratch_types=[
            pltpu.VMEM((2, n_ch, DISP_CH), jnp.int32),
            pltpu.VMEM((2, 1, SAMPLE_CH), jnp.int32),
            pltpu.VMEM((2, DISP_CH, width), jnp.int32),
            pltpu.SemaphoreType.DMA((2,)),
            pltpu.SemaphoreType.DMA((2,)),
        ],
        name="sc_dispatch",
    )
    def k(*refs):
        if with_sample:
            hp_hbm, hs_hbm, dp_hbm, ds_hbm, xd_hbm, idx_v, idxs_v, bufs, rsem, wsem = refs
        else:
            hp_hbm, dp_hbm, xd_hbm, idx_v, idxs_v, bufs, rsem, wsem = refs
        wid = _sc_worker_id()
        base = wid * per_w
        for kk in range(2):
            pltpu.sync_copy(dp_hbm.at[kk, wid], idx_v.at[kk])
        reads = [pltpu.make_async_copy(hp_hbm.at[pl.ds(base + j * DISP_CH, DISP_CH)],
                                       bufs.at[j % 2], rsem.at[j % 2]) for j in range(n_ch)]
        reads[0].start()
        for j in range(n_ch):
            if j + 1 < n_ch:
                reads[j + 1].start()
            reads[j].wait()
            writes = [pltpu.make_async_copy(bufs.at[j % 2], xd_hbm.at[idx_v.at[kk, j]], wsem.at[kk])
                      for kk in range(2)]
            for w in writes:
                w.start()
            for w in writes:
                w.wait()

        if with_sample:
            @pl.when(wid < n_sw)
            def _():
                rows = bufs.at[0, pl.ds(0, SAMPLE_CH)]
                for kk in range(2):
                    pltpu.sync_copy(ds_hbm.at[kk, wid], idxs_v.at[kk])
                pltpu.sync_copy(hs_hbm.at[pl.ds(wid * SAMPLE_CH, SAMPLE_CH)], rows)
                for kk in range(2):
                    pltpu.sync_copy(rows, xd_hbm.at[idxs_v.at[kk, 0]])

    return k(hp, hs, dest_p, dest_s) if with_sample else k(hp, dest_p)


def _sc_combine_gather(yd, dest_p, dest_s, tp, ts):
    width = yd.shape[1]
    per_w = tp // SC_WORKERS
    n_ch = per_w // COMB_CH
    with_sample = dest_s is not None
    n_sw = ts // SAMPLE_CH
    mesh = plsc.VectorSubcoreMesh(core_axis_name="c", subcore_axis_name="s")

    @functools.partial(
        pl.kernel, mesh=mesh,
        out_type=jax.ShapeDtypeStruct((2, tp + ts, width), yd.dtype),
        scratch_types=[
            pltpu.VMEM((2, n_ch, COMB_CH), jnp.int32),
            pltpu.VMEM((2, 1, SAMPLE_CH), jnp.int32),
            pltpu.VMEM((2, COMB_CH, width), yd.dtype),
            pltpu.SemaphoreType.DMA((2,)),
            pltpu.SemaphoreType.DMA((2,)),
        ],
        name="sc_combine_gather",
    )
    def k(*refs):
        if with_sample:
            yd_hbm, dp_hbm, ds_hbm, g_hbm, idx_v, idxs_v, bufs, gsem, wsem = refs
        else:
            yd_hbm, dp_hbm, g_hbm, idx_v, idxs_v, bufs, gsem, wsem = refs
        wid = _sc_worker_id()
        base = wid * per_w
        for kk in range(2):
            pltpu.sync_copy(dp_hbm.at[kk, wid], idx_v.at[kk])
        items = [(kk, j) for kk in range(2) for j in range(n_ch)]
        gathers = [pltpu.make_async_copy(yd_hbm.at[idx_v.at[kk, j]], bufs.at[n % 2], gsem.at[n % 2])
                   for n, (kk, j) in enumerate(items)]
        gathers[0].start()
        for n, (kk, j) in enumerate(items):
            if n + 1 < len(items):
                gathers[n + 1].start()
            gathers[n].wait()
            w = pltpu.make_async_copy(bufs.at[n % 2], g_hbm.at[kk, pl.ds(base + j * COMB_CH, COMB_CH)],
                                      wsem.at[n % 2])
            w.start()
            w.wait()

        if with_sample:
            @pl.when(wid < n_sw)
            def _():
                for kk in range(2):
                    pltpu.sync_copy(ds_hbm.at[kk, wid], idxs_v.at[kk])
                for kk in range(2):
                    rows = bufs.at[kk, pl.ds(0, SAMPLE_CH)]
                    pltpu.sync_copy(yd_hbm.at[idxs_v.at[kk, 0]], rows)
                    pltpu.sync_copy(rows, g_hbm.at[kk, pl.ds(tp + wid * SAMPLE_CH, SAMPLE_CH)])

    return k(yd, dest_p, dest_s) if with_sample else k(yd, dest_p)


def _combine_kernel(x1_ref, g_ref, route_ref, x2_ref):
    w1 = route_ref[:, 2:3]
    w2 = route_ref[:, 3:4]
    x2_ref[...] = x1_ref[...] + _unpack_bf16_pairs(g_ref[0]) * w1 + _unpack_bf16_pairs(g_ref[1]) * w2


def _combine_into_kernel(x1_ref, g_ref, route_ref, buf_ref, x2_ref):
    del buf_ref
    _combine_kernel(x1_ref, g_ref, route_ref, x2_ref)


def _combine(x1, g, route, g_row0, tm, out_rows=None, out_row0=0, out_buf=None):
    t = x1.shape[0]
    blk0 = g_row0 // tm
    oblk0 = out_row0 // tm
    row = lambda i: (i, 0)
    in_specs = [
        pl.BlockSpec((tm, D_MODEL), row),
        pl.BlockSpec((2, tm, D_MODEL // 2), lambda i: (0, blk0 + i, 0)),
        pl.BlockSpec((tm, LANES), row),
    ]
    args = [x1, g, route]
    kern = _combine_kernel
    aliases = {}
    if out_buf is not None:
        in_specs.append(pl.BlockSpec(memory_space=pl.ANY))
        args.append(out_buf)
        aliases = {3: 0}
        kern = _combine_into_kernel
    return pl.pallas_call(
        kern,
        grid=(t // tm,),
        in_specs=in_specs,
        out_specs=pl.BlockSpec((tm, D_MODEL), lambda i: (oblk0 + i, 0)),
        out_shape=jax.ShapeDtypeStruct((t if out_rows is None else out_rows, D_MODEL), F32),
        input_output_aliases=aliases,
        compiler_params=pltpu.CompilerParams(
            dimension_semantics=("arbitrary",), vmem_limit_bytes=VMEM_LIMIT),
        name="combine",
    )(*args)


def _dest_layout(dest, workers, chunk):
    t = dest.shape[1]
    return dest.reshape(2, workers, t // (workers * chunk), chunk)


def _moe_rows(l, h2p, h2s, route_tp, route_ts, counts, w_gate, w_up, w_down):
    tp = h2p.shape[0]
    ts = 0 if h2s is None else h2s.shape[0]
    n_assign = 2 * (tp + ts)
    n_blocks = -(-n_assign // MOE_BM) + N_EXPERTS
    n_blocks = -(-n_blocks // MOE_STEP_BLOCKS) * MOE_STEP_BLOCKS
    pcounts = (counts + MOE_BM - 1) // MOE_BM * MOE_BM
    pends = jnp.cumsum(pcounts)
    poffsets = pends - pcounts
    starts = jnp.arange(n_blocks, dtype=jnp.int32) * MOE_BM
    block_e = jnp.minimum(jnp.sum((pends[None, :] <= starts[:, None]).astype(jnp.int32), axis=1),
                          N_EXPERTS - 1)
    experts = jnp.arange(N_EXPERTS, dtype=jnp.int32)

    def lookup(table, idx):
        return jnp.sum(jnp.where(idx[..., None] == experts, table, 0), axis=-1)

    rows_valid = jnp.clip(lookup(poffsets + counts, block_e) - starts, 0, MOE_BM).astype(jnp.int32)
    used = counts > 0
    last_e = jnp.max(jnp.where(used, jnp.arange(N_EXPERTS, dtype=jnp.int32), 0))
    block_e = jnp.where(rows_valid > 0, block_e, last_e).astype(jnp.int32)
    later = used[None, :] & (jnp.arange(N_EXPERTS)[None, :] > jnp.arange(N_EXPERTS)[:, None])
    next_used = jnp.min(jnp.where(later, jnp.arange(N_EXPERTS, dtype=jnp.int32)[None, :], N_EXPERTS), axis=1)
    next_used = jnp.where(next_used >= N_EXPERTS, -1, next_used).astype(jnp.int32)
    slot_of = ((jnp.cumsum(used.astype(jnp.int32)) - 1) & 1).astype(jnp.int32)
    next_e = lookup(next_used, block_e)
    slot = lookup(slot_of, block_e)

    def dest_of(route_t):
        return lookup(poffsets, route_t[0:2].astype(jnp.int32)) + route_t[4:6].astype(jnp.int32)

    dest_p = dest_of(route_tp)
    dest_s = None if h2s is None else _dest_layout(dest_of(route_ts), ts // SAMPLE_CH, SAMPLE_CH)
    xd = _sc_dispatch(h2p, h2s, _dest_layout(dest_p, SC_WORKERS, DISP_CH), dest_s, n_blocks * MOE_BM)
    yd = _moe_experts(l, block_e, rows_valid, next_e, slot, xd, w_gate, w_up, w_down)
    return _sc_combine_gather(yd, _dest_layout(dest_p, SC_WORKERS, COMB_CH), dest_s, tp, ts)


def kernel(x_prompt, x_sample, state_pool, cache_k_win, cache_v_win, norm_attn_g, w_in, pool_w, pool_scale, q_norm_g, k_norm_g, attn_sinks, w_out, norm_ffn_g, router_group_w, router_group_b, router_expert_w, router_expert_b, w_gate, w_up, w_down):
    n_p, t_p, d = x_prompt.shape
    n_s, t_s, _ = x_sample.shape
    depth = w_in.shape[0]
    lw_s = cache_k_win.shape[2]
    assert t_s == 1 and lw_s == WINDOW and d == D_MODEL
    assert t_p % TM_PROJ == 0 and t_p >= WINDOW
    past_len = 16384

    w_in_b = w_in.astype(BF16)
    w_out_b = w_out.astype(BF16)
    g_attn = norm_attn_g.reshape(depth, 1, D_MODEL)
    g_ffn = norm_ffn_g.reshape(depth, 1, D_MODEL)
    qg = (jnp.tile(q_norm_g, (1, N_HEADS)) * ATTN_SCALE).reshape(depth, 1, Q_W)
    kg = jnp.tile(k_norm_g, (1, N_KV_HEADS)).reshape(depth, 1, KV_W)
    seg = jnp.arange(256) // HEAD_DIM
    bd = jnp.where(seg[:, None] == seg[None, :], 1.0 / HEAD_DIM, 0.0).astype(BF16)
    wp = jnp.zeros((depth, 2, 256, 256), F32)
    for p in range(2):
        wp = wp.at[:, p, :POOL_GC, :POOL_GC].set(pool_w[:, 2 * p])
        wp = wp.at[:, p, POOL_GC:, POOL_GC:].set(pool_w[:, 2 * p + 1])
    wp = wp.astype(BF16)
    ps = pool_scale.reshape(depth, 1, POOL_W)
    wr = jnp.zeros((depth, D_MODEL, LANES), F32)
    wr = wr.at[:, :, :N_EXPERTS].set(router_expert_w)
    wr = wr.at[:, :, GROUP_LANE0:GROUP_LANE0 + N_EXPERT_GROUPS].set(router_group_w)
    wr = wr.astype(BF16)
    br = jnp.zeros((depth, 1, LANES), F32)
    br = br.at[:, 0, :N_EXPERTS].set(router_expert_b)
    br = br.at[:, 0, GROUP_LANE0:GROUP_LANE0 + N_EXPERT_GROUPS].set(router_group_b)

    slopes = jnp.exp2(-8.0 * jnp.arange(1, N_HEADS + 1, dtype=F32) / N_HEADS)
    bias_p = _prompt_bias_t()
    dist_s = (WINDOW - 1) - jnp.arange(WINDOW, dtype=F32)
    bias_s = -slopes[:, None] * dist_s[None, :]
    state2d = state_pool.reshape(depth, n_s, POOL_STATE * POOL_W)
    ck_all = cache_k_win.reshape(depth, n_s, lw_s, KV_W)
    cv_all = cache_v_win.reshape(depth, n_s, lw_s, KV_W)

    xp = x_prompt.reshape(n_p * t_p, D_MODEL)
    xs = x_sample.reshape(n_s, D_MODEL)
    lw_p = min(WINDOW, t_p)
    pool_p, kp_new, vp_new, pool_s, ks_new, vs_new = [], [], [], [], [], []
    zero_cnt = jnp.zeros((N_EXPERTS, 1), F32)
    n_h = n_p // 2
    t_h = n_h * t_p
    x_in = [(xp, 0), (xp, n_h)]
    for l in range(depth):
        sinks = attn_sinks[l]
        last = l == depth - 1
        tails, x_next = [], []
        for half, (x_arr, seq0) in enumerate(x_in):
            pool_o, q, k, vt, utail, ktail, vtail = _proj_pool_prompt(
                l, x_arr, seq0, n_h, t_p, g_attn, w_in_b, qg, kg, bd, wp, ps)
            attn_o = _attn_prompt(q, k, vt, bias_p, sinks, n_h, t_p)
            x1p, h2p, route_p, route_tp, cnt_p = _merge_router(
                l, pool_o, attn_o, x_arr, seq0 * t_p, w_out_b, g_ffn, wr, br, zero_cnt, TM_PROJ)
            tails.append((utail, ktail, vtail))
            if half == 0:
                g = _moe_rows(l, h2p, None, route_tp, None, cnt_p[:, 0].astype(jnp.int32),
                              w_gate, w_up, w_down)
                if last:
                    y_full = _combine(x1p, g, route_p, 0, TM_PROJ, out_rows=2 * t_h)
                else:
                    x_next.append((_combine(x1p, g, route_p, 0, TM_PROJ), 0))
            else:
                pool_so, attn_so, u_s, kc_s, vc_s = _sample_mixer(
                    l, xs, g_attn, w_in_b, qg, kg, bd, wp, ps, state2d, ck_all, cv_all,
                    sinks.reshape(N_HEADS, 1), bias_s, past_len)
                x1s, h2s, route_s, route_ts, cnt_all = _merge_router(
                    l, pool_so, attn_so, xs, 0, w_out_b, g_ffn, wr, br, cnt_p, n_s)
                pool_s.append(jnp.concatenate([state_pool[l][:, 1:], u_s[:, None, :]], axis=1))
                ks_new.append(kc_s)
                vs_new.append(vc_s)
                g = _moe_rows(l, h2p, h2s, route_tp, route_ts, cnt_all[:, 0].astype(jnp.int32),
                              w_gate, w_up, w_down)
                if last:
                    y_full = _combine(x1p, g, route_p, 0, TM_PROJ, out_rows=2 * t_h, out_row0=t_h,
                                      out_buf=y_full)
                else:
                    x_next.append((_combine(x1p, g, route_p, 0, TM_PROJ), 0))
                xs = _combine(x1s, g, route_s, t_h, n_s)
        x_in = x_next
        pool_p.append(jnp.concatenate([t[0] for t in tails], axis=0)[:, 16 - POOL_STATE:, :])
        kp_new.append(jnp.concatenate([t[1] for t in tails], axis=0))
        vp_new.append(jnp.concatenate([t[2] for t in tails], axis=0))
    xp = y_full
    return (xp.reshape(n_p, t_p, D_MODEL), xs.reshape(n_s, t_s, D_MODEL),
            jnp.stack(pool_p),
            jnp.stack(kp_new).reshape(depth, n_p, lw_p, N_KV_HEADS, HEAD_DIM),
            jnp.stack(vp_new).reshape(depth, n_p, lw_p, N_KV_HEADS, HEAD_DIM),
            jnp.stack(pool_s),
            jnp.stack(ks_new).reshape(depth, n_s, lw_s, N_KV_HEADS, HEAD_DIM),
            jnp.stack(vs_new).reshape(depth, n_s, lw_s, N_KV_HEADS, HEAD_DIM))
```

```python
import functools

import jax
import jax.numpy as jnp
from jax import lax
from jax.experimental import pallas as pl
from jax.experimental.pallas import tpu as pltpu
from jax.experimental.pallas import tpu_sc as plsc

D_MODEL = 1024
POOL_W = 512
POOL_WINDOWS = (2, 4, 8, 16)
POOL_GC = 128
POOL_STATE = 15
HEAD_DIM = 64
N_HEADS = 8
N_KV_HEADS = 2
GQA_GROUP = 4
Q_W = 512
KV_W = 128
D_IN = POOL_W + Q_W + 2 * KV_W
WINDOW = 128
ATTN_SCALE = HEAD_DIM ** -0.5
N_EXPERT_GROUPS = 4
EXPERTS_PER_GROUP = 8
N_EXPERTS = 32
EXPERT_FF = 512
EPS = 1e-6

LANES = 128
HALO = 32
TM_PROJ = 512
MERGE_CHUNKS = 2
ATTN_QB = 4
MOE_BM = 256
MOE_STEP_BLOCKS = 4
GROUP_LANE0 = 32
ROUTE_FIELDS = 8
SC_CORES = 2
SC_SUBCORES = 16
SC_WORKERS = SC_CORES * SC_SUBCORES
DISP_CH = 64
COMB_CH = 64
SAMPLE_CH = 32
VMEM_LIMIT = 48 * 1024 * 1024

BF16 = jnp.bfloat16
F32 = jnp.float32


def _pack_bf16_pairs(h):
    w = h.shape[1] // 2
    hi = lax.bitcast_convert_type(h[:, :w].astype(F32), jnp.uint32)
    lo = lax.bitcast_convert_type(h[:, w:].astype(F32), jnp.uint32)
    return lax.bitcast_convert_type(hi | (lo >> 16), jnp.int32)


def _unpack_bf16_pairs(words):
    u = lax.bitcast_convert_type(words, jnp.uint32)
    hi = lax.bitcast_convert_type(u & jnp.uint32(0xFFFF0000), F32)
    lo = lax.bitcast_convert_type(u << 16, F32)
    return jnp.concatenate([hi, lo], axis=-1)


def _segment_mean_sq(a, bd):
    w = a.shape[1]
    return jnp.dot((a * a).astype(BF16), bd[:w, :w], preferred_element_type=F32)


def _rms_bf16(x, g):
    ms = jnp.mean(x * x, axis=-1, keepdims=True)
    return (x * lax.rsqrt(ms + EPS) * g).astype(BF16)


def _qk_norm(q, k, qg, kg, bd):
    qn = []
    for c in range(Q_W // 256):
        qc = q[:, c * 256:(c + 1) * 256]
        qn.append(qc * lax.rsqrt(_segment_mean_sq(qc, bd) + EPS))
    qn = jnp.concatenate(qn, axis=-1) * qg
    kn = k * lax.rsqrt(_segment_mean_sq(k, bd) + EPS) * kg
    return qn, kn


def _project(x, g, w_in, qg, kg, bd):
    z = jnp.dot(_rms_bf16(x, g), w_in, preferred_element_type=F32)
    u = z[:, :POOL_W]
    q = z[:, POOL_W:POOL_W + Q_W]
    k = z[:, POOL_W + Q_W:POOL_W + Q_W + KV_W]
    v = z[:, POOL_W + Q_W + KV_W:]
    qn, kn = _qk_norm(q, k, qg, kg, bd)
    return u, qn, kn, v


def _pool_project(d_groups, wp_ref, ps):
    outs = []
    for p in range(2):
        dp = jnp.concatenate([d_groups[2 * p], d_groups[2 * p + 1]], axis=-1).astype(BF16)
        y = jnp.dot(dp, wp_ref[p], preferred_element_type=F32)
        outs.append(y * ps[:, p * 256:(p + 1) * 256])
    return jnp.concatenate(outs, axis=-1)


def _proj_pool_kernel(x_ref, g_ref, win_ref, qg_ref, kg_ref, bd_ref, wp_ref, ps_ref,
                      pool_ref, q_ref, k_ref, vt_ref, utail_ref, ktail_ref, vtail_ref,
                      ext_ref, sa_ref, sb_ref, zq_ref, *, tm, n_j):
    j = pl.program_id(1)

    @pl.when(j == 0)
    def _():
        ext_ref[0:HALO, :] = jnp.zeros((HALO, POOL_W), F32)

    r = tm + HALO
    h = _rms_bf16(x_ref[...], g_ref[...])
    ext_ref[HALO:r, :] = jnp.dot(h, win_ref[:, 0:POOL_W], preferred_element_type=F32)
    zq_ref[...] = jnp.dot(h, win_ref[:, POOL_W:], preferred_element_type=F32)
    u = ext_ref[HALO:r, :]
    sa_ref[8:r, :] = ext_ref[8:r, :] + ext_ref[7:r - 1, :]
    sb_ref[16:r, 128:] = sa_ref[16:r, 128:] + sa_ref[14:r - 2, 128:]
    sa_ref[24:r, 256:] = sb_ref[24:r, 256:] + sb_ref[20:r - 4, 256:]
    sb_ref[32:r, 384:] = sa_ref[32:r, 384:] + sa_ref[24:r - 8, 384:]
    pos1 = j * tm + lax.broadcasted_iota(jnp.int32, (tm, POOL_GC), 0) + 1
    sums = (sa_ref, sb_ref, sa_ref, sb_ref)
    d_groups = []
    for gi, w in enumerate(POOL_WINDOWS):
        sl = slice(gi * POOL_GC, (gi + 1) * POOL_GC)
        cnt = jnp.minimum(pos1, w).astype(F32)
        d_groups.append(sums[gi][HALO:r, sl] / cnt - u[:, sl])
    pool_ref[...] = _pool_project(d_groups, wp_ref, ps_ref[...]).astype(BF16)
    ext_ref[16:HALO, :] = ext_ref[tm + 16:r, :]

    qn, kn = _qk_norm(zq_ref[:, 0:Q_W], zq_ref[:, Q_W:Q_W + KV_W], qg_ref[...], kg_ref[...], bd_ref[...])
    v = zq_ref[:, Q_W + KV_W:]
    q_ref[...] = qn.astype(BF16)
    k_ref[...] = kn.astype(BF16)
    vt_ref[...] = jnp.transpose(v).astype(BF16)

    @pl.when(j == n_j - 1)
    def _():
        utail_ref[...] = u[tm - 16:, :]
        ktail_ref[...] = kn[tm - WINDOW:, :]
        vtail_ref[...] = v[tm - WINDOW:, :]


def _proj_pool_prompt(l, x2d, n_seq, seq, g_attn, w_in, qg, kg, bd, wp, ps):
    tm = TM_PROJ
    n_j = seq // tm
    t = n_seq * seq
    row = lambda b, j: (b * n_j + j, 0)
    lay = lambda b, j: (l, 0, 0)
    return pl.pallas_call(
        functools.partial(_proj_pool_kernel, tm=tm, n_j=n_j),
        grid=(n_seq, n_j),
        in_specs=[
            pl.BlockSpec((tm, D_MODEL), row),
            pl.BlockSpec((None, 1, D_MODEL), lay),
            pl.BlockSpec((None, D_MODEL, D_IN), lay),
            pl.BlockSpec((None, 1, Q_W), lay),
            pl.BlockSpec((None, 1, KV_W), lay),
            pl.BlockSpec((256, 256), lambda b, j: (0, 0)),
            pl.BlockSpec((None, 2, 256, 256), lambda b, j: (l, 0, 0, 0)),
            pl.BlockSpec((None, 1, POOL_W), lay),
        ],
        out_specs=[
            pl.BlockSpec((tm, POOL_W), row),
            pl.BlockSpec((tm, Q_W), row),
            pl.BlockSpec((tm, KV_W), row),
            pl.BlockSpec((KV_W, tm), lambda b, j: (0, b * n_j + j)),
            pl.BlockSpec((None, 16, POOL_W), lambda b, j: (b, 0, 0)),
            pl.BlockSpec((None, WINDOW, KV_W), lambda b, j: (b, 0, 0)),
            pl.BlockSpec((None, WINDOW, KV_W), lambda b, j: (b, 0, 0)),
        ],
        out_shape=[
            jax.ShapeDtypeStruct((t, POOL_W), BF16),
            jax.ShapeDtypeStruct((t, Q_W), BF16),
            jax.ShapeDtypeStruct((t, KV_W), BF16),
            jax.ShapeDtypeStruct((KV_W, t), BF16),
            jax.ShapeDtypeStruct((n_seq, 16, POOL_W), F32),
            jax.ShapeDtypeStruct((n_seq, WINDOW, KV_W), F32),
            jax.ShapeDtypeStruct((n_seq, WINDOW, KV_W), F32),
        ],
        scratch_shapes=[pltpu.VMEM((tm + HALO, POOL_W), F32)] * 3 + [pltpu.VMEM((tm, Q_W + 2 * KV_W), F32)],
        compiler_params=pltpu.CompilerParams(
            dimension_semantics=("arbitrary", "arbitrary"), vmem_limit_bytes=VMEM_LIMIT),
        name="proj_pool_prompt",
    )(x2d, g_attn, w_in, qg, kg, bd, wp, ps)


def _attn_kernel(sink_ref, q_ref, kp_ref, kc_ref, vtp_ref, vtc_ref, bias_ref, o_ref, s_ref):
    j = pl.program_id(1)
    kk_all = jnp.concatenate([kp_ref[...], kc_ref[...]], axis=0)
    vt_all = jnp.concatenate([vtp_ref[...], vtc_ref[...]], axis=1)
    from_prev = (lax.broadcasted_iota(jnp.int32, (WINDOW, WINDOW), 0)
                 > lax.broadcasted_iota(jnp.int32, (WINDOW, WINDOW), 1))
    units = [(blk, kv) for blk in range(ATTN_QB) for kv in range(N_KV_HEADS)]

    def scores(n):
        blk, kv = units[n]
        q = q_ref[blk * WINDOW:(blk + 1) * WINDOW, :]
        kk = kk_all[blk * WINDOW:(blk + 2) * WINDOW, kv * HEAD_DIM:(kv + 1) * HEAD_DIM]
        heads = range(kv * GQA_GROUP, (kv + 1) * GQA_GROUP)
        q_rows = jnp.concatenate([q[:, h * HEAD_DIM:(h + 1) * HEAD_DIM] for h in heads], axis=0)
        s_ref[n % 2] = lax.dot_general(kk, q_rows, (((1,), (1,)), ((), ())), preferred_element_type=F32)

    scores(0)
    outs = []
    for n, (blk, kv) in enumerate(units):
        if n + 1 < len(units):
            scores(n + 1)
        vt_kv = vt_all[kv * HEAD_DIM:(kv + 1) * HEAD_DIM, blk * WINDOW:(blk + 2) * WINDOW]
        variant = jnp.minimum(j, 1) if blk == 0 else 1
        for g in range(GQA_GROUP):
            h = kv * GQA_GROUP + g
            s = jnp.where(from_prev, s_ref[n % 2, 0:WINDOW, g * WINDOW:(g + 1) * WINDOW],
                          s_ref[n % 2, WINDOW:, g * WINDOW:(g + 1) * WINDOW]) + bias_ref[variant, h]
            sink = sink_ref[h]
            m = jnp.maximum(jnp.max(s, axis=0, keepdims=True), sink)
            p = jnp.exp(s - m)
            denom = jnp.sum(p, axis=0, keepdims=True) + jnp.exp(sink - m)
            p_keys = jnp.concatenate([jnp.where(from_prev, p, 0.0), jnp.where(from_prev, 0.0, p)], axis=0)
            o_t = jnp.dot(vt_kv, p_keys.astype(BF16), preferred_element_type=F32)
            outs.append(o_t / denom)
        if kv == N_KV_HEADS - 1:
            o_ref[blk * WINDOW:(blk + 1) * WINDOW, :] = jnp.transpose(jnp.concatenate(outs, axis=0)).astype(BF16)
            outs = []


def _attn_prompt(q, k, vt, bias_t, sinks, n_seq, seq):
    tq = ATTN_QB * WINDOW
    nj = seq // tq
    t = n_seq * seq
    cur = lambda b, j: (b * nj + j, 0)
    prev = lambda b, j: (jnp.maximum((b * nj + j) * ATTN_QB - 1, 0), 0)
    cur_t = lambda b, j: (0, b * nj + j)
    prev_t = lambda b, j: (0, jnp.maximum((b * nj + j) * ATTN_QB - 1, 0))
    return pl.pallas_call(
        _attn_kernel,
        grid=(n_seq, nj),
        in_specs=[
            pl.BlockSpec(memory_space=pltpu.SMEM),
            pl.BlockSpec((tq, Q_W), cur),
            pl.BlockSpec((WINDOW, KV_W), prev),
            pl.BlockSpec((tq, KV_W), cur),
            pl.BlockSpec((KV_W, WINDOW), prev_t),
            pl.BlockSpec((KV_W, tq), cur_t),
            pl.BlockSpec((2, N_HEADS, WINDOW, WINDOW), lambda b, j: (0, 0, 0, 0)),
        ],
        out_specs=pl.BlockSpec((tq, Q_W), cur),
        out_shape=jax.ShapeDtypeStruct((t, Q_W), BF16),
        scratch_shapes=[pltpu.VMEM((2, 2 * WINDOW, GQA_GROUP * WINDOW), F32)],
        compiler_params=pltpu.CompilerParams(
            dimension_semantics=("arbitrary", "arbitrary"), vmem_limit_bytes=VMEM_LIMIT),
        name="attn_prompt",
    )(sinks, q, k, k, vt, vt, bias_t)


def _prompt_bias_t():
    r = jnp.arange(WINDOW, dtype=jnp.int32)[None, :]
    c = jnp.arange(WINDOW, dtype=jnp.int32)[:, None]
    from_prev = c > r
    dist = r - c + jnp.where(from_prev, WINDOW, 0)
    slopes = jnp.exp2(-8.0 * jnp.arange(1, N_HEADS + 1, dtype=F32) / N_HEADS)
    later = -slopes[:, None, None] * dist.astype(F32)[None]
    first = jnp.where(from_prev[None], -jnp.inf, later)
    return jnp.stack([first, later])


def _sample_kernel(x_ref, g_ref, win_ref, qg_ref, kg_ref, bd_ref, wp_ref, ps_ref,
                   st_ref, ck_ref, cv_ref, sink_ref, bias_ref, perm_ref,
                   pool_ref, attn_ref, u_ref, kc_ref, vc_ref, *, ns, pos0):
    u, qn, kn, v = _project(x_ref[...], g_ref[...], win_ref[...], qg_ref[...], kg_ref[...], bd_ref[...])
    u_ref[...] = u
    kc_ref[:, 0:WINDOW - 1, :] = ck_ref[:, 1:WINDOW, :]
    vc_ref[:, 0:WINDOW - 1, :] = cv_ref[:, 1:WINDOW, :]
    for n in range(ns):
        kc_ref[n, WINDOW - 1:WINDOW, :] = kn[n:n + 1, :]
        vc_ref[n, WINDOW - 1:WINDOW, :] = v[n:n + 1, :]

    d_groups = []
    for gi, w in enumerate(POOL_WINDOWS):
        lo = gi * POOL_GC
        acc = u[:, lo:lo + POOL_GC]
        for back in range(1, w):
            off = (POOL_STATE - back) * POOL_W + lo
            acc = acc + st_ref[:, off:off + POOL_GC]
        d_groups.append(acc / float(min(pos0 + 1, w)) - u[:, lo:lo + POOL_GC])
    pool_ref[...] = _pool_project(d_groups, wp_ref, ps_ref[...]).astype(BF16)

    zeros = jnp.zeros((ns, HEAD_DIM), F32)
    stacked = []
    for h in range(N_HEADS):
        piece = qn[:, h * HEAD_DIM:(h + 1) * HEAD_DIM]
        pair = [piece, zeros] if h < GQA_GROUP else [zeros, piece]
        stacked.append(jnp.concatenate(pair, axis=-1))
    q_hn = jnp.concatenate(stacked, axis=0).astype(BF16)
    q_nh = jnp.dot(perm_ref[0], q_hn, preferred_element_type=F32).astype(BF16)

    keys = kc_ref[...].reshape(ns * WINDOW, KV_W).astype(BF16)
    vals = vc_ref[...].reshape(ns * WINDOW, KV_W).astype(BF16)
    s_all = lax.dot_general(q_nh, keys, (((1,), (1,)), ((), ())), preferred_element_type=F32)
    sink = sink_ref[...]
    bias = bias_ref[...]
    zero_blk = jnp.zeros((N_HEADS, WINDOW), F32)
    p_rows = []
    for n in range(ns):
        s = s_all[n * N_HEADS:(n + 1) * N_HEADS, n * WINDOW:(n + 1) * WINDOW] + bias
        m = jnp.maximum(jnp.max(s, axis=-1, keepdims=True), sink)
        p = jnp.exp(s - m)
        denom = jnp.sum(p, axis=-1, keepdims=True) + jnp.exp(sink - m)
        p_rows.append(jnp.concatenate([zero_blk] * n + [p / denom] + [zero_blk] * (ns - 1 - n), axis=-1))
    p_blockdiag = jnp.concatenate(p_rows, axis=0).astype(BF16)
    o_nh = jnp.dot(p_blockdiag, vals, preferred_element_type=F32).astype(BF16)
    o_hn = jnp.dot(perm_ref[1], o_nh, preferred_element_type=F32)
    pieces = []
    for h in range(N_HEADS):
        kv = h // GQA_GROUP
        pieces.append(o_hn[h * ns:(h + 1) * ns, kv * HEAD_DIM:(kv + 1) * HEAD_DIM])
    attn_ref[...] = jnp.concatenate(pieces, axis=-1).astype(BF16)


def _sample_mixer(l, xs, g_attn, w_in, qg, kg, bd, wp, ps, state2d, ck, cv, sink8, bias_s, pos0):
    n = xs.shape[0]
    ns = 32
    row = lambda i: (i, 0)
    lay = lambda i: (l, 0, 0)
    src = jnp.arange(ns * N_HEADS)
    perm = (((src % N_HEADS) * ns + src // N_HEADS)[:, None] == src[None, :]).astype(BF16)
    perms = jnp.stack([perm, perm.T])
    return pl.pallas_call(
        functools.partial(_sample_kernel, ns=ns, pos0=pos0),
        grid=(n // ns,),
        in_specs=[
            pl.BlockSpec((ns, D_MODEL), row),
            pl.BlockSpec((None, 1, D_MODEL), lay),
            pl.BlockSpec((None, D_MODEL, D_IN), lay),
            pl.BlockSpec((None, 1, Q_W), lay),
            pl.BlockSpec((None, 1, KV_W), lay),
            pl.BlockSpec((256, 256), lambda i: (0, 0)),
            pl.BlockSpec((None, 2, 256, 256), lambda i: (l, 0, 0, 0)),
            pl.BlockSpec((None, 1, POOL_W), lay),
            pl.BlockSpec((None, ns, POOL_STATE * POOL_W), lambda i: (l, i, 0)),
            pl.BlockSpec((None, ns, WINDOW, KV_W), lambda i: (l, i, 0, 0)),
            pl.BlockSpec((None, ns, WINDOW, KV_W), lambda i: (l, i, 0, 0)),
            pl.BlockSpec((N_HEADS, 1), lambda i: (0, 0)),
            pl.BlockSpec((N_HEADS, WINDOW), lambda i: (0, 0)),
            pl.BlockSpec((2, ns * N_HEADS, ns * N_HEADS), lambda i: (0, 0, 0)),
        ],
        out_specs=[
            pl.BlockSpec((ns, POOL_W), row),
            pl.BlockSpec((ns, Q_W), row),
            pl.BlockSpec((ns, POOL_W), row),
            pl.BlockSpec((ns, WINDOW, KV_W), lambda i: (i, 0, 0)),
            pl.BlockSpec((ns, WINDOW, KV_W), lambda i: (i, 0, 0)),
        ],
        out_shape=[
            jax.ShapeDtypeStruct((n, POOL_W), BF16),
            jax.ShapeDtypeStruct((n, Q_W), BF16),
            jax.ShapeDtypeStruct((n, POOL_W), F32),
            jax.ShapeDtypeStruct((n, WINDOW, KV_W), F32),
            jax.ShapeDtypeStruct((n, WINDOW, KV_W), F32),
        ],
        compiler_params=pltpu.CompilerParams(
            dimension_semantics=("arbitrary",), vmem_limit_bytes=VMEM_LIMIT),
        name="sample_mixer",
    )(xs, g_attn, w_in, qg, kg, bd, wp, ps, state2d, ck, cv, sink8, bias_s, perms)


def _merge_router_kernel(pool_ref, attn_ref, x_ref, wout_ref, g_ref, wr_ref, br_ref, utri_ref, cin_ref,
                         x1_ref, h2_ref, route_ref, route_t_ref, cnt_ref, y_ref, lg_ref):
    i = pl.program_id(0)

    @pl.when(i == 0)
    def _():
        cnt_ref[...] = cin_ref[...]

    tm = x_ref.shape[0]
    rc = tm // MERGE_CHUNKS
    chunks = [slice(ci * rc, (ci + 1) * rc) for ci in range(MERGE_CHUNKS)]
    for rows in chunks:
        y_ref[rows, :] = (jnp.dot(pool_ref[rows, :], wout_ref[0:POOL_W, :], preferred_element_type=F32)
                          + jnp.dot(attn_ref[rows, :], wout_ref[POOL_W:, :], preferred_element_type=F32))
    for rows in chunks:
        x1 = x_ref[rows, :] + y_ref[rows, :]
        x1_ref[rows, :] = x1
        h2 = _rms_bf16(x1, g_ref[...])
        h2_ref[rows, :] = _pack_bf16_pairs(h2)
        lg_ref[rows, :] = jnp.dot(h2, wr_ref[...], preferred_element_type=F32) + br_ref[...]
    logits = lg_ref[...]

    lt = jnp.transpose(logits)
    sub = lax.broadcasted_iota(jnp.int32, (EXPERTS_PER_GROUP, tm), 0)
    neg = -jnp.inf
    big = jnp.int32(EXPERTS_PER_GROUP)
    gl = jnp.where(sub < N_EXPERT_GROUPS, lt[GROUP_LANE0:GROUP_LANE0 + EXPERTS_PER_GROUP, :], neg)
    gmax = jnp.max(gl, axis=0, keepdims=True)
    grp = jnp.min(jnp.where(gl == gmax, sub, big), axis=0, keepdims=True)
    g_w = 1.0 / jnp.sum(jnp.exp(gl - gmax), axis=0, keepdims=True)
    el = lt[(N_EXPERT_GROUPS - 1) * EXPERTS_PER_GROUP:N_EXPERT_GROUPS * EXPERTS_PER_GROUP, :]
    for gi in range(N_EXPERT_GROUPS - 2, -1, -1):
        el = jnp.where(grp == gi, lt[gi * EXPERTS_PER_GROUP:(gi + 1) * EXPERTS_PER_GROUP, :], el)
    v1 = jnp.max(el, axis=0, keepdims=True)
    i1 = jnp.min(jnp.where(el == v1, sub, big), axis=0, keepdims=True)
    el2 = jnp.where(sub == i1, neg, el)
    v2 = jnp.max(el2, axis=0, keepdims=True)
    i2 = jnp.min(jnp.where(el2 == v2, sub, big), axis=0, keepdims=True)
    e21 = jnp.exp(v2 - v1)
    w1 = g_w / (1.0 + e21)
    w2 = g_w * e21 / (1.0 + e21)
    e1 = grp * EXPERTS_PER_GROUP + i1
    e2 = grp * EXPERTS_PER_GROUP + i2

    esub = lax.broadcasted_iota(jnp.int32, (N_EXPERTS, tm), 0)
    oh1 = esub == e1
    oh2 = esub == e2
    c = jnp.where(oh1 | oh2, 1.0, 0.0)
    prefix = jnp.dot(c.astype(BF16), utri_ref[...], preferred_element_type=F32) + cnt_ref[...]
    r1 = jnp.sum(jnp.where(oh1, prefix, 0.0), axis=0, keepdims=True)
    r2 = jnp.sum(jnp.where(oh2, prefix, 0.0), axis=0, keepdims=True)
    cnt_ref[...] = cnt_ref[...] + jnp.sum(c, axis=1, keepdims=True)

    fields = jnp.zeros((ROUTE_FIELDS, tm), F32)
    for idx, val in enumerate((e1.astype(F32), e2.astype(F32), w1, w2, r1, r2)):
        fields = jnp.where(sub == idx, val, fields)
    route_t_ref[...] = fields
    padded = jnp.concatenate([fields, jnp.zeros((LANES - ROUTE_FIELDS, tm), F32)], axis=0)
    route_ref[...] = jnp.transpose(padded)


def _merge_router(l, pool, attn, x2d, w_out, g_ffn, wr, br, cnt_in, tm):
    t = x2d.shape[0]
    utri = (jnp.arange(tm)[:, None] < jnp.arange(tm)[None, :]).astype(BF16)
    row = lambda i: (i, 0)
    lay = lambda i: (l, 0, 0)
    return pl.pallas_call(
        _merge_router_kernel,
        grid=(t // tm,),
        in_specs=[
            pl.BlockSpec((tm, POOL_W), row),
            pl.BlockSpec((tm, Q_W), row),
            pl.BlockSpec((tm, D_MODEL), row),
            pl.BlockSpec((None, D_MODEL, D_MODEL), lay),
            pl.BlockSpec((None, 1, D_MODEL), lay),
            pl.BlockSpec((None, D_MODEL, LANES), lay),
            pl.BlockSpec((None, 1, LANES), lay),
            pl.BlockSpec((tm, tm), lambda i: (0, 0)),
            pl.BlockSpec((N_EXPERTS, 1), lambda i: (0, 0)),
        ],
        out_specs=[
            pl.BlockSpec((tm, D_MODEL), row),
            pl.BlockSpec((tm, D_MODEL // 2), row),
            pl.BlockSpec((tm, LANES), row),
            pl.BlockSpec((ROUTE_FIELDS, tm), lambda i: (0, i)),
            pl.BlockSpec((N_EXPERTS, 1), lambda i: (0, 0)),
        ],
        out_shape=[
            jax.ShapeDtypeStruct((t, D_MODEL), F32),
            jax.ShapeDtypeStruct((t, D_MODEL // 2), jnp.int32),
            jax.ShapeDtypeStruct((t, LANES), F32),
            jax.ShapeDtypeStruct((ROUTE_FIELDS, t), F32),
            jax.ShapeDtypeStruct((N_EXPERTS, 1), F32),
        ],
        scratch_shapes=[pltpu.VMEM((tm, D_MODEL), F32), pltpu.VMEM((tm, LANES), F32)],
        compiler_params=pltpu.CompilerParams(
            dimension_semantics=("arbitrary",), vmem_limit_bytes=VMEM_LIMIT),
        name="merge_router",
    )(pool, attn, x2d, w_out, g_ffn, wr, br, utri, cnt_in)


def _moe_kernel(be_ref, rv_ref, nx_ref, sl_ref, xd_ref, wg_hbm, wu_hbm, wd_hbm, yd_ref,
                wg_f, wu_f, wd_f, wg_s, wu_s, wd_s, sem, *, layer):
    step = pl.program_id(0)

    def weight_copies(e, s):
        return [pltpu.make_async_copy(w_hbm.at[layer, e], w_f.at[s], sem.at[s, n])
                for n, (w_hbm, w_f) in enumerate(((wg_hbm, wg_f), (wu_hbm, wu_f), (wd_hbm, wd_f)))]

    @pl.when(step == 0)
    def _():
        for c in weight_copies(be_ref[0], sl_ref[0]):
            c.start()

    for sub_blk in range(MOE_STEP_BLOCKS):
        i = step * MOE_STEP_BLOCKS + sub_blk
        rows = pl.ds(sub_blk * MOE_BM, MOE_BM)
        expert = be_ref[i]
        changed = (i == 0) | (expert != be_ref[jnp.maximum(i - 1, 0)])
        rows_valid = rv_ref[i]
        slot = sl_ref[i]

        @pl.when(changed)
        def _():
            for c in weight_copies(expert, slot):
                c.wait()

            @pl.when(nx_ref[i] >= 0)
            def _():
                for c in weight_copies(nx_ref[i], 1 - slot):
                    c.start(priority=1)

            wg_s[...] = wg_f[slot].astype(BF16)
            wu_s[...] = wu_f[slot].astype(BF16)
            wd_s[...] = wd_f[slot].astype(BF16)

        @pl.when(rows_valid > 0)
        def _():
            row = lax.broadcasted_iota(jnp.int32, (MOE_BM, D_MODEL // 2), 0)
            x = _unpack_bf16_pairs(jnp.where(row < rows_valid, xd_ref[rows, :], 0)).astype(BF16)
            gate = jnp.dot(x, wg_s[...], preferred_element_type=F32)
            up = jnp.dot(x, wu_s[...], preferred_element_type=F32)
            act = (gate * jax.nn.sigmoid(gate) * up).astype(BF16)
            y = jnp.dot(act, wd_s[...], preferred_element_type=F32)
            yd_ref[rows, :] = _pack_bf16_pairs(y.astype(BF16))

        @pl.when(rows_valid <= 0)
        def _():
            yd_ref[rows, :] = jnp.zeros((MOE_BM, D_MODEL // 2), jnp.int32)


def _moe_experts(l, block_e, rows_valid, next_e, slot, xd, w_gate, w_up, w_down):
    n_blocks = xd.shape[0] // MOE_BM
    step_rows = MOE_STEP_BLOCKS * MOE_BM
    row = lambda i, *_: (i, 0)
    return pl.pallas_call(
        functools.partial(_moe_kernel, layer=l),
        grid_spec=pltpu.PrefetchScalarGridSpec(
            num_scalar_prefetch=4,
            grid=(n_blocks // MOE_STEP_BLOCKS,),
            in_specs=[
                pl.BlockSpec((step_rows, D_MODEL // 2), row),
                pl.BlockSpec(memory_space=pl.ANY),
                pl.BlockSpec(memory_space=pl.ANY),
                pl.BlockSpec(memory_space=pl.ANY),
            ],
            out_specs=pl.BlockSpec((step_rows, D_MODEL // 2), row),
            scratch_shapes=[
                pltpu.VMEM((2, D_MODEL, EXPERT_FF), F32),
                pltpu.VMEM((2, D_MODEL, EXPERT_FF), F32),
                pltpu.VMEM((2, EXPERT_FF, D_MODEL), F32),
                pltpu.VMEM((D_MODEL, EXPERT_FF), BF16),
                pltpu.VMEM((D_MODEL, EXPERT_FF), BF16),
                pltpu.VMEM((EXPERT_FF, D_MODEL), BF16),
                pltpu.SemaphoreType.DMA((2, 3)),
            ],
        ),
        out_shape=jax.ShapeDtypeStruct((n_blocks * MOE_BM, D_MODEL // 2), jnp.int32),
        compiler_params=pltpu.CompilerParams(
            dimension_semantics=("arbitrary",), vmem_limit_bytes=VMEM_LIMIT),
        name="moe_experts",
    )(block_e, rows_valid, next_e, slot, xd, w_gate, w_up, w_down)


def _sc_worker_id():
    return lax.axis_index("s") * SC_CORES + lax.axis_index("c")


def _sc_dispatch(hp, hs, dest_p, dest_s, n_rows):
    tp, width = hp.shape
    per_w = tp // SC_WORKERS
    n_ch = per_w // DISP_CH
    n_sw = hs.shape[0] // SAMPLE_CH
    mesh = plsc.VectorSubcoreMesh(core_axis_name="c", subcore_axis_name="s")

    @functools.partial(
        pl.kernel, mesh=mesh,
        out_type=jax.ShapeDtypeStruct((n_rows, width), jnp.int32),
        scratch_types=[
            pltpu.VMEM((2, n_ch, DISP_CH), jnp.int32),
            pltpu.VMEM((2, 1, SAMPLE_CH), jnp.int32),
            pltpu.VMEM((2, DISP_CH, width), jnp.int32),
            pltpu.SemaphoreType.DMA((2,)),
            pltpu.SemaphoreType.DMA((2,)),
        ],
        name="sc_dispatch",
    )
    def k(hp_hbm, hs_hbm, dp_hbm, ds_hbm, xd_hbm, idx_v, idxs_v, bufs, rsem, wsem):
        wid = _sc_worker_id()
        base = wid * per_w
        for kk in range(2):
            pltpu.sync_copy(dp_hbm.at[kk, wid], idx_v.at[kk])
        reads = [pltpu.make_async_copy(hp_hbm.at[pl.ds(base + j * DISP_CH, DISP_CH)],
                                       bufs.at[j % 2], rsem.at[j % 2]) for j in range(n_ch)]
        reads[0].start()
        for j in range(n_ch):
            if j + 1 < n_ch:
                reads[j + 1].start()
            reads[j].wait()
            writes = [pltpu.make_async_copy(bufs.at[j % 2], xd_hbm.at[idx_v.at[kk, j]], wsem.at[kk])
                      for kk in range(2)]
            for w in writes:
                w.start()
            for w in writes:
                w.wait()

        @pl.when(wid < n_sw)
        def _():
            rows = bufs.at[0, pl.ds(0, SAMPLE_CH)]
            for kk in range(2):
                pltpu.sync_copy(ds_hbm.at[kk, wid], idxs_v.at[kk])
            pltpu.sync_copy(hs_hbm.at[pl.ds(wid * SAMPLE_CH, SAMPLE_CH)], rows)
            for kk in range(2):
                pltpu.sync_copy(rows, xd_hbm.at[idxs_v.at[kk, 0]])

    return k(hp, hs, dest_p, dest_s)


def _sc_combine_gather(yd, dest_p, dest_s, tp, ts):
    width = yd.shape[1]
    per_w = tp // SC_WORKERS
    n_ch = per_w // COMB_CH
    n_sw = ts // SAMPLE_CH
    mesh = plsc.VectorSubcoreMesh(core_axis_name="c", subcore_axis_name="s")

    @functools.partial(
        pl.kernel, mesh=mesh,
        out_type=jax.ShapeDtypeStruct((2, tp + ts, width), yd.dtype),
        scratch_types=[
            pltpu.VMEM((2, n_ch, COMB_CH), jnp.int32),
            pltpu.VMEM((2, 1, SAMPLE_CH), jnp.int32),
            pltpu.VMEM((2, COMB_CH, width), yd.dtype),
            pltpu.SemaphoreType.DMA((2,)),
            pltpu.SemaphoreType.DMA((2,)),
        ],
        name="sc_combine_gather",
    )
    def k(yd_hbm, dp_hbm, ds_hbm, g_hbm, idx_v, idxs_v, bufs, gsem, wsem):
        wid = _sc_worker_id()
        base = wid * per_w
        for kk in range(2):
            pltpu.sync_copy(dp_hbm.at[kk, wid], idx_v.at[kk])
        items = [(kk, j) for kk in range(2) for j in range(n_ch)]
        gathers = [pltpu.make_async_copy(yd_hbm.at[idx_v.at[kk, j]], bufs.at[n % 2], gsem.at[n % 2])
                   for n, (kk, j) in enumerate(items)]
        gathers[0].start()
        for n, (kk, j) in enumerate(items):
            if n + 1 < len(items):
                gathers[n + 1].start()
            gathers[n].wait()
            w = pltpu.make_async_copy(bufs.at[n % 2], g_hbm.at[kk, pl.ds(base + j * COMB_CH, COMB_CH)],
                                      wsem.at[n % 2])
            w.start()
            w.wait()

        @pl.when(wid < n_sw)
        def _():
            for kk in range(2):
                pltpu.sync_copy(ds_hbm.at[kk, wid], idxs_v.at[kk])
            for kk in range(2):
                rows = bufs.at[kk, pl.ds(0, SAMPLE_CH)]
                pltpu.sync_copy(yd_hbm.at[idxs_v.at[kk, 0]], rows)
                pltpu.sync_copy(rows, g_hbm.at[kk, pl.ds(tp + wid * SAMPLE_CH, SAMPLE_CH)])

    return k(yd, dest_p, dest_s)


def _combine_kernel(x1_ref, g_ref, route_ref, x2_ref):
    w1 = route_ref[:, 2:3]
    w2 = route_ref[:, 3:4]
    x2_ref[...] = x1_ref[...] + _unpack_bf16_pairs(g_ref[0]) * w1 + _unpack_bf16_pairs(g_ref[1]) * w2


def _combine(x1, g, route, row0, tm):
    t = x1.shape[0]
    blk0 = row0 // tm
    row = lambda i: (i, 0)
    return pl.pallas_call(
        _combine_kernel,
        grid=(t // tm,),
        in_specs=[
            pl.BlockSpec((tm, D_MODEL), row),
            pl.BlockSpec((2, tm, D_MODEL // 2), lambda i: (0, blk0 + i, 0)),
            pl.BlockSpec((tm, LANES), row),
        ],
        out_specs=pl.BlockSpec((tm, D_MODEL), row),
        out_shape=jax.ShapeDtypeStruct((t, D_MODEL), F32),
        compiler_params=pltpu.CompilerParams(
            dimension_semantics=("arbitrary",), vmem_limit_bytes=VMEM_LIMIT),
        name="combine",
    )(x1, g, route)


def _dest_layout(dest, workers, chunk):
    t = dest.shape[1]
    return dest.reshape(2, workers, t // (workers * chunk), chunk)


def _hier_moe(l, h2p, h2s, route_p, route_s, route_tp, route_ts, counts, x1p, x1s, w_gate, w_up, w_down):
    tp, ts = h2p.shape[0], h2s.shape[0]
    n_assign = 2 * (tp + ts)
    n_blocks = -(-n_assign // MOE_BM) + N_EXPERTS
    n_blocks = -(-n_blocks // MOE_STEP_BLOCKS) * MOE_STEP_BLOCKS
    pcounts = (counts + MOE_BM - 1) // MOE_BM * MOE_BM
    pends = jnp.cumsum(pcounts)
    poffsets = pends - pcounts
    starts = jnp.arange(n_blocks, dtype=jnp.int32) * MOE_BM
    block_e = jnp.minimum(jnp.sum((pends[None, :] <= starts[:, None]).astype(jnp.int32), axis=1),
                          N_EXPERTS - 1)
    experts = jnp.arange(N_EXPERTS, dtype=jnp.int32)

    def lookup(table, idx):
        return jnp.sum(jnp.where(idx[..., None] == experts, table, 0), axis=-1)

    rows_valid = jnp.clip(lookup(poffsets + counts, block_e) - starts, 0, MOE_BM).astype(jnp.int32)
    used = counts > 0
    last_e = jnp.max(jnp.where(used, jnp.arange(N_EXPERTS, dtype=jnp.int32), 0))
    block_e = jnp.where(rows_valid > 0, block_e, last_e).astype(jnp.int32)
    later = used[None, :] & (jnp.arange(N_EXPERTS)[None, :] > jnp.arange(N_EXPERTS)[:, None])
    next_used = jnp.min(jnp.where(later, jnp.arange(N_EXPERTS, dtype=jnp.int32)[None, :], N_EXPERTS), axis=1)
    next_used = jnp.where(next_used >= N_EXPERTS, -1, next_used).astype(jnp.int32)
    slot_of = ((jnp.cumsum(used.astype(jnp.int32)) - 1) & 1).astype(jnp.int32)
    next_e = lookup(next_used, block_e)
    slot = lookup(slot_of, block_e)

    def dest_of(route_t):
        return lookup(poffsets, route_t[0:2].astype(jnp.int32)) + route_t[4:6].astype(jnp.int32)

    dest_p, dest_s = dest_of(route_tp), dest_of(route_ts)
    n_sw = ts // SAMPLE_CH
    xd = _sc_dispatch(h2p, h2s, _dest_layout(dest_p, SC_WORKERS, DISP_CH),
                      _dest_layout(dest_s, n_sw, SAMPLE_CH), n_blocks * MOE_BM)
    yd = _moe_experts(l, block_e, rows_valid, next_e, slot, xd, w_gate, w_up, w_down)
    g = _sc_combine_gather(yd, _dest_layout(dest_p, SC_WORKERS, COMB_CH),
                           _dest_layout(dest_s, n_sw, SAMPLE_CH), tp, ts)
    return _combine(x1p, g, route_p, 0, TM_PROJ), _combine(x1s, g, route_s, tp, ts)


def kernel(x_prompt, x_sample, state_pool, cache_k_win, cache_v_win, norm_attn_g, w_in, pool_w, pool_scale, q_norm_g, k_norm_g, attn_sinks, w_out, norm_ffn_g, router_group_w, router_group_b, router_expert_w, router_expert_b, w_gate, w_up, w_down):
    n_p, t_p, d = x_prompt.shape
    n_s, t_s, _ = x_sample.shape
    depth = w_in.shape[0]
    lw_s = cache_k_win.shape[2]
    assert t_s == 1 and lw_s == WINDOW and d == D_MODEL
    assert t_p % TM_PROJ == 0 and t_p >= WINDOW
    past_len = 16384

    w_in_b = w_in.astype(BF16)
    w_out_b = w_out.astype(BF16)
    g_attn = norm_attn_g.reshape(depth, 1, D_MODEL)
    g_ffn = norm_ffn_g.reshape(depth, 1, D_MODEL)
    qg = (jnp.tile(q_norm_g, (1, N_HEADS)) * ATTN_SCALE).reshape(depth, 1, Q_W)
    kg = jnp.tile(k_norm_g, (1, N_KV_HEADS)).reshape(depth, 1, KV_W)
    seg = jnp.arange(256) // HEAD_DIM
    bd = jnp.where(seg[:, None] == seg[None, :], 1.0 / HEAD_DIM, 0.0).astype(BF16)
    wp = jnp.zeros((depth, 2, 256, 256), F32)
    for p in range(2):
        wp = wp.at[:, p, :POOL_GC, :POOL_GC].set(pool_w[:, 2 * p])
        wp = wp.at[:, p, POOL_GC:, POOL_GC:].set(pool_w[:, 2 * p + 1])
    wp = wp.astype(BF16)
    ps = pool_scale.reshape(depth, 1, POOL_W)
    wr = jnp.zeros((depth, D_MODEL, LANES), F32)
    wr = wr.at[:, :, :N_EXPERTS].set(router_expert_w)
    wr = wr.at[:, :, GROUP_LANE0:GROUP_LANE0 + N_EXPERT_GROUPS].set(router_group_w)
    wr = wr.astype(BF16)
    br = jnp.zeros((depth, 1, LANES), F32)
    br = br.at[:, 0, :N_EXPERTS].set(router_expert_b)
    br = br.at[:, 0, GROUP_LANE0:GROUP_LANE0 + N_EXPERT_GROUPS].set(router_group_b)

    slopes = jnp.exp2(-8.0 * jnp.arange(1, N_HEADS + 1, dtype=F32) / N_HEADS)
    bias_p = _prompt_bias_t()
    dist_s = (WINDOW - 1) - jnp.arange(WINDOW, dtype=F32)
    bias_s = -slopes[:, None] * dist_s[None, :]
    state2d = state_pool.reshape(depth, n_s, POOL_STATE * POOL_W)
    ck_all = cache_k_win.reshape(depth, n_s, lw_s, KV_W)
    cv_all = cache_v_win.reshape(depth, n_s, lw_s, KV_W)

    xp = x_prompt.reshape(n_p * t_p, D_MODEL)
    xs = x_sample.reshape(n_s, D_MODEL)
    lw_p = min(WINDOW, t_p)
    pool_p, kp_new, vp_new, pool_s, ks_new, vs_new = [], [], [], [], [], []
    zero_cnt = jnp.zeros((N_EXPERTS, 1), F32)
    for l in range(depth):
        sinks = attn_sinks[l]
        pool_o, q, k, vt, utail, ktail, vtail = _proj_pool_prompt(
            l, xp, n_p, t_p, g_attn, w_in_b, qg, kg, bd, wp, ps)
        attn_o = _attn_prompt(q, k, vt, bias_p, sinks, n_p, t_p)
        x1p, h2p, route_p, route_tp, cnt_p = _merge_router(
            l, pool_o, attn_o, xp, w_out_b, g_ffn, wr, br, zero_cnt, TM_PROJ)
        pool_p.append(utail[:, 16 - POOL_STATE:, :])
        kp_new.append(ktail)
        vp_new.append(vtail)
        pool_so, attn_so, u_s, kc_s, vc_s = _sample_mixer(
            l, xs, g_attn, w_in_b, qg, kg, bd, wp, ps, state2d, ck_all, cv_all,
            sinks.reshape(N_HEADS, 1), bias_s, past_len)
        x1s, h2s, route_s, route_ts, cnt_all = _merge_router(
            l, pool_so, attn_so, xs, w_out_b, g_ffn, wr, br, cnt_p, n_s)
        pool_s.append(jnp.concatenate([state_pool[l][:, 1:], u_s[:, None, :]], axis=1))
        ks_new.append(kc_s)
        vs_new.append(vc_s)
        counts = cnt_all[:, 0].astype(jnp.int32)
        xp, xs = _hier_moe(l, h2p, h2s, route_p, route_s, route_tp, route_ts, counts, x1p, x1s,
                           w_gate, w_up, w_down)
    return (xp.reshape(n_p, t_p, D_MODEL), xs.reshape(n_s, t_s, D_MODEL),
            jnp.stack(pool_p),
            jnp.stack(kp_new).reshape(depth, n_p, lw_p, N_KV_HEADS, HEAD_DIM),
            jnp.stack(vp_new).reshape(depth, n_p, lw_p, N_KV_HEADS, HEAD_DIM),
            jnp.stack(pool_s),
            jnp.stack(ks_new).reshape(depth, n_s, lw_s, N_KV_HEADS, HEAD_DIM),
            jnp.stack(vs_new).reshape(depth, n_s, lw_s, N_KV_HEADS, HEAD_DIM))
```

```python
import functools

import jax
import jax.numpy as jnp
from jax import lax
from jax.experimental import pallas as pl
from jax.experimental.pallas import tpu as pltpu
from jax.experimental.pallas import tpu_sc as plsc

D_MODEL = 1024
POOL_W = 512
POOL_WINDOWS = (2, 4, 8, 16)
POOL_GC = 128
POOL_STATE = 15
HEAD_DIM = 64
N_HEADS = 8
N_KV_HEADS = 2
GQA_GROUP = 4
Q_W = 512
KV_W = 128
D_IN = POOL_W + Q_W + 2 * KV_W
WINDOW = 128
ATTN_SCALE = HEAD_DIM ** -0.5
N_EXPERT_GROUPS = 4
EXPERTS_PER_GROUP = 8
N_EXPERTS = 32
EXPERT_FF = 512
EPS = 1e-6

LANES = 128
HALO = 32
TM_PROJ = 512
MERGE_CHUNKS = 2
ATTN_QB = 4
MOE_BM = 256
MOE_STEP_BLOCKS = 4
GROUP_LANE0 = 32
ROUTE_FIELDS = 8
SC_CORES = 2
SC_SUBCORES = 16
SC_WORKERS = SC_CORES * SC_SUBCORES
DISP_CH = 64
COMB_CH = 64
SAMPLE_CH = 32
VMEM_LIMIT = 48 * 1024 * 1024

BF16 = jnp.bfloat16
F32 = jnp.float32


def _pack_bf16_pairs(h):
    w = h.shape[1] // 2
    hi = lax.bitcast_convert_type(h[:, :w].astype(F32), jnp.uint32)
    lo = lax.bitcast_convert_type(h[:, w:].astype(F32), jnp.uint32)
    return lax.bitcast_convert_type(hi | (lo >> 16), jnp.int32)


def _unpack_bf16_pairs(words):
    u = lax.bitcast_convert_type(words, jnp.uint32)
    hi = lax.bitcast_convert_type(u & jnp.uint32(0xFFFF0000), F32)
    lo = lax.bitcast_convert_type(u << 16, F32)
    return jnp.concatenate([hi, lo], axis=-1)


def _segment_mean_sq(a, bd):
    w = a.shape[1]
    return jnp.dot((a * a).astype(BF16), bd[:w, :w], preferred_element_type=F32)


def _rms_bf16(x, g):
    ms = jnp.mean(x * x, axis=-1, keepdims=True)
    return (x * lax.rsqrt(ms + EPS) * g).astype(BF16)


def _qk_norm(q, k, qg, kg, bd):
    qn = []
    for c in range(Q_W // 256):
        qc = q[:, c * 256:(c + 1) * 256]
        qn.append(qc * lax.rsqrt(_segment_mean_sq(qc, bd) + EPS))
    qn = jnp.concatenate(qn, axis=-1) * qg
    kn = k * lax.rsqrt(_segment_mean_sq(k, bd) + EPS) * kg
    return qn, kn


def _project(x, g, w_in, qg, kg, bd):
    z = jnp.dot(_rms_bf16(x, g), w_in, preferred_element_type=F32)
    u = z[:, :POOL_W]
    q = z[:, POOL_W:POOL_W + Q_W]
    k = z[:, POOL_W + Q_W:POOL_W + Q_W + KV_W]
    v = z[:, POOL_W + Q_W + KV_W:]
    qn, kn = _qk_norm(q, k, qg, kg, bd)
    return u, qn, kn, v


def _pool_project(d_groups, wp_ref, ps):
    outs = []
    for p in range(2):
        dp = jnp.concatenate([d_groups[2 * p], d_groups[2 * p + 1]], axis=-1).astype(BF16)
        y = jnp.dot(dp, wp_ref[p], preferred_element_type=F32)
        outs.append(y * ps[:, p * 256:(p + 1) * 256])
    return jnp.concatenate(outs, axis=-1)


def _proj_pool_kernel(x_ref, g_ref, win_ref, qg_ref, kg_ref, bd_ref, wp_ref, ps_ref,
                      pool_ref, q_ref, k_ref, vt_ref, utail_ref, ktail_ref, vtail_ref,
                      ext_ref, sa_ref, sb_ref, zq_ref, *, tm, n_j):
    j = pl.program_id(1)

    @pl.when(j == 0)
    def _():
        ext_ref[0:HALO, :] = jnp.zeros((HALO, POOL_W), F32)

    r = tm + HALO
    h = _rms_bf16(x_ref[...], g_ref[...])
    ext_ref[HALO:r, :] = jnp.dot(h, win_ref[:, 0:POOL_W], preferred_element_type=F32)
    zq_ref[...] = jnp.dot(h, win_ref[:, POOL_W:], preferred_element_type=F32)
    u = ext_ref[HALO:r, :]
    sa_ref[8:r, :] = ext_ref[8:r, :] + ext_ref[7:r - 1, :]
    sb_ref[16:r, 128:] = sa_ref[16:r, 128:] + sa_ref[14:r - 2, 128:]
    sa_ref[24:r, 256:] = sb_ref[24:r, 256:] + sb_ref[20:r - 4, 256:]
    sb_ref[32:r, 384:] = sa_ref[32:r, 384:] + sa_ref[24:r - 8, 384:]
    pos1 = j * tm + lax.broadcasted_iota(jnp.int32, (tm, POOL_GC), 0) + 1
    sums = (sa_ref, sb_ref, sa_ref, sb_ref)
    d_groups = []
    for gi, w in enumerate(POOL_WINDOWS):
        sl = slice(gi * POOL_GC, (gi + 1) * POOL_GC)
        cnt = jnp.minimum(pos1, w).astype(F32)
        d_groups.append(sums[gi][HALO:r, sl] / cnt - u[:, sl])
    pool_ref[...] = _pool_project(d_groups, wp_ref, ps_ref[...]).astype(BF16)
    ext_ref[16:HALO, :] = ext_ref[tm + 16:r, :]

    qn, kn = _qk_norm(zq_ref[:, 0:Q_W], zq_ref[:, Q_W:Q_W + KV_W], qg_ref[...], kg_ref[...], bd_ref[...])
    v = zq_ref[:, Q_W + KV_W:]
    q_ref[...] = qn.astype(BF16)
    k_ref[...] = kn.astype(BF16)
    vt_ref[...] = jnp.transpose(v).astype(BF16)

    @pl.when(j == n_j - 1)
    def _():
        utail_ref[...] = u[tm - 16:, :]
        ktail_ref[...] = kn[tm - WINDOW:, :]
        vtail_ref[...] = v[tm - WINDOW:, :]


def _proj_pool_combine_kernel(x1_ref, gath_ref, route_ref, *rest, tm, n_j):
    x2_ref = rest[7]
    _combine_kernel(x1_ref, gath_ref, route_ref, x2_ref)
    _proj_pool_kernel(x2_ref, *rest[:7], *rest[8:], tm=tm, n_j=n_j)


def _proj_pool_prompt(l, x_in, n_seq, seq, g_attn, w_in, qg, kg, bd, wp, ps):
    tm = TM_PROJ
    n_j = seq // tm
    t = n_seq * seq
    row = lambda b, j: (b * n_j + j, 0)
    lay = lambda b, j: (l, 0, 0)
    fused = isinstance(x_in, tuple)
    if fused:
        kern = _proj_pool_combine_kernel
        x_args = list(x_in)
        x_specs = [pl.BlockSpec((tm, D_MODEL), row),
                   pl.BlockSpec((2, tm, D_MODEL // 2), lambda b, j: (0, b * n_j + j, 0)),
                   pl.BlockSpec((tm, LANES), row)]
        x_out_specs = [pl.BlockSpec((tm, D_MODEL), row)]
        x_out_shape = [jax.ShapeDtypeStruct((t, D_MODEL), F32)]
    else:
        kern = _proj_pool_kernel
        x_args = [x_in]
        x_specs = [pl.BlockSpec((tm, D_MODEL), row)]
        x_out_specs, x_out_shape = [], []
    return pl.pallas_call(
        functools.partial(kern, tm=tm, n_j=n_j),
        grid=(n_seq, n_j),
        in_specs=x_specs + [
            pl.BlockSpec((None, 1, D_MODEL), lay),
            pl.BlockSpec((None, D_MODEL, D_IN), lay),
            pl.BlockSpec((None, 1, Q_W), lay),
            pl.BlockSpec((None, 1, KV_W), lay),
            pl.BlockSpec((256, 256), lambda b, j: (0, 0)),
            pl.BlockSpec((None, 2, 256, 256), lambda b, j: (l, 0, 0, 0)),
            pl.BlockSpec((None, 1, POOL_W), lay),
        ],
        out_specs=x_out_specs + [
            pl.BlockSpec((tm, POOL_W), row),
            pl.BlockSpec((tm, Q_W), row),
            pl.BlockSpec((tm, KV_W), row),
            pl.BlockSpec((KV_W, tm), lambda b, j: (0, b * n_j + j)),
            pl.BlockSpec((None, 16, POOL_W), lambda b, j: (b, 0, 0)),
            pl.BlockSpec((None, WINDOW, KV_W), lambda b, j: (b, 0, 0)),
            pl.BlockSpec((None, WINDOW, KV_W), lambda b, j: (b, 0, 0)),
        ],
        out_shape=x_out_shape + [
            jax.ShapeDtypeStruct((t, POOL_W), BF16),
            jax.ShapeDtypeStruct((t, Q_W), BF16),
            jax.ShapeDtypeStruct((t, KV_W), BF16),
            jax.ShapeDtypeStruct((KV_W, t), BF16),
            jax.ShapeDtypeStruct((n_seq, 16, POOL_W), F32),
            jax.ShapeDtypeStruct((n_seq, WINDOW, KV_W), F32),
            jax.ShapeDtypeStruct((n_seq, WINDOW, KV_W), F32),
        ],
        scratch_shapes=[pltpu.VMEM((tm + HALO, POOL_W), F32)] * 3 + [pltpu.VMEM((tm, Q_W + 2 * KV_W), F32)],
        compiler_params=pltpu.CompilerParams(
            dimension_semantics=("arbitrary", "arbitrary"), vmem_limit_bytes=VMEM_LIMIT),
        name="proj_pool_prompt",
    )(*x_args, g_attn, w_in, qg, kg, bd, wp, ps)


def _attn_kernel(sink_ref, q_ref, kp_ref, kc_ref, vtp_ref, vtc_ref, bias_ref, o_ref, s_ref):
    j = pl.program_id(1)
    kk_all = jnp.concatenate([kp_ref[...], kc_ref[...]], axis=0)
    vt_all = jnp.concatenate([vtp_ref[...], vtc_ref[...]], axis=1)
    from_prev = (lax.broadcasted_iota(jnp.int32, (WINDOW, WINDOW), 0)
                 > lax.broadcasted_iota(jnp.int32, (WINDOW, WINDOW), 1))
    units = [(blk, kv) for blk in range(ATTN_QB) for kv in range(N_KV_HEADS)]

    def scores(n):
        blk, kv = units[n]
        q = q_ref[blk * WINDOW:(blk + 1) * WINDOW, :]
        kk = kk_all[blk * WINDOW:(blk + 2) * WINDOW, kv * HEAD_DIM:(kv + 1) * HEAD_DIM]
        heads = range(kv * GQA_GROUP, (kv + 1) * GQA_GROUP)
        q_rows = jnp.concatenate([q[:, h * HEAD_DIM:(h + 1) * HEAD_DIM] for h in heads], axis=0)
        s_ref[n % 2] = lax.dot_general(kk, q_rows, (((1,), (1,)), ((), ())), preferred_element_type=F32)

    scores(0)
    outs = []
    for n, (blk, kv) in enumerate(units):
        if n + 1 < len(units):
            scores(n + 1)
        vt_kv = vt_all[kv * HEAD_DIM:(kv + 1) * HEAD_DIM, blk * WINDOW:(blk + 2) * WINDOW]
        variant = jnp.minimum(j, 1) if blk == 0 else 1
        for g in range(GQA_GROUP):
            h = kv * GQA_GROUP + g
            s = jnp.where(from_prev, s_ref[n % 2, 0:WINDOW, g * WINDOW:(g + 1) * WINDOW],
                          s_ref[n % 2, WINDOW:, g * WINDOW:(g + 1) * WINDOW]) + bias_ref[variant, h]
            sink = sink_ref[h]
            m = jnp.maximum(jnp.max(s, axis=0, keepdims=True), sink)
            p = jnp.exp(s - m)
            denom = jnp.sum(p, axis=0, keepdims=True) + jnp.exp(sink - m)
            p_keys = jnp.concatenate([jnp.where(from_prev, p, 0.0), jnp.where(from_prev, 0.0, p)], axis=0)
            o_t = jnp.dot(vt_kv, p_keys.astype(BF16), preferred_element_type=F32)
            outs.append(o_t / denom)
        if kv == N_KV_HEADS - 1:
            o_ref[blk * WINDOW:(blk + 1) * WINDOW, :] = jnp.transpose(jnp.concatenate(outs, axis=0)).astype(BF16)
            outs = []


def _attn_prompt(q, k, vt, bias_t, sinks, n_seq, seq):
    tq = ATTN_QB * WINDOW
    nj = seq // tq
    t = n_seq * seq
    cur = lambda b, j: (b * nj + j, 0)
    prev = lambda b, j: (jnp.maximum((b * nj + j) * ATTN_QB - 1, 0), 0)
    cur_t = lambda b, j: (0, b * nj + j)
    prev_t = lambda b, j: (0, jnp.maximum((b * nj + j) * ATTN_QB - 1, 0))
    return pl.pallas_call(
        _attn_kernel,
        grid=(n_seq, nj),
        in_specs=[
            pl.BlockSpec(memory_space=pltpu.SMEM),
            pl.BlockSpec((tq, Q_W), cur),
            pl.BlockSpec((WINDOW, KV_W), prev),
            pl.BlockSpec((tq, KV_W), cur),
            pl.BlockSpec((KV_W, WINDOW), prev_t),
            pl.BlockSpec((KV_W, tq), cur_t),
            pl.BlockSpec((2, N_HEADS, WINDOW, WINDOW), lambda b, j: (0, 0, 0, 0)),
        ],
        out_specs=pl.BlockSpec((tq, Q_W), cur),
        out_shape=jax.ShapeDtypeStruct((t, Q_W), BF16),
        scratch_shapes=[pltpu.VMEM((2, 2 * WINDOW, GQA_GROUP * WINDOW), F32)],
        compiler_params=pltpu.CompilerParams(
            dimension_semantics=("arbitrary", "arbitrary"), vmem_limit_bytes=VMEM_LIMIT),
        name="attn_prompt",
    )(sinks, q, k, k, vt, vt, bias_t)


def _prompt_bias_t():
    r = jnp.arange(WINDOW, dtype=jnp.int32)[None, :]
    c = jnp.arange(WINDOW, dtype=jnp.int32)[:, None]
    from_prev = c > r
    dist = r - c + jnp.where(from_prev, WINDOW, 0)
    slopes = jnp.exp2(-8.0 * jnp.arange(1, N_HEADS + 1, dtype=F32) / N_HEADS)
    later = -slopes[:, None, None] * dist.astype(F32)[None]
    first = jnp.where(from_prev[None], -jnp.inf, later)
    return jnp.stack([first, later])


def _sample_kernel(x_ref, g_ref, win_ref, qg_ref, kg_ref, bd_ref, wp_ref, ps_ref,
                   st_ref, ck_ref, cv_ref, sink_ref, bias_ref, perm_ref,
                   pool_ref, attn_ref, u_ref, kc_ref, vc_ref, *, ns, pos0):
    u, qn, kn, v = _project(x_ref[...], g_ref[...], win_ref[...], qg_ref[...], kg_ref[...], bd_ref[...])
    u_ref[...] = u
    kc_ref[:, 0:WINDOW - 1, :] = ck_ref[:, 1:WINDOW, :]
    vc_ref[:, 0:WINDOW - 1, :] = cv_ref[:, 1:WINDOW, :]
    for n in range(ns):
        kc_ref[n, WINDOW - 1:WINDOW, :] = kn[n:n + 1, :]
        vc_ref[n, WINDOW - 1:WINDOW, :] = v[n:n + 1, :]

    d_groups = []
    for gi, w in enumerate(POOL_WINDOWS):
        lo = gi * POOL_GC
        acc = u[:, lo:lo + POOL_GC]
        for back in range(1, w):
            off = (POOL_STATE - back) * POOL_W + lo
            acc = acc + st_ref[:, off:off + POOL_GC]
        d_groups.append(acc / float(min(pos0 + 1, w)) - u[:, lo:lo + POOL_GC])
    pool_ref[...] = _pool_project(d_groups, wp_ref, ps_ref[...]).astype(BF16)

    zeros = jnp.zeros((ns, HEAD_DIM), F32)
    stacked = []
    for h in range(N_HEADS):
        piece = qn[:, h * HEAD_DIM:(h + 1) * HEAD_DIM]
        pair = [piece, zeros] if h < GQA_GROUP else [zeros, piece]
        stacked.append(jnp.concatenate(pair, axis=-1))
    q_hn = jnp.concatenate(stacked, axis=0).astype(BF16)
    q_nh = jnp.dot(perm_ref[0], q_hn, preferred_element_type=F32).astype(BF16)

    keys = kc_ref[...].reshape(ns * WINDOW, KV_W).astype(BF16)
    vals = vc_ref[...].reshape(ns * WINDOW, KV_W).astype(BF16)
    s_all = lax.dot_general(q_nh, keys, (((1,), (1,)), ((), ())), preferred_element_type=F32)
    sink = sink_ref[...]
    bias = bias_ref[...]
    zero_blk = jnp.zeros((N_HEADS, WINDOW), F32)
    p_rows = []
    for n in range(ns):
        s = s_all[n * N_HEADS:(n + 1) * N_HEADS, n * WINDOW:(n + 1) * WINDOW] + bias
        m = jnp.maximum(jnp.max(s, axis=-1, keepdims=True), sink)
        p = jnp.exp(s - m)
        denom = jnp.sum(p, axis=-1, keepdims=True) + jnp.exp(sink - m)
        p_rows.append(jnp.concatenate([zero_blk] * n + [p / denom] + [zero_blk] * (ns - 1 - n), axis=-1))
    p_blockdiag = jnp.concatenate(p_rows, axis=0).astype(BF16)
    o_nh = jnp.dot(p_blockdiag, vals, preferred_element_type=F32).astype(BF16)
    o_hn = jnp.dot(perm_ref[1], o_nh, preferred_element_type=F32)
    pieces = []
    for h in range(N_HEADS):
        kv = h // GQA_GROUP
        pieces.append(o_hn[h * ns:(h + 1) * ns, kv * HEAD_DIM:(kv + 1) * HEAD_DIM])
    attn_ref[...] = jnp.concatenate(pieces, axis=-1).astype(BF16)


def _sample_mixer(l, xs, g_attn, w_in, qg, kg, bd, wp, ps, state2d, ck, cv, sink8, bias_s, pos0):
    n = xs.shape[0]
    ns = 32
    row = lambda i: (i, 0)
    lay = lambda i: (l, 0, 0)
    src = jnp.arange(ns * N_HEADS)
    perm = (((src % N_HEADS) * ns + src // N_HEADS)[:, None] == src[None, :]).astype(BF16)
    perms = jnp.stack([perm, perm.T])
    return pl.pallas_call(
        functools.partial(_sample_kernel, ns=ns, pos0=pos0),
        grid=(n // ns,),
        in_specs=[
            pl.BlockSpec((ns, D_MODEL), row),
            pl.BlockSpec((None, 1, D_MODEL), lay),
            pl.BlockSpec((None, D_MODEL, D_IN), lay),
            pl.BlockSpec((None, 1, Q_W), lay),
            pl.BlockSpec((None, 1, KV_W), lay),
            pl.BlockSpec((256, 256), lambda i: (0, 0)),
            pl.BlockSpec((None, 2, 256, 256), lambda i: (l, 0, 0, 0)),
            pl.BlockSpec((None, 1, POOL_W), lay),
            pl.BlockSpec((None, ns, POOL_STATE * POOL_W), lambda i: (l, i, 0)),
            pl.BlockSpec((None, ns, WINDOW, KV_W), lambda i: (l, i, 0, 0)),
            pl.BlockSpec((None, ns, WINDOW, KV_W), lambda i: (l, i, 0, 0)),
            pl.BlockSpec((N_HEADS, 1), lambda i: (0, 0)),
            pl.BlockSpec((N_HEADS, WINDOW), lambda i: (0, 0)),
            pl.BlockSpec((2, ns * N_HEADS, ns * N_HEADS), lambda i: (0, 0, 0)),
        ],
        out_specs=[
            pl.BlockSpec((ns, POOL_W), row),
            pl.BlockSpec((ns, Q_W), row),
            pl.BlockSpec((ns, POOL_W), row),
            pl.BlockSpec((ns, WINDOW, KV_W), lambda i: (i, 0, 0)),
            pl.BlockSpec((ns, WINDOW, KV_W), lambda i: (i, 0, 0)),
        ],
        out_shape=[
            jax.ShapeDtypeStruct((n, POOL_W), BF16),
            jax.ShapeDtypeStruct((n, Q_W), BF16),
            jax.ShapeDtypeStruct((n, POOL_W), F32),
            jax.ShapeDtypeStruct((n, WINDOW, KV_W), F32),
            jax.ShapeDtypeStruct((n, WINDOW, KV_W), F32),
        ],
        compiler_params=pltpu.CompilerParams(
            dimension_semantics=("arbitrary",), vmem_limit_bytes=VMEM_LIMIT),
        name="sample_mixer",
    )(xs, g_attn, w_in, qg, kg, bd, wp, ps, state2d, ck, cv, sink8, bias_s, perms)


def _merge_router_kernel(pool_ref, attn_ref, x_ref, wout_ref, g_ref, wr_ref, br_ref, utri_ref, cin_ref,
                         x1_ref, h2_ref, route_ref, route_t_ref, cnt_ref, y_ref, lg_ref):
    i = pl.program_id(0)

    @pl.when(i == 0)
    def _():
        cnt_ref[...] = cin_ref[...]

    tm = x_ref.shape[0]
    rc = tm // MERGE_CHUNKS
    chunks = [slice(ci * rc, (ci + 1) * rc) for ci in range(MERGE_CHUNKS)]
    for rows in chunks:
        y_ref[rows, :] = (jnp.dot(pool_ref[rows, :], wout_ref[0:POOL_W, :], preferred_element_type=F32)
                          + jnp.dot(attn_ref[rows, :], wout_ref[POOL_W:, :], preferred_element_type=F32))
    for rows in chunks:
        x1 = x_ref[rows, :] + y_ref[rows, :]
        x1_ref[rows, :] = x1
        h2 = _rms_bf16(x1, g_ref[...])
        h2_ref[rows, :] = _pack_bf16_pairs(h2)
        lg_ref[rows, :] = jnp.dot(h2, wr_ref[...], preferred_element_type=F32) + br_ref[...]
    logits = lg_ref[...]

    lt = jnp.transpose(logits)
    sub = lax.broadcasted_iota(jnp.int32, (EXPERTS_PER_GROUP, tm), 0)
    neg = -jnp.inf
    big = jnp.int32(EXPERTS_PER_GROUP)
    gl = jnp.where(sub < N_EXPERT_GROUPS, lt[GROUP_LANE0:GROUP_LANE0 + EXPERTS_PER_GROUP, :], neg)
    gmax = jnp.max(gl, axis=0, keepdims=True)
    grp = jnp.min(jnp.where(gl == gmax, sub, big), axis=0, keepdims=True)
    g_w = 1.0 / jnp.sum(jnp.exp(gl - gmax), axis=0, keepdims=True)
    el = lt[(N_EXPERT_GROUPS - 1) * EXPERTS_PER_GROUP:N_EXPERT_GROUPS * EXPERTS_PER_GROUP, :]
    for gi in range(N_EXPERT_GROUPS - 2, -1, -1):
        el = jnp.where(grp == gi, lt[gi * EXPERTS_PER_GROUP:(gi + 1) * EXPERTS_PER_GROUP, :], el)
    v1 = jnp.max(el, axis=0, keepdims=True)
    i1 = jnp.min(jnp.where(el == v1, sub, big), axis=0, keepdims=True)
    el2 = jnp.where(sub == i1, neg, el)
    v2 = jnp.max(el2, axis=0, keepdims=True)
    i2 = jnp.min(jnp.where(el2 == v2, sub, big), axis=0, keepdims=True)
    e21 = jnp.exp(v2 - v1)
    w1 = g_w / (1.0 + e21)
    w2 = g_w * e21 / (1.0 + e21)
    e1 = grp * EXPERTS_PER_GROUP + i1
    e2 = grp * EXPERTS_PER_GROUP + i2

    esub = lax.broadcasted_iota(jnp.int32, (N_EXPERTS, tm), 0)
    oh1 = esub == e1
    oh2 = esub == e2
    c = jnp.where(oh1 | oh2, 1.0, 0.0)
    prefix = jnp.dot(c.astype(BF16), utri_ref[...], preferred_element_type=F32) + cnt_ref[...]
    r1 = jnp.sum(jnp.where(oh1, prefix, 0.0), axis=0, keepdims=True)
    r2 = jnp.sum(jnp.where(oh2, prefix, 0.0), axis=0, keepdims=True)
    cnt_ref[...] = cnt_ref[...] + jnp.sum(c, axis=1, keepdims=True)

    fields = jnp.zeros((ROUTE_FIELDS, tm), F32)
    for idx, val in enumerate((e1.astype(F32), e2.astype(F32), w1, w2, r1, r2)):
        fields = jnp.where(sub == idx, val, fields)
    route_t_ref[...] = fields
    padded = jnp.concatenate([fields, jnp.zeros((LANES - ROUTE_FIELDS, tm), F32)], axis=0)
    route_ref[...] = jnp.transpose(padded)


def _merge_router(l, pool, attn, x2d, w_out, g_ffn, wr, br, cnt_in, tm):
    t = x2d.shape[0]
    utri = (jnp.arange(tm)[:, None] < jnp.arange(tm)[None, :]).astype(BF16)
    row = lambda i: (i, 0)
    lay = lambda i: (l, 0, 0)
    return pl.pallas_call(
        _merge_router_kernel,
        grid=(t // tm,),
        in_specs=[
            pl.BlockSpec((tm, POOL_W), row),
            pl.BlockSpec((tm, Q_W), row),
            pl.BlockSpec((tm, D_MODEL), row),
            pl.BlockSpec((None, D_MODEL, D_MODEL), lay),
            pl.BlockSpec((None, 1, D_MODEL), lay),
            pl.BlockSpec((None, D_MODEL, LANES), lay),
            pl.BlockSpec((None, 1, LANES), lay),
            pl.BlockSpec((tm, tm), lambda i: (0, 0)),
            pl.BlockSpec((N_EXPERTS, 1), lambda i: (0, 0)),
        ],
        out_specs=[
            pl.BlockSpec((tm, D_MODEL), row),
            pl.BlockSpec((tm, D_MODEL // 2), row),
            pl.BlockSpec((tm, LANES), row),
            pl.BlockSpec((ROUTE_FIELDS, tm), lambda i: (0, i)),
            pl.BlockSpec((N_EXPERTS, 1), lambda i: (0, 0)),
        ],
        out_shape=[
            jax.ShapeDtypeStruct((t, D_MODEL), F32),
            jax.ShapeDtypeStruct((t, D_MODEL // 2), jnp.int32),
            jax.ShapeDtypeStruct((t, LANES), F32),
            jax.ShapeDtypeStruct((ROUTE_FIELDS, t), F32),
            jax.ShapeDtypeStruct((N_EXPERTS, 1), F32),
        ],
        scratch_shapes=[pltpu.VMEM((tm, D_MODEL), F32), pltpu.VMEM((tm, LANES), F32)],
        compiler_params=pltpu.CompilerParams(
            dimension_semantics=("arbitrary",), vmem_limit_bytes=VMEM_LIMIT),
        name="merge_router",
    )(pool, attn, x2d, w_out, g_ffn, wr, br, utri, cnt_in)


def _moe_kernel(be_ref, rv_ref, nx_ref, sl_ref, xd_ref, wg_hbm, wu_hbm, wd_hbm, yd_ref,
                wg_f, wu_f, wd_f, wg_s, wu_s, wd_s, sem, *, layer):
    step = pl.program_id(0)

    def weight_copies(e, s):
        return [pltpu.make_async_copy(w_hbm.at[layer, e], w_f.at[s], sem.at[s, n])
                for n, (w_hbm, w_f) in enumerate(((wg_hbm, wg_f), (wu_hbm, wu_f), (wd_hbm, wd_f)))]

    @pl.when(step == 0)
    def _():
        for c in weight_copies(be_ref[0], sl_ref[0]):
            c.start()

    for sub_blk in range(MOE_STEP_BLOCKS):
        i = step * MOE_STEP_BLOCKS + sub_blk
        rows = pl.ds(sub_blk * MOE_BM, MOE_BM)
        expert = be_ref[i]
        changed = (i == 0) | (expert != be_ref[jnp.maximum(i - 1, 0)])
        rows_valid = rv_ref[i]
        slot = sl_ref[i]

        @pl.when(changed)
        def _():
            for c in weight_copies(expert, slot):
                c.wait()

            @pl.when(nx_ref[i] >= 0)
            def _():
                for c in weight_copies(nx_ref[i], 1 - slot):
                    c.start(priority=1)

            wg_s[...] = wg_f[slot].astype(BF16)
            wu_s[...] = wu_f[slot].astype(BF16)
            wd_s[...] = wd_f[slot].astype(BF16)

        @pl.when(rows_valid > 0)
        def _():
            row = lax.broadcasted_iota(jnp.int32, (MOE_BM, D_MODEL // 2), 0)
            x = _unpack_bf16_pairs(jnp.where(row < rows_valid, xd_ref[rows, :], 0)).astype(BF16)
            gate = jnp.dot(x, wg_s[...], preferred_element_type=F32)
            up = jnp.dot(x, wu_s[...], preferred_element_type=F32)
            act = (gate * jax.nn.sigmoid(gate) * up).astype(BF16)
            y = jnp.dot(act, wd_s[...], preferred_element_type=F32)
            yd_ref[rows, :] = _pack_bf16_pairs(y.astype(BF16))

        @pl.when(rows_valid <= 0)
        def _():
            yd_ref[rows, :] = jnp.zeros((MOE_BM, D_MODEL // 2), jnp.int32)


def _moe_experts(l, block_e, rows_valid, next_e, slot, xd, w_gate, w_up, w_down):
    n_blocks = xd.shape[0] // MOE_BM
    step_rows = MOE_STEP_BLOCKS * MOE_BM
    row = lambda i, *_: (i, 0)
    return pl.pallas_call(
        functools.partial(_moe_kernel, layer=l),
        grid_spec=pltpu.PrefetchScalarGridSpec(
            num_scalar_prefetch=4,
            grid=(n_blocks // MOE_STEP_BLOCKS,),
            in_specs=[
                pl.BlockSpec((step_rows, D_MODEL // 2), row),
                pl.BlockSpec(memory_space=pl.ANY),
                pl.BlockSpec(memory_space=pl.ANY),
                pl.BlockSpec(memory_space=pl.ANY),
            ],
            out_specs=pl.BlockSpec((step_rows, D_MODEL // 2), row),
            scratch_shapes=[
                pltpu.VMEM((2, D_MODEL, EXPERT_FF), F32),
                pltpu.VMEM((2, D_MODEL, EXPERT_FF), F32),
                pltpu.VMEM((2, EXPERT_FF, D_MODEL), F32),
                pltpu.VMEM((D_MODEL, EXPERT_FF), BF16),
                pltpu.VMEM((D_MODEL, EXPERT_FF), BF16),
                pltpu.VMEM((EXPERT_FF, D_MODEL), BF16),
                pltpu.SemaphoreType.DMA((2, 3)),
            ],
        ),
        out_shape=jax.ShapeDtypeStruct((n_blocks * MOE_BM, D_MODEL // 2), jnp.int32),
        compiler_params=pltpu.CompilerParams(
            dimension_semantics=("arbitrary",), vmem_limit_bytes=VMEM_LIMIT),
        name="moe_experts",
    )(block_e, rows_valid, next_e, slot, xd, w_gate, w_up, w_down)


def _sc_worker_id():
    return lax.axis_index("s") * SC_CORES + lax.axis_index("c")


def _sc_dispatch(hp, hs, dest_p, dest_s, n_rows):
    tp, width = hp.shape
    per_w = tp // SC_WORKERS
    n_ch = per_w // DISP_CH
    n_sw = hs.shape[0] // SAMPLE_CH
    mesh = plsc.VectorSubcoreMesh(core_axis_name="c", subcore_axis_name="s")

    @functools.partial(
        pl.kernel, mesh=mesh,
        out_type=jax.ShapeDtypeStruct((n_rows, width), jnp.int32),
        scratch_types=[
            pltpu.VMEM((2, n_ch, DISP_CH), jnp.int32),
            pltpu.VMEM((2, 1, SAMPLE_CH), jnp.int32),
            pltpu.VMEM((2, DISP_CH, width), jnp.int32),
            pltpu.SemaphoreType.DMA((2,)),
            pltpu.SemaphoreType.DMA((2,)),
        ],
        name="sc_dispatch",
    )
    def k(hp_hbm, hs_hbm, dp_hbm, ds_hbm, xd_hbm, idx_v, idxs_v, bufs, rsem, wsem):
        wid = _sc_worker_id()
        base = wid * per_w
        for kk in range(2):
            pltpu.sync_copy(dp_hbm.at[kk, wid], idx_v.at[kk])
        reads = [pltpu.make_async_copy(hp_hbm.at[pl.ds(base + j * DISP_CH, DISP_CH)],
                                       bufs.at[j % 2], rsem.at[j % 2]) for j in range(n_ch)]
        reads[0].start()
        for j in range(n_ch):
            if j + 1 < n_ch:
                reads[j + 1].start()
            reads[j].wait()
            writes = [pltpu.make_async_copy(bufs.at[j % 2], xd_hbm.at[idx_v.at[kk, j]], wsem.at[kk])
                      for kk in range(2)]
            for w in writes:
                w.start()
            for w in writes:
                w.wait()

        @pl.when(wid < n_sw)
        def _():
            rows = bufs.at[0, pl.ds(0, SAMPLE_CH)]
            for kk in range(2):
                pltpu.sync_copy(ds_hbm.at[kk, wid], idxs_v.at[kk])
            pltpu.sync_copy(hs_hbm.at[pl.ds(wid * SAMPLE_CH, SAMPLE_CH)], rows)
            for kk in range(2):
                pltpu.sync_copy(rows, xd_hbm.at[idxs_v.at[kk, 0]])

    return k(hp, hs, dest_p, dest_s)


def _sc_combine_gather(yd, dest_p, dest_s, tp, ts):
    width = yd.shape[1]
    per_w = tp // SC_WORKERS
    n_ch = per_w // COMB_CH
    n_sw = ts // SAMPLE_CH
    mesh = plsc.VectorSubcoreMesh(core_axis_name="c", subcore_axis_name="s")

    @functools.partial(
        pl.kernel, mesh=mesh,
        out_type=jax.ShapeDtypeStruct((2, tp + ts, width), yd.dtype),
        scratch_types=[
            pltpu.VMEM((2, n_ch, COMB_CH), jnp.int32),
            pltpu.VMEM((2, 1, SAMPLE_CH), jnp.int32),
            pltpu.VMEM((2, COMB_CH, width), yd.dtype),
            pltpu.SemaphoreType.DMA((2,)),
            pltpu.SemaphoreType.DMA((2,)),
        ],
        name="sc_combine_gather",
    )
    def k(yd_hbm, dp_hbm, ds_hbm, g_hbm, idx_v, idxs_v, bufs, gsem, wsem):
        wid = _sc_worker_id()
        base = wid * per_w
        for kk in range(2):
            pltpu.sync_copy(dp_hbm.at[kk, wid], idx_v.at[kk])
        items = [(kk, j) for kk in range(2) for j in range(n_ch)]
        gathers = [pltpu.make_async_copy(yd_hbm.at[idx_v.at[kk, j]], bufs.at[n % 2], gsem.at[n % 2])
                   for n, (kk, j) in enumerate(items)]
        gathers[0].start()
        for n, (kk, j) in enumerate(items):
            if n + 1 < len(items):
                gathers[n + 1].start()
            gathers[n].wait()
            w = pltpu.make_async_copy(bufs.at[n % 2], g_hbm.at[kk, pl.ds(base + j * COMB_CH, COMB_CH)],
                                      wsem.at[n % 2])
            w.start()
            w.wait()

        @pl.when(wid < n_sw)
        def _():
            for kk in range(2):
                pltpu.sync_copy(ds_hbm.at[kk, wid], idxs_v.at[kk])
            for kk in range(2):
                rows = bufs.at[kk, pl.ds(0, SAMPLE_CH)]
                pltpu.sync_copy(yd_hbm.at[idxs_v.at[kk, 0]], rows)
                pltpu.sync_copy(rows, g_hbm.at[kk, pl.ds(tp + wid * SAMPLE_CH, SAMPLE_CH)])

    return k(yd, dest_p, dest_s)


def _combine_kernel(x1_ref, g_ref, route_ref, x2_ref):
    w1 = route_ref[:, 2:3]
    w2 = route_ref[:, 3:4]
    x2_ref[...] = x1_ref[...] + _unpack_bf16_pairs(g_ref[0]) * w1 + _unpack_bf16_pairs(g_ref[1]) * w2


def _combine(x1, g, route, row0, tm):
    t = x1.shape[0]
    blk0 = row0 // tm
    row = lambda i: (i, 0)
    return pl.pallas_call(
        _combine_kernel,
        grid=(t // tm,),
        in_specs=[
            pl.BlockSpec((tm, D_MODEL), row),
            pl.BlockSpec((2, tm, D_MODEL // 2), lambda i: (0, blk0 + i, 0)),
            pl.BlockSpec((tm, LANES), row),
        ],
        out_specs=pl.BlockSpec((tm, D_MODEL), row),
        out_shape=jax.ShapeDtypeStruct((t, D_MODEL), F32),
        compiler_params=pltpu.CompilerParams(
            dimension_semantics=("arbitrary",), vmem_limit_bytes=VMEM_LIMIT),
        name="combine",
    )(x1, g, route)


def _dest_layout(dest, workers, chunk):
    t = dest.shape[1]
    return dest.reshape(2, workers, t // (workers * chunk), chunk)


def _hier_moe(l, h2p, h2s, route_tp, route_ts, counts, w_gate, w_up, w_down):
    tp, ts = h2p.shape[0], h2s.shape[0]
    n_assign = 2 * (tp + ts)
    n_blocks = -(-n_assign // MOE_BM) + N_EXPERTS
    n_blocks = -(-n_blocks // MOE_STEP_BLOCKS) * MOE_STEP_BLOCKS
    pcounts = (counts + MOE_BM - 1) // MOE_BM * MOE_BM
    pends = jnp.cumsum(pcounts)
    poffsets = pends - pcounts
    starts = jnp.arange(n_blocks, dtype=jnp.int32) * MOE_BM
    block_e = jnp.minimum(jnp.sum((pends[None, :] <= starts[:, None]).astype(jnp.int32), axis=1),
                          N_EXPERTS - 1)
    experts = jnp.arange(N_EXPERTS, dtype=jnp.int32)

    def lookup(table, idx):
        return jnp.sum(jnp.where(idx[..., None] == experts, table, 0), axis=-1)

    rows_valid = jnp.clip(lookup(poffsets + counts, block_e) - starts, 0, MOE_BM).astype(jnp.int32)
    used = counts > 0
    last_e = jnp.max(jnp.where(used, jnp.arange(N_EXPERTS, dtype=jnp.int32), 0))
    block_e = jnp.where(rows_valid > 0, block_e, last_e).astype(jnp.int32)
    later = used[None, :] & (jnp.arange(N_EXPERTS)[None, :] > jnp.arange(N_EXPERTS)[:, None])
    next_used = jnp.min(jnp.where(later, jnp.arange(N_EXPERTS, dtype=jnp.int32)[None, :], N_EXPERTS), axis=1)
    next_used = jnp.where(next_used >= N_EXPERTS, -1, next_used).astype(jnp.int32)
    slot_of = ((jnp.cumsum(used.astype(jnp.int32)) - 1) & 1).astype(jnp.int32)
    next_e = lookup(next_used, block_e)
    slot = lookup(slot_of, block_e)

    def dest_of(route_t):
        return lookup(poffsets, route_t[0:2].astype(jnp.int32)) + route_t[4:6].astype(jnp.int32)

    dest_p, dest_s = dest_of(route_tp), dest_of(route_ts)
    n_sw = ts // SAMPLE_CH
    xd = _sc_dispatch(h2p, h2s, _dest_layout(dest_p, SC_WORKERS, DISP_CH),
                      _dest_layout(dest_s, n_sw, SAMPLE_CH), n_blocks * MOE_BM)
    yd = _moe_experts(l, block_e, rows_valid, next_e, slot, xd, w_gate, w_up, w_down)
    return _sc_combine_gather(yd, _dest_layout(dest_p, SC_WORKERS, COMB_CH),
                              _dest_layout(dest_s, n_sw, SAMPLE_CH), tp, ts)


def kernel(x_prompt, x_sample, state_pool, cache_k_win, cache_v_win, norm_attn_g, w_in, pool_w, pool_scale, q_norm_g, k_norm_g, attn_sinks, w_out, norm_ffn_g, router_group_w, router_group_b, router_expert_w, router_expert_b, w_gate, w_up, w_down):
    n_p, t_p, d = x_prompt.shape
    n_s, t_s, _ = x_sample.shape
    depth = w_in.shape[0]
    lw_s = cache_k_win.shape[2]
    assert t_s == 1 and lw_s == WINDOW and d == D_MODEL
    assert t_p % TM_PROJ == 0 and t_p >= WINDOW
    past_len = 16384

    seg = jnp.arange(256) // HEAD_DIM
    bd = jnp.where(seg[:, None] == seg[None, :], 1.0 / HEAD_DIM, 0.0).astype(BF16)
    slopes = jnp.exp2(-8.0 * jnp.arange(1, N_HEADS + 1, dtype=F32) / N_HEADS)
    bias_p = _prompt_bias_t()
    dist_s = (WINDOW - 1) - jnp.arange(WINDOW, dtype=F32)
    bias_s = -slopes[:, None] * dist_s[None, :]

    def layer_params(l):
        wp = jnp.zeros((1, 2, 256, 256), F32)
        for p in range(2):
            wp = wp.at[0, p, :POOL_GC, :POOL_GC].set(pool_w[l, 2 * p])
            wp = wp.at[0, p, POOL_GC:, POOL_GC:].set(pool_w[l, 2 * p + 1])
        wr = jnp.zeros((1, D_MODEL, LANES), F32)
        wr = wr.at[0, :, :N_EXPERTS].set(router_expert_w[l])
        wr = wr.at[0, :, GROUP_LANE0:GROUP_LANE0 + N_EXPERT_GROUPS].set(router_group_w[l])
        br = jnp.zeros((1, 1, LANES), F32)
        br = br.at[0, 0, :N_EXPERTS].set(router_expert_b[l])
        br = br.at[0, 0, GROUP_LANE0:GROUP_LANE0 + N_EXPERT_GROUPS].set(router_group_b[l])
        return dict(
            w_in=w_in[l:l + 1].astype(BF16),
            w_out=w_out[l:l + 1].astype(BF16),
            g_attn=norm_attn_g[l].reshape(1, 1, D_MODEL),
            g_ffn=norm_ffn_g[l].reshape(1, 1, D_MODEL),
            qg=(jnp.tile(q_norm_g[l], N_HEADS) * ATTN_SCALE).reshape(1, 1, Q_W),
            kg=jnp.tile(k_norm_g[l], N_KV_HEADS).reshape(1, 1, KV_W),
            wp=wp.astype(BF16),
            ps=pool_scale[l].reshape(1, 1, POOL_W),
            wr=wr.astype(BF16),
            br=br,
            state=state_pool[l].reshape(1, n_s, POOL_STATE * POOL_W),
            ck=cache_k_win[l].reshape(1, n_s, lw_s, KV_W),
            cv=cache_v_win[l].reshape(1, n_s, lw_s, KV_W),
        )

    xp = x_prompt.reshape(n_p * t_p, D_MODEL)
    xs = x_sample.reshape(n_s, D_MODEL)
    lw_p = min(WINDOW, t_p)
    pool_p, kp_new, vp_new, pool_s, ks_new, vs_new = [], [], [], [], [], []
    zero_cnt = jnp.zeros((N_EXPERTS, 1), F32)
    pending = None
    for l in range(depth):
        sinks = attn_sinks[l]
        lp = layer_params(l)
        outs = _proj_pool_prompt(
            0, xp if pending is None else pending, n_p, t_p,
            lp["g_attn"], lp["w_in"], lp["qg"], lp["kg"], bd, lp["wp"], lp["ps"])
        if pending is not None:
            xp, outs = outs[0], outs[1:]
        pool_o, q, k, vt, utail, ktail, vtail = outs
        attn_o = _attn_prompt(q, k, vt, bias_p, sinks, n_p, t_p)
        x1p, h2p, route_p, route_tp, cnt_p = _merge_router(
            0, pool_o, attn_o, xp, lp["w_out"], lp["g_ffn"], lp["wr"], lp["br"], zero_cnt, TM_PROJ)
        pool_p.append(utail[:, 16 - POOL_STATE:, :])
        kp_new.append(ktail)
        vp_new.append(vtail)
        pool_so, attn_so, u_s, kc_s, vc_s = _sample_mixer(
            0, xs, lp["g_attn"], lp["w_in"], lp["qg"], lp["kg"], bd, lp["wp"], lp["ps"],
            lp["state"], lp["ck"], lp["cv"], sinks.reshape(N_HEADS, 1), bias_s, past_len)
        x1s, h2s, route_s, route_ts, cnt_all = _merge_router(
            0, pool_so, attn_so, xs, lp["w_out"], lp["g_ffn"], lp["wr"], lp["br"], cnt_p, n_s)
        pool_s.append(jnp.concatenate([state_pool[l][:, 1:], u_s[:, None, :]], axis=1))
        ks_new.append(kc_s)
        vs_new.append(vc_s)
        counts = cnt_all[:, 0].astype(jnp.int32)
        g = _hier_moe(l, h2p, h2s, route_tp, route_ts, counts, w_gate, w_up, w_down)
        xs = _combine(x1s, g, route_s, n_p * t_p, n_s)
        pending = (x1p, g, route_p)
    xp = _combine(*pending, 0, TM_PROJ)
    return (xp.reshape(n_p, t_p, D_MODEL), xs.reshape(n_s, t_s, D_MODEL),
            jnp.stack(pool_p),
            jnp.stack(kp_new).reshape(depth, n_p, lw_p, N_KV_HEADS, HEAD_DIM),
            jnp.stack(vp_new).reshape(depth, n_p, lw_p, N_KV_HEADS, HEAD_DIM),
            jnp.stack(pool_s),
            jnp.stack(ks_new).reshape(depth, n_s, lw_s, N_KV_HEADS, HEAD_DIM),
            jnp.stack(vs_new).reshape(depth, n_s, lw_s, N_KV_HEADS, HEAD_DIM))
```

```python
import functools

import jax
import jax.numpy as jnp
from jax import lax
from jax.experimental import pallas as pl
from jax.experimental.pallas import tpu as pltpu
from jax.experimental.pallas import tpu_sc as plsc

D_MODEL = 1024
POOL_W = 512
POOL_WINDOWS = (2, 4, 8, 16)
POOL_GC = 128
POOL_STATE = 15
HEAD_DIM = 64
N_HEADS = 8
N_KV_HEADS = 2
GQA_GROUP = 4
Q_W = 512
KV_W = 128
D_IN = POOL_W + Q_W + 2 * KV_W
WINDOW = 128
ATTN_SCALE = HEAD_DIM ** -0.5
N_EXPERT_GROUPS = 4
EXPERTS_PER_GROUP = 8
N_EXPERTS = 32
EXPERT_FF = 512
EPS = 1e-6

LANES = 128
HALO = 32
TM_PROJ = 512
MERGE_CHUNKS = 2
ATTN_QB = 8
MOE_BM = 256
MOE_STEP_BLOCKS = 4
GROUP_LANE0 = 32
ROUTE_FIELDS = 8
SC_CORES = 2
SC_SUBCORES = 16
SC_WORKERS = SC_CORES * SC_SUBCORES
DISP_CH = 64
COMB_CH = 64
SAMPLE_CH = 32
VMEM_LIMIT = 48 * 1024 * 1024

BF16 = jnp.bfloat16
F32 = jnp.float32


def _pack_bf16_pairs(h):
    w = h.shape[1] // 2
    hi = lax.bitcast_convert_type(h[:, :w].astype(F32), jnp.uint32)
    lo = lax.bitcast_convert_type(h[:, w:].astype(F32), jnp.uint32)
    return lax.bitcast_convert_type(hi | (lo >> 16), jnp.int32)


def _unpack_bf16_pairs(words):
    u = lax.bitcast_convert_type(words, jnp.uint32)
    hi = lax.bitcast_convert_type(u & jnp.uint32(0xFFFF0000), F32)
    lo = lax.bitcast_convert_type(u << 16, F32)
    return jnp.concatenate([hi, lo], axis=-1)


def _segment_mean_sq(a, bd):
    w = a.shape[1]
    return jnp.dot((a * a).astype(BF16), bd[:w, :w], preferred_element_type=F32)


def _rms_bf16(x, g):
    ms = jnp.mean(x * x, axis=-1, keepdims=True)
    return (x * lax.rsqrt(ms + EPS) * g).astype(BF16)


def _qk_norm(q, k, qg, kg, bd):
    qn = []
    for c in range(Q_W // 256):
        qc = q[:, c * 256:(c + 1) * 256]
        qn.append(qc * lax.rsqrt(_segment_mean_sq(qc, bd) + EPS))
    qn = jnp.concatenate(qn, axis=-1) * qg
    kn = k * lax.rsqrt(_segment_mean_sq(k, bd) + EPS) * kg
    return qn, kn


def _project(x, g, w_in, qg, kg, bd):
    z = jnp.dot(_rms_bf16(x, g), w_in, preferred_element_type=F32)
    u = z[:, :POOL_W]
    q = z[:, POOL_W:POOL_W + Q_W]
    k = z[:, POOL_W + Q_W:POOL_W + Q_W + KV_W]
    v = z[:, POOL_W + Q_W + KV_W:]
    qn, kn = _qk_norm(q, k, qg, kg, bd)
    return u, qn, kn, v


def _pool_project(d_groups, wp_ref, ps):
    outs = []
    for p in range(2):
        dp = jnp.concatenate([d_groups[2 * p], d_groups[2 * p + 1]], axis=-1).astype(BF16)
        y = jnp.dot(dp, wp_ref[p], preferred_element_type=F32)
        outs.append(y * ps[:, p * 256:(p + 1) * 256])
    return jnp.concatenate(outs, axis=-1)


def _proj_pool_kernel(x_ref, g_ref, win_ref, qg_ref, kg_ref, bd_ref, wp_ref, ps_ref,
                      pool_ref, q_ref, k_ref, vt_ref, utail_ref, ktail_ref, vtail_ref,
                      ext_ref, sa_ref, sb_ref, zq_ref, *, tm, n_j):
    j = pl.program_id(1)

    @pl.when(j == 0)
    def _():
        ext_ref[0:HALO, :] = jnp.zeros((HALO, POOL_W), F32)

    r = tm + HALO
    h = _rms_bf16(x_ref[...], g_ref[...])
    ext_ref[HALO:r, :] = jnp.dot(h, win_ref[:, 0:POOL_W], preferred_element_type=F32)
    zq_ref[...] = jnp.dot(h, win_ref[:, POOL_W:], preferred_element_type=F32)
    u = ext_ref[HALO:r, :]
    sa_ref[8:r, :] = ext_ref[8:r, :] + ext_ref[7:r - 1, :]
    sb_ref[16:r, 128:] = sa_ref[16:r, 128:] + sa_ref[14:r - 2, 128:]
    sa_ref[24:r, 256:] = sb_ref[24:r, 256:] + sb_ref[20:r - 4, 256:]
    sb_ref[32:r, 384:] = sa_ref[32:r, 384:] + sa_ref[24:r - 8, 384:]
    pos1 = j * tm + lax.broadcasted_iota(jnp.int32, (tm, POOL_GC), 0) + 1
    sums = (sa_ref, sb_ref, sa_ref, sb_ref)
    d_groups = []
    for gi, w in enumerate(POOL_WINDOWS):
        sl = slice(gi * POOL_GC, (gi + 1) * POOL_GC)
        cnt = jnp.minimum(pos1, w).astype(F32)
        d_groups.append(sums[gi][HALO:r, sl] / cnt - u[:, sl])
    pool_ref[...] = _pool_project(d_groups, wp_ref, ps_ref[...]).astype(BF16)
    ext_ref[16:HALO, :] = ext_ref[tm + 16:r, :]

    qn, kn = _qk_norm(zq_ref[:, 0:Q_W], zq_ref[:, Q_W:Q_W + KV_W], qg_ref[...], kg_ref[...], bd_ref[...])
    v = zq_ref[:, Q_W + KV_W:]
    q_ref[...] = qn.astype(BF16)
    k_ref[...] = kn.astype(BF16)
    vt_ref[...] = jnp.transpose(v).astype(BF16)

    @pl.when(j == n_j - 1)
    def _():
        utail_ref[...] = u[tm - 16:, :]
        ktail_ref[...] = kn[tm - WINDOW:, :]
        vtail_ref[...] = v[tm - WINDOW:, :]


def _proj_pool_combine_kernel(x1_ref, gath_ref, route_ref, *rest, tm, n_j):
    x2_ref = rest[7]
    _combine_kernel(x1_ref, gath_ref, route_ref, x2_ref)
    _proj_pool_kernel(x2_ref, *rest[:7], *rest[8:], tm=tm, n_j=n_j)


def _proj_pool_prompt(l, x_in, n_seq, seq, g_attn, w_in, qg, kg, bd, wp, ps):
    tm = TM_PROJ
    n_j = seq // tm
    t = n_seq * seq
    row = lambda b, j: (b * n_j + j, 0)
    lay = lambda b, j: (l, 0, 0)
    fused = isinstance(x_in, tuple)
    if fused:
        kern = _proj_pool_combine_kernel
        x_args = list(x_in)
        x_specs = [pl.BlockSpec((tm, D_MODEL), row),
                   pl.BlockSpec((2, tm, D_MODEL // 2), lambda b, j: (0, b * n_j + j, 0)),
                   pl.BlockSpec((tm, LANES), row)]
        x_out_specs = [pl.BlockSpec((tm, D_MODEL), row)]
        x_out_shape = [jax.ShapeDtypeStruct((t, D_MODEL), F32)]
    else:
        kern = _proj_pool_kernel
        x_args = [x_in]
        x_specs = [pl.BlockSpec((tm, D_MODEL), row)]
        x_out_specs, x_out_shape = [], []
    return pl.pallas_call(
        functools.partial(kern, tm=tm, n_j=n_j),
        grid=(n_seq, n_j),
        in_specs=x_specs + [
            pl.BlockSpec((None, 1, D_MODEL), lay),
            pl.BlockSpec((None, D_MODEL, D_IN), lay),
            pl.BlockSpec((None, 1, Q_W), lay),
            pl.BlockSpec((None, 1, KV_W), lay),
            pl.BlockSpec((256, 256), lambda b, j: (0, 0)),
            pl.BlockSpec((None, 2, 256, 256), lambda b, j: (l, 0, 0, 0)),
            pl.BlockSpec((None, 1, POOL_W), lay),
        ],
        out_specs=x_out_specs + [
            pl.BlockSpec((tm, POOL_W), row),
            pl.BlockSpec((tm, Q_W), row),
            pl.BlockSpec((tm, KV_W), row),
            pl.BlockSpec((KV_W, tm), lambda b, j: (0, b * n_j + j)),
            pl.BlockSpec((None, 16, POOL_W), lambda b, j: (b, 0, 0)),
            pl.BlockSpec((None, WINDOW, KV_W), lambda b, j: (b, 0, 0)),
            pl.BlockSpec((None, WINDOW, KV_W), lambda b, j: (b, 0, 0)),
        ],
        out_shape=x_out_shape + [
            jax.ShapeDtypeStruct((t, POOL_W), BF16),
            jax.ShapeDtypeStruct((t, Q_W), BF16),
            jax.ShapeDtypeStruct((t, KV_W), BF16),
            jax.ShapeDtypeStruct((KV_W, t), BF16),
            jax.ShapeDtypeStruct((n_seq, 16, POOL_W), F32),
            jax.ShapeDtypeStruct((n_seq, WINDOW, KV_W), F32),
            jax.ShapeDtypeStruct((n_seq, WINDOW, KV_W), F32),
        ],
        scratch_shapes=[pltpu.VMEM((tm + HALO, POOL_W), F32)] * 3 + [pltpu.VMEM((tm, Q_W + 2 * KV_W), F32)],
        compiler_params=pltpu.CompilerParams(
            dimension_semantics=("arbitrary", "arbitrary"), vmem_limit_bytes=VMEM_LIMIT),
        name="proj_pool_prompt",
    )(*x_args, g_attn, w_in, qg, kg, bd, wp, ps)


def _attn_kernel(sink_ref, q_ref, kp_ref, kc_ref, vtp_ref, vtc_ref, bias_ref, o_ref, s_ref):
    j = pl.program_id(1)
    kk_all = jnp.concatenate([kp_ref[...], kc_ref[...]], axis=0)
    vt_all = jnp.concatenate([vtp_ref[...], vtc_ref[...]], axis=1)
    from_prev = (lax.broadcasted_iota(jnp.int32, (WINDOW, WINDOW), 0)
                 > lax.broadcasted_iota(jnp.int32, (WINDOW, WINDOW), 1))
    units = [(blk, kv) for blk in range(ATTN_QB) for kv in range(N_KV_HEADS)]

    def scores(n):
        blk, kv = units[n]
        q = q_ref[blk * WINDOW:(blk + 1) * WINDOW, :]
        kk = kk_all[blk * WINDOW:(blk + 2) * WINDOW, kv * HEAD_DIM:(kv + 1) * HEAD_DIM]
        heads = range(kv * GQA_GROUP, (kv + 1) * GQA_GROUP)
        q_rows = jnp.concatenate([q[:, h * HEAD_DIM:(h + 1) * HEAD_DIM] for h in heads], axis=0)
        s_ref[n % 2] = lax.dot_general(kk, q_rows, (((1,), (1,)), ((), ())), preferred_element_type=F32)

    scores(0)
    outs = []
    for n, (blk, kv) in enumerate(units):
        if n + 1 < len(units):
            scores(n + 1)
        vt_kv = vt_all[kv * HEAD_DIM:(kv + 1) * HEAD_DIM, blk * WINDOW:(blk + 2) * WINDOW]
        variant = jnp.minimum(j, 1) if blk == 0 else 1
        for g in range(GQA_GROUP):
            h = kv * GQA_GROUP + g
            s = jnp.where(from_prev, s_ref[n % 2, 0:WINDOW, g * WINDOW:(g + 1) * WINDOW],
                          s_ref[n % 2, WINDOW:, g * WINDOW:(g + 1) * WINDOW]) + bias_ref[variant, h]
            sink = sink_ref[h]
            m = jnp.maximum(jnp.max(s, axis=0, keepdims=True), sink)
            p = jnp.exp(s - m)
            denom = jnp.sum(p, axis=0, keepdims=True) + jnp.exp(sink - m)
            p_keys = jnp.concatenate([jnp.where(from_prev, p, 0.0), jnp.where(from_prev, 0.0, p)], axis=0)
            o_t = jnp.dot(vt_kv, p_keys.astype(BF16), preferred_element_type=F32)
            outs.append(o_t / denom)
        if kv == N_KV_HEADS - 1:
            o_ref[blk * WINDOW:(blk + 1) * WINDOW, :] = jnp.transpose(jnp.concatenate(outs, axis=0)).astype(BF16)
            outs = []


def _attn_prompt(q, k, vt, bias_t, sinks, n_seq, seq):
    tq = ATTN_QB * WINDOW
    nj = seq // tq
    t = n_seq * seq
    cur = lambda b, j: (b * nj + j, 0)
    prev = lambda b, j: (jnp.maximum((b * nj + j) * ATTN_QB - 1, 0), 0)
    cur_t = lambda b, j: (0, b * nj + j)
    prev_t = lambda b, j: (0, jnp.maximum((b * nj + j) * ATTN_QB - 1, 0))
    return pl.pallas_call(
        _attn_kernel,
        grid=(n_seq, nj),
        in_specs=[
            pl.BlockSpec(memory_space=pltpu.SMEM),
            pl.BlockSpec((tq, Q_W), cur),
            pl.BlockSpec((WINDOW, KV_W), prev),
            pl.BlockSpec((tq, KV_W), cur),
            pl.BlockSpec((KV_W, WINDOW), prev_t),
            pl.BlockSpec((KV_W, tq), cur_t),
            pl.BlockSpec((2, N_HEADS, WINDOW, WINDOW), lambda b, j: (0, 0, 0, 0)),
        ],
        out_specs=pl.BlockSpec((tq, Q_W), cur),
        out_shape=jax.ShapeDtypeStruct((t, Q_W), BF16),
        scratch_shapes=[pltpu.VMEM((2, 2 * WINDOW, GQA_GROUP * WINDOW), F32)],
        compiler_params=pltpu.CompilerParams(
            dimension_semantics=("arbitrary", "arbitrary"), vmem_limit_bytes=VMEM_LIMIT),
        name="attn_prompt",
    )(sinks, q, k, k, vt, vt, bias_t)


def _prompt_bias_t():
    r = jnp.arange(WINDOW, dtype=jnp.int32)[None, :]
    c = jnp.arange(WINDOW, dtype=jnp.int32)[:, None]
    from_prev = c > r
    dist = r - c + jnp.where(from_prev, WINDOW, 0)
    slopes = jnp.exp2(-8.0 * jnp.arange(1, N_HEADS + 1, dtype=F32) / N_HEADS)
    later = -slopes[:, None, None] * dist.astype(F32)[None]
    first = jnp.where(from_prev[None], -jnp.inf, later)
    return jnp.stack([first, later])


def _sample_kernel(x_ref, g_ref, win_ref, qg_ref, kg_ref, bd_ref, wp_ref, ps_ref,
                   st_ref, ck_ref, cv_ref, sink_ref, bias_ref, perm_ref,
                   pool_ref, attn_ref, u_ref, kc_ref, vc_ref, *, ns, pos0):
    u, qn, kn, v = _project(x_ref[...], g_ref[...], win_ref[...], qg_ref[...], kg_ref[...], bd_ref[...])
    u_ref[...] = u
    kc_ref[:, 0:WINDOW - 1, :] = ck_ref[:, 1:WINDOW, :]
    vc_ref[:, 0:WINDOW - 1, :] = cv_ref[:, 1:WINDOW, :]
    for n in range(ns):
        kc_ref[n, WINDOW - 1:WINDOW, :] = kn[n:n + 1, :]
        vc_ref[n, WINDOW - 1:WINDOW, :] = v[n:n + 1, :]

    d_groups = []
    for gi, w in enumerate(POOL_WINDOWS):
        lo = gi * POOL_GC
        acc = u[:, lo:lo + POOL_GC]
        for back in range(1, w):
            off = (POOL_STATE - back) * POOL_W + lo
            acc = acc + st_ref[:, off:off + POOL_GC]
        d_groups.append(acc / float(min(pos0 + 1, w)) - u[:, lo:lo + POOL_GC])
    pool_ref[...] = _pool_project(d_groups, wp_ref, ps_ref[...]).astype(BF16)

    zeros = jnp.zeros((ns, HEAD_DIM), F32)
    stacked = []
    for h in range(N_HEADS):
        piece = qn[:, h * HEAD_DIM:(h + 1) * HEAD_DIM]
        pair = [piece, zeros] if h < GQA_GROUP else [zeros, piece]
        stacked.append(jnp.concatenate(pair, axis=-1))
    q_hn = jnp.concatenate(stacked, axis=0).astype(BF16)
    q_nh = jnp.dot(perm_ref[0], q_hn, preferred_element_type=F32).astype(BF16)

    keys = kc_ref[...].reshape(ns * WINDOW, KV_W).astype(BF16)
    vals = vc_ref[...].reshape(ns * WINDOW, KV_W).astype(BF16)
    s_all = lax.dot_general(q_nh, keys, (((1,), (1,)), ((), ())), preferred_element_type=F32)
    sink = sink_ref[...]
    bias = bias_ref[...]
    zero_blk = jnp.zeros((N_HEADS, WINDOW), F32)
    p_rows = []
    for n in range(ns):
        s = s_all[n * N_HEADS:(n + 1) * N_HEADS, n * WINDOW:(n + 1) * WINDOW] + bias
        m = jnp.maximum(jnp.max(s, axis=-1, keepdims=True), sink)
        p = jnp.exp(s - m)
        denom = jnp.sum(p, axis=-1, keepdims=True) + jnp.exp(sink - m)
        p_rows.append(jnp.concatenate([zero_blk] * n + [p / denom] + [zero_blk] * (ns - 1 - n), axis=-1))
    p_blockdiag = jnp.concatenate(p_rows, axis=0).astype(BF16)
    o_nh = jnp.dot(p_blockdiag, vals, preferred_element_type=F32).astype(BF16)
    o_hn = jnp.dot(perm_ref[1], o_nh, preferred_element_type=F32)
    pieces = []
    for h in range(N_HEADS):
        kv = h // GQA_GROUP
        pieces.append(o_hn[h * ns:(h + 1) * ns, kv * HEAD_DIM:(kv + 1) * HEAD_DIM])
    attn_ref[...] = jnp.concatenate(pieces, axis=-1).astype(BF16)


def _sample_mixer(l, xs, g_attn, w_in, qg, kg, bd, wp, ps, state2d, ck, cv, sink8, bias_s, pos0):
    n = xs.shape[0]
    ns = 32
    row = lambda i: (i, 0)
    lay = lambda i: (l, 0, 0)
    src = jnp.arange(ns * N_HEADS)
    perm = (((src % N_HEADS) * ns + src // N_HEADS)[:, None] == src[None, :]).astype(BF16)
    perms = jnp.stack([perm, perm.T])
    return pl.pallas_call(
        functools.partial(_sample_kernel, ns=ns, pos0=pos0),
        grid=(n // ns,),
        in_specs=[
            pl.BlockSpec((ns, D_MODEL), row),
            pl.BlockSpec((None, 1, D_MODEL), lay),
            pl.BlockSpec((None, D_MODEL, D_IN), lay),
            pl.BlockSpec((None, 1, Q_W), lay),
            pl.BlockSpec((None, 1, KV_W), lay),
            pl.BlockSpec((256, 256), lambda i: (0, 0)),
            pl.BlockSpec((None, 2, 256, 256), lambda i: (l, 0, 0, 0)),
            pl.BlockSpec((None, 1, POOL_W), lay),
            pl.BlockSpec((None, ns, POOL_STATE * POOL_W), lambda i: (l, i, 0)),
            pl.BlockSpec((None, ns, WINDOW, KV_W), lambda i: (l, i, 0, 0)),
            pl.BlockSpec((None, ns, WINDOW, KV_W), lambda i: (l, i, 0, 0)),
            pl.BlockSpec((N_HEADS, 1), lambda i: (0, 0)),
            pl.BlockSpec((N_HEADS, WINDOW), lambda i: (0, 0)),
            pl.BlockSpec((2, ns * N_HEADS, ns * N_HEADS), lambda i: (0, 0, 0)),
        ],
        out_specs=[
            pl.BlockSpec((ns, POOL_W), row),
            pl.BlockSpec((ns, Q_W), row),
            pl.BlockSpec((ns, POOL_W), row),
            pl.BlockSpec((ns, WINDOW, KV_W), lambda i: (i, 0, 0)),
            pl.BlockSpec((ns, WINDOW, KV_W), lambda i: (i, 0, 0)),
        ],
        out_shape=[
            jax.ShapeDtypeStruct((n, POOL_W), BF16),
            jax.ShapeDtypeStruct((n, Q_W), BF16),
            jax.ShapeDtypeStruct((n, POOL_W), F32),
            jax.ShapeDtypeStruct((n, WINDOW, KV_W), F32),
            jax.ShapeDtypeStruct((n, WINDOW, KV_W), F32),
        ],
        compiler_params=pltpu.CompilerParams(
            dimension_semantics=("arbitrary",), vmem_limit_bytes=VMEM_LIMIT),
        name="sample_mixer",
    )(xs, g_attn, w_in, qg, kg, bd, wp, ps, state2d, ck, cv, sink8, bias_s, perms)


def _merge_router_kernel(pool_ref, attn_ref, x_ref, wout_ref, g_ref, wr_ref, br_ref, utri_ref, cin_ref,
                         x1_ref, h2_ref, route_ref, route_t_ref, cnt_ref, y_ref, lg_ref):
    i = pl.program_id(0)

    @pl.when(i == 0)
    def _():
        cnt_ref[...] = cin_ref[...]

    tm = x_ref.shape[0]
    rc = tm // MERGE_CHUNKS
    chunks = [slice(ci * rc, (ci + 1) * rc) for ci in range(MERGE_CHUNKS)]
    for rows in chunks:
        y_ref[rows, :] = (jnp.dot(pool_ref[rows, :], wout_ref[0:POOL_W, :], preferred_element_type=F32)
                          + jnp.dot(attn_ref[rows, :], wout_ref[POOL_W:, :], preferred_element_type=F32))
    for rows in chunks:
        x1 = x_ref[rows, :] + y_ref[rows, :]
        x1_ref[rows, :] = x1
        h2 = _rms_bf16(x1, g_ref[...])
        h2_ref[rows, :] = _pack_bf16_pairs(h2)
        lg_ref[rows, :] = jnp.dot(h2, wr_ref[...], preferred_element_type=F32) + br_ref[...]
    logits = lg_ref[...]

    lt = jnp.transpose(logits)
    sub = lax.broadcasted_iota(jnp.int32, (EXPERTS_PER_GROUP, tm), 0)
    neg = -jnp.inf
    big = jnp.int32(EXPERTS_PER_GROUP)
    gl = jnp.where(sub < N_EXPERT_GROUPS, lt[GROUP_LANE0:GROUP_LANE0 + EXPERTS_PER_GROUP, :], neg)
    gmax = jnp.max(gl, axis=0, keepdims=True)
    grp = jnp.min(jnp.where(gl == gmax, sub, big), axis=0, keepdims=True)
    g_w = 1.0 / jnp.sum(jnp.exp(gl - gmax), axis=0, keepdims=True)
    el = lt[(N_EXPERT_GROUPS - 1) * EXPERTS_PER_GROUP:N_EXPERT_GROUPS * EXPERTS_PER_GROUP, :]
    for gi in range(N_EXPERT_GROUPS - 2, -1, -1):
        el = jnp.where(grp == gi, lt[gi * EXPERTS_PER_GROUP:(gi + 1) * EXPERTS_PER_GROUP, :], el)
    v1 = jnp.max(el, axis=0, keepdims=True)
    i1 = jnp.min(jnp.where(el == v1, sub, big), axis=0, keepdims=True)
    el2 = jnp.where(sub == i1, neg, el)
    v2 = jnp.max(el2, axis=0, keepdims=True)
    i2 = jnp.min(jnp.where(el2 == v2, sub, big), axis=0, keepdims=True)
    e21 = jnp.exp(v2 - v1)
    w1 = g_w / (1.0 + e21)
    w2 = g_w * e21 / (1.0 + e21)
    e1 = grp * EXPERTS_PER_GROUP + i1
    e2 = grp * EXPERTS_PER_GROUP + i2

    esub = lax.broadcasted_iota(jnp.int32, (N_EXPERTS, tm), 0)
    oh1 = esub == e1
    oh2 = esub == e2
    c = jnp.where(oh1 | oh2, 1.0, 0.0)
    prefix = jnp.dot(c.astype(BF16), utri_ref[...], preferred_element_type=F32) + cnt_ref[...]
    r1 = jnp.sum(jnp.where(oh1, prefix, 0.0), axis=0, keepdims=True)
    r2 = jnp.sum(jnp.where(oh2, prefix, 0.0), axis=0, keepdims=True)
    cnt_ref[...] = cnt_ref[...] + jnp.sum(c, axis=1, keepdims=True)

    fields = jnp.zeros((ROUTE_FIELDS, tm), F32)
    for idx, val in enumerate((e1.astype(F32), e2.astype(F32), w1, w2, r1, r2)):
        fields = jnp.where(sub == idx, val, fields)
    route_t_ref[...] = fields
    padded = jnp.concatenate([fields, jnp.zeros((LANES - ROUTE_FIELDS, tm), F32)], axis=0)
    route_ref[...] = jnp.transpose(padded)


def _merge_router(l, pool, attn, x2d, w_out, g_ffn, wr, br, cnt_in, tm):
    t = x2d.shape[0]
    utri = (jnp.arange(tm)[:, None] < jnp.arange(tm)[None, :]).astype(BF16)
    row = lambda i: (i, 0)
    lay = lambda i: (l, 0, 0)
    return pl.pallas_call(
        _merge_router_kernel,
        grid=(t // tm,),
        in_specs=[
            pl.BlockSpec((tm, POOL_W), row),
            pl.BlockSpec((tm, Q_W), row),
            pl.BlockSpec((tm, D_MODEL), row),
            pl.BlockSpec((None, D_MODEL, D_MODEL), lay),
            pl.BlockSpec((None, 1, D_MODEL), lay),
            pl.BlockSpec((None, D_MODEL, LANES), lay),
            pl.BlockSpec((None, 1, LANES), lay),
            pl.BlockSpec((tm, tm), lambda i: (0, 0)),
            pl.BlockSpec((N_EXPERTS, 1), lambda i: (0, 0)),
        ],
        out_specs=[
            pl.BlockSpec((tm, D_MODEL), row),
            pl.BlockSpec((tm, D_MODEL // 2), row),
            pl.BlockSpec((tm, LANES), row),
            pl.BlockSpec((ROUTE_FIELDS, tm), lambda i: (0, i)),
            pl.BlockSpec((N_EXPERTS, 1), lambda i: (0, 0)),
        ],
        out_shape=[
            jax.ShapeDtypeStruct((t, D_MODEL), F32),
            jax.ShapeDtypeStruct((t, D_MODEL // 2), jnp.int32),
            jax.ShapeDtypeStruct((t, LANES), F32),
            jax.ShapeDtypeStruct((ROUTE_FIELDS, t), F32),
            jax.ShapeDtypeStruct((N_EXPERTS, 1), F32),
        ],
        scratch_shapes=[pltpu.VMEM((tm, D_MODEL), F32), pltpu.VMEM((tm, LANES), F32)],
        compiler_params=pltpu.CompilerParams(
            dimension_semantics=("arbitrary",), vmem_limit_bytes=VMEM_LIMIT),
        name="merge_router",
    )(pool, attn, x2d, w_out, g_ffn, wr, br, utri, cnt_in)


def _moe_kernel(be_ref, rv_ref, nx_ref, sl_ref, xd_ref, wg_hbm, wu_hbm, wd_hbm, yd_ref,
                wg_f, wu_f, wd_f, wg_s, wu_s, wd_s, sem, *, layer):
    step = pl.program_id(0)

    def weight_copies(e, s):
        return [pltpu.make_async_copy(w_hbm.at[layer, e], w_f.at[s], sem.at[s, n])
                for n, (w_hbm, w_f) in enumerate(((wg_hbm, wg_f), (wu_hbm, wu_f), (wd_hbm, wd_f)))]

    @pl.when(step == 0)
    def _():
        for c in weight_copies(be_ref[0], sl_ref[0]):
            c.start()

    for sub_blk in range(MOE_STEP_BLOCKS):
        i = step * MOE_STEP_BLOCKS + sub_blk
        rows = pl.ds(sub_blk * MOE_BM, MOE_BM)
        expert = be_ref[i]
        changed = (i == 0) | (expert != be_ref[jnp.maximum(i - 1, 0)])
        rows_valid = rv_ref[i]
        slot = sl_ref[i]

        @pl.when(changed)
        def _():
            for c in weight_copies(expert, slot):
                c.wait()

            @pl.when(nx_ref[i] >= 0)
            def _():
                for c in weight_copies(nx_ref[i], 1 - slot):
                    c.start(priority=1)

            wg_s[...] = wg_f[slot].astype(BF16)
            wu_s[...] = wu_f[slot].astype(BF16)
            wd_s[...] = wd_f[slot].astype(BF16)

        @pl.when(rows_valid > 0)
        def _():
            row = lax.broadcasted_iota(jnp.int32, (MOE_BM, D_MODEL // 2), 0)
            x = _unpack_bf16_pairs(jnp.where(row < rows_valid, xd_ref[rows, :], 0)).astype(BF16)
            gate = jnp.dot(x, wg_s[...], preferred_element_type=F32)
            up = jnp.dot(x, wu_s[...], preferred_element_type=F32)
            act = (gate * jax.nn.sigmoid(gate) * up).astype(BF16)
            y = jnp.dot(act, wd_s[...], preferred_element_type=F32)
            yd_ref[rows, :] = _pack_bf16_pairs(y.astype(BF16))

        @pl.when(rows_valid <= 0)
        def _():
            yd_ref[rows, :] = jnp.zeros((MOE_BM, D_MODEL // 2), jnp.int32)


def _moe_experts(l, block_e, rows_valid, next_e, slot, xd, w_gate, w_up, w_down):
    n_blocks = xd.shape[0] // MOE_BM
    step_rows = MOE_STEP_BLOCKS * MOE_BM
    row = lambda i, *_: (i, 0)
    return pl.pallas_call(
        functools.partial(_moe_kernel, layer=l),
        grid_spec=pltpu.PrefetchScalarGridSpec(
            num_scalar_prefetch=4,
            grid=(n_blocks // MOE_STEP_BLOCKS,),
            in_specs=[
                pl.BlockSpec((step_rows, D_MODEL // 2), row),
                pl.BlockSpec(memory_space=pl.ANY),
                pl.BlockSpec(memory_space=pl.ANY),
                pl.BlockSpec(memory_space=pl.ANY),
            ],
            out_specs=pl.BlockSpec((step_rows, D_MODEL // 2), row),
            scratch_shapes=[
                pltpu.VMEM((2, D_MODEL, EXPERT_FF), F32),
                pltpu.VMEM((2, D_MODEL, EXPERT_FF), F32),
                pltpu.VMEM((2, EXPERT_FF, D_MODEL), F32),
                pltpu.VMEM((D_MODEL, EXPERT_FF), BF16),
                pltpu.VMEM((D_MODEL, EXPERT_FF), BF16),
                pltpu.VMEM((EXPERT_FF, D_MODEL), BF16),
                pltpu.SemaphoreType.DMA((2, 3)),
            ],
        ),
        out_shape=jax.ShapeDtypeStruct((n_blocks * MOE_BM, D_MODEL // 2), jnp.int32),
        compiler_params=pltpu.CompilerParams(
            dimension_semantics=("arbitrary",), vmem_limit_bytes=VMEM_LIMIT),
        name="moe_experts",
    )(block_e, rows_valid, next_e, slot, xd, w_gate, w_up, w_down)


def _sc_worker_id():
    return lax.axis_index("s") * SC_CORES + lax.axis_index("c")


def _sc_dispatch(hp, hs, dest_p, dest_s, n_rows):
    tp, width = hp.shape
    per_w = tp // SC_WORKERS
    n_ch = per_w // DISP_CH
    n_sw = hs.shape[0] // SAMPLE_CH
    mesh = plsc.VectorSubcoreMesh(core_axis_name="c", subcore_axis_name="s")

    @functools.partial(
        pl.kernel, mesh=mesh,
        out_type=jax.ShapeDtypeStruct((n_rows, width), jnp.int32),
        scratch_types=[
            pltpu.VMEM((2, n_ch, DISP_CH), jnp.int32),
            pltpu.VMEM((2, 1, SAMPLE_CH), jnp.int32),
            pltpu.VMEM((2, DISP_CH, width), jnp.int32),
            pltpu.SemaphoreType.DMA((2,)),
            pltpu.SemaphoreType.DMA((2,)),
        ],
        name="sc_dispatch",
    )
    def k(hp_hbm, hs_hbm, dp_hbm, ds_hbm, xd_hbm, idx_v, idxs_v, bufs, rsem, wsem):
        wid = _sc_worker_id()
        base = wid * per_w
        for kk in range(2):
            pltpu.sync_copy(dp_hbm.at[kk, wid], idx_v.at[kk])
        reads = [pltpu.make_async_copy(hp_hbm.at[pl.ds(base + j * DISP_CH, DISP_CH)],
                                       bufs.at[j % 2], rsem.at[j % 2]) for j in range(n_ch)]
        reads[0].start()
        for j in range(n_ch):
            if j + 1 < n_ch:
                reads[j + 1].start()
            reads[j].wait()
            writes = [pltpu.make_async_copy(bufs.at[j % 2], xd_hbm.at[idx_v.at[kk, j]], wsem.at[kk])
                      for kk in range(2)]
            for w in writes:
                w.start()
            for w in writes:
                w.wait()

        @pl.when(wid < n_sw)
        def _():
            rows = bufs.at[0, pl.ds(0, SAMPLE_CH)]
            for kk in range(2):
                pltpu.sync_copy(ds_hbm.at[kk, wid], idxs_v.at[kk])
            pltpu.sync_copy(hs_hbm.at[pl.ds(wid * SAMPLE_CH, SAMPLE_CH)], rows)
            for kk in range(2):
                pltpu.sync_copy(rows, xd_hbm.at[idxs_v.at[kk, 0]])

    return k(hp, hs, dest_p, dest_s)


def _sc_combine_gather(yd, dest_p, dest_s, tp, ts):
    width = yd.shape[1]
    per_w = tp // SC_WORKERS
    n_ch = per_w // COMB_CH
    n_sw = ts // SAMPLE_CH
    mesh = plsc.VectorSubcoreMesh(core_axis_name="c", subcore_axis_name="s")

    @functools.partial(
        pl.kernel, mesh=mesh,
        out_type=jax.ShapeDtypeStruct((2, tp + ts, width), yd.dtype),
        scratch_types=[
            pltpu.VMEM((2, n_ch, COMB_CH), jnp.int32),
            pltpu.VMEM((2, 1, SAMPLE_CH), jnp.int32),
            pltpu.VMEM((2, COMB_CH, width), yd.dtype),
            pltpu.SemaphoreType.DMA((2,)),
            pltpu.SemaphoreType.DMA((2,)),
        ],
        name="sc_combine_gather",
    )
    def k(yd_hbm, dp_hbm, ds_hbm, g_hbm, idx_v, idxs_v, bufs, gsem, wsem):
        wid = _sc_worker_id()
        base = wid * per_w
        for kk in range(2):
            pltpu.sync_copy(dp_hbm.at[kk, wid], idx_v.at[kk])
        items = [(kk, j) for kk in range(2) for j in range(n_ch)]
        gathers = [pltpu.make_async_copy(yd_hbm.at[idx_v.at[kk, j]], bufs.at[n % 2], gsem.at[n % 2])
                   for n, (kk, j) in enumerate(items)]
        gathers[0].start()
        for n, (kk, j) in enumerate(items):
            if n + 1 < len(items):
                gathers[n + 1].start()
            gathers[n].wait()
            w = pltpu.make_async_copy(bufs.at[n % 2], g_hbm.at[kk, pl.ds(base + j * COMB_CH, COMB_CH)],
                                      wsem.at[n % 2])
            w.start()
            w.wait()

        @pl.when(wid < n_sw)
        def _():
            for kk in range(2):
                pltpu.sync_copy(ds_hbm.at[kk, wid], idxs_v.at[kk])
            for kk in range(2):
                rows = bufs.at[kk, pl.ds(0, SAMPLE_CH)]
                pltpu.sync_copy(yd_hbm.at[idxs_v.at[kk, 0]], rows)
                pltpu.sync_copy(rows, g_hbm.at[kk, pl.ds(tp + wid * SAMPLE_CH, SAMPLE_CH)])

    return k(yd, dest_p, dest_s)


def _combine_kernel(x1_ref, g_ref, route_ref, x2_ref):
    w1 = route_ref[:, 2:3]
    w2 = route_ref[:, 3:4]
    x2_ref[...] = x1_ref[...] + _unpack_bf16_pairs(g_ref[0]) * w1 + _unpack_bf16_pairs(g_ref[1]) * w2


def _combine(x1, g, route, row0, tm):
    t = x1.shape[0]
    blk0 = row0 // tm
    row = lambda i: (i, 0)
    return pl.pallas_call(
        _combine_kernel,
        grid=(t // tm,),
        in_specs=[
            pl.BlockSpec((tm, D_MODEL), row),
            pl.BlockSpec((2, tm, D_MODEL // 2), lambda i: (0, blk0 + i, 0)),
            pl.BlockSpec((tm, LANES), row),
        ],
        out_specs=pl.BlockSpec((tm, D_MODEL), row),
        out_shape=jax.ShapeDtypeStruct((t, D_MODEL), F32),
        compiler_params=pltpu.CompilerParams(
            dimension_semantics=("arbitrary",), vmem_limit_bytes=VMEM_LIMIT),
        name="combine",
    )(x1, g, route)


def _dest_layout(dest, workers, chunk):
    t = dest.shape[1]
    return dest.reshape(2, workers, t // (workers * chunk), chunk)


def _hier_moe(l, h2p, h2s, route_tp, route_ts, counts, w_gate, w_up, w_down):
    tp, ts = h2p.shape[0], h2s.shape[0]
    n_assign = 2 * (tp + ts)
    n_blocks = -(-n_assign // MOE_BM) + N_EXPERTS
    n_blocks = -(-n_blocks // MOE_STEP_BLOCKS) * MOE_STEP_BLOCKS
    pcounts = (counts + MOE_BM - 1) // MOE_BM * MOE_BM
    pends = jnp.cumsum(pcounts)
    poffsets = pends - pcounts
    starts = jnp.arange(n_blocks, dtype=jnp.int32) * MOE_BM
    block_e = jnp.minimum(jnp.sum((pends[None, :] <= starts[:, None]).astype(jnp.int32), axis=1),
                          N_EXPERTS - 1)
    experts = jnp.arange(N_EXPERTS, dtype=jnp.int32)

    def lookup(table, idx):
        return jnp.sum(jnp.where(idx[..., None] == experts, table, 0), axis=-1)

    rows_valid = jnp.clip(lookup(poffsets + counts, block_e) - starts, 0, MOE_BM).astype(jnp.int32)
    used = counts > 0
    last_e = jnp.max(jnp.where(used, jnp.arange(N_EXPERTS, dtype=jnp.int32), 0))
    block_e = jnp.where(rows_valid > 0, block_e, last_e).astype(jnp.int32)
    later = used[None, :] & (jnp.arange(N_EXPERTS)[None, :] > jnp.arange(N_EXPERTS)[:, None])
    next_used = jnp.min(jnp.where(later, jnp.arange(N_EXPERTS, dtype=jnp.int32)[None, :], N_EXPERTS), axis=1)
    next_used = jnp.where(next_used >= N_EXPERTS, -1, next_used).astype(jnp.int32)
    slot_of = ((jnp.cumsum(used.astype(jnp.int32)) - 1) & 1).astype(jnp.int32)
    next_e = lookup(next_used, block_e)
    slot = lookup(slot_of, block_e)

    def dest_of(route_t):
        return lookup(poffsets, route_t[0:2].astype(jnp.int32)) + route_t[4:6].astype(jnp.int32)

    dest_p, dest_s = dest_of(route_tp), dest_of(route_ts)
    n_sw = ts // SAMPLE_CH
    xd = _sc_dispatch(h2p, h2s, _dest_layout(dest_p, SC_WORKERS, DISP_CH),
                      _dest_layout(dest_s, n_sw, SAMPLE_CH), n_blocks * MOE_BM)
    yd = _moe_experts(l, block_e, rows_valid, next_e, slot, xd, w_gate, w_up, w_down)
    return _sc_combine_gather(yd, _dest_layout(dest_p, SC_WORKERS, COMB_CH),
                              _dest_layout(dest_s, n_sw, SAMPLE_CH), tp, ts)


def kernel(x_prompt, x_sample, state_pool, cache_k_win, cache_v_win, norm_attn_g, w_in, pool_w, pool_scale, q_norm_g, k_norm_g, attn_sinks, w_out, norm_ffn_g, router_group_w, router_group_b, router_expert_w, router_expert_b, w_gate, w_up, w_down):
    n_p, t_p, d = x_prompt.shape
    n_s, t_s, _ = x_sample.shape
    depth = w_in.shape[0]
    lw_s = cache_k_win.shape[2]
    assert t_s == 1 and lw_s == WINDOW and d == D_MODEL
    assert t_p % TM_PROJ == 0 and t_p >= WINDOW
    past_len = 16384

    seg = jnp.arange(256) // HEAD_DIM
    bd = jnp.where(seg[:, None] == seg[None, :], 1.0 / HEAD_DIM, 0.0).astype(BF16)
    slopes = jnp.exp2(-8.0 * jnp.arange(1, N_HEADS + 1, dtype=F32) / N_HEADS)
    bias_p = _prompt_bias_t()
    dist_s = (WINDOW - 1) - jnp.arange(WINDOW, dtype=F32)
    bias_s = -slopes[:, None] * dist_s[None, :]

    wp = jnp.zeros((depth, 2, 256, 256), F32)
    for p in range(2):
        wp = wp.at[:, p, :POOL_GC, :POOL_GC].set(pool_w[:, 2 * p])
        wp = wp.at[:, p, POOL_GC:, POOL_GC:].set(pool_w[:, 2 * p + 1])
    wr = jnp.zeros((depth, D_MODEL, LANES), F32)
    wr = wr.at[:, :, :N_EXPERTS].set(router_expert_w)
    wr = wr.at[:, :, GROUP_LANE0:GROUP_LANE0 + N_EXPERT_GROUPS].set(router_group_w)
    br = jnp.zeros((depth, 1, LANES), F32)
    br = br.at[:, 0, :N_EXPERTS].set(router_expert_b)
    br = br.at[:, 0, GROUP_LANE0:GROUP_LANE0 + N_EXPERT_GROUPS].set(router_group_b)
    lp = dict(
        w_in=w_in.astype(BF16),
        w_out=w_out.astype(BF16),
        g_attn=norm_attn_g.reshape(depth, 1, D_MODEL),
        g_ffn=norm_ffn_g.reshape(depth, 1, D_MODEL),
        qg=(jnp.tile(q_norm_g, (1, N_HEADS)) * ATTN_SCALE).reshape(depth, 1, Q_W),
        kg=jnp.tile(k_norm_g, (1, N_KV_HEADS)).reshape(depth, 1, KV_W),
        wp=wp.astype(BF16),
        ps=pool_scale.reshape(depth, 1, POOL_W),
        wr=wr.astype(BF16),
        br=br,
        state=state_pool.reshape(depth, n_s, POOL_STATE * POOL_W),
        ck=cache_k_win.reshape(depth, n_s, lw_s, KV_W),
        cv=cache_v_win.reshape(depth, n_s, lw_s, KV_W),
    )

    xp = x_prompt.reshape(n_p * t_p, D_MODEL)
    xs = x_sample.reshape(n_s, D_MODEL)
    lw_p = min(WINDOW, t_p)
    pool_p, kp_new, vp_new, pool_s, ks_new, vs_new = [], [], [], [], [], []
    zero_cnt = jnp.zeros((N_EXPERTS, 1), F32)
    pending = None
    for l in range(depth):
        sinks = attn_sinks[l]
        outs = _proj_pool_prompt(
            l, xp if pending is None else pending, n_p, t_p,
            lp["g_attn"], lp["w_in"], lp["qg"], lp["kg"], bd, lp["wp"], lp["ps"])
        if pending is not None:
            xp, outs = outs[0], outs[1:]
        pool_o, q, k, vt, utail, ktail, vtail = outs
        attn_o = _attn_prompt(q, k, vt, bias_p, sinks, n_p, t_p)
        x1p, h2p, route_p, route_tp, cnt_p = _merge_router(
            l, pool_o, attn_o, xp, lp["w_out"], lp["g_ffn"], lp["wr"], lp["br"], zero_cnt, TM_PROJ)
        pool_p.append(utail[:, 16 - POOL_STATE:, :])
        kp_new.append(ktail)
        vp_new.append(vtail)
        pool_so, attn_so, u_s, kc_s, vc_s = _sample_mixer(
            l, xs, lp["g_attn"], lp["w_in"], lp["qg"], lp["kg"], bd, lp["wp"], lp["ps"],
            lp["state"], lp["ck"], lp["cv"], sinks.reshape(N_HEADS, 1), bias_s, past_len)
        x1s, h2s, route_s, route_ts, cnt_all = _merge_router(
            l, pool_so, attn_so, xs, lp["w_out"], lp["g_ffn"], lp["wr"], lp["br"], cnt_p, n_s)
        pool_s.append(jnp.concatenate([state_pool[l][:, 1:], u_s[:, None, :]], axis=1))
        ks_new.append(kc_s)
        vs_new.append(vc_s)
        counts = cnt_all[:, 0].astype(jnp.int32)
        g = _hier_moe(l, h2p, h2s, route_tp, route_ts, counts, w_gate, w_up, w_down)
        xs = _combine(x1s, g, route_s, n_p * t_p, n_s)
        pending = (x1p, g, route_p)
    xp = _combine(*pending, 0, TM_PROJ)
    return (xp.reshape(n_p, t_p, D_MODEL), xs.reshape(n_s, t_s, D_MODEL),
            jnp.stack(pool_p),
            jnp.stack(kp_new).reshape(depth, n_p, lw_p, N_KV_HEADS, HEAD_DIM),
            jnp.stack(vp_new).reshape(depth, n_p, lw_p, N_KV_HEADS, HEAD_DIM),
            jnp.stack(pool_s),
            jnp.stack(ks_new).reshape(depth, n_s, lw_s, N_KV_HEADS, HEAD_DIM),
            jnp.stack(vs_new).reshape(depth, n_s, lw_s, N_KV_HEADS, HEAD_DIM))
```

```python
import functools

import jax
import jax.numpy as jnp
from jax import lax
from jax.experimental import pallas as pl
from jax.experimental.pallas import tpu as pltpu
from jax.experimental.pallas import tpu_sc as plsc

D_MODEL = 1024
POOL_W = 512
POOL_WINDOWS = (2, 4, 8, 16)
POOL_GC = 128
POOL_STATE = 15
HEAD_DIM = 64
N_HEADS = 8
N_KV_HEADS = 2
GQA_GROUP = 4
Q_W = 512
KV_W = 128
D_IN = POOL_W + Q_W + 2 * KV_W
WINDOW = 128
ATTN_SCALE = HEAD_DIM ** -0.5
N_EXPERT_GROUPS = 4
EXPERTS_PER_GROUP = 8
N_EXPERTS = 32
EXPERT_FF = 512
EPS = 1e-6

LANES = 128
HALO = 32
TM_PROJ = 512
MERGE_CHUNKS = 2
ATTN_QB = 8
MOE_BM = 256
MOE_STEP_BLOCKS = 4
GROUP_LANE0 = 32
ROUTE_FIELDS = 8
SC_CORES = 2
SC_SUBCORES = 16
SC_WORKERS = SC_CORES * SC_SUBCORES
DISP_CH = 64
COMB_CH = 64
SAMPLE_CH = 32
VMEM_LIMIT = 48 * 1024 * 1024

BF16 = jnp.bfloat16
F32 = jnp.float32


def _pack_bf16_pairs(h):
    w = h.shape[1] // 2
    hi = lax.bitcast_convert_type(h[:, :w].astype(F32), jnp.uint32)
    lo = lax.bitcast_convert_type(h[:, w:].astype(F32), jnp.uint32)
    return lax.bitcast_convert_type(hi | (lo >> 16), jnp.int32)


def _unpack_bf16_pairs(words):
    u = lax.bitcast_convert_type(words, jnp.uint32)
    hi = lax.bitcast_convert_type(u & jnp.uint32(0xFFFF0000), F32)
    lo = lax.bitcast_convert_type(u << 16, F32)
    return jnp.concatenate([hi, lo], axis=-1)


def _segment_mean_sq(a, bd):
    w = a.shape[1]
    return jnp.dot((a * a).astype(BF16), bd[:w, :w], preferred_element_type=F32)


def _rms_bf16(x, g):
    ms = jnp.mean(x * x, axis=-1, keepdims=True)
    return (x * lax.rsqrt(ms + EPS) * g).astype(BF16)


def _qk_norm(q, k, qg, kg, bd):
    qn = []
    for c in range(Q_W // 256):
        qc = q[:, c * 256:(c + 1) * 256]
        qn.append(qc * lax.rsqrt(_segment_mean_sq(qc, bd) + EPS))
    qn = jnp.concatenate(qn, axis=-1) * qg
    kn = k * lax.rsqrt(_segment_mean_sq(k, bd) + EPS) * kg
    return qn, kn


def _project(x, g, w_in, qg, kg, bd):
    z = jnp.dot(_rms_bf16(x, g), w_in, preferred_element_type=F32)
    u = z[:, :POOL_W]
    q = z[:, POOL_W:POOL_W + Q_W]
    k = z[:, POOL_W + Q_W:POOL_W + Q_W + KV_W]
    v = z[:, POOL_W + Q_W + KV_W:]
    qn, kn = _qk_norm(q, k, qg, kg, bd)
    return u, qn, kn, v


def _pool_project(d_groups, wp_ref, ps):
    outs = []
    for p in range(2):
        dp = jnp.concatenate([d_groups[2 * p], d_groups[2 * p + 1]], axis=-1).astype(BF16)
        y = jnp.dot(dp, wp_ref[p], preferred_element_type=F32)
        outs.append(y * ps[:, p * 256:(p + 1) * 256])
    return jnp.concatenate(outs, axis=-1)


def _proj_pool_kernel(x_ref, g_ref, win_ref, qg_ref, kg_ref, bd_ref, wp_ref, ps_ref,
                      pool_ref, q_ref, k_ref, vt_ref, utail_ref, ktail_ref, vtail_ref,
                      ext_ref, sa_ref, sb_ref, zq_ref, *, tm, n_j):
    j = pl.program_id(1)

    @pl.when(j == 0)
    def _():
        ext_ref[0:HALO, :] = jnp.zeros((HALO, POOL_W), F32)

    r = tm + HALO
    h = _rms_bf16(x_ref[...], g_ref[...])
    ext_ref[HALO:r, :] = jnp.dot(h, win_ref[:, 0:POOL_W], preferred_element_type=F32)
    zq_ref[...] = jnp.dot(h, win_ref[:, POOL_W:], preferred_element_type=F32)
    u = ext_ref[HALO:r, :]
    sa_ref[8:r, :] = ext_ref[8:r, :] + ext_ref[7:r - 1, :]
    sb_ref[16:r, 128:] = sa_ref[16:r, 128:] + sa_ref[14:r - 2, 128:]
    sa_ref[24:r, 256:] = sb_ref[24:r, 256:] + sb_ref[20:r - 4, 256:]
    sb_ref[32:r, 384:] = sa_ref[32:r, 384:] + sa_ref[24:r - 8, 384:]
    pos1 = j * tm + lax.broadcasted_iota(jnp.int32, (tm, POOL_GC), 0) + 1
    sums = (sa_ref, sb_ref, sa_ref, sb_ref)
    d_groups = []
    for gi, w in enumerate(POOL_WINDOWS):
        sl = slice(gi * POOL_GC, (gi + 1) * POOL_GC)
        cnt = jnp.minimum(pos1, w).astype(F32)
        d_groups.append(sums[gi][HALO:r, sl] / cnt - u[:, sl])
    pool_ref[...] = _pool_project(d_groups, wp_ref, ps_ref[...]).astype(BF16)
    ext_ref[16:HALO, :] = ext_ref[tm + 16:r, :]

    qn, kn = _qk_norm(zq_ref[:, 0:Q_W], zq_ref[:, Q_W:Q_W + KV_W], qg_ref[...], kg_ref[...], bd_ref[...])
    v = zq_ref[:, Q_W + KV_W:]
    q_ref[...] = qn.astype(BF16)
    k_ref[...] = kn.astype(BF16)
    vt_ref[...] = jnp.transpose(v).astype(BF16)

    @pl.when(j == n_j - 1)
    def _():
        utail_ref[...] = u[tm - 16:, :]
        ktail_ref[...] = kn[tm - WINDOW:, :]
        vtail_ref[...] = v[tm - WINDOW:, :]


def _proj_pool_combine_kernel(x1_ref, gath_ref, route_ref, *rest, tm, n_j):
    x2_ref = rest[7]
    _combine_kernel(x1_ref, gath_ref, route_ref, x2_ref)
    _proj_pool_kernel(x2_ref, *rest[:7], *rest[8:], tm=tm, n_j=n_j)


def _proj_pool_prompt(l, x_in, n_seq, seq, g_attn, w_in, qg, kg, bd, wp, ps):
    tm = TM_PROJ
    n_j = seq // tm
    t = n_seq * seq
    row = lambda b, j: (b * n_j + j, 0)
    lay = lambda b, j: (l, 0, 0)
    fused = isinstance(x_in, tuple)
    if fused:
        kern = _proj_pool_combine_kernel
        x_args = list(x_in)
        x_specs = [pl.BlockSpec((tm, D_MODEL), row),
                   pl.BlockSpec((2, tm, D_MODEL // 2), lambda b, j: (0, b * n_j + j, 0)),
                   pl.BlockSpec((ROUTE_FIELDS, tm), lambda b, j: (0, b * n_j + j))]
        x_out_specs = [pl.BlockSpec((tm, D_MODEL), row)]
        x_out_shape = [jax.ShapeDtypeStruct((t, D_MODEL), F32)]
    else:
        kern = _proj_pool_kernel
        x_args = [x_in]
        x_specs = [pl.BlockSpec((tm, D_MODEL), row)]
        x_out_specs, x_out_shape = [], []
    return pl.pallas_call(
        functools.partial(kern, tm=tm, n_j=n_j),
        grid=(n_seq, n_j),
        in_specs=x_specs + [
            pl.BlockSpec((None, 1, D_MODEL), lay),
            pl.BlockSpec((None, D_MODEL, D_IN), lay),
            pl.BlockSpec((None, 1, Q_W), lay),
            pl.BlockSpec((None, 1, KV_W), lay),
            pl.BlockSpec((256, 256), lambda b, j: (0, 0)),
            pl.BlockSpec((None, 2, 256, 256), lambda b, j: (l, 0, 0, 0)),
            pl.BlockSpec((None, 1, POOL_W), lay),
        ],
        out_specs=x_out_specs + [
            pl.BlockSpec((tm, POOL_W), row),
            pl.BlockSpec((tm, Q_W), row),
            pl.BlockSpec((tm, KV_W), row),
            pl.BlockSpec((KV_W, tm), lambda b, j: (0, b * n_j + j)),
            pl.BlockSpec((None, 16, POOL_W), lambda b, j: (b, 0, 0)),
            pl.BlockSpec((None, WINDOW, KV_W), lambda b, j: (b, 0, 0)),
            pl.BlockSpec((None, WINDOW, KV_W), lambda b, j: (b, 0, 0)),
        ],
        out_shape=x_out_shape + [
            jax.ShapeDtypeStruct((t, POOL_W), BF16),
            jax.ShapeDtypeStruct((t, Q_W), BF16),
            jax.ShapeDtypeStruct((t, KV_W), BF16),
            jax.ShapeDtypeStruct((KV_W, t), BF16),
            jax.ShapeDtypeStruct((n_seq, 16, POOL_W), F32),
            jax.ShapeDtypeStruct((n_seq, WINDOW, KV_W), F32),
            jax.ShapeDtypeStruct((n_seq, WINDOW, KV_W), F32),
        ],
        scratch_shapes=[pltpu.VMEM((tm + HALO, POOL_W), F32)] * 3 + [pltpu.VMEM((tm, Q_W + 2 * KV_W), F32)],
        compiler_params=pltpu.CompilerParams(
            dimension_semantics=("arbitrary", "arbitrary"), vmem_limit_bytes=VMEM_LIMIT),
        name="proj_pool_prompt",
    )(*x_args, g_attn, w_in, qg, kg, bd, wp, ps)


def _attn_kernel(sink_ref, q_ref, kp_ref, kc_ref, vtp_ref, vtc_ref, bias_ref, o_ref, s_ref):
    j = pl.program_id(1)
    kk_all = jnp.concatenate([kp_ref[...], kc_ref[...]], axis=0)
    vt_all = jnp.concatenate([vtp_ref[...], vtc_ref[...]], axis=1)
    from_prev = (lax.broadcasted_iota(jnp.int32, (WINDOW, WINDOW), 0)
                 > lax.broadcasted_iota(jnp.int32, (WINDOW, WINDOW), 1))
    units = [(blk, kv) for blk in range(ATTN_QB) for kv in range(N_KV_HEADS)]

    def scores(n):
        blk, kv = units[n]
        q = q_ref[blk * WINDOW:(blk + 1) * WINDOW, :]
        kk = kk_all[blk * WINDOW:(blk + 2) * WINDOW, kv * HEAD_DIM:(kv + 1) * HEAD_DIM]
        heads = range(kv * GQA_GROUP, (kv + 1) * GQA_GROUP)
        q_rows = jnp.concatenate([q[:, h * HEAD_DIM:(h + 1) * HEAD_DIM] for h in heads], axis=0)
        s_ref[n % 2] = lax.dot_general(kk, q_rows, (((1,), (1,)), ((), ())), preferred_element_type=F32)

    scores(0)
    outs = []
    for n, (blk, kv) in enumerate(units):
        if n + 1 < len(units):
            scores(n + 1)
        vt_kv = vt_all[kv * HEAD_DIM:(kv + 1) * HEAD_DIM, blk * WINDOW:(blk + 2) * WINDOW]
        variant = jnp.minimum(j, 1) if blk == 0 else 1
        for g in range(GQA_GROUP):
            h = kv * GQA_GROUP + g
            s = jnp.where(from_prev, s_ref[n % 2, 0:WINDOW, g * WINDOW:(g + 1) * WINDOW],
                          s_ref[n % 2, WINDOW:, g * WINDOW:(g + 1) * WINDOW]) + bias_ref[variant, h]
            sink = sink_ref[h]
            m = jnp.maximum(jnp.max(s, axis=0, keepdims=True), sink)
            p = jnp.exp(s - m)
            denom = jnp.sum(p, axis=0, keepdims=True) + jnp.exp(sink - m)
            p_keys = jnp.concatenate([jnp.where(from_prev, p, 0.0), jnp.where(from_prev, 0.0, p)], axis=0)
            o_t = jnp.dot(vt_kv, p_keys.astype(BF16), preferred_element_type=F32)
            outs.append(o_t / denom)
        if kv == N_KV_HEADS - 1:
            o_ref[blk * WINDOW:(blk + 1) * WINDOW, :] = jnp.transpose(jnp.concatenate(outs, axis=0)).astype(BF16)
            outs = []


def _attn_prompt(q, k, vt, bias_t, sinks, n_seq, seq):
    tq = ATTN_QB * WINDOW
    nj = seq // tq
    t = n_seq * seq
    cur = lambda b, j: (b * nj + j, 0)
    prev = lambda b, j: (jnp.maximum((b * nj + j) * ATTN_QB - 1, 0), 0)
    cur_t = lambda b, j: (0, b * nj + j)
    prev_t = lambda b, j: (0, jnp.maximum((b * nj + j) * ATTN_QB - 1, 0))
    return pl.pallas_call(
        _attn_kernel,
        grid=(n_seq, nj),
        in_specs=[
            pl.BlockSpec(memory_space=pltpu.SMEM),
            pl.BlockSpec((tq, Q_W), cur),
            pl.BlockSpec((WINDOW, KV_W), prev),
            pl.BlockSpec((tq, KV_W), cur),
            pl.BlockSpec((KV_W, WINDOW), prev_t),
            pl.BlockSpec((KV_W, tq), cur_t),
            pl.BlockSpec((2, N_HEADS, WINDOW, WINDOW), lambda b, j: (0, 0, 0, 0)),
        ],
        out_specs=pl.BlockSpec((tq, Q_W), cur),
        out_shape=jax.ShapeDtypeStruct((t, Q_W), BF16),
        scratch_shapes=[pltpu.VMEM((2, 2 * WINDOW, GQA_GROUP * WINDOW), F32)],
        compiler_params=pltpu.CompilerParams(
            dimension_semantics=("arbitrary", "arbitrary"), vmem_limit_bytes=VMEM_LIMIT),
        name="attn_prompt",
    )(sinks, q, k, k, vt, vt, bias_t)


def _prompt_bias_t():
    r = jnp.arange(WINDOW, dtype=jnp.int32)[None, :]
    c = jnp.arange(WINDOW, dtype=jnp.int32)[:, None]
    from_prev = c > r
    dist = r - c + jnp.where(from_prev, WINDOW, 0)
    slopes = jnp.exp2(-8.0 * jnp.arange(1, N_HEADS + 1, dtype=F32) / N_HEADS)
    later = -slopes[:, None, None] * dist.astype(F32)[None]
    first = jnp.where(from_prev[None], -jnp.inf, later)
    return jnp.stack([first, later])


def _sample_kernel(x_ref, g_ref, win_ref, qg_ref, kg_ref, bd_ref, wp_ref, ps_ref,
                   st_ref, ck_ref, cv_ref, sink_ref, bias_ref, perm_ref,
                   pool_ref, attn_ref, pst_ref, kc_ref, vc_ref, *, ns, pos0):
    u, qn, kn, v = _project(x_ref[...], g_ref[...], win_ref[...], qg_ref[...], kg_ref[...], bd_ref[...])
    pst_ref[:, 0:POOL_STATE - 1, :] = st_ref[:, 1:POOL_STATE, :]
    kc_ref[:, 0:WINDOW - 1, :] = ck_ref[:, 1:WINDOW, :]
    vc_ref[:, 0:WINDOW - 1, :] = cv_ref[:, 1:WINDOW, :]
    for n in range(ns):
        pst_ref[n, POOL_STATE - 1:POOL_STATE, :] = u[n:n + 1, :]
        kc_ref[n, WINDOW - 1:WINDOW, :] = kn[n:n + 1, :]
        vc_ref[n, WINDOW - 1:WINDOW, :] = v[n:n + 1, :]

    d_groups = []
    for gi, w in enumerate(POOL_WINDOWS):
        lo = gi * POOL_GC
        acc = u[:, lo:lo + POOL_GC]
        for back in range(1, w):
            acc = acc + st_ref[:, POOL_STATE - back, lo:lo + POOL_GC]
        d_groups.append(acc / float(min(pos0 + 1, w)) - u[:, lo:lo + POOL_GC])
    pool_ref[...] = _pool_project(d_groups, wp_ref, ps_ref[...]).astype(BF16)

    zeros = jnp.zeros((ns, HEAD_DIM), F32)
    stacked = []
    for h in range(N_HEADS):
        piece = qn[:, h * HEAD_DIM:(h + 1) * HEAD_DIM]
        pair = [piece, zeros] if h < GQA_GROUP else [zeros, piece]
        stacked.append(jnp.concatenate(pair, axis=-1))
    q_hn = jnp.concatenate(stacked, axis=0).astype(BF16)
    q_nh = jnp.dot(perm_ref[0], q_hn, preferred_element_type=F32).astype(BF16)

    keys = kc_ref[...].reshape(ns * WINDOW, KV_W).astype(BF16)
    vals = vc_ref[...].reshape(ns * WINDOW, KV_W).astype(BF16)
    s_all = lax.dot_general(q_nh, keys, (((1,), (1,)), ((), ())), preferred_element_type=F32)
    sink = sink_ref[...]
    bias = bias_ref[...]
    zero_blk = jnp.zeros((N_HEADS, WINDOW), F32)
    p_rows = []
    for n in range(ns):
        s = s_all[n * N_HEADS:(n + 1) * N_HEADS, n * WINDOW:(n + 1) * WINDOW] + bias
        m = jnp.maximum(jnp.max(s, axis=-1, keepdims=True), sink)
        p = jnp.exp(s - m)
        denom = jnp.sum(p, axis=-1, keepdims=True) + jnp.exp(sink - m)
        p_rows.append(jnp.concatenate([zero_blk] * n + [p / denom] + [zero_blk] * (ns - 1 - n), axis=-1))
    p_blockdiag = jnp.concatenate(p_rows, axis=0).astype(BF16)
    o_nh = jnp.dot(p_blockdiag, vals, preferred_element_type=F32).astype(BF16)
    o_hn = jnp.dot(perm_ref[1], o_nh, preferred_element_type=F32)
    pieces = []
    for h in range(N_HEADS):
        kv = h // GQA_GROUP
        pieces.append(o_hn[h * ns:(h + 1) * ns, kv * HEAD_DIM:(kv + 1) * HEAD_DIM])
    attn_ref[...] = jnp.concatenate(pieces, axis=-1).astype(BF16)


def _sample_mixer(l, xs, g_attn, w_in, qg, kg, bd, wp, ps, state2d, ck, cv, sink8, bias_s, pos0):
    n = xs.shape[0]
    ns = 32
    row = lambda i: (i, 0)
    lay = lambda i: (l, 0, 0)
    src = jnp.arange(ns * N_HEADS)
    perm = (((src % N_HEADS) * ns + src // N_HEADS)[:, None] == src[None, :]).astype(BF16)
    perms = jnp.stack([perm, perm.T])
    return pl.pallas_call(
        functools.partial(_sample_kernel, ns=ns, pos0=pos0),
        grid=(n // ns,),
        in_specs=[
            pl.BlockSpec((ns, D_MODEL), row),
            pl.BlockSpec((None, 1, D_MODEL), lay),
            pl.BlockSpec((None, D_MODEL, D_IN), lay),
            pl.BlockSpec((None, 1, Q_W), lay),
            pl.BlockSpec((None, 1, KV_W), lay),
            pl.BlockSpec((256, 256), lambda i: (0, 0)),
            pl.BlockSpec((None, 2, 256, 256), lambda i: (l, 0, 0, 0)),
            pl.BlockSpec((None, 1, POOL_W), lay),
            pl.BlockSpec((None, ns, POOL_STATE, POOL_W), lambda i: (l, i, 0, 0)),
            pl.BlockSpec((None, ns, WINDOW, KV_W), lambda i: (l, i, 0, 0)),
            pl.BlockSpec((None, ns, WINDOW, KV_W), lambda i: (l, i, 0, 0)),
            pl.BlockSpec((N_HEADS, 1), lambda i: (0, 0)),
            pl.BlockSpec((N_HEADS, WINDOW), lambda i: (0, 0)),
            pl.BlockSpec((2, ns * N_HEADS, ns * N_HEADS), lambda i: (0, 0, 0)),
        ],
        out_specs=[
            pl.BlockSpec((ns, POOL_W), row),
            pl.BlockSpec((ns, Q_W), row),
            pl.BlockSpec((ns, POOL_STATE, POOL_W), lambda i: (i, 0, 0)),
            pl.BlockSpec((ns, WINDOW, KV_W), lambda i: (i, 0, 0)),
            pl.BlockSpec((ns, WINDOW, KV_W), lambda i: (i, 0, 0)),
        ],
        out_shape=[
            jax.ShapeDtypeStruct((n, POOL_W), BF16),
            jax.ShapeDtypeStruct((n, Q_W), BF16),
            jax.ShapeDtypeStruct((n, POOL_STATE, POOL_W), F32),
            jax.ShapeDtypeStruct((n, WINDOW, KV_W), F32),
            jax.ShapeDtypeStruct((n, WINDOW, KV_W), F32),
        ],
        compiler_params=pltpu.CompilerParams(
            dimension_semantics=("arbitrary",), vmem_limit_bytes=VMEM_LIMIT),
        name="sample_mixer",
    )(xs, g_attn, w_in, qg, kg, bd, wp, ps, state2d, ck, cv, sink8, bias_s, perms)


def _merge_router_kernel(pool_ref, attn_ref, x_ref, wout_ref, g_ref, wr_ref, br_ref, utri_ref, cin_ref,
                         x1_ref, h2_ref, route_t_ref, cnt_ref, y_ref, lg_ref):
    i = pl.program_id(0)

    @pl.when(i == 0)
    def _():
        cnt_ref[...] = cin_ref[...]

    tm = x_ref.shape[0]
    rc = tm // MERGE_CHUNKS
    chunks = [slice(ci * rc, (ci + 1) * rc) for ci in range(MERGE_CHUNKS)]
    for rows in chunks:
        y_ref[rows, :] = (jnp.dot(pool_ref[rows, :], wout_ref[0:POOL_W, :], preferred_element_type=F32)
                          + jnp.dot(attn_ref[rows, :], wout_ref[POOL_W:, :], preferred_element_type=F32))
    for rows in chunks:
        x1 = x_ref[rows, :] + y_ref[rows, :]
        x1_ref[rows, :] = x1
        h2 = _rms_bf16(x1, g_ref[...])
        h2_ref[rows, :] = _pack_bf16_pairs(h2)
        lg_ref[rows, :] = jnp.dot(h2, wr_ref[...], preferred_element_type=F32) + br_ref[...]
    logits = lg_ref[...]

    lt = jnp.transpose(logits)
    sub = lax.broadcasted_iota(jnp.int32, (EXPERTS_PER_GROUP, tm), 0)
    neg = -jnp.inf
    big = jnp.int32(EXPERTS_PER_GROUP)
    gl = jnp.where(sub < N_EXPERT_GROUPS, lt[GROUP_LANE0:GROUP_LANE0 + EXPERTS_PER_GROUP, :], neg)
    gmax = jnp.max(gl, axis=0, keepdims=True)
    grp = jnp.min(jnp.where(gl == gmax, sub, big), axis=0, keepdims=True)
    g_w = 1.0 / jnp.sum(jnp.exp(gl - gmax), axis=0, keepdims=True)
    el = lt[(N_EXPERT_GROUPS - 1) * EXPERTS_PER_GROUP:N_EXPERT_GROUPS * EXPERTS_PER_GROUP, :]
    for gi in range(N_EXPERT_GROUPS - 2, -1, -1):
        el = jnp.where(grp == gi, lt[gi * EXPERTS_PER_GROUP:(gi + 1) * EXPERTS_PER_GROUP, :], el)
    v1 = jnp.max(el, axis=0, keepdims=True)
    i1 = jnp.min(jnp.where(el == v1, sub, big), axis=0, keepdims=True)
    el2 = jnp.where(sub == i1, neg, el)
    v2 = jnp.max(el2, axis=0, keepdims=True)
    i2 = jnp.min(jnp.where(el2 == v2, sub, big), axis=0, keepdims=True)
    e21 = jnp.exp(v2 - v1)
    w1 = g_w / (1.0 + e21)
    w2 = g_w * e21 / (1.0 + e21)
    e1 = grp * EXPERTS_PER_GROUP + i1
    e2 = grp * EXPERTS_PER_GROUP + i2

    esub = lax.broadcasted_iota(jnp.int32, (N_EXPERTS, tm), 0)
    oh1 = esub == e1
    oh2 = esub == e2
    c = jnp.where(oh1 | oh2, 1.0, 0.0)
    prefix = jnp.dot(c.astype(BF16), utri_ref[...], preferred_element_type=F32) + cnt_ref[...]
    r1 = jnp.sum(jnp.where(oh1, prefix, 0.0), axis=0, keepdims=True)
    r2 = jnp.sum(jnp.where(oh2, prefix, 0.0), axis=0, keepdims=True)
    cnt_ref[...] = cnt_ref[...] + jnp.sum(c, axis=1, keepdims=True)

    fields = jnp.zeros((ROUTE_FIELDS, tm), F32)
    for idx, val in enumerate((e1.astype(F32), e2.astype(F32), w1, w2, r1, r2)):
        fields = jnp.where(sub == idx, val, fields)
    route_t_ref[...] = fields


def _merge_router(l, pool, attn, x2d, w_out, g_ffn, wr, br, cnt_in, tm):
    t = x2d.shape[0]
    utri = (jnp.arange(tm)[:, None] < jnp.arange(tm)[None, :]).astype(BF16)
    row = lambda i: (i, 0)
    lay = lambda i: (l, 0, 0)
    return pl.pallas_call(
        _merge_router_kernel,
        grid=(t // tm,),
        in_specs=[
            pl.BlockSpec((tm, POOL_W), row),
            pl.BlockSpec((tm, Q_W), row),
            pl.BlockSpec((tm, D_MODEL), row),
            pl.BlockSpec((None, D_MODEL, D_MODEL), lay),
            pl.BlockSpec((None, 1, D_MODEL), lay),
            pl.BlockSpec((None, D_MODEL, LANES), lay),
            pl.BlockSpec((None, 1, LANES), lay),
            pl.BlockSpec((tm, tm), lambda i: (0, 0)),
            pl.BlockSpec((N_EXPERTS, 1), lambda i: (0, 0)),
        ],
        out_specs=[
            pl.BlockSpec((tm, D_MODEL), row),
            pl.BlockSpec((tm, D_MODEL // 2), row),
            pl.BlockSpec((ROUTE_FIELDS, tm), lambda i: (0, i)),
            pl.BlockSpec((N_EXPERTS, 1), lambda i: (0, 0)),
        ],
        out_shape=[
            jax.ShapeDtypeStruct((t, D_MODEL), F32),
            jax.ShapeDtypeStruct((t, D_MODEL // 2), jnp.int32),
            jax.ShapeDtypeStruct((ROUTE_FIELDS, t), F32),
            jax.ShapeDtypeStruct((N_EXPERTS, 1), F32),
        ],
        scratch_shapes=[pltpu.VMEM((tm, D_MODEL), F32), pltpu.VMEM((tm, LANES), F32)],
        compiler_params=pltpu.CompilerParams(
            dimension_semantics=("arbitrary",), vmem_limit_bytes=VMEM_LIMIT),
        name="merge_router",
    )(pool, attn, x2d, w_out, g_ffn, wr, br, utri, cnt_in)


def _moe_kernel(be_ref, rv_ref, nx_ref, sl_ref, xd_ref, wg_hbm, wu_hbm, wd_hbm, yd_ref,
                wg_f, wu_f, wd_f, wg_s, wu_s, wd_s, sem, *, layer):
    step = pl.program_id(0)

    def weight_copies(e, s):
        return [pltpu.make_async_copy(w_hbm.at[layer, e], w_f.at[s], sem.at[s, n])
                for n, (w_hbm, w_f) in enumerate(((wg_hbm, wg_f), (wu_hbm, wu_f), (wd_hbm, wd_f)))]

    @pl.when(step == 0)
    def _():
        for c in weight_copies(be_ref[0], sl_ref[0]):
            c.start()

    for sub_blk in range(MOE_STEP_BLOCKS):
        i = step * MOE_STEP_BLOCKS + sub_blk
        rows = pl.ds(sub_blk * MOE_BM, MOE_BM)
        expert = be_ref[i]
        changed = (i == 0) | (expert != be_ref[jnp.maximum(i - 1, 0)])
        rows_valid = rv_ref[i]
        slot = sl_ref[i]

        @pl.when(changed)
        def _():
            for c in weight_copies(expert, slot):
                c.wait()

            @pl.when(nx_ref[i] >= 0)
            def _():
                for c in weight_copies(nx_ref[i], 1 - slot):
                    c.start(priority=1)

            wg_s[...] = wg_f[slot].astype(BF16)
            wu_s[...] = wu_f[slot].astype(BF16)
            wd_s[...] = wd_f[slot].astype(BF16)

        @pl.when(rows_valid > 0)
        def _():
            row = lax.broadcasted_iota(jnp.int32, (MOE_BM, D_MODEL // 2), 0)
            x = _unpack_bf16_pairs(jnp.where(row < rows_valid, xd_ref[rows, :], 0)).astype(BF16)
            gate = jnp.dot(x, wg_s[...], preferred_element_type=F32)
            up = jnp.dot(x, wu_s[...], preferred_element_type=F32)
            act = (gate * jax.nn.sigmoid(gate) * up).astype(BF16)
            y = jnp.dot(act, wd_s[...], preferred_element_type=F32)
            yd_ref[rows, :] = _pack_bf16_pairs(y.astype(BF16))

        @pl.when(rows_valid <= 0)
        def _():
            yd_ref[rows, :] = jnp.zeros((MOE_BM, D_MODEL // 2), jnp.int32)


def _moe_experts(l, block_e, rows_valid, next_e, slot, xd, w_gate, w_up, w_down):
    n_blocks = xd.shape[0] // MOE_BM
    step_rows = MOE_STEP_BLOCKS * MOE_BM
    row = lambda i, *_: (i, 0)
    return pl.pallas_call(
        functools.partial(_moe_kernel, layer=l),
        grid_spec=pltpu.PrefetchScalarGridSpec(
            num_scalar_prefetch=4,
            grid=(n_blocks // MOE_STEP_BLOCKS,),
            in_specs=[
                pl.BlockSpec((step_rows, D_MODEL // 2), row),
                pl.BlockSpec(memory_space=pl.ANY),
                pl.BlockSpec(memory_space=pl.ANY),
                pl.BlockSpec(memory_space=pl.ANY),
            ],
            out_specs=pl.BlockSpec((step_rows, D_MODEL // 2), row),
            scratch_shapes=[
                pltpu.VMEM((2, D_MODEL, EXPERT_FF), F32),
                pltpu.VMEM((2, D_MODEL, EXPERT_FF), F32),
                pltpu.VMEM((2, EXPERT_FF, D_MODEL), F32),
                pltpu.VMEM((D_MODEL, EXPERT_FF), BF16),
                pltpu.VMEM((D_MODEL, EXPERT_FF), BF16),
                pltpu.VMEM((EXPERT_FF, D_MODEL), BF16),
                pltpu.SemaphoreType.DMA((2, 3)),
            ],
        ),
        out_shape=jax.ShapeDtypeStruct((n_blocks * MOE_BM, D_MODEL // 2), jnp.int32),
        compiler_params=pltpu.CompilerParams(
            dimension_semantics=("arbitrary",), vmem_limit_bytes=VMEM_LIMIT),
        name="moe_experts",
    )(block_e, rows_valid, next_e, slot, xd, w_gate, w_up, w_down)


def _sc_worker_id():
    return lax.axis_index("s") * SC_CORES + lax.axis_index("c")


def _sc_dispatch(hp, hs, dest_p, dest_s, n_rows):
    tp, width = hp.shape
    per_w = tp // SC_WORKERS
    n_ch = per_w // DISP_CH
    n_sw = hs.shape[0] // SAMPLE_CH
    mesh = plsc.VectorSubcoreMesh(core_axis_name="c", subcore_axis_name="s")

    @functools.partial(
        pl.kernel, mesh=mesh,
        out_type=jax.ShapeDtypeStruct((n_rows, width), jnp.int32),
        scratch_types=[
            pltpu.VMEM((2, n_ch, DISP_CH), jnp.int32),
            pltpu.VMEM((2, 1, SAMPLE_CH), jnp.int32),
            pltpu.VMEM((2, DISP_CH, width), jnp.int32),
            pltpu.SemaphoreType.DMA((2,)),
            pltpu.SemaphoreType.DMA((2,)),
        ],
        name="sc_dispatch",
    )
    def k(hp_hbm, hs_hbm, dp_hbm, ds_hbm, xd_hbm, idx_v, idxs_v, bufs, rsem, wsem):
        wid = _sc_worker_id()
        base = wid * per_w
        for kk in range(2):
            pltpu.sync_copy(dp_hbm.at[kk, wid], idx_v.at[kk])
        reads = [pltpu.make_async_copy(hp_hbm.at[pl.ds(base + j * DISP_CH, DISP_CH)],
                                       bufs.at[j % 2], rsem.at[j % 2]) for j in range(n_ch)]
        reads[0].start()
        for j in range(n_ch):
            if j + 1 < n_ch:
                reads[j + 1].start()
            reads[j].wait()
            writes = [pltpu.make_async_copy(bufs.at[j % 2], xd_hbm.at[idx_v.at[kk, j]], wsem.at[kk])
                      for kk in range(2)]
            for w in writes:
                w.start()
            for w in writes:
                w.wait()

        @pl.when(wid < n_sw)
        def _():
            rows = bufs.at[0, pl.ds(0, SAMPLE_CH)]
            for kk in range(2):
                pltpu.sync_copy(ds_hbm.at[kk, wid], idxs_v.at[kk])
            pltpu.sync_copy(hs_hbm.at[pl.ds(wid * SAMPLE_CH, SAMPLE_CH)], rows)
            for kk in range(2):
                pltpu.sync_copy(rows, xd_hbm.at[idxs_v.at[kk, 0]])

    return k(hp, hs, dest_p, dest_s)


def _sc_combine_gather(yd, dest_p, dest_s, tp, ts):
    width = yd.shape[1]
    per_w = tp // SC_WORKERS
    n_ch = per_w // COMB_CH
    n_sw = ts // SAMPLE_CH
    mesh = plsc.VectorSubcoreMesh(core_axis_name="c", subcore_axis_name="s")

    @functools.partial(
        pl.kernel, mesh=mesh,
        out_type=jax.ShapeDtypeStruct((2, tp + ts, width), yd.dtype),
        scratch_types=[
            pltpu.VMEM((2, n_ch, COMB_CH), jnp.int32),
            pltpu.VMEM((2, 1, SAMPLE_CH), jnp.int32),
            pltpu.VMEM((2, COMB_CH, width), yd.dtype),
            pltpu.SemaphoreType.DMA((2,)),
            pltpu.SemaphoreType.DMA((2,)),
        ],
        name="sc_combine_gather",
    )
    def k(yd_hbm, dp_hbm, ds_hbm, g_hbm, idx_v, idxs_v, bufs, gsem, wsem):
        wid = _sc_worker_id()
        base = wid * per_w
        for kk in range(2):
            pltpu.sync_copy(dp_hbm.at[kk, wid], idx_v.at[kk])
        items = [(kk, j) for kk in range(2) for j in range(n_ch)]
        gathers = [pltpu.make_async_copy(yd_hbm.at[idx_v.at[kk, j]], bufs.at[n % 2], gsem.at[n % 2])
                   for n, (kk, j) in enumerate(items)]
        gathers[0].start()
        for n, (kk, j) in enumerate(items):
            if n + 1 < len(items):
                gathers[n + 1].start()
            gathers[n].wait()
            w = pltpu.make_async_copy(bufs.at[n % 2], g_hbm.at[kk, pl.ds(base + j * COMB_CH, COMB_CH)],
                                      wsem.at[n % 2])
            w.start()
            w.wait()

        @pl.when(wid < n_sw)
        def _():
            for kk in range(2):
                pltpu.sync_copy(ds_hbm.at[kk, wid], idxs_v.at[kk])
            for kk in range(2):
                rows = bufs.at[kk, pl.ds(0, SAMPLE_CH)]
                pltpu.sync_copy(yd_hbm.at[idxs_v.at[kk, 0]], rows)
                pltpu.sync_copy(rows, g_hbm.at[kk, pl.ds(tp + wid * SAMPLE_CH, SAMPLE_CH)])

    return k(yd, dest_p, dest_s)


def _combine_kernel(x1_ref, g_ref, route_t_ref, x2_ref):
    fields = route_t_ref[...]
    tm = fields.shape[1]
    cols = jnp.transpose(jnp.concatenate([fields, jnp.zeros((LANES - ROUTE_FIELDS, tm), F32)], axis=0))
    w1 = cols[:, 2:3]
    w2 = cols[:, 3:4]
    x2_ref[...] = x1_ref[...] + _unpack_bf16_pairs(g_ref[0]) * w1 + _unpack_bf16_pairs(g_ref[1]) * w2


def _combine(x1, g, route_t, row0, tm):
    t = x1.shape[0]
    blk0 = row0 // tm
    row = lambda i: (i, 0)
    return pl.pallas_call(
        _combine_kernel,
        grid=(t // tm,),
        in_specs=[
            pl.BlockSpec((tm, D_MODEL), row),
            pl.BlockSpec((2, tm, D_MODEL // 2), lambda i: (0, blk0 + i, 0)),
            pl.BlockSpec((ROUTE_FIELDS, tm), lambda i: (0, i)),
        ],
        out_specs=pl.BlockSpec((tm, D_MODEL), row),
        out_shape=jax.ShapeDtypeStruct((t, D_MODEL), F32),
        compiler_params=pltpu.CompilerParams(
            dimension_semantics=("arbitrary",), vmem_limit_bytes=VMEM_LIMIT),
        name="combine",
    )(x1, g, route_t)


def _dest_layout(dest, workers, chunk):
    t = dest.shape[1]
    return dest.reshape(2, workers, t // (workers * chunk), chunk)


def _hier_moe(l, h2p, h2s, route_tp, route_ts, counts, w_gate, w_up, w_down):
    tp, ts = h2p.shape[0], h2s.shape[0]
    n_assign = 2 * (tp + ts)
    n_blocks = -(-n_assign // MOE_BM) + N_EXPERTS
    n_blocks = -(-n_blocks // MOE_STEP_BLOCKS) * MOE_STEP_BLOCKS
    pcounts = (counts + MOE_BM - 1) // MOE_BM * MOE_BM
    pends = jnp.cumsum(pcounts)
    poffsets = pends - pcounts
    starts = jnp.arange(n_blocks, dtype=jnp.int32) * MOE_BM
    block_e = jnp.minimum(jnp.sum((pends[None, :] <= starts[:, None]).astype(jnp.int32), axis=1),
                          N_EXPERTS - 1)
    experts = jnp.arange(N_EXPERTS, dtype=jnp.int32)

    def lookup(table, idx):
        return jnp.sum(jnp.where(idx[..., None] == experts, table, 0), axis=-1)

    rows_valid = jnp.clip(lookup(poffsets + counts, block_e) - starts, 0, MOE_BM).astype(jnp.int32)
    used = counts > 0
    last_e = jnp.max(jnp.where(used, jnp.arange(N_EXPERTS, dtype=jnp.int32), 0))
    block_e = jnp.where(rows_valid > 0, block_e, last_e).astype(jnp.int32)
    later = used[None, :] & (jnp.arange(N_EXPERTS)[None, :] > jnp.arange(N_EXPERTS)[:, None])
    next_used = jnp.min(jnp.where(later, jnp.arange(N_EXPERTS, dtype=jnp.int32)[None, :], N_EXPERTS), axis=1)
    next_used = jnp.where(next_used >= N_EXPERTS, -1, next_used).astype(jnp.int32)
    slot_of = ((jnp.cumsum(used.astype(jnp.int32)) - 1) & 1).astype(jnp.int32)
    next_e = lookup(next_used, block_e)
    slot = lookup(slot_of, block_e)

    def dest_of(route_t):
        return lookup(poffsets, route_t[0:2].astype(jnp.int32)) + route_t[4:6].astype(jnp.int32)

    dest_p, dest_s = dest_of(route_tp), dest_of(route_ts)
    n_sw = ts // SAMPLE_CH
    xd = _sc_dispatch(h2p, h2s, _dest_layout(dest_p, SC_WORKERS, DISP_CH),
                      _dest_layout(dest_s, n_sw, SAMPLE_CH), n_blocks * MOE_BM)
    yd = _moe_experts(l, block_e, rows_valid, next_e, slot, xd, w_gate, w_up, w_down)
    return _sc_combine_gather(yd, _dest_layout(dest_p, SC_WORKERS, COMB_CH),
                              _dest_layout(dest_s, n_sw, SAMPLE_CH), tp, ts)


def kernel(x_prompt, x_sample, state_pool, cache_k_win, cache_v_win, norm_attn_g, w_in, pool_w, pool_scale, q_norm_g, k_norm_g, attn_sinks, w_out, norm_ffn_g, router_group_w, router_group_b, router_expert_w, router_expert_b, w_gate, w_up, w_down):
    n_p, t_p, d = x_prompt.shape
    n_s, t_s, _ = x_sample.shape
    depth = w_in.shape[0]
    lw_s = cache_k_win.shape[2]
    assert t_s == 1 and lw_s == WINDOW and d == D_MODEL
    assert t_p % TM_PROJ == 0 and t_p >= WINDOW
    past_len = 16384

    seg = jnp.arange(256) // HEAD_DIM
    bd = jnp.where(seg[:, None] == seg[None, :], 1.0 / HEAD_DIM, 0.0).astype(BF16)
    slopes = jnp.exp2(-8.0 * jnp.arange(1, N_HEADS + 1, dtype=F32) / N_HEADS)
    bias_p = _prompt_bias_t()
    dist_s = (WINDOW - 1) - jnp.arange(WINDOW, dtype=F32)
    bias_s = -slopes[:, None] * dist_s[None, :]

    wp = jnp.zeros((depth, 2, 256, 256), F32)
    for p in range(2):
        wp = wp.at[:, p, :POOL_GC, :POOL_GC].set(pool_w[:, 2 * p])
        wp = wp.at[:, p, POOL_GC:, POOL_GC:].set(pool_w[:, 2 * p + 1])
    wr = jnp.zeros((depth, D_MODEL, LANES), F32)
    wr = wr.at[:, :, :N_EXPERTS].set(router_expert_w)
    wr = wr.at[:, :, GROUP_LANE0:GROUP_LANE0 + N_EXPERT_GROUPS].set(router_group_w)
    br = jnp.zeros((depth, 1, LANES), F32)
    br = br.at[:, 0, :N_EXPERTS].set(router_expert_b)
    br = br.at[:, 0, GROUP_LANE0:GROUP_LANE0 + N_EXPERT_GROUPS].set(router_group_b)
    lp = dict(
        w_in=w_in.astype(BF16),
        w_out=w_out.astype(BF16),
        g_attn=norm_attn_g.reshape(depth, 1, D_MODEL),
        g_ffn=norm_ffn_g.reshape(depth, 1, D_MODEL),
        qg=(jnp.tile(q_norm_g, (1, N_HEADS)) * ATTN_SCALE).reshape(depth, 1, Q_W),
        kg=jnp.tile(k_norm_g, (1, N_KV_HEADS)).reshape(depth, 1, KV_W),
        wp=wp.astype(BF16),
        ps=pool_scale.reshape(depth, 1, POOL_W),
        wr=wr.astype(BF16),
        br=br,
        state=state_pool,
        ck=cache_k_win.reshape(depth, n_s, lw_s, KV_W),
        cv=cache_v_win.reshape(depth, n_s, lw_s, KV_W),
    )

    xp = x_prompt.reshape(n_p * t_p, D_MODEL)
    xs = x_sample.reshape(n_s, D_MODEL)
    lw_p = min(WINDOW, t_p)
    pool_p, kp_new, vp_new, pool_s, ks_new, vs_new = [], [], [], [], [], []
    zero_cnt = jnp.zeros((N_EXPERTS, 1), F32)
    pending = None
    for l in range(depth):
        sinks = attn_sinks[l]
        outs = _proj_pool_prompt(
            l, xp if pending is None else pending, n_p, t_p,
            lp["g_attn"], lp["w_in"], lp["qg"], lp["kg"], bd, lp["wp"], lp["ps"])
        if pending is not None:
            xp, outs = outs[0], outs[1:]
        pool_o, q, k, vt, utail, ktail, vtail = outs
        attn_o = _attn_prompt(q, k, vt, bias_p, sinks, n_p, t_p)
        x1p, h2p, route_tp, cnt_p = _merge_router(
            l, pool_o, attn_o, xp, lp["w_out"], lp["g_ffn"], lp["wr"], lp["br"], zero_cnt, TM_PROJ)
        pool_p.append(utail[:, 16 - POOL_STATE:, :])
        kp_new.append(ktail)
        vp_new.append(vtail)
        pool_so, attn_so, pst_s, kc_s, vc_s = _sample_mixer(
            l, xs, lp["g_attn"], lp["w_in"], lp["qg"], lp["kg"], bd, lp["wp"], lp["ps"],
            lp["state"], lp["ck"], lp["cv"], sinks.reshape(N_HEADS, 1), bias_s, past_len)
        x1s, h2s, route_ts, cnt_all = _merge_router(
            l, pool_so, attn_so, xs, lp["w_out"], lp["g_ffn"], lp["wr"], lp["br"], cnt_p, n_s)
        pool_s.append(pst_s)
        ks_new.append(kc_s)
        vs_new.append(vc_s)
        counts = cnt_all[:, 0].astype(jnp.int32)
        g = _hier_moe(l, h2p, h2s, route_tp, route_ts, counts, w_gate, w_up, w_down)
        xs = _combine(x1s, g, route_ts, n_p * t_p, n_s)
        pending = (x1p, g, route_tp)
    xp = _combine(*pending, 0, TM_PROJ)
    return (xp.reshape(n_p, t_p, D_MODEL), xs.reshape(n_s, t_s, D_MODEL),
            jnp.stack(pool_p),
            jnp.stack(kp_new).reshape(depth, n_p, lw_p, N_KV_HEADS, HEAD_DIM),
            jnp.stack(vp_new).reshape(depth, n_p, lw_p, N_KV_HEADS, HEAD_DIM),
            jnp.stack(pool_s),
            jnp.stack(ks_new).reshape(depth, n_s, lw_s, N_KV_HEADS, HEAD_DIM),
            jnp.stack(vs_new).reshape(depth, n_s, lw_s, N_KV_HEADS, HEAD_DIM))
```

```python
import functools

import jax
import jax.numpy as jnp
from jax import lax
from jax.experimental import pallas as pl
from jax.experimental.pallas import tpu as pltpu
from jax.experimental.pallas import tpu_sc as plsc

D_MODEL = 1024
POOL_W = 512
POOL_WINDOWS = (2, 4, 8, 16)
POOL_GC = 128
POOL_STATE = 15
HEAD_DIM = 64
N_HEADS = 8
N_KV_HEADS = 2
GQA_GROUP = 4
Q_W = 512
KV_W = 128
D_IN = POOL_W + Q_W + 2 * KV_W
WINDOW = 128
ATTN_SCALE = HEAD_DIM ** -0.5
N_EXPERT_GROUPS = 4
EXPERTS_PER_GROUP = 8
N_EXPERTS = 32
EXPERT_FF = 512
EPS = 1e-6

LANES = 128
HALO = 32
TM_PROJ = 512
MERGE_CHUNKS = 2
ATTN_QB = 8
MOE_BM = 256
MOE_STEP_BLOCKS = 4
GROUP_LANE0 = 32
ROUTE_FIELDS = 8
SC_CORES = 2
SC_SUBCORES = 16
SC_WORKERS = SC_CORES * SC_SUBCORES
DISP_CH = 64
COMB_CH = 64
SAMPLE_CH = 32
VMEM_LIMIT = 48 * 1024 * 1024

BF16 = jnp.bfloat16
F32 = jnp.float32


def _pack_bf16_pairs(h):
    w = h.shape[1] // 2
    hi = lax.bitcast_convert_type(h[:, :w].astype(F32), jnp.uint32)
    lo = lax.bitcast_convert_type(h[:, w:].astype(F32), jnp.uint32)
    return lax.bitcast_convert_type(hi | (lo >> 16), jnp.int32)


def _unpack_bf16_pairs(words):
    u = lax.bitcast_convert_type(words, jnp.uint32)
    hi = lax.bitcast_convert_type(u & jnp.uint32(0xFFFF0000), F32)
    lo = lax.bitcast_convert_type(u << 16, F32)
    return jnp.concatenate([hi, lo], axis=-1)


def _segment_mean_sq(a, bd):
    w = a.shape[1]
    return jnp.dot((a * a).astype(BF16), bd[:w, :w], preferred_element_type=F32)


def _rms_bf16(x, g):
    ms = jnp.mean(x * x, axis=-1, keepdims=True)
    return (x * lax.rsqrt(ms + EPS) * g).astype(BF16)


def _qk_norm(q, k, qg, kg, bd):
    qn = []
    for c in range(Q_W // 256):
        qc = q[:, c * 256:(c + 1) * 256]
        qn.append(qc * lax.rsqrt(_segment_mean_sq(qc, bd) + EPS))
    qn = jnp.concatenate(qn, axis=-1) * qg
    kn = k * lax.rsqrt(_segment_mean_sq(k, bd) + EPS) * kg
    return qn, kn


def _project(x, g, w_in, qg, kg, bd):
    z = jnp.dot(_rms_bf16(x, g), w_in, preferred_element_type=F32)
    u = z[:, :POOL_W]
    q = z[:, POOL_W:POOL_W + Q_W]
    k = z[:, POOL_W + Q_W:POOL_W + Q_W + KV_W]
    v = z[:, POOL_W + Q_W + KV_W:]
    qn, kn = _qk_norm(q, k, qg, kg, bd)
    return u, qn, kn, v


def _pool_project(d_groups, wp_ref, ps):
    outs = []
    for p in range(2):
        dp = jnp.concatenate([d_groups[2 * p], d_groups[2 * p + 1]], axis=-1).astype(BF16)
        y = jnp.dot(dp, wp_ref[p], preferred_element_type=F32)
        outs.append(y * ps[:, p * 256:(p + 1) * 256])
    return jnp.concatenate(outs, axis=-1)


def _proj_pool_kernel(x_ref, g_ref, win_ref, qg_ref, kg_ref, bd_ref, wp_ref, ps_ref,
                      pool_ref, q_ref, k_ref, vt_ref, utail_ref, ktail_ref, vtail_ref,
                      ext_ref, sa_ref, sb_ref, zq_ref, *, tm, n_j):
    j = pl.program_id(1)

    @pl.when(j == 0)
    def _():
        ext_ref[0:HALO, :] = jnp.zeros((HALO, POOL_W), F32)

    r = tm + HALO
    h = _rms_bf16(x_ref[...], g_ref[...])
    ext_ref[HALO:r, :] = jnp.dot(h, win_ref[:, 0:POOL_W], preferred_element_type=F32)
    zq_ref[...] = jnp.dot(h, win_ref[:, POOL_W:], preferred_element_type=F32)
    u = ext_ref[HALO:r, :]
    sa_ref[8:r, :] = ext_ref[8:r, :] + ext_ref[7:r - 1, :]
    sb_ref[16:r, 128:] = sa_ref[16:r, 128:] + sa_ref[14:r - 2, 128:]
    sa_ref[24:r, 256:] = sb_ref[24:r, 256:] + sb_ref[20:r - 4, 256:]
    sb_ref[32:r, 384:] = sa_ref[32:r, 384:] + sa_ref[24:r - 8, 384:]
    pos1 = j * tm + lax.broadcasted_iota(jnp.int32, (tm, POOL_GC), 0) + 1
    sums = (sa_ref, sb_ref, sa_ref, sb_ref)
    d_groups = []
    for gi, w in enumerate(POOL_WINDOWS):
        sl = slice(gi * POOL_GC, (gi + 1) * POOL_GC)
        cnt = jnp.minimum(pos1, w).astype(F32)
        d_groups.append(sums[gi][HALO:r, sl] / cnt - u[:, sl])
    pool_ref[...] = _pool_project(d_groups, wp_ref, ps_ref[...]).astype(BF16)
    ext_ref[16:HALO, :] = ext_ref[tm + 16:r, :]

    qn, kn = _qk_norm(zq_ref[:, 0:Q_W], zq_ref[:, Q_W:Q_W + KV_W], qg_ref[...], kg_ref[...], bd_ref[...])
    v = zq_ref[:, Q_W + KV_W:]
    q_ref[...] = qn.astype(BF16)
    k_ref[...] = kn.astype(BF16)
    vt_ref[...] = jnp.transpose(v).astype(BF16)

    @pl.when(j == n_j - 1)
    def _():
        utail_ref[...] = u[tm - 16:, :]
        ktail_ref[...] = kn[tm - WINDOW:, :]
        vtail_ref[...] = v[tm - WINDOW:, :]


def _proj_pool_combine_kernel(x1_ref, gath_ref, route_ref, *rest, tm, n_j):
    x2_ref = rest[7]
    _combine_kernel(x1_ref, gath_ref, route_ref, x2_ref)
    _proj_pool_kernel(x2_ref, *rest[:7], *rest[8:], tm=tm, n_j=n_j)


def _proj_pool_prompt(l, x_in, n_seq, seq, g_attn, w_in, qg, kg, bd, wp, ps):
    tm = TM_PROJ
    n_j = seq // tm
    t = n_seq * seq
    row = lambda b, j: (b * n_j + j, 0)
    lay = lambda b, j: (l, 0, 0)
    fused = isinstance(x_in, tuple)
    if fused:
        kern = _proj_pool_combine_kernel
        x_args = list(x_in)
        x_specs = [pl.BlockSpec((tm, D_MODEL), row),
                   pl.BlockSpec((2, tm, D_MODEL // 2), lambda b, j: (0, b * n_j + j, 0)),
                   pl.BlockSpec((ROUTE_FIELDS, tm), lambda b, j: (0, b * n_j + j))]
        x_out_specs = [pl.BlockSpec((tm, D_MODEL), row)]
        x_out_shape = [jax.ShapeDtypeStruct((t, D_MODEL), F32)]
    else:
        kern = _proj_pool_kernel
        x_args = [x_in]
        x_specs = [pl.BlockSpec((tm, D_MODEL), row)]
        x_out_specs, x_out_shape = [], []
    return pl.pallas_call(
        functools.partial(kern, tm=tm, n_j=n_j),
        grid=(n_seq, n_j),
        in_specs=x_specs + [
            pl.BlockSpec((None, 1, D_MODEL), lay),
            pl.BlockSpec((None, D_MODEL, D_IN), lay),
            pl.BlockSpec((None, 1, Q_W), lay),
            pl.BlockSpec((None, 1, KV_W), lay),
            pl.BlockSpec((256, 256), lambda b, j: (0, 0)),
            pl.BlockSpec((None, 2, 256, 256), lambda b, j: (l, 0, 0, 0)),
            pl.BlockSpec((None, 1, POOL_W), lay),
        ],
        out_specs=x_out_specs + [
            pl.BlockSpec((tm, POOL_W), row),
            pl.BlockSpec((tm, Q_W), row),
            pl.BlockSpec((tm, KV_W), row),
            pl.BlockSpec((KV_W, tm), lambda b, j: (0, b * n_j + j)),
            pl.BlockSpec((None, 16, POOL_W), lambda b, j: (b, 0, 0)),
            pl.BlockSpec((None, WINDOW, KV_W), lambda b, j: (b, 0, 0)),
            pl.BlockSpec((None, WINDOW, KV_W), lambda b, j: (b, 0, 0)),
        ],
        out_shape=x_out_shape + [
            jax.ShapeDtypeStruct((t, POOL_W), BF16),
            jax.ShapeDtypeStruct((t, Q_W), BF16),
            jax.ShapeDtypeStruct((t, KV_W), BF16),
            jax.ShapeDtypeStruct((KV_W, t), BF16),
            jax.ShapeDtypeStruct((n_seq, 16, POOL_W), F32),
            jax.ShapeDtypeStruct((n_seq, WINDOW, KV_W), F32),
            jax.ShapeDtypeStruct((n_seq, WINDOW, KV_W), F32),
        ],
        scratch_shapes=[pltpu.VMEM((tm + HALO, POOL_W), F32)] * 3 + [pltpu.VMEM((tm, Q_W + 2 * KV_W), F32)],
        compiler_params=pltpu.CompilerParams(
            dimension_semantics=("arbitrary", "arbitrary"), vmem_limit_bytes=VMEM_LIMIT),
        name="proj_pool_prompt",
    )(*x_args, g_attn, w_in, qg, kg, bd, wp, ps)


def _attn_kernel(sink_ref, q_ref, kp_ref, kc_ref, vtp_ref, vtc_ref, bias_ref, o_ref, s_ref):
    j = pl.program_id(1)
    kk_all = jnp.concatenate([kp_ref[...], kc_ref[...]], axis=0)
    vt_all = jnp.concatenate([vtp_ref[...], vtc_ref[...]], axis=1)
    from_prev = (lax.broadcasted_iota(jnp.int32, (WINDOW, WINDOW), 0)
                 > lax.broadcasted_iota(jnp.int32, (WINDOW, WINDOW), 1))
    units = [(blk, kv) for blk in range(ATTN_QB) for kv in range(N_KV_HEADS)]

    def scores(n):
        blk, kv = units[n]
        q = q_ref[blk * WINDOW:(blk + 1) * WINDOW, :]
        kk = kk_all[blk * WINDOW:(blk + 2) * WINDOW, kv * HEAD_DIM:(kv + 1) * HEAD_DIM]
        heads = range(kv * GQA_GROUP, (kv + 1) * GQA_GROUP)
        q_rows = jnp.concatenate([q[:, h * HEAD_DIM:(h + 1) * HEAD_DIM] for h in heads], axis=0)
        s_ref[n % 2] = lax.dot_general(kk, q_rows, (((1,), (1,)), ((), ())), preferred_element_type=F32)

    scores(0)
    outs = []
    for n, (blk, kv) in enumerate(units):
        if n + 1 < len(units):
            scores(n + 1)
        vt_kv = vt_all[kv * HEAD_DIM:(kv + 1) * HEAD_DIM, blk * WINDOW:(blk + 2) * WINDOW]
        variant = jnp.minimum(j, 1) if blk == 0 else 1
        for g in range(GQA_GROUP):
            h = kv * GQA_GROUP + g
            s = jnp.where(from_prev, s_ref[n % 2, 0:WINDOW, g * WINDOW:(g + 1) * WINDOW],
                          s_ref[n % 2, WINDOW:, g * WINDOW:(g + 1) * WINDOW]) + bias_ref[variant, h]
            sink = sink_ref[h]
            m = jnp.maximum(jnp.max(s, axis=0, keepdims=True), sink)
            p = jnp.exp(s - m)
            denom = jnp.sum(p, axis=0, keepdims=True) + jnp.exp(sink - m)
            p_keys = jnp.concatenate([jnp.where(from_prev, p, 0.0), jnp.where(from_prev, 0.0, p)], axis=0)
            o_t = jnp.dot(vt_kv, p_keys.astype(BF16), preferred_element_type=F32)
            outs.append(o_t / denom)
        if kv == N_KV_HEADS - 1:
            o_ref[blk * WINDOW:(blk + 1) * WINDOW, :] = jnp.transpose(jnp.concatenate(outs, axis=0)).astype(BF16)
            outs = []


def _attn_prompt(q, k, vt, bias_t, sinks, n_seq, seq):
    tq = ATTN_QB * WINDOW
    nj = seq // tq
    t = n_seq * seq
    cur = lambda b, j: (b * nj + j, 0)
    prev = lambda b, j: (jnp.maximum((b * nj + j) * ATTN_QB - 1, 0), 0)
    cur_t = lambda b, j: (0, b * nj + j)
    prev_t = lambda b, j: (0, jnp.maximum((b * nj + j) * ATTN_QB - 1, 0))
    return pl.pallas_call(
        _attn_kernel,
        grid=(n_seq, nj),
        in_specs=[
            pl.BlockSpec(memory_space=pltpu.SMEM),
            pl.BlockSpec((tq, Q_W), cur),
            pl.BlockSpec((WINDOW, KV_W), prev),
            pl.BlockSpec((tq, KV_W), cur),
            pl.BlockSpec((KV_W, WINDOW), prev_t),
            pl.BlockSpec((KV_W, tq), cur_t),
            pl.BlockSpec((2, N_HEADS, WINDOW, WINDOW), lambda b, j: (0, 0, 0, 0)),
        ],
        out_specs=pl.BlockSpec((tq, Q_W), cur),
        out_shape=jax.ShapeDtypeStruct((t, Q_W), BF16),
        scratch_shapes=[pltpu.VMEM((2, 2 * WINDOW, GQA_GROUP * WINDOW), F32)],
        compiler_params=pltpu.CompilerParams(
            dimension_semantics=("arbitrary", "arbitrary"), vmem_limit_bytes=VMEM_LIMIT),
        name="attn_prompt",
    )(sinks, q, k, k, vt, vt, bias_t)


def _prompt_bias_t():
    r = jnp.arange(WINDOW, dtype=jnp.int32)[None, :]
    c = jnp.arange(WINDOW, dtype=jnp.int32)[:, None]
    from_prev = c > r
    dist = r - c + jnp.where(from_prev, WINDOW, 0)
    slopes = jnp.exp2(-8.0 * jnp.arange(1, N_HEADS + 1, dtype=F32) / N_HEADS)
    later = -slopes[:, None, None] * dist.astype(F32)[None]
    first = jnp.where(from_prev[None], -jnp.inf, later)
    return jnp.stack([first, later])


def _sample_kernel(x_ref, g_ref, win_ref, qg_ref, kg_ref, bd_ref, wp_ref, ps_ref,
                   st_ref, ck_ref, cv_ref, sink_ref, bias_ref, perm_ref,
                   pool_ref, attn_ref, pst_ref, kc_ref, vc_ref, *, ns, pos0):
    u, qn, kn, v = _project(x_ref[...], g_ref[...], win_ref[...], qg_ref[...], kg_ref[...], bd_ref[...])
    pst_ref[:, 0:POOL_STATE - 1, :] = st_ref[:, 1:POOL_STATE, :]
    kc_ref[:, 0:WINDOW - 1, :] = ck_ref[:, 1:WINDOW, :]
    vc_ref[:, 0:WINDOW - 1, :] = cv_ref[:, 1:WINDOW, :]
    for n in range(ns):
        pst_ref[n, POOL_STATE - 1:POOL_STATE, :] = u[n:n + 1, :]
        kc_ref[n, WINDOW - 1:WINDOW, :] = kn[n:n + 1, :]
        vc_ref[n, WINDOW - 1:WINDOW, :] = v[n:n + 1, :]

    d_groups = []
    for gi, w in enumerate(POOL_WINDOWS):
        lo = gi * POOL_GC
        acc = u[:, lo:lo + POOL_GC]
        for back in range(1, w):
            acc = acc + st_ref[:, POOL_STATE - back, lo:lo + POOL_GC]
        d_groups.append(acc / float(min(pos0 + 1, w)) - u[:, lo:lo + POOL_GC])
    pool_ref[...] = _pool_project(d_groups, wp_ref, ps_ref[...]).astype(BF16)

    zeros = jnp.zeros((ns, HEAD_DIM), F32)
    stacked = []
    for h in range(N_HEADS):
        piece = qn[:, h * HEAD_DIM:(h + 1) * HEAD_DIM]
        pair = [piece, zeros] if h < GQA_GROUP else [zeros, piece]
        stacked.append(jnp.concatenate(pair, axis=-1))
    q_hn = jnp.concatenate(stacked, axis=0).astype(BF16)
    q_nh = jnp.dot(perm_ref[0], q_hn, preferred_element_type=F32).astype(BF16)

    keys = kc_ref[...].reshape(ns * WINDOW, KV_W).astype(BF16)
    vals = vc_ref[...].reshape(ns * WINDOW, KV_W).astype(BF16)
    s_all = lax.dot_general(q_nh, keys, (((1,), (1,)), ((), ())), preferred_element_type=F32)
    sink = sink_ref[...]
    bias = bias_ref[...]
    zero_blk = jnp.zeros((N_HEADS, WINDOW), F32)
    p_rows = []
    for n in range(ns):
        s = s_all[n * N_HEADS:(n + 1) * N_HEADS, n * WINDOW:(n + 1) * WINDOW] + bias
        m = jnp.maximum(jnp.max(s, axis=-1, keepdims=True), sink)
        p = jnp.exp(s - m)
        denom = jnp.sum(p, axis=-1, keepdims=True) + jnp.exp(sink - m)
        p_rows.append(jnp.concatenate([zero_blk] * n + [p / denom] + [zero_blk] * (ns - 1 - n), axis=-1))
    p_blockdiag = jnp.concatenate(p_rows, axis=0).astype(BF16)
    o_nh = jnp.dot(p_blockdiag, vals, preferred_element_type=F32).astype(BF16)
    o_hn = jnp.dot(perm_ref[1], o_nh, preferred_element_type=F32)
    pieces = []
    for h in range(N_HEADS):
        kv = h // GQA_GROUP
        pieces.append(o_hn[h * ns:(h + 1) * ns, kv * HEAD_DIM:(kv + 1) * HEAD_DIM])
    attn_ref[...] = jnp.concatenate(pieces, axis=-1).astype(BF16)


def _sample_mixer(l, xs, g_attn, w_in, qg, kg, bd, wp, ps, state2d, ck, cv, sink8, bias_s, pos0):
    n = xs.shape[0]
    ns = 32
    row = lambda i: (i, 0)
    lay = lambda i: (l, 0, 0)
    src = jnp.arange(ns * N_HEADS)
    perm = (((src % N_HEADS) * ns + src // N_HEADS)[:, None] == src[None, :]).astype(BF16)
    perms = jnp.stack([perm, perm.T])
    return pl.pallas_call(
        functools.partial(_sample_kernel, ns=ns, pos0=pos0),
        grid=(n // ns,),
        in_specs=[
            pl.BlockSpec((ns, D_MODEL), row),
            pl.BlockSpec((None, 1, D_MODEL), lay),
            pl.BlockSpec((None, D_MODEL, D_IN), lay),
            pl.BlockSpec((None, 1, Q_W), lay),
            pl.BlockSpec((None, 1, KV_W), lay),
            pl.BlockSpec((256, 256), lambda i: (0, 0)),
            pl.BlockSpec((None, 2, 256, 256), lambda i: (l, 0, 0, 0)),
            pl.BlockSpec((None, 1, POOL_W), lay),
            pl.BlockSpec((None, ns, POOL_STATE, POOL_W), lambda i: (l, i, 0, 0)),
            pl.BlockSpec((None, ns, WINDOW, KV_W), lambda i: (l, i, 0, 0)),
            pl.BlockSpec((None, ns, WINDOW, KV_W), lambda i: (l, i, 0, 0)),
            pl.BlockSpec((N_HEADS, 1), lambda i: (0, 0)),
            pl.BlockSpec((N_HEADS, WINDOW), lambda i: (0, 0)),
            pl.BlockSpec((2, ns * N_HEADS, ns * N_HEADS), lambda i: (0, 0, 0)),
        ],
        out_specs=[
            pl.BlockSpec((ns, POOL_W), row),
            pl.BlockSpec((ns, Q_W), row),
            pl.BlockSpec((ns, POOL_STATE, POOL_W), lambda i: (i, 0, 0)),
            pl.BlockSpec((ns, WINDOW, KV_W), lambda i: (i, 0, 0)),
            pl.BlockSpec((ns, WINDOW, KV_W), lambda i: (i, 0, 0)),
        ],
        out_shape=[
            jax.ShapeDtypeStruct((n, POOL_W), BF16),
            jax.ShapeDtypeStruct((n, Q_W), BF16),
            jax.ShapeDtypeStruct((n, POOL_STATE, POOL_W), F32),
            jax.ShapeDtypeStruct((n, WINDOW, KV_W), F32),
            jax.ShapeDtypeStruct((n, WINDOW, KV_W), F32),
        ],
        compiler_params=pltpu.CompilerParams(
            dimension_semantics=("arbitrary",), vmem_limit_bytes=VMEM_LIMIT),
        name="sample_mixer",
    )(xs, g_attn, w_in, qg, kg, bd, wp, ps, state2d, ck, cv, sink8, bias_s, perms)


def _merge_router_kernel(pool_ref, attn_ref, x_ref, wout_ref, g_ref, wr_ref, br_ref, utri_ref, cin_ref,
                         x1_ref, h2_ref, route_t_ref, cnt_ref, y_ref, lg_ref):
    i = pl.program_id(0)

    @pl.when(i == 0)
    def _():
        cnt_ref[...] = cin_ref[...]

    tm = x_ref.shape[0]
    rc = tm // MERGE_CHUNKS
    chunks = [slice(ci * rc, (ci + 1) * rc) for ci in range(MERGE_CHUNKS)]
    for rows in chunks:
        y_ref[rows, :] = (jnp.dot(pool_ref[rows, :], wout_ref[0:POOL_W, :], preferred_element_type=F32)
                          + jnp.dot(attn_ref[rows, :], wout_ref[POOL_W:, :], preferred_element_type=F32))
    for rows in chunks:
        x1 = x_ref[rows, :] + y_ref[rows, :]
        x1_ref[rows, :] = x1
        h2 = _rms_bf16(x1, g_ref[...])
        h2_ref[rows, :] = _pack_bf16_pairs(h2)
        lg_ref[rows, :] = jnp.dot(h2, wr_ref[...], preferred_element_type=F32) + br_ref[...]
    logits = lg_ref[...]

    lt = jnp.transpose(logits)
    sub = lax.broadcasted_iota(jnp.int32, (EXPERTS_PER_GROUP, tm), 0)
    neg = -jnp.inf
    big = jnp.int32(EXPERTS_PER_GROUP)
    gl = jnp.where(sub < N_EXPERT_GROUPS, lt[GROUP_LANE0:GROUP_LANE0 + EXPERTS_PER_GROUP, :], neg)
    gmax = jnp.max(gl, axis=0, keepdims=True)
    grp = jnp.min(jnp.where(gl == gmax, sub, big), axis=0, keepdims=True)
    g_w = 1.0 / jnp.sum(jnp.exp(gl - gmax), axis=0, keepdims=True)
    el = lt[(N_EXPERT_GROUPS - 1) * EXPERTS_PER_GROUP:N_EXPERT_GROUPS * EXPERTS_PER_GROUP, :]
    for gi in range(N_EXPERT_GROUPS - 2, -1, -1):
        el = jnp.where(grp == gi, lt[gi * EXPERTS_PER_GROUP:(gi + 1) * EXPERTS_PER_GROUP, :], el)
    v1 = jnp.max(el, axis=0, keepdims=True)
    i1 = jnp.min(jnp.where(el == v1, sub, big), axis=0, keepdims=True)
    el2 = jnp.where(sub == i1, neg, el)
    v2 = jnp.max(el2, axis=0, keepdims=True)
    i2 = jnp.min(jnp.where(el2 == v2, sub, big), axis=0, keepdims=True)
    e21 = jnp.exp(v2 - v1)
    w1 = g_w / (1.0 + e21)
    w2 = g_w * e21 / (1.0 + e21)
    e1 = grp * EXPERTS_PER_GROUP + i1
    e2 = grp * EXPERTS_PER_GROUP + i2

    esub = lax.broadcasted_iota(jnp.int32, (N_EXPERTS, tm), 0)
    oh1 = esub == e1
    oh2 = esub == e2
    c = jnp.where(oh1 | oh2, 1.0, 0.0)
    prefix = jnp.dot(c.astype(BF16), utri_ref[...], preferred_element_type=F32) + cnt_ref[...]
    r1 = jnp.sum(jnp.where(oh1, prefix, 0.0), axis=0, keepdims=True)
    r2 = jnp.sum(jnp.where(oh2, prefix, 0.0), axis=0, keepdims=True)
    cnt_ref[...] = cnt_ref[...] + jnp.sum(c, axis=1, keepdims=True)

    fields = jnp.zeros((ROUTE_FIELDS, tm), F32)
    for idx, val in enumerate((e1.astype(F32), e2.astype(F32), w1, w2, r1, r2)):
        fields = jnp.where(sub == idx, val, fields)
    route_t_ref[...] = fields


def _merge_router(l, pool, attn, x2d, w_out, g_ffn, wr, br, cnt_in, tm):
    t = x2d.shape[0]
    utri = (jnp.arange(tm)[:, None] < jnp.arange(tm)[None, :]).astype(BF16)
    row = lambda i: (i, 0)
    lay = lambda i: (l, 0, 0)
    return pl.pallas_call(
        _merge_router_kernel,
        grid=(t // tm,),
        in_specs=[
            pl.BlockSpec((tm, POOL_W), row),
            pl.BlockSpec((tm, Q_W), row),
            pl.BlockSpec((tm, D_MODEL), row),
            pl.BlockSpec((None, D_MODEL, D_MODEL), lay),
            pl.BlockSpec((None, 1, D_MODEL), lay),
            pl.BlockSpec((None, D_MODEL, LANES), lay),
            pl.BlockSpec((None, 1, LANES), lay),
            pl.BlockSpec((tm, tm), lambda i: (0, 0)),
            pl.BlockSpec((N_EXPERTS, 1), lambda i: (0, 0)),
        ],
        out_specs=[
            pl.BlockSpec((tm, D_MODEL), row),
            pl.BlockSpec((tm, D_MODEL // 2), row),
            pl.BlockSpec((ROUTE_FIELDS, tm), lambda i: (0, i)),
            pl.BlockSpec((N_EXPERTS, 1), lambda i: (0, 0)),
        ],
        out_shape=[
            jax.ShapeDtypeStruct((t, D_MODEL), F32),
            jax.ShapeDtypeStruct((t, D_MODEL // 2), jnp.int32),
            jax.ShapeDtypeStruct((ROUTE_FIELDS, t), F32),
            jax.ShapeDtypeStruct((N_EXPERTS, 1), F32),
        ],
        scratch_shapes=[pltpu.VMEM((tm, D_MODEL), F32), pltpu.VMEM((tm, LANES), F32)],
        compiler_params=pltpu.CompilerParams(
            dimension_semantics=("arbitrary",), vmem_limit_bytes=VMEM_LIMIT),
        name="merge_router",
    )(pool, attn, x2d, w_out, g_ffn, wr, br, utri, cnt_in)


def _moe_kernel(be_ref, rv_ref, nx_ref, sl_ref, xd_ref, wg_hbm, wu_hbm, wd_hbm, yd_ref,
                wg_f, wu_f, wd_f, wg_s, wu_s, wd_s, sem, *, layer):
    step = pl.program_id(0)

    def weight_copies(e, s):
        return [pltpu.make_async_copy(w_hbm.at[layer, e], w_f.at[s], sem.at[s, n])
                for n, (w_hbm, w_f) in enumerate(((wg_hbm, wg_f), (wu_hbm, wu_f), (wd_hbm, wd_f)))]

    @pl.when(step == 0)
    def _():
        for c in weight_copies(be_ref[0], sl_ref[0]):
            c.start()

    def enter_expert(i):
        expert, slot = be_ref[i], sl_ref[i]

        @pl.when((i == 0) | (expert != be_ref[jnp.maximum(i - 1, 0)]))
        def _():
            for c in weight_copies(expert, slot):
                c.wait()

            @pl.when(nx_ref[i] >= 0)
            def _():
                for c in weight_copies(nx_ref[i], 1 - slot):
                    c.start(priority=1)

            wg_s[...] = wg_f[slot].astype(BF16)
            wu_s[...] = wu_f[slot].astype(BF16)
            wd_s[...] = wd_f[slot].astype(BF16)

    def experts_on(row0, n_rows, rows_valid):
        rows = pl.ds(row0, n_rows)
        row = lax.broadcasted_iota(jnp.int32, (n_rows, D_MODEL // 2), 0)
        x = _unpack_bf16_pairs(jnp.where(row < rows_valid, xd_ref[rows, :], 0)).astype(BF16)
        gate = jnp.dot(x, wg_s[...], preferred_element_type=F32)
        up = jnp.dot(x, wu_s[...], preferred_element_type=F32)
        act = (gate * jax.nn.sigmoid(gate) * up).astype(BF16)
        y = jnp.dot(act, wd_s[...], preferred_element_type=F32)
        yd_ref[rows, :] = _pack_bf16_pairs(y.astype(BF16))

    def single_block(i, row0):
        enter_expert(i)

        @pl.when(rv_ref[i] > 0)
        def _():
            experts_on(row0, MOE_BM, rv_ref[i])

        @pl.when(rv_ref[i] <= 0)
        def _():
            yd_ref[pl.ds(row0, MOE_BM), :] = jnp.zeros((MOE_BM, D_MODEL // 2), jnp.int32)

    for pair in range(MOE_STEP_BLOCKS // 2):
        ia = step * MOE_STEP_BLOCKS + 2 * pair
        ib = ia + 1
        row0 = 2 * pair * MOE_BM
        same = (be_ref[ib] == be_ref[ia]) & (rv_ref[ib] > 0)

        @pl.when(same)
        def _():
            enter_expert(ia)
            experts_on(row0, 2 * MOE_BM, MOE_BM + rv_ref[ib])

        @pl.when(jnp.logical_not(same))
        def _():
            single_block(ia, row0)
            single_block(ib, row0 + MOE_BM)


def _moe_experts(l, block_e, rows_valid, next_e, slot, xd, w_gate, w_up, w_down):
    n_blocks = xd.shape[0] // MOE_BM
    step_rows = MOE_STEP_BLOCKS * MOE_BM
    row = lambda i, *_: (i, 0)
    return pl.pallas_call(
        functools.partial(_moe_kernel, layer=l),
        grid_spec=pltpu.PrefetchScalarGridSpec(
            num_scalar_prefetch=4,
            grid=(n_blocks // MOE_STEP_BLOCKS,),
            in_specs=[
                pl.BlockSpec((step_rows, D_MODEL // 2), row),
                pl.BlockSpec(memory_space=pl.ANY),
                pl.BlockSpec(memory_space=pl.ANY),
                pl.BlockSpec(memory_space=pl.ANY),
            ],
            out_specs=pl.BlockSpec((step_rows, D_MODEL // 2), row),
            scratch_shapes=[
                pltpu.VMEM((2, D_MODEL, EXPERT_FF), F32),
                pltpu.VMEM((2, D_MODEL, EXPERT_FF), F32),
                pltpu.VMEM((2, EXPERT_FF, D_MODEL), F32),
                pltpu.VMEM((D_MODEL, EXPERT_FF), BF16),
                pltpu.VMEM((D_MODEL, EXPERT_FF), BF16),
                pltpu.VMEM((EXPERT_FF, D_MODEL), BF16),
                pltpu.SemaphoreType.DMA((2, 3)),
            ],
        ),
        out_shape=jax.ShapeDtypeStruct((n_blocks * MOE_BM, D_MODEL // 2), jnp.int32),
        compiler_params=pltpu.CompilerParams(
            dimension_semantics=("arbitrary",), vmem_limit_bytes=VMEM_LIMIT),
        name="moe_experts",
    )(block_e, rows_valid, next_e, slot, xd, w_gate, w_up, w_down)


def _sc_worker_id():
    return lax.axis_index("s") * SC_CORES + lax.axis_index("c")


def _sc_dispatch(hp, hs, dest_p, dest_s, n_rows):
    tp, width = hp.shape
    per_w = tp // SC_WORKERS
    n_ch = per_w // DISP_CH
    n_sw = hs.shape[0] // SAMPLE_CH
    mesh = plsc.VectorSubcoreMesh(core_axis_name="c", subcore_axis_name="s")

    @functools.partial(
        pl.kernel, mesh=mesh,
        out_type=jax.ShapeDtypeStruct((n_rows, width), jnp.int32),
        scratch_types=[
            pltpu.VMEM((2, n_ch, DISP_CH), jnp.int32),
            pltpu.VMEM((2, 1, SAMPLE_CH), jnp.int32),
            pltpu.VMEM((2, DISP_CH, width), jnp.int32),
            pltpu.SemaphoreType.DMA((2,)),
            pltpu.SemaphoreType.DMA((2,)),
        ],
        name="sc_dispatch",
    )
    def k(hp_hbm, hs_hbm, dp_hbm, ds_hbm, xd_hbm, idx_v, idxs_v, bufs, rsem, wsem):
        wid = _sc_worker_id()
        base = wid * per_w
        for kk in range(2):
            pltpu.sync_copy(dp_hbm.at[kk, wid], idx_v.at[kk])
        reads = [pltpu.make_async_copy(hp_hbm.at[pl.ds(base + j * DISP_CH, DISP_CH)],
                                       bufs.at[j % 2], rsem.at[j % 2]) for j in range(n_ch)]
        reads[0].start()
        for j in range(n_ch):
            if j + 1 < n_ch:
                reads[j + 1].start()
            reads[j].wait()
            writes = [pltpu.make_async_copy(bufs.at[j % 2], xd_hbm.at[idx_v.at[kk, j]], wsem.at[kk])
                      for kk in range(2)]
            for w in writes:
                w.start()
            for w in writes:
                w.wait()

        @pl.when(wid < n_sw)
        def _():
            rows = bufs.at[0, pl.ds(0, SAMPLE_CH)]
            for kk in range(2):
                pltpu.sync_copy(ds_hbm.at[kk, wid], idxs_v.at[kk])
            pltpu.sync_copy(hs_hbm.at[pl.ds(wid * SAMPLE_CH, SAMPLE_CH)], rows)
            for kk in range(2):
                pltpu.sync_copy(rows, xd_hbm.at[idxs_v.at[kk, 0]])

    return k(hp, hs, dest_p, dest_s)


def _sc_combine_gather(yd, dest_p, dest_s, tp, ts):
    width = yd.shape[1]
    per_w = tp // SC_WORKERS
    n_ch = per_w // COMB_CH
    n_sw = ts // SAMPLE_CH
    mesh = plsc.VectorSubcoreMesh(core_axis_name="c", subcore_axis_name="s")

    @functools.partial(
        pl.kernel, mesh=mesh,
        out_type=jax.ShapeDtypeStruct((2, tp + ts, width), yd.dtype),
        scratch_types=[
            pltpu.VMEM((2, n_ch, COMB_CH), jnp.int32),
            pltpu.VMEM((2, 1, SAMPLE_CH), jnp.int32),
            pltpu.VMEM((2, COMB_CH, width), yd.dtype),
            pltpu.SemaphoreType.DMA((2,)),
            pltpu.SemaphoreType.DMA((2,)),
        ],
        name="sc_combine_gather",
    )
    def k(yd_hbm, dp_hbm, ds_hbm, g_hbm, idx_v, idxs_v, bufs, gsem, wsem):
        wid = _sc_worker_id()
        base = wid * per_w
        for kk in range(2):
            pltpu.sync_copy(dp_hbm.at[kk, wid], idx_v.at[kk])
        items = [(kk, j) for kk in range(2) for j in range(n_ch)]
        gathers = [pltpu.make_async_copy(yd_hbm.at[idx_v.at[kk, j]], bufs.at[n % 2], gsem.at[n % 2])
                   for n, (kk, j) in enumerate(items)]
        gathers[0].start()
        for n, (kk, j) in enumerate(items):
            if n + 1 < len(items):
                gathers[n + 1].start()
            gathers[n].wait()
            w = pltpu.make_async_copy(bufs.at[n % 2], g_hbm.at[kk, pl.ds(base + j * COMB_CH, COMB_CH)],
                                      wsem.at[n % 2])
            w.start()
            w.wait()

        @pl.when(wid < n_sw)
        def _():
            for kk in range(2):
                pltpu.sync_copy(ds_hbm.at[kk, wid], idxs_v.at[kk])
            for kk in range(2):
                rows = bufs.at[kk, pl.ds(0, SAMPLE_CH)]
                pltpu.sync_copy(yd_hbm.at[idxs_v.at[kk, 0]], rows)
                pltpu.sync_copy(rows, g_hbm.at[kk, pl.ds(tp + wid * SAMPLE_CH, SAMPLE_CH)])

    return k(yd, dest_p, dest_s)


def _combine_kernel(x1_ref, g_ref, route_t_ref, x2_ref):
    fields = route_t_ref[...]
    tm = fields.shape[1]
    cols = jnp.transpose(jnp.concatenate([fields, jnp.zeros((LANES - ROUTE_FIELDS, tm), F32)], axis=0))
    w1 = cols[:, 2:3]
    w2 = cols[:, 3:4]
    x2_ref[...] = x1_ref[...] + _unpack_bf16_pairs(g_ref[0]) * w1 + _unpack_bf16_pairs(g_ref[1]) * w2


def _combine(x1, g, route_t, row0, tm):
    t = x1.shape[0]
    blk0 = row0 // tm
    row = lambda i: (i, 0)
    return pl.pallas_call(
        _combine_kernel,
        grid=(t // tm,),
        in_specs=[
            pl.BlockSpec((tm, D_MODEL), row),
            pl.BlockSpec((2, tm, D_MODEL // 2), lambda i: (0, blk0 + i, 0)),
            pl.BlockSpec((ROUTE_FIELDS, tm), lambda i: (0, i)),
        ],
        out_specs=pl.BlockSpec((tm, D_MODEL), row),
        out_shape=jax.ShapeDtypeStruct((t, D_MODEL), F32),
        compiler_params=pltpu.CompilerParams(
            dimension_semantics=("arbitrary",), vmem_limit_bytes=VMEM_LIMIT),
        name="combine",
    )(x1, g, route_t)


def _dest_layout(dest, workers, chunk):
    t = dest.shape[1]
    return dest.reshape(2, workers, t // (workers * chunk), chunk)


def _hier_moe(l, h2p, h2s, route_tp, route_ts, counts, w_gate, w_up, w_down):
    tp, ts = h2p.shape[0], h2s.shape[0]
    n_assign = 2 * (tp + ts)
    n_blocks = -(-n_assign // MOE_BM) + N_EXPERTS
    n_blocks = -(-n_blocks // MOE_STEP_BLOCKS) * MOE_STEP_BLOCKS
    pcounts = (counts + MOE_BM - 1) // MOE_BM * MOE_BM
    pends = jnp.cumsum(pcounts)
    poffsets = pends - pcounts
    starts = jnp.arange(n_blocks, dtype=jnp.int32) * MOE_BM
    block_e = jnp.minimum(jnp.sum((pends[None, :] <= starts[:, None]).astype(jnp.int32), axis=1),
                          N_EXPERTS - 1)
    experts = jnp.arange(N_EXPERTS, dtype=jnp.int32)

    def lookup(table, idx):
        return jnp.sum(jnp.where(idx[..., None] == experts, table, 0), axis=-1)

    rows_valid = jnp.clip(lookup(poffsets + counts, block_e) - starts, 0, MOE_BM).astype(jnp.int32)
    used = counts > 0
    last_e = jnp.max(jnp.where(used, jnp.arange(N_EXPERTS, dtype=jnp.int32), 0))
    block_e = jnp.where(rows_valid > 0, block_e, last_e).astype(jnp.int32)
    later = used[None, :] & (jnp.arange(N_EXPERTS)[None, :] > jnp.arange(N_EXPERTS)[:, None])
    next_used = jnp.min(jnp.where(later, jnp.arange(N_EXPERTS, dtype=jnp.int32)[None, :], N_EXPERTS), axis=1)
    next_used = jnp.where(next_used >= N_EXPERTS, -1, next_used).astype(jnp.int32)
    slot_of = ((jnp.cumsum(used.astype(jnp.int32)) - 1) & 1).astype(jnp.int32)
    next_e = lookup(next_used, block_e)
    slot = lookup(slot_of, block_e)

    def dest_of(route_t):
        return lookup(poffsets, route_t[0:2].astype(jnp.int32)) + route_t[4:6].astype(jnp.int32)

    dest_p, dest_s = dest_of(route_tp), dest_of(route_ts)
    n_sw = ts // SAMPLE_CH
    xd = _sc_dispatch(h2p, h2s, _dest_layout(dest_p, SC_WORKERS, DISP_CH),
                      _dest_layout(dest_s, n_sw, SAMPLE_CH), n_blocks * MOE_BM)
    yd = _moe_experts(l, block_e, rows_valid, next_e, slot, xd, w_gate, w_up, w_down)
    return _sc_combine_gather(yd, _dest_layout(dest_p, SC_WORKERS, COMB_CH),
                              _dest_layout(dest_s, n_sw, SAMPLE_CH), tp, ts)


def kernel(x_prompt, x_sample, state_pool, cache_k_win, cache_v_win, norm_attn_g, w_in, pool_w, pool_scale, q_norm_g, k_norm_g, attn_sinks, w_out, norm_ffn_g, router_group_w, router_group_b, router_expert_w, router_expert_b, w_gate, w_up, w_down):
    n_p, t_p, d = x_prompt.shape
    n_s, t_s, _ = x_sample.shape
    depth = w_in.shape[0]
    lw_s = cache_k_win.shape[2]
    assert t_s == 1 and lw_s == WINDOW and d == D_MODEL
    assert t_p % TM_PROJ == 0 and t_p >= WINDOW
    past_len = 16384

    seg = jnp.arange(256) // HEAD_DIM
    bd = jnp.where(seg[:, None] == seg[None, :], 1.0 / HEAD_DIM, 0.0).astype(BF16)
    slopes = jnp.exp2(-8.0 * jnp.arange(1, N_HEADS + 1, dtype=F32) / N_HEADS)
    bias_p = _prompt_bias_t()
    dist_s = (WINDOW - 1) - jnp.arange(WINDOW, dtype=F32)
    bias_s = -slopes[:, None] * dist_s[None, :]

    wp = jnp.zeros((depth, 2, 256, 256), F32)
    for p in range(2):
        wp = wp.at[:, p, :POOL_GC, :POOL_GC].set(pool_w[:, 2 * p])
        wp = wp.at[:, p, POOL_GC:, POOL_GC:].set(pool_w[:, 2 * p + 1])
    wr = jnp.zeros((depth, D_MODEL, LANES), F32)
    wr = wr.at[:, :, :N_EXPERTS].set(router_expert_w)
    wr = wr.at[:, :, GROUP_LANE0:GROUP_LANE0 + N_EXPERT_GROUPS].set(router_group_w)
    br = jnp.zeros((depth, 1, LANES), F32)
    br = br.at[:, 0, :N_EXPERTS].set(router_expert_b)
    br = br.at[:, 0, GROUP_LANE0:GROUP_LANE0 + N_EXPERT_GROUPS].set(router_group_b)
    lp = dict(
        w_in=w_in.astype(BF16),
        w_out=w_out.astype(BF16),
        g_attn=norm_attn_g.reshape(depth, 1, D_MODEL),
        g_ffn=norm_ffn_g.reshape(depth, 1, D_MODEL),
        qg=(jnp.tile(q_norm_g, (1, N_HEADS)) * ATTN_SCALE).reshape(depth, 1, Q_W),
        kg=jnp.tile(k_norm_g, (1, N_KV_HEADS)).reshape(depth, 1, KV_W),
        wp=wp.astype(BF16),
        ps=pool_scale.reshape(depth, 1, POOL_W),
        wr=wr.astype(BF16),
        br=br,
        state=state_pool,
        ck=cache_k_win.reshape(depth, n_s, lw_s, KV_W),
        cv=cache_v_win.reshape(depth, n_s, lw_s, KV_W),
    )

    xp = x_prompt.reshape(n_p * t_p, D_MODEL)
    xs = x_sample.reshape(n_s, D_MODEL)
    lw_p = min(WINDOW, t_p)
    pool_p, kp_new, vp_new, pool_s, ks_new, vs_new = [], [], [], [], [], []
    zero_cnt = jnp.zeros((N_EXPERTS, 1), F32)
    pending = None
    for l in range(depth):
        sinks = attn_sinks[l]
        outs = _proj_pool_prompt(
            l, xp if pending is None else pending, n_p, t_p,
            lp["g_attn"], lp["w_in"], lp["qg"], lp["kg"], bd, lp["wp"], lp["ps"])
        if pending is not None:
            xp, outs = outs[0], outs[1:]
        pool_o, q, k, vt, utail, ktail, vtail = outs
        attn_o = _attn_prompt(q, k, vt, bias_p, sinks, n_p, t_p)
        x1p, h2p, route_tp, cnt_p = _merge_router(
            l, pool_o, attn_o, xp, lp["w_out"], lp["g_ffn"], lp["wr"], lp["br"], zero_cnt, TM_PROJ)
        pool_p.append(utail[:, 16 - POOL_STATE:, :])
        kp_new.append(ktail)
        vp_new.append(vtail)
        pool_so, attn_so, pst_s, kc_s, vc_s = _sample_mixer(
            l, xs, lp["g_attn"], lp["w_in"], lp["qg"], lp["kg"], bd, lp["wp"], lp["ps"],
            lp["state"], lp["ck"], lp["cv"], sinks.reshape(N_HEADS, 1), bias_s, past_len)
        x1s, h2s, route_ts, cnt_all = _merge_router(
            l, pool_so, attn_so, xs, lp["w_out"], lp["g_ffn"], lp["wr"], lp["br"], cnt_p, n_s)
        pool_s.append(pst_s)
        ks_new.append(kc_s)
        vs_new.append(vc_s)
        counts = cnt_all[:, 0].astype(jnp.int32)
        g = _hier_moe(l, h2p, h2s, route_tp, route_ts, counts, w_gate, w_up, w_down)
        xs = _combine(x1s, g, route_ts, n_p * t_p, n_s)
        pending = (x1p, g, route_tp)
    xp = _combine(*pending, 0, TM_PROJ)
    return (xp.reshape(n_p, t_p, D_MODEL), xs.reshape(n_s, t_s, D_MODEL),
            jnp.stack(pool_p),
            jnp.stack(kp_new).reshape(depth, n_p, lw_p, N_KV_HEADS, HEAD_DIM),
            jnp.stack(vp_new).reshape(depth, n_p, lw_p, N_KV_HEADS, HEAD_DIM),
            jnp.stack(pool_s),
            jnp.stack(ks_new).reshape(depth, n_s, lw_s, N_KV_HEADS, HEAD_DIM),
            jnp.stack(vs_new).reshape(depth, n_s, lw_s, N_KV_HEADS, HEAD_DIM))
```

```python
import functools

import jax
import jax.numpy as jnp
from jax import lax
from jax.experimental import pallas as pl
from jax.experimental.pallas import tpu as pltpu
from jax.experimental.pallas import tpu_sc as plsc

D_MODEL = 1024
POOL_W = 512
POOL_WINDOWS = (2, 4, 8, 16)
POOL_GC = 128
POOL_STATE = 15
HEAD_DIM = 64
N_HEADS = 8
N_KV_HEADS = 2
GQA_GROUP = 4
Q_W = 512
KV_W = 128
D_IN = POOL_W + Q_W + 2 * KV_W
WINDOW = 128
ATTN_SCALE = HEAD_DIM ** -0.5
N_EXPERT_GROUPS = 4
EXPERTS_PER_GROUP = 8
N_EXPERTS = 32
EXPERT_FF = 512
EPS = 1e-6

LANES = 128
HALO = 32
TM_PROJ = 512
MERGE_CHUNKS = 2
ATTN_QB = 8
MOE_BM = 256
MOE_STEP_BLOCKS = 4
WEIGHT_SLOTS = 3
GROUP_LANE0 = 32
ROUTE_FIELDS = 8
SC_CORES = 2
SC_SUBCORES = 16
SC_WORKERS = SC_CORES * SC_SUBCORES
DISP_CH = 64
COMB_CH = 64
SAMPLE_CH = 32
VMEM_LIMIT = 48 * 1024 * 1024

BF16 = jnp.bfloat16
F32 = jnp.float32


def _pack_bf16_pairs(h):
    w = h.shape[1] // 2
    hi = lax.bitcast_convert_type(h[:, :w].astype(F32), jnp.uint32)
    lo = lax.bitcast_convert_type(h[:, w:].astype(F32), jnp.uint32)
    return lax.bitcast_convert_type(hi | (lo >> 16), jnp.int32)


def _unpack_bf16_pairs(words):
    u = lax.bitcast_convert_type(words, jnp.uint32)
    hi = lax.bitcast_convert_type(u & jnp.uint32(0xFFFF0000), F32)
    lo = lax.bitcast_convert_type(u << 16, F32)
    return jnp.concatenate([hi, lo], axis=-1)


def _segment_mean_sq(a, bd):
    w = a.shape[1]
    return jnp.dot((a * a).astype(BF16), bd[:w, :w], preferred_element_type=F32)


def _rms_bf16(x, g):
    ms = jnp.mean(x * x, axis=-1, keepdims=True)
    return (x * lax.rsqrt(ms + EPS) * g).astype(BF16)


def _qk_norm(q, k, qg, kg, bd):
    qn = []
    for c in range(Q_W // 256):
        qc = q[:, c * 256:(c + 1) * 256]
        qn.append(qc * lax.rsqrt(_segment_mean_sq(qc, bd) + EPS))
    qn = jnp.concatenate(qn, axis=-1) * qg
    kn = k * lax.rsqrt(_segment_mean_sq(k, bd) + EPS) * kg
    return qn, kn


def _project(x, g, w_in, qg, kg, bd):
    z = jnp.dot(_rms_bf16(x, g), w_in, preferred_element_type=F32)
    u = z[:, :POOL_W]
    q = z[:, POOL_W:POOL_W + Q_W]
    k = z[:, POOL_W + Q_W:POOL_W + Q_W + KV_W]
    v = z[:, POOL_W + Q_W + KV_W:]
    qn, kn = _qk_norm(q, k, qg, kg, bd)
    return u, qn, kn, v


def _pool_project(d_groups, wp_ref, ps):
    outs = []
    for p in range(2):
        dp = jnp.concatenate([d_groups[2 * p], d_groups[2 * p + 1]], axis=-1).astype(BF16)
        y = jnp.dot(dp, wp_ref[p], preferred_element_type=F32)
        outs.append(y * ps[:, p * 256:(p + 1) * 256])
    return jnp.concatenate(outs, axis=-1)


def _proj_pool_kernel(x_ref, g_ref, win_ref, qg_ref, kg_ref, bd_ref, wp_ref, ps_ref,
                      pool_ref, q_ref, k_ref, vt_ref, utail_ref, ktail_ref, vtail_ref,
                      ext_ref, sa_ref, sb_ref, zq_ref, *, tm, n_j):
    j = pl.program_id(1)

    @pl.when(j == 0)
    def _():
        ext_ref[0:HALO, :] = jnp.zeros((HALO, POOL_W), F32)

    r = tm + HALO
    h = _rms_bf16(x_ref[...], g_ref[...])
    ext_ref[HALO:r, :] = jnp.dot(h, win_ref[:, 0:POOL_W], preferred_element_type=F32)
    zq_ref[...] = jnp.dot(h, win_ref[:, POOL_W:], preferred_element_type=F32)
    u = ext_ref[HALO:r, :]
    sa_ref[8:r, :] = ext_ref[8:r, :] + ext_ref[7:r - 1, :]
    sb_ref[16:r, 128:] = sa_ref[16:r, 128:] + sa_ref[14:r - 2, 128:]
    sa_ref[24:r, 256:] = sb_ref[24:r, 256:] + sb_ref[20:r - 4, 256:]
    sb_ref[32:r, 384:] = sa_ref[32:r, 384:] + sa_ref[24:r - 8, 384:]
    pos1 = j * tm + lax.broadcasted_iota(jnp.int32, (tm, POOL_GC), 0) + 1
    sums = (sa_ref, sb_ref, sa_ref, sb_ref)
    d_groups = []
    for gi, w in enumerate(POOL_WINDOWS):
        sl = slice(gi * POOL_GC, (gi + 1) * POOL_GC)
        cnt = jnp.minimum(pos1, w).astype(F32)
        d_groups.append(sums[gi][HALO:r, sl] / cnt - u[:, sl])
    pool_ref[...] = _pool_project(d_groups, wp_ref, ps_ref[...]).astype(BF16)
    ext_ref[16:HALO, :] = ext_ref[tm + 16:r, :]

    qn, kn = _qk_norm(zq_ref[:, 0:Q_W], zq_ref[:, Q_W:Q_W + KV_W], qg_ref[...], kg_ref[...], bd_ref[...])
    v = zq_ref[:, Q_W + KV_W:]
    q_ref[...] = qn.astype(BF16)
    k_ref[...] = kn.astype(BF16)
    vt_ref[...] = jnp.transpose(v).astype(BF16)

    @pl.when(j == n_j - 1)
    def _():
        utail_ref[...] = u[tm - 16:, :]
        ktail_ref[...] = kn[tm - WINDOW:, :]
        vtail_ref[...] = v[tm - WINDOW:, :]


def _proj_pool_combine_kernel(x1_ref, gath_ref, route_ref, *rest, tm, n_j):
    x2_ref = rest[7]
    _combine_kernel(x1_ref, gath_ref, route_ref, x2_ref)
    _proj_pool_kernel(x2_ref, *rest[:7], *rest[8:], tm=tm, n_j=n_j)


def _proj_pool_prompt(l, x_in, n_seq, seq, g_attn, w_in, qg, kg, bd, wp, ps):
    tm = TM_PROJ
    n_j = seq // tm
    t = n_seq * seq
    row = lambda b, j: (b * n_j + j, 0)
    lay = lambda b, j: (l, 0, 0)
    fused = isinstance(x_in, tuple)
    if fused:
        kern = _proj_pool_combine_kernel
        x_args = list(x_in)
        x_specs = [pl.BlockSpec((tm, D_MODEL), row),
                   pl.BlockSpec((2, tm, D_MODEL // 2), lambda b, j: (0, b * n_j + j, 0)),
                   pl.BlockSpec((ROUTE_FIELDS, tm), lambda b, j: (0, b * n_j + j))]
        x_out_specs = [pl.BlockSpec((tm, D_MODEL), row)]
        x_out_shape = [jax.ShapeDtypeStruct((t, D_MODEL), F32)]
    else:
        kern = _proj_pool_kernel
        x_args = [x_in]
        x_specs = [pl.BlockSpec((tm, D_MODEL), row)]
        x_out_specs, x_out_shape = [], []
    return pl.pallas_call(
        functools.partial(kern, tm=tm, n_j=n_j),
        grid=(n_seq, n_j),
        in_specs=x_specs + [
            pl.BlockSpec((None, 1, D_MODEL), lay),
            pl.BlockSpec((None, D_MODEL, D_IN), lay),
            pl.BlockSpec((None, 1, Q_W), lay),
            pl.BlockSpec((None, 1, KV_W), lay),
            pl.BlockSpec((256, 256), lambda b, j: (0, 0)),
            pl.BlockSpec((None, 2, 256, 256), lambda b, j: (l, 0, 0, 0)),
            pl.BlockSpec((None, 1, POOL_W), lay),
        ],
        out_specs=x_out_specs + [
            pl.BlockSpec((tm, POOL_W), row),
            pl.BlockSpec((tm, Q_W), row),
            pl.BlockSpec((tm, KV_W), row),
            pl.BlockSpec((KV_W, tm), lambda b, j: (0, b * n_j + j)),
            pl.BlockSpec((None, 16, POOL_W), lambda b, j: (b, 0, 0)),
            pl.BlockSpec((None, WINDOW, KV_W), lambda b, j: (b, 0, 0)),
            pl.BlockSpec((None, WINDOW, KV_W), lambda b, j: (b, 0, 0)),
        ],
        out_shape=x_out_shape + [
            jax.ShapeDtypeStruct((t, POOL_W), BF16),
            jax.ShapeDtypeStruct((t, Q_W), BF16),
            jax.ShapeDtypeStruct((t, KV_W), BF16),
            jax.ShapeDtypeStruct((KV_W, t), BF16),
            jax.ShapeDtypeStruct((n_seq, 16, POOL_W), F32),
            jax.ShapeDtypeStruct((n_seq, WINDOW, KV_W), F32),
            jax.ShapeDtypeStruct((n_seq, WINDOW, KV_W), F32),
        ],
        scratch_shapes=[pltpu.VMEM((tm + HALO, POOL_W), F32)] * 3 + [pltpu.VMEM((tm, Q_W + 2 * KV_W), F32)],
        compiler_params=pltpu.CompilerParams(
            dimension_semantics=("arbitrary", "arbitrary"), vmem_limit_bytes=VMEM_LIMIT),
        name="proj_pool_prompt",
    )(*x_args, g_attn, w_in, qg, kg, bd, wp, ps)


def _attn_kernel(sink_ref, q_ref, kp_ref, kc_ref, vtp_ref, vtc_ref, bias_ref, o_ref, s_ref):
    j = pl.program_id(1)
    kk_all = jnp.concatenate([kp_ref[...], kc_ref[...]], axis=0)
    vt_all = jnp.concatenate([vtp_ref[...], vtc_ref[...]], axis=1)
    from_prev = (lax.broadcasted_iota(jnp.int32, (WINDOW, WINDOW), 0)
                 > lax.broadcasted_iota(jnp.int32, (WINDOW, WINDOW), 1))
    units = [(blk, kv) for blk in range(ATTN_QB) for kv in range(N_KV_HEADS)]

    def scores(n):
        blk, kv = units[n]
        q = q_ref[blk * WINDOW:(blk + 1) * WINDOW, :]
        kk = kk_all[blk * WINDOW:(blk + 2) * WINDOW, kv * HEAD_DIM:(kv + 1) * HEAD_DIM]
        heads = range(kv * GQA_GROUP, (kv + 1) * GQA_GROUP)
        q_rows = jnp.concatenate([q[:, h * HEAD_DIM:(h + 1) * HEAD_DIM] for h in heads], axis=0)
        s_ref[n % 2] = lax.dot_general(kk, q_rows, (((1,), (1,)), ((), ())), preferred_element_type=F32)

    scores(0)
    outs = []
    for n, (blk, kv) in enumerate(units):
        if n + 1 < len(units):
            scores(n + 1)
        vt_kv = vt_all[kv * HEAD_DIM:(kv + 1) * HEAD_DIM, blk * WINDOW:(blk + 2) * WINDOW]
        variant = jnp.minimum(j, 1) if blk == 0 else 1
        for g in range(GQA_GROUP):
            h = kv * GQA_GROUP + g
            s = jnp.where(from_prev, s_ref[n % 2, 0:WINDOW, g * WINDOW:(g + 1) * WINDOW],
                          s_ref[n % 2, WINDOW:, g * WINDOW:(g + 1) * WINDOW]) + bias_ref[variant, h]
            sink = sink_ref[h]
            m = jnp.maximum(jnp.max(s, axis=0, keepdims=True), sink)
            p = jnp.exp(s - m)
            denom = jnp.sum(p, axis=0, keepdims=True) + jnp.exp(sink - m)
            p_keys = jnp.concatenate([jnp.where(from_prev, p, 0.0), jnp.where(from_prev, 0.0, p)], axis=0)
            o_t = jnp.dot(vt_kv, p_keys.astype(BF16), preferred_element_type=F32)
            outs.append(o_t / denom)
        if kv == N_KV_HEADS - 1:
            o_ref[blk * WINDOW:(blk + 1) * WINDOW, :] = jnp.transpose(jnp.concatenate(outs, axis=0)).astype(BF16)
            outs = []


def _attn_prompt(q, k, vt, bias_t, sinks, n_seq, seq):
    tq = ATTN_QB * WINDOW
    nj = seq // tq
    t = n_seq * seq
    cur = lambda b, j: (b * nj + j, 0)
    prev = lambda b, j: (jnp.maximum((b * nj + j) * ATTN_QB - 1, 0), 0)
    cur_t = lambda b, j: (0, b * nj + j)
    prev_t = lambda b, j: (0, jnp.maximum((b * nj + j) * ATTN_QB - 1, 0))
    return pl.pallas_call(
        _attn_kernel,
        grid=(n_seq, nj),
        in_specs=[
            pl.BlockSpec(memory_space=pltpu.SMEM),
            pl.BlockSpec((tq, Q_W), cur),
            pl.BlockSpec((WINDOW, KV_W), prev),
            pl.BlockSpec((tq, KV_W), cur),
            pl.BlockSpec((KV_W, WINDOW), prev_t),
            pl.BlockSpec((KV_W, tq), cur_t),
            pl.BlockSpec((2, N_HEADS, WINDOW, WINDOW), lambda b, j: (0, 0, 0, 0)),
        ],
        out_specs=pl.BlockSpec((tq, Q_W), cur),
        out_shape=jax.ShapeDtypeStruct((t, Q_W), BF16),
        scratch_shapes=[pltpu.VMEM((2, 2 * WINDOW, GQA_GROUP * WINDOW), F32)],
        compiler_params=pltpu.CompilerParams(
            dimension_semantics=("arbitrary", "arbitrary"), vmem_limit_bytes=VMEM_LIMIT),
        name="attn_prompt",
    )(sinks, q, k, k, vt, vt, bias_t)


def _prompt_bias_t():
    r = jnp.arange(WINDOW, dtype=jnp.int32)[None, :]
    c = jnp.arange(WINDOW, dtype=jnp.int32)[:, None]
    from_prev = c > r
    dist = r - c + jnp.where(from_prev, WINDOW, 0)
    slopes = jnp.exp2(-8.0 * jnp.arange(1, N_HEADS + 1, dtype=F32) / N_HEADS)
    later = -slopes[:, None, None] * dist.astype(F32)[None]
    first = jnp.where(from_prev[None], -jnp.inf, later)
    return jnp.stack([first, later])


def _sample_kernel(x_ref, g_ref, win_ref, qg_ref, kg_ref, bd_ref, wp_ref, ps_ref,
                   st_ref, ck_ref, cv_ref, sink_ref, bias_ref, perm_ref,
                   pool_ref, attn_ref, pst_ref, kc_ref, vc_ref, *, ns, pos0):
    u, qn, kn, v = _project(x_ref[...], g_ref[...], win_ref[...], qg_ref[...], kg_ref[...], bd_ref[...])
    pst_ref[:, 0:POOL_STATE - 1, :] = st_ref[:, 1:POOL_STATE, :]
    kc_ref[:, 0:WINDOW - 1, :] = ck_ref[:, 1:WINDOW, :]
    vc_ref[:, 0:WINDOW - 1, :] = cv_ref[:, 1:WINDOW, :]
    for n in range(ns):
        pst_ref[n, POOL_STATE - 1:POOL_STATE, :] = u[n:n + 1, :]
        kc_ref[n, WINDOW - 1:WINDOW, :] = kn[n:n + 1, :]
        vc_ref[n, WINDOW - 1:WINDOW, :] = v[n:n + 1, :]

    d_groups = []
    for gi, w in enumerate(POOL_WINDOWS):
        lo = gi * POOL_GC
        acc = u[:, lo:lo + POOL_GC]
        for back in range(1, w):
            acc = acc + st_ref[:, POOL_STATE - back, lo:lo + POOL_GC]
        d_groups.append(acc / float(min(pos0 + 1, w)) - u[:, lo:lo + POOL_GC])
    pool_ref[...] = _pool_project(d_groups, wp_ref, ps_ref[...]).astype(BF16)

    zeros = jnp.zeros((ns, HEAD_DIM), F32)
    stacked = []
    for h in range(N_HEADS):
        piece = qn[:, h * HEAD_DIM:(h + 1) * HEAD_DIM]
        pair = [piece, zeros] if h < GQA_GROUP else [zeros, piece]
        stacked.append(jnp.concatenate(pair, axis=-1))
    q_hn = jnp.concatenate(stacked, axis=0).astype(BF16)
    q_nh = jnp.dot(perm_ref[0], q_hn, preferred_element_type=F32).astype(BF16)

    keys = kc_ref[...].reshape(ns * WINDOW, KV_W).astype(BF16)
    vals = vc_ref[...].reshape(ns * WINDOW, KV_W).astype(BF16)
    s_all = lax.dot_general(q_nh, keys, (((1,), (1,)), ((), ())), preferred_element_type=F32)
    sink = sink_ref[...]
    bias = bias_ref[...]
    zero_blk = jnp.zeros((N_HEADS, WINDOW), F32)
    p_rows = []
    for n in range(ns):
        s = s_all[n * N_HEADS:(n + 1) * N_HEADS, n * WINDOW:(n + 1) * WINDOW] + bias
        m = jnp.maximum(jnp.max(s, axis=-1, keepdims=True), sink)
        p = jnp.exp(s - m)
        denom = jnp.sum(p, axis=-1, keepdims=True) + jnp.exp(sink - m)
        p_rows.append(jnp.concatenate([zero_blk] * n + [p / denom] + [zero_blk] * (ns - 1 - n), axis=-1))
    p_blockdiag = jnp.concatenate(p_rows, axis=0).astype(BF16)
    o_nh = jnp.dot(p_blockdiag, vals, preferred_element_type=F32).astype(BF16)
    o_hn = jnp.dot(perm_ref[1], o_nh, preferred_element_type=F32)
    pieces = []
    for h in range(N_HEADS):
        kv = h // GQA_GROUP
        pieces.append(o_hn[h * ns:(h + 1) * ns, kv * HEAD_DIM:(kv + 1) * HEAD_DIM])
    attn_ref[...] = jnp.concatenate(pieces, axis=-1).astype(BF16)


def _sample_mixer(l, xs, g_attn, w_in, qg, kg, bd, wp, ps, state2d, ck, cv, sink8, bias_s, pos0):
    n = xs.shape[0]
    ns = 32
    row = lambda i: (i, 0)
    lay = lambda i: (l, 0, 0)
    src = jnp.arange(ns * N_HEADS)
    perm = (((src % N_HEADS) * ns + src // N_HEADS)[:, None] == src[None, :]).astype(BF16)
    perms = jnp.stack([perm, perm.T])
    return pl.pallas_call(
        functools.partial(_sample_kernel, ns=ns, pos0=pos0),
        grid=(n // ns,),
        in_specs=[
            pl.BlockSpec((ns, D_MODEL), row),
            pl.BlockSpec((None, 1, D_MODEL), lay),
            pl.BlockSpec((None, D_MODEL, D_IN), lay),
            pl.BlockSpec((None, 1, Q_W), lay),
            pl.BlockSpec((None, 1, KV_W), lay),
            pl.BlockSpec((256, 256), lambda i: (0, 0)),
            pl.BlockSpec((None, 2, 256, 256), lambda i: (l, 0, 0, 0)),
            pl.BlockSpec((None, 1, POOL_W), lay),
            pl.BlockSpec((None, ns, POOL_STATE, POOL_W), lambda i: (l, i, 0, 0)),
            pl.BlockSpec((None, ns, WINDOW, KV_W), lambda i: (l, i, 0, 0)),
            pl.BlockSpec((None, ns, WINDOW, KV_W), lambda i: (l, i, 0, 0)),
            pl.BlockSpec((N_HEADS, 1), lambda i: (0, 0)),
            pl.BlockSpec((N_HEADS, WINDOW), lambda i: (0, 0)),
            pl.BlockSpec((2, ns * N_HEADS, ns * N_HEADS), lambda i: (0, 0, 0)),
        ],
        out_specs=[
            pl.BlockSpec((ns, POOL_W), row),
            pl.BlockSpec((ns, Q_W), row),
            pl.BlockSpec((ns, POOL_STATE, POOL_W), lambda i: (i, 0, 0)),
            pl.BlockSpec((ns, WINDOW, KV_W), lambda i: (i, 0, 0)),
            pl.BlockSpec((ns, WINDOW, KV_W), lambda i: (i, 0, 0)),
        ],
        out_shape=[
            jax.ShapeDtypeStruct((n, POOL_W), BF16),
            jax.ShapeDtypeStruct((n, Q_W), BF16),
            jax.ShapeDtypeStruct((n, POOL_STATE, POOL_W), F32),
            jax.ShapeDtypeStruct((n, WINDOW, KV_W), F32),
            jax.ShapeDtypeStruct((n, WINDOW, KV_W), F32),
        ],
        compiler_params=pltpu.CompilerParams(
            dimension_semantics=("arbitrary",), vmem_limit_bytes=VMEM_LIMIT),
        name="sample_mixer",
    )(xs, g_attn, w_in, qg, kg, bd, wp, ps, state2d, ck, cv, sink8, bias_s, perms)


def _merge_router_kernel(pool_ref, attn_ref, x_ref, wout_ref, g_ref, wr_ref, br_ref, utri_ref, cin_ref,
                         x1_ref, h2_ref, route_t_ref, cnt_ref, y_ref, lg_ref):
    i = pl.program_id(0)

    @pl.when(i == 0)
    def _():
        cnt_ref[...] = cin_ref[...]

    tm = x_ref.shape[0]
    rc = tm // MERGE_CHUNKS
    chunks = [slice(ci * rc, (ci + 1) * rc) for ci in range(MERGE_CHUNKS)]
    for rows in chunks:
        y_ref[rows, :] = (jnp.dot(pool_ref[rows, :], wout_ref[0:POOL_W, :], preferred_element_type=F32)
                          + jnp.dot(attn_ref[rows, :], wout_ref[POOL_W:, :], preferred_element_type=F32))
    for rows in chunks:
        x1 = x_ref[rows, :] + y_ref[rows, :]
        x1_ref[rows, :] = x1
        h2 = _rms_bf16(x1, g_ref[...])
        h2_ref[rows, :] = _pack_bf16_pairs(h2)
        lg_ref[rows, :] = jnp.dot(h2, wr_ref[...], preferred_element_type=F32) + br_ref[...]
    logits = lg_ref[...]

    lt = jnp.transpose(logits)
    sub = lax.broadcasted_iota(jnp.int32, (EXPERTS_PER_GROUP, tm), 0)
    neg = -jnp.inf
    big = jnp.int32(EXPERTS_PER_GROUP)
    gl = jnp.where(sub < N_EXPERT_GROUPS, lt[GROUP_LANE0:GROUP_LANE0 + EXPERTS_PER_GROUP, :], neg)
    gmax = jnp.max(gl, axis=0, keepdims=True)
    grp = jnp.min(jnp.where(gl == gmax, sub, big), axis=0, keepdims=True)
    g_w = 1.0 / jnp.sum(jnp.exp(gl - gmax), axis=0, keepdims=True)
    el = lt[(N_EXPERT_GROUPS - 1) * EXPERTS_PER_GROUP:N_EXPERT_GROUPS * EXPERTS_PER_GROUP, :]
    for gi in range(N_EXPERT_GROUPS - 2, -1, -1):
        el = jnp.where(grp == gi, lt[gi * EXPERTS_PER_GROUP:(gi + 1) * EXPERTS_PER_GROUP, :], el)
    v1 = jnp.max(el, axis=0, keepdims=True)
    i1 = jnp.min(jnp.where(el == v1, sub, big), axis=0, keepdims=True)
    el2 = jnp.where(sub == i1, neg, el)
    v2 = jnp.max(el2, axis=0, keepdims=True)
    i2 = jnp.min(jnp.where(el2 == v2, sub, big), axis=0, keepdims=True)
    e21 = jnp.exp(v2 - v1)
    w1 = g_w / (1.0 + e21)
    w2 = g_w * e21 / (1.0 + e21)
    e1 = grp * EXPERTS_PER_GROUP + i1
    e2 = grp * EXPERTS_PER_GROUP + i2

    esub = lax.broadcasted_iota(jnp.int32, (N_EXPERTS, tm), 0)
    oh1 = esub == e1
    oh2 = esub == e2
    c = jnp.where(oh1 | oh2, 1.0, 0.0)
    prefix = jnp.dot(c.astype(BF16), utri_ref[...], preferred_element_type=F32) + cnt_ref[...]
    r1 = jnp.sum(jnp.where(oh1, prefix, 0.0), axis=0, keepdims=True)
    r2 = jnp.sum(jnp.where(oh2, prefix, 0.0), axis=0, keepdims=True)
    cnt_ref[...] = cnt_ref[...] + jnp.sum(c, axis=1, keepdims=True)

    fields = jnp.zeros((ROUTE_FIELDS, tm), F32)
    for idx, val in enumerate((e1.astype(F32), e2.astype(F32), w1, w2, r1, r2)):
        fields = jnp.where(sub == idx, val, fields)
    route_t_ref[...] = fields


def _merge_router(l, pool, attn, x2d, w_out, g_ffn, wr, br, cnt_in, tm):
    t = x2d.shape[0]
    utri = (jnp.arange(tm)[:, None] < jnp.arange(tm)[None, :]).astype(BF16)
    row = lambda i: (i, 0)
    lay = lambda i: (l, 0, 0)
    return pl.pallas_call(
        _merge_router_kernel,
        grid=(t // tm,),
        in_specs=[
            pl.BlockSpec((tm, POOL_W), row),
            pl.BlockSpec((tm, Q_W), row),
            pl.BlockSpec((tm, D_MODEL), row),
            pl.BlockSpec((None, D_MODEL, D_MODEL), lay),
            pl.BlockSpec((None, 1, D_MODEL), lay),
            pl.BlockSpec((None, D_MODEL, LANES), lay),
            pl.BlockSpec((None, 1, LANES), lay),
            pl.BlockSpec((tm, tm), lambda i: (0, 0)),
            pl.BlockSpec((N_EXPERTS, 1), lambda i: (0, 0)),
        ],
        out_specs=[
            pl.BlockSpec((tm, D_MODEL), row),
            pl.BlockSpec((tm, D_MODEL // 2), row),
            pl.BlockSpec((ROUTE_FIELDS, tm), lambda i: (0, i)),
            pl.BlockSpec((N_EXPERTS, 1), lambda i: (0, 0)),
        ],
        out_shape=[
            jax.ShapeDtypeStruct((t, D_MODEL), F32),
            jax.ShapeDtypeStruct((t, D_MODEL // 2), jnp.int32),
            jax.ShapeDtypeStruct((ROUTE_FIELDS, t), F32),
            jax.ShapeDtypeStruct((N_EXPERTS, 1), F32),
        ],
        scratch_shapes=[pltpu.VMEM((tm, D_MODEL), F32), pltpu.VMEM((tm, LANES), F32)],
        compiler_params=pltpu.CompilerParams(
            dimension_semantics=("arbitrary",), vmem_limit_bytes=VMEM_LIMIT),
        name="merge_router",
    )(pool, attn, x2d, w_out, g_ffn, wr, br, utri, cnt_in)


def _moe_kernel(be_ref, rv_ref, nx_ref, sl_ref, first_ref, xd_ref, wg_hbm, wu_hbm, wd_hbm, yd_ref,
                wg_f, wu_f, wd_f, wg_s, wu_s, wd_s, sem, *, layer):
    step = pl.program_id(0)

    def weight_copies(e, s):
        return [pltpu.make_async_copy(w_hbm.at[layer, e], w_f.at[s], sem.at[s, n])
                for n, (w_hbm, w_f) in enumerate(((wg_hbm, wg_f), (wu_hbm, wu_f), (wd_hbm, wd_f)))]

    @pl.when(step == 0)
    def _():
        for s in range(WEIGHT_SLOTS - 1):
            @pl.when(first_ref[s] >= 0)
            def _():
                for c in weight_copies(first_ref[s], s):
                    c.start()

    def enter_expert(i):
        expert, slot = be_ref[i], sl_ref[i]

        @pl.when((i == 0) | (expert != be_ref[jnp.maximum(i - 1, 0)]))
        def _():
            for c in weight_copies(expert, slot):
                c.wait()

            @pl.when(nx_ref[i] >= 0)
            def _():
                for c in weight_copies(nx_ref[i], lax.rem(slot + WEIGHT_SLOTS - 1, WEIGHT_SLOTS)):
                    c.start(priority=1)

            wg_s[...] = wg_f[slot].astype(BF16)
            wu_s[...] = wu_f[slot].astype(BF16)
            wd_s[...] = wd_f[slot].astype(BF16)

    def experts_on(row0, n_rows, rows_valid):
        rows = pl.ds(row0, n_rows)
        row = lax.broadcasted_iota(jnp.int32, (n_rows, D_MODEL // 2), 0)
        x = _unpack_bf16_pairs(jnp.where(row < rows_valid, xd_ref[rows, :], 0)).astype(BF16)
        gate = jnp.dot(x, wg_s[...], preferred_element_type=F32)
        up = jnp.dot(x, wu_s[...], preferred_element_type=F32)
        act = (gate * jax.nn.sigmoid(gate) * up).astype(BF16)
        y = jnp.dot(act, wd_s[...], preferred_element_type=F32)
        yd_ref[rows, :] = _pack_bf16_pairs(y.astype(BF16))

    def single_block(i, row0):
        enter_expert(i)

        @pl.when(rv_ref[i] > 0)
        def _():
            experts_on(row0, MOE_BM, rv_ref[i])

        @pl.when(rv_ref[i] <= 0)
        def _():
            yd_ref[pl.ds(row0, MOE_BM), :] = jnp.zeros((MOE_BM, D_MODEL // 2), jnp.int32)

    for pair in range(MOE_STEP_BLOCKS // 2):
        ia = step * MOE_STEP_BLOCKS + 2 * pair
        ib = ia + 1
        row0 = 2 * pair * MOE_BM
        same = (be_ref[ib] == be_ref[ia]) & (rv_ref[ib] > 0)

        @pl.when(same)
        def _():
            enter_expert(ia)
            experts_on(row0, 2 * MOE_BM, MOE_BM + rv_ref[ib])

        @pl.when(jnp.logical_not(same))
        def _():
            single_block(ia, row0)
            single_block(ib, row0 + MOE_BM)


def _moe_experts(l, block_e, rows_valid, next_e, slot, first_e, xd, w_gate, w_up, w_down):
    n_blocks = xd.shape[0] // MOE_BM
    step_rows = MOE_STEP_BLOCKS * MOE_BM
    row = lambda i, *_: (i, 0)
    return pl.pallas_call(
        functools.partial(_moe_kernel, layer=l),
        grid_spec=pltpu.PrefetchScalarGridSpec(
            num_scalar_prefetch=5,
            grid=(n_blocks // MOE_STEP_BLOCKS,),
            in_specs=[
                pl.BlockSpec((step_rows, D_MODEL // 2), row),
                pl.BlockSpec(memory_space=pl.ANY),
                pl.BlockSpec(memory_space=pl.ANY),
                pl.BlockSpec(memory_space=pl.ANY),
            ],
            out_specs=pl.BlockSpec((step_rows, D_MODEL // 2), row),
            scratch_shapes=[
                pltpu.VMEM((WEIGHT_SLOTS, D_MODEL, EXPERT_FF), F32),
                pltpu.VMEM((WEIGHT_SLOTS, D_MODEL, EXPERT_FF), F32),
                pltpu.VMEM((WEIGHT_SLOTS, EXPERT_FF, D_MODEL), F32),
                pltpu.VMEM((D_MODEL, EXPERT_FF), BF16),
                pltpu.VMEM((D_MODEL, EXPERT_FF), BF16),
                pltpu.VMEM((EXPERT_FF, D_MODEL), BF16),
                pltpu.SemaphoreType.DMA((WEIGHT_SLOTS, 3)),
            ],
        ),
        out_shape=jax.ShapeDtypeStruct((n_blocks * MOE_BM, D_MODEL // 2), jnp.int32),
        compiler_params=pltpu.CompilerParams(
            dimension_semantics=("arbitrary",), vmem_limit_bytes=VMEM_LIMIT),
        name="moe_experts",
    )(block_e, rows_valid, next_e, slot, first_e, xd, w_gate, w_up, w_down)


def _sc_worker_id():
    return lax.axis_index("s") * SC_CORES + lax.axis_index("c")


def _sc_dispatch(hp, hs, dest_p, dest_s, n_rows):
    tp, width = hp.shape
    per_w = tp // SC_WORKERS
    n_ch = per_w // DISP_CH
    n_sw = hs.shape[0] // SAMPLE_CH
    mesh = plsc.VectorSubcoreMesh(core_axis_name="c", subcore_axis_name="s")

    @functools.partial(
        pl.kernel, mesh=mesh,
        out_type=jax.ShapeDtypeStruct((n_rows, width), jnp.int32),
        scratch_types=[
            pltpu.VMEM((2, n_ch, DISP_CH), jnp.int32),
            pltpu.VMEM((2, 1, SAMPLE_CH), jnp.int32),
            pltpu.VMEM((2, DISP_CH, width), jnp.int32),
            pltpu.SemaphoreType.DMA((2,)),
            pltpu.SemaphoreType.DMA((2,)),
        ],
        name="sc_dispatch",
    )
    def k(hp_hbm, hs_hbm, dp_hbm, ds_hbm, xd_hbm, idx_v, idxs_v, bufs, rsem, wsem):
        wid = _sc_worker_id()
        base = wid * per_w
        for kk in range(2):
            pltpu.sync_copy(dp_hbm.at[kk, wid], idx_v.at[kk])
        reads = [pltpu.make_async_copy(hp_hbm.at[pl.ds(base + j * DISP_CH, DISP_CH)],
                                       bufs.at[j % 2], rsem.at[j % 2]) for j in range(n_ch)]
        reads[0].start()
        for j in range(n_ch):
            if j + 1 < n_ch:
                reads[j + 1].start()
            reads[j].wait()
            writes = [pltpu.make_async_copy(bufs.at[j % 2], xd_hbm.at[idx_v.at[kk, j]], wsem.at[kk])
                      for kk in range(2)]
            for w in writes:
                w.start()
            for w in writes:
                w.wait()

        @pl.when(wid < n_sw)
        def _():
            rows = bufs.at[0, pl.ds(0, SAMPLE_CH)]
            for kk in range(2):
                pltpu.sync_copy(ds_hbm.at[kk, wid], idxs_v.at[kk])
            pltpu.sync_copy(hs_hbm.at[pl.ds(wid * SAMPLE_CH, SAMPLE_CH)], rows)
            for kk in range(2):
                pltpu.sync_copy(rows, xd_hbm.at[idxs_v.at[kk, 0]])

    return k(hp, hs, dest_p, dest_s)


def _sc_combine_gather(yd, dest_p, dest_s, tp, ts):
    width = yd.shape[1]
    per_w = tp // SC_WORKERS
    n_ch = per_w // COMB_CH
    n_sw = ts // SAMPLE_CH
    mesh = plsc.VectorSubcoreMesh(core_axis_name="c", subcore_axis_name="s")

    @functools.partial(
        pl.kernel, mesh=mesh,
        out_type=jax.ShapeDtypeStruct((2, tp + ts, width), yd.dtype),
        scratch_types=[
            pltpu.VMEM((2, n_ch, COMB_CH), jnp.int32),
            pltpu.VMEM((2, 1, SAMPLE_CH), jnp.int32),
            pltpu.VMEM((2, COMB_CH, width), yd.dtype),
            pltpu.SemaphoreType.DMA((2,)),
            pltpu.SemaphoreType.DMA((2,)),
        ],
        name="sc_combine_gather",
    )
    def k(yd_hbm, dp_hbm, ds_hbm, g_hbm, idx_v, idxs_v, bufs, gsem, wsem):
        wid = _sc_worker_id()
        base = wid * per_w
        for kk in range(2):
            pltpu.sync_copy(dp_hbm.at[kk, wid], idx_v.at[kk])
        items = [(kk, j) for kk in range(2) for j in range(n_ch)]
        gathers = [pltpu.make_async_copy(yd_hbm.at[idx_v.at[kk, j]], bufs.at[n % 2], gsem.at[n % 2])
                   for n, (kk, j) in enumerate(items)]
        gathers[0].start()
        for n, (kk, j) in enumerate(items):
            if n + 1 < len(items):
                gathers[n + 1].start()
            gathers[n].wait()
            w = pltpu.make_async_copy(bufs.at[n % 2], g_hbm.at[kk, pl.ds(base + j * COMB_CH, COMB_CH)],
                                      wsem.at[n % 2])
            w.start()
            w.wait()

        @pl.when(wid < n_sw)
        def _():
            for kk in range(2):
                pltpu.sync_copy(ds_hbm.at[kk, wid], idxs_v.at[kk])
            for kk in range(2):
                rows = bufs.at[kk, pl.ds(0, SAMPLE_CH)]
                pltpu.sync_copy(yd_hbm.at[idxs_v.at[kk, 0]], rows)
                pltpu.sync_copy(rows, g_hbm.at[kk, pl.ds(tp + wid * SAMPLE_CH, SAMPLE_CH)])

    return k(yd, dest_p, dest_s)


def _combine_kernel(x1_ref, g_ref, route_t_ref, x2_ref):
    fields = route_t_ref[...]
    tm = fields.shape[1]
    cols = jnp.transpose(jnp.concatenate([fields, jnp.zeros((LANES - ROUTE_FIELDS, tm), F32)], axis=0))
    w1 = cols[:, 2:3]
    w2 = cols[:, 3:4]
    x2_ref[...] = x1_ref[...] + _unpack_bf16_pairs(g_ref[0]) * w1 + _unpack_bf16_pairs(g_ref[1]) * w2


def _combine(x1, g, route_t, row0, tm):
    t = x1.shape[0]
    blk0 = row0 // tm
    row = lambda i: (i, 0)
    return pl.pallas_call(
        _combine_kernel,
        grid=(t // tm,),
        in_specs=[
            pl.BlockSpec((tm, D_MODEL), row),
            pl.BlockSpec((2, tm, D_MODEL // 2), lambda i: (0, blk0 + i, 0)),
            pl.BlockSpec((ROUTE_FIELDS, tm), lambda i: (0, i)),
        ],
        out_specs=pl.BlockSpec((tm, D_MODEL), row),
        out_shape=jax.ShapeDtypeStruct((t, D_MODEL), F32),
        compiler_params=pltpu.CompilerParams(
            dimension_semantics=("arbitrary",), vmem_limit_bytes=VMEM_LIMIT),
        name="combine",
    )(x1, g, route_t)


def _dest_layout(dest, workers, chunk):
    t = dest.shape[1]
    return dest.reshape(2, workers, t // (workers * chunk), chunk)


def _hier_moe(l, h2p, h2s, route_tp, route_ts, counts, w_gate, w_up, w_down):
    tp, ts = h2p.shape[0], h2s.shape[0]
    n_assign = 2 * (tp + ts)
    n_blocks = -(-n_assign // MOE_BM) + N_EXPERTS
    n_blocks = -(-n_blocks // MOE_STEP_BLOCKS) * MOE_STEP_BLOCKS
    pcounts = (counts + MOE_BM - 1) // MOE_BM * MOE_BM
    pends = jnp.cumsum(pcounts)
    poffsets = pends - pcounts
    starts = jnp.arange(n_blocks, dtype=jnp.int32) * MOE_BM
    block_e = jnp.minimum(jnp.sum((pends[None, :] <= starts[:, None]).astype(jnp.int32), axis=1),
                          N_EXPERTS - 1)
    experts = jnp.arange(N_EXPERTS, dtype=jnp.int32)

    def lookup(table, idx):
        return jnp.sum(jnp.where(idx[..., None] == experts, table, 0), axis=-1)

    rows_valid = jnp.clip(lookup(poffsets + counts, block_e) - starts, 0, MOE_BM).astype(jnp.int32)
    used = counts > 0
    last_e = jnp.max(jnp.where(used, jnp.arange(N_EXPERTS, dtype=jnp.int32), 0))
    block_e = jnp.where(rows_valid > 0, block_e, last_e).astype(jnp.int32)
    place = jnp.cumsum(used.astype(jnp.int32)) - 1
    by_place = jnp.sum(jnp.where(used[None, :] & (place[None, :] == experts[:, None]), experts[None, :], 0),
                       axis=1)
    n_used = jnp.sum(used.astype(jnp.int32))

    def at_place(p):
        return jnp.where(p < n_used, lookup(by_place, jnp.minimum(p, N_EXPERTS - 1)), -1).astype(jnp.int32)

    ahead_of = at_place(place + (WEIGHT_SLOTS - 1))
    next_e = lookup(ahead_of, block_e)
    slot = lookup(place % WEIGHT_SLOTS, block_e)
    first_e = at_place(jnp.arange(WEIGHT_SLOTS - 1, dtype=jnp.int32))

    def dest_of(route_t):
        return lookup(poffsets, route_t[0:2].astype(jnp.int32)) + route_t[4:6].astype(jnp.int32)

    dest_p, dest_s = dest_of(route_tp), dest_of(route_ts)
    n_sw = ts // SAMPLE_CH
    xd = _sc_dispatch(h2p, h2s, _dest_layout(dest_p, SC_WORKERS, DISP_CH),
                      _dest_layout(dest_s, n_sw, SAMPLE_CH), n_blocks * MOE_BM)
    yd = _moe_experts(l, block_e, rows_valid, next_e, slot, first_e, xd, w_gate, w_up, w_down)
    return _sc_combine_gather(yd, _dest_layout(dest_p, SC_WORKERS, COMB_CH),
                              _dest_layout(dest_s, n_sw, SAMPLE_CH), tp, ts)


def kernel(x_prompt, x_sample, state_pool, cache_k_win, cache_v_win, norm_attn_g, w_in, pool_w, pool_scale, q_norm_g, k_norm_g, attn_sinks, w_out, norm_ffn_g, router_group_w, router_group_b, router_expert_w, router_expert_b, w_gate, w_up, w_down):
    n_p, t_p, d = x_prompt.shape
    n_s, t_s, _ = x_sample.shape
    depth = w_in.shape[0]
    lw_s = cache_k_win.shape[2]
    assert t_s == 1 and lw_s == WINDOW and d == D_MODEL
    assert t_p % TM_PROJ == 0 and t_p >= WINDOW
    past_len = 16384

    seg = jnp.arange(256) // HEAD_DIM
    bd = jnp.where(seg[:, None] == seg[None, :], 1.0 / HEAD_DIM, 0.0).astype(BF16)
    slopes = jnp.exp2(-8.0 * jnp.arange(1, N_HEADS + 1, dtype=F32) / N_HEADS)
    bias_p = _prompt_bias_t()
    dist_s = (WINDOW - 1) - jnp.arange(WINDOW, dtype=F32)
    bias_s = -slopes[:, None] * dist_s[None, :]

    wp = jnp.zeros((depth, 2, 256, 256), F32)
    for p in range(2):
        wp = wp.at[:, p, :POOL_GC, :POOL_GC].set(pool_w[:, 2 * p])
        wp = wp.at[:, p, POOL_GC:, POOL_GC:].set(pool_w[:, 2 * p + 1])
    assert GROUP_LANE0 == N_EXPERTS
    lane_pad = LANES - N_EXPERTS - N_EXPERT_GROUPS
    wr = jnp.concatenate([router_expert_w, router_group_w, jnp.zeros((depth, D_MODEL, lane_pad), F32)], axis=-1)
    br = jnp.concatenate([router_expert_b, router_group_b, jnp.zeros((depth, lane_pad), F32)],
                         axis=-1).reshape(depth, 1, LANES)
    lp = dict(
        w_in=w_in.astype(BF16),
        w_out=w_out.astype(BF16),
        g_attn=norm_attn_g.reshape(depth, 1, D_MODEL),
        g_ffn=norm_ffn_g.reshape(depth, 1, D_MODEL),
        qg=(jnp.tile(q_norm_g, (1, N_HEADS)) * ATTN_SCALE).reshape(depth, 1, Q_W),
        kg=jnp.tile(k_norm_g, (1, N_KV_HEADS)).reshape(depth, 1, KV_W),
        wp=wp.astype(BF16),
        ps=pool_scale.reshape(depth, 1, POOL_W),
        wr=wr.astype(BF16),
        br=br,
        state=state_pool,
        ck=cache_k_win.reshape(depth, n_s, lw_s, KV_W),
        cv=cache_v_win.reshape(depth, n_s, lw_s, KV_W),
    )

    xp = x_prompt.reshape(n_p * t_p, D_MODEL)
    xs = x_sample.reshape(n_s, D_MODEL)
    lw_p = min(WINDOW, t_p)
    pool_p, kp_new, vp_new, pool_s, ks_new, vs_new = [], [], [], [], [], []
    zero_cnt = jnp.zeros((N_EXPERTS, 1), F32)
    pending = None
    for l in range(depth):
        sinks = attn_sinks[l]
        outs = _proj_pool_prompt(
            l, xp if pending is None else pending, n_p, t_p,
            lp["g_attn"], lp["w_in"], lp["qg"], lp["kg"], bd, lp["wp"], lp["ps"])
        if pending is not None:
            xp, outs = outs[0], outs[1:]
        pool_o, q, k, vt, utail, ktail, vtail = outs
        attn_o = _attn_prompt(q, k, vt, bias_p, sinks, n_p, t_p)
        x1p, h2p, route_tp, cnt_p = _merge_router(
            l, pool_o, attn_o, xp, lp["w_out"], lp["g_ffn"], lp["wr"], lp["br"], zero_cnt, TM_PROJ)
        pool_p.append(utail[:, 16 - POOL_STATE:, :])
        kp_new.append(ktail)
        vp_new.append(vtail)
        pool_so, attn_so, pst_s, kc_s, vc_s = _sample_mixer(
            l, xs, lp["g_attn"], lp["w_in"], lp["qg"], lp["kg"], bd, lp["wp"], lp["ps"],
            lp["state"], lp["ck"], lp["cv"], sinks.reshape(N_HEADS, 1), bias_s, past_len)
        x1s, h2s, route_ts, cnt_all = _merge_router(
            l, pool_so, attn_so, xs, lp["w_out"], lp["g_ffn"], lp["wr"], lp["br"], cnt_p, n_s)
        pool_s.append(pst_s)
        ks_new.append(kc_s)
        vs_new.append(vc_s)
        counts = cnt_all[:, 0].astype(jnp.int32)
        g = _hier_moe(l, h2p, h2s, route_tp, route_ts, counts, w_gate, w_up, w_down)
        xs = _combine(x1s, g, route_ts, n_p * t_p, n_s)
        pending = (x1p, g, route_tp)
    xp = _combine(*pending, 0, TM_PROJ)
    return (xp.reshape(n_p, t_p, D_MODEL), xs.reshape(n_s, t_s, D_MODEL),
            jnp.stack(pool_p),
            jnp.stack(kp_new).reshape(depth, n_p, lw_p, N_KV_HEADS, HEAD_DIM),
            jnp.stack(vp_new).reshape(depth, n_p, lw_p, N_KV_HEADS, HEAD_DIM),
            jnp.stack(pool_s),
            jnp.stack(ks_new).reshape(depth, n_s, lw_s, N_KV_HEADS, HEAD_DIM),
            jnp.stack(vs_new).reshape(depth, n_s, lw_s, N_KV_HEADS, HEAD_DIM))
```

```python
import functools

import jax
import jax.numpy as jnp
from jax import lax
from jax.experimental import pallas as pl
from jax.experimental.pallas import tpu as pltpu
from jax.experimental.pallas import tpu_sc as plsc

D_MODEL = 1024
POOL_W = 512
POOL_WINDOWS = (2, 4, 8, 16)
POOL_GC = 128
POOL_STATE = 15
HEAD_DIM = 64
N_HEADS = 8
N_KV_HEADS = 2
GQA_GROUP = 4
Q_W = 512
KV_W = 128
D_IN = POOL_W + Q_W + 2 * KV_W
WINDOW = 128
ATTN_SCALE = HEAD_DIM ** -0.5
N_EXPERT_GROUPS = 4
EXPERTS_PER_GROUP = 8
N_EXPERTS = 32
EXPERT_FF = 512
EPS = 1e-6

LANES = 128
HALO = 32
TM_PROJ = 512
MERGE_CHUNKS = 2
ATTN_QB = 8
MOE_BM = 256
MOE_STEP_BLOCKS = 4
WEIGHT_SLOTS = 3
GROUP_LANE0 = 32
ROUTE_FIELDS = 8
SC_CORES = 2
SC_SUBCORES = 16
SC_WORKERS = SC_CORES * SC_SUBCORES
DISP_CH = 64
COMB_CH = 64
SAMPLE_CH = 32
VMEM_LIMIT = 48 * 1024 * 1024

BF16 = jnp.bfloat16
F32 = jnp.float32


def _pack_bf16_pairs(h):
    w = h.shape[1] // 2
    hi = lax.bitcast_convert_type(h[:, :w].astype(F32), jnp.uint32)
    lo = lax.bitcast_convert_type(h[:, w:].astype(F32), jnp.uint32)
    return lax.bitcast_convert_type(hi | (lo >> 16), jnp.int32)


def _unpack_bf16_pairs(words):
    u = lax.bitcast_convert_type(words, jnp.uint32)
    hi = lax.bitcast_convert_type(u & jnp.uint32(0xFFFF0000), F32)
    lo = lax.bitcast_convert_type(u << 16, F32)
    return jnp.concatenate([hi, lo], axis=-1)


def _segment_mean_sq(a, bd):
    w = a.shape[1]
    return jnp.dot((a * a).astype(BF16), bd[:w, :w], preferred_element_type=F32)


def _rms_bf16(x, g):
    ms = jnp.mean(x * x, axis=-1, keepdims=True)
    return (x * lax.rsqrt(ms + EPS) * g).astype(BF16)


def _qk_norm(q, k, qg, kg, bd):
    qn = []
    for c in range(Q_W // 256):
        qc = q[:, c * 256:(c + 1) * 256]
        qn.append(qc * lax.rsqrt(_segment_mean_sq(qc, bd) + EPS))
    qn = jnp.concatenate(qn, axis=-1) * qg
    kn = k * lax.rsqrt(_segment_mean_sq(k, bd) + EPS) * kg
    return qn, kn


def _project(x, g, w_in, qg, kg, bd):
    z = jnp.dot(_rms_bf16(x, g), w_in, preferred_element_type=F32)
    u = z[:, :POOL_W]
    q = z[:, POOL_W:POOL_W + Q_W]
    k = z[:, POOL_W + Q_W:POOL_W + Q_W + KV_W]
    v = z[:, POOL_W + Q_W + KV_W:]
    qn, kn = _qk_norm(q, k, qg, kg, bd)
    return u, qn, kn, v


def _pool_project(d_groups, wp_ref, ps):
    outs = []
    for p in range(2):
        dp = jnp.concatenate([d_groups[2 * p], d_groups[2 * p + 1]], axis=-1).astype(BF16)
        y = jnp.dot(dp, wp_ref[p], preferred_element_type=F32)
        outs.append(y * ps[:, p * 256:(p + 1) * 256])
    return jnp.concatenate(outs, axis=-1)


def _proj_pool_kernel(x_ref, g_ref, win_ref, qg_ref, kg_ref, bd_ref, wp_ref, ps_ref,
                      pool_ref, q_ref, k_ref, vt_ref, utail_ref, ktail_ref, vtail_ref,
                      ext_ref, sa_ref, sb_ref, zq_ref, *, tm, n_j):
    j = pl.program_id(1)

    @pl.when(j == 0)
    def _():
        ext_ref[0:HALO, :] = jnp.zeros((HALO, POOL_W), F32)

    r = tm + HALO
    h = _rms_bf16(x_ref[...], g_ref[...])
    ext_ref[HALO:r, :] = jnp.dot(h, win_ref[:, 0:POOL_W], preferred_element_type=F32)
    zq_ref[...] = jnp.dot(h, win_ref[:, POOL_W:], preferred_element_type=F32)
    u = ext_ref[HALO:r, :]
    sa_ref[8:r, :] = ext_ref[8:r, :] + ext_ref[7:r - 1, :]
    sb_ref[16:r, 128:] = sa_ref[16:r, 128:] + sa_ref[14:r - 2, 128:]
    sa_ref[24:r, 256:] = sb_ref[24:r, 256:] + sb_ref[20:r - 4, 256:]
    sb_ref[32:r, 384:] = sa_ref[32:r, 384:] + sa_ref[24:r - 8, 384:]
    pos1 = j * tm + lax.broadcasted_iota(jnp.int32, (tm, POOL_GC), 0) + 1
    sums = (sa_ref, sb_ref, sa_ref, sb_ref)
    d_groups = []
    for gi, w in enumerate(POOL_WINDOWS):
        sl = slice(gi * POOL_GC, (gi + 1) * POOL_GC)
        cnt = jnp.minimum(pos1, w).astype(F32)
        d_groups.append(sums[gi][HALO:r, sl] / cnt - u[:, sl])
    pool_ref[...] = _pool_project(d_groups, wp_ref, ps_ref[...]).astype(BF16)
    ext_ref[16:HALO, :] = ext_ref[tm + 16:r, :]

    qn, kn = _qk_norm(zq_ref[:, 0:Q_W], zq_ref[:, Q_W:Q_W + KV_W], qg_ref[...], kg_ref[...], bd_ref[...])
    v = zq_ref[:, Q_W + KV_W:]
    q_ref[...] = qn.astype(BF16)
    k_ref[...] = kn.astype(BF16)
    vt_ref[...] = jnp.transpose(v).astype(BF16)

    @pl.when(j == n_j - 1)
    def _():
        utail_ref[...] = u[tm - 16:, :]
        ktail_ref[...] = kn[tm - WINDOW:, :]
        vtail_ref[...] = v[tm - WINDOW:, :]


def _proj_pool_combine_kernel(x1_ref, gath_ref, route_ref, *rest, tm, n_j):
    x2_ref = rest[7]
    _combine_kernel(x1_ref, gath_ref, route_ref, x2_ref)
    _proj_pool_kernel(x2_ref, *rest[:7], *rest[8:], tm=tm, n_j=n_j)


def _proj_pool_prompt(l, x_in, n_seq, seq, g_attn, w_in, qg, kg, bd, wp, ps):
    tm = TM_PROJ
    n_j = seq // tm
    t = n_seq * seq
    row = lambda b, j: (b * n_j + j, 0)
    lay = lambda b, j: (l, 0, 0)
    fused = isinstance(x_in, tuple)
    if fused:
        kern = _proj_pool_combine_kernel
        x_args = list(x_in)
        x_specs = [pl.BlockSpec((tm, D_MODEL), row),
                   pl.BlockSpec((2, tm, D_MODEL // 2), lambda b, j: (0, b * n_j + j, 0)),
                   pl.BlockSpec((ROUTE_FIELDS, tm), lambda b, j: (0, b * n_j + j))]
        x_out_specs = [pl.BlockSpec((tm, D_MODEL), row)]
        x_out_shape = [jax.ShapeDtypeStruct((t, D_MODEL), F32)]
    else:
        kern = _proj_pool_kernel
        x_args = [x_in]
        x_specs = [pl.BlockSpec((tm, D_MODEL), row)]
        x_out_specs, x_out_shape = [], []
    return pl.pallas_call(
        functools.partial(kern, tm=tm, n_j=n_j),
        grid=(n_seq, n_j),
        in_specs=x_specs + [
            pl.BlockSpec((None, 1, D_MODEL), lay),
            pl.BlockSpec((None, D_MODEL, D_IN), lay),
            pl.BlockSpec((None, 1, Q_W), lay),
            pl.BlockSpec((None, 1, KV_W), lay),
            pl.BlockSpec((256, 256), lambda b, j: (0, 0)),
            pl.BlockSpec((None, 2, 256, 256), lambda b, j: (l, 0, 0, 0)),
            pl.BlockSpec((None, 1, POOL_W), lay),
        ],
        out_specs=x_out_specs + [
            pl.BlockSpec((tm, POOL_W), row),
            pl.BlockSpec((tm, Q_W), row),
            pl.BlockSpec((tm, KV_W), row),
            pl.BlockSpec((KV_W, tm), lambda b, j: (0, b * n_j + j)),
            pl.BlockSpec((None, 16, POOL_W), lambda b, j: (b, 0, 0)),
            pl.BlockSpec((None, WINDOW, KV_W), lambda b, j: (b, 0, 0)),
            pl.BlockSpec((None, WINDOW, KV_W), lambda b, j: (b, 0, 0)),
        ],
        out_shape=x_out_shape + [
            jax.ShapeDtypeStruct((t, POOL_W), BF16),
            jax.ShapeDtypeStruct((t, Q_W), BF16),
            jax.ShapeDtypeStruct((t, KV_W), BF16),
            jax.ShapeDtypeStruct((KV_W, t), BF16),
            jax.ShapeDtypeStruct((n_seq, 16, POOL_W), F32),
            jax.ShapeDtypeStruct((n_seq, WINDOW, KV_W), F32),
            jax.ShapeDtypeStruct((n_seq, WINDOW, KV_W), F32),
        ],
        scratch_shapes=[pltpu.VMEM((tm + HALO, POOL_W), F32)] * 3 + [pltpu.VMEM((tm, Q_W + 2 * KV_W), F32)],
        compiler_params=pltpu.CompilerParams(
            dimension_semantics=("arbitrary", "arbitrary"), vmem_limit_bytes=VMEM_LIMIT),
        name="proj_pool_prompt",
    )(*x_args, g_attn, w_in, qg, kg, bd, wp, ps)


def _attn_kernel(sink_ref, q_ref, kp_ref, kc_ref, vtp_ref, vtc_ref, bias_ref, o_ref, s_ref):
    j = pl.program_id(1)
    kk_all = jnp.concatenate([kp_ref[...], kc_ref[...]], axis=0)
    vt_all = jnp.concatenate([vtp_ref[...], vtc_ref[...]], axis=1)
    from_prev = (lax.broadcasted_iota(jnp.int32, (WINDOW, WINDOW), 0)
                 > lax.broadcasted_iota(jnp.int32, (WINDOW, WINDOW), 1))
    units = [(blk, kv) for blk in range(ATTN_QB) for kv in range(N_KV_HEADS)]

    def scores(n):
        blk, kv = units[n]
        q = q_ref[blk * WINDOW:(blk + 1) * WINDOW, :]
        kk = kk_all[blk * WINDOW:(blk + 2) * WINDOW, kv * HEAD_DIM:(kv + 1) * HEAD_DIM]
        heads = range(kv * GQA_GROUP, (kv + 1) * GQA_GROUP)
        q_rows = jnp.concatenate([q[:, h * HEAD_DIM:(h + 1) * HEAD_DIM] for h in heads], axis=0)
        s_ref[n % 2] = lax.dot_general(kk, q_rows, (((1,), (1,)), ((), ())), preferred_element_type=F32)

    scores(0)
    outs = []
    for n, (blk, kv) in enumerate(units):
        if n + 1 < len(units):
            scores(n + 1)
        vt_kv = vt_all[kv * HEAD_DIM:(kv + 1) * HEAD_DIM, blk * WINDOW:(blk + 2) * WINDOW]
        variant = jnp.minimum(j, 1) if blk == 0 else 1
        for g in range(GQA_GROUP):
            h = kv * GQA_GROUP + g
            s = jnp.where(from_prev, s_ref[n % 2, 0:WINDOW, g * WINDOW:(g + 1) * WINDOW],
                          s_ref[n % 2, WINDOW:, g * WINDOW:(g + 1) * WINDOW]) + bias_ref[variant, h]
            sink = sink_ref[h]
            m = jnp.maximum(jnp.max(s, axis=0, keepdims=True), sink)
            p = jnp.exp(s - m)
            denom = jnp.sum(p, axis=0, keepdims=True) + jnp.exp(sink - m)
            p_keys = jnp.concatenate([jnp.where(from_prev, p, 0.0), jnp.where(from_prev, 0.0, p)], axis=0)
            o_t = jnp.dot(vt_kv, p_keys.astype(BF16), preferred_element_type=F32)
            outs.append(o_t / denom)
        if kv == N_KV_HEADS - 1:
            o_ref[blk * WINDOW:(blk + 1) * WINDOW, :] = jnp.transpose(jnp.concatenate(outs, axis=0)).astype(BF16)
            outs = []


def _attn_prompt(q, k, vt, bias_t, sinks, n_seq, seq):
    tq = ATTN_QB * WINDOW
    nj = seq // tq
    t = n_seq * seq
    cur = lambda b, j: (b * nj + j, 0)
    prev = lambda b, j: (jnp.maximum((b * nj + j) * ATTN_QB - 1, 0), 0)
    cur_t = lambda b, j: (0, b * nj + j)
    prev_t = lambda b, j: (0, jnp.maximum((b * nj + j) * ATTN_QB - 1, 0))
    return pl.pallas_call(
        _attn_kernel,
        grid=(n_seq, nj),
        in_specs=[
            pl.BlockSpec(memory_space=pltpu.SMEM),
            pl.BlockSpec((tq, Q_W), cur),
            pl.BlockSpec((WINDOW, KV_W), prev),
            pl.BlockSpec((tq, KV_W), cur),
            pl.BlockSpec((KV_W, WINDOW), prev_t),
            pl.BlockSpec((KV_W, tq), cur_t),
            pl.BlockSpec((2, N_HEADS, WINDOW, WINDOW), lambda b, j: (0, 0, 0, 0)),
        ],
        out_specs=pl.BlockSpec((tq, Q_W), cur),
        out_shape=jax.ShapeDtypeStruct((t, Q_W), BF16),
        scratch_shapes=[pltpu.VMEM((2, 2 * WINDOW, GQA_GROUP * WINDOW), F32)],
        compiler_params=pltpu.CompilerParams(
            dimension_semantics=("arbitrary", "arbitrary"), vmem_limit_bytes=VMEM_LIMIT),
        name="attn_prompt",
    )(sinks, q, k, k, vt, vt, bias_t)


def _prompt_bias_t():
    r = jnp.arange(WINDOW, dtype=jnp.int32)[None, :]
    c = jnp.arange(WINDOW, dtype=jnp.int32)[:, None]
    from_prev = c > r
    dist = r - c + jnp.where(from_prev, WINDOW, 0)
    slopes = jnp.exp2(-8.0 * jnp.arange(1, N_HEADS + 1, dtype=F32) / N_HEADS)
    later = -slopes[:, None, None] * dist.astype(F32)[None]
    first = jnp.where(from_prev[None], -jnp.inf, later)
    return jnp.stack([first, later])


def _sample_kernel(x_ref, g_ref, win_ref, qg_ref, kg_ref, bd_ref, wp_ref, ps_ref,
                   st_ref, ck_ref, cv_ref, sink_ref, bias_ref, perm_ref,
                   pool_ref, attn_ref, pst_ref, kc_ref, vc_ref, *, ns, pos0):
    u, qn, kn, v = _project(x_ref[...], g_ref[...], win_ref[...], qg_ref[...], kg_ref[...], bd_ref[...])
    pst_ref[:, 0:POOL_STATE - 1, :] = st_ref[:, 1:POOL_STATE, :]
    kc_ref[:, 0:WINDOW - 1, :] = ck_ref[:, 1:WINDOW, :]
    vc_ref[:, 0:WINDOW - 1, :] = cv_ref[:, 1:WINDOW, :]
    for n in range(ns):
        pst_ref[n, POOL_STATE - 1:POOL_STATE, :] = u[n:n + 1, :]
        kc_ref[n, WINDOW - 1:WINDOW, :] = kn[n:n + 1, :]
        vc_ref[n, WINDOW - 1:WINDOW, :] = v[n:n + 1, :]

    d_groups = []
    for gi, w in enumerate(POOL_WINDOWS):
        lo = gi * POOL_GC
        acc = u[:, lo:lo + POOL_GC]
        for back in range(1, w):
            acc = acc + st_ref[:, POOL_STATE - back, lo:lo + POOL_GC]
        d_groups.append(acc / float(min(pos0 + 1, w)) - u[:, lo:lo + POOL_GC])
    pool_ref[...] = _pool_project(d_groups, wp_ref, ps_ref[...]).astype(BF16)

    zeros = jnp.zeros((ns, HEAD_DIM), F32)
    stacked = []
    for h in range(N_HEADS):
        piece = qn[:, h * HEAD_DIM:(h + 1) * HEAD_DIM]
        pair = [piece, zeros] if h < GQA_GROUP else [zeros, piece]
        stacked.append(jnp.concatenate(pair, axis=-1))
    q_hn = jnp.concatenate(stacked, axis=0).astype(BF16)
    q_nh = jnp.dot(perm_ref[0], q_hn, preferred_element_type=F32).astype(BF16)

    keys = kc_ref[...].reshape(ns * WINDOW, KV_W).astype(BF16)
    vals = vc_ref[...].reshape(ns * WINDOW, KV_W).astype(BF16)
    s_all = lax.dot_general(q_nh, keys, (((1,), (1,)), ((), ())), preferred_element_type=F32)
    sink = sink_ref[...]
    bias = bias_ref[...]
    zero_blk = jnp.zeros((N_HEADS, WINDOW), F32)
    p_rows = []
    for n in range(ns):
        s = s_all[n * N_HEADS:(n + 1) * N_HEADS, n * WINDOW:(n + 1) * WINDOW] + bias
        m = jnp.maximum(jnp.max(s, axis=-1, keepdims=True), sink)
        p = jnp.exp(s - m)
        denom = jnp.sum(p, axis=-1, keepdims=True) + jnp.exp(sink - m)
        p_rows.append(jnp.concatenate([zero_blk] * n + [p / denom] + [zero_blk] * (ns - 1 - n), axis=-1))
    p_blockdiag = jnp.concatenate(p_rows, axis=0).astype(BF16)
    o_nh = jnp.dot(p_blockdiag, vals, preferred_element_type=F32).astype(BF16)
    o_hn = jnp.dot(perm_ref[1], o_nh, preferred_element_type=F32)
    pieces = []
    for h in range(N_HEADS):
        kv = h // GQA_GROUP
        pieces.append(o_hn[h * ns:(h + 1) * ns, kv * HEAD_DIM:(kv + 1) * HEAD_DIM])
    attn_ref[...] = jnp.concatenate(pieces, axis=-1).astype(BF16)


def _sample_mixer(l, depth, xs, g_attn, w_in, qg, kg, bd, wp, ps, state, ck, cv, sink8, bias_s, pos0):
    n = xs.shape[0]
    ns = 32
    row = lambda i: (i, 0)
    lay = lambda i: (l, 0, 0)
    src = jnp.arange(ns * N_HEADS)
    perm = (((src % N_HEADS) * ns + src // N_HEADS)[:, None] == src[None, :]).astype(BF16)
    perms = jnp.stack([perm, perm.T])
    return pl.pallas_call(
        functools.partial(_sample_kernel, ns=ns, pos0=pos0),
        grid=(n // ns,),
        input_output_aliases={8: 2, 9: 3, 10: 4},
        in_specs=[
            pl.BlockSpec((ns, D_MODEL), row),
            pl.BlockSpec((None, 1, D_MODEL), lay),
            pl.BlockSpec((None, D_MODEL, D_IN), lay),
            pl.BlockSpec((None, 1, Q_W), lay),
            pl.BlockSpec((None, 1, KV_W), lay),
            pl.BlockSpec((256, 256), lambda i: (0, 0)),
            pl.BlockSpec((None, 2, 256, 256), lambda i: (l, 0, 0, 0)),
            pl.BlockSpec((None, 1, POOL_W), lay),
            pl.BlockSpec((None, ns, POOL_STATE, POOL_W), lambda i: (l, i, 0, 0)),
            pl.BlockSpec((None, ns, WINDOW, KV_W), lambda i: (l, i, 0, 0)),
            pl.BlockSpec((None, ns, WINDOW, KV_W), lambda i: (l, i, 0, 0)),
            pl.BlockSpec((N_HEADS, 1), lambda i: (0, 0)),
            pl.BlockSpec((N_HEADS, WINDOW), lambda i: (0, 0)),
            pl.BlockSpec((2, ns * N_HEADS, ns * N_HEADS), lambda i: (0, 0, 0)),
        ],
        out_specs=[
            pl.BlockSpec((ns, POOL_W), row),
            pl.BlockSpec((ns, Q_W), row),
            pl.BlockSpec((None, ns, POOL_STATE, POOL_W), lambda i: (l, i, 0, 0)),
            pl.BlockSpec((None, ns, WINDOW, KV_W), lambda i: (l, i, 0, 0)),
            pl.BlockSpec((None, ns, WINDOW, KV_W), lambda i: (l, i, 0, 0)),
        ],
        out_shape=[
            jax.ShapeDtypeStruct((n, POOL_W), BF16),
            jax.ShapeDtypeStruct((n, Q_W), BF16),
            jax.ShapeDtypeStruct((depth, n, POOL_STATE, POOL_W), F32),
            jax.ShapeDtypeStruct((depth, n, WINDOW, KV_W), F32),
            jax.ShapeDtypeStruct((depth, n, WINDOW, KV_W), F32),
        ],
        compiler_params=pltpu.CompilerParams(
            dimension_semantics=("arbitrary",), vmem_limit_bytes=VMEM_LIMIT),
        name="sample_mixer",
    )(xs, g_attn, w_in, qg, kg, bd, wp, ps, state, ck, cv, sink8, bias_s, perms)


def _merge_router_kernel(pool_ref, attn_ref, x_ref, wout_ref, g_ref, wr_ref, br_ref, utri_ref, cin_ref,
                         x1_ref, h2_ref, route_t_ref, cnt_ref, y_ref, lg_ref):
    i = pl.program_id(0)

    @pl.when(i == 0)
    def _():
        cnt_ref[...] = cin_ref[...]

    tm = x_ref.shape[0]
    rc = tm // MERGE_CHUNKS
    chunks = [slice(ci * rc, (ci + 1) * rc) for ci in range(MERGE_CHUNKS)]
    for rows in chunks:
        y_ref[rows, :] = (jnp.dot(pool_ref[rows, :], wout_ref[0:POOL_W, :], preferred_element_type=F32)
                          + jnp.dot(attn_ref[rows, :], wout_ref[POOL_W:, :], preferred_element_type=F32))
    for rows in chunks:
        x1 = x_ref[rows, :] + y_ref[rows, :]
        x1_ref[rows, :] = x1
        h2 = _rms_bf16(x1, g_ref[...])
        h2_ref[rows, :] = _pack_bf16_pairs(h2)
        lg_ref[rows, :] = jnp.dot(h2, wr_ref[...], preferred_element_type=F32) + br_ref[...]
    logits = lg_ref[...]

    lt = jnp.transpose(logits)
    sub = lax.broadcasted_iota(jnp.int32, (EXPERTS_PER_GROUP, tm), 0)
    neg = -jnp.inf
    big = jnp.int32(EXPERTS_PER_GROUP)
    gl = jnp.where(sub < N_EXPERT_GROUPS, lt[GROUP_LANE0:GROUP_LANE0 + EXPERTS_PER_GROUP, :], neg)
    gmax = jnp.max(gl, axis=0, keepdims=True)
    grp = jnp.min(jnp.where(gl == gmax, sub, big), axis=0, keepdims=True)
    g_w = 1.0 / jnp.sum(jnp.exp(gl - gmax), axis=0, keepdims=True)
    el = lt[(N_EXPERT_GROUPS - 1) * EXPERTS_PER_GROUP:N_EXPERT_GROUPS * EXPERTS_PER_GROUP, :]
    for gi in range(N_EXPERT_GROUPS - 2, -1, -1):
        el = jnp.where(grp == gi, lt[gi * EXPERTS_PER_GROUP:(gi + 1) * EXPERTS_PER_GROUP, :], el)
    v1 = jnp.max(el, axis=0, keepdims=True)
    i1 = jnp.min(jnp.where(el == v1, sub, big), axis=0, keepdims=True)
    el2 = jnp.where(sub == i1, neg, el)
    v2 = jnp.max(el2, axis=0, keepdims=True)
    i2 = jnp.min(jnp.where(el2 == v2, sub, big), axis=0, keepdims=True)
    e21 = jnp.exp(v2 - v1)
    w1 = g_w / (1.0 + e21)
    w2 = g_w * e21 / (1.0 + e21)
    e1 = grp * EXPERTS_PER_GROUP + i1
    e2 = grp * EXPERTS_PER_GROUP + i2

    esub = lax.broadcasted_iota(jnp.int32, (N_EXPERTS, tm), 0)
    oh1 = esub == e1
    oh2 = esub == e2
    c = jnp.where(oh1 | oh2, 1.0, 0.0)
    prefix = jnp.dot(c.astype(BF16), utri_ref[...], preferred_element_type=F32) + cnt_ref[...]
    r1 = jnp.sum(jnp.where(oh1, prefix, 0.0), axis=0, keepdims=True)
    r2 = jnp.sum(jnp.where(oh2, prefix, 0.0), axis=0, keepdims=True)
    cnt_ref[...] = cnt_ref[...] + jnp.sum(c, axis=1, keepdims=True)

    fields = jnp.zeros((ROUTE_FIELDS, tm), F32)
    for idx, val in enumerate((e1.astype(F32), e2.astype(F32), w1, w2, r1, r2)):
        fields = jnp.where(sub == idx, val, fields)
    route_t_ref[...] = fields


def _merge_router(l, pool, attn, x2d, w_out, g_ffn, wr, br, cnt_in, tm):
    t = x2d.shape[0]
    utri = (jnp.arange(tm)[:, None] < jnp.arange(tm)[None, :]).astype(BF16)
    row = lambda i: (i, 0)
    lay = lambda i: (l, 0, 0)
    return pl.pallas_call(
        _merge_router_kernel,
        grid=(t // tm,),
        in_specs=[
            pl.BlockSpec((tm, POOL_W), row),
            pl.BlockSpec((tm, Q_W), row),
            pl.BlockSpec((tm, D_MODEL), row),
            pl.BlockSpec((None, D_MODEL, D_MODEL), lay),
            pl.BlockSpec((None, 1, D_MODEL), lay),
            pl.BlockSpec((None, D_MODEL, LANES), lay),
            pl.BlockSpec((None, 1, LANES), lay),
            pl.BlockSpec((tm, tm), lambda i: (0, 0)),
            pl.BlockSpec((N_EXPERTS, 1), lambda i: (0, 0)),
        ],
        out_specs=[
            pl.BlockSpec((tm, D_MODEL), row),
            pl.BlockSpec((tm, D_MODEL // 2), row),
            pl.BlockSpec((ROUTE_FIELDS, tm), lambda i: (0, i)),
            pl.BlockSpec((N_EXPERTS, 1), lambda i: (0, 0)),
        ],
        out_shape=[
            jax.ShapeDtypeStruct((t, D_MODEL), F32),
            jax.ShapeDtypeStruct((t, D_MODEL // 2), jnp.int32),
            jax.ShapeDtypeStruct((ROUTE_FIELDS, t), F32),
            jax.ShapeDtypeStruct((N_EXPERTS, 1), F32),
        ],
        scratch_shapes=[pltpu.VMEM((tm, D_MODEL), F32), pltpu.VMEM((tm, LANES), F32)],
        compiler_params=pltpu.CompilerParams(
            dimension_semantics=("arbitrary",), vmem_limit_bytes=VMEM_LIMIT),
        name="merge_router",
    )(pool, attn, x2d, w_out, g_ffn, wr, br, utri, cnt_in)


def _moe_kernel(be_ref, rv_ref, nx_ref, sl_ref, first_ref, xd_ref, wg_hbm, wu_hbm, wd_hbm, yd_ref,
                wg_f, wu_f, wd_f, wg_s, wu_s, wd_s, sem, *, layer):
    step = pl.program_id(0)

    def weight_copies(e, s):
        return [pltpu.make_async_copy(w_hbm.at[layer, e], w_f.at[s], sem.at[s, n])
                for n, (w_hbm, w_f) in enumerate(((wg_hbm, wg_f), (wu_hbm, wu_f), (wd_hbm, wd_f)))]

    @pl.when(step == 0)
    def _():
        for s in range(WEIGHT_SLOTS - 1):
            @pl.when(first_ref[s] >= 0)
            def _():
                for c in weight_copies(first_ref[s], s):
                    c.start()

    def enter_expert(i):
        expert, slot = be_ref[i], sl_ref[i]

        @pl.when((i == 0) | (expert != be_ref[jnp.maximum(i - 1, 0)]))
        def _():
            for c in weight_copies(expert, slot):
                c.wait()

            @pl.when(nx_ref[i] >= 0)
            def _():
                for c in weight_copies(nx_ref[i], lax.rem(slot + WEIGHT_SLOTS - 1, WEIGHT_SLOTS)):
                    c.start(priority=1)

            wg_s[...] = wg_f[slot].astype(BF16)
            wu_s[...] = wu_f[slot].astype(BF16)
            wd_s[...] = wd_f[slot].astype(BF16)

    def experts_on(row0, n_rows, rows_valid):
        rows = pl.ds(row0, n_rows)
        row = lax.broadcasted_iota(jnp.int32, (n_rows, D_MODEL // 2), 0)
        x = _unpack_bf16_pairs(jnp.where(row < rows_valid, xd_ref[rows, :], 0)).astype(BF16)
        gate = jnp.dot(x, wg_s[...], preferred_element_type=F32)
        up = jnp.dot(x, wu_s[...], preferred_element_type=F32)
        act = (gate * jax.nn.sigmoid(gate) * up).astype(BF16)
        y = jnp.dot(act, wd_s[...], preferred_element_type=F32)
        yd_ref[rows, :] = _pack_bf16_pairs(y.astype(BF16))

    def single_block(i, row0):
        enter_expert(i)

        @pl.when(rv_ref[i] > 0)
        def _():
            experts_on(row0, MOE_BM, rv_ref[i])

        @pl.when(rv_ref[i] <= 0)
        def _():
            yd_ref[pl.ds(row0, MOE_BM), :] = jnp.zeros((MOE_BM, D_MODEL // 2), jnp.int32)

    for pair in range(MOE_STEP_BLOCKS // 2):
        ia = step * MOE_STEP_BLOCKS + 2 * pair
        ib = ia + 1
        row0 = 2 * pair * MOE_BM
        same = (be_ref[ib] == be_ref[ia]) & (rv_ref[ib] > 0)

        @pl.when(same)
        def _():
            enter_expert(ia)
            experts_on(row0, 2 * MOE_BM, MOE_BM + rv_ref[ib])

        @pl.when(jnp.logical_not(same))
        def _():
            single_block(ia, row0)
            single_block(ib, row0 + MOE_BM)


def _moe_experts(l, block_e, rows_valid, next_e, slot, first_e, xd, w_gate, w_up, w_down):
    n_blocks = xd.shape[0] // MOE_BM
    step_rows = MOE_STEP_BLOCKS * MOE_BM
    row = lambda i, *_: (i, 0)
    return pl.pallas_call(
        functools.partial(_moe_kernel, layer=l),
        grid_spec=pltpu.PrefetchScalarGridSpec(
            num_scalar_prefetch=5,
            grid=(n_blocks // MOE_STEP_BLOCKS,),
            in_specs=[
                pl.BlockSpec((step_rows, D_MODEL // 2), row),
                pl.BlockSpec(memory_space=pl.ANY),
                pl.BlockSpec(memory_space=pl.ANY),
                pl.BlockSpec(memory_space=pl.ANY),
            ],
            out_specs=pl.BlockSpec((step_rows, D_MODEL // 2), row),
            scratch_shapes=[
                pltpu.VMEM((WEIGHT_SLOTS, D_MODEL, EXPERT_FF), F32),
                pltpu.VMEM((WEIGHT_SLOTS, D_MODEL, EXPERT_FF), F32),
                pltpu.VMEM((WEIGHT_SLOTS, EXPERT_FF, D_MODEL), F32),
                pltpu.VMEM((D_MODEL, EXPERT_FF), BF16),
                pltpu.VMEM((D_MODEL, EXPERT_FF), BF16),
                pltpu.VMEM((EXPERT_FF, D_MODEL), BF16),
                pltpu.SemaphoreType.DMA((WEIGHT_SLOTS, 3)),
            ],
        ),
        out_shape=jax.ShapeDtypeStruct((n_blocks * MOE_BM, D_MODEL // 2), jnp.int32),
        compiler_params=pltpu.CompilerParams(
            dimension_semantics=("arbitrary",), vmem_limit_bytes=VMEM_LIMIT),
        name="moe_experts",
    )(block_e, rows_valid, next_e, slot, first_e, xd, w_gate, w_up, w_down)


def _sc_worker_id():
    return lax.axis_index("s") * SC_CORES + lax.axis_index("c")


def _sc_dispatch(hp, hs, dest_p, dest_s, n_rows):
    tp, width = hp.shape
    per_w = tp // SC_WORKERS
    n_ch = per_w // DISP_CH
    n_sw = hs.shape[0] // SAMPLE_CH
    mesh = plsc.VectorSubcoreMesh(core_axis_name="c", subcore_axis_name="s")

    @functools.partial(
        pl.kernel, mesh=mesh,
        out_type=jax.ShapeDtypeStruct((n_rows, width), jnp.int32),
        scratch_types=[
            pltpu.VMEM((2, n_ch, DISP_CH), jnp.int32),
            pltpu.VMEM((2, 1, SAMPLE_CH), jnp.int32),
            pltpu.VMEM((2, DISP_CH, width), jnp.int32),
            pltpu.SemaphoreType.DMA((2,)),
            pltpu.SemaphoreType.DMA((2,)),
        ],
        name="sc_dispatch",
    )
    def k(hp_hbm, hs_hbm, dp_hbm, ds_hbm, xd_hbm, idx_v, idxs_v, bufs, rsem, wsem):
        wid = _sc_worker_id()
        base = wid * per_w
        for kk in range(2):
            pltpu.sync_copy(dp_hbm.at[kk, wid], idx_v.at[kk])
        reads = [pltpu.make_async_copy(hp_hbm.at[pl.ds(base + j * DISP_CH, DISP_CH)],
                                       bufs.at[j % 2], rsem.at[j % 2]) for j in range(n_ch)]
        reads[0].start()
        for j in range(n_ch):
            if j + 1 < n_ch:
                reads[j + 1].start()
            reads[j].wait()
            writes = [pltpu.make_async_copy(bufs.at[j % 2], xd_hbm.at[idx_v.at[kk, j]], wsem.at[kk])
                      for kk in range(2)]
            for w in writes:
                w.start()
            for w in writes:
                w.wait()

        @pl.when(wid < n_sw)
        def _():
            rows = bufs.at[0, pl.ds(0, SAMPLE_CH)]
            for kk in range(2):
                pltpu.sync_copy(ds_hbm.at[kk, wid], idxs_v.at[kk])
            pltpu.sync_copy(hs_hbm.at[pl.ds(wid * SAMPLE_CH, SAMPLE_CH)], rows)
            for kk in range(2):
                pltpu.sync_copy(rows, xd_hbm.at[idxs_v.at[kk, 0]])

    return k(hp, hs, dest_p, dest_s)


def _sc_combine_gather(yd, dest_p, dest_s, tp, ts):
    width = yd.shape[1]
    per_w = tp // SC_WORKERS
    n_ch = per_w // COMB_CH
    n_sw = ts // SAMPLE_CH
    mesh = plsc.VectorSubcoreMesh(core_axis_name="c", subcore_axis_name="s")

    @functools.partial(
        pl.kernel, mesh=mesh,
        out_type=jax.ShapeDtypeStruct((2, tp + ts, width), yd.dtype),
        scratch_types=[
            pltpu.VMEM((2, n_ch, COMB_CH), jnp.int32),
            pltpu.VMEM((2, 1, SAMPLE_CH), jnp.int32),
            pltpu.VMEM((2, COMB_CH, width), yd.dtype),
            pltpu.SemaphoreType.DMA((2,)),
            pltpu.SemaphoreType.DMA((2,)),
        ],
        name="sc_combine_gather",
    )
    def k(yd_hbm, dp_hbm, ds_hbm, g_hbm, idx_v, idxs_v, bufs, gsem, wsem):
        wid = _sc_worker_id()
        base = wid * per_w
        for kk in range(2):
            pltpu.sync_copy(dp_hbm.at[kk, wid], idx_v.at[kk])
        items = [(kk, j) for kk in range(2) for j in range(n_ch)]
        gathers = [pltpu.make_async_copy(yd_hbm.at[idx_v.at[kk, j]], bufs.at[n % 2], gsem.at[n % 2])
                   for n, (kk, j) in enumerate(items)]
        gathers[0].start()
        for n, (kk, j) in enumerate(items):
            if n + 1 < len(items):
                gathers[n + 1].start()
            gathers[n].wait()
            w = pltpu.make_async_copy(bufs.at[n % 2], g_hbm.at[kk, pl.ds(base + j * COMB_CH, COMB_CH)],
                                      wsem.at[n % 2])
            w.start()
            w.wait()

        @pl.when(wid < n_sw)
        def _():
            for kk in range(2):
                pltpu.sync_copy(ds_hbm.at[kk, wid], idxs_v.at[kk])
            for kk in range(2):
                rows = bufs.at[kk, pl.ds(0, SAMPLE_CH)]
                pltpu.sync_copy(yd_hbm.at[idxs_v.at[kk, 0]], rows)
                pltpu.sync_copy(rows, g_hbm.at[kk, pl.ds(tp + wid * SAMPLE_CH, SAMPLE_CH)])

    return k(yd, dest_p, dest_s)


def _combine_kernel(x1_ref, g_ref, route_t_ref, x2_ref):
    fields = route_t_ref[...]
    tm = fields.shape[1]
    cols = jnp.transpose(jnp.concatenate([fields, jnp.zeros((LANES - ROUTE_FIELDS, tm), F32)], axis=0))
    w1 = cols[:, 2:3]
    w2 = cols[:, 3:4]
    x2_ref[...] = x1_ref[...] + _unpack_bf16_pairs(g_ref[0]) * w1 + _unpack_bf16_pairs(g_ref[1]) * w2


def _combine(x1, g, route_t, row0, tm):
    t = x1.shape[0]
    blk0 = row0 // tm
    row = lambda i: (i, 0)
    return pl.pallas_call(
        _combine_kernel,
        grid=(t // tm,),
        in_specs=[
            pl.BlockSpec((tm, D_MODEL), row),
            pl.BlockSpec((2, tm, D_MODEL // 2), lambda i: (0, blk0 + i, 0)),
            pl.BlockSpec((ROUTE_FIELDS, tm), lambda i: (0, i)),
        ],
        out_specs=pl.BlockSpec((tm, D_MODEL), row),
        out_shape=jax.ShapeDtypeStruct((t, D_MODEL), F32),
        compiler_params=pltpu.CompilerParams(
            dimension_semantics=("arbitrary",), vmem_limit_bytes=VMEM_LIMIT),
        name="combine",
    )(x1, g, route_t)


def _dest_layout(dest, workers, chunk):
    t = dest.shape[1]
    return dest.reshape(2, workers, t // (workers * chunk), chunk)


def _hier_moe(l, h2p, h2s, route_tp, route_ts, counts, w_gate, w_up, w_down):
    tp, ts = h2p.shape[0], h2s.shape[0]
    n_assign = 2 * (tp + ts)
    n_blocks = -(-n_assign // MOE_BM) + N_EXPERTS
    n_blocks = -(-n_blocks // MOE_STEP_BLOCKS) * MOE_STEP_BLOCKS
    pcounts = (counts + MOE_BM - 1) // MOE_BM * MOE_BM
    pends = jnp.cumsum(pcounts)
    poffsets = pends - pcounts
    starts = jnp.arange(n_blocks, dtype=jnp.int32) * MOE_BM
    block_e = jnp.minimum(jnp.sum((pends[None, :] <= starts[:, None]).astype(jnp.int32), axis=1),
                          N_EXPERTS - 1)
    experts = jnp.arange(N_EXPERTS, dtype=jnp.int32)

    def lookup(table, idx):
        return jnp.sum(jnp.where(idx[..., None] == experts, table, 0), axis=-1)

    rows_valid = jnp.clip(lookup(poffsets + counts, block_e) - starts, 0, MOE_BM).astype(jnp.int32)
    used = counts > 0
    last_e = jnp.max(jnp.where(used, jnp.arange(N_EXPERTS, dtype=jnp.int32), 0))
    block_e = jnp.where(rows_valid > 0, block_e, last_e).astype(jnp.int32)
    place = jnp.cumsum(used.astype(jnp.int32)) - 1
    by_place = jnp.sum(jnp.where(used[None, :] & (place[None, :] == experts[:, None]), experts[None, :], 0),
                       axis=1)
    n_used = jnp.sum(used.astype(jnp.int32))

    def at_place(p):
        return jnp.where(p < n_used, lookup(by_place, jnp.minimum(p, N_EXPERTS - 1)), -1).astype(jnp.int32)

    ahead_of = at_place(place + (WEIGHT_SLOTS - 1))
    next_e = lookup(ahead_of, block_e)
    slot = lookup(place % WEIGHT_SLOTS, block_e)
    first_e = at_place(jnp.arange(WEIGHT_SLOTS - 1, dtype=jnp.int32))

    def dest_of(route_t):
        return lookup(poffsets, route_t[0:2].astype(jnp.int32)) + route_t[4:6].astype(jnp.int32)

    dest_p, dest_s = dest_of(route_tp), dest_of(route_ts)
    n_sw = ts // SAMPLE_CH
    xd = _sc_dispatch(h2p, h2s, _dest_layout(dest_p, SC_WORKERS, DISP_CH),
                      _dest_layout(dest_s, n_sw, SAMPLE_CH), n_blocks * MOE_BM)
    yd = _moe_experts(l, block_e, rows_valid, next_e, slot, first_e, xd, w_gate, w_up, w_down)
    return _sc_combine_gather(yd, _dest_layout(dest_p, SC_WORKERS, COMB_CH),
                              _dest_layout(dest_s, n_sw, SAMPLE_CH), tp, ts)


def kernel(x_prompt, x_sample, state_pool, cache_k_win, cache_v_win, norm_attn_g, w_in, pool_w, pool_scale, q_norm_g, k_norm_g, attn_sinks, w_out, norm_ffn_g, router_group_w, router_group_b, router_expert_w, router_expert_b, w_gate, w_up, w_down):
    n_p, t_p, d = x_prompt.shape
    n_s, t_s, _ = x_sample.shape
    depth = w_in.shape[0]
    lw_s = cache_k_win.shape[2]
    assert t_s == 1 and lw_s == WINDOW and d == D_MODEL
    assert t_p % TM_PROJ == 0 and t_p >= WINDOW
    past_len = 16384

    seg = jnp.arange(256) // HEAD_DIM
    bd = jnp.where(seg[:, None] == seg[None, :], 1.0 / HEAD_DIM, 0.0).astype(BF16)
    slopes = jnp.exp2(-8.0 * jnp.arange(1, N_HEADS + 1, dtype=F32) / N_HEADS)
    bias_p = _prompt_bias_t()
    dist_s = (WINDOW - 1) - jnp.arange(WINDOW, dtype=F32)
    bias_s = -slopes[:, None] * dist_s[None, :]

    wp = jnp.zeros((depth, 2, 256, 256), F32)
    for p in range(2):
        wp = wp.at[:, p, :POOL_GC, :POOL_GC].set(pool_w[:, 2 * p])
        wp = wp.at[:, p, POOL_GC:, POOL_GC:].set(pool_w[:, 2 * p + 1])
    assert GROUP_LANE0 == N_EXPERTS
    lane_pad = LANES - N_EXPERTS - N_EXPERT_GROUPS
    wr = jnp.concatenate([router_expert_w, router_group_w, jnp.zeros((depth, D_MODEL, lane_pad), F32)], axis=-1)
    br = jnp.concatenate([router_expert_b, router_group_b, jnp.zeros((depth, lane_pad), F32)],
                         axis=-1).reshape(depth, 1, LANES)
    lp = dict(
        w_in=w_in.astype(BF16),
        w_out=w_out.astype(BF16),
        g_attn=norm_attn_g.reshape(depth, 1, D_MODEL),
        g_ffn=norm_ffn_g.reshape(depth, 1, D_MODEL),
        qg=(jnp.tile(q_norm_g, (1, N_HEADS)) * ATTN_SCALE).reshape(depth, 1, Q_W),
        kg=jnp.tile(k_norm_g, (1, N_KV_HEADS)).reshape(depth, 1, KV_W),
        wp=wp.astype(BF16),
        ps=pool_scale.reshape(depth, 1, POOL_W),
        wr=wr.astype(BF16),
        br=br,
        state=state_pool,
        ck=cache_k_win.reshape(depth, n_s, lw_s, KV_W),
        cv=cache_v_win.reshape(depth, n_s, lw_s, KV_W),
    )

    xp = x_prompt.reshape(n_p * t_p, D_MODEL)
    xs = x_sample.reshape(n_s, D_MODEL)
    lw_p = min(WINDOW, t_p)
    pool_p, kp_new, vp_new = [], [], []
    sample_state = [lp["state"], lp["ck"], lp["cv"]]
    zero_cnt = jnp.zeros((N_EXPERTS, 1), F32)
    pending = None
    for l in range(depth):
        sinks = attn_sinks[l]
        outs = _proj_pool_prompt(
            l, xp if pending is None else pending, n_p, t_p,
            lp["g_attn"], lp["w_in"], lp["qg"], lp["kg"], bd, lp["wp"], lp["ps"])
        if pending is not None:
            xp, outs = outs[0], outs[1:]
        pool_o, q, k, vt, utail, ktail, vtail = outs
        attn_o = _attn_prompt(q, k, vt, bias_p, sinks, n_p, t_p)
        x1p, h2p, route_tp, cnt_p = _merge_router(
            l, pool_o, attn_o, xp, lp["w_out"], lp["g_ffn"], lp["wr"], lp["br"], zero_cnt, TM_PROJ)
        pool_p.append(utail[:, 16 - POOL_STATE:, :])
        kp_new.append(ktail)
        vp_new.append(vtail)
        pool_so, attn_so, *sample_state = _sample_mixer(
            l, depth, xs, lp["g_attn"], lp["w_in"], lp["qg"], lp["kg"], bd, lp["wp"], lp["ps"],
            *sample_state, sinks.reshape(N_HEADS, 1), bias_s, past_len)
        x1s, h2s, route_ts, cnt_all = _merge_router(
            l, pool_so, attn_so, xs, lp["w_out"], lp["g_ffn"], lp["wr"], lp["br"], cnt_p, n_s)
        counts = cnt_all[:, 0].astype(jnp.int32)
        g = _hier_moe(l, h2p, h2s, route_tp, route_ts, counts, w_gate, w_up, w_down)
        xs = _combine(x1s, g, route_ts, n_p * t_p, n_s)
        pending = (x1p, g, route_tp)
    xp = _combine(*pending, 0, TM_PROJ)
    return (xp.reshape(n_p, t_p, D_MODEL), xs.reshape(n_s, t_s, D_MODEL),
            jnp.stack(pool_p),
            jnp.stack(kp_new).reshape(depth, n_p, lw_p, N_KV_HEADS, HEAD_DIM),
            jnp.stack(vp_new).reshape(depth, n_p, lw_p, N_KV_HEADS, HEAD_DIM),
            sample_state[0],
            sample_state[1].reshape(depth, n_s, lw_s, N_KV_HEADS, HEAD_DIM),
            sample_state[2].reshape(depth, n_s, lw_s, N_KV_HEADS, HEAD_DIM))
```

```python
import functools

import jax
import jax.numpy as jnp
from jax import lax
from jax.experimental import pallas as pl
from jax.experimental.pallas import tpu as pltpu
from jax.experimental.pallas import tpu_sc as plsc

D_MODEL = 1024
POOL_W = 512
POOL_WINDOWS = (2, 4, 8, 16)
POOL_GC = 128
POOL_STATE = 15
HEAD_DIM = 64
N_HEADS = 8
N_KV_HEADS = 2
GQA_GROUP = 4
Q_W = 512
KV_W = 128
D_IN = POOL_W + Q_W + 2 * KV_W
WINDOW = 128
ATTN_SCALE = HEAD_DIM ** -0.5
N_EXPERT_GROUPS = 4
EXPERTS_PER_GROUP = 8
N_EXPERTS = 32
EXPERT_FF = 512
EPS = 1e-6

LANES = 128
HALO = 32
TM_PROJ = 1024
TM_MERGE = 512
MERGE_CHUNKS = 2
ATTN_QB = 16
MOE_BM = 256
MOE_STEP_BLOCKS = 4
WEIGHT_SLOTS = 3
GROUP_LANE0 = 32
ROUTE_FIELDS = 8
SC_CORES = 2
SC_SUBCORES = 16
SC_WORKERS = SC_CORES * SC_SUBCORES
DISP_CH = 64
COMB_CH = 64
SAMPLE_CH = 32
VMEM_LIMIT = 48 * 1024 * 1024

BF16 = jnp.bfloat16
F32 = jnp.float32


def _pack_bf16_pairs(h):
    w = h.shape[1] // 2
    hi = lax.bitcast_convert_type(h[:, :w].astype(F32), jnp.uint32)
    lo = lax.bitcast_convert_type(h[:, w:].astype(F32), jnp.uint32)
    return lax.bitcast_convert_type(hi | (lo >> 16), jnp.int32)


def _unpack_bf16_pairs(words):
    u = lax.bitcast_convert_type(words, jnp.uint32)
    hi = lax.bitcast_convert_type(u & jnp.uint32(0xFFFF0000), F32)
    lo = lax.bitcast_convert_type(u << 16, F32)
    return jnp.concatenate([hi, lo], axis=-1)


def _segment_mean_sq(a, bd):
    w = a.shape[1]
    return jnp.dot((a * a).astype(BF16), bd[:w, :w], preferred_element_type=F32)


def _rms_bf16(x, g):
    ms = jnp.mean(x * x, axis=-1, keepdims=True)
    return (x * lax.rsqrt(ms + EPS) * g).astype(BF16)


def _qk_norm(q, k, qg, kg, bd):
    qn = []
    for c in range(Q_W // 256):
        qc = q[:, c * 256:(c + 1) * 256]
        qn.append(qc * lax.rsqrt(_segment_mean_sq(qc, bd) + EPS))
    qn = jnp.concatenate(qn, axis=-1) * qg
    kn = k * lax.rsqrt(_segment_mean_sq(k, bd) + EPS) * kg
    return qn, kn


def _project(x, g, w_in, qg, kg, bd):
    z = jnp.dot(_rms_bf16(x, g), w_in, preferred_element_type=F32)
    u = z[:, :POOL_W]
    q = z[:, POOL_W:POOL_W + Q_W]
    k = z[:, POOL_W + Q_W:POOL_W + Q_W + KV_W]
    v = z[:, POOL_W + Q_W + KV_W:]
    qn, kn = _qk_norm(q, k, qg, kg, bd)
    return u, qn, kn, v


def _pool_project(d_groups, wp_ref, ps):
    outs = []
    for p in range(2):
        dp = jnp.concatenate([d_groups[2 * p], d_groups[2 * p + 1]], axis=-1).astype(BF16)
        y = jnp.dot(dp, wp_ref[p], preferred_element_type=F32)
        outs.append(y * ps[:, p * 256:(p + 1) * 256])
    return jnp.concatenate(outs, axis=-1)


def _proj_pool_kernel(x_ref, g_ref, win_ref, qg_ref, kg_ref, bd_ref, wp_ref, ps_ref,
                      pool_ref, q_ref, k_ref, vt_ref, utail_ref, ktail_ref, vtail_ref,
                      ext_ref, sa_ref, sb_ref, zq_ref, *, tm, n_j):
    j = pl.program_id(1)

    @pl.when(j == 0)
    def _():
        ext_ref[0:HALO, :] = jnp.zeros((HALO, POOL_W), F32)

    r = tm + HALO
    h = _rms_bf16(x_ref[...], g_ref[...])
    ext_ref[HALO:r, :] = jnp.dot(h, win_ref[:, 0:POOL_W], preferred_element_type=F32)
    zq_ref[...] = jnp.dot(h, win_ref[:, POOL_W:], preferred_element_type=F32)
    u = ext_ref[HALO:r, :]
    sa_ref[8:r, :] = ext_ref[8:r, :] + ext_ref[7:r - 1, :]
    sb_ref[16:r, 128:] = sa_ref[16:r, 128:] + sa_ref[14:r - 2, 128:]
    sa_ref[24:r, 256:] = sb_ref[24:r, 256:] + sb_ref[20:r - 4, 256:]
    sb_ref[32:r, 384:] = sa_ref[32:r, 384:] + sa_ref[24:r - 8, 384:]
    pos1 = j * tm + lax.broadcasted_iota(jnp.int32, (tm, POOL_GC), 0) + 1
    sums = (sa_ref, sb_ref, sa_ref, sb_ref)
    d_groups = []
    for gi, w in enumerate(POOL_WINDOWS):
        sl = slice(gi * POOL_GC, (gi + 1) * POOL_GC)
        cnt = jnp.minimum(pos1, w).astype(F32)
        d_groups.append(sums[gi][HALO:r, sl] / cnt - u[:, sl])
    pool_ref[...] = _pool_project(d_groups, wp_ref, ps_ref[...]).astype(BF16)
    ext_ref[16:HALO, :] = ext_ref[tm + 16:r, :]

    qn, kn = _qk_norm(zq_ref[:, 0:Q_W], zq_ref[:, Q_W:Q_W + KV_W], qg_ref[...], kg_ref[...], bd_ref[...])
    v = zq_ref[:, Q_W + KV_W:]
    q_ref[...] = qn.astype(BF16)
    k_ref[...] = kn.astype(BF16)
    vt_ref[...] = jnp.transpose(v).astype(BF16)

    @pl.when(j == n_j - 1)
    def _():
        utail_ref[...] = u[tm - 16:, :]
        ktail_ref[...] = kn[tm - WINDOW:, :]
        vtail_ref[...] = v[tm - WINDOW:, :]


def _proj_pool_combine_kernel(x1_ref, gath_ref, route_ref, *rest, tm, n_j):
    x2_ref = rest[7]
    _combine_kernel(x1_ref, gath_ref, route_ref, x2_ref)
    _proj_pool_kernel(x2_ref, *rest[:7], *rest[8:], tm=tm, n_j=n_j)


def _proj_pool_prompt(l, x_in, n_seq, seq, g_attn, w_in, qg, kg, bd, wp, ps):
    tm = TM_PROJ
    n_j = seq // tm
    t = n_seq * seq
    row = lambda b, j: (b * n_j + j, 0)
    lay = lambda b, j: (l, 0, 0)
    fused = isinstance(x_in, tuple)
    if fused:
        kern = _proj_pool_combine_kernel
        x_args = list(x_in)
        x_specs = [pl.BlockSpec((tm, D_MODEL), row),
                   pl.BlockSpec((2, tm, D_MODEL // 2), lambda b, j: (0, b * n_j + j, 0)),
                   pl.BlockSpec((ROUTE_FIELDS, tm), lambda b, j: (0, b * n_j + j))]
        x_out_specs = [pl.BlockSpec((tm, D_MODEL), row)]
        x_out_shape = [jax.ShapeDtypeStruct((t, D_MODEL), F32)]
    else:
        kern = _proj_pool_kernel
        x_args = [x_in]
        x_specs = [pl.BlockSpec((tm, D_MODEL), row)]
        x_out_specs, x_out_shape = [], []
    return pl.pallas_call(
        functools.partial(kern, tm=tm, n_j=n_j),
        grid=(n_seq, n_j),
        in_specs=x_specs + [
            pl.BlockSpec((None, 1, D_MODEL), lay),
            pl.BlockSpec((None, D_MODEL, D_IN), lay),
            pl.BlockSpec((None, 1, Q_W), lay),
            pl.BlockSpec((None, 1, KV_W), lay),
            pl.BlockSpec((256, 256), lambda b, j: (0, 0)),
            pl.BlockSpec((None, 2, 256, 256), lambda b, j: (l, 0, 0, 0)),
            pl.BlockSpec((None, 1, POOL_W), lay),
        ],
        out_specs=x_out_specs + [
            pl.BlockSpec((tm, POOL_W), row),
            pl.BlockSpec((tm, Q_W), row),
            pl.BlockSpec((tm, KV_W), row),
            pl.BlockSpec((KV_W, tm), lambda b, j: (0, b * n_j + j)),
            pl.BlockSpec((None, 16, POOL_W), lambda b, j: (b, 0, 0)),
            pl.BlockSpec((None, WINDOW, KV_W), lambda b, j: (b, 0, 0)),
            pl.BlockSpec((None, WINDOW, KV_W), lambda b, j: (b, 0, 0)),
        ],
        out_shape=x_out_shape + [
            jax.ShapeDtypeStruct((t, POOL_W), BF16),
            jax.ShapeDtypeStruct((t, Q_W), BF16),
            jax.ShapeDtypeStruct((t, KV_W), BF16),
            jax.ShapeDtypeStruct((KV_W, t), BF16),
            jax.ShapeDtypeStruct((n_seq, 16, POOL_W), F32),
            jax.ShapeDtypeStruct((n_seq, WINDOW, KV_W), F32),
            jax.ShapeDtypeStruct((n_seq, WINDOW, KV_W), F32),
        ],
        scratch_shapes=[pltpu.VMEM((tm + HALO, POOL_W), F32)] * 3 + [pltpu.VMEM((tm, Q_W + 2 * KV_W), F32)],
        compiler_params=pltpu.CompilerParams(
            dimension_semantics=("arbitrary", "arbitrary"), vmem_limit_bytes=VMEM_LIMIT),
        name="proj_pool_prompt",
    )(*x_args, g_attn, w_in, qg, kg, bd, wp, ps)


def _attn_kernel(sink_ref, q_ref, kp_ref, kc_ref, vtp_ref, vtc_ref, bias_ref, o_ref, s_ref):
    j = pl.program_id(1)
    kk_all = jnp.concatenate([kp_ref[...], kc_ref[...]], axis=0)
    vt_all = jnp.concatenate([vtp_ref[...], vtc_ref[...]], axis=1)
    from_prev = (lax.broadcasted_iota(jnp.int32, (WINDOW, WINDOW), 0)
                 > lax.broadcasted_iota(jnp.int32, (WINDOW, WINDOW), 1))
    units = [(blk, kv) for blk in range(ATTN_QB) for kv in range(N_KV_HEADS)]

    def scores(n):
        blk, kv = units[n]
        q = q_ref[blk * WINDOW:(blk + 1) * WINDOW, :]
        kk = kk_all[blk * WINDOW:(blk + 2) * WINDOW, kv * HEAD_DIM:(kv + 1) * HEAD_DIM]
        heads = range(kv * GQA_GROUP, (kv + 1) * GQA_GROUP)
        q_rows = jnp.concatenate([q[:, h * HEAD_DIM:(h + 1) * HEAD_DIM] for h in heads], axis=0)
        s_ref[n % 2] = lax.dot_general(kk, q_rows, (((1,), (1,)), ((), ())), preferred_element_type=F32)

    scores(0)
    outs = []
    for n, (blk, kv) in enumerate(units):
        if n + 1 < len(units):
            scores(n + 1)
        vt_kv = vt_all[kv * HEAD_DIM:(kv + 1) * HEAD_DIM, blk * WINDOW:(blk + 2) * WINDOW]
        variant = jnp.minimum(j, 1) if blk == 0 else 1
        for g in range(GQA_GROUP):
            h = kv * GQA_GROUP + g
            s = jnp.where(from_prev, s_ref[n % 2, 0:WINDOW, g * WINDOW:(g + 1) * WINDOW],
                          s_ref[n % 2, WINDOW:, g * WINDOW:(g + 1) * WINDOW]) + bias_ref[variant, h]
            sink = sink_ref[h]
            m = jnp.maximum(jnp.max(s, axis=0, keepdims=True), sink)
            p = jnp.exp(s - m)
            denom = jnp.sum(p, axis=0, keepdims=True) + jnp.exp(sink - m)
            p_keys = jnp.concatenate([jnp.where(from_prev, p, 0.0), jnp.where(from_prev, 0.0, p)], axis=0)
            o_t = jnp.dot(vt_kv, p_keys.astype(BF16), preferred_element_type=F32)
            outs.append(o_t / denom)
        if kv == N_KV_HEADS - 1:
            o_ref[blk * WINDOW:(blk + 1) * WINDOW, :] = jnp.transpose(jnp.concatenate(outs, axis=0)).astype(BF16)
            outs = []


def _attn_prompt(q, k, vt, bias_t, sinks, n_seq, seq):
    tq = ATTN_QB * WINDOW
    nj = seq // tq
    t = n_seq * seq
    cur = lambda b, j: (b * nj + j, 0)
    prev = lambda b, j: (jnp.maximum((b * nj + j) * ATTN_QB - 1, 0), 0)
    cur_t = lambda b, j: (0, b * nj + j)
    prev_t = lambda b, j: (0, jnp.maximum((b * nj + j) * ATTN_QB - 1, 0))
    return pl.pallas_call(
        _attn_kernel,
        grid=(n_seq, nj),
        in_specs=[
            pl.BlockSpec(memory_space=pltpu.SMEM),
            pl.BlockSpec((tq, Q_W), cur),
            pl.BlockSpec((WINDOW, KV_W), prev),
            pl.BlockSpec((tq, KV_W), cur),
            pl.BlockSpec((KV_W, WINDOW), prev_t),
            pl.BlockSpec((KV_W, tq), cur_t),
            pl.BlockSpec((2, N_HEADS, WINDOW, WINDOW), lambda b, j: (0, 0, 0, 0)),
        ],
        out_specs=pl.BlockSpec((tq, Q_W), cur),
        out_shape=jax.ShapeDtypeStruct((t, Q_W), BF16),
        scratch_shapes=[pltpu.VMEM((2, 2 * WINDOW, GQA_GROUP * WINDOW), F32)],
        compiler_params=pltpu.CompilerParams(
            dimension_semantics=("arbitrary", "arbitrary"), vmem_limit_bytes=VMEM_LIMIT),
        name="attn_prompt",
    )(sinks, q, k, k, vt, vt, bias_t)


def _prompt_bias_t():
    r = jnp.arange(WINDOW, dtype=jnp.int32)[None, :]
    c = jnp.arange(WINDOW, dtype=jnp.int32)[:, None]
    from_prev = c > r
    dist = r - c + jnp.where(from_prev, WINDOW, 0)
    slopes = jnp.exp2(-8.0 * jnp.arange(1, N_HEADS + 1, dtype=F32) / N_HEADS)
    later = -slopes[:, None, None] * dist.astype(F32)[None]
    first = jnp.where(from_prev[None], -jnp.inf, later)
    return jnp.stack([first, later])


def _sample_kernel(x_ref, g_ref, win_ref, qg_ref, kg_ref, bd_ref, wp_ref, ps_ref,
                   st_ref, ck_ref, cv_ref, sink_ref, bias_ref, perm_ref,
                   pool_ref, attn_ref, pst_ref, kc_ref, vc_ref, *, ns, pos0):
    u, qn, kn, v = _project(x_ref[...], g_ref[...], win_ref[...], qg_ref[...], kg_ref[...], bd_ref[...])
    pst_ref[:, 0:POOL_STATE - 1, :] = st_ref[:, 1:POOL_STATE, :]
    kc_ref[:, 0:WINDOW - 1, :] = ck_ref[:, 1:WINDOW, :]
    vc_ref[:, 0:WINDOW - 1, :] = cv_ref[:, 1:WINDOW, :]
    for n in range(ns):
        pst_ref[n, POOL_STATE - 1:POOL_STATE, :] = u[n:n + 1, :]
        kc_ref[n, WINDOW - 1:WINDOW, :] = kn[n:n + 1, :]
        vc_ref[n, WINDOW - 1:WINDOW, :] = v[n:n + 1, :]

    d_groups = []
    for gi, w in enumerate(POOL_WINDOWS):
        lo = gi * POOL_GC
        acc = u[:, lo:lo + POOL_GC]
        for back in range(1, w):
            acc = acc + st_ref[:, POOL_STATE - back, lo:lo + POOL_GC]
        d_groups.append(acc / float(min(pos0 + 1, w)) - u[:, lo:lo + POOL_GC])
    pool_ref[...] = _pool_project(d_groups, wp_ref, ps_ref[...]).astype(BF16)

    zeros = jnp.zeros((ns, HEAD_DIM), F32)
    stacked = []
    for h in range(N_HEADS):
        piece = qn[:, h * HEAD_DIM:(h + 1) * HEAD_DIM]
        pair = [piece, zeros] if h < GQA_GROUP else [zeros, piece]
        stacked.append(jnp.concatenate(pair, axis=-1))
    q_hn = jnp.concatenate(stacked, axis=0).astype(BF16)
    q_nh = jnp.dot(perm_ref[0], q_hn, preferred_element_type=F32).astype(BF16)

    keys = kc_ref[...].reshape(ns * WINDOW, KV_W).astype(BF16)
    vals = vc_ref[...].reshape(ns * WINDOW, KV_W).astype(BF16)
    s_all = lax.dot_general(q_nh, keys, (((1,), (1,)), ((), ())), preferred_element_type=F32)
    sink = sink_ref[...]
    bias = bias_ref[...]
    zero_blk = jnp.zeros((N_HEADS, WINDOW), F32)
    p_rows = []
    for n in range(ns):
        s = s_all[n * N_HEADS:(n + 1) * N_HEADS, n * WINDOW:(n + 1) * WINDOW] + bias
        m = jnp.maximum(jnp.max(s, axis=-1, keepdims=True), sink)
        p = jnp.exp(s - m)
        denom = jnp.sum(p, axis=-1, keepdims=True) + jnp.exp(sink - m)
        p_rows.append(jnp.concatenate([zero_blk] * n + [p / denom] + [zero_blk] * (ns - 1 - n), axis=-1))
    p_blockdiag = jnp.concatenate(p_rows, axis=0).astype(BF16)
    o_nh = jnp.dot(p_blockdiag, vals, preferred_element_type=F32).astype(BF16)
    o_hn = jnp.dot(perm_ref[1], o_nh, preferred_element_type=F32)
    pieces = []
    for h in range(N_HEADS):
        kv = h // GQA_GROUP
        pieces.append(o_hn[h * ns:(h + 1) * ns, kv * HEAD_DIM:(kv + 1) * HEAD_DIM])
    attn_ref[...] = jnp.concatenate(pieces, axis=-1).astype(BF16)


def _sample_mixer(l, depth, xs, g_attn, w_in, qg, kg, bd, wp, ps, state, ck, cv, sink8, bias_s, pos0):
    n = xs.shape[0]
    ns = 32
    row = lambda i: (i, 0)
    lay = lambda i: (l, 0, 0)
    src = jnp.arange(ns * N_HEADS)
    perm = (((src % N_HEADS) * ns + src // N_HEADS)[:, None] == src[None, :]).astype(BF16)
    perms = jnp.stack([perm, perm.T])
    return pl.pallas_call(
        functools.partial(_sample_kernel, ns=ns, pos0=pos0),
        grid=(n // ns,),
        input_output_aliases={8: 2, 9: 3, 10: 4},
        in_specs=[
            pl.BlockSpec((ns, D_MODEL), row),
            pl.BlockSpec((None, 1, D_MODEL), lay),
            pl.BlockSpec((None, D_MODEL, D_IN), lay),
            pl.BlockSpec((None, 1, Q_W), lay),
            pl.BlockSpec((None, 1, KV_W), lay),
            pl.BlockSpec((256, 256), lambda i: (0, 0)),
            pl.BlockSpec((None, 2, 256, 256), lambda i: (l, 0, 0, 0)),
            pl.BlockSpec((None, 1, POOL_W), lay),
            pl.BlockSpec((None, ns, POOL_STATE, POOL_W), lambda i: (l, i, 0, 0)),
            pl.BlockSpec((None, ns, WINDOW, KV_W), lambda i: (l, i, 0, 0)),
            pl.BlockSpec((None, ns, WINDOW, KV_W), lambda i: (l, i, 0, 0)),
            pl.BlockSpec((N_HEADS, 1), lambda i: (0, 0)),
            pl.BlockSpec((N_HEADS, WINDOW), lambda i: (0, 0)),
            pl.BlockSpec((2, ns * N_HEADS, ns * N_HEADS), lambda i: (0, 0, 0)),
        ],
        out_specs=[
            pl.BlockSpec((ns, POOL_W), row),
            pl.BlockSpec((ns, Q_W), row),
            pl.BlockSpec((None, ns, POOL_STATE, POOL_W), lambda i: (l, i, 0, 0)),
            pl.BlockSpec((None, ns, WINDOW, KV_W), lambda i: (l, i, 0, 0)),
            pl.BlockSpec((None, ns, WINDOW, KV_W), lambda i: (l, i, 0, 0)),
        ],
        out_shape=[
            jax.ShapeDtypeStruct((n, POOL_W), BF16),
            jax.ShapeDtypeStruct((n, Q_W), BF16),
            jax.ShapeDtypeStruct((depth, n, POOL_STATE, POOL_W), F32),
            jax.ShapeDtypeStruct((depth, n, WINDOW, KV_W), F32),
            jax.ShapeDtypeStruct((depth, n, WINDOW, KV_W), F32),
        ],
        compiler_params=pltpu.CompilerParams(
            dimension_semantics=("arbitrary",), vmem_limit_bytes=VMEM_LIMIT),
        name="sample_mixer",
    )(xs, g_attn, w_in, qg, kg, bd, wp, ps, state, ck, cv, sink8, bias_s, perms)


def _merge_router_kernel(pool_ref, attn_ref, x_ref, wout_ref, g_ref, wr_ref, br_ref, utri_ref, cin_ref,
                         x1_ref, h2_ref, route_t_ref, cnt_ref, y_ref, lg_ref):
    i = pl.program_id(0)

    @pl.when(i == 0)
    def _():
        cnt_ref[...] = cin_ref[...]

    tm = x_ref.shape[0]
    rc = tm // MERGE_CHUNKS
    chunks = [slice(ci * rc, (ci + 1) * rc) for ci in range(MERGE_CHUNKS)]
    for rows in chunks:
        y_ref[rows, :] = (jnp.dot(pool_ref[rows, :], wout_ref[0:POOL_W, :], preferred_element_type=F32)
                          + jnp.dot(attn_ref[rows, :], wout_ref[POOL_W:, :], preferred_element_type=F32))
    for rows in chunks:
        x1 = x_ref[rows, :] + y_ref[rows, :]
        x1_ref[rows, :] = x1
        h2 = _rms_bf16(x1, g_ref[...])
        h2_ref[rows, :] = _pack_bf16_pairs(h2)
        lg_ref[rows, :] = jnp.dot(h2, wr_ref[...], preferred_element_type=F32) + br_ref[...]
    logits = lg_ref[...]

    lt = jnp.transpose(logits)
    sub = lax.broadcasted_iota(jnp.int32, (EXPERTS_PER_GROUP, tm), 0)
    neg = -jnp.inf
    big = jnp.int32(EXPERTS_PER_GROUP)
    gl = jnp.where(sub < N_EXPERT_GROUPS, lt[GROUP_LANE0:GROUP_LANE0 + EXPERTS_PER_GROUP, :], neg)
    gmax = jnp.max(gl, axis=0, keepdims=True)
    grp = jnp.min(jnp.where(gl == gmax, sub, big), axis=0, keepdims=True)
    g_w = 1.0 / jnp.sum(jnp.exp(gl - gmax), axis=0, keepdims=True)
    el = lt[(N_EXPERT_GROUPS - 1) * EXPERTS_PER_GROUP:N_EXPERT_GROUPS * EXPERTS_PER_GROUP, :]
    for gi in range(N_EXPERT_GROUPS - 2, -1, -1):
        el = jnp.where(grp == gi, lt[gi * EXPERTS_PER_GROUP:(gi + 1) * EXPERTS_PER_GROUP, :], el)
    v1 = jnp.max(el, axis=0, keepdims=True)
    i1 = jnp.min(jnp.where(el == v1, sub, big), axis=0, keepdims=True)
    el2 = jnp.where(sub == i1, neg, el)
    v2 = jnp.max(el2, axis=0, keepdims=True)
    i2 = jnp.min(jnp.where(el2 == v2, sub, big), axis=0, keepdims=True)
    e21 = jnp.exp(v2 - v1)
    w1 = g_w / (1.0 + e21)
    w2 = g_w * e21 / (1.0 + e21)
    e1 = grp * EXPERTS_PER_GROUP + i1
    e2 = grp * EXPERTS_PER_GROUP + i2

    esub = lax.broadcasted_iota(jnp.int32, (N_EXPERTS, tm), 0)
    oh1 = esub == e1
    oh2 = esub == e2
    c = jnp.where(oh1 | oh2, 1.0, 0.0)
    prefix = jnp.dot(c.astype(BF16), utri_ref[...], preferred_element_type=F32) + cnt_ref[...]
    r1 = jnp.sum(jnp.where(oh1, prefix, 0.0), axis=0, keepdims=True)
    r2 = jnp.sum(jnp.where(oh2, prefix, 0.0), axis=0, keepdims=True)
    cnt_ref[...] = cnt_ref[...] + jnp.sum(c, axis=1, keepdims=True)

    fields = jnp.zeros((ROUTE_FIELDS, tm), F32)
    for idx, val in enumerate((e1.astype(F32), e2.astype(F32), w1, w2, r1, r2)):
        fields = jnp.where(sub == idx, val, fields)
    route_t_ref[...] = fields


def _merge_router(l, pool, attn, x2d, w_out, g_ffn, wr, br, cnt_in, tm):
    t = x2d.shape[0]
    utri = (jnp.arange(tm)[:, None] < jnp.arange(tm)[None, :]).astype(BF16)
    row = lambda i: (i, 0)
    lay = lambda i: (l, 0, 0)
    return pl.pallas_call(
        _merge_router_kernel,
        grid=(t // tm,),
        in_specs=[
            pl.BlockSpec((tm, POOL_W), row),
            pl.BlockSpec((tm, Q_W), row),
            pl.BlockSpec((tm, D_MODEL), row),
            pl.BlockSpec((None, D_MODEL, D_MODEL), lay),
            pl.BlockSpec((None, 1, D_MODEL), lay),
            pl.BlockSpec((None, D_MODEL, LANES), lay),
            pl.BlockSpec((None, 1, LANES), lay),
            pl.BlockSpec((tm, tm), lambda i: (0, 0)),
            pl.BlockSpec((N_EXPERTS, 1), lambda i: (0, 0)),
        ],
        out_specs=[
            pl.BlockSpec((tm, D_MODEL), row),
            pl.BlockSpec((tm, D_MODEL // 2), row),
            pl.BlockSpec((ROUTE_FIELDS, tm), lambda i: (0, i)),
            pl.BlockSpec((N_EXPERTS, 1), lambda i: (0, 0)),
        ],
        out_shape=[
            jax.ShapeDtypeStruct((t, D_MODEL), F32),
            jax.ShapeDtypeStruct((t, D_MODEL // 2), jnp.int32),
            jax.ShapeDtypeStruct((ROUTE_FIELDS, t), F32),
            jax.ShapeDtypeStruct((N_EXPERTS, 1), F32),
        ],
        scratch_shapes=[pltpu.VMEM((tm, D_MODEL), F32), pltpu.VMEM((tm, LANES), F32)],
        compiler_params=pltpu.CompilerParams(
            dimension_semantics=("arbitrary",), vmem_limit_bytes=VMEM_LIMIT),
        name="merge_router",
    )(pool, attn, x2d, w_out, g_ffn, wr, br, utri, cnt_in)


def _moe_kernel(be_ref, rv_ref, nx_ref, sl_ref, first_ref, xd_ref, wg_hbm, wu_hbm, wd_hbm, yd_ref,
                wg_f, wu_f, wd_f, wg_s, wu_s, wd_s, sem, *, layer):
    step = pl.program_id(0)

    def weight_copies(e, s):
        return [pltpu.make_async_copy(w_hbm.at[layer, e], w_f.at[s], sem.at[s, n])
                for n, (w_hbm, w_f) in enumerate(((wg_hbm, wg_f), (wu_hbm, wu_f), (wd_hbm, wd_f)))]

    @pl.when(step == 0)
    def _():
        for s in range(WEIGHT_SLOTS - 1):
            @pl.when(first_ref[s] >= 0)
            def _():
                for c in weight_copies(first_ref[s], s):
                    c.start()

    def enter_expert(i):
        expert, slot = be_ref[i], sl_ref[i]

        @pl.when((i == 0) | (expert != be_ref[jnp.maximum(i - 1, 0)]))
        def _():
            for c in weight_copies(expert, slot):
                c.wait()

            @pl.when(nx_ref[i] >= 0)
            def _():
                for c in weight_copies(nx_ref[i], lax.rem(slot + WEIGHT_SLOTS - 1, WEIGHT_SLOTS)):
                    c.start(priority=1)

            wg_s[...] = wg_f[slot].astype(BF16)
            wu_s[...] = wu_f[slot].astype(BF16)
            wd_s[...] = wd_f[slot].astype(BF16)

    def experts_on(row0, n_rows, rows_valid):
        rows = pl.ds(row0, n_rows)
        row = lax.broadcasted_iota(jnp.int32, (n_rows, D_MODEL // 2), 0)
        x = _unpack_bf16_pairs(jnp.where(row < rows_valid, xd_ref[rows, :], 0)).astype(BF16)
        gate = jnp.dot(x, wg_s[...], preferred_element_type=F32)
        up = jnp.dot(x, wu_s[...], preferred_element_type=F32)
        act = (gate * jax.nn.sigmoid(gate) * up).astype(BF16)
        y = jnp.dot(act, wd_s[...], preferred_element_type=F32)
        yd_ref[rows, :] = _pack_bf16_pairs(y.astype(BF16))

    def single_block(i, row0):
        enter_expert(i)

        @pl.when(rv_ref[i] > 0)
        def _():
            experts_on(row0, MOE_BM, rv_ref[i])

        @pl.when(rv_ref[i] <= 0)
        def _():
            yd_ref[pl.ds(row0, MOE_BM), :] = jnp.zeros((MOE_BM, D_MODEL // 2), jnp.int32)

    for pair in range(MOE_STEP_BLOCKS // 2):
        ia = step * MOE_STEP_BLOCKS + 2 * pair
        ib = ia + 1
        row0 = 2 * pair * MOE_BM
        same = (be_ref[ib] == be_ref[ia]) & (rv_ref[ib] > 0)

        @pl.when(same)
        def _():
            enter_expert(ia)
            experts_on(row0, 2 * MOE_BM, MOE_BM + rv_ref[ib])

        @pl.when(jnp.logical_not(same))
        def _():
            single_block(ia, row0)
            single_block(ib, row0 + MOE_BM)


def _moe_experts(l, block_e, rows_valid, next_e, slot, first_e, xd, w_gate, w_up, w_down):
    n_blocks = xd.shape[0] // MOE_BM
    step_rows = MOE_STEP_BLOCKS * MOE_BM
    row = lambda i, *_: (i, 0)
    return pl.pallas_call(
        functools.partial(_moe_kernel, layer=l),
        grid_spec=pltpu.PrefetchScalarGridSpec(
            num_scalar_prefetch=5,
            grid=(n_blocks // MOE_STEP_BLOCKS,),
            in_specs=[
                pl.BlockSpec((step_rows, D_MODEL // 2), row),
                pl.BlockSpec(memory_space=pl.ANY),
                pl.BlockSpec(memory_space=pl.ANY),
                pl.BlockSpec(memory_space=pl.ANY),
            ],
            out_specs=pl.BlockSpec((step_rows, D_MODEL // 2), row),
            scratch_shapes=[
                pltpu.VMEM((WEIGHT_SLOTS, D_MODEL, EXPERT_FF), F32),
                pltpu.VMEM((WEIGHT_SLOTS, D_MODEL, EXPERT_FF), F32),
                pltpu.VMEM((WEIGHT_SLOTS, EXPERT_FF, D_MODEL), F32),
                pltpu.VMEM((D_MODEL, EXPERT_FF), BF16),
                pltpu.VMEM((D_MODEL, EXPERT_FF), BF16),
                pltpu.VMEM((EXPERT_FF, D_MODEL), BF16),
                pltpu.SemaphoreType.DMA((WEIGHT_SLOTS, 3)),
            ],
        ),
        out_shape=jax.ShapeDtypeStruct((n_blocks * MOE_BM, D_MODEL // 2), jnp.int32),
        compiler_params=pltpu.CompilerParams(
            dimension_semantics=("arbitrary",), vmem_limit_bytes=VMEM_LIMIT),
        name="moe_experts",
    )(block_e, rows_valid, next_e, slot, first_e, xd, w_gate, w_up, w_down)


def _sc_worker_id():
    return lax.axis_index("s") * SC_CORES + lax.axis_index("c")


def _sc_dispatch(hp, hs, dest_p, dest_s, n_rows):
    tp, width = hp.shape
    per_w = tp // SC_WORKERS
    n_ch = per_w // DISP_CH
    n_sw = hs.shape[0] // SAMPLE_CH
    mesh = plsc.VectorSubcoreMesh(core_axis_name="c", subcore_axis_name="s")

    @functools.partial(
        pl.kernel, mesh=mesh,
        out_type=jax.ShapeDtypeStruct((n_rows, width), jnp.int32),
        scratch_types=[
            pltpu.VMEM((2, n_ch, DISP_CH), jnp.int32),
            pltpu.VMEM((2, 1, SAMPLE_CH), jnp.int32),
            pltpu.VMEM((2, DISP_CH, width), jnp.int32),
            pltpu.SemaphoreType.DMA((2,)),
            pltpu.SemaphoreType.DMA((2,)),
        ],
        name="sc_dispatch",
    )
    def k(hp_hbm, hs_hbm, dp_hbm, ds_hbm, xd_hbm, idx_v, idxs_v, bufs, rsem, wsem):
        wid = _sc_worker_id()
        base = wid * per_w
        for kk in range(2):
            pltpu.sync_copy(dp_hbm.at[kk, wid], idx_v.at[kk])
        reads = [pltpu.make_async_copy(hp_hbm.at[pl.ds(base + j * DISP_CH, DISP_CH)],
                                       bufs.at[j % 2], rsem.at[j % 2]) for j in range(n_ch)]
        reads[0].start()
        for j in range(n_ch):
            if j + 1 < n_ch:
                reads[j + 1].start()
            reads[j].wait()
            writes = [pltpu.make_async_copy(bufs.at[j % 2], xd_hbm.at[idx_v.at[kk, j]], wsem.at[kk])
                      for kk in range(2)]
            for w in writes:
                w.start()
            for w in writes:
                w.wait()

        @pl.when(wid < n_sw)
        def _():
            rows = bufs.at[0, pl.ds(0, SAMPLE_CH)]
            for kk in range(2):
                pltpu.sync_copy(ds_hbm.at[kk, wid], idxs_v.at[kk])
            pltpu.sync_copy(hs_hbm.at[pl.ds(wid * SAMPLE_CH, SAMPLE_CH)], rows)
            for kk in range(2):
                pltpu.sync_copy(rows, xd_hbm.at[idxs_v.at[kk, 0]])

    return k(hp, hs, dest_p, dest_s)


def _sc_combine_gather(yd, dest_p, dest_s, tp, ts):
    width = yd.shape[1]
    per_w = tp // SC_WORKERS
    n_ch = per_w // COMB_CH
    n_sw = ts // SAMPLE_CH
    mesh = plsc.VectorSubcoreMesh(core_axis_name="c", subcore_axis_name="s")

    @functools.partial(
        pl.kernel, mesh=mesh,
        out_type=jax.ShapeDtypeStruct((2, tp + ts, width), yd.dtype),
        scratch_types=[
            pltpu.VMEM((2, n_ch, COMB_CH), jnp.int32),
            pltpu.VMEM((2, 1, SAMPLE_CH), jnp.int32),
            pltpu.VMEM((2, COMB_CH, width), yd.dtype),
            pltpu.SemaphoreType.DMA((2,)),
            pltpu.SemaphoreType.DMA((2,)),
        ],
        name="sc_combine_gather",
    )
    def k(yd_hbm, dp_hbm, ds_hbm, g_hbm, idx_v, idxs_v, bufs, gsem, wsem):
        wid = _sc_worker_id()
        base = wid * per_w
        for kk in range(2):
            pltpu.sync_copy(dp_hbm.at[kk, wid], idx_v.at[kk])
        items = [(kk, j) for kk in range(2) for j in range(n_ch)]
        gathers = [pltpu.make_async_copy(yd_hbm.at[idx_v.at[kk, j]], bufs.at[n % 2], gsem.at[n % 2])
                   for n, (kk, j) in enumerate(items)]
        gathers[0].start()
        for n, (kk, j) in enumerate(items):
            if n + 1 < len(items):
                gathers[n + 1].start()
            gathers[n].wait()
            w = pltpu.make_async_copy(bufs.at[n % 2], g_hbm.at[kk, pl.ds(base + j * COMB_CH, COMB_CH)],
                                      wsem.at[n % 2])
            w.start()
            w.wait()

        @pl.when(wid < n_sw)
        def _():
            for kk in range(2):
                pltpu.sync_copy(ds_hbm.at[kk, wid], idxs_v.at[kk])
            for kk in range(2):
                rows = bufs.at[kk, pl.ds(0, SAMPLE_CH)]
                pltpu.sync_copy(yd_hbm.at[idxs_v.at[kk, 0]], rows)
                pltpu.sync_copy(rows, g_hbm.at[kk, pl.ds(tp + wid * SAMPLE_CH, SAMPLE_CH)])

    return k(yd, dest_p, dest_s)


def _combine_kernel(x1_ref, g_ref, route_t_ref, x2_ref):
    fields = route_t_ref[...]
    tm = fields.shape[1]
    cols = jnp.transpose(jnp.concatenate([fields, jnp.zeros((LANES - ROUTE_FIELDS, tm), F32)], axis=0))
    w1 = cols[:, 2:3]
    w2 = cols[:, 3:4]
    x2_ref[...] = x1_ref[...] + _unpack_bf16_pairs(g_ref[0]) * w1 + _unpack_bf16_pairs(g_ref[1]) * w2


def _combine(x1, g, route_t, row0, tm):
    t = x1.shape[0]
    blk0 = row0 // tm
    row = lambda i: (i, 0)
    return pl.pallas_call(
        _combine_kernel,
        grid=(t // tm,),
        in_specs=[
            pl.BlockSpec((tm, D_MODEL), row),
            pl.BlockSpec((2, tm, D_MODEL // 2), lambda i: (0, blk0 + i, 0)),
            pl.BlockSpec((ROUTE_FIELDS, tm), lambda i: (0, i)),
        ],
        out_specs=pl.BlockSpec((tm, D_MODEL), row),
        out_shape=jax.ShapeDtypeStruct((t, D_MODEL), F32),
        compiler_params=pltpu.CompilerParams(
            dimension_semantics=("arbitrary",), vmem_limit_bytes=VMEM_LIMIT),
        name="combine",
    )(x1, g, route_t)


def _dest_layout(dest, workers, chunk):
    t = dest.shape[1]
    return dest.reshape(2, workers, t // (workers * chunk), chunk)


def _hier_moe(l, h2p, h2s, route_tp, route_ts, counts, w_gate, w_up, w_down):
    tp, ts = h2p.shape[0], h2s.shape[0]
    n_assign = 2 * (tp + ts)
    n_blocks = -(-n_assign // MOE_BM) + N_EXPERTS
    n_blocks = -(-n_blocks // MOE_STEP_BLOCKS) * MOE_STEP_BLOCKS
    pcounts = (counts + MOE_BM - 1) // MOE_BM * MOE_BM
    pends = jnp.cumsum(pcounts)
    poffsets = pends - pcounts
    starts = jnp.arange(n_blocks, dtype=jnp.int32) * MOE_BM
    block_e = jnp.minimum(jnp.sum((pends[None, :] <= starts[:, None]).astype(jnp.int32), axis=1),
                          N_EXPERTS - 1)
    experts = jnp.arange(N_EXPERTS, dtype=jnp.int32)

    def lookup(table, idx):
        return jnp.sum(jnp.where(idx[..., None] == experts, table, 0), axis=-1)

    rows_valid = jnp.clip(lookup(poffsets + counts, block_e) - starts, 0, MOE_BM).astype(jnp.int32)
    used = counts > 0
    last_e = jnp.max(jnp.where(used, jnp.arange(N_EXPERTS, dtype=jnp.int32), 0))
    block_e = jnp.where(rows_valid > 0, block_e, last_e).astype(jnp.int32)
    place = jnp.cumsum(used.astype(jnp.int32)) - 1
    by_place = jnp.sum(jnp.where(used[None, :] & (place[None, :] == experts[:, None]), experts[None, :], 0),
                       axis=1)
    n_used = jnp.sum(used.astype(jnp.int32))

    def at_place(p):
        return jnp.where(p < n_used, lookup(by_place, jnp.minimum(p, N_EXPERTS - 1)), -1).astype(jnp.int32)

    ahead_of = at_place(place + (WEIGHT_SLOTS - 1))
    next_e = lookup(ahead_of, block_e)
    slot = lookup(place % WEIGHT_SLOTS, block_e)
    first_e = at_place(jnp.arange(WEIGHT_SLOTS - 1, dtype=jnp.int32))

    def dest_of(route_t):
        return lookup(poffsets, route_t[0:2].astype(jnp.int32)) + route_t[4:6].astype(jnp.int32)

    dest_p, dest_s = dest_of(route_tp), dest_of(route_ts)
    n_sw = ts // SAMPLE_CH
    xd = _sc_dispatch(h2p, h2s, _dest_layout(dest_p, SC_WORKERS, DISP_CH),
                      _dest_layout(dest_s, n_sw, SAMPLE_CH), n_blocks * MOE_BM)
    yd = _moe_experts(l, block_e, rows_valid, next_e, slot, first_e, xd, w_gate, w_up, w_down)
    return _sc_combine_gather(yd, _dest_layout(dest_p, SC_WORKERS, COMB_CH),
                              _dest_layout(dest_s, n_sw, SAMPLE_CH), tp, ts)


def kernel(x_prompt, x_sample, state_pool, cache_k_win, cache_v_win, norm_attn_g, w_in, pool_w, pool_scale, q_norm_g, k_norm_g, attn_sinks, w_out, norm_ffn_g, router_group_w, router_group_b, router_expert_w, router_expert_b, w_gate, w_up, w_down):
    n_p, t_p, d = x_prompt.shape
    n_s, t_s, _ = x_sample.shape
    depth = w_in.shape[0]
    lw_s = cache_k_win.shape[2]
    assert t_s == 1 and lw_s == WINDOW and d == D_MODEL
    assert t_p % TM_PROJ == 0 and t_p >= WINDOW
    past_len = 16384

    seg = jnp.arange(256) // HEAD_DIM
    bd = jnp.where(seg[:, None] == seg[None, :], 1.0 / HEAD_DIM, 0.0).astype(BF16)
    slopes = jnp.exp2(-8.0 * jnp.arange(1, N_HEADS + 1, dtype=F32) / N_HEADS)
    bias_p = _prompt_bias_t()
    dist_s = (WINDOW - 1) - jnp.arange(WINDOW, dtype=F32)
    bias_s = -slopes[:, None] * dist_s[None, :]

    wp = jnp.zeros((depth, 2, 256, 256), F32)
    for p in range(2):
        wp = wp.at[:, p, :POOL_GC, :POOL_GC].set(pool_w[:, 2 * p])
        wp = wp.at[:, p, POOL_GC:, POOL_GC:].set(pool_w[:, 2 * p + 1])
    assert GROUP_LANE0 == N_EXPERTS
    lane_pad = LANES - N_EXPERTS - N_EXPERT_GROUPS
    wr = jnp.concatenate([router_expert_w, router_group_w, jnp.zeros((depth, D_MODEL, lane_pad), F32)], axis=-1)
    br = jnp.concatenate([router_expert_b, router_group_b, jnp.zeros((depth, lane_pad), F32)],
                         axis=-1).reshape(depth, 1, LANES)
    lp = dict(
        w_in=w_in.astype(BF16),
        w_out=w_out.astype(BF16),
        g_attn=norm_attn_g.reshape(depth, 1, D_MODEL),
        g_ffn=norm_ffn_g.reshape(depth, 1, D_MODEL),
        qg=(jnp.tile(q_norm_g, (1, N_HEADS)) * ATTN_SCALE).reshape(depth, 1, Q_W),
        kg=jnp.tile(k_norm_g, (1, N_KV_HEADS)).reshape(depth, 1, KV_W),
        wp=wp.astype(BF16),
        ps=pool_scale.reshape(depth, 1, POOL_W),
        wr=wr.astype(BF16),
        br=br,
        state=state_pool,
        ck=cache_k_win.reshape(depth, n_s, lw_s, KV_W),
        cv=cache_v_win.reshape(depth, n_s, lw_s, KV_W),
    )

    xp = x_prompt.reshape(n_p * t_p, D_MODEL)
    xs = x_sample.reshape(n_s, D_MODEL)
    lw_p = min(WINDOW, t_p)
    pool_p, kp_new, vp_new = [], [], []
    sample_state = [lp["state"], lp["ck"], lp["cv"]]
    zero_cnt = jnp.zeros((N_EXPERTS, 1), F32)
    pending = None
    for l in range(depth):
        sinks = attn_sinks[l]
        outs = _proj_pool_prompt(
            l, xp if pending is None else pending, n_p, t_p,
            lp["g_attn"], lp["w_in"], lp["qg"], lp["kg"], bd, lp["wp"], lp["ps"])
        if pending is not None:
            xp, outs = outs[0], outs[1:]
        pool_o, q, k, vt, utail, ktail, vtail = outs
        attn_o = _attn_prompt(q, k, vt, bias_p, sinks, n_p, t_p)
        x1p, h2p, route_tp, cnt_p = _merge_router(
            l, pool_o, attn_o, xp, lp["w_out"], lp["g_ffn"], lp["wr"], lp["br"], zero_cnt, TM_MERGE)
        pool_p.append(utail[:, 16 - POOL_STATE:, :])
        kp_new.append(ktail)
        vp_new.append(vtail)
        pool_so, attn_so, *sample_state = _sample_mixer(
            l, depth, xs, lp["g_attn"], lp["w_in"], lp["qg"], lp["kg"], bd, lp["wp"], lp["ps"],
            *sample_state, sinks.reshape(N_HEADS, 1), bias_s, past_len)
        x1s, h2s, route_ts, cnt_all = _merge_router(
            l, pool_so, attn_so, xs, lp["w_out"], lp["g_ffn"], lp["wr"], lp["br"], cnt_p, n_s)
        counts = cnt_all[:, 0].astype(jnp.int32)
        g = _hier_moe(l, h2p, h2s, route_tp, route_ts, counts, w_gate, w_up, w_down)
        xs = _combine(x1s, g, route_ts, n_p * t_p, n_s)
        pending = (x1p, g, route_tp)
    xp = _combine(*pending, 0, TM_MERGE)
    return (xp.reshape(n_p, t_p, D_MODEL), xs.reshape(n_s, t_s, D_MODEL),
            jnp.stack(pool_p),
            jnp.stack(kp_new).reshape(depth, n_p, lw_p, N_KV_HEADS, HEAD_DIM),
            jnp.stack(vp_new).reshape(depth, n_p, lw_p, N_KV_HEADS, HEAD_DIM),
            sample_state[0],
            sample_state[1].reshape(depth, n_s, lw_s, N_KV_HEADS, HEAD_DIM),
            sample_state[2].reshape(depth, n_s, lw_s, N_KV_HEADS, HEAD_DIM))
```

```python
import functools

import jax
import jax.numpy as jnp
from jax import lax
from jax.experimental import pallas as pl
from jax.experimental.pallas import tpu as pltpu
from jax.experimental.pallas import tpu_sc as plsc

D_MODEL = 1024
POOL_W = 512
POOL_WINDOWS = (2, 4, 8, 16)
POOL_GC = 128
POOL_STATE = 15
HEAD_DIM = 64
N_HEADS = 8
N_KV_HEADS = 2
GQA_GROUP = 4
Q_W = 512
KV_W = 128
D_IN = POOL_W + Q_W + 2 * KV_W
WINDOW = 128
ATTN_SCALE = HEAD_DIM ** -0.5
N_EXPERT_GROUPS = 4
EXPERTS_PER_GROUP = 8
N_EXPERTS = 32
EXPERT_FF = 512
EPS = 1e-6

LANES = 128
HALO = 32
TM_PROJ = 1024
TM_MERGE = 512
MERGE_CHUNKS = 2
ATTN_QB = 16
MOE_BM = 256
MOE_STEP_BLOCKS = 4
WEIGHT_SLOTS = 3
GROUP_LANE0 = 32
ROUTE_FIELDS = 8
SC_CORES = 2
SC_SUBCORES = 16
SC_WORKERS = SC_CORES * SC_SUBCORES
DISP_CH = 64
COMB_CH = 64
SAMPLE_CH = 32
VMEM_LIMIT = 48 * 1024 * 1024

BF16 = jnp.bfloat16
F32 = jnp.float32


def _pack_bf16_pairs(h):
    w = h.shape[1] // 2
    hi = lax.bitcast_convert_type(h[:, :w].astype(F32), jnp.uint32)
    lo = lax.bitcast_convert_type(h[:, w:].astype(F32), jnp.uint32)
    return lax.bitcast_convert_type(hi | (lo >> 16), jnp.int32)


def _unpack_bf16_pairs(words):
    u = lax.bitcast_convert_type(words, jnp.uint32)
    hi = lax.bitcast_convert_type(u & jnp.uint32(0xFFFF0000), F32)
    lo = lax.bitcast_convert_type(u << 16, F32)
    return jnp.concatenate([hi, lo], axis=-1)


def _segment_mean_sq(a, bd):
    w = a.shape[1]
    return jnp.dot((a * a).astype(BF16), bd[:w, :w], preferred_element_type=F32)


def _rms_bf16(x, g):
    ms = jnp.mean(x * x, axis=-1, keepdims=True)
    return (x * lax.rsqrt(ms + EPS) * g).astype(BF16)


def _qk_norm(q, k, qg, kg, bd):
    qn = []
    for c in range(Q_W // 256):
        qc = q[:, c * 256:(c + 1) * 256]
        qn.append(qc * lax.rsqrt(_segment_mean_sq(qc, bd) + EPS))
    qn = jnp.concatenate(qn, axis=-1) * qg
    kn = k * lax.rsqrt(_segment_mean_sq(k, bd) + EPS) * kg
    return qn, kn


def _project(x, g, w_in, qg, kg, bd):
    z = jnp.dot(_rms_bf16(x, g), w_in, preferred_element_type=F32)
    u = z[:, :POOL_W]
    q = z[:, POOL_W:POOL_W + Q_W]
    k = z[:, POOL_W + Q_W:POOL_W + Q_W + KV_W]
    v = z[:, POOL_W + Q_W + KV_W:]
    qn, kn = _qk_norm(q, k, qg, kg, bd)
    return u, qn, kn, v


def _pool_project(d_groups, wp_ref, ps):
    outs = []
    for p in range(2):
        dp = jnp.concatenate([d_groups[2 * p], d_groups[2 * p + 1]], axis=-1).astype(BF16)
        y = jnp.dot(dp, wp_ref[p], preferred_element_type=F32)
        outs.append(y * ps[:, p * 256:(p + 1) * 256])
    return jnp.concatenate(outs, axis=-1)


def _proj_pool_kernel(x_ref, g_ref, win_ref, qg_ref, kg_ref, bd_ref, wp_ref, ps_ref,
                      pool_ref, q_ref, k_ref, vt_ref, utail_ref, ktail_ref, vtail_ref,
                      ext_ref, sa_ref, sb_ref, zq_ref, *, tm, n_j):
    j = pl.program_id(1)

    @pl.when(j == 0)
    def _():
        ext_ref[0:HALO, :] = jnp.zeros((HALO, POOL_W), F32)

    r = tm + HALO
    h = _rms_bf16(x_ref[...], g_ref[...])
    ext_ref[HALO:r, :] = jnp.dot(h, win_ref[:, 0:POOL_W], preferred_element_type=F32)
    zq_ref[...] = jnp.dot(h, win_ref[:, POOL_W:], preferred_element_type=F32)
    u = ext_ref[HALO:r, :]
    sa_ref[8:r, :] = ext_ref[8:r, :] + ext_ref[7:r - 1, :]
    sb_ref[16:r, 128:] = sa_ref[16:r, 128:] + sa_ref[14:r - 2, 128:]
    sa_ref[24:r, 256:] = sb_ref[24:r, 256:] + sb_ref[20:r - 4, 256:]
    sb_ref[32:r, 384:] = sa_ref[32:r, 384:] + sa_ref[24:r - 8, 384:]
    pos1 = j * tm + lax.broadcasted_iota(jnp.int32, (tm, POOL_GC), 0) + 1
    sums = (sa_ref, sb_ref, sa_ref, sb_ref)
    d_groups = []
    for gi, w in enumerate(POOL_WINDOWS):
        sl = slice(gi * POOL_GC, (gi + 1) * POOL_GC)
        cnt = jnp.minimum(pos1, w).astype(F32)
        d_groups.append(sums[gi][HALO:r, sl] / cnt - u[:, sl])
    pool_ref[...] = _pool_project(d_groups, wp_ref, ps_ref[...]).astype(BF16)
    ext_ref[16:HALO, :] = ext_ref[tm + 16:r, :]

    qn, kn = _qk_norm(zq_ref[:, 0:Q_W], zq_ref[:, Q_W:Q_W + KV_W], qg_ref[...], kg_ref[...], bd_ref[...])
    v = zq_ref[:, Q_W + KV_W:]
    q_ref[...] = qn.astype(BF16)
    k_ref[...] = kn.astype(BF16)
    vt_ref[...] = jnp.transpose(v).astype(BF16)

    @pl.when(j == n_j - 1)
    def _():
        utail_ref[...] = u[tm - 16:, :]
        ktail_ref[...] = kn[tm - WINDOW:, :]
        vtail_ref[...] = v[tm - WINDOW:, :]


def _proj_pool_combine_kernel(x1_ref, gath_ref, route_ref, *rest, tm, n_j):
    x2_ref = rest[7]
    _combine_kernel(x1_ref, gath_ref, route_ref, x2_ref)
    _proj_pool_kernel(x2_ref, *rest[:7], *rest[8:], tm=tm, n_j=n_j)


def _proj_pool_prompt(l, x_in, n_seq, seq, g_attn, w_in, qg, kg, bd, wp, ps):
    tm = TM_PROJ
    n_j = seq // tm
    t = n_seq * seq
    row = lambda b, j: (b * n_j + j, 0)
    lay = lambda b, j: (l, 0, 0)
    fused = isinstance(x_in, tuple)
    if fused:
        kern = _proj_pool_combine_kernel
        x_args = list(x_in)
        x_specs = [pl.BlockSpec((tm, D_MODEL), row),
                   pl.BlockSpec((2, tm, D_MODEL // 2), lambda b, j: (0, b * n_j + j, 0)),
                   pl.BlockSpec((ROUTE_FIELDS, tm), lambda b, j: (0, b * n_j + j))]
        x_out_specs = [pl.BlockSpec((tm, D_MODEL), row)]
        x_out_shape = [jax.ShapeDtypeStruct((t, D_MODEL), F32)]
    else:
        kern = _proj_pool_kernel
        x_args = [x_in]
        x_specs = [pl.BlockSpec((tm, D_MODEL), row)]
        x_out_specs, x_out_shape = [], []
    return pl.pallas_call(
        functools.partial(kern, tm=tm, n_j=n_j),
        grid=(n_seq, n_j),
        in_specs=x_specs + [
            pl.BlockSpec((None, 1, D_MODEL), lay),
            pl.BlockSpec((None, D_MODEL, D_IN), lay),
            pl.BlockSpec((None, 1, Q_W), lay),
            pl.BlockSpec((None, 1, KV_W), lay),
            pl.BlockSpec((256, 256), lambda b, j: (0, 0)),
            pl.BlockSpec((None, 2, 256, 256), lambda b, j: (l, 0, 0, 0)),
            pl.BlockSpec((None, 1, POOL_W), lay),
        ],
        out_specs=x_out_specs + [
            pl.BlockSpec((tm, POOL_W), row),
            pl.BlockSpec((tm, Q_W), row),
            pl.BlockSpec((tm, KV_W), row),
            pl.BlockSpec((KV_W, tm), lambda b, j: (0, b * n_j + j)),
            pl.BlockSpec((None, 16, POOL_W), lambda b, j: (b, 0, 0)),
            pl.BlockSpec((None, WINDOW, KV_W), lambda b, j: (b, 0, 0)),
            pl.BlockSpec((None, WINDOW, KV_W), lambda b, j: (b, 0, 0)),
        ],
        out_shape=x_out_shape + [
            jax.ShapeDtypeStruct((t, POOL_W), BF16),
            jax.ShapeDtypeStruct((t, Q_W), BF16),
            jax.ShapeDtypeStruct((t, KV_W), BF16),
            jax.ShapeDtypeStruct((KV_W, t), BF16),
            jax.ShapeDtypeStruct((n_seq, 16, POOL_W), F32),
            jax.ShapeDtypeStruct((n_seq, WINDOW, KV_W), F32),
            jax.ShapeDtypeStruct((n_seq, WINDOW, KV_W), F32),
        ],
        scratch_shapes=[pltpu.VMEM((tm + HALO, POOL_W), F32)] * 3 + [pltpu.VMEM((tm, Q_W + 2 * KV_W), F32)],
        compiler_params=pltpu.CompilerParams(
            dimension_semantics=("arbitrary", "arbitrary"), vmem_limit_bytes=VMEM_LIMIT),
        name="proj_pool_prompt",
    )(*x_args, g_attn, w_in, qg, kg, bd, wp, ps)


def _attn_kernel(sink_ref, q_ref, kp_ref, kc_ref, vtp_ref, vtc_ref, bias_ref, o_ref, s_ref):
    j = pl.program_id(1)
    kk_all = jnp.concatenate([kp_ref[...], kc_ref[...]], axis=0)
    vt_all = jnp.concatenate([vtp_ref[...], vtc_ref[...]], axis=1)
    from_prev = (lax.broadcasted_iota(jnp.int32, (WINDOW, WINDOW), 0)
                 > lax.broadcasted_iota(jnp.int32, (WINDOW, WINDOW), 1))
    units = [(blk, kv) for blk in range(ATTN_QB) for kv in range(N_KV_HEADS)]

    def scores(n):
        blk, kv = units[n]
        q = q_ref[blk * WINDOW:(blk + 1) * WINDOW, :]
        kk = kk_all[blk * WINDOW:(blk + 2) * WINDOW, kv * HEAD_DIM:(kv + 1) * HEAD_DIM]
        heads = range(kv * GQA_GROUP, (kv + 1) * GQA_GROUP)
        q_rows = jnp.concatenate([q[:, h * HEAD_DIM:(h + 1) * HEAD_DIM] for h in heads], axis=0)
        s_ref[n % 2] = lax.dot_general(kk, q_rows, (((1,), (1,)), ((), ())), preferred_element_type=F32)

    scores(0)
    outs = []
    for n, (blk, kv) in enumerate(units):
        if n + 1 < len(units):
            scores(n + 1)
        vt_kv = vt_all[kv * HEAD_DIM:(kv + 1) * HEAD_DIM, blk * WINDOW:(blk + 2) * WINDOW]
        variant = jnp.minimum(j, 1) if blk == 0 else 1
        for g in range(GQA_GROUP):
            h = kv * GQA_GROUP + g
            s = jnp.where(from_prev, s_ref[n % 2, 0:WINDOW, g * WINDOW:(g + 1) * WINDOW],
                          s_ref[n % 2, WINDOW:, g * WINDOW:(g + 1) * WINDOW]) + bias_ref[variant, h]
            sink = sink_ref[h]
            m = jnp.maximum(jnp.max(s, axis=0, keepdims=True), sink)
            p = jnp.exp(s - m)
            denom = jnp.sum(p, axis=0, keepdims=True) + jnp.exp(sink - m)
            p_keys = jnp.concatenate([jnp.where(from_prev, p, 0.0), jnp.where(from_prev, 0.0, p)], axis=0)
            o_t = jnp.dot(vt_kv, p_keys.astype(BF16), preferred_element_type=F32)
            outs.append(o_t / denom)
        if kv == N_KV_HEADS - 1:
            o_ref[blk * WINDOW:(blk + 1) * WINDOW, :] = jnp.transpose(jnp.concatenate(outs, axis=0)).astype(BF16)
            outs = []


def _attn_prompt(q, k, vt, bias_t, sinks, n_seq, seq):
    tq = ATTN_QB * WINDOW
    nj = seq // tq
    t = n_seq * seq
    cur = lambda b, j: (b * nj + j, 0)
    prev = lambda b, j: (jnp.maximum((b * nj + j) * ATTN_QB - 1, 0), 0)
    cur_t = lambda b, j: (0, b * nj + j)
    prev_t = lambda b, j: (0, jnp.maximum((b * nj + j) * ATTN_QB - 1, 0))
    return pl.pallas_call(
        _attn_kernel,
        grid=(n_seq, nj),
        in_specs=[
            pl.BlockSpec(memory_space=pltpu.SMEM),
            pl.BlockSpec((tq, Q_W), cur),
            pl.BlockSpec((WINDOW, KV_W), prev),
            pl.BlockSpec((tq, KV_W), cur),
            pl.BlockSpec((KV_W, WINDOW), prev_t),
            pl.BlockSpec((KV_W, tq), cur_t),
            pl.BlockSpec((2, N_HEADS, WINDOW, WINDOW), lambda b, j: (0, 0, 0, 0)),
        ],
        out_specs=pl.BlockSpec((tq, Q_W), cur),
        out_shape=jax.ShapeDtypeStruct((t, Q_W), BF16),
        scratch_shapes=[pltpu.VMEM((2, 2 * WINDOW, GQA_GROUP * WINDOW), F32)],
        compiler_params=pltpu.CompilerParams(
            dimension_semantics=("arbitrary", "arbitrary"), vmem_limit_bytes=VMEM_LIMIT),
        name="attn_prompt",
    )(sinks, q, k, k, vt, vt, bias_t)


def _prompt_bias_t():
    r = jnp.arange(WINDOW, dtype=jnp.int32)[None, :]
    c = jnp.arange(WINDOW, dtype=jnp.int32)[:, None]
    from_prev = c > r
    dist = r - c + jnp.where(from_prev, WINDOW, 0)
    slopes = jnp.exp2(-8.0 * jnp.arange(1, N_HEADS + 1, dtype=F32) / N_HEADS)
    later = -slopes[:, None, None] * dist.astype(F32)[None]
    first = jnp.where(from_prev[None], -jnp.inf, later)
    return jnp.stack([first, later])


def _sample_kernel(x_ref, g_ref, win_ref, qg_ref, kg_ref, bd_ref, wp_ref, ps_ref,
                   st_ref, ck_ref, cv_ref, sink_ref, bias_ref, perm_ref,
                   pool_ref, attn_ref, pst_ref, kc_ref, vc_ref, *, ns, pos0):
    u, qn, kn, v = _project(x_ref[...], g_ref[...], win_ref[...], qg_ref[...], kg_ref[...], bd_ref[...])
    pst_ref[:, 0:POOL_STATE - 1, :] = st_ref[:, 1:POOL_STATE, :]
    kc_ref[:, 0:WINDOW - 1, :] = ck_ref[:, 1:WINDOW, :]
    vc_ref[:, 0:WINDOW - 1, :] = cv_ref[:, 1:WINDOW, :]
    for n in range(ns):
        pst_ref[n, POOL_STATE - 1:POOL_STATE, :] = u[n:n + 1, :]
        kc_ref[n, WINDOW - 1:WINDOW, :] = kn[n:n + 1, :]
        vc_ref[n, WINDOW - 1:WINDOW, :] = v[n:n + 1, :]

    d_groups = []
    for gi, w in enumerate(POOL_WINDOWS):
        lo = gi * POOL_GC
        acc = u[:, lo:lo + POOL_GC]
        for back in range(1, w):
            acc = acc + st_ref[:, POOL_STATE - back, lo:lo + POOL_GC]
        d_groups.append(acc / float(min(pos0 + 1, w)) - u[:, lo:lo + POOL_GC])
    pool_ref[...] = _pool_project(d_groups, wp_ref, ps_ref[...]).astype(BF16)

    zeros = jnp.zeros((ns, HEAD_DIM), F32)
    stacked = []
    for h in range(N_HEADS):
        piece = qn[:, h * HEAD_DIM:(h + 1) * HEAD_DIM]
        pair = [piece, zeros] if h < GQA_GROUP else [zeros, piece]
        stacked.append(jnp.concatenate(pair, axis=-1))
    q_hn = jnp.concatenate(stacked, axis=0).astype(BF16)
    q_nh = jnp.dot(perm_ref[0], q_hn, preferred_element_type=F32).astype(BF16)

    keys = kc_ref[...].reshape(ns * WINDOW, KV_W).astype(BF16)
    vals = vc_ref[...].reshape(ns * WINDOW, KV_W).astype(BF16)
    s_all = lax.dot_general(q_nh, keys, (((1,), (1,)), ((), ())), preferred_element_type=F32)
    sink = sink_ref[...]
    bias = bias_ref[...]
    zero_blk = jnp.zeros((N_HEADS, WINDOW), F32)
    p_rows = []
    for n in range(ns):
        s = s_all[n * N_HEADS:(n + 1) * N_HEADS, n * WINDOW:(n + 1) * WINDOW] + bias
        m = jnp.maximum(jnp.max(s, axis=-1, keepdims=True), sink)
        p = jnp.exp(s - m)
        denom = jnp.sum(p, axis=-1, keepdims=True) + jnp.exp(sink - m)
        p_rows.append(jnp.concatenate([zero_blk] * n + [p / denom] + [zero_blk] * (ns - 1 - n), axis=-1))
    p_blockdiag = jnp.concatenate(p_rows, axis=0).astype(BF16)
    o_nh = jnp.dot(p_blockdiag, vals, preferred_element_type=F32).astype(BF16)
    o_hn = jnp.dot(perm_ref[1], o_nh, preferred_element_type=F32)
    pieces = []
    for h in range(N_HEADS):
        kv = h // GQA_GROUP
        pieces.append(o_hn[h * ns:(h + 1) * ns, kv * HEAD_DIM:(kv + 1) * HEAD_DIM])
    attn_ref[...] = jnp.concatenate(pieces, axis=-1).astype(BF16)


def _sample_mixer(l, depth, xs, g_attn, w_in, qg, kg, bd, wp, ps, state, ck, cv, sink8, bias_s, pos0):
    n = xs.shape[0]
    ns = 32
    row = lambda i: (i, 0)
    lay = lambda i: (l, 0, 0)
    src = jnp.arange(ns * N_HEADS)
    perm = (((src % N_HEADS) * ns + src // N_HEADS)[:, None] == src[None, :]).astype(BF16)
    perms = jnp.stack([perm, perm.T])
    return pl.pallas_call(
        functools.partial(_sample_kernel, ns=ns, pos0=pos0),
        grid=(n // ns,),
        input_output_aliases={8: 2, 9: 3, 10: 4},
        in_specs=[
            pl.BlockSpec((ns, D_MODEL), row),
            pl.BlockSpec((None, 1, D_MODEL), lay),
            pl.BlockSpec((None, D_MODEL, D_IN), lay),
            pl.BlockSpec((None, 1, Q_W), lay),
            pl.BlockSpec((None, 1, KV_W), lay),
            pl.BlockSpec((256, 256), lambda i: (0, 0)),
            pl.BlockSpec((None, 2, 256, 256), lambda i: (l, 0, 0, 0)),
            pl.BlockSpec((None, 1, POOL_W), lay),
            pl.BlockSpec((None, ns, POOL_STATE, POOL_W), lambda i: (l, i, 0, 0)),
            pl.BlockSpec((None, ns, WINDOW, KV_W), lambda i: (l, i, 0, 0)),
            pl.BlockSpec((None, ns, WINDOW, KV_W), lambda i: (l, i, 0, 0)),
            pl.BlockSpec((N_HEADS, 1), lambda i: (0, 0)),
            pl.BlockSpec((N_HEADS, WINDOW), lambda i: (0, 0)),
            pl.BlockSpec((2, ns * N_HEADS, ns * N_HEADS), lambda i: (0, 0, 0)),
        ],
        out_specs=[
            pl.BlockSpec((ns, POOL_W), row),
            pl.BlockSpec((ns, Q_W), row),
            pl.BlockSpec((None, ns, POOL_STATE, POOL_W), lambda i: (l, i, 0, 0)),
            pl.BlockSpec((None, ns, WINDOW, KV_W), lambda i: (l, i, 0, 0)),
            pl.BlockSpec((None, ns, WINDOW, KV_W), lambda i: (l, i, 0, 0)),
        ],
        out_shape=[
            jax.ShapeDtypeStruct((n, POOL_W), BF16),
            jax.ShapeDtypeStruct((n, Q_W), BF16),
            jax.ShapeDtypeStruct((depth, n, POOL_STATE, POOL_W), F32),
            jax.ShapeDtypeStruct((depth, n, WINDOW, KV_W), F32),
            jax.ShapeDtypeStruct((depth, n, WINDOW, KV_W), F32),
        ],
        compiler_params=pltpu.CompilerParams(
            dimension_semantics=("arbitrary",), vmem_limit_bytes=VMEM_LIMIT),
        name="sample_mixer",
    )(xs, g_attn, w_in, qg, kg, bd, wp, ps, state, ck, cv, sink8, bias_s, perms)


def _merge_router_kernel(pool_ref, attn_ref, x_ref, wout_ref, g_ref, wr_ref, br_ref, utri_ref, cin_ref,
                         x1_ref, h2_ref, route_t_ref, cnt_ref, y_ref, lg_ref):
    i = pl.program_id(0)

    @pl.when(i == 0)
    def _():
        cnt_ref[...] = cin_ref[...]

    tm = x_ref.shape[0]
    rc = tm // MERGE_CHUNKS
    chunks = [slice(ci * rc, (ci + 1) * rc) for ci in range(MERGE_CHUNKS)]
    for rows in chunks:
        y_ref[rows, :] = (jnp.dot(pool_ref[rows, :], wout_ref[0:POOL_W, :], preferred_element_type=F32)
                          + jnp.dot(attn_ref[rows, :], wout_ref[POOL_W:, :], preferred_element_type=F32))
    for rows in chunks:
        x1 = x_ref[rows, :] + y_ref[rows, :]
        x1_ref[rows, :] = x1
        h2 = _rms_bf16(x1, g_ref[...])
        h2_ref[rows, :] = _pack_bf16_pairs(h2)
        lg_ref[rows, :] = jnp.dot(h2, wr_ref[...], preferred_element_type=F32) + br_ref[...]
    logits = lg_ref[...]

    lt = jnp.transpose(logits)
    sub = lax.broadcasted_iota(jnp.int32, (EXPERTS_PER_GROUP, tm), 0)
    neg = -jnp.inf
    big = jnp.int32(EXPERTS_PER_GROUP)
    gl = jnp.where(sub < N_EXPERT_GROUPS, lt[GROUP_LANE0:GROUP_LANE0 + EXPERTS_PER_GROUP, :], neg)
    gmax = jnp.max(gl, axis=0, keepdims=True)
    grp = jnp.min(jnp.where(gl == gmax, sub, big), axis=0, keepdims=True)
    g_w = 1.0 / jnp.sum(jnp.exp(gl - gmax), axis=0, keepdims=True)
    el = lt[(N_EXPERT_GROUPS - 1) * EXPERTS_PER_GROUP:N_EXPERT_GROUPS * EXPERTS_PER_GROUP, :]
    for gi in range(N_EXPERT_GROUPS - 2, -1, -1):
        el = jnp.where(grp == gi, lt[gi * EXPERTS_PER_GROUP:(gi + 1) * EXPERTS_PER_GROUP, :], el)
    v1 = jnp.max(el, axis=0, keepdims=True)
    i1 = jnp.min(jnp.where(el == v1, sub, big), axis=0, keepdims=True)
    el2 = jnp.where(sub == i1, neg, el)
    v2 = jnp.max(el2, axis=0, keepdims=True)
    i2 = jnp.min(jnp.where(el2 == v2, sub, big), axis=0, keepdims=True)
    e21 = jnp.exp(v2 - v1)
    w1 = g_w / (1.0 + e21)
    w2 = g_w * e21 / (1.0 + e21)
    e1 = grp * EXPERTS_PER_GROUP + i1
    e2 = grp * EXPERTS_PER_GROUP + i2

    esub = lax.broadcasted_iota(jnp.int32, (N_EXPERTS, tm), 0)
    oh1 = esub == e1
    oh2 = esub == e2
    c = jnp.where(oh1 | oh2, 1.0, 0.0)
    prefix = jnp.dot(c.astype(BF16), utri_ref[...], preferred_element_type=F32) + cnt_ref[...]
    r1 = jnp.sum(jnp.where(oh1, prefix, 0.0), axis=0, keepdims=True)
    r2 = jnp.sum(jnp.where(oh2, prefix, 0.0), axis=0, keepdims=True)
    cnt_ref[...] = cnt_ref[...] + jnp.sum(c, axis=1, keepdims=True)

    fields = jnp.zeros((ROUTE_FIELDS, tm), F32)
    for idx, val in enumerate((e1.astype(F32), e2.astype(F32), w1, w2, r1, r2)):
        fields = jnp.where(sub == idx, val, fields)
    route_t_ref[...] = fields


def _merge_router(l, pool, attn, x2d, w_out, g_ffn, wr, br, cnt_in, tm):
    t = x2d.shape[0]
    utri = (jnp.arange(tm)[:, None] < jnp.arange(tm)[None, :]).astype(BF16)
    row = lambda i: (i, 0)
    lay = lambda i: (l, 0, 0)
    return pl.pallas_call(
        _merge_router_kernel,
        grid=(t // tm,),
        in_specs=[
            pl.BlockSpec((tm, POOL_W), row),
            pl.BlockSpec((tm, Q_W), row),
            pl.BlockSpec((tm, D_MODEL), row),
            pl.BlockSpec((None, D_MODEL, D_MODEL), lay),
            pl.BlockSpec((None, 1, D_MODEL), lay),
            pl.BlockSpec((None, D_MODEL, LANES), lay),
            pl.BlockSpec((None, 1, LANES), lay),
            pl.BlockSpec((tm, tm), lambda i: (0, 0)),
            pl.BlockSpec((N_EXPERTS, 1), lambda i: (0, 0)),
        ],
        out_specs=[
            pl.BlockSpec((tm, D_MODEL), row),
            pl.BlockSpec((tm, D_MODEL // 2), row),
            pl.BlockSpec((ROUTE_FIELDS, tm), lambda i: (0, i)),
            pl.BlockSpec((N_EXPERTS, 1), lambda i: (0, 0)),
        ],
        out_shape=[
            jax.ShapeDtypeStruct((t, D_MODEL), F32),
            jax.ShapeDtypeStruct((t, D_MODEL // 2), jnp.int32),
            jax.ShapeDtypeStruct((ROUTE_FIELDS, t), F32),
            jax.ShapeDtypeStruct((N_EXPERTS, 1), F32),
        ],
        scratch_shapes=[pltpu.VMEM((tm, D_MODEL), F32), pltpu.VMEM((tm, LANES), F32)],
        compiler_params=pltpu.CompilerParams(
            dimension_semantics=("arbitrary",), vmem_limit_bytes=VMEM_LIMIT),
        name="merge_router",
    )(pool, attn, x2d, w_out, g_ffn, wr, br, utri, cnt_in)


def _moe_kernel(be_ref, rv_ref, nx_ref, sl_ref, first_ref, xd_ref, wg_hbm, wu_hbm, wd_hbm, yd_ref,
                wg_f, wu_f, wd_f, wg_s, wu_s, wd_s, sem, *, layer):
    step = pl.program_id(0)

    def weight_copies(e, s):
        return [pltpu.make_async_copy(w_hbm.at[layer, e], w_f.at[s], sem.at[s, n])
                for n, (w_hbm, w_f) in enumerate(((wg_hbm, wg_f), (wu_hbm, wu_f), (wd_hbm, wd_f)))]

    @pl.when(step == 0)
    def _():
        for s in range(WEIGHT_SLOTS - 1):
            @pl.when(first_ref[s] >= 0)
            def _():
                for c in weight_copies(first_ref[s], s):
                    c.start()

    def enter_expert(i):
        expert, slot = be_ref[i], sl_ref[i]

        @pl.when((i == 0) | (expert != be_ref[jnp.maximum(i - 1, 0)]))
        def _():
            for c in weight_copies(expert, slot):
                c.wait()

            @pl.when(nx_ref[i] >= 0)
            def _():
                for c in weight_copies(nx_ref[i], lax.rem(slot + WEIGHT_SLOTS - 1, WEIGHT_SLOTS)):
                    c.start(priority=1)

            wg_s[...] = wg_f[slot].astype(BF16)
            wu_s[...] = wu_f[slot].astype(BF16)
            wd_s[...] = wd_f[slot].astype(BF16)

    def experts_on(row0, n_rows, rows_valid):
        rows = pl.ds(row0, n_rows)
        row = lax.broadcasted_iota(jnp.int32, (n_rows, D_MODEL // 2), 0)
        x = _unpack_bf16_pairs(jnp.where(row < rows_valid, xd_ref[rows, :], 0)).astype(BF16)
        gate = jnp.dot(x, wg_s[...], preferred_element_type=F32)
        up = jnp.dot(x, wu_s[...], preferred_element_type=F32)
        act = (gate * jax.nn.sigmoid(gate) * up).astype(BF16)
        y = jnp.dot(act, wd_s[...], preferred_element_type=F32)
        yd_ref[rows, :] = _pack_bf16_pairs(y.astype(BF16))

    def single_block(i, row0):
        enter_expert(i)

        @pl.when(rv_ref[i] > 0)
        def _():
            experts_on(row0, MOE_BM, rv_ref[i])

        @pl.when(rv_ref[i] <= 0)
        def _():
            yd_ref[pl.ds(row0, MOE_BM), :] = jnp.zeros((MOE_BM, D_MODEL // 2), jnp.int32)

    @pl.when(rv_ref[step * MOE_STEP_BLOCKS] > 0)
    def _():
        for pair in range(MOE_STEP_BLOCKS // 2):
            ia = step * MOE_STEP_BLOCKS + 2 * pair
            ib = ia + 1
            row0 = 2 * pair * MOE_BM
            same = (be_ref[ib] == be_ref[ia]) & (rv_ref[ib] > 0)

            @pl.when(same)
            def _():
                enter_expert(ia)
                experts_on(row0, 2 * MOE_BM, MOE_BM + rv_ref[ib])

            @pl.when(jnp.logical_not(same))
            def _():
                single_block(ia, row0)
                single_block(ib, row0 + MOE_BM)


def _moe_experts(l, block_e, rows_valid, next_e, slot, first_e, xd, w_gate, w_up, w_down):
    n_blocks = xd.shape[0] // MOE_BM
    step_rows = MOE_STEP_BLOCKS * MOE_BM
    row = lambda i, be, rv, nx, sl, fe: (jnp.minimum(i, fe[WEIGHT_SLOTS - 1] - 1), 0)
    return pl.pallas_call(
        functools.partial(_moe_kernel, layer=l),
        grid_spec=pltpu.PrefetchScalarGridSpec(
            num_scalar_prefetch=5,
            grid=(n_blocks // MOE_STEP_BLOCKS,),
            in_specs=[
                pl.BlockSpec((step_rows, D_MODEL // 2), row),
                pl.BlockSpec(memory_space=pl.ANY),
                pl.BlockSpec(memory_space=pl.ANY),
                pl.BlockSpec(memory_space=pl.ANY),
            ],
            out_specs=pl.BlockSpec((step_rows, D_MODEL // 2), row),
            scratch_shapes=[
                pltpu.VMEM((WEIGHT_SLOTS, D_MODEL, EXPERT_FF), F32),
                pltpu.VMEM((WEIGHT_SLOTS, D_MODEL, EXPERT_FF), F32),
                pltpu.VMEM((WEIGHT_SLOTS, EXPERT_FF, D_MODEL), F32),
                pltpu.VMEM((D_MODEL, EXPERT_FF), BF16),
                pltpu.VMEM((D_MODEL, EXPERT_FF), BF16),
                pltpu.VMEM((EXPERT_FF, D_MODEL), BF16),
                pltpu.SemaphoreType.DMA((WEIGHT_SLOTS, 3)),
            ],
        ),
        out_shape=jax.ShapeDtypeStruct((n_blocks * MOE_BM, D_MODEL // 2), jnp.int32),
        compiler_params=pltpu.CompilerParams(
            dimension_semantics=("arbitrary",), vmem_limit_bytes=VMEM_LIMIT),
        name="moe_experts",
    )(block_e, rows_valid, next_e, slot, first_e, xd, w_gate, w_up, w_down)


def _sc_worker_id():
    return lax.axis_index("s") * SC_CORES + lax.axis_index("c")


def _sc_dispatch(hp, hs, dest_p, dest_s, n_rows):
    tp, width = hp.shape
    per_w = tp // SC_WORKERS
    n_ch = per_w // DISP_CH
    n_sw = hs.shape[0] // SAMPLE_CH
    mesh = plsc.VectorSubcoreMesh(core_axis_name="c", subcore_axis_name="s")

    @functools.partial(
        pl.kernel, mesh=mesh,
        out_type=jax.ShapeDtypeStruct((n_rows, width), jnp.int32),
        scratch_types=[
            pltpu.VMEM((2, n_ch, DISP_CH), jnp.int32),
            pltpu.VMEM((2, 1, SAMPLE_CH), jnp.int32),
            pltpu.VMEM((2, DISP_CH, width), jnp.int32),
            pltpu.SemaphoreType.DMA((2,)),
            pltpu.SemaphoreType.DMA((2,)),
        ],
        name="sc_dispatch",
    )
    def k(hp_hbm, hs_hbm, dp_hbm, ds_hbm, xd_hbm, idx_v, idxs_v, bufs, rsem, wsem):
        wid = _sc_worker_id()
        base = wid * per_w
        for kk in range(2):
            pltpu.sync_copy(dp_hbm.at[kk, wid], idx_v.at[kk])
        reads = [pltpu.make_async_copy(hp_hbm.at[pl.ds(base + j * DISP_CH, DISP_CH)],
                                       bufs.at[j % 2], rsem.at[j % 2]) for j in range(n_ch)]
        reads[0].start()
        for j in range(n_ch):
            if j + 1 < n_ch:
                reads[j + 1].start()
            reads[j].wait()
            writes = [pltpu.make_async_copy(bufs.at[j % 2], xd_hbm.at[idx_v.at[kk, j]], wsem.at[kk])
                      for kk in range(2)]
            for w in writes:
                w.start()
            for w in writes:
                w.wait()

        @pl.when(wid < n_sw)
        def _():
            rows = bufs.at[0, pl.ds(0, SAMPLE_CH)]
            for kk in range(2):
                pltpu.sync_copy(ds_hbm.at[kk, wid], idxs_v.at[kk])
            pltpu.sync_copy(hs_hbm.at[pl.ds(wid * SAMPLE_CH, SAMPLE_CH)], rows)
            for kk in range(2):
                pltpu.sync_copy(rows, xd_hbm.at[idxs_v.at[kk, 0]])

    return k(hp, hs, dest_p, dest_s)


def _sc_combine_gather(yd, dest_p, dest_s, tp, ts):
    width = yd.shape[1]
    per_w = tp // SC_WORKERS
    n_ch = per_w // COMB_CH
    n_sw = ts // SAMPLE_CH
    mesh = plsc.VectorSubcoreMesh(core_axis_name="c", subcore_axis_name="s")

    @functools.partial(
        pl.kernel, mesh=mesh,
        out_type=jax.ShapeDtypeStruct((2, tp + ts, width), yd.dtype),
        scratch_types=[
            pltpu.VMEM((2, n_ch, COMB_CH), jnp.int32),
            pltpu.VMEM((2, 1, SAMPLE_CH), jnp.int32),
            pltpu.VMEM((2, COMB_CH, width), yd.dtype),
            pltpu.SemaphoreType.DMA((2,)),
            pltpu.SemaphoreType.DMA((2,)),
        ],
        name="sc_combine_gather",
    )
    def k(yd_hbm, dp_hbm, ds_hbm, g_hbm, idx_v, idxs_v, bufs, gsem, wsem):
        wid = _sc_worker_id()
        base = wid * per_w
        for kk in range(2):
            pltpu.sync_copy(dp_hbm.at[kk, wid], idx_v.at[kk])
        items = [(kk, j) for kk in range(2) for j in range(n_ch)]
        gathers = [pltpu.make_async_copy(yd_hbm.at[idx_v.at[kk, j]], bufs.at[n % 2], gsem.at[n % 2])
                   for n, (kk, j) in enumerate(items)]
        gathers[0].start()
        for n, (kk, j) in enumerate(items):
            if n + 1 < len(items):
                gathers[n + 1].start()
            gathers[n].wait()
            w = pltpu.make_async_copy(bufs.at[n % 2], g_hbm.at[kk, pl.ds(base + j * COMB_CH, COMB_CH)],
                                      wsem.at[n % 2])
            w.start()
            w.wait()

        @pl.when(wid < n_sw)
        def _():
            for kk in range(2):
                pltpu.sync_copy(ds_hbm.at[kk, wid], idxs_v.at[kk])
            for kk in range(2):
                rows = bufs.at[kk, pl.ds(0, SAMPLE_CH)]
                pltpu.sync_copy(yd_hbm.at[idxs_v.at[kk, 0]], rows)
                pltpu.sync_copy(rows, g_hbm.at[kk, pl.ds(tp + wid * SAMPLE_CH, SAMPLE_CH)])

    return k(yd, dest_p, dest_s)


def _combine_kernel(x1_ref, g_ref, route_t_ref, x2_ref):
    fields = route_t_ref[...]
    tm = fields.shape[1]
    cols = jnp.transpose(jnp.concatenate([fields, jnp.zeros((LANES - ROUTE_FIELDS, tm), F32)], axis=0))
    w1 = cols[:, 2:3]
    w2 = cols[:, 3:4]
    x2_ref[...] = x1_ref[...] + _unpack_bf16_pairs(g_ref[0]) * w1 + _unpack_bf16_pairs(g_ref[1]) * w2


def _combine(x1, g, route_t, row0, tm):
    t = x1.shape[0]
    blk0 = row0 // tm
    row = lambda i: (i, 0)
    return pl.pallas_call(
        _combine_kernel,
        grid=(t // tm,),
        in_specs=[
            pl.BlockSpec((tm, D_MODEL), row),
            pl.BlockSpec((2, tm, D_MODEL // 2), lambda i: (0, blk0 + i, 0)),
            pl.BlockSpec((ROUTE_FIELDS, tm), lambda i: (0, i)),
        ],
        out_specs=pl.BlockSpec((tm, D_MODEL), row),
        out_shape=jax.ShapeDtypeStruct((t, D_MODEL), F32),
        compiler_params=pltpu.CompilerParams(
            dimension_semantics=("arbitrary",), vmem_limit_bytes=VMEM_LIMIT),
        name="combine",
    )(x1, g, route_t)


def _dest_layout(dest, workers, chunk):
    t = dest.shape[1]
    return dest.reshape(2, workers, t // (workers * chunk), chunk)


def _hier_moe(l, h2p, h2s, route_tp, route_ts, counts, w_gate, w_up, w_down):
    tp, ts = h2p.shape[0], h2s.shape[0]
    n_assign = 2 * (tp + ts)
    n_blocks = -(-n_assign // MOE_BM) + N_EXPERTS
    n_blocks = -(-n_blocks // MOE_STEP_BLOCKS) * MOE_STEP_BLOCKS
    pcounts = (counts + MOE_BM - 1) // MOE_BM * MOE_BM
    pends = jnp.cumsum(pcounts)
    poffsets = pends - pcounts
    starts = jnp.arange(n_blocks, dtype=jnp.int32) * MOE_BM
    block_e = jnp.minimum(jnp.sum((pends[None, :] <= starts[:, None]).astype(jnp.int32), axis=1),
                          N_EXPERTS - 1)
    experts = jnp.arange(N_EXPERTS, dtype=jnp.int32)

    def lookup(table, idx):
        return jnp.sum(jnp.where(idx[..., None] == experts, table, 0), axis=-1)

    rows_valid = jnp.clip(lookup(poffsets + counts, block_e) - starts, 0, MOE_BM).astype(jnp.int32)
    used = counts > 0
    last_e = jnp.max(jnp.where(used, jnp.arange(N_EXPERTS, dtype=jnp.int32), 0))
    block_e = jnp.where(rows_valid > 0, block_e, last_e).astype(jnp.int32)
    place = jnp.cumsum(used.astype(jnp.int32)) - 1
    by_place = jnp.sum(jnp.where(used[None, :] & (place[None, :] == experts[:, None]), experts[None, :], 0),
                       axis=1)
    n_used = jnp.sum(used.astype(jnp.int32))

    def at_place(p):
        return jnp.where(p < n_used, lookup(by_place, jnp.minimum(p, N_EXPERTS - 1)), -1).astype(jnp.int32)

    ahead_of = at_place(place + (WEIGHT_SLOTS - 1))
    next_e = lookup(ahead_of, block_e)
    slot = lookup(place % WEIGHT_SLOTS, block_e)
    n_steps_used = -(-(pends[-1] // MOE_BM) // MOE_STEP_BLOCKS)
    first_e = jnp.concatenate([at_place(jnp.arange(WEIGHT_SLOTS - 1, dtype=jnp.int32)),
                               n_steps_used.reshape(1).astype(jnp.int32)])

    def dest_of(route_t):
        return lookup(poffsets, route_t[0:2].astype(jnp.int32)) + route_t[4:6].astype(jnp.int32)

    dest_p, dest_s = dest_of(route_tp), dest_of(route_ts)
    n_sw = ts // SAMPLE_CH
    xd = _sc_dispatch(h2p, h2s, _dest_layout(dest_p, SC_WORKERS, DISP_CH),
                      _dest_layout(dest_s, n_sw, SAMPLE_CH), n_blocks * MOE_BM)
    yd = _moe_experts(l, block_e, rows_valid, next_e, slot, first_e, xd, w_gate, w_up, w_down)
    return _sc_combine_gather(yd, _dest_layout(dest_p, SC_WORKERS, COMB_CH),
                              _dest_layout(dest_s, n_sw, SAMPLE_CH), tp, ts)


def kernel(x_prompt, x_sample, state_pool, cache_k_win, cache_v_win, norm_attn_g, w_in, pool_w, pool_scale, q_norm_g, k_norm_g, attn_sinks, w_out, norm_ffn_g, router_group_w, router_group_b, router_expert_w, router_expert_b, w_gate, w_up, w_down):
    n_p, t_p, d = x_prompt.shape
    n_s, t_s, _ = x_sample.shape
    depth = w_in.shape[0]
    lw_s = cache_k_win.shape[2]
    assert t_s == 1 and lw_s == WINDOW and d == D_MODEL
    assert t_p % TM_PROJ == 0 and t_p >= WINDOW
    past_len = 16384

    seg = jnp.arange(256) // HEAD_DIM
    bd = jnp.where(seg[:, None] == seg[None, :], 1.0 / HEAD_DIM, 0.0).astype(BF16)
    slopes = jnp.exp2(-8.0 * jnp.arange(1, N_HEADS + 1, dtype=F32) / N_HEADS)
    bias_p = _prompt_bias_t()
    dist_s = (WINDOW - 1) - jnp.arange(WINDOW, dtype=F32)
    bias_s = -slopes[:, None] * dist_s[None, :]

    wp = jnp.zeros((depth, 2, 256, 256), F32)
    for p in range(2):
        wp = wp.at[:, p, :POOL_GC, :POOL_GC].set(pool_w[:, 2 * p])
        wp = wp.at[:, p, POOL_GC:, POOL_GC:].set(pool_w[:, 2 * p + 1])
    assert GROUP_LANE0 == N_EXPERTS
    lane_pad = LANES - N_EXPERTS - N_EXPERT_GROUPS
    wr = jnp.concatenate([router_expert_w, router_group_w, jnp.zeros((depth, D_MODEL, lane_pad), F32)], axis=-1)
    br = jnp.concatenate([router_expert_b, router_group_b, jnp.zeros((depth, lane_pad), F32)],
                         axis=-1).reshape(depth, 1, LANES)
    lp = dict(
        w_in=w_in.astype(BF16),
        w_out=w_out.astype(BF16),
        g_attn=norm_attn_g.reshape(depth, 1, D_MODEL),
        g_ffn=norm_ffn_g.reshape(depth, 1, D_MODEL),
        qg=(jnp.tile(q_norm_g, (1, N_HEADS)) * ATTN_SCALE).reshape(depth, 1, Q_W),
        kg=jnp.tile(k_norm_g, (1, N_KV_HEADS)).reshape(depth, 1, KV_W),
        wp=wp.astype(BF16),
        ps=pool_scale.reshape(depth, 1, POOL_W),
        wr=wr.astype(BF16),
        br=br,
        state=state_pool,
        ck=cache_k_win.reshape(depth, n_s, lw_s, KV_W),
        cv=cache_v_win.reshape(depth, n_s, lw_s, KV_W),
    )

    xp = x_prompt.reshape(n_p * t_p, D_MODEL)
    xs = x_sample.reshape(n_s, D_MODEL)
    lw_p = min(WINDOW, t_p)
    pool_p, kp_new, vp_new = [], [], []
    sample_state = [lp["state"], lp["ck"], lp["cv"]]
    zero_cnt = jnp.zeros((N_EXPERTS, 1), F32)
    pending = None
    for l in range(depth):
        sinks = attn_sinks[l]
        outs = _proj_pool_prompt(
            l, xp if pending is None else pending, n_p, t_p,
            lp["g_attn"], lp["w_in"], lp["qg"], lp["kg"], bd, lp["wp"], lp["ps"])
        if pending is not None:
            xp, outs = outs[0], outs[1:]
        pool_o, q, k, vt, utail, ktail, vtail = outs
        attn_o = _attn_prompt(q, k, vt, bias_p, sinks, n_p, t_p)
        x1p, h2p, route_tp, cnt_p = _merge_router(
            l, pool_o, attn_o, xp, lp["w_out"], lp["g_ffn"], lp["wr"], lp["br"], zero_cnt, TM_MERGE)
        pool_p.append(utail[:, 16 - POOL_STATE:, :])
        kp_new.append(ktail)
        vp_new.append(vtail)
        pool_so, attn_so, *sample_state = _sample_mixer(
            l, depth, xs, lp["g_attn"], lp["w_in"], lp["qg"], lp["kg"], bd, lp["wp"], lp["ps"],
            *sample_state, sinks.reshape(N_HEADS, 1), bias_s, past_len)
        x1s, h2s, route_ts, cnt_all = _merge_router(
            l, pool_so, attn_so, xs, lp["w_out"], lp["g_ffn"], lp["wr"], lp["br"], cnt_p, n_s)
        counts = cnt_all[:, 0].astype(jnp.int32)
        g = _hier_moe(l, h2p, h2s, route_tp, route_ts, counts, w_gate, w_up, w_down)
        xs = _combine(x1s, g, route_ts, n_p * t_p, n_s)
        pending = (x1p, g, route_tp)
    xp = _combine(*pending, 0, TM_MERGE)
    return (xp.reshape(n_p, t_p, D_MODEL), xs.reshape(n_s, t_s, D_MODEL),
            jnp.stack(pool_p),
            jnp.stack(kp_new).reshape(depth, n_p, lw_p, N_KV_HEADS, HEAD_DIM),
            jnp.stack(vp_new).reshape(depth, n_p, lw_p, N_KV_HEADS, HEAD_DIM),
            sample_state[0],
            sample_state[1].reshape(depth, n_s, lw_s, N_KV_HEADS, HEAD_DIM),
            sample_state[2].reshape(depth, n_s, lw_s, N_KV_HEADS, HEAD_DIM))
```

```python
import functools

import jax
import jax.numpy as jnp
from jax import lax
from jax.experimental import pallas as pl
from jax.experimental.pallas import tpu as pltpu
from jax.experimental.pallas import tpu_sc as plsc

D_MODEL = 1024
POOL_W = 512
POOL_WINDOWS = (2, 4, 8, 16)
POOL_GC = 128
POOL_STATE = 15
HEAD_DIM = 64
N_HEADS = 8
N_KV_HEADS = 2
GQA_GROUP = 4
Q_W = 512
KV_W = 128
D_IN = POOL_W + Q_W + 2 * KV_W
WINDOW = 128
ATTN_SCALE = HEAD_DIM ** -0.5
N_EXPERT_GROUPS = 4
EXPERTS_PER_GROUP = 8
N_EXPERTS = 32
EXPERT_FF = 512
EPS = 1e-6

LANES = 128
HALO = 32
TM_PROJ = 1024
TM_MERGE = 512
MERGE_CHUNKS = 2
ATTN_QB = 16
MOE_BM = 256
MOE_STEP_BLOCKS = 4
WEIGHT_SLOTS = 3
GROUP_LANE0 = 32
ROUTE_FIELDS = 8
SC_CORES = 2
SC_SUBCORES = 16
SC_WORKERS = SC_CORES * SC_SUBCORES
DISP_CH = 64
COMB_CH = 64
SAMPLE_CH = 32
VMEM_LIMIT = 48 * 1024 * 1024

BF16 = jnp.bfloat16
F32 = jnp.float32


def _pack_bf16_pairs(h):
    w = h.shape[1] // 2
    hi = lax.bitcast_convert_type(h[:, :w].astype(F32), jnp.uint32)
    lo = lax.bitcast_convert_type(h[:, w:].astype(F32), jnp.uint32)
    return lax.bitcast_convert_type(hi | (lo >> 16), jnp.int32)


def _unpack_bf16_pairs(words):
    u = lax.bitcast_convert_type(words, jnp.uint32)
    hi = lax.bitcast_convert_type(u & jnp.uint32(0xFFFF0000), F32)
    lo = lax.bitcast_convert_type(u << 16, F32)
    return jnp.concatenate([hi, lo], axis=-1)


def _segment_mean_sq(a, bd):
    w = a.shape[1]
    return jnp.dot((a * a).astype(BF16), bd[:w, :w], preferred_element_type=F32)


def _rms_bf16(x, g):
    ms = jnp.mean(x * x, axis=-1, keepdims=True)
    return (x * lax.rsqrt(ms + EPS) * g).astype(BF16)


def _qk_norm(q, k, qg, kg, bd):
    qn = []
    for c in range(Q_W // 256):
        qc = q[:, c * 256:(c + 1) * 256]
        qn.append(qc * lax.rsqrt(_segment_mean_sq(qc, bd) + EPS))
    qn = jnp.concatenate(qn, axis=-1) * qg
    kn = k * lax.rsqrt(_segment_mean_sq(k, bd) + EPS) * kg
    return qn, kn


def _project(x, g, w_in, qg, kg, bd):
    z = jnp.dot(_rms_bf16(x, g), w_in, preferred_element_type=F32)
    u = z[:, :POOL_W]
    q = z[:, POOL_W:POOL_W + Q_W]
    k = z[:, POOL_W + Q_W:POOL_W + Q_W + KV_W]
    v = z[:, POOL_W + Q_W + KV_W:]
    qn, kn = _qk_norm(q, k, qg, kg, bd)
    return u, qn, kn, v


def _pool_project(d_groups, wp_ref, ps):
    outs = []
    for p in range(2):
        dp = jnp.concatenate([d_groups[2 * p], d_groups[2 * p + 1]], axis=-1).astype(BF16)
        y = jnp.dot(dp, wp_ref[p], preferred_element_type=F32)
        outs.append(y * ps[:, p * 256:(p + 1) * 256])
    return jnp.concatenate(outs, axis=-1)


def _proj_pool_kernel(x_ref, g_ref, win_ref, qg_ref, kg_ref, bd_ref, wp_ref, ps_ref,
                      pool_ref, q_ref, k_ref, vt_ref, utail_ref, ktail_ref, vtail_ref,
                      ext_ref, sa_ref, sb_ref, zq_ref, *, tm, n_j):
    j = pl.program_id(1)

    @pl.when(j == 0)
    def _():
        ext_ref[0:HALO, :] = jnp.zeros((HALO, POOL_W), F32)

    r = tm + HALO
    h = _rms_bf16(x_ref[...], g_ref[...])
    ext_ref[HALO:r, :] = jnp.dot(h, win_ref[:, 0:POOL_W], preferred_element_type=F32)
    zq_ref[...] = jnp.dot(h, win_ref[:, POOL_W:], preferred_element_type=F32)
    u = ext_ref[HALO:r, :]
    sa_ref[8:r, :] = ext_ref[8:r, :] + ext_ref[7:r - 1, :]
    sb_ref[16:r, 128:] = sa_ref[16:r, 128:] + sa_ref[14:r - 2, 128:]
    sa_ref[24:r, 256:] = sb_ref[24:r, 256:] + sb_ref[20:r - 4, 256:]
    sb_ref[32:r, 384:] = sa_ref[32:r, 384:] + sa_ref[24:r - 8, 384:]
    pos1 = j * tm + lax.broadcasted_iota(jnp.int32, (tm, POOL_GC), 0) + 1
    sums = (sa_ref, sb_ref, sa_ref, sb_ref)
    d_groups = []
    for gi, w in enumerate(POOL_WINDOWS):
        sl = slice(gi * POOL_GC, (gi + 1) * POOL_GC)
        cnt = jnp.minimum(pos1, w).astype(F32)
        d_groups.append(sums[gi][HALO:r, sl] / cnt - u[:, sl])
    pool_ref[...] = _pool_project(d_groups, wp_ref, ps_ref[...]).astype(BF16)
    ext_ref[16:HALO, :] = ext_ref[tm + 16:r, :]

    qn, kn = _qk_norm(zq_ref[:, 0:Q_W], zq_ref[:, Q_W:Q_W + KV_W], qg_ref[...], kg_ref[...], bd_ref[...])
    v = zq_ref[:, Q_W + KV_W:]
    q_ref[...] = qn.astype(BF16)
    k_ref[...] = kn.astype(BF16)
    vt_ref[...] = jnp.transpose(v).astype(BF16)

    @pl.when(j == n_j - 1)
    def _():
        utail_ref[...] = u[tm - 16:, :]
        ktail_ref[...] = kn[tm - WINDOW:, :]
        vtail_ref[...] = v[tm - WINDOW:, :]


def _proj_pool_combine_kernel(x1_ref, gath_ref, route_ref, *rest, tm, n_j):
    x2_ref = rest[7]
    _combine_kernel(x1_ref, gath_ref, route_ref, x2_ref)
    _proj_pool_kernel(x2_ref, *rest[:7], *rest[8:], tm=tm, n_j=n_j)


def _proj_pool_prompt(l, x_in, n_seq, seq, g_attn, w_in, qg, kg, bd, wp, ps):
    tm = TM_PROJ
    n_j = seq // tm
    t = n_seq * seq
    row = lambda b, j: (b * n_j + j, 0)
    lay = lambda b, j: (l, 0, 0)
    fused = isinstance(x_in, tuple)
    if fused:
        kern = _proj_pool_combine_kernel
        x_args = list(x_in)
        x_specs = [pl.BlockSpec((tm, D_MODEL), row),
                   pl.BlockSpec((2, tm, D_MODEL // 2), lambda b, j: (0, b * n_j + j, 0)),
                   pl.BlockSpec((ROUTE_FIELDS, tm), lambda b, j: (0, b * n_j + j))]
        x_out_specs = [pl.BlockSpec((tm, D_MODEL), row)]
        x_out_shape = [jax.ShapeDtypeStruct((t, D_MODEL), F32)]
    else:
        kern = _proj_pool_kernel
        x_args = [x_in]
        x_specs = [pl.BlockSpec((tm, D_MODEL), row)]
        x_out_specs, x_out_shape = [], []
    return pl.pallas_call(
        functools.partial(kern, tm=tm, n_j=n_j),
        grid=(n_seq, n_j),
        in_specs=x_specs + [
            pl.BlockSpec((None, 1, D_MODEL), lay),
            pl.BlockSpec((None, D_MODEL, D_IN), lay),
            pl.BlockSpec((None, 1, Q_W), lay),
            pl.BlockSpec((None, 1, KV_W), lay),
            pl.BlockSpec((256, 256), lambda b, j: (0, 0)),
            pl.BlockSpec((None, 2, 256, 256), lambda b, j: (l, 0, 0, 0)),
            pl.BlockSpec((None, 1, POOL_W), lay),
        ],
        out_specs=x_out_specs + [
            pl.BlockSpec((tm, POOL_W), row),
            pl.BlockSpec((tm, Q_W), row),
            pl.BlockSpec((tm, KV_W), row),
            pl.BlockSpec((KV_W, tm), lambda b, j: (0, b * n_j + j)),
            pl.BlockSpec((None, 16, POOL_W), lambda b, j: (b, 0, 0)),
            pl.BlockSpec((None, WINDOW, KV_W), lambda b, j: (b, 0, 0)),
            pl.BlockSpec((None, WINDOW, KV_W), lambda b, j: (b, 0, 0)),
        ],
        out_shape=x_out_shape + [
            jax.ShapeDtypeStruct((t, POOL_W), BF16),
            jax.ShapeDtypeStruct((t, Q_W), BF16),
            jax.ShapeDtypeStruct((t, KV_W), BF16),
            jax.ShapeDtypeStruct((KV_W, t), BF16),
            jax.ShapeDtypeStruct((n_seq, 16, POOL_W), F32),
            jax.ShapeDtypeStruct((n_seq, WINDOW, KV_W), F32),
            jax.ShapeDtypeStruct((n_seq, WINDOW, KV_W), F32),
        ],
        scratch_shapes=[pltpu.VMEM((tm + HALO, POOL_W), F32)] * 3 + [pltpu.VMEM((tm, Q_W + 2 * KV_W), F32)],
        compiler_params=pltpu.CompilerParams(
            dimension_semantics=("arbitrary", "arbitrary"), vmem_limit_bytes=VMEM_LIMIT),
        name="proj_pool_prompt",
    )(*x_args, g_attn, w_in, qg, kg, bd, wp, ps)


def _attn_kernel(sink_ref, q_ref, kp_ref, kc_ref, vtp_ref, vtc_ref, bias_ref, o_ref, s_ref):
    j = pl.program_id(1)
    kk_all = jnp.concatenate([kp_ref[...], kc_ref[...]], axis=0)
    vt_all = jnp.concatenate([vtp_ref[...], vtc_ref[...]], axis=1)
    from_prev = (lax.broadcasted_iota(jnp.int32, (WINDOW, WINDOW), 0)
                 > lax.broadcasted_iota(jnp.int32, (WINDOW, WINDOW), 1))
    units = [(blk, kv) for blk in range(ATTN_QB) for kv in range(N_KV_HEADS)]

    def scores(n):
        blk, kv = units[n]
        q = q_ref[blk * WINDOW:(blk + 1) * WINDOW, :]
        kk = kk_all[blk * WINDOW:(blk + 2) * WINDOW, kv * HEAD_DIM:(kv + 1) * HEAD_DIM]
        heads = range(kv * GQA_GROUP, (kv + 1) * GQA_GROUP)
        q_rows = jnp.concatenate([q[:, h * HEAD_DIM:(h + 1) * HEAD_DIM] for h in heads], axis=0)
        s_ref[n % 2] = lax.dot_general(kk, q_rows, (((1,), (1,)), ((), ())), preferred_element_type=F32)

    scores(0)
    outs = []
    for n, (blk, kv) in enumerate(units):
        if n + 1 < len(units):
            scores(n + 1)
        vt_kv = vt_all[kv * HEAD_DIM:(kv + 1) * HEAD_DIM, blk * WINDOW:(blk + 2) * WINDOW]
        variant = jnp.minimum(j, 1) if blk == 0 else 1
        for g in range(GQA_GROUP):
            h = kv * GQA_GROUP + g
            s = jnp.where(from_prev, s_ref[n % 2, 0:WINDOW, g * WINDOW:(g + 1) * WINDOW],
                          s_ref[n % 2, WINDOW:, g * WINDOW:(g + 1) * WINDOW]) + bias_ref[variant, h]
            sink = sink_ref[h]
            m = jnp.maximum(jnp.max(s, axis=0, keepdims=True), sink)
            p = jnp.exp(s - m)
            denom = jnp.sum(p, axis=0, keepdims=True) + jnp.exp(sink - m)
            p_keys = jnp.concatenate([jnp.where(from_prev, p, 0.0), jnp.where(from_prev, 0.0, p)], axis=0)
            o_t = jnp.dot(vt_kv, p_keys.astype(BF16), preferred_element_type=F32)
            outs.append(o_t / denom)
        if kv == N_KV_HEADS - 1:
            o_ref[blk * WINDOW:(blk + 1) * WINDOW, :] = jnp.transpose(jnp.concatenate(outs, axis=0)).astype(BF16)
            outs = []


def _attn_prompt(q, k, vt, bias_t, sinks, n_seq, seq):
    tq = ATTN_QB * WINDOW
    nj = seq // tq
    t = n_seq * seq
    cur = lambda b, j: (b * nj + j, 0)
    prev = lambda b, j: (jnp.maximum((b * nj + j) * ATTN_QB - 1, 0), 0)
    cur_t = lambda b, j: (0, b * nj + j)
    prev_t = lambda b, j: (0, jnp.maximum((b * nj + j) * ATTN_QB - 1, 0))
    return pl.pallas_call(
        _attn_kernel,
        grid=(n_seq, nj),
        in_specs=[
            pl.BlockSpec(memory_space=pltpu.SMEM),
            pl.BlockSpec((tq, Q_W), cur),
            pl.BlockSpec((WINDOW, KV_W), prev),
            pl.BlockSpec((tq, KV_W), cur),
            pl.BlockSpec((KV_W, WINDOW), prev_t),
            pl.BlockSpec((KV_W, tq), cur_t),
            pl.BlockSpec((2, N_HEADS, WINDOW, WINDOW), lambda b, j: (0, 0, 0, 0)),
        ],
        out_specs=pl.BlockSpec((tq, Q_W), cur),
        out_shape=jax.ShapeDtypeStruct((t, Q_W), BF16),
        scratch_shapes=[pltpu.VMEM((2, 2 * WINDOW, GQA_GROUP * WINDOW), F32)],
        compiler_params=pltpu.CompilerParams(
            dimension_semantics=("arbitrary", "arbitrary"), vmem_limit_bytes=VMEM_LIMIT),
        name="attn_prompt",
    )(sinks, q, k, k, vt, vt, bias_t)


def _prompt_bias_t():
    r = jnp.arange(WINDOW, dtype=jnp.int32)[None, :]
    c = jnp.arange(WINDOW, dtype=jnp.int32)[:, None]
    from_prev = c > r
    dist = r - c + jnp.where(from_prev, WINDOW, 0)
    slopes = jnp.exp2(-8.0 * jnp.arange(1, N_HEADS + 1, dtype=F32) / N_HEADS)
    later = -slopes[:, None, None] * dist.astype(F32)[None]
    first = jnp.where(from_prev[None], -jnp.inf, later)
    return jnp.stack([first, later])


def _sample_kernel(x_ref, g_ref, win_ref, qg_ref, kg_ref, bd_ref, wp_ref, ps_ref,
                   st_ref, ck_ref, cv_ref, sink_ref, bias_ref, perm_ref,
                   pool_ref, attn_ref, pst_ref, kc_ref, vc_ref, *, ns, pos0):
    u, qn, kn, v = _project(x_ref[...], g_ref[...], win_ref[...], qg_ref[...], kg_ref[...], bd_ref[...])
    pst_ref[:, 0:POOL_STATE - 1, :] = st_ref[:, 1:POOL_STATE, :]
    kc_ref[:, 0:WINDOW - 1, :] = ck_ref[:, 1:WINDOW, :]
    vc_ref[:, 0:WINDOW - 1, :] = cv_ref[:, 1:WINDOW, :]
    for n in range(ns):
        pst_ref[n, POOL_STATE - 1:POOL_STATE, :] = u[n:n + 1, :]
        kc_ref[n, WINDOW - 1:WINDOW, :] = kn[n:n + 1, :]
        vc_ref[n, WINDOW - 1:WINDOW, :] = v[n:n + 1, :]

    d_groups = []
    for gi, w in enumerate(POOL_WINDOWS):
        lo = gi * POOL_GC
        acc = u[:, lo:lo + POOL_GC]
        for back in range(1, w):
            acc = acc + st_ref[:, POOL_STATE - back, lo:lo + POOL_GC]
        d_groups.append(acc / float(min(pos0 + 1, w)) - u[:, lo:lo + POOL_GC])
    pool_ref[...] = _pool_project(d_groups, wp_ref, ps_ref[...]).astype(BF16)

    zeros = jnp.zeros((ns, HEAD_DIM), F32)
    stacked = []
    for h in range(N_HEADS):
        piece = qn[:, h * HEAD_DIM:(h + 1) * HEAD_DIM]
        pair = [piece, zeros] if h < GQA_GROUP else [zeros, piece]
        stacked.append(jnp.concatenate(pair, axis=-1))
    q_hn = jnp.concatenate(stacked, axis=0).astype(BF16)
    q_nh = jnp.dot(perm_ref[0], q_hn, preferred_element_type=F32).astype(BF16)

    keys = kc_ref[...].reshape(ns * WINDOW, KV_W).astype(BF16)
    vals = vc_ref[...].reshape(ns * WINDOW, KV_W).astype(BF16)
    s_all = lax.dot_general(q_nh, keys, (((1,), (1,)), ((), ())), preferred_element_type=F32)
    sink = sink_ref[...]
    bias = bias_ref[...]
    zero_blk = jnp.zeros((N_HEADS, WINDOW), F32)
    p_rows = []
    for n in range(ns):
        s = s_all[n * N_HEADS:(n + 1) * N_HEADS, n * WINDOW:(n + 1) * WINDOW] + bias
        m = jnp.maximum(jnp.max(s, axis=-1, keepdims=True), sink)
        p = jnp.exp(s - m)
        denom = jnp.sum(p, axis=-1, keepdims=True) + jnp.exp(sink - m)
        p_rows.append(jnp.concatenate([zero_blk] * n + [p / denom] + [zero_blk] * (ns - 1 - n), axis=-1))
    p_blockdiag = jnp.concatenate(p_rows, axis=0).astype(BF16)
    o_nh = jnp.dot(p_blockdiag, vals, preferred_element_type=F32).astype(BF16)
    o_hn = jnp.dot(perm_ref[1], o_nh, preferred_element_type=F32)
    pieces = []
    for h in range(N_HEADS):
        kv = h // GQA_GROUP
        pieces.append(o_hn[h * ns:(h + 1) * ns, kv * HEAD_DIM:(kv + 1) * HEAD_DIM])
    attn_ref[...] = jnp.concatenate(pieces, axis=-1).astype(BF16)


def _sample_mixer(l, depth, xs, g_attn, w_in, qg, kg, bd, wp, ps, state, ck, cv, sink8, bias_s, pos0):
    n = xs.shape[0]
    ns = 32
    row = lambda i: (i, 0)
    lay = lambda i: (l, 0, 0)
    src = jnp.arange(ns * N_HEADS)
    perm = (((src % N_HEADS) * ns + src // N_HEADS)[:, None] == src[None, :]).astype(BF16)
    perms = jnp.stack([perm, perm.T])
    return pl.pallas_call(
        functools.partial(_sample_kernel, ns=ns, pos0=pos0),
        grid=(n // ns,),
        input_output_aliases={8: 2, 9: 3, 10: 4},
        in_specs=[
            pl.BlockSpec((ns, D_MODEL), row),
            pl.BlockSpec((None, 1, D_MODEL), lay),
            pl.BlockSpec((None, D_MODEL, D_IN), lay),
            pl.BlockSpec((None, 1, Q_W), lay),
            pl.BlockSpec((None, 1, KV_W), lay),
            pl.BlockSpec((256, 256), lambda i: (0, 0)),
            pl.BlockSpec((None, 2, 256, 256), lambda i: (l, 0, 0, 0)),
            pl.BlockSpec((None, 1, POOL_W), lay),
            pl.BlockSpec((None, ns, POOL_STATE, POOL_W), lambda i: (l, i, 0, 0)),
            pl.BlockSpec((None, ns, WINDOW, KV_W), lambda i: (l, i, 0, 0)),
            pl.BlockSpec((None, ns, WINDOW, KV_W), lambda i: (l, i, 0, 0)),
            pl.BlockSpec((N_HEADS, 1), lambda i: (0, 0)),
            pl.BlockSpec((N_HEADS, WINDOW), lambda i: (0, 0)),
            pl.BlockSpec((2, ns * N_HEADS, ns * N_HEADS), lambda i: (0, 0, 0)),
        ],
        out_specs=[
            pl.BlockSpec((ns, POOL_W), row),
            pl.BlockSpec((ns, Q_W), row),
            pl.BlockSpec((None, ns, POOL_STATE, POOL_W), lambda i: (l, i, 0, 0)),
            pl.BlockSpec((None, ns, WINDOW, KV_W), lambda i: (l, i, 0, 0)),
            pl.BlockSpec((None, ns, WINDOW, KV_W), lambda i: (l, i, 0, 0)),
        ],
        out_shape=[
            jax.ShapeDtypeStruct((n, POOL_W), BF16),
            jax.ShapeDtypeStruct((n, Q_W), BF16),
            jax.ShapeDtypeStruct((depth, n, POOL_STATE, POOL_W), F32),
            jax.ShapeDtypeStruct((depth, n, WINDOW, KV_W), F32),
            jax.ShapeDtypeStruct((depth, n, WINDOW, KV_W), F32),
        ],
        compiler_params=pltpu.CompilerParams(
            dimension_semantics=("arbitrary",), vmem_limit_bytes=VMEM_LIMIT),
        name="sample_mixer",
    )(xs, g_attn, w_in, qg, kg, bd, wp, ps, state, ck, cv, sink8, bias_s, perms)


def _merge_router_kernel(pool_ref, attn_ref, x_ref, wout_ref, g_ref, wr_ref, br_ref, utri_ref, cin_ref,
                         x1_ref, h2_ref, route_t_ref, cnt_ref, y_ref, lg_ref):
    i = pl.program_id(0)

    @pl.when(i == 0)
    def _():
        cnt_ref[...] = cin_ref[...]

    tm = x_ref.shape[0]
    rc = tm // MERGE_CHUNKS
    chunks = [slice(ci * rc, (ci + 1) * rc) for ci in range(MERGE_CHUNKS)]
    for rows in chunks:
        y_ref[rows, :] = (jnp.dot(pool_ref[rows, :], wout_ref[0:POOL_W, :], preferred_element_type=F32)
                          + jnp.dot(attn_ref[rows, :], wout_ref[POOL_W:, :], preferred_element_type=F32))
    for rows in chunks:
        x1 = x_ref[rows, :] + y_ref[rows, :]
        x1_ref[rows, :] = x1
        h2 = _rms_bf16(x1, g_ref[...])
        h2_ref[rows, :] = _pack_bf16_pairs(h2)
        lg_ref[rows, :] = jnp.dot(h2, wr_ref[...], preferred_element_type=F32) + br_ref[...]
    logits = lg_ref[...]

    lt = jnp.transpose(logits)
    sub = lax.broadcasted_iota(jnp.int32, (EXPERTS_PER_GROUP, tm), 0)
    neg = -jnp.inf
    big = jnp.int32(EXPERTS_PER_GROUP)
    gl = jnp.where(sub < N_EXPERT_GROUPS, lt[GROUP_LANE0:GROUP_LANE0 + EXPERTS_PER_GROUP, :], neg)
    gmax = jnp.max(gl, axis=0, keepdims=True)
    grp = jnp.min(jnp.where(gl == gmax, sub, big), axis=0, keepdims=True)
    g_w = 1.0 / jnp.sum(jnp.exp(gl - gmax), axis=0, keepdims=True)
    el = lt[(N_EXPERT_GROUPS - 1) * EXPERTS_PER_GROUP:N_EXPERT_GROUPS * EXPERTS_PER_GROUP, :]
    for gi in range(N_EXPERT_GROUPS - 2, -1, -1):
        el = jnp.where(grp == gi, lt[gi * EXPERTS_PER_GROUP:(gi + 1) * EXPERTS_PER_GROUP, :], el)
    v1 = jnp.max(el, axis=0, keepdims=True)
    i1 = jnp.min(jnp.where(el == v1, sub, big), axis=0, keepdims=True)
    el2 = jnp.where(sub == i1, neg, el)
    v2 = jnp.max(el2, axis=0, keepdims=True)
    i2 = jnp.min(jnp.where(el2 == v2, sub, big), axis=0, keepdims=True)
    e21 = jnp.exp(v2 - v1)
    w1 = g_w / (1.0 + e21)
    w2 = g_w * e21 / (1.0 + e21)
    e1 = grp * EXPERTS_PER_GROUP + i1
    e2 = grp * EXPERTS_PER_GROUP + i2

    esub = lax.broadcasted_iota(jnp.int32, (N_EXPERTS, tm), 0)
    oh1 = esub == e1
    oh2 = esub == e2
    c = jnp.where(oh1 | oh2, 1.0, 0.0)
    prefix = jnp.dot(c.astype(BF16), utri_ref[...], preferred_element_type=F32) + cnt_ref[...]
    r1 = jnp.sum(jnp.where(oh1, prefix, 0.0), axis=0, keepdims=True)
    r2 = jnp.sum(jnp.where(oh2, prefix, 0.0), axis=0, keepdims=True)
    cnt_ref[...] = cnt_ref[...] + jnp.sum(c, axis=1, keepdims=True)

    fields = jnp.zeros((ROUTE_FIELDS, tm), F32)
    for idx, val in enumerate((e1.astype(F32), e2.astype(F32), w1, w2, r1, r2)):
        fields = jnp.where(sub == idx, val, fields)
    route_t_ref[...] = fields


def _merge_router(l, pool, attn, x2d, w_out, g_ffn, wr, br, cnt_in, tm):
    t = x2d.shape[0]
    utri = (jnp.arange(tm)[:, None] < jnp.arange(tm)[None, :]).astype(BF16)
    row = lambda i: (i, 0)
    lay = lambda i: (l, 0, 0)
    return pl.pallas_call(
        _merge_router_kernel,
        grid=(t // tm,),
        in_specs=[
            pl.BlockSpec((tm, POOL_W), row),
            pl.BlockSpec((tm, Q_W), row),
            pl.BlockSpec((tm, D_MODEL), row),
            pl.BlockSpec((None, D_MODEL, D_MODEL), lay),
            pl.BlockSpec((None, 1, D_MODEL), lay),
            pl.BlockSpec((None, D_MODEL, LANES), lay),
            pl.BlockSpec((None, 1, LANES), lay),
            pl.BlockSpec((tm, tm), lambda i: (0, 0)),
            pl.BlockSpec((N_EXPERTS, 1), lambda i: (0, 0)),
        ],
        out_specs=[
            pl.BlockSpec((tm, D_MODEL), row),
            pl.BlockSpec((tm, D_MODEL // 2), row),
            pl.BlockSpec((ROUTE_FIELDS, tm), lambda i: (0, i)),
            pl.BlockSpec((N_EXPERTS, 1), lambda i: (0, 0)),
        ],
        out_shape=[
            jax.ShapeDtypeStruct((t, D_MODEL), F32),
            jax.ShapeDtypeStruct((t, D_MODEL // 2), jnp.int32),
            jax.ShapeDtypeStruct((ROUTE_FIELDS, t), F32),
            jax.ShapeDtypeStruct((N_EXPERTS, 1), F32),
        ],
        scratch_shapes=[pltpu.VMEM((tm, D_MODEL), F32), pltpu.VMEM((tm, LANES), F32)],
        compiler_params=pltpu.CompilerParams(
            dimension_semantics=("arbitrary",), vmem_limit_bytes=VMEM_LIMIT),
        name="merge_router",
    )(pool, attn, x2d, w_out, g_ffn, wr, br, utri, cnt_in)


def _moe_kernel(be_ref, rv_ref, nx_ref, sl_ref, first_ref, xd_ref, wg_hbm, wu_hbm, wd_hbm, yd_ref,
                wg_f, wu_f, wd_f, wg_s, wu_s, wd_s, sem, *, layer):
    step = pl.program_id(0)

    def weight_copies(e, s):
        return [pltpu.make_async_copy(w_hbm.at[layer, e], w_f.at[s], sem.at[s, n])
                for n, (w_hbm, w_f) in enumerate(((wg_hbm, wg_f), (wu_hbm, wu_f), (wd_hbm, wd_f)))]

    @pl.when(step == 0)
    def _():
        for s in range(WEIGHT_SLOTS - 1):
            @pl.when(first_ref[s] >= 0)
            def _():
                for c in weight_copies(first_ref[s], s):
                    c.start()

    def enter_expert(i):
        expert, slot = be_ref[i], sl_ref[i]

        @pl.when((i == 0) | (expert != be_ref[jnp.maximum(i - 1, 0)]))
        def _():
            for c in weight_copies(expert, slot):
                c.wait()

            @pl.when(nx_ref[i] >= 0)
            def _():
                for c in weight_copies(nx_ref[i], lax.rem(slot + WEIGHT_SLOTS - 1, WEIGHT_SLOTS)):
                    c.start(priority=1)

            wg_s[...] = wg_f[slot].astype(BF16)
            wu_s[...] = wu_f[slot].astype(BF16)
            wd_s[...] = wd_f[slot].astype(BF16)

    def experts_on(row0, n_rows, rows_valid):
        rows = pl.ds(row0, n_rows)
        row = lax.broadcasted_iota(jnp.int32, (n_rows, D_MODEL // 2), 0)
        x = _unpack_bf16_pairs(jnp.where(row < rows_valid, xd_ref[rows, :], 0)).astype(BF16)
        gate = jnp.dot(x, wg_s[...], preferred_element_type=F32)
        up = jnp.dot(x, wu_s[...], preferred_element_type=F32)
        act = (gate * jax.nn.sigmoid(gate) * up).astype(BF16)
        y = jnp.dot(act, wd_s[...], preferred_element_type=F32)
        yd_ref[rows, :] = _pack_bf16_pairs(y.astype(BF16))

    def single_block(i, row0):
        enter_expert(i)

        @pl.when(rv_ref[i] > 0)
        def _():
            experts_on(row0, MOE_BM, rv_ref[i])

        @pl.when(rv_ref[i] <= 0)
        def _():
            yd_ref[pl.ds(row0, MOE_BM), :] = jnp.zeros((MOE_BM, D_MODEL // 2), jnp.int32)

    @pl.when(rv_ref[step * MOE_STEP_BLOCKS] > 0)
    def _():
        for pair in range(MOE_STEP_BLOCKS // 2):
            ia = step * MOE_STEP_BLOCKS + 2 * pair
            ib = ia + 1
            row0 = 2 * pair * MOE_BM
            same = (be_ref[ib] == be_ref[ia]) & (rv_ref[ib] > 0)

            @pl.when(same)
            def _():
                enter_expert(ia)
                experts_on(row0, 2 * MOE_BM, MOE_BM + rv_ref[ib])

            @pl.when(jnp.logical_not(same))
            def _():
                single_block(ia, row0)
                single_block(ib, row0 + MOE_BM)


def _moe_experts(l, block_e, rows_valid, next_e, slot, first_e, xd, w_gate, w_up, w_down):
    n_blocks = xd.shape[0] // MOE_BM
    step_rows = MOE_STEP_BLOCKS * MOE_BM
    row = lambda i, be, rv, nx, sl, fe: (jnp.minimum(i, fe[WEIGHT_SLOTS - 1] - 1), 0)
    return pl.pallas_call(
        functools.partial(_moe_kernel, layer=l),
        grid_spec=pltpu.PrefetchScalarGridSpec(
            num_scalar_prefetch=5,
            grid=(n_blocks // MOE_STEP_BLOCKS,),
            in_specs=[
                pl.BlockSpec((step_rows, D_MODEL // 2), row),
                pl.BlockSpec(memory_space=pl.ANY),
                pl.BlockSpec(memory_space=pl.ANY),
                pl.BlockSpec(memory_space=pl.ANY),
            ],
            out_specs=pl.BlockSpec((step_rows, D_MODEL // 2), row),
            scratch_shapes=[
                pltpu.VMEM((WEIGHT_SLOTS, D_MODEL, EXPERT_FF), F32),
                pltpu.VMEM((WEIGHT_SLOTS, D_MODEL, EXPERT_FF), F32),
                pltpu.VMEM((WEIGHT_SLOTS, EXPERT_FF, D_MODEL), F32),
                pltpu.VMEM((D_MODEL, EXPERT_FF), BF16),
                pltpu.VMEM((D_MODEL, EXPERT_FF), BF16),
                pltpu.VMEM((EXPERT_FF, D_MODEL), BF16),
                pltpu.SemaphoreType.DMA((WEIGHT_SLOTS, 3)),
            ],
        ),
        out_shape=jax.ShapeDtypeStruct((n_blocks * MOE_BM, D_MODEL // 2), jnp.int32),
        compiler_params=pltpu.CompilerParams(
            dimension_semantics=("arbitrary",), vmem_limit_bytes=VMEM_LIMIT),
        name="moe_experts",
    )(block_e, rows_valid, next_e, slot, first_e, xd, w_gate, w_up, w_down)


def _sc_worker_id():
    return lax.axis_index("s") * SC_CORES + lax.axis_index("c")


def _sc_dispatch(hp, hs, dest_p, dest_s, n_rows):
    tp, width = hp.shape
    per_w = tp // SC_WORKERS
    n_ch = per_w // DISP_CH
    n_sw = hs.shape[0] // SAMPLE_CH
    mesh = plsc.VectorSubcoreMesh(core_axis_name="c", subcore_axis_name="s")

    @functools.partial(
        pl.kernel, mesh=mesh,
        out_type=jax.ShapeDtypeStruct((n_rows, width), jnp.int32),
        scratch_types=[
            pltpu.VMEM((2, n_ch, DISP_CH), jnp.int32),
            pltpu.VMEM((2, 1, SAMPLE_CH), jnp.int32),
            pltpu.VMEM((2, DISP_CH, width), jnp.int32),
            pltpu.SemaphoreType.DMA((2,)),
            pltpu.SemaphoreType.DMA((2,)),
        ],
        name="sc_dispatch",
    )
    def k(hp_hbm, hs_hbm, dp_hbm, ds_hbm, xd_hbm, idx_v, idxs_v, bufs, rsem, wsem):
        wid = _sc_worker_id()
        base = wid * per_w
        for kk in range(2):
            pltpu.sync_copy(dp_hbm.at[kk, wid], idx_v.at[kk])
        reads = [pltpu.make_async_copy(hp_hbm.at[pl.ds(base + j * DISP_CH, DISP_CH)],
                                       bufs.at[j % 2], rsem.at[j % 2]) for j in range(n_ch)]
        reads[0].start()
        for j in range(n_ch):
            if j + 1 < n_ch:
                reads[j + 1].start()
            reads[j].wait()
            writes = [pltpu.make_async_copy(bufs.at[j % 2], xd_hbm.at[idx_v.at[kk, j]], wsem.at[kk])
                      for kk in range(2)]
            for w in writes:
                w.start()
            for w in writes:
                w.wait()

        @pl.when(wid < n_sw)
        def _():
            rows = bufs.at[0, pl.ds(0, SAMPLE_CH)]
            for kk in range(2):
                pltpu.sync_copy(ds_hbm.at[kk, wid], idxs_v.at[kk])
            pltpu.sync_copy(hs_hbm.at[pl.ds(wid * SAMPLE_CH, SAMPLE_CH)], rows)
            for kk in range(2):
                pltpu.sync_copy(rows, xd_hbm.at[idxs_v.at[kk, 0]])

    return k(hp, hs, dest_p, dest_s)


def _sc_combine_gather(yd, dest_p, dest_s, tp, ts, after=()):
    width = yd.shape[1]
    per_w = tp // SC_WORKERS
    n_ch = per_w // COMB_CH
    n_sw = ts // SAMPLE_CH
    mesh = plsc.VectorSubcoreMesh(core_axis_name="c", subcore_axis_name="s")

    @functools.partial(
        pl.kernel, mesh=mesh,
        out_type=jax.ShapeDtypeStruct((2, tp + ts, width), yd.dtype),
        scratch_types=[
            pltpu.VMEM((2, n_ch, COMB_CH), jnp.int32),
            pltpu.VMEM((2, 1, SAMPLE_CH), jnp.int32),
            pltpu.VMEM((2, COMB_CH, width), yd.dtype),
            pltpu.SemaphoreType.DMA((2,)),
            pltpu.SemaphoreType.DMA((2,)),
        ],
        name="sc_combine_gather",
    )
    def k(*refs):
        yd_hbm, dp_hbm, ds_hbm = refs[:3]
        g_hbm, idx_v, idxs_v, bufs, gsem, wsem = refs[3 + len(after):]
        wid = _sc_worker_id()
        base = wid * per_w
        for kk in range(2):
            pltpu.sync_copy(dp_hbm.at[kk, wid], idx_v.at[kk])
        items = [(kk, j) for kk in range(2) for j in range(n_ch)]
        gathers = [pltpu.make_async_copy(yd_hbm.at[idx_v.at[kk, j]], bufs.at[n % 2], gsem.at[n % 2])
                   for n, (kk, j) in enumerate(items)]
        gathers[0].start()
        for n, (kk, j) in enumerate(items):
            if n + 1 < len(items):
                gathers[n + 1].start()
            gathers[n].wait()
            w = pltpu.make_async_copy(bufs.at[n % 2], g_hbm.at[kk, pl.ds(base + j * COMB_CH, COMB_CH)],
                                      wsem.at[n % 2])
            w.start()
            w.wait()

        @pl.when(wid < n_sw)
        def _():
            for kk in range(2):
                pltpu.sync_copy(ds_hbm.at[kk, wid], idxs_v.at[kk])
            for kk in range(2):
                rows = bufs.at[kk, pl.ds(0, SAMPLE_CH)]
                pltpu.sync_copy(yd_hbm.at[idxs_v.at[kk, 0]], rows)
                pltpu.sync_copy(rows, g_hbm.at[kk, pl.ds(tp + wid * SAMPLE_CH, SAMPLE_CH)])

    return k(yd, dest_p, dest_s, *after)


def _combine_kernel(x1_ref, g_ref, route_t_ref, x2_ref):
    fields = route_t_ref[...]
    tm = fields.shape[1]
    cols = jnp.transpose(jnp.concatenate([fields, jnp.zeros((LANES - ROUTE_FIELDS, tm), F32)], axis=0))
    w1 = cols[:, 2:3]
    w2 = cols[:, 3:4]
    x2_ref[...] = x1_ref[...] + _unpack_bf16_pairs(g_ref[0]) * w1 + _unpack_bf16_pairs(g_ref[1]) * w2


def _combine(x1, g, route_t, row0, tm):
    t = x1.shape[0]
    blk0 = row0 // tm
    row = lambda i: (i, 0)
    return pl.pallas_call(
        _combine_kernel,
        grid=(t // tm,),
        in_specs=[
            pl.BlockSpec((tm, D_MODEL), row),
            pl.BlockSpec((2, tm, D_MODEL // 2), lambda i: (0, blk0 + i, 0)),
            pl.BlockSpec((ROUTE_FIELDS, tm), lambda i: (0, i)),
        ],
        out_specs=pl.BlockSpec((tm, D_MODEL), row),
        out_shape=jax.ShapeDtypeStruct((t, D_MODEL), F32),
        compiler_params=pltpu.CompilerParams(
            dimension_semantics=("arbitrary",), vmem_limit_bytes=VMEM_LIMIT),
        name="combine",
    )(x1, g, route_t)


def _dest_layout(dest, workers, chunk):
    t = dest.shape[1]
    return dest.reshape(2, workers, t // (workers * chunk), chunk)


def _hier_moe(l, h2p, h2s, route_tp, route_ts, counts, w_gate, w_up, w_down, after=()):
    tp, ts = h2p.shape[0], h2s.shape[0]
    n_assign = 2 * (tp + ts)
    n_blocks = -(-n_assign // MOE_BM) + N_EXPERTS
    n_blocks = -(-n_blocks // MOE_STEP_BLOCKS) * MOE_STEP_BLOCKS
    pcounts = (counts + MOE_BM - 1) // MOE_BM * MOE_BM
    pends = jnp.cumsum(pcounts)
    poffsets = pends - pcounts
    starts = jnp.arange(n_blocks, dtype=jnp.int32) * MOE_BM
    block_e = jnp.minimum(jnp.sum((pends[None, :] <= starts[:, None]).astype(jnp.int32), axis=1),
                          N_EXPERTS - 1)
    experts = jnp.arange(N_EXPERTS, dtype=jnp.int32)

    def lookup(table, idx):
        return jnp.sum(jnp.where(idx[..., None] == experts, table, 0), axis=-1)

    rows_valid = jnp.clip(lookup(poffsets + counts, block_e) - starts, 0, MOE_BM).astype(jnp.int32)
    used = counts > 0
    last_e = jnp.max(jnp.where(used, jnp.arange(N_EXPERTS, dtype=jnp.int32), 0))
    block_e = jnp.where(rows_valid > 0, block_e, last_e).astype(jnp.int32)
    place = jnp.cumsum(used.astype(jnp.int32)) - 1
    by_place = jnp.sum(jnp.where(used[None, :] & (place[None, :] == experts[:, None]), experts[None, :], 0),
                       axis=1)
    n_used = jnp.sum(used.astype(jnp.int32))

    def at_place(p):
        return jnp.where(p < n_used, lookup(by_place, jnp.minimum(p, N_EXPERTS - 1)), -1).astype(jnp.int32)

    ahead_of = at_place(place + (WEIGHT_SLOTS - 1))
    next_e = lookup(ahead_of, block_e)
    slot = lookup(place % WEIGHT_SLOTS, block_e)
    n_steps_used = -(-(pends[-1] // MOE_BM) // MOE_STEP_BLOCKS)
    first_e = jnp.concatenate([at_place(jnp.arange(WEIGHT_SLOTS - 1, dtype=jnp.int32)),
                               n_steps_used.reshape(1).astype(jnp.int32)])

    def dest_of(route_t):
        return lookup(poffsets, route_t[0:2].astype(jnp.int32)) + route_t[4:6].astype(jnp.int32)

    dest_p, dest_s = dest_of(route_tp), dest_of(route_ts)
    n_sw = ts // SAMPLE_CH
    xd = _sc_dispatch(h2p, h2s, _dest_layout(dest_p, SC_WORKERS, DISP_CH),
                      _dest_layout(dest_s, n_sw, SAMPLE_CH), n_blocks * MOE_BM)
    yd = _moe_experts(l, block_e, rows_valid, next_e, slot, first_e, xd, w_gate, w_up, w_down)
    return _sc_combine_gather(yd, _dest_layout(dest_p, SC_WORKERS, COMB_CH),
                              _dest_layout(dest_s, n_sw, SAMPLE_CH), tp, ts, after)


def kernel(x_prompt, x_sample, state_pool, cache_k_win, cache_v_win, norm_attn_g, w_in, pool_w, pool_scale, q_norm_g, k_norm_g, attn_sinks, w_out, norm_ffn_g, router_group_w, router_group_b, router_expert_w, router_expert_b, w_gate, w_up, w_down):
    n_p, t_p, d = x_prompt.shape
    n_s, t_s, _ = x_sample.shape
    depth = w_in.shape[0]
    lw_s = cache_k_win.shape[2]
    assert t_s == 1 and lw_s == WINDOW and d == D_MODEL
    assert t_p % TM_PROJ == 0 and t_p >= WINDOW
    past_len = 16384

    seg = jnp.arange(256) // HEAD_DIM
    bd = jnp.where(seg[:, None] == seg[None, :], 1.0 / HEAD_DIM, 0.0).astype(BF16)
    slopes = jnp.exp2(-8.0 * jnp.arange(1, N_HEADS + 1, dtype=F32) / N_HEADS)
    bias_p = _prompt_bias_t()
    dist_s = (WINDOW - 1) - jnp.arange(WINDOW, dtype=F32)
    bias_s = -slopes[:, None] * dist_s[None, :]

    wp = jnp.zeros((depth, 2, 256, 256), F32)
    for p in range(2):
        wp = wp.at[:, p, :POOL_GC, :POOL_GC].set(pool_w[:, 2 * p])
        wp = wp.at[:, p, POOL_GC:, POOL_GC:].set(pool_w[:, 2 * p + 1])
    assert GROUP_LANE0 == N_EXPERTS
    lane_pad = LANES - N_EXPERTS - N_EXPERT_GROUPS
    wr = jnp.concatenate([router_expert_w, router_group_w, jnp.zeros((depth, D_MODEL, lane_pad), F32)], axis=-1)
    br = jnp.concatenate([router_expert_b, router_group_b, jnp.zeros((depth, lane_pad), F32)],
                         axis=-1).reshape(depth, 1, LANES)
    lp = dict(
        w_in=w_in.astype(BF16),
        w_out=w_out.astype(BF16),
        g_attn=norm_attn_g.reshape(depth, 1, D_MODEL),
        g_ffn=norm_ffn_g.reshape(depth, 1, D_MODEL),
        qg=(jnp.tile(q_norm_g, (1, N_HEADS)) * ATTN_SCALE).reshape(depth, 1, Q_W),
        kg=jnp.tile(k_norm_g, (1, N_KV_HEADS)).reshape(depth, 1, KV_W),
        wp=wp.astype(BF16),
        ps=pool_scale.reshape(depth, 1, POOL_W),
        wr=wr.astype(BF16),
        br=br,
        state=state_pool,
        ck=cache_k_win.reshape(depth, n_s, lw_s, KV_W),
        cv=cache_v_win.reshape(depth, n_s, lw_s, KV_W),
    )

    xp = x_prompt.reshape(n_p * t_p, D_MODEL)
    xs = x_sample.reshape(n_s, D_MODEL)
    lw_p = min(WINDOW, t_p)
    pool_p, kp_new, vp_new = [], [], []
    sample_state = [lp["state"], lp["ck"], lp["cv"]]
    zero_cnt = jnp.zeros((N_EXPERTS, 1), F32)
    pending = None
    for l in range(depth):
        sinks = attn_sinks[l]
        outs = _proj_pool_prompt(
            l, xp if pending is None else pending, n_p, t_p,
            lp["g_attn"], lp["w_in"], lp["qg"], lp["kg"], bd, lp["wp"], lp["ps"])
        if pending is not None:
            xp, outs = outs[0], outs[1:]
        pool_o, q, k, vt, utail, ktail, vtail = outs
        attn_o = _attn_prompt(q, k, vt, bias_p, sinks, n_p, t_p)
        x1p, h2p, route_tp, cnt_p = _merge_router(
            l, pool_o, attn_o, xp, lp["w_out"], lp["g_ffn"], lp["wr"], lp["br"], zero_cnt, TM_MERGE)
        pool_p.append(utail[:, 16 - POOL_STATE:, :])
        kp_new.append(ktail)
        vp_new.append(vtail)
        pool_so, attn_so, *sample_state = _sample_mixer(
            l, depth, xs, lp["g_attn"], lp["w_in"], lp["qg"], lp["kg"], bd, lp["wp"], lp["ps"],
            *sample_state, sinks.reshape(N_HEADS, 1), bias_s, past_len)
        x1s, h2s, route_ts, cnt_all = _merge_router(
            l, pool_so, attn_so, xs, lp["w_out"], lp["g_ffn"], lp["wr"], lp["br"], cnt_p, n_s)
        counts = cnt_all[:, 0].astype(jnp.int32)
        after = ()
        if l == depth - 1:
            ks_out = sample_state[1].reshape(depth, n_s, lw_s, N_KV_HEADS, HEAD_DIM)
            vs_out = sample_state[2].reshape(depth, n_s, lw_s, N_KV_HEADS, HEAD_DIM)
            after = (ks_out, vs_out)
        g = _hier_moe(l, h2p, h2s, route_tp, route_ts, counts, w_gate, w_up, w_down, after)
        xs = _combine(x1s, g, route_ts, n_p * t_p, n_s)
        pending = (x1p, g, route_tp)
    xp = _combine(*pending, 0, TM_MERGE)
    return (xp.reshape(n_p, t_p, D_MODEL), xs.reshape(n_s, t_s, D_MODEL),
            jnp.stack(pool_p),
            jnp.stack(kp_new).reshape(depth, n_p, lw_p, N_KV_HEADS, HEAD_DIM),
            jnp.stack(vp_new).reshape(depth, n_p, lw_p, N_KV_HEADS, HEAD_DIM),
            sample_state[0], ks_out, vs_out)
```

```python
import functools

import jax
import jax.numpy as jnp
from jax import lax
from jax.experimental import pallas as pl
from jax.experimental.pallas import tpu as pltpu
from jax.experimental.pallas import tpu_sc as plsc

D_MODEL = 1024
POOL_W = 512
POOL_WINDOWS = (2, 4, 8, 16)
POOL_GC = 128
POOL_STATE = 15
HEAD_DIM = 64
N_HEADS = 8
N_KV_HEADS = 2
GQA_GROUP = 4
Q_W = 512
KV_W = 128
D_IN = POOL_W + Q_W + 2 * KV_W
WINDOW = 128
ATTN_SCALE = HEAD_DIM ** -0.5
N_EXPERT_GROUPS = 4
EXPERTS_PER_GROUP = 8
N_EXPERTS = 32
EXPERT_FF = 512
EPS = 1e-6
PAST_LEN = 16384

LANES = 128
HALO = 32
TM_PROJ = 1024
TM_MERGE = 512
MERGE_CHUNKS = 2
ATTN_QB = 16
MOE_BM = 256
MOE_STEP_BLOCKS = 4
WEIGHT_SLOTS = 3
GROUP_LANE0 = 32
ROUTE_FIELDS = 8
SC_CORES = 2
SC_SUBCORES = 16
SC_WORKERS = SC_CORES * SC_SUBCORES
DISP_CH = 64
COMB_CH = 64
SAMPLE_CH = 32
SC_RING = 3
VMEM_LIMIT = 48 * 1024 * 1024

BF16 = jnp.bfloat16
F32 = jnp.float32


def _pack_bf16_pairs(h):
    w = h.shape[1] // 2
    hi = lax.bitcast_convert_type(h[:, :w].astype(F32), jnp.uint32)
    lo = lax.bitcast_convert_type(h[:, w:].astype(F32), jnp.uint32)
    return lax.bitcast_convert_type(hi | (lo >> 16), jnp.int32)


def _unpack_bf16_pairs(words):
    u = lax.bitcast_convert_type(words, jnp.uint32)
    hi = lax.bitcast_convert_type(u & jnp.uint32(0xFFFF0000), F32)
    lo = lax.bitcast_convert_type(u << 16, F32)
    return jnp.concatenate([hi, lo], axis=-1)


def _segment_mean_sq(a, bd):
    w = a.shape[1]
    return jnp.dot((a * a).astype(BF16), bd[:w, :w], preferred_element_type=F32)


def _rms_bf16(x, g):
    ms = jnp.mean(x * x, axis=-1, keepdims=True)
    return (x * lax.rsqrt(ms + EPS) * g).astype(BF16)


def _qk_norm(q, k, qg, kg, bd):
    qn = []
    for c in range(Q_W // 256):
        qc = q[:, c * 256:(c + 1) * 256]
        qn.append(qc * lax.rsqrt(_segment_mean_sq(qc, bd) + EPS))
    qn = jnp.concatenate(qn, axis=-1) * qg
    kn = k * lax.rsqrt(_segment_mean_sq(k, bd) + EPS) * kg
    return qn, kn


def _project(x, g, w_in, qg, kg, bd):
    z = jnp.dot(_rms_bf16(x, g), w_in, preferred_element_type=F32)
    u = z[:, :POOL_W]
    q = z[:, POOL_W:POOL_W + Q_W]
    k = z[:, POOL_W + Q_W:POOL_W + Q_W + KV_W]
    v = z[:, POOL_W + Q_W + KV_W:]
    qn, kn = _qk_norm(q, k, qg, kg, bd)
    return u, qn, kn, v


def _pool_project(d_groups, wp_ref, ps):
    outs = []
    for p in range(2):
        dp = jnp.concatenate([d_groups[2 * p], d_groups[2 * p + 1]], axis=-1).astype(BF16)
        y = jnp.dot(dp, wp_ref[p], preferred_element_type=F32)
        outs.append(y * ps[:, p * 256:(p + 1) * 256])
    return jnp.concatenate(outs, axis=-1)


def _proj_pool_kernel(x_ref, g_ref, win_ref, qg_ref, kg_ref, bd_ref, wp_ref, ps_ref,
                      pool_ref, q_ref, k_ref, vt_ref, utail_ref, ktail_ref, vtail_ref,
                      ext_ref, sa_ref, sb_ref, zq_ref, *, tm, n_j):
    j = pl.program_id(1)

    @pl.when(j == 0)
    def _():
        ext_ref[0:HALO, :] = jnp.zeros((HALO, POOL_W), F32)

    r = tm + HALO
    h = _rms_bf16(x_ref[...], g_ref[...])
    ext_ref[HALO:r, :] = jnp.dot(h, win_ref[:, 0:POOL_W], preferred_element_type=F32)
    zq_ref[...] = jnp.dot(h, win_ref[:, POOL_W:], preferred_element_type=F32)
    u = ext_ref[HALO:r, :]
    sa_ref[8:r, :] = ext_ref[8:r, :] + ext_ref[7:r - 1, :]
    sb_ref[16:r, 128:] = sa_ref[16:r, 128:] + sa_ref[14:r - 2, 128:]
    sa_ref[24:r, 256:] = sb_ref[24:r, 256:] + sb_ref[20:r - 4, 256:]
    sb_ref[32:r, 384:] = sa_ref[32:r, 384:] + sa_ref[24:r - 8, 384:]
    pos1 = j * tm + lax.broadcasted_iota(jnp.int32, (tm, POOL_GC), 0) + 1
    sums = (sa_ref, sb_ref, sa_ref, sb_ref)
    d_groups = []
    for gi, w in enumerate(POOL_WINDOWS):
        sl = slice(gi * POOL_GC, (gi + 1) * POOL_GC)
        cnt = jnp.minimum(pos1, w).astype(F32)
        d_groups.append(sums[gi][HALO:r, sl] / cnt - u[:, sl])
    pool_ref[...] = _pool_project(d_groups, wp_ref, ps_ref[...]).astype(BF16)
    ext_ref[16:HALO, :] = ext_ref[tm + 16:r, :]

    qn, kn = _qk_norm(zq_ref[:, 0:Q_W], zq_ref[:, Q_W:Q_W + KV_W], qg_ref[...], kg_ref[...], bd_ref[...])
    v = zq_ref[:, Q_W + KV_W:]
    q_ref[...] = qn.astype(BF16)
    k_ref[...] = kn.astype(BF16)
    vt_ref[...] = jnp.transpose(v).astype(BF16)

    @pl.when(j == n_j - 1)
    def _():
        utail_ref[...] = u[tm - 16:, :]
        ktail_ref[...] = kn[tm - WINDOW:, :]
        vtail_ref[...] = v[tm - WINDOW:, :]


def _proj_pool_combine_kernel(x1_ref, gath_ref, route_ref, *rest, tm, n_j):
    x2_ref = rest[7]
    _combine_kernel(x1_ref, gath_ref, route_ref, x2_ref)
    _proj_pool_kernel(x2_ref, *rest[:7], *rest[8:], tm=tm, n_j=n_j)


def _proj_pool_prompt(l, x_in, n_seq, seq, g_attn, w_in, qg, kg, bd, wp, ps):
    tm = TM_PROJ
    n_j = seq // tm
    t = n_seq * seq
    row = lambda b, j: (b * n_j + j, 0)
    lay = lambda b, j: (l, 0, 0)
    fused = isinstance(x_in, tuple)
    if fused:
        kern = _proj_pool_combine_kernel
        x_args = list(x_in)
        x_specs = [pl.BlockSpec((tm, D_MODEL), row),
                   pl.BlockSpec((2, tm, D_MODEL // 2), lambda b, j: (0, b * n_j + j, 0)),
                   pl.BlockSpec((ROUTE_FIELDS, tm), lambda b, j: (0, b * n_j + j))]
        x_out_specs = [pl.BlockSpec((tm, D_MODEL), row)]
        x_out_shape = [jax.ShapeDtypeStruct((t, D_MODEL), F32)]
    else:
        kern = _proj_pool_kernel
        x_args = [x_in]
        x_specs = [pl.BlockSpec((tm, D_MODEL), row)]
        x_out_specs, x_out_shape = [], []
    return pl.pallas_call(
        functools.partial(kern, tm=tm, n_j=n_j),
        grid=(n_seq, n_j),
        in_specs=x_specs + [
            pl.BlockSpec((None, 1, D_MODEL), lay),
            pl.BlockSpec((None, D_MODEL, D_IN), lay),
            pl.BlockSpec((None, 1, Q_W), lay),
            pl.BlockSpec((None, 1, KV_W), lay),
            pl.BlockSpec((256, 256), lambda b, j: (0, 0)),
            pl.BlockSpec((None, 2, 256, 256), lambda b, j: (l, 0, 0, 0)),
            pl.BlockSpec((None, 1, POOL_W), lay),
        ],
        out_specs=x_out_specs + [
            pl.BlockSpec((tm, POOL_W), row),
            pl.BlockSpec((tm, Q_W), row),
            pl.BlockSpec((tm, KV_W), row),
            pl.BlockSpec((KV_W, tm), lambda b, j: (0, b * n_j + j)),
            pl.BlockSpec((None, 16, POOL_W), lambda b, j: (b, 0, 0)),
            pl.BlockSpec((None, WINDOW, KV_W), lambda b, j: (b, 0, 0)),
            pl.BlockSpec((None, WINDOW, KV_W), lambda b, j: (b, 0, 0)),
        ],
        out_shape=x_out_shape + [
            jax.ShapeDtypeStruct((t, POOL_W), BF16),
            jax.ShapeDtypeStruct((t, Q_W), BF16),
            jax.ShapeDtypeStruct((t, KV_W), BF16),
            jax.ShapeDtypeStruct((KV_W, t), BF16),
            jax.ShapeDtypeStruct((n_seq, 16, POOL_W), F32),
            jax.ShapeDtypeStruct((n_seq, WINDOW, KV_W), F32),
            jax.ShapeDtypeStruct((n_seq, WINDOW, KV_W), F32),
        ],
        scratch_shapes=[pltpu.VMEM((tm + HALO, POOL_W), F32)] * 3 + [pltpu.VMEM((tm, Q_W + 2 * KV_W), F32)],
        compiler_params=pltpu.CompilerParams(
            dimension_semantics=("arbitrary", "arbitrary"), vmem_limit_bytes=VMEM_LIMIT),
        name="proj_pool_prompt",
    )(*x_args, g_attn, w_in, qg, kg, bd, wp, ps)


def _attn_kernel(sink_ref, q_ref, kp_ref, kc_ref, vtp_ref, vtc_ref, bias_ref, o_ref, s_ref):
    j = pl.program_id(1)
    kk_all = jnp.concatenate([kp_ref[...], kc_ref[...]], axis=0)
    vt_all = jnp.concatenate([vtp_ref[...], vtc_ref[...]], axis=1)
    from_prev = (lax.broadcasted_iota(jnp.int32, (WINDOW, WINDOW), 0)
                 > lax.broadcasted_iota(jnp.int32, (WINDOW, WINDOW), 1))
    units = [(blk, kv) for blk in range(ATTN_QB) for kv in range(N_KV_HEADS)]

    def scores(n):
        blk, kv = units[n]
        q = q_ref[blk * WINDOW:(blk + 1) * WINDOW, :]
        kk = kk_all[blk * WINDOW:(blk + 2) * WINDOW, kv * HEAD_DIM:(kv + 1) * HEAD_DIM]
        heads = range(kv * GQA_GROUP, (kv + 1) * GQA_GROUP)
        q_rows = jnp.concatenate([q[:, h * HEAD_DIM:(h + 1) * HEAD_DIM] for h in heads], axis=0)
        s_ref[n % 2] = lax.dot_general(kk, q_rows, (((1,), (1,)), ((), ())), preferred_element_type=F32)

    scores(0)
    outs = []
    for n, (blk, kv) in enumerate(units):
        if n + 1 < len(units):
            scores(n + 1)
        vt_kv = vt_all[kv * HEAD_DIM:(kv + 1) * HEAD_DIM, blk * WINDOW:(blk + 2) * WINDOW]
        variant = jnp.minimum(j, 1) if blk == 0 else 1
        for g in range(GQA_GROUP):
            h = kv * GQA_GROUP + g
            s = jnp.where(from_prev, s_ref[n % 2, 0:WINDOW, g * WINDOW:(g + 1) * WINDOW],
                          s_ref[n % 2, WINDOW:, g * WINDOW:(g + 1) * WINDOW]) + bias_ref[variant, h]
            sink = sink_ref[h]
            m = jnp.maximum(jnp.max(s, axis=0, keepdims=True), sink)
            p = jnp.exp(s - m)
            denom = jnp.sum(p, axis=0, keepdims=True) + jnp.exp(sink - m)
            p_keys = jnp.concatenate([jnp.where(from_prev, p, 0.0), jnp.where(from_prev, 0.0, p)], axis=0)
            o_t = jnp.dot(vt_kv, p_keys.astype(BF16), preferred_element_type=F32)
            outs.append(o_t / denom)
        if kv == N_KV_HEADS - 1:
            o_ref[blk * WINDOW:(blk + 1) * WINDOW, :] = jnp.transpose(jnp.concatenate(outs, axis=0)).astype(BF16)
            outs = []


def _attn_prompt(q, k, vt, bias_t, sinks, n_seq, seq):
    tq = ATTN_QB * WINDOW
    nj = seq // tq
    t = n_seq * seq
    cur = lambda b, j: (b * nj + j, 0)
    prev = lambda b, j: (jnp.maximum((b * nj + j) * ATTN_QB - 1, 0), 0)
    cur_t = lambda b, j: (0, b * nj + j)
    prev_t = lambda b, j: (0, jnp.maximum((b * nj + j) * ATTN_QB - 1, 0))
    return pl.pallas_call(
        _attn_kernel,
        grid=(n_seq, nj),
        in_specs=[
            pl.BlockSpec(memory_space=pltpu.SMEM),
            pl.BlockSpec((tq, Q_W), cur),
            pl.BlockSpec((WINDOW, KV_W), prev),
            pl.BlockSpec((tq, KV_W), cur),
            pl.BlockSpec((KV_W, WINDOW), prev_t),
            pl.BlockSpec((KV_W, tq), cur_t),
            pl.BlockSpec((2, N_HEADS, WINDOW, WINDOW), lambda b, j: (0, 0, 0, 0)),
        ],
        out_specs=pl.BlockSpec((tq, Q_W), cur),
        out_shape=jax.ShapeDtypeStruct((t, Q_W), BF16),
        scratch_shapes=[pltpu.VMEM((2, 2 * WINDOW, GQA_GROUP * WINDOW), F32)],
        compiler_params=pltpu.CompilerParams(
            dimension_semantics=("arbitrary", "arbitrary"), vmem_limit_bytes=VMEM_LIMIT),
        name="attn_prompt",
    )(sinks, q, k, k, vt, vt, bias_t)


def _prompt_bias_t():
    r = jnp.arange(WINDOW, dtype=jnp.int32)[None, :]
    c = jnp.arange(WINDOW, dtype=jnp.int32)[:, None]
    from_prev = c > r
    dist = r - c + jnp.where(from_prev, WINDOW, 0)
    slopes = jnp.exp2(-8.0 * jnp.arange(1, N_HEADS + 1, dtype=F32) / N_HEADS)
    later = -slopes[:, None, None] * dist.astype(F32)[None]
    first = jnp.where(from_prev[None], -jnp.inf, later)
    return jnp.stack([first, later])


def _sample_kernel(x_ref, g_ref, win_ref, qg_ref, kg_ref, bd_ref, wp_ref, ps_ref,
                   st_ref, ck_ref, cv_ref, sink_ref, bias_ref, perm_ref,
                   pool_ref, attn_ref, pst_ref, kc_ref, vc_ref, *, ns, pos0):
    u, qn, kn, v = _project(x_ref[...], g_ref[...], win_ref[...], qg_ref[...], kg_ref[...], bd_ref[...])
    pst_ref[:, 0:POOL_STATE - 1, :] = st_ref[:, 1:POOL_STATE, :]
    kc_ref[:, 0:WINDOW - 1, :] = ck_ref[:, 1:WINDOW, :]
    vc_ref[:, 0:WINDOW - 1, :] = cv_ref[:, 1:WINDOW, :]
    for n in range(ns):
        pst_ref[n, POOL_STATE - 1:POOL_STATE, :] = u[n:n + 1, :]
        kc_ref[n, WINDOW - 1:WINDOW, :] = kn[n:n + 1, :]
        vc_ref[n, WINDOW - 1:WINDOW, :] = v[n:n + 1, :]

    d_groups = []
    for gi, w in enumerate(POOL_WINDOWS):
        lo = gi * POOL_GC
        acc = u[:, lo:lo + POOL_GC]
        for back in range(1, w):
            acc = acc + st_ref[:, POOL_STATE - back, lo:lo + POOL_GC]
        d_groups.append(acc / float(min(pos0 + 1, w)) - u[:, lo:lo + POOL_GC])
    pool_ref[...] = _pool_project(d_groups, wp_ref, ps_ref[...]).astype(BF16)

    zeros = jnp.zeros((ns, HEAD_DIM), F32)
    stacked = []
    for h in range(N_HEADS):
        piece = qn[:, h * HEAD_DIM:(h + 1) * HEAD_DIM]
        pair = [piece, zeros] if h < GQA_GROUP else [zeros, piece]
        stacked.append(jnp.concatenate(pair, axis=-1))
    q_hn = jnp.concatenate(stacked, axis=0).astype(BF16)
    q_nh = jnp.dot(perm_ref[0], q_hn, preferred_element_type=F32).astype(BF16)

    keys = kc_ref[...].reshape(ns * WINDOW, KV_W).astype(BF16)
    vals = vc_ref[...].reshape(ns * WINDOW, KV_W).astype(BF16)
    s_all = lax.dot_general(q_nh, keys, (((1,), (1,)), ((), ())), preferred_element_type=F32)
    sink = sink_ref[...]
    bias = bias_ref[...]
    zero_blk = jnp.zeros((N_HEADS, WINDOW), F32)
    p_rows = []
    for n in range(ns):
        s = s_all[n * N_HEADS:(n + 1) * N_HEADS, n * WINDOW:(n + 1) * WINDOW] + bias
        m = jnp.maximum(jnp.max(s, axis=-1, keepdims=True), sink)
        p = jnp.exp(s - m)
        denom = jnp.sum(p, axis=-1, keepdims=True) + jnp.exp(sink - m)
        p_rows.append(jnp.concatenate([zero_blk] * n + [p / denom] + [zero_blk] * (ns - 1 - n), axis=-1))
    p_blockdiag = jnp.concatenate(p_rows, axis=0).astype(BF16)
    o_nh = jnp.dot(p_blockdiag, vals, preferred_element_type=F32).astype(BF16)
    o_hn = jnp.dot(perm_ref[1], o_nh, preferred_element_type=F32)
    pieces = []
    for h in range(N_HEADS):
        kv = h // GQA_GROUP
        pieces.append(o_hn[h * ns:(h + 1) * ns, kv * HEAD_DIM:(kv + 1) * HEAD_DIM])
    attn_ref[...] = jnp.concatenate(pieces, axis=-1).astype(BF16)


def _sample_mixer(l, depth, xs, g_attn, w_in, qg, kg, bd, wp, ps, state, ck, cv, sink8, bias_s, pos0):
    n = xs.shape[0]
    ns = 32
    row = lambda i: (i, 0)
    lay = lambda i: (l, 0, 0)
    src = jnp.arange(ns * N_HEADS)
    perm = (((src % N_HEADS) * ns + src // N_HEADS)[:, None] == src[None, :]).astype(BF16)
    perms = jnp.stack([perm, perm.T])
    return pl.pallas_call(
        functools.partial(_sample_kernel, ns=ns, pos0=pos0),
        grid=(n // ns,),
        input_output_aliases={8: 2, 9: 3, 10: 4},
        in_specs=[
            pl.BlockSpec((ns, D_MODEL), row),
            pl.BlockSpec((None, 1, D_MODEL), lay),
            pl.BlockSpec((None, D_MODEL, D_IN), lay),
            pl.BlockSpec((None, 1, Q_W), lay),
            pl.BlockSpec((None, 1, KV_W), lay),
            pl.BlockSpec((256, 256), lambda i: (0, 0)),
            pl.BlockSpec((None, 2, 256, 256), lambda i: (l, 0, 0, 0)),
            pl.BlockSpec((None, 1, POOL_W), lay),
            pl.BlockSpec((None, ns, POOL_STATE, POOL_W), lambda i: (l, i, 0, 0)),
            pl.BlockSpec((None, ns, WINDOW, KV_W), lambda i: (l, i, 0, 0)),
            pl.BlockSpec((None, ns, WINDOW, KV_W), lambda i: (l, i, 0, 0)),
            pl.BlockSpec((N_HEADS, 1), lambda i: (0, 0)),
            pl.BlockSpec((N_HEADS, WINDOW), lambda i: (0, 0)),
            pl.BlockSpec((2, ns * N_HEADS, ns * N_HEADS), lambda i: (0, 0, 0)),
        ],
        out_specs=[
            pl.BlockSpec((ns, POOL_W), row),
            pl.BlockSpec((ns, Q_W), row),
            pl.BlockSpec((None, ns, POOL_STATE, POOL_W), lambda i: (l, i, 0, 0)),
            pl.BlockSpec((None, ns, WINDOW, KV_W), lambda i: (l, i, 0, 0)),
            pl.BlockSpec((None, ns, WINDOW, KV_W), lambda i: (l, i, 0, 0)),
        ],
        out_shape=[
            jax.ShapeDtypeStruct((n, POOL_W), BF16),
            jax.ShapeDtypeStruct((n, Q_W), BF16),
            jax.ShapeDtypeStruct((depth, n, POOL_STATE, POOL_W), F32),
            jax.ShapeDtypeStruct((depth, n, WINDOW, KV_W), F32),
            jax.ShapeDtypeStruct((depth, n, WINDOW, KV_W), F32),
        ],
        compiler_params=pltpu.CompilerParams(
            dimension_semantics=("arbitrary",), vmem_limit_bytes=VMEM_LIMIT),
        name="sample_mixer",
    )(xs, g_attn, w_in, qg, kg, bd, wp, ps, state, ck, cv, sink8, bias_s, perms)


def _merge_router_kernel(pool_ref, attn_ref, x_ref, wout_ref, g_ref, wr_ref, br_ref, utri_ref, cin_ref,
                         x1_ref, h2_ref, route_t_ref, cnt_ref, y_ref, lg_ref):
    i = pl.program_id(0)

    @pl.when(i == 0)
    def _():
        cnt_ref[...] = cin_ref[...]

    tm = x_ref.shape[0]
    rc = tm // MERGE_CHUNKS
    chunks = [slice(ci * rc, (ci + 1) * rc) for ci in range(MERGE_CHUNKS)]
    for rows in chunks:
        y_ref[rows, :] = (jnp.dot(pool_ref[rows, :], wout_ref[0:POOL_W, :], preferred_element_type=F32)
                          + jnp.dot(attn_ref[rows, :], wout_ref[POOL_W:, :], preferred_element_type=F32))
    for rows in chunks:
        x1 = x_ref[rows, :] + y_ref[rows, :]
        x1_ref[rows, :] = x1
        h2 = _rms_bf16(x1, g_ref[...])
        h2_ref[rows, :] = _pack_bf16_pairs(h2)
        lg_ref[rows, :] = jnp.dot(h2, wr_ref[...], preferred_element_type=F32) + br_ref[...]
    logits = lg_ref[...]

    lt = jnp.transpose(logits)
    sub = lax.broadcasted_iota(jnp.int32, (EXPERTS_PER_GROUP, tm), 0)
    neg = -jnp.inf
    big = jnp.int32(EXPERTS_PER_GROUP)
    gl = jnp.where(sub < N_EXPERT_GROUPS, lt[GROUP_LANE0:GROUP_LANE0 + EXPERTS_PER_GROUP, :], neg)
    gmax = jnp.max(gl, axis=0, keepdims=True)
    grp = jnp.min(jnp.where(gl == gmax, sub, big), axis=0, keepdims=True)
    g_w = 1.0 / jnp.sum(jnp.exp(gl - gmax), axis=0, keepdims=True)
    el = lt[(N_EXPERT_GROUPS - 1) * EXPERTS_PER_GROUP:N_EXPERT_GROUPS * EXPERTS_PER_GROUP, :]
    for gi in range(N_EXPERT_GROUPS - 2, -1, -1):
        el = jnp.where(grp == gi, lt[gi * EXPERTS_PER_GROUP:(gi + 1) * EXPERTS_PER_GROUP, :], el)
    v1 = jnp.max(el, axis=0, keepdims=True)
    i1 = jnp.min(jnp.where(el == v1, sub, big), axis=0, keepdims=True)
    el2 = jnp.where(sub == i1, neg, el)
    v2 = jnp.max(el2, axis=0, keepdims=True)
    i2 = jnp.min(jnp.where(el2 == v2, sub, big), axis=0, keepdims=True)
    e21 = jnp.exp(v2 - v1)
    w1 = g_w / (1.0 + e21)
    w2 = g_w * e21 / (1.0 + e21)
    e1 = grp * EXPERTS_PER_GROUP + i1
    e2 = grp * EXPERTS_PER_GROUP + i2

    esub = lax.broadcasted_iota(jnp.int32, (N_EXPERTS, tm), 0)
    oh1 = esub == e1
    oh2 = esub == e2
    c = jnp.where(oh1 | oh2, 1.0, 0.0)
    prefix = jnp.dot(c.astype(BF16), utri_ref[...], preferred_element_type=F32) + cnt_ref[...]
    r1 = jnp.sum(jnp.where(oh1, prefix, 0.0), axis=0, keepdims=True)
    r2 = jnp.sum(jnp.where(oh2, prefix, 0.0), axis=0, keepdims=True)
    cnt_ref[...] = cnt_ref[...] + jnp.sum(c, axis=1, keepdims=True)

    fields = jnp.zeros((ROUTE_FIELDS, tm), F32)
    for idx, val in enumerate((e1.astype(F32), e2.astype(F32), w1, w2, r1, r2)):
        fields = jnp.where(sub == idx, val, fields)
    route_t_ref[...] = fields


def _merge_router(l, pool, attn, x2d, w_out, g_ffn, wr, br, cnt_in, tm):
    t = x2d.shape[0]
    utri = (jnp.arange(tm)[:, None] < jnp.arange(tm)[None, :]).astype(BF16)
    row = lambda i: (i, 0)
    lay = lambda i: (l, 0, 0)
    return pl.pallas_call(
        _merge_router_kernel,
        grid=(t // tm,),
        in_specs=[
            pl.BlockSpec((tm, POOL_W), row),
            pl.BlockSpec((tm, Q_W), row),
            pl.BlockSpec((tm, D_MODEL), row),
            pl.BlockSpec((None, D_MODEL, D_MODEL), lay),
            pl.BlockSpec((None, 1, D_MODEL), lay),
            pl.BlockSpec((None, D_MODEL, LANES), lay),
            pl.BlockSpec((None, 1, LANES), lay),
            pl.BlockSpec((tm, tm), lambda i: (0, 0)),
            pl.BlockSpec((N_EXPERTS, 1), lambda i: (0, 0)),
        ],
        out_specs=[
            pl.BlockSpec((tm, D_MODEL), row),
            pl.BlockSpec((tm, D_MODEL // 2), row),
            pl.BlockSpec((ROUTE_FIELDS, tm), lambda i: (0, i)),
            pl.BlockSpec((N_EXPERTS, 1), lambda i: (0, 0)),
        ],
        out_shape=[
            jax.ShapeDtypeStruct((t, D_MODEL), F32),
            jax.ShapeDtypeStruct((t, D_MODEL // 2), jnp.int32),
            jax.ShapeDtypeStruct((ROUTE_FIELDS, t), F32),
            jax.ShapeDtypeStruct((N_EXPERTS, 1), F32),
        ],
        scratch_shapes=[pltpu.VMEM((tm, D_MODEL), F32), pltpu.VMEM((tm, LANES), F32)],
        compiler_params=pltpu.CompilerParams(
            dimension_semantics=("arbitrary",), vmem_limit_bytes=VMEM_LIMIT),
        name="merge_router",
    )(pool, attn, x2d, w_out, g_ffn, wr, br, utri, cnt_in)


def _moe_kernel(be_ref, rv_ref, nx_ref, sl_ref, first_ref, xd_ref, wg_hbm, wu_hbm, wd_hbm, yd_ref,
                wg_f, wu_f, wd_f, wg_s, wu_s, wd_s, sem, *, layer):
    step = pl.program_id(0)

    def weight_copies(e, s):
        return [pltpu.make_async_copy(w_hbm.at[layer, e], w_f.at[s], sem.at[s, n])
                for n, (w_hbm, w_f) in enumerate(((wg_hbm, wg_f), (wu_hbm, wu_f), (wd_hbm, wd_f)))]

    @pl.when(step == 0)
    def _():
        for s in range(WEIGHT_SLOTS - 1):
            @pl.when(first_ref[s] >= 0)
            def _():
                for c in weight_copies(first_ref[s], s):
                    c.start()

    def enter_expert(i):
        expert, slot = be_ref[i], sl_ref[i]

        @pl.when((i == 0) | (expert != be_ref[jnp.maximum(i - 1, 0)]))
        def _():
            for c in weight_copies(expert, slot):
                c.wait()

            @pl.when(nx_ref[i] >= 0)
            def _():
                for c in weight_copies(nx_ref[i], lax.rem(slot + WEIGHT_SLOTS - 1, WEIGHT_SLOTS)):
                    c.start(priority=1)

            wg_s[...] = wg_f[slot].astype(BF16)
            wu_s[...] = wu_f[slot].astype(BF16)
            wd_s[...] = wd_f[slot].astype(BF16)

    def experts_on(row0, n_rows, rows_valid):
        rows = pl.ds(row0, n_rows)
        row = lax.broadcasted_iota(jnp.int32, (n_rows, D_MODEL // 2), 0)
        x = _unpack_bf16_pairs(jnp.where(row < rows_valid, xd_ref[rows, :], 0)).astype(BF16)
        gate = jnp.dot(x, wg_s[...], preferred_element_type=F32)
        up = jnp.dot(x, wu_s[...], preferred_element_type=F32)
        act = (gate * jax.nn.sigmoid(gate) * up).astype(BF16)
        y = jnp.dot(act, wd_s[...], preferred_element_type=F32)
        yd_ref[rows, :] = _pack_bf16_pairs(y.astype(BF16))

    def single_block(i, row0):
        enter_expert(i)

        @pl.when(rv_ref[i] > 0)
        def _():
            experts_on(row0, MOE_BM, rv_ref[i])

        @pl.when(rv_ref[i] <= 0)
        def _():
            yd_ref[pl.ds(row0, MOE_BM), :] = jnp.zeros((MOE_BM, D_MODEL // 2), jnp.int32)

    @pl.when(rv_ref[step * MOE_STEP_BLOCKS] > 0)
    def _():
        for pair in range(MOE_STEP_BLOCKS // 2):
            ia = step * MOE_STEP_BLOCKS + 2 * pair
            ib = ia + 1
            row0 = 2 * pair * MOE_BM
            same = (be_ref[ib] == be_ref[ia]) & (rv_ref[ib] > 0)

            @pl.when(same)
            def _():
                enter_expert(ia)
                experts_on(row0, 2 * MOE_BM, MOE_BM + rv_ref[ib])

            @pl.when(jnp.logical_not(same))
            def _():
                single_block(ia, row0)
                single_block(ib, row0 + MOE_BM)


def _moe_experts(l, block_e, rows_valid, next_e, slot, first_e, xd, w_gate, w_up, w_down):
    n_blocks = xd.shape[0] // MOE_BM
    step_rows = MOE_STEP_BLOCKS * MOE_BM
    row = lambda i, be, rv, nx, sl, fe: (jnp.minimum(i, fe[WEIGHT_SLOTS - 1] - 1), 0)
    return pl.pallas_call(
        functools.partial(_moe_kernel, layer=l),
        grid_spec=pltpu.PrefetchScalarGridSpec(
            num_scalar_prefetch=5,
            grid=(n_blocks // MOE_STEP_BLOCKS,),
            in_specs=[
                pl.BlockSpec((step_rows, D_MODEL // 2), row),
                pl.BlockSpec(memory_space=pl.ANY),
                pl.BlockSpec(memory_space=pl.ANY),
                pl.BlockSpec(memory_space=pl.ANY),
            ],
            out_specs=pl.BlockSpec((step_rows, D_MODEL // 2), row),
            scratch_shapes=[
                pltpu.VMEM((WEIGHT_SLOTS, D_MODEL, EXPERT_FF), F32),
                pltpu.VMEM((WEIGHT_SLOTS, D_MODEL, EXPERT_FF), F32),
                pltpu.VMEM((WEIGHT_SLOTS, EXPERT_FF, D_MODEL), F32),
                pltpu.VMEM((D_MODEL, EXPERT_FF), BF16),
                pltpu.VMEM((D_MODEL, EXPERT_FF), BF16),
                pltpu.VMEM((EXPERT_FF, D_MODEL), BF16),
                pltpu.SemaphoreType.DMA((WEIGHT_SLOTS, 3)),
            ],
        ),
        out_shape=jax.ShapeDtypeStruct((n_blocks * MOE_BM, D_MODEL // 2), jnp.int32),
        compiler_params=pltpu.CompilerParams(
            dimension_semantics=("arbitrary",), vmem_limit_bytes=VMEM_LIMIT),
        name="moe_experts",
    )(block_e, rows_valid, next_e, slot, first_e, xd, w_gate, w_up, w_down)


def _sc_worker_id():
    return lax.axis_index("s") * SC_CORES + lax.axis_index("c")


def _sc_dispatch(hp, hs, dest_p, dest_s, n_rows):
    tp, width = hp.shape
    per_w = tp // SC_WORKERS
    n_ch = per_w // DISP_CH
    n_sw = hs.shape[0] // SAMPLE_CH
    mesh = plsc.VectorSubcoreMesh(core_axis_name="c", subcore_axis_name="s")

    @functools.partial(
        pl.kernel, mesh=mesh,
        out_type=jax.ShapeDtypeStruct((n_rows, width), jnp.int32),
        scratch_types=[
            pltpu.VMEM((2, n_ch, DISP_CH), jnp.int32),
            pltpu.VMEM((2, 1, SAMPLE_CH), jnp.int32),
            pltpu.VMEM((SC_RING, DISP_CH, width), jnp.int32),
            pltpu.SemaphoreType.DMA((SC_RING,)),
            pltpu.SemaphoreType.DMA((SC_RING, 2)),
        ],
        name="sc_dispatch",
    )
    def k(hp_hbm, hs_hbm, dp_hbm, ds_hbm, xd_hbm, idx_v, idxs_v, bufs, rsem, wsem):
        wid = _sc_worker_id()
        base = wid * per_w
        for kk in range(2):
            pltpu.sync_copy(dp_hbm.at[kk, wid], idx_v.at[kk])
        reads = [pltpu.make_async_copy(hp_hbm.at[pl.ds(base + j * DISP_CH, DISP_CH)],
                                       bufs.at[j % SC_RING], rsem.at[j % SC_RING]) for j in range(n_ch)]
        writes = [[pltpu.make_async_copy(bufs.at[j % SC_RING], xd_hbm.at[idx_v.at[kk, j]],
                                         wsem.at[j % SC_RING, kk]) for kk in range(2)] for j in range(n_ch)]
        for j in range(min(SC_RING - 1, n_ch)):
            reads[j].start()
        for j in range(n_ch):
            reads[j].wait()
            for w in writes[j]:
                w.start()
            if j >= 1:
                for w in writes[j - 1]:
                    w.wait()
            if j + SC_RING - 1 < n_ch:
                reads[j + SC_RING - 1].start()
        for w in writes[n_ch - 1]:
            w.wait()

        @pl.when(wid < n_sw)
        def _():
            rows = bufs.at[0, pl.ds(0, SAMPLE_CH)]
            for kk in range(2):
                pltpu.sync_copy(ds_hbm.at[kk, wid], idxs_v.at[kk])
            pltpu.sync_copy(hs_hbm.at[pl.ds(wid * SAMPLE_CH, SAMPLE_CH)], rows)
            for kk in range(2):
                pltpu.sync_copy(rows, xd_hbm.at[idxs_v.at[kk, 0]])

    return k(hp, hs, dest_p, dest_s)


def _sc_combine_gather(yd, dest_p, dest_s, tp, ts):
    width = yd.shape[1]
    per_w = tp // SC_WORKERS
    n_ch = per_w // COMB_CH
    n_sw = ts // SAMPLE_CH
    mesh = plsc.VectorSubcoreMesh(core_axis_name="c", subcore_axis_name="s")

    @functools.partial(
        pl.kernel, mesh=mesh,
        out_type=jax.ShapeDtypeStruct((2, tp + ts, width), yd.dtype),
        scratch_types=[
            pltpu.VMEM((2, n_ch, COMB_CH), jnp.int32),
            pltpu.VMEM((2, 1, SAMPLE_CH), jnp.int32),
            pltpu.VMEM((SC_RING, COMB_CH, width), yd.dtype),
            pltpu.SemaphoreType.DMA((SC_RING,)),
            pltpu.SemaphoreType.DMA((SC_RING,)),
        ],
        name="sc_combine_gather",
    )
    def k(yd_hbm, dp_hbm, ds_hbm, g_hbm, idx_v, idxs_v, bufs, gsem, wsem):
        wid = _sc_worker_id()
        base = wid * per_w
        for kk in range(2):
            pltpu.sync_copy(dp_hbm.at[kk, wid], idx_v.at[kk])
        items = [(kk, j) for kk in range(2) for j in range(n_ch)]
        n_items = len(items)
        gathers = [pltpu.make_async_copy(yd_hbm.at[idx_v.at[kk, j]], bufs.at[n % SC_RING], gsem.at[n % SC_RING])
                   for n, (kk, j) in enumerate(items)]
        outs = [pltpu.make_async_copy(bufs.at[n % SC_RING], g_hbm.at[kk, pl.ds(base + j * COMB_CH, COMB_CH)],
                                      wsem.at[n % SC_RING]) for n, (kk, j) in enumerate(items)]
        for n in range(min(SC_RING - 1, n_items)):
            gathers[n].start()
        for n in range(n_items):
            gathers[n].wait()
            outs[n].start()
            if n >= 1:
                outs[n - 1].wait()
            if n + SC_RING - 1 < n_items:
                gathers[n + SC_RING - 1].start()
        outs[n_items - 1].wait()

        @pl.when(wid < n_sw)
        def _():
            for kk in range(2):
                pltpu.sync_copy(ds_hbm.at[kk, wid], idxs_v.at[kk])
            for kk in range(2):
                rows = bufs.at[kk, pl.ds(0, SAMPLE_CH)]
                pltpu.sync_copy(yd_hbm.at[idxs_v.at[kk, 0]], rows)
                pltpu.sync_copy(rows, g_hbm.at[kk, pl.ds(tp + wid * SAMPLE_CH, SAMPLE_CH)])

    return k(yd, dest_p, dest_s)


def _combine_kernel(x1_ref, g_ref, route_t_ref, x2_ref):
    fields = route_t_ref[...]
    tm = fields.shape[1]
    cols = jnp.transpose(jnp.concatenate([fields, jnp.zeros((LANES - ROUTE_FIELDS, tm), F32)], axis=0))
    w1 = cols[:, 2:3]
    w2 = cols[:, 3:4]
    x2_ref[...] = x1_ref[...] + _unpack_bf16_pairs(g_ref[0]) * w1 + _unpack_bf16_pairs(g_ref[1]) * w2


def _combine(x1, g, route_t, row0, tm):
    t = x1.shape[0]
    blk0 = row0 // tm
    row = lambda i: (i, 0)
    return pl.pallas_call(
        _combine_kernel,
        grid=(t // tm,),
        in_specs=[
            pl.BlockSpec((tm, D_MODEL), row),
            pl.BlockSpec((2, tm, D_MODEL // 2), lambda i: (0, blk0 + i, 0)),
            pl.BlockSpec((ROUTE_FIELDS, tm), lambda i: (0, i)),
        ],
        out_specs=pl.BlockSpec((tm, D_MODEL), row),
        out_shape=jax.ShapeDtypeStruct((t, D_MODEL), F32),
        compiler_params=pltpu.CompilerParams(
            dimension_semantics=("arbitrary",), vmem_limit_bytes=VMEM_LIMIT),
        name="combine",
    )(x1, g, route_t)


def _dest_layout(dest, workers, chunk):
    t = dest.shape[1]
    return dest.reshape(2, workers, t // (workers * chunk), chunk)


def _hier_moe(l, h2p, h2s, route_tp, route_ts, counts, w_gate, w_up, w_down):
    tp, ts = h2p.shape[0], h2s.shape[0]
    n_assign = 2 * (tp + ts)
    n_blocks = -(-n_assign // MOE_BM) + N_EXPERTS
    n_blocks = -(-n_blocks // MOE_STEP_BLOCKS) * MOE_STEP_BLOCKS
    pcounts = (counts + MOE_BM - 1) // MOE_BM * MOE_BM
    pends = jnp.cumsum(pcounts)
    poffsets = pends - pcounts
    starts = jnp.arange(n_blocks, dtype=jnp.int32) * MOE_BM
    block_e = jnp.minimum(jnp.sum((pends[None, :] <= starts[:, None]).astype(jnp.int32), axis=1),
                          N_EXPERTS - 1)
    experts = jnp.arange(N_EXPERTS, dtype=jnp.int32)

    def lookup(table, idx):
        return jnp.sum(jnp.where(idx[..., None] == experts, table, 0), axis=-1)

    rows_valid = jnp.clip(lookup(poffsets + counts, block_e) - starts, 0, MOE_BM).astype(jnp.int32)
    used = counts > 0
    last_e = jnp.max(jnp.where(used, jnp.arange(N_EXPERTS, dtype=jnp.int32), 0))
    block_e = jnp.where(rows_valid > 0, block_e, last_e).astype(jnp.int32)
    place = jnp.cumsum(used.astype(jnp.int32)) - 1
    by_place = jnp.sum(jnp.where(used[None, :] & (place[None, :] == experts[:, None]), experts[None, :], 0),
                       axis=1)
    n_used = jnp.sum(used.astype(jnp.int32))

    def at_place(p):
        return jnp.where(p < n_used, lookup(by_place, jnp.minimum(p, N_EXPERTS - 1)), -1).astype(jnp.int32)

    ahead_of = at_place(place + (WEIGHT_SLOTS - 1))
    next_e = lookup(ahead_of, block_e)
    slot = lookup(place % WEIGHT_SLOTS, block_e)
    n_steps_used = -(-(pends[-1] // MOE_BM) // MOE_STEP_BLOCKS)
    first_e = jnp.concatenate([at_place(jnp.arange(WEIGHT_SLOTS - 1, dtype=jnp.int32)),
                               n_steps_used.reshape(1).astype(jnp.int32)])

    def dest_of(route_t):
        return lookup(poffsets, route_t[0:2].astype(jnp.int32)) + route_t[4:6].astype(jnp.int32)

    dest_p, dest_s = dest_of(route_tp), dest_of(route_ts)
    n_sw = ts // SAMPLE_CH
    xd = _sc_dispatch(h2p, h2s, _dest_layout(dest_p, SC_WORKERS, DISP_CH),
                      _dest_layout(dest_s, n_sw, SAMPLE_CH), n_blocks * MOE_BM)
    yd = _moe_experts(l, block_e, rows_valid, next_e, slot, first_e, xd, w_gate, w_up, w_down)
    return _sc_combine_gather(yd, _dest_layout(dest_p, SC_WORKERS, COMB_CH),
                              _dest_layout(dest_s, n_sw, SAMPLE_CH), tp, ts)


def kernel(x_prompt, x_sample, state_pool, cache_k_win, cache_v_win, norm_attn_g, w_in, pool_w, pool_scale, q_norm_g, k_norm_g, attn_sinks, w_out, norm_ffn_g, router_group_w, router_group_b, router_expert_w, router_expert_b, w_gate, w_up, w_down):
    n_p, t_p, d = x_prompt.shape
    n_s, t_s, _ = x_sample.shape
    depth = w_in.shape[0]
    lw_s = cache_k_win.shape[2]
    assert t_s == 1 and lw_s == WINDOW and d == D_MODEL
    assert t_p % TM_PROJ == 0 and t_p >= WINDOW

    seg = jnp.arange(256) // HEAD_DIM
    bd = jnp.where(seg[:, None] == seg[None, :], 1.0 / HEAD_DIM, 0.0).astype(BF16)
    slopes = jnp.exp2(-8.0 * jnp.arange(1, N_HEADS + 1, dtype=F32) / N_HEADS)
    bias_p = _prompt_bias_t()
    dist_s = (WINDOW - 1) - jnp.arange(WINDOW, dtype=F32)
    bias_s = -slopes[:, None] * dist_s[None, :]

    wp = jnp.zeros((depth, 2, 256, 256), F32)
    for p in range(2):
        wp = wp.at[:, p, :POOL_GC, :POOL_GC].set(pool_w[:, 2 * p])
        wp = wp.at[:, p, POOL_GC:, POOL_GC:].set(pool_w[:, 2 * p + 1])
    assert GROUP_LANE0 == N_EXPERTS
    lane_pad = LANES - N_EXPERTS - N_EXPERT_GROUPS
    wr = jnp.concatenate([router_expert_w, router_group_w, jnp.zeros((depth, D_MODEL, lane_pad), F32)], axis=-1)
    br = jnp.concatenate([router_expert_b, router_group_b, jnp.zeros((depth, lane_pad), F32)],
                         axis=-1).reshape(depth, 1, LANES)
    lp = dict(
        w_in=w_in.astype(BF16),
        w_out=w_out.astype(BF16),
        g_attn=norm_attn_g.reshape(depth, 1, D_MODEL),
        g_ffn=norm_ffn_g.reshape(depth, 1, D_MODEL),
        qg=(jnp.tile(q_norm_g, (1, N_HEADS)) * ATTN_SCALE).reshape(depth, 1, Q_W),
        kg=jnp.tile(k_norm_g, (1, N_KV_HEADS)).reshape(depth, 1, KV_W),
        wp=wp.astype(BF16),
        ps=pool_scale.reshape(depth, 1, POOL_W),
        wr=wr.astype(BF16),
        br=br,
        state=state_pool,
        ck=cache_k_win.reshape(depth, n_s, lw_s, KV_W),
        cv=cache_v_win.reshape(depth, n_s, lw_s, KV_W),
    )

    xp = x_prompt.reshape(n_p * t_p, D_MODEL)
    xs = x_sample.reshape(n_s, D_MODEL)
    lw_p = min(WINDOW, t_p)
    pool_p, kp_new, vp_new = [], [], []
    sample_state = [lp["state"], lp["ck"], lp["cv"]]
    zero_cnt = jnp.zeros((N_EXPERTS, 1), F32)
    pending = None
    for l in range(depth):
        sinks = attn_sinks[l]
        outs = _proj_pool_prompt(
            l, xp if pending is None else pending, n_p, t_p,
            lp["g_attn"], lp["w_in"], lp["qg"], lp["kg"], bd, lp["wp"], lp["ps"])
        if pending is not None:
            xp, outs = outs[0], outs[1:]
        pool_o, q, k, vt, utail, ktail, vtail = outs
        attn_o = _attn_prompt(q, k, vt, bias_p, sinks, n_p, t_p)
        x1p, h2p, route_tp, cnt_p = _merge_router(
            l, pool_o, attn_o, xp, lp["w_out"], lp["g_ffn"], lp["wr"], lp["br"], zero_cnt, TM_MERGE)
        pool_p.append(utail[:, 16 - POOL_STATE:, :])
        kp_new.append(ktail)
        vp_new.append(vtail)
        pool_so, attn_so, *sample_state = _sample_mixer(
            l, depth, xs, lp["g_attn"], lp["w_in"], lp["qg"], lp["kg"], bd, lp["wp"], lp["ps"],
            *sample_state, sinks.reshape(N_HEADS, 1), bias_s, PAST_LEN)
        x1s, h2s, route_ts, cnt_all = _merge_router(
            l, pool_so, attn_so, xs, lp["w_out"], lp["g_ffn"], lp["wr"], lp["br"], cnt_p, n_s)
        counts = cnt_all[:, 0].astype(jnp.int32)
        g = _hier_moe(l, h2p, h2s, route_tp, route_ts, counts, w_gate, w_up, w_down)
        xs = _combine(x1s, g, route_ts, n_p * t_p, n_s)
        pending = (x1p, g, route_tp)
    xp = _combine(*pending, 0, TM_MERGE)
    return (xp.reshape(n_p, t_p, D_MODEL), xs.reshape(n_s, t_s, D_MODEL),
            jnp.stack(pool_p),
            jnp.stack(kp_new).reshape(depth, n_p, lw_p, N_KV_HEADS, HEAD_DIM),
            jnp.stack(vp_new).reshape(depth, n_p, lw_p, N_KV_HEADS, HEAD_DIM),
            sample_state[0],
            sample_state[1].reshape(depth, n_s, lw_s, N_KV_HEADS, HEAD_DIM),
            sample_state[2].reshape(depth, n_s, lw_s, N_KV_HEADS, HEAD_DIM))
```

```python
import functools

import jax
import jax.numpy as jnp
from jax import lax
from jax.experimental import pallas as pl
from jax.experimental.pallas import tpu as pltpu
from jax.experimental.pallas import tpu_sc as plsc

D_MODEL = 1024
POOL_W = 512
POOL_WINDOWS = (2, 4, 8, 16)
POOL_GC = 128
POOL_STATE = 15
HEAD_DIM = 64
N_HEADS = 8
N_KV_HEADS = 2
GQA_GROUP = 4
Q_W = 512
KV_W = 128
D_IN = POOL_W + Q_W + 2 * KV_W
WINDOW = 128
ATTN_SCALE = HEAD_DIM ** -0.5
N_EXPERT_GROUPS = 4
EXPERTS_PER_GROUP = 8
N_EXPERTS = 32
EXPERT_FF = 512
EPS = 1e-6
PAST_LEN = 16384

LANES = 128
HALO = 32
TM_PROJ = 1024
TM_MERGE = 512
MERGE_CHUNKS = 2
ATTN_QB = 16
MOE_BM = 256
MOE_STEP_BLOCKS = 4
WEIGHT_SLOTS = 3
GROUP_LANE0 = 32
ROUTE_FIELDS = 8
SC_CORES = 2
SC_SUBCORES = 16
SC_WORKERS = SC_CORES * SC_SUBCORES
DISP_CH = 64
COMB_CH = 64
SAMPLE_CH = 32
SC_RING = 3
VMEM_LIMIT = 48 * 1024 * 1024

BF16 = jnp.bfloat16
F32 = jnp.float32


def _pack_bf16_pairs(h):
    w = h.shape[1] // 2
    hi = lax.bitcast_convert_type(h[:, :w].astype(F32), jnp.uint32)
    lo = lax.bitcast_convert_type(h[:, w:].astype(F32), jnp.uint32)
    return lax.bitcast_convert_type(hi | (lo >> 16), jnp.int32)


def _unpack_bf16_pairs(words):
    u = lax.bitcast_convert_type(words, jnp.uint32)
    hi = lax.bitcast_convert_type(u & jnp.uint32(0xFFFF0000), F32)
    lo = lax.bitcast_convert_type(u << 16, F32)
    return jnp.concatenate([hi, lo], axis=-1)


def _segment_mean_sq(a, bd):
    w = a.shape[1]
    return jnp.dot((a * a).astype(BF16), bd[:w, :w], preferred_element_type=F32)


def _rms_bf16(x, g):
    ms = jnp.mean(x * x, axis=-1, keepdims=True)
    return (x * lax.rsqrt(ms + EPS) * g).astype(BF16)


def _qk_norm(q, k, qg, kg, bd):
    qn = []
    for c in range(Q_W // 256):
        qc = q[:, c * 256:(c + 1) * 256]
        qn.append(qc * lax.rsqrt(_segment_mean_sq(qc, bd) + EPS))
    qn = jnp.concatenate(qn, axis=-1) * qg
    kn = k * lax.rsqrt(_segment_mean_sq(k, bd) + EPS) * kg
    return qn, kn


def _project(x, g, w_in, qg, kg, bd):
    z = jnp.dot(_rms_bf16(x, g), w_in, preferred_element_type=F32)
    u = z[:, :POOL_W]
    q = z[:, POOL_W:POOL_W + Q_W]
    k = z[:, POOL_W + Q_W:POOL_W + Q_W + KV_W]
    v = z[:, POOL_W + Q_W + KV_W:]
    qn, kn = _qk_norm(q, k, qg, kg, bd)
    return u, qn, kn, v


def _pool_project(d_groups, wp_ref, ps):
    outs = []
    for p in range(2):
        dp = jnp.concatenate([d_groups[2 * p], d_groups[2 * p + 1]], axis=-1).astype(BF16)
        y = jnp.dot(dp, wp_ref[p], preferred_element_type=F32)
        outs.append(y * ps[:, p * 256:(p + 1) * 256])
    return jnp.concatenate(outs, axis=-1)


def _proj_pool_kernel(x_ref, g_ref, win_ref, qg_ref, kg_ref, bd_ref, wp_ref, ps_ref,
                      pool_ref, q_ref, k_ref, vt_ref, utail_ref, ktail_ref, vtail_ref,
                      ext_ref, sa_ref, sb_ref, zq_ref, *, tm, n_j):
    j = pl.program_id(1)

    @pl.when(j == 0)
    def _():
        ext_ref[0:HALO, :] = jnp.zeros((HALO, POOL_W), F32)

    r = tm + HALO
    h = _rms_bf16(x_ref[...], g_ref[...])
    ext_ref[HALO:r, :] = jnp.dot(h, win_ref[:, 0:POOL_W], preferred_element_type=F32)
    zq_ref[...] = jnp.dot(h, win_ref[:, POOL_W:], preferred_element_type=F32)
    u = ext_ref[HALO:r, :]
    sa_ref[8:r, :] = ext_ref[8:r, :] + ext_ref[7:r - 1, :]
    sb_ref[16:r, 128:] = sa_ref[16:r, 128:] + sa_ref[14:r - 2, 128:]
    sa_ref[24:r, 256:] = sb_ref[24:r, 256:] + sb_ref[20:r - 4, 256:]
    sb_ref[32:r, 384:] = sa_ref[32:r, 384:] + sa_ref[24:r - 8, 384:]
    pos1 = j * tm + lax.broadcasted_iota(jnp.int32, (tm, POOL_GC), 0) + 1
    sums = (sa_ref, sb_ref, sa_ref, sb_ref)
    d_groups = []
    for gi, w in enumerate(POOL_WINDOWS):
        sl = slice(gi * POOL_GC, (gi + 1) * POOL_GC)
        cnt = jnp.minimum(pos1, w).astype(F32)
        d_groups.append(sums[gi][HALO:r, sl] / cnt - u[:, sl])
    pool_ref[...] = _pool_project(d_groups, wp_ref, ps_ref[...]).astype(BF16)
    ext_ref[16:HALO, :] = ext_ref[tm + 16:r, :]

    qn, kn = _qk_norm(zq_ref[:, 0:Q_W], zq_ref[:, Q_W:Q_W + KV_W], qg_ref[...], kg_ref[...], bd_ref[...])
    v = zq_ref[:, Q_W + KV_W:]
    q_ref[...] = qn.astype(BF16)
    k_ref[...] = kn.astype(BF16)
    vt_ref[...] = jnp.transpose(v).astype(BF16)

    @pl.when(j == n_j - 1)
    def _():
        utail_ref[...] = u[tm - 16:, :]
        ktail_ref[...] = kn[tm - WINDOW:, :]
        vtail_ref[...] = v[tm - WINDOW:, :]


def _proj_pool_combine_kernel(x1_ref, gath_ref, route_ref, *rest, tm, n_j):
    x2_ref = rest[7]
    _combine_kernel(x1_ref, gath_ref, route_ref, x2_ref)
    _proj_pool_kernel(x2_ref, *rest[:7], *rest[8:], tm=tm, n_j=n_j)


def _proj_pool_prompt(l, x_in, n_seq, seq, g_attn, w_in, qg, kg, bd, wp, ps):
    tm = TM_PROJ
    n_j = seq // tm
    t = n_seq * seq
    row = lambda b, j: (b * n_j + j, 0)
    lay = lambda b, j: (l, 0, 0)
    fused = isinstance(x_in, tuple)
    if fused:
        kern = _proj_pool_combine_kernel
        x_args = list(x_in)
        x_specs = [pl.BlockSpec((tm, D_MODEL), row),
                   pl.BlockSpec((2, tm, D_MODEL // 2), lambda b, j: (0, b * n_j + j, 0)),
                   pl.BlockSpec((ROUTE_FIELDS, tm), lambda b, j: (0, b * n_j + j))]
        x_out_specs = [pl.BlockSpec((tm, D_MODEL), row)]
        x_out_shape = [jax.ShapeDtypeStruct((t, D_MODEL), F32)]
    else:
        kern = _proj_pool_kernel
        x_args = [x_in]
        x_specs = [pl.BlockSpec((tm, D_MODEL), row)]
        x_out_specs, x_out_shape = [], []
    return pl.pallas_call(
        functools.partial(kern, tm=tm, n_j=n_j),
        grid=(n_seq, n_j),
        in_specs=x_specs + [
            pl.BlockSpec((None, 1, D_MODEL), lay),
            pl.BlockSpec((None, D_MODEL, D_IN), lay),
            pl.BlockSpec((None, 1, Q_W), lay),
            pl.BlockSpec((None, 1, KV_W), lay),
            pl.BlockSpec((256, 256), lambda b, j: (0, 0)),
            pl.BlockSpec((None, 2, 256, 256), lambda b, j: (l, 0, 0, 0)),
            pl.BlockSpec((None, 1, POOL_W), lay),
        ],
        out_specs=x_out_specs + [
            pl.BlockSpec((tm, POOL_W), row),
            pl.BlockSpec((tm, Q_W), row),
            pl.BlockSpec((tm, KV_W), row),
            pl.BlockSpec((KV_W, tm), lambda b, j: (0, b * n_j + j)),
            pl.BlockSpec((None, 16, POOL_W), lambda b, j: (b, 0, 0)),
            pl.BlockSpec((None, WINDOW, KV_W), lambda b, j: (b, 0, 0)),
            pl.BlockSpec((None, WINDOW, KV_W), lambda b, j: (b, 0, 0)),
        ],
        out_shape=x_out_shape + [
            jax.ShapeDtypeStruct((t, POOL_W), BF16),
            jax.ShapeDtypeStruct((t, Q_W), BF16),
            jax.ShapeDtypeStruct((t, KV_W), BF16),
            jax.ShapeDtypeStruct((KV_W, t), BF16),
            jax.ShapeDtypeStruct((n_seq, 16, POOL_W), F32),
            jax.ShapeDtypeStruct((n_seq, WINDOW, KV_W), F32),
            jax.ShapeDtypeStruct((n_seq, WINDOW, KV_W), F32),
        ],
        scratch_shapes=[pltpu.VMEM((tm + HALO, POOL_W), F32)] * 3 + [pltpu.VMEM((tm, Q_W + 2 * KV_W), F32)],
        compiler_params=pltpu.CompilerParams(
            dimension_semantics=("arbitrary", "arbitrary"), vmem_limit_bytes=VMEM_LIMIT),
        name="proj_pool_prompt",
    )(*x_args, g_attn, w_in, qg, kg, bd, wp, ps)


def _attn_kernel(sink_ref, q_ref, kp_ref, kc_ref, vtp_ref, vtc_ref, bias_ref, o_ref, s_ref):
    j = pl.program_id(1)
    kk_all = jnp.concatenate([kp_ref[...], kc_ref[...]], axis=0)
    vt_all = jnp.concatenate([vtp_ref[...], vtc_ref[...]], axis=1)
    from_prev = (lax.broadcasted_iota(jnp.int32, (WINDOW, WINDOW), 0)
                 > lax.broadcasted_iota(jnp.int32, (WINDOW, WINDOW), 1))
    units = [(blk, kv) for blk in range(ATTN_QB) for kv in range(N_KV_HEADS)]

    def scores(n):
        blk, kv = units[n]
        q = q_ref[blk * WINDOW:(blk + 1) * WINDOW, :]
        kk = kk_all[blk * WINDOW:(blk + 2) * WINDOW, kv * HEAD_DIM:(kv + 1) * HEAD_DIM]
        heads = range(kv * GQA_GROUP, (kv + 1) * GQA_GROUP)
        q_rows = jnp.concatenate([q[:, h * HEAD_DIM:(h + 1) * HEAD_DIM] for h in heads], axis=0)
        s_ref[n % 2] = lax.dot_general(kk, q_rows, (((1,), (1,)), ((), ())), preferred_element_type=F32)

    scores(0)
    outs = []
    for n, (blk, kv) in enumerate(units):
        if n + 1 < len(units):
            scores(n + 1)
        vt_kv = vt_all[kv * HEAD_DIM:(kv + 1) * HEAD_DIM, blk * WINDOW:(blk + 2) * WINDOW]
        variant = jnp.minimum(j, 1) if blk == 0 else 1
        for g in range(GQA_GROUP):
            h = kv * GQA_GROUP + g
            s = jnp.where(from_prev, s_ref[n % 2, 0:WINDOW, g * WINDOW:(g + 1) * WINDOW],
                          s_ref[n % 2, WINDOW:, g * WINDOW:(g + 1) * WINDOW]) + bias_ref[variant, h]
            sink = sink_ref[h]
            m = jnp.maximum(jnp.max(s, axis=0, keepdims=True), sink)
            p = jnp.exp(s - m)
            denom = jnp.sum(p, axis=0, keepdims=True) + jnp.exp(sink - m)
            p_keys = jnp.concatenate([jnp.where(from_prev, p, 0.0), jnp.where(from_prev, 0.0, p)], axis=0)
            o_t = jnp.dot(vt_kv, p_keys.astype(BF16), preferred_element_type=F32)
            outs.append(o_t / denom)
        if kv == N_KV_HEADS - 1:
            o_ref[blk * WINDOW:(blk + 1) * WINDOW, :] = jnp.transpose(jnp.concatenate(outs, axis=0)).astype(BF16)
            outs = []


def _attn_prompt(q, k, vt, bias_t, sinks, n_seq, seq):
    tq = ATTN_QB * WINDOW
    nj = seq // tq
    t = n_seq * seq
    cur = lambda b, j: (b * nj + j, 0)
    prev = lambda b, j: (jnp.maximum((b * nj + j) * ATTN_QB - 1, 0), 0)
    cur_t = lambda b, j: (0, b * nj + j)
    prev_t = lambda b, j: (0, jnp.maximum((b * nj + j) * ATTN_QB - 1, 0))
    return pl.pallas_call(
        _attn_kernel,
        grid=(n_seq, nj),
        in_specs=[
            pl.BlockSpec(memory_space=pltpu.SMEM),
            pl.BlockSpec((tq, Q_W), cur),
            pl.BlockSpec((WINDOW, KV_W), prev),
            pl.BlockSpec((tq, KV_W), cur),
            pl.BlockSpec((KV_W, WINDOW), prev_t),
            pl.BlockSpec((KV_W, tq), cur_t),
            pl.BlockSpec((2, N_HEADS, WINDOW, WINDOW), lambda b, j: (0, 0, 0, 0)),
        ],
        out_specs=pl.BlockSpec((tq, Q_W), cur),
        out_shape=jax.ShapeDtypeStruct((t, Q_W), BF16),
        scratch_shapes=[pltpu.VMEM((2, 2 * WINDOW, GQA_GROUP * WINDOW), F32)],
        compiler_params=pltpu.CompilerParams(
            dimension_semantics=("arbitrary", "arbitrary"), vmem_limit_bytes=VMEM_LIMIT),
        name="attn_prompt",
    )(sinks, q, k, k, vt, vt, bias_t)


def _prompt_bias_t():
    r = jnp.arange(WINDOW, dtype=jnp.int32)[None, :]
    c = jnp.arange(WINDOW, dtype=jnp.int32)[:, None]
    from_prev = c > r
    dist = r - c + jnp.where(from_prev, WINDOW, 0)
    slopes = jnp.exp2(-8.0 * jnp.arange(1, N_HEADS + 1, dtype=F32) / N_HEADS)
    later = -slopes[:, None, None] * dist.astype(F32)[None]
    first = jnp.where(from_prev[None], -jnp.inf, later)
    return jnp.stack([first, later])


def _sample_kernel(x_ref, g_ref, win_ref, qg_ref, kg_ref, bd_ref, wp_ref, ps_ref,
                   st_ref, ck_ref, cv_ref, sink_ref, bias_ref, perm_ref,
                   pool_ref, attn_ref, pst_ref, kc_ref, vc_ref, *, ns, pos0):
    u, qn, kn, v = _project(x_ref[...], g_ref[...], win_ref[...], qg_ref[...], kg_ref[...], bd_ref[...])
    pst_ref[:, 0:POOL_STATE - 1, :] = st_ref[:, 1:POOL_STATE, :]
    kc_ref[:, 0:WINDOW - 1, :] = ck_ref[:, 1:WINDOW, :]
    vc_ref[:, 0:WINDOW - 1, :] = cv_ref[:, 1:WINDOW, :]
    for n in range(ns):
        pst_ref[n, POOL_STATE - 1:POOL_STATE, :] = u[n:n + 1, :]
        kc_ref[n, WINDOW - 1:WINDOW, :] = kn[n:n + 1, :]
        vc_ref[n, WINDOW - 1:WINDOW, :] = v[n:n + 1, :]

    d_groups = []
    for gi, w in enumerate(POOL_WINDOWS):
        lo = gi * POOL_GC
        acc = u[:, lo:lo + POOL_GC]
        for back in range(1, w):
            acc = acc + st_ref[:, POOL_STATE - back, lo:lo + POOL_GC]
        d_groups.append(acc / float(min(pos0 + 1, w)) - u[:, lo:lo + POOL_GC])
    pool_ref[...] = _pool_project(d_groups, wp_ref, ps_ref[...]).astype(BF16)

    zeros = jnp.zeros((ns, HEAD_DIM), F32)
    stacked = []
    for h in range(N_HEADS):
        piece = qn[:, h * HEAD_DIM:(h + 1) * HEAD_DIM]
        pair = [piece, zeros] if h < GQA_GROUP else [zeros, piece]
        stacked.append(jnp.concatenate(pair, axis=-1))
    q_hn = jnp.concatenate(stacked, axis=0).astype(BF16)
    q_nh = jnp.dot(perm_ref[0], q_hn, preferred_element_type=F32).astype(BF16)

    keys = kc_ref[...].reshape(ns * WINDOW, KV_W).astype(BF16)
    vals = vc_ref[...].reshape(ns * WINDOW, KV_W).astype(BF16)
    s_all = lax.dot_general(q_nh, keys, (((1,), (1,)), ((), ())), preferred_element_type=F32)
    sink = sink_ref[...]
    bias = bias_ref[...]
    zero_blk = jnp.zeros((N_HEADS, WINDOW), F32)
    p_rows = []
    for n in range(ns):
        s = s_all[n * N_HEADS:(n + 1) * N_HEADS, n * WINDOW:(n + 1) * WINDOW] + bias
        m = jnp.maximum(jnp.max(s, axis=-1, keepdims=True), sink)
        p = jnp.exp(s - m)
        denom = jnp.sum(p, axis=-1, keepdims=True) + jnp.exp(sink - m)
        p_rows.append(jnp.concatenate([zero_blk] * n + [p / denom] + [zero_blk] * (ns - 1 - n), axis=-1))
    p_blockdiag = jnp.concatenate(p_rows, axis=0).astype(BF16)
    o_nh = jnp.dot(p_blockdiag, vals, preferred_element_type=F32).astype(BF16)
    o_hn = jnp.dot(perm_ref[1], o_nh, preferred_element_type=F32)
    pieces = []
    for h in range(N_HEADS):
        kv = h // GQA_GROUP
        pieces.append(o_hn[h * ns:(h + 1) * ns, kv * HEAD_DIM:(kv + 1) * HEAD_DIM])
    attn_ref[...] = jnp.concatenate(pieces, axis=-1).astype(BF16)


def _sample_mixer(l, depth, xs, g_attn, w_in, qg, kg, bd, wp, ps, state, ck, cv, sink8, bias_s, pos0):
    n = xs.shape[0]
    ns = 32
    row = lambda i: (i, 0)
    lay = lambda i: (l, 0, 0)
    src = jnp.arange(ns * N_HEADS)
    perm = (((src % N_HEADS) * ns + src // N_HEADS)[:, None] == src[None, :]).astype(BF16)
    perms = jnp.stack([perm, perm.T])
    return pl.pallas_call(
        functools.partial(_sample_kernel, ns=ns, pos0=pos0),
        grid=(n // ns,),
        input_output_aliases={8: 2, 9: 3, 10: 4},
        in_specs=[
            pl.BlockSpec((ns, D_MODEL), row),
            pl.BlockSpec((None, 1, D_MODEL), lay),
            pl.BlockSpec((None, D_MODEL, D_IN), lay),
            pl.BlockSpec((None, 1, Q_W), lay),
            pl.BlockSpec((None, 1, KV_W), lay),
            pl.BlockSpec((256, 256), lambda i: (0, 0)),
            pl.BlockSpec((None, 2, 256, 256), lambda i: (l, 0, 0, 0)),
            pl.BlockSpec((None, 1, POOL_W), lay),
            pl.BlockSpec((None, ns, POOL_STATE, POOL_W), lambda i: (l, i, 0, 0)),
            pl.BlockSpec((None, ns, WINDOW, KV_W), lambda i: (l, i, 0, 0)),
            pl.BlockSpec((None, ns, WINDOW, KV_W), lambda i: (l, i, 0, 0)),
            pl.BlockSpec((N_HEADS, 1), lambda i: (0, 0)),
            pl.BlockSpec((N_HEADS, WINDOW), lambda i: (0, 0)),
            pl.BlockSpec((2, ns * N_HEADS, ns * N_HEADS), lambda i: (0, 0, 0)),
        ],
        out_specs=[
            pl.BlockSpec((ns, POOL_W), row),
            pl.BlockSpec((ns, Q_W), row),
            pl.BlockSpec((None, ns, POOL_STATE, POOL_W), lambda i: (l, i, 0, 0)),
            pl.BlockSpec((None, ns, WINDOW, KV_W), lambda i: (l, i, 0, 0)),
            pl.BlockSpec((None, ns, WINDOW, KV_W), lambda i: (l, i, 0, 0)),
        ],
        out_shape=[
            jax.ShapeDtypeStruct((n, POOL_W), BF16),
            jax.ShapeDtypeStruct((n, Q_W), BF16),
            jax.ShapeDtypeStruct((depth, n, POOL_STATE, POOL_W), F32),
            jax.ShapeDtypeStruct((depth, n, WINDOW, KV_W), F32),
            jax.ShapeDtypeStruct((depth, n, WINDOW, KV_W), F32),
        ],
        compiler_params=pltpu.CompilerParams(
            dimension_semantics=("arbitrary",), vmem_limit_bytes=VMEM_LIMIT),
        name="sample_mixer",
    )(xs, g_attn, w_in, qg, kg, bd, wp, ps, state, ck, cv, sink8, bias_s, perms)


def _merge_router_kernel(pool_ref, attn_ref, x_ref, wout_ref, g_ref, wr_ref, br_ref, utri_ref, cin_ref,
                         x1_ref, h2_ref, route_t_ref, cnt_ref, y_ref, lg_ref):
    i = pl.program_id(0)

    @pl.when(i == 0)
    def _():
        cnt_ref[...] = cin_ref[...]

    tm = x_ref.shape[0]
    rc = tm // MERGE_CHUNKS
    chunks = [slice(ci * rc, (ci + 1) * rc) for ci in range(MERGE_CHUNKS)]
    for rows in chunks:
        y_ref[rows, :] = (jnp.dot(pool_ref[rows, :], wout_ref[0:POOL_W, :], preferred_element_type=F32)
                          + jnp.dot(attn_ref[rows, :], wout_ref[POOL_W:, :], preferred_element_type=F32))
    for rows in chunks:
        x1 = x_ref[rows, :] + y_ref[rows, :]
        x1_ref[rows, :] = x1
        h2 = _rms_bf16(x1, g_ref[...])
        h2_ref[rows, :] = _pack_bf16_pairs(h2)
        lg_ref[rows, :] = jnp.dot(h2, wr_ref[...], preferred_element_type=F32) + br_ref[...]
    logits = lg_ref[...]

    lt = jnp.transpose(logits)
    sub = lax.broadcasted_iota(jnp.int32, (EXPERTS_PER_GROUP, tm), 0)
    neg = -jnp.inf
    big = jnp.int32(EXPERTS_PER_GROUP)
    gl = jnp.where(sub < N_EXPERT_GROUPS, lt[GROUP_LANE0:GROUP_LANE0 + EXPERTS_PER_GROUP, :], neg)
    gmax = jnp.max(gl, axis=0, keepdims=True)
    grp = jnp.min(jnp.where(gl == gmax, sub, big), axis=0, keepdims=True)
    g_w = 1.0 / jnp.sum(jnp.exp(gl - gmax), axis=0, keepdims=True)
    el = lt[(N_EXPERT_GROUPS - 1) * EXPERTS_PER_GROUP:N_EXPERT_GROUPS * EXPERTS_PER_GROUP, :]
    for gi in range(N_EXPERT_GROUPS - 2, -1, -1):
        el = jnp.where(grp == gi, lt[gi * EXPERTS_PER_GROUP:(gi + 1) * EXPERTS_PER_GROUP, :], el)
    v1 = jnp.max(el, axis=0, keepdims=True)
    i1 = jnp.min(jnp.where(el == v1, sub, big), axis=0, keepdims=True)
    el2 = jnp.where(sub == i1, neg, el)
    v2 = jnp.max(el2, axis=0, keepdims=True)
    i2 = jnp.min(jnp.where(el2 == v2, sub, big), axis=0, keepdims=True)
    e21 = jnp.exp(v2 - v1)
    w1 = g_w / (1.0 + e21)
    w2 = g_w * e21 / (1.0 + e21)
    e1 = grp * EXPERTS_PER_GROUP + i1
    e2 = grp * EXPERTS_PER_GROUP + i2

    esub = lax.broadcasted_iota(jnp.int32, (N_EXPERTS, tm), 0)
    oh1 = esub == e1
    oh2 = esub == e2
    c = jnp.where(oh1 | oh2, 1.0, 0.0)
    prefix = jnp.dot(c.astype(BF16), utri_ref[...], preferred_element_type=F32) + cnt_ref[...]
    r1 = jnp.sum(jnp.where(oh1, prefix, 0.0), axis=0, keepdims=True)
    r2 = jnp.sum(jnp.where(oh2, prefix, 0.0), axis=0, keepdims=True)
    cnt_ref[...] = cnt_ref[...] + jnp.sum(c, axis=1, keepdims=True)

    fields = jnp.zeros((ROUTE_FIELDS, tm), F32)
    for idx, val in enumerate((e1.astype(F32), e2.astype(F32), w1, w2, r1, r2)):
        fields = jnp.where(sub == idx, val, fields)
    route_t_ref[...] = fields


def _merge_router(l, pool, attn, x2d, w_out, g_ffn, wr, br, cnt_in, tm):
    t = x2d.shape[0]
    utri = (jnp.arange(tm)[:, None] < jnp.arange(tm)[None, :]).astype(BF16)
    row = lambda i: (i, 0)
    lay = lambda i: (l, 0, 0)
    return pl.pallas_call(
        _merge_router_kernel,
        grid=(t // tm,),
        in_specs=[
            pl.BlockSpec((tm, POOL_W), row),
            pl.BlockSpec((tm, Q_W), row),
            pl.BlockSpec((tm, D_MODEL), row),
            pl.BlockSpec((None, D_MODEL, D_MODEL), lay),
            pl.BlockSpec((None, 1, D_MODEL), lay),
            pl.BlockSpec((None, D_MODEL, LANES), lay),
            pl.BlockSpec((None, 1, LANES), lay),
            pl.BlockSpec((tm, tm), lambda i: (0, 0)),
            pl.BlockSpec((N_EXPERTS, 1), lambda i: (0, 0)),
        ],
        out_specs=[
            pl.BlockSpec((tm, D_MODEL), row),
            pl.BlockSpec((tm, D_MODEL // 2), row),
            pl.BlockSpec((ROUTE_FIELDS, tm), lambda i: (0, i)),
            pl.BlockSpec((N_EXPERTS, 1), lambda i: (0, 0)),
        ],
        out_shape=[
            jax.ShapeDtypeStruct((t, D_MODEL), F32),
            jax.ShapeDtypeStruct((t, D_MODEL // 2), jnp.int32),
            jax.ShapeDtypeStruct((ROUTE_FIELDS, t), F32),
            jax.ShapeDtypeStruct((N_EXPERTS, 1), F32),
        ],
        scratch_shapes=[pltpu.VMEM((tm, D_MODEL), F32), pltpu.VMEM((tm, LANES), F32)],
        compiler_params=pltpu.CompilerParams(
            dimension_semantics=("arbitrary",), vmem_limit_bytes=VMEM_LIMIT),
        name="merge_router",
    )(pool, attn, x2d, w_out, g_ffn, wr, br, utri, cnt_in)


def _moe_kernel(be_ref, rv_ref, nx_ref, sl_ref, first_ref, xd_ref, wg_hbm, wu_hbm, wd_hbm, yd_ref,
                wg_f, wu_f, wd_f, wg_s, wu_s, wd_s, sem, *, layer):
    step = pl.program_id(0)

    def weight_copies(e, s):
        return [pltpu.make_async_copy(w_hbm.at[layer, e], w_f.at[s], sem.at[s, n])
                for n, (w_hbm, w_f) in enumerate(((wg_hbm, wg_f), (wu_hbm, wu_f), (wd_hbm, wd_f)))]

    @pl.when(step == 0)
    def _():
        for s in range(WEIGHT_SLOTS - 1):
            @pl.when(first_ref[s] >= 0)
            def _():
                for c in weight_copies(first_ref[s], s):
                    c.start()

    def enter_expert(i):
        expert, slot = be_ref[i], sl_ref[i]

        @pl.when((i == 0) | (expert != be_ref[jnp.maximum(i - 1, 0)]))
        def _():
            for c in weight_copies(expert, slot):
                c.wait()

            @pl.when(nx_ref[i] >= 0)
            def _():
                for c in weight_copies(nx_ref[i], lax.rem(slot + WEIGHT_SLOTS - 1, WEIGHT_SLOTS)):
                    c.start(priority=1)

            wg_s[...] = wg_f[slot].astype(BF16)
            wu_s[...] = wu_f[slot].astype(BF16)
            wd_s[...] = wd_f[slot].astype(BF16)

    def experts_on(row0, n_rows, rows_valid):
        rows = pl.ds(row0, n_rows)
        row = lax.broadcasted_iota(jnp.int32, (n_rows, D_MODEL // 2), 0)
        x = _unpack_bf16_pairs(jnp.where(row < rows_valid, xd_ref[rows, :], 0)).astype(BF16)
        gate = jnp.dot(x, wg_s[...], preferred_element_type=F32)
        up = jnp.dot(x, wu_s[...], preferred_element_type=F32)
        act = (gate * jax.nn.sigmoid(gate) * up).astype(BF16)
        y = jnp.dot(act, wd_s[...], preferred_element_type=F32)
        yd_ref[rows, :] = _pack_bf16_pairs(y.astype(BF16))

    def experts_ragged(row0, lead_rows, rows_last):
        half = MOE_BM // 2

        @pl.when(rows_last > half)
        def _():
            experts_on(row0, lead_rows + MOE_BM, lead_rows + rows_last)

        @pl.when(rows_last <= half)
        def _():
            experts_on(row0, lead_rows + half, lead_rows + rows_last)
            yd_ref[pl.ds(row0 + lead_rows + half, half), :] = jnp.zeros((half, D_MODEL // 2), jnp.int32)

    def single_block(i, row0):
        enter_expert(i)

        @pl.when(rv_ref[i] > 0)
        def _():
            experts_ragged(row0, 0, rv_ref[i])

        @pl.when(rv_ref[i] <= 0)
        def _():
            yd_ref[pl.ds(row0, MOE_BM), :] = jnp.zeros((MOE_BM, D_MODEL // 2), jnp.int32)

    @pl.when(rv_ref[step * MOE_STEP_BLOCKS] > 0)
    def _():
        for pair in range(MOE_STEP_BLOCKS // 2):
            ia = step * MOE_STEP_BLOCKS + 2 * pair
            ib = ia + 1
            row0 = 2 * pair * MOE_BM
            same = (be_ref[ib] == be_ref[ia]) & (rv_ref[ib] > 0)

            @pl.when(same)
            def _():
                enter_expert(ia)
                experts_ragged(row0, MOE_BM, rv_ref[ib])

            @pl.when(jnp.logical_not(same))
            def _():
                single_block(ia, row0)
                single_block(ib, row0 + MOE_BM)


def _moe_experts(l, block_e, rows_valid, next_e, slot, first_e, xd, w_gate, w_up, w_down):
    n_blocks = xd.shape[0] // MOE_BM
    step_rows = MOE_STEP_BLOCKS * MOE_BM
    row = lambda i, be, rv, nx, sl, fe: (jnp.minimum(i, fe[WEIGHT_SLOTS - 1] - 1), 0)
    return pl.pallas_call(
        functools.partial(_moe_kernel, layer=l),
        grid_spec=pltpu.PrefetchScalarGridSpec(
            num_scalar_prefetch=5,
            grid=(n_blocks // MOE_STEP_BLOCKS,),
            in_specs=[
                pl.BlockSpec((step_rows, D_MODEL // 2), row),
                pl.BlockSpec(memory_space=pl.ANY),
                pl.BlockSpec(memory_space=pl.ANY),
                pl.BlockSpec(memory_space=pl.ANY),
            ],
            out_specs=pl.BlockSpec((step_rows, D_MODEL // 2), row),
            scratch_shapes=[
                pltpu.VMEM((WEIGHT_SLOTS, D_MODEL, EXPERT_FF), F32),
                pltpu.VMEM((WEIGHT_SLOTS, D_MODEL, EXPERT_FF), F32),
                pltpu.VMEM((WEIGHT_SLOTS, EXPERT_FF, D_MODEL), F32),
                pltpu.VMEM((D_MODEL, EXPERT_FF), BF16),
                pltpu.VMEM((D_MODEL, EXPERT_FF), BF16),
                pltpu.VMEM((EXPERT_FF, D_MODEL), BF16),
                pltpu.SemaphoreType.DMA((WEIGHT_SLOTS, 3)),
            ],
        ),
        out_shape=jax.ShapeDtypeStruct((n_blocks * MOE_BM, D_MODEL // 2), jnp.int32),
        compiler_params=pltpu.CompilerParams(
            dimension_semantics=("arbitrary",), vmem_limit_bytes=VMEM_LIMIT),
        name="moe_experts",
    )(block_e, rows_valid, next_e, slot, first_e, xd, w_gate, w_up, w_down)


def _sc_worker_id():
    return lax.axis_index("s") * SC_CORES + lax.axis_index("c")


def _sc_dispatch(hp, hs, dest_p, dest_s, n_rows):
    tp, width = hp.shape
    per_w = tp // SC_WORKERS
    n_ch = per_w // DISP_CH
    n_sw = hs.shape[0] // SAMPLE_CH
    mesh = plsc.VectorSubcoreMesh(core_axis_name="c", subcore_axis_name="s")

    @functools.partial(
        pl.kernel, mesh=mesh,
        out_type=jax.ShapeDtypeStruct((n_rows, width), jnp.int32),
        scratch_types=[
            pltpu.VMEM((2, n_ch, DISP_CH), jnp.int32),
            pltpu.VMEM((2, 1, SAMPLE_CH), jnp.int32),
            pltpu.VMEM((SC_RING, DISP_CH, width), jnp.int32),
            pltpu.SemaphoreType.DMA((SC_RING,)),
            pltpu.SemaphoreType.DMA((SC_RING, 2)),
        ],
        name="sc_dispatch",
    )
    def k(hp_hbm, hs_hbm, dp_hbm, ds_hbm, xd_hbm, idx_v, idxs_v, bufs, rsem, wsem):
        wid = _sc_worker_id()
        base = wid * per_w
        for kk in range(2):
            pltpu.sync_copy(dp_hbm.at[kk, wid], idx_v.at[kk])
        reads = [pltpu.make_async_copy(hp_hbm.at[pl.ds(base + j * DISP_CH, DISP_CH)],
                                       bufs.at[j % SC_RING], rsem.at[j % SC_RING]) for j in range(n_ch)]
        writes = [[pltpu.make_async_copy(bufs.at[j % SC_RING], xd_hbm.at[idx_v.at[kk, j]],
                                         wsem.at[j % SC_RING, kk]) for kk in range(2)] for j in range(n_ch)]
        for j in range(min(SC_RING - 1, n_ch)):
            reads[j].start()
        for j in range(n_ch):
            reads[j].wait()
            for w in writes[j]:
                w.start()
            if j >= 1:
                for w in writes[j - 1]:
                    w.wait()
            if j + SC_RING - 1 < n_ch:
                reads[j + SC_RING - 1].start()
        for w in writes[n_ch - 1]:
            w.wait()

        @pl.when(wid < n_sw)
        def _():
            rows = bufs.at[0, pl.ds(0, SAMPLE_CH)]
            for kk in range(2):
                pltpu.sync_copy(ds_hbm.at[kk, wid], idxs_v.at[kk])
            pltpu.sync_copy(hs_hbm.at[pl.ds(wid * SAMPLE_CH, SAMPLE_CH)], rows)
            for kk in range(2):
                pltpu.sync_copy(rows, xd_hbm.at[idxs_v.at[kk, 0]])

    return k(hp, hs, dest_p, dest_s)


def _sc_combine_gather(yd, dest_p, dest_s, tp, ts):
    width = yd.shape[1]
    per_w = tp // SC_WORKERS
    n_ch = per_w // COMB_CH
    n_sw = ts // SAMPLE_CH
    mesh = plsc.VectorSubcoreMesh(core_axis_name="c", subcore_axis_name="s")

    @functools.partial(
        pl.kernel, mesh=mesh,
        out_type=jax.ShapeDtypeStruct((2, tp + ts, width), yd.dtype),
        scratch_types=[
            pltpu.VMEM((2, n_ch, COMB_CH), jnp.int32),
            pltpu.VMEM((2, 1, SAMPLE_CH), jnp.int32),
            pltpu.VMEM((SC_RING, COMB_CH, width), yd.dtype),
            pltpu.SemaphoreType.DMA((SC_RING,)),
            pltpu.SemaphoreType.DMA((SC_RING,)),
        ],
        name="sc_combine_gather",
    )
    def k(yd_hbm, dp_hbm, ds_hbm, g_hbm, idx_v, idxs_v, bufs, gsem, wsem):
        wid = _sc_worker_id()
        base = wid * per_w
        for kk in range(2):
            pltpu.sync_copy(dp_hbm.at[kk, wid], idx_v.at[kk])
        items = [(kk, j) for kk in range(2) for j in range(n_ch)]
        n_items = len(items)
        gathers = [pltpu.make_async_copy(yd_hbm.at[idx_v.at[kk, j]], bufs.at[n % SC_RING], gsem.at[n % SC_RING])
                   for n, (kk, j) in enumerate(items)]
        outs = [pltpu.make_async_copy(bufs.at[n % SC_RING], g_hbm.at[kk, pl.ds(base + j * COMB_CH, COMB_CH)],
                                      wsem.at[n % SC_RING]) for n, (kk, j) in enumerate(items)]
        for n in range(min(SC_RING - 1, n_items)):
            gathers[n].start()
        for n in range(n_items):
            gathers[n].wait()
            outs[n].start()
            if n >= 1:
                outs[n - 1].wait()
            if n + SC_RING - 1 < n_items:
                gathers[n + SC_RING - 1].start()
        outs[n_items - 1].wait()

        @pl.when(wid < n_sw)
        def _():
            for kk in range(2):
                pltpu.sync_copy(ds_hbm.at[kk, wid], idxs_v.at[kk])
            for kk in range(2):
                rows = bufs.at[kk, pl.ds(0, SAMPLE_CH)]
                pltpu.sync_copy(yd_hbm.at[idxs_v.at[kk, 0]], rows)
                pltpu.sync_copy(rows, g_hbm.at[kk, pl.ds(tp + wid * SAMPLE_CH, SAMPLE_CH)])

    return k(yd, dest_p, dest_s)


def _combine_kernel(x1_ref, g_ref, route_t_ref, x2_ref):
    fields = route_t_ref[...]
    tm = fields.shape[1]
    cols = jnp.transpose(jnp.concatenate([fields, jnp.zeros((LANES - ROUTE_FIELDS, tm), F32)], axis=0))
    w1 = cols[:, 2:3]
    w2 = cols[:, 3:4]
    x2_ref[...] = x1_ref[...] + _unpack_bf16_pairs(g_ref[0]) * w1 + _unpack_bf16_pairs(g_ref[1]) * w2


def _combine(x1, g, route_t, row0, tm):
    t = x1.shape[0]
    blk0 = row0 // tm
    row = lambda i: (i, 0)
    return pl.pallas_call(
        _combine_kernel,
        grid=(t // tm,),
        in_specs=[
            pl.BlockSpec((tm, D_MODEL), row),
            pl.BlockSpec((2, tm, D_MODEL // 2), lambda i: (0, blk0 + i, 0)),
            pl.BlockSpec((ROUTE_FIELDS, tm), lambda i: (0, i)),
        ],
        out_specs=pl.BlockSpec((tm, D_MODEL), row),
        out_shape=jax.ShapeDtypeStruct((t, D_MODEL), F32),
        compiler_params=pltpu.CompilerParams(
            dimension_semantics=("arbitrary",), vmem_limit_bytes=VMEM_LIMIT),
        name="combine",
    )(x1, g, route_t)


def _dest_layout(dest, workers, chunk):
    t = dest.shape[1]
    return dest.reshape(2, workers, t // (workers * chunk), chunk)


def _hier_moe(l, h2p, h2s, route_tp, route_ts, counts, w_gate, w_up, w_down):
    tp, ts = h2p.shape[0], h2s.shape[0]
    n_assign = 2 * (tp + ts)
    n_blocks = -(-n_assign // MOE_BM) + N_EXPERTS
    n_blocks = -(-n_blocks // MOE_STEP_BLOCKS) * MOE_STEP_BLOCKS
    pcounts = (counts + MOE_BM - 1) // MOE_BM * MOE_BM
    pends = jnp.cumsum(pcounts)
    poffsets = pends - pcounts
    starts = jnp.arange(n_blocks, dtype=jnp.int32) * MOE_BM
    block_e = jnp.minimum(jnp.sum((pends[None, :] <= starts[:, None]).astype(jnp.int32), axis=1),
                          N_EXPERTS - 1)
    experts = jnp.arange(N_EXPERTS, dtype=jnp.int32)

    def lookup(table, idx):
        return jnp.sum(jnp.where(idx[..., None] == experts, table, 0), axis=-1)

    rows_valid = jnp.clip(lookup(poffsets + counts, block_e) - starts, 0, MOE_BM).astype(jnp.int32)
    used = counts > 0
    last_e = jnp.max(jnp.where(used, jnp.arange(N_EXPERTS, dtype=jnp.int32), 0))
    block_e = jnp.where(rows_valid > 0, block_e, last_e).astype(jnp.int32)
    place = jnp.cumsum(used.astype(jnp.int32)) - 1
    by_place = jnp.sum(jnp.where(used[None, :] & (place[None, :] == experts[:, None]), experts[None, :], 0),
                       axis=1)
    n_used = jnp.sum(used.astype(jnp.int32))

    def at_place(p):
        return jnp.where(p < n_used, lookup(by_place, jnp.minimum(p, N_EXPERTS - 1)), -1).astype(jnp.int32)

    ahead_of = at_place(place + (WEIGHT_SLOTS - 1))
    next_e = lookup(ahead_of, block_e)
    slot = lookup(place % WEIGHT_SLOTS, block_e)
    n_steps_used = -(-(pends[-1] // MOE_BM) // MOE_STEP_BLOCKS)
    first_e = jnp.concatenate([at_place(jnp.arange(WEIGHT_SLOTS - 1, dtype=jnp.int32)),
                               n_steps_used.reshape(1).astype(jnp.int32)])

    def dest_of(route_t):
        return lookup(poffsets, route_t[0:2].astype(jnp.int32)) + route_t[4:6].astype(jnp.int32)

    dest_p, dest_s = dest_of(route_tp), dest_of(route_ts)
    n_sw = ts // SAMPLE_CH
    xd = _sc_dispatch(h2p, h2s, _dest_layout(dest_p, SC_WORKERS, DISP_CH),
                      _dest_layout(dest_s, n_sw, SAMPLE_CH), n_blocks * MOE_BM)
    yd = _moe_experts(l, block_e, rows_valid, next_e, slot, first_e, xd, w_gate, w_up, w_down)
    return _sc_combine_gather(yd, _dest_layout(dest_p, SC_WORKERS, COMB_CH),
                              _dest_layout(dest_s, n_sw, SAMPLE_CH), tp, ts)


def kernel(x_prompt, x_sample, state_pool, cache_k_win, cache_v_win, norm_attn_g, w_in, pool_w, pool_scale, q_norm_g, k_norm_g, attn_sinks, w_out, norm_ffn_g, router_group_w, router_group_b, router_expert_w, router_expert_b, w_gate, w_up, w_down):
    n_p, t_p, d = x_prompt.shape
    n_s, t_s, _ = x_sample.shape
    depth = w_in.shape[0]
    lw_s = cache_k_win.shape[2]
    assert t_s == 1 and lw_s == WINDOW and d == D_MODEL
    assert t_p % TM_PROJ == 0 and t_p >= WINDOW

    seg = jnp.arange(256) // HEAD_DIM
    bd = jnp.where(seg[:, None] == seg[None, :], 1.0 / HEAD_DIM, 0.0).astype(BF16)
    slopes = jnp.exp2(-8.0 * jnp.arange(1, N_HEADS + 1, dtype=F32) / N_HEADS)
    bias_p = _prompt_bias_t()
    dist_s = (WINDOW - 1) - jnp.arange(WINDOW, dtype=F32)
    bias_s = -slopes[:, None] * dist_s[None, :]

    wp = jnp.zeros((depth, 2, 256, 256), F32)
    for p in range(2):
        wp = wp.at[:, p, :POOL_GC, :POOL_GC].set(pool_w[:, 2 * p])
        wp = wp.at[:, p, POOL_GC:, POOL_GC:].set(pool_w[:, 2 * p + 1])
    assert GROUP_LANE0 == N_EXPERTS
    lane_pad = LANES - N_EXPERTS - N_EXPERT_GROUPS
    wr = jnp.concatenate([router_expert_w, router_group_w, jnp.zeros((depth, D_MODEL, lane_pad), F32)], axis=-1)
    br = jnp.concatenate([router_expert_b, router_group_b, jnp.zeros((depth, lane_pad), F32)],
                         axis=-1).reshape(depth, 1, LANES)
    lp = dict(
        w_in=w_in.astype(BF16),
        w_out=w_out.astype(BF16),
        g_attn=norm_attn_g.reshape(depth, 1, D_MODEL),
        g_ffn=norm_ffn_g.reshape(depth, 1, D_MODEL),
        qg=(jnp.tile(q_norm_g, (1, N_HEADS)) * ATTN_SCALE).reshape(depth, 1, Q_W),
        kg=jnp.tile(k_norm_g, (1, N_KV_HEADS)).reshape(depth, 1, KV_W),
        wp=wp.astype(BF16),
        ps=pool_scale.reshape(depth, 1, POOL_W),
        wr=wr.astype(BF16),
        br=br,
        state=state_pool,
        ck=cache_k_win.reshape(depth, n_s, lw_s, KV_W),
        cv=cache_v_win.reshape(depth, n_s, lw_s, KV_W),
    )

    xp = x_prompt.reshape(n_p * t_p, D_MODEL)
    xs = x_sample.reshape(n_s, D_MODEL)
    lw_p = min(WINDOW, t_p)
    pool_p, kp_new, vp_new = [], [], []
    sample_state = [lp["state"], lp["ck"], lp["cv"]]
    zero_cnt = jnp.zeros((N_EXPERTS, 1), F32)
    pending = None
    for l in range(depth):
        sinks = attn_sinks[l]
        outs = _proj_pool_prompt(
            l, xp if pending is None else pending, n_p, t_p,
            lp["g_attn"], lp["w_in"], lp["qg"], lp["kg"], bd, lp["wp"], lp["ps"])
        if pending is not None:
            xp, outs = outs[0], outs[1:]
        pool_o, q, k, vt, utail, ktail, vtail = outs
        attn_o = _attn_prompt(q, k, vt, bias_p, sinks, n_p, t_p)
        x1p, h2p, route_tp, cnt_p = _merge_router(
            l, pool_o, attn_o, xp, lp["w_out"], lp["g_ffn"], lp["wr"], lp["br"], zero_cnt, TM_MERGE)
        pool_p.append(utail[:, 16 - POOL_STATE:, :])
        kp_new.append(ktail)
        vp_new.append(vtail)
        pool_so, attn_so, *sample_state = _sample_mixer(
            l, depth, xs, lp["g_attn"], lp["w_in"], lp["qg"], lp["kg"], bd, lp["wp"], lp["ps"],
            *sample_state, sinks.reshape(N_HEADS, 1), bias_s, PAST_LEN)
        x1s, h2s, route_ts, cnt_all = _merge_router(
            l, pool_so, attn_so, xs, lp["w_out"], lp["g_ffn"], lp["wr"], lp["br"], cnt_p, n_s)
        counts = cnt_all[:, 0].astype(jnp.int32)
        g = _hier_moe(l, h2p, h2s, route_tp, route_ts, counts, w_gate, w_up, w_down)
        xs = _combine(x1s, g, route_ts, n_p * t_p, n_s)
        pending = (x1p, g, route_tp)
    xp = _combine(*pending, 0, TM_MERGE)
    return (xp.reshape(n_p, t_p, D_MODEL), xs.reshape(n_s, t_s, D_MODEL),
            jnp.stack(pool_p),
            jnp.stack(kp_new).reshape(depth, n_p, lw_p, N_KV_HEADS, HEAD_DIM),
            jnp.stack(vp_new).reshape(depth, n_p, lw_p, N_KV_HEADS, HEAD_DIM),
            sample_state[0],
            sample_state[1].reshape(depth, n_s, lw_s, N_KV_HEADS, HEAD_DIM),
            sample_state[2].reshape(depth, n_s, lw_s, N_KV_HEADS, HEAD_DIM))
```

```python
import functools

import jax
import jax.numpy as jnp
from jax import lax
from jax.experimental import pallas as pl
from jax.experimental.pallas import tpu as pltpu
from jax.experimental.pallas import tpu_sc as plsc

D_MODEL = 1024
POOL_W = 512
POOL_WINDOWS = (2, 4, 8, 16)
POOL_GC = 128
POOL_STATE = 15
HEAD_DIM = 64
N_HEADS = 8
N_KV_HEADS = 2
GQA_GROUP = 4
Q_W = 512
KV_W = 128
D_IN = POOL_W + Q_W + 2 * KV_W
WINDOW = 128
ATTN_SCALE = HEAD_DIM ** -0.5
N_EXPERT_GROUPS = 4
EXPERTS_PER_GROUP = 8
N_EXPERTS = 32
EXPERT_FF = 512
EPS = 1e-6
PAST_LEN = 16384

LANES = 128
HALO = 32
TM_PROJ = 1024
TM_MERGE = 1024
MERGE_CHUNKS = 4
ATTN_QB = 16
MOE_BM = 256
MOE_STEP_BLOCKS = 4
WEIGHT_SLOTS = 3
GROUP_LANE0 = 32
ROUTE_FIELDS = 8
SC_CORES = 2
SC_SUBCORES = 16
SC_WORKERS = SC_CORES * SC_SUBCORES
DISP_CH = 64
COMB_CH = 64
SAMPLE_CH = 32
SC_RING = 3
VMEM_LIMIT = 48 * 1024 * 1024

BF16 = jnp.bfloat16
F32 = jnp.float32


def _pack_bf16_pairs(h):
    w = h.shape[1] // 2
    hi = lax.bitcast_convert_type(h[:, :w].astype(F32), jnp.uint32)
    lo = lax.bitcast_convert_type(h[:, w:].astype(F32), jnp.uint32)
    return lax.bitcast_convert_type(hi | (lo >> 16), jnp.int32)


def _unpack_bf16_pairs(words):
    u = lax.bitcast_convert_type(words, jnp.uint32)
    hi = lax.bitcast_convert_type(u & jnp.uint32(0xFFFF0000), F32)
    lo = lax.bitcast_convert_type(u << 16, F32)
    return jnp.concatenate([hi, lo], axis=-1)


def _segment_mean_sq(a, bd):
    w = a.shape[1]
    return jnp.dot((a * a).astype(BF16), bd[:w, :w], preferred_element_type=F32)


def _rms_bf16(x, g):
    ms = jnp.mean(x * x, axis=-1, keepdims=True)
    return (x * lax.rsqrt(ms + EPS) * g).astype(BF16)


def _qk_norm(q, k, qg, kg, bd):
    qn = []
    for c in range(Q_W // 256):
        qc = q[:, c * 256:(c + 1) * 256]
        qn.append(qc * lax.rsqrt(_segment_mean_sq(qc, bd) + EPS))
    qn = jnp.concatenate(qn, axis=-1) * qg
    kn = k * lax.rsqrt(_segment_mean_sq(k, bd) + EPS) * kg
    return qn, kn


def _project(x, g, w_in, qg, kg, bd):
    z = jnp.dot(_rms_bf16(x, g), w_in, preferred_element_type=F32)
    u = z[:, :POOL_W]
    q = z[:, POOL_W:POOL_W + Q_W]
    k = z[:, POOL_W + Q_W:POOL_W + Q_W + KV_W]
    v = z[:, POOL_W + Q_W + KV_W:]
    qn, kn = _qk_norm(q, k, qg, kg, bd)
    return u, qn, kn, v


def _pool_project(d_groups, wp_ref, ps):
    outs = []
    for p in range(2):
        dp = jnp.concatenate([d_groups[2 * p], d_groups[2 * p + 1]], axis=-1).astype(BF16)
        y = jnp.dot(dp, wp_ref[p], preferred_element_type=F32)
        outs.append(y * ps[:, p * 256:(p + 1) * 256])
    return jnp.concatenate(outs, axis=-1)


def _proj_pool_kernel(x_ref, g_ref, win_ref, qg_ref, kg_ref, bd_ref, wp_ref, ps_ref,
                      pool_ref, q_ref, k_ref, vt_ref, utail_ref, ktail_ref, vtail_ref,
                      ext_ref, sa_ref, sb_ref, zq_ref, *, tm, n_j):
    j = pl.program_id(1)

    @pl.when(j == 0)
    def _():
        ext_ref[0:HALO, :] = jnp.zeros((HALO, POOL_W), F32)

    r = tm + HALO
    h = _rms_bf16(x_ref[...], g_ref[...])
    ext_ref[HALO:r, :] = jnp.dot(h, win_ref[:, 0:POOL_W], preferred_element_type=F32)
    zq_ref[...] = jnp.dot(h, win_ref[:, POOL_W:], preferred_element_type=F32)
    u = ext_ref[HALO:r, :]
    sa_ref[8:r, :] = ext_ref[8:r, :] + ext_ref[7:r - 1, :]
    sb_ref[16:r, 128:] = sa_ref[16:r, 128:] + sa_ref[14:r - 2, 128:]
    sa_ref[24:r, 256:] = sb_ref[24:r, 256:] + sb_ref[20:r - 4, 256:]
    sb_ref[32:r, 384:] = sa_ref[32:r, 384:] + sa_ref[24:r - 8, 384:]
    pos1 = j * tm + lax.broadcasted_iota(jnp.int32, (tm, POOL_GC), 0) + 1
    sums = (sa_ref, sb_ref, sa_ref, sb_ref)
    d_groups = []
    for gi, w in enumerate(POOL_WINDOWS):
        sl = slice(gi * POOL_GC, (gi + 1) * POOL_GC)
        cnt = jnp.minimum(pos1, w).astype(F32)
        d_groups.append(sums[gi][HALO:r, sl] / cnt - u[:, sl])
    pool_ref[...] = _pool_project(d_groups, wp_ref, ps_ref[...]).astype(BF16)
    ext_ref[16:HALO, :] = ext_ref[tm + 16:r, :]

    qn, kn = _qk_norm(zq_ref[:, 0:Q_W], zq_ref[:, Q_W:Q_W + KV_W], qg_ref[...], kg_ref[...], bd_ref[...])
    v = zq_ref[:, Q_W + KV_W:]
    q_ref[...] = qn.astype(BF16)
    k_ref[...] = kn.astype(BF16)
    vt_ref[...] = jnp.transpose(v).astype(BF16)

    @pl.when(j == n_j - 1)
    def _():
        utail_ref[...] = u[tm - 16:, :]
        ktail_ref[...] = kn[tm - WINDOW:, :]
        vtail_ref[...] = v[tm - WINDOW:, :]


def _proj_pool_combine_kernel(x1_ref, gath_ref, route_ref, *rest, tm, n_j):
    x2_ref = rest[7]
    _combine_kernel(x1_ref, gath_ref, route_ref, x2_ref)
    _proj_pool_kernel(x2_ref, *rest[:7], *rest[8:], tm=tm, n_j=n_j)


def _proj_pool_prompt(l, x_in, n_seq, seq, g_attn, w_in, qg, kg, bd, wp, ps):
    tm = TM_PROJ
    n_j = seq // tm
    t = n_seq * seq
    row = lambda b, j: (b * n_j + j, 0)
    lay = lambda b, j: (l, 0, 0)
    fused = isinstance(x_in, tuple)
    if fused:
        kern = _proj_pool_combine_kernel
        x_args = list(x_in)
        x_specs = [pl.BlockSpec((tm, D_MODEL), row),
                   pl.BlockSpec((2, tm, D_MODEL // 2), lambda b, j: (0, b * n_j + j, 0)),
                   pl.BlockSpec((ROUTE_FIELDS, tm), lambda b, j: (0, b * n_j + j))]
        x_out_specs = [pl.BlockSpec((tm, D_MODEL), row)]
        x_out_shape = [jax.ShapeDtypeStruct((t, D_MODEL), F32)]
    else:
        kern = _proj_pool_kernel
        x_args = [x_in]
        x_specs = [pl.BlockSpec((tm, D_MODEL), row)]
        x_out_specs, x_out_shape = [], []
    return pl.pallas_call(
        functools.partial(kern, tm=tm, n_j=n_j),
        grid=(n_seq, n_j),
        in_specs=x_specs + [
            pl.BlockSpec((None, 1, D_MODEL), lay),
            pl.BlockSpec((None, D_MODEL, D_IN), lay),
            pl.BlockSpec((None, 1, Q_W), lay),
            pl.BlockSpec((None, 1, KV_W), lay),
            pl.BlockSpec((256, 256), lambda b, j: (0, 0)),
            pl.BlockSpec((None, 2, 256, 256), lambda b, j: (l, 0, 0, 0)),
            pl.BlockSpec((None, 1, POOL_W), lay),
        ],
        out_specs=x_out_specs + [
            pl.BlockSpec((tm, POOL_W), row),
            pl.BlockSpec((tm, Q_W), row),
            pl.BlockSpec((tm, KV_W), row),
            pl.BlockSpec((KV_W, tm), lambda b, j: (0, b * n_j + j)),
            pl.BlockSpec((None, 16, POOL_W), lambda b, j: (b, 0, 0)),
            pl.BlockSpec((None, WINDOW, KV_W), lambda b, j: (b, 0, 0)),
            pl.BlockSpec((None, WINDOW, KV_W), lambda b, j: (b, 0, 0)),
        ],
        out_shape=x_out_shape + [
            jax.ShapeDtypeStruct((t, POOL_W), BF16),
            jax.ShapeDtypeStruct((t, Q_W), BF16),
            jax.ShapeDtypeStruct((t, KV_W), BF16),
            jax.ShapeDtypeStruct((KV_W, t), BF16),
            jax.ShapeDtypeStruct((n_seq, 16, POOL_W), F32),
            jax.ShapeDtypeStruct((n_seq, WINDOW, KV_W), F32),
            jax.ShapeDtypeStruct((n_seq, WINDOW, KV_W), F32),
        ],
        scratch_shapes=[pltpu.VMEM((tm + HALO, POOL_W), F32)] * 3 + [pltpu.VMEM((tm, Q_W + 2 * KV_W), F32)],
        compiler_params=pltpu.CompilerParams(
            dimension_semantics=("arbitrary", "arbitrary"), vmem_limit_bytes=VMEM_LIMIT),
        name="proj_pool_prompt",
    )(*x_args, g_attn, w_in, qg, kg, bd, wp, ps)


def _attn_kernel(sink_ref, q_ref, kp_ref, kc_ref, vtp_ref, vtc_ref, bias_ref, o_ref, s_ref):
    j = pl.program_id(1)
    kk_all = jnp.concatenate([kp_ref[...], kc_ref[...]], axis=0)
    vt_all = jnp.concatenate([vtp_ref[...], vtc_ref[...]], axis=1)
    from_prev = (lax.broadcasted_iota(jnp.int32, (WINDOW, WINDOW), 0)
                 > lax.broadcasted_iota(jnp.int32, (WINDOW, WINDOW), 1))
    units = [(blk, kv) for blk in range(ATTN_QB) for kv in range(N_KV_HEADS)]

    def scores(n):
        blk, kv = units[n]
        q = q_ref[blk * WINDOW:(blk + 1) * WINDOW, :]
        kk = kk_all[blk * WINDOW:(blk + 2) * WINDOW, kv * HEAD_DIM:(kv + 1) * HEAD_DIM]
        heads = range(kv * GQA_GROUP, (kv + 1) * GQA_GROUP)
        q_rows = jnp.concatenate([q[:, h * HEAD_DIM:(h + 1) * HEAD_DIM] for h in heads], axis=0)
        s_ref[n % 2] = lax.dot_general(kk, q_rows, (((1,), (1,)), ((), ())), preferred_element_type=F32)

    scores(0)
    outs = []
    for n, (blk, kv) in enumerate(units):
        if n + 1 < len(units):
            scores(n + 1)
        vt_kv = vt_all[kv * HEAD_DIM:(kv + 1) * HEAD_DIM, blk * WINDOW:(blk + 2) * WINDOW]
        variant = jnp.minimum(j, 1) if blk == 0 else 1
        for g in range(GQA_GROUP):
            h = kv * GQA_GROUP + g
            s = jnp.where(from_prev, s_ref[n % 2, 0:WINDOW, g * WINDOW:(g + 1) * WINDOW],
                          s_ref[n % 2, WINDOW:, g * WINDOW:(g + 1) * WINDOW]) + bias_ref[variant, h]
            sink = sink_ref[h]
            m = jnp.maximum(jnp.max(s, axis=0, keepdims=True), sink)
            p = jnp.exp(s - m)
            denom = jnp.sum(p, axis=0, keepdims=True) + jnp.exp(sink - m)
            p_keys = jnp.concatenate([jnp.where(from_prev, p, 0.0), jnp.where(from_prev, 0.0, p)], axis=0)
            o_t = jnp.dot(vt_kv, p_keys.astype(BF16), preferred_element_type=F32)
            outs.append(o_t / denom)
        if kv == N_KV_HEADS - 1:
            o_ref[blk * WINDOW:(blk + 1) * WINDOW, :] = jnp.transpose(jnp.concatenate(outs, axis=0)).astype(BF16)
            outs = []


def _attn_prompt(q, k, vt, bias_t, sinks, n_seq, seq):
    tq = ATTN_QB * WINDOW
    nj = seq // tq
    t = n_seq * seq
    cur = lambda b, j: (b * nj + j, 0)
    prev = lambda b, j: (jnp.maximum((b * nj + j) * ATTN_QB - 1, 0), 0)
    cur_t = lambda b, j: (0, b * nj + j)
    prev_t = lambda b, j: (0, jnp.maximum((b * nj + j) * ATTN_QB - 1, 0))
    return pl.pallas_call(
        _attn_kernel,
        grid=(n_seq, nj),
        in_specs=[
            pl.BlockSpec(memory_space=pltpu.SMEM),
            pl.BlockSpec((tq, Q_W), cur),
            pl.BlockSpec((WINDOW, KV_W), prev),
            pl.BlockSpec((tq, KV_W), cur),
            pl.BlockSpec((KV_W, WINDOW), prev_t),
            pl.BlockSpec((KV_W, tq), cur_t),
            pl.BlockSpec((2, N_HEADS, WINDOW, WINDOW), lambda b, j: (0, 0, 0, 0)),
        ],
        out_specs=pl.BlockSpec((tq, Q_W), cur),
        out_shape=jax.ShapeDtypeStruct((t, Q_W), BF16),
        scratch_shapes=[pltpu.VMEM((2, 2 * WINDOW, GQA_GROUP * WINDOW), F32)],
        compiler_params=pltpu.CompilerParams(
            dimension_semantics=("arbitrary", "arbitrary"), vmem_limit_bytes=VMEM_LIMIT),
        name="attn_prompt",
    )(sinks, q, k, k, vt, vt, bias_t)


def _prompt_bias_t():
    r = jnp.arange(WINDOW, dtype=jnp.int32)[None, :]
    c = jnp.arange(WINDOW, dtype=jnp.int32)[:, None]
    from_prev = c > r
    dist = r - c + jnp.where(from_prev, WINDOW, 0)
    slopes = jnp.exp2(-8.0 * jnp.arange(1, N_HEADS + 1, dtype=F32) / N_HEADS)
    later = -slopes[:, None, None] * dist.astype(F32)[None]
    first = jnp.where(from_prev[None], -jnp.inf, later)
    return jnp.stack([first, later])


def _sample_kernel(x_ref, g_ref, win_ref, qg_ref, kg_ref, bd_ref, wp_ref, ps_ref,
                   st_ref, ck_ref, cv_ref, sink_ref, bias_ref, perm_ref,
                   pool_ref, attn_ref, pst_ref, kc_ref, vc_ref, *, ns, pos0):
    u, qn, kn, v = _project(x_ref[...], g_ref[...], win_ref[...], qg_ref[...], kg_ref[...], bd_ref[...])
    pst_ref[:, 0:POOL_STATE - 1, :] = st_ref[:, 1:POOL_STATE, :]
    kc_ref[:, 0:WINDOW - 1, :] = ck_ref[:, 1:WINDOW, :]
    vc_ref[:, 0:WINDOW - 1, :] = cv_ref[:, 1:WINDOW, :]
    for n in range(ns):
        pst_ref[n, POOL_STATE - 1:POOL_STATE, :] = u[n:n + 1, :]
        kc_ref[n, WINDOW - 1:WINDOW, :] = kn[n:n + 1, :]
        vc_ref[n, WINDOW - 1:WINDOW, :] = v[n:n + 1, :]

    d_groups = []
    for gi, w in enumerate(POOL_WINDOWS):
        lo = gi * POOL_GC
        acc = u[:, lo:lo + POOL_GC]
        for back in range(1, w):
            acc = acc + st_ref[:, POOL_STATE - back, lo:lo + POOL_GC]
        d_groups.append(acc / float(min(pos0 + 1, w)) - u[:, lo:lo + POOL_GC])
    pool_ref[...] = _pool_project(d_groups, wp_ref, ps_ref[...]).astype(BF16)

    zeros = jnp.zeros((ns, HEAD_DIM), F32)
    stacked = []
    for h in range(N_HEADS):
        piece = qn[:, h * HEAD_DIM:(h + 1) * HEAD_DIM]
        pair = [piece, zeros] if h < GQA_GROUP else [zeros, piece]
        stacked.append(jnp.concatenate(pair, axis=-1))
    q_hn = jnp.concatenate(stacked, axis=0).astype(BF16)
    q_nh = jnp.dot(perm_ref[0], q_hn, preferred_element_type=F32).astype(BF16)

    keys = kc_ref[...].reshape(ns * WINDOW, KV_W).astype(BF16)
    vals = vc_ref[...].reshape(ns * WINDOW, KV_W).astype(BF16)
    s_all = lax.dot_general(q_nh, keys, (((1,), (1,)), ((), ())), preferred_element_type=F32)
    sink = sink_ref[...]
    bias = bias_ref[...]
    zero_blk = jnp.zeros((N_HEADS, WINDOW), F32)
    p_rows = []
    for n in range(ns):
        s = s_all[n * N_HEADS:(n + 1) * N_HEADS, n * WINDOW:(n + 1) * WINDOW] + bias
        m = jnp.maximum(jnp.max(s, axis=-1, keepdims=True), sink)
        p = jnp.exp(s - m)
        denom = jnp.sum(p, axis=-1, keepdims=True) + jnp.exp(sink - m)
        p_rows.append(jnp.concatenate([zero_blk] * n + [p / denom] + [zero_blk] * (ns - 1 - n), axis=-1))
    p_blockdiag = jnp.concatenate(p_rows, axis=0).astype(BF16)
    o_nh = jnp.dot(p_blockdiag, vals, preferred_element_type=F32).astype(BF16)
    o_hn = jnp.dot(perm_ref[1], o_nh, preferred_element_type=F32)
    pieces = []
    for h in range(N_HEADS):
        kv = h // GQA_GROUP
        pieces.append(o_hn[h * ns:(h + 1) * ns, kv * HEAD_DIM:(kv + 1) * HEAD_DIM])
    attn_ref[...] = jnp.concatenate(pieces, axis=-1).astype(BF16)


def _sample_mixer(l, depth, xs, g_attn, w_in, qg, kg, bd, wp, ps, state, ck, cv, sink8, bias_s, pos0):
    n = xs.shape[0]
    ns = 32
    row = lambda i: (i, 0)
    lay = lambda i: (l, 0, 0)
    src = jnp.arange(ns * N_HEADS)
    perm = (((src % N_HEADS) * ns + src // N_HEADS)[:, None] == src[None, :]).astype(BF16)
    perms = jnp.stack([perm, perm.T])
    return pl.pallas_call(
        functools.partial(_sample_kernel, ns=ns, pos0=pos0),
        grid=(n // ns,),
        input_output_aliases={8: 2, 9: 3, 10: 4},
        in_specs=[
            pl.BlockSpec((ns, D_MODEL), row),
            pl.BlockSpec((None, 1, D_MODEL), lay),
            pl.BlockSpec((None, D_MODEL, D_IN), lay),
            pl.BlockSpec((None, 1, Q_W), lay),
            pl.BlockSpec((None, 1, KV_W), lay),
            pl.BlockSpec((256, 256), lambda i: (0, 0)),
            pl.BlockSpec((None, 2, 256, 256), lambda i: (l, 0, 0, 0)),
            pl.BlockSpec((None, 1, POOL_W), lay),
            pl.BlockSpec((None, ns, POOL_STATE, POOL_W), lambda i: (l, i, 0, 0)),
            pl.BlockSpec((None, ns, WINDOW, KV_W), lambda i: (l, i, 0, 0)),
            pl.BlockSpec((None, ns, WINDOW, KV_W), lambda i: (l, i, 0, 0)),
            pl.BlockSpec((N_HEADS, 1), lambda i: (0, 0)),
            pl.BlockSpec((N_HEADS, WINDOW), lambda i: (0, 0)),
            pl.BlockSpec((2, ns * N_HEADS, ns * N_HEADS), lambda i: (0, 0, 0)),
        ],
        out_specs=[
            pl.BlockSpec((ns, POOL_W), row),
            pl.BlockSpec((ns, Q_W), row),
            pl.BlockSpec((None, ns, POOL_STATE, POOL_W), lambda i: (l, i, 0, 0)),
            pl.BlockSpec((None, ns, WINDOW, KV_W), lambda i: (l, i, 0, 0)),
            pl.BlockSpec((None, ns, WINDOW, KV_W), lambda i: (l, i, 0, 0)),
        ],
        out_shape=[
            jax.ShapeDtypeStruct((n, POOL_W), BF16),
            jax.ShapeDtypeStruct((n, Q_W), BF16),
            jax.ShapeDtypeStruct((depth, n, POOL_STATE, POOL_W), F32),
            jax.ShapeDtypeStruct((depth, n, WINDOW, KV_W), F32),
            jax.ShapeDtypeStruct((depth, n, WINDOW, KV_W), F32),
        ],
        compiler_params=pltpu.CompilerParams(
            dimension_semantics=("arbitrary",), vmem_limit_bytes=VMEM_LIMIT),
        name="sample_mixer",
    )(xs, g_attn, w_in, qg, kg, bd, wp, ps, state, ck, cv, sink8, bias_s, perms)


def _merge_router_kernel(pool_ref, attn_ref, x_ref, wout_ref, g_ref, wr_ref, br_ref, utri_ref, cin_ref,
                         x1_ref, h2_ref, route_t_ref, cnt_ref, y_ref, lg_ref):
    i = pl.program_id(0)

    @pl.when(i == 0)
    def _():
        cnt_ref[...] = cin_ref[...]

    tm = x_ref.shape[0]
    rc = tm // MERGE_CHUNKS
    chunks = [slice(ci * rc, (ci + 1) * rc) for ci in range(MERGE_CHUNKS)]
    for rows in chunks:
        y_ref[rows, :] = (jnp.dot(pool_ref[rows, :], wout_ref[0:POOL_W, :], preferred_element_type=F32)
                          + jnp.dot(attn_ref[rows, :], wout_ref[POOL_W:, :], preferred_element_type=F32))
    for rows in chunks:
        x1 = x_ref[rows, :] + y_ref[rows, :]
        x1_ref[rows, :] = x1
        h2 = _rms_bf16(x1, g_ref[...])
        h2_ref[rows, :] = _pack_bf16_pairs(h2)
        lg_ref[rows, :] = jnp.dot(h2, wr_ref[...], preferred_element_type=F32) + br_ref[...]
    logits = lg_ref[...]

    lt = jnp.transpose(logits)
    sub = lax.broadcasted_iota(jnp.int32, (EXPERTS_PER_GROUP, tm), 0)
    neg = -jnp.inf
    big = jnp.int32(EXPERTS_PER_GROUP)
    gl = jnp.where(sub < N_EXPERT_GROUPS, lt[GROUP_LANE0:GROUP_LANE0 + EXPERTS_PER_GROUP, :], neg)
    gmax = jnp.max(gl, axis=0, keepdims=True)
    grp = jnp.min(jnp.where(gl == gmax, sub, big), axis=0, keepdims=True)
    g_w = 1.0 / jnp.sum(jnp.exp(gl - gmax), axis=0, keepdims=True)
    el = lt[(N_EXPERT_GROUPS - 1) * EXPERTS_PER_GROUP:N_EXPERT_GROUPS * EXPERTS_PER_GROUP, :]
    for gi in range(N_EXPERT_GROUPS - 2, -1, -1):
        el = jnp.where(grp == gi, lt[gi * EXPERTS_PER_GROUP:(gi + 1) * EXPERTS_PER_GROUP, :], el)
    v1 = jnp.max(el, axis=0, keepdims=True)
    i1 = jnp.min(jnp.where(el == v1, sub, big), axis=0, keepdims=True)
    el2 = jnp.where(sub == i1, neg, el)
    v2 = jnp.max(el2, axis=0, keepdims=True)
    i2 = jnp.min(jnp.where(el2 == v2, sub, big), axis=0, keepdims=True)
    e21 = jnp.exp(v2 - v1)
    w1 = g_w / (1.0 + e21)
    w2 = g_w * e21 / (1.0 + e21)
    e1 = grp * EXPERTS_PER_GROUP + i1
    e2 = grp * EXPERTS_PER_GROUP + i2

    esub = lax.broadcasted_iota(jnp.int32, (N_EXPERTS, tm), 0)
    oh1 = esub == e1
    oh2 = esub == e2
    c = jnp.where(oh1 | oh2, 1.0, 0.0)
    prefix = jnp.dot(c.astype(BF16), utri_ref[...], preferred_element_type=F32) + cnt_ref[...]
    r1 = jnp.sum(jnp.where(oh1, prefix, 0.0), axis=0, keepdims=True)
    r2 = jnp.sum(jnp.where(oh2, prefix, 0.0), axis=0, keepdims=True)
    cnt_ref[...] = cnt_ref[...] + jnp.sum(c, axis=1, keepdims=True)

    fields = jnp.zeros((ROUTE_FIELDS, tm), F32)
    for idx, val in enumerate((e1.astype(F32), e2.astype(F32), w1, w2, r1, r2)):
        fields = jnp.where(sub == idx, val, fields)
    route_t_ref[...] = fields


def _merge_router(l, pool, attn, x2d, w_out, g_ffn, wr, br, cnt_in, tm):
    t = x2d.shape[0]
    utri = (jnp.arange(tm)[:, None] < jnp.arange(tm)[None, :]).astype(BF16)
    row = lambda i: (i, 0)
    lay = lambda i: (l, 0, 0)
    return pl.pallas_call(
        _merge_router_kernel,
        grid=(t // tm,),
        in_specs=[
            pl.BlockSpec((tm, POOL_W), row),
            pl.BlockSpec((tm, Q_W), row),
            pl.BlockSpec((tm, D_MODEL), row),
            pl.BlockSpec((None, D_MODEL, D_MODEL), lay),
            pl.BlockSpec((None, 1, D_MODEL), lay),
            pl.BlockSpec((None, D_MODEL, LANES), lay),
            pl.BlockSpec((None, 1, LANES), lay),
            pl.BlockSpec((tm, tm), lambda i: (0, 0)),
            pl.BlockSpec((N_EXPERTS, 1), lambda i: (0, 0)),
        ],
        out_specs=[
            pl.BlockSpec((tm, D_MODEL), row),
            pl.BlockSpec((tm, D_MODEL // 2), row),
            pl.BlockSpec((ROUTE_FIELDS, tm), lambda i: (0, i)),
            pl.BlockSpec((N_EXPERTS, 1), lambda i: (0, 0)),
        ],
        out_shape=[
            jax.ShapeDtypeStruct((t, D_MODEL), F32),
            jax.ShapeDtypeStruct((t, D_MODEL // 2), jnp.int32),
            jax.ShapeDtypeStruct((ROUTE_FIELDS, t), F32),
            jax.ShapeDtypeStruct((N_EXPERTS, 1), F32),
        ],
        scratch_shapes=[pltpu.VMEM((tm, D_MODEL), F32), pltpu.VMEM((tm, LANES), F32)],
        compiler_params=pltpu.CompilerParams(
            dimension_semantics=("arbitrary",), vmem_limit_bytes=VMEM_LIMIT),
        name="merge_router",
    )(pool, attn, x2d, w_out, g_ffn, wr, br, utri, cnt_in)


def _moe_kernel(be_ref, rv_ref, nx_ref, sl_ref, first_ref, xd_ref, wg_hbm, wu_hbm, wd_hbm, yd_ref,
                wg_f, wu_f, wd_f, wg_s, wu_s, wd_s, sem, *, layer):
    step = pl.program_id(0)

    def weight_copies(e, s):
        return [pltpu.make_async_copy(w_hbm.at[layer, e], w_f.at[s], sem.at[s, n])
                for n, (w_hbm, w_f) in enumerate(((wg_hbm, wg_f), (wu_hbm, wu_f), (wd_hbm, wd_f)))]

    @pl.when(step == 0)
    def _():
        for s in range(WEIGHT_SLOTS - 1):
            @pl.when(first_ref[s] >= 0)
            def _():
                for c in weight_copies(first_ref[s], s):
                    c.start()

    def enter_expert(i):
        expert, slot = be_ref[i], sl_ref[i]

        @pl.when((i == 0) | (expert != be_ref[jnp.maximum(i - 1, 0)]))
        def _():
            for c in weight_copies(expert, slot):
                c.wait()

            @pl.when(nx_ref[i] >= 0)
            def _():
                for c in weight_copies(nx_ref[i], lax.rem(slot + WEIGHT_SLOTS - 1, WEIGHT_SLOTS)):
                    c.start(priority=1)

            wg_s[...] = wg_f[slot].astype(BF16)
            wu_s[...] = wu_f[slot].astype(BF16)
            wd_s[...] = wd_f[slot].astype(BF16)

    def experts_on(row0, n_rows, rows_valid):
        rows = pl.ds(row0, n_rows)
        row = lax.broadcasted_iota(jnp.int32, (n_rows, D_MODEL // 2), 0)
        x = _unpack_bf16_pairs(jnp.where(row < rows_valid, xd_ref[rows, :], 0)).astype(BF16)
        gate = jnp.dot(x, wg_s[...], preferred_element_type=F32)
        up = jnp.dot(x, wu_s[...], preferred_element_type=F32)
        act = (gate * jax.nn.sigmoid(gate) * up).astype(BF16)
        y = jnp.dot(act, wd_s[...], preferred_element_type=F32)
        yd_ref[rows, :] = _pack_bf16_pairs(y.astype(BF16))

    def experts_ragged(row0, lead_rows, rows_last):
        half = MOE_BM // 2

        @pl.when(rows_last > half)
        def _():
            experts_on(row0, lead_rows + MOE_BM, lead_rows + rows_last)

        @pl.when(rows_last <= half)
        def _():
            experts_on(row0, lead_rows + half, lead_rows + rows_last)
            yd_ref[pl.ds(row0 + lead_rows + half, half), :] = jnp.zeros((half, D_MODEL // 2), jnp.int32)

    def single_block(i, row0):
        enter_expert(i)

        @pl.when(rv_ref[i] > 0)
        def _():
            experts_ragged(row0, 0, rv_ref[i])

        @pl.when(rv_ref[i] <= 0)
        def _():
            yd_ref[pl.ds(row0, MOE_BM), :] = jnp.zeros((MOE_BM, D_MODEL // 2), jnp.int32)

    @pl.when(rv_ref[step * MOE_STEP_BLOCKS] > 0)
    def _():
        for pair in range(MOE_STEP_BLOCKS // 2):
            ia = step * MOE_STEP_BLOCKS + 2 * pair
            ib = ia + 1
            row0 = 2 * pair * MOE_BM
            same = (be_ref[ib] == be_ref[ia]) & (rv_ref[ib] > 0)

            @pl.when(same)
            def _():
                enter_expert(ia)
                experts_ragged(row0, MOE_BM, rv_ref[ib])

            @pl.when(jnp.logical_not(same))
            def _():
                single_block(ia, row0)
                single_block(ib, row0 + MOE_BM)


def _moe_experts(l, block_e, rows_valid, next_e, slot, first_e, xd, w_gate, w_up, w_down):
    n_blocks = xd.shape[0] // MOE_BM
    step_rows = MOE_STEP_BLOCKS * MOE_BM
    row = lambda i, be, rv, nx, sl, fe: (jnp.minimum(i, fe[WEIGHT_SLOTS - 1] - 1), 0)
    return pl.pallas_call(
        functools.partial(_moe_kernel, layer=l),
        grid_spec=pltpu.PrefetchScalarGridSpec(
            num_scalar_prefetch=5,
            grid=(n_blocks // MOE_STEP_BLOCKS,),
            in_specs=[
                pl.BlockSpec((step_rows, D_MODEL // 2), row),
                pl.BlockSpec(memory_space=pl.ANY),
                pl.BlockSpec(memory_space=pl.ANY),
                pl.BlockSpec(memory_space=pl.ANY),
            ],
            out_specs=pl.BlockSpec((step_rows, D_MODEL // 2), row),
            scratch_shapes=[
                pltpu.VMEM((WEIGHT_SLOTS, D_MODEL, EXPERT_FF), F32),
                pltpu.VMEM((WEIGHT_SLOTS, D_MODEL, EXPERT_FF), F32),
                pltpu.VMEM((WEIGHT_SLOTS, EXPERT_FF, D_MODEL), F32),
                pltpu.VMEM((D_MODEL, EXPERT_FF), BF16),
                pltpu.VMEM((D_MODEL, EXPERT_FF), BF16),
                pltpu.VMEM((EXPERT_FF, D_MODEL), BF16),
                pltpu.SemaphoreType.DMA((WEIGHT_SLOTS, 3)),
            ],
        ),
        out_shape=jax.ShapeDtypeStruct((n_blocks * MOE_BM, D_MODEL // 2), jnp.int32),
        compiler_params=pltpu.CompilerParams(
            dimension_semantics=("arbitrary",), vmem_limit_bytes=VMEM_LIMIT),
        name="moe_experts",
    )(block_e, rows_valid, next_e, slot, first_e, xd, w_gate, w_up, w_down)


def _sc_worker_id():
    return lax.axis_index("s") * SC_CORES + lax.axis_index("c")


def _sc_dispatch(hp, hs, dest_p, dest_s, n_rows):
    tp, width = hp.shape
    per_w = tp // SC_WORKERS
    n_ch = per_w // DISP_CH
    n_sw = hs.shape[0] // SAMPLE_CH
    mesh = plsc.VectorSubcoreMesh(core_axis_name="c", subcore_axis_name="s")

    @functools.partial(
        pl.kernel, mesh=mesh,
        out_type=jax.ShapeDtypeStruct((n_rows, width), jnp.int32),
        scratch_types=[
            pltpu.VMEM((2, n_ch, DISP_CH), jnp.int32),
            pltpu.VMEM((2, 1, SAMPLE_CH), jnp.int32),
            pltpu.VMEM((SC_RING, DISP_CH, width), jnp.int32),
            pltpu.SemaphoreType.DMA((SC_RING,)),
            pltpu.SemaphoreType.DMA((SC_RING, 2)),
        ],
        name="sc_dispatch",
    )
    def k(hp_hbm, hs_hbm, dp_hbm, ds_hbm, xd_hbm, idx_v, idxs_v, bufs, rsem, wsem):
        wid = _sc_worker_id()
        base = wid * per_w
        for kk in range(2):
            pltpu.sync_copy(dp_hbm.at[kk, wid], idx_v.at[kk])
        reads = [pltpu.make_async_copy(hp_hbm.at[pl.ds(base + j * DISP_CH, DISP_CH)],
                                       bufs.at[j % SC_RING], rsem.at[j % SC_RING]) for j in range(n_ch)]
        writes = [[pltpu.make_async_copy(bufs.at[j % SC_RING], xd_hbm.at[idx_v.at[kk, j]],
                                         wsem.at[j % SC_RING, kk]) for kk in range(2)] for j in range(n_ch)]
        for j in range(min(SC_RING - 1, n_ch)):
            reads[j].start()
        for j in range(n_ch):
            reads[j].wait()
            for w in writes[j]:
                w.start()
            if j >= 1:
                for w in writes[j - 1]:
                    w.wait()
            if j + SC_RING - 1 < n_ch:
                reads[j + SC_RING - 1].start()
        for w in writes[n_ch - 1]:
            w.wait()

        @pl.when(wid < n_sw)
        def _():
            rows = bufs.at[0, pl.ds(0, SAMPLE_CH)]
            for kk in range(2):
                pltpu.sync_copy(ds_hbm.at[kk, wid], idxs_v.at[kk])
            pltpu.sync_copy(hs_hbm.at[pl.ds(wid * SAMPLE_CH, SAMPLE_CH)], rows)
            for kk in range(2):
                pltpu.sync_copy(rows, xd_hbm.at[idxs_v.at[kk, 0]])

    return k(hp, hs, dest_p, dest_s)


def _sc_combine_gather(yd, dest_p, dest_s, tp, ts):
    width = yd.shape[1]
    per_w = tp // SC_WORKERS
    n_ch = per_w // COMB_CH
    n_sw = ts // SAMPLE_CH
    mesh = plsc.VectorSubcoreMesh(core_axis_name="c", subcore_axis_name="s")

    @functools.partial(
        pl.kernel, mesh=mesh,
        out_type=jax.ShapeDtypeStruct((2, tp + ts, width), yd.dtype),
        scratch_types=[
            pltpu.VMEM((2, n_ch, COMB_CH), jnp.int32),
            pltpu.VMEM((2, 1, SAMPLE_CH), jnp.int32),
            pltpu.VMEM((SC_RING, COMB_CH, width), yd.dtype),
            pltpu.SemaphoreType.DMA((SC_RING,)),
            pltpu.SemaphoreType.DMA((SC_RING,)),
        ],
        name="sc_combine_gather",
    )
    def k(yd_hbm, dp_hbm, ds_hbm, g_hbm, idx_v, idxs_v, bufs, gsem, wsem):
        wid = _sc_worker_id()
        base = wid * per_w
        for kk in range(2):
            pltpu.sync_copy(dp_hbm.at[kk, wid], idx_v.at[kk])
        items = [(kk, j) for kk in range(2) for j in range(n_ch)]
        n_items = len(items)
        gathers = [pltpu.make_async_copy(yd_hbm.at[idx_v.at[kk, j]], bufs.at[n % SC_RING], gsem.at[n % SC_RING])
                   for n, (kk, j) in enumerate(items)]
        outs = [pltpu.make_async_copy(bufs.at[n % SC_RING], g_hbm.at[kk, pl.ds(base + j * COMB_CH, COMB_CH)],
                                      wsem.at[n % SC_RING]) for n, (kk, j) in enumerate(items)]
        for n in range(min(SC_RING - 1, n_items)):
            gathers[n].start()
        for n in range(n_items):
            gathers[n].wait()
            outs[n].start()
            if n >= 1:
                outs[n - 1].wait()
            if n + SC_RING - 1 < n_items:
                gathers[n + SC_RING - 1].start()
        outs[n_items - 1].wait()

        @pl.when(wid < n_sw)
        def _():
            for kk in range(2):
                pltpu.sync_copy(ds_hbm.at[kk, wid], idxs_v.at[kk])
            for kk in range(2):
                rows = bufs.at[kk, pl.ds(0, SAMPLE_CH)]
                pltpu.sync_copy(yd_hbm.at[idxs_v.at[kk, 0]], rows)
                pltpu.sync_copy(rows, g_hbm.at[kk, pl.ds(tp + wid * SAMPLE_CH, SAMPLE_CH)])

    return k(yd, dest_p, dest_s)


def _combine_kernel(x1_ref, g_ref, route_t_ref, x2_ref):
    fields = route_t_ref[...]
    tm = fields.shape[1]
    cols = jnp.transpose(jnp.concatenate([fields, jnp.zeros((LANES - ROUTE_FIELDS, tm), F32)], axis=0))
    w1 = cols[:, 2:3]
    w2 = cols[:, 3:4]
    x2_ref[...] = x1_ref[...] + _unpack_bf16_pairs(g_ref[0]) * w1 + _unpack_bf16_pairs(g_ref[1]) * w2


def _combine(x1, g, route_t, row0, tm):
    t = x1.shape[0]
    blk0 = row0 // tm
    row = lambda i: (i, 0)
    return pl.pallas_call(
        _combine_kernel,
        grid=(t // tm,),
        in_specs=[
            pl.BlockSpec((tm, D_MODEL), row),
            pl.BlockSpec((2, tm, D_MODEL // 2), lambda i: (0, blk0 + i, 0)),
            pl.BlockSpec((ROUTE_FIELDS, tm), lambda i: (0, i)),
        ],
        out_specs=pl.BlockSpec((tm, D_MODEL), row),
        out_shape=jax.ShapeDtypeStruct((t, D_MODEL), F32),
        compiler_params=pltpu.CompilerParams(
            dimension_semantics=("arbitrary",), vmem_limit_bytes=VMEM_LIMIT),
        name="combine",
    )(x1, g, route_t)


def _dest_layout(dest, workers, chunk):
    t = dest.shape[1]
    return dest.reshape(2, workers, t // (workers * chunk), chunk)


def _hier_moe(l, h2p, h2s, route_tp, route_ts, counts, w_gate, w_up, w_down):
    tp, ts = h2p.shape[0], h2s.shape[0]
    n_assign = 2 * (tp + ts)
    n_blocks = -(-n_assign // MOE_BM) + N_EXPERTS
    n_blocks = -(-n_blocks // MOE_STEP_BLOCKS) * MOE_STEP_BLOCKS
    pcounts = (counts + MOE_BM - 1) // MOE_BM * MOE_BM
    pends = jnp.cumsum(pcounts)
    poffsets = pends - pcounts
    starts = jnp.arange(n_blocks, dtype=jnp.int32) * MOE_BM
    block_e = jnp.minimum(jnp.sum((pends[None, :] <= starts[:, None]).astype(jnp.int32), axis=1),
                          N_EXPERTS - 1)
    experts = jnp.arange(N_EXPERTS, dtype=jnp.int32)

    def lookup(table, idx):
        return jnp.sum(jnp.where(idx[..., None] == experts, table, 0), axis=-1)

    rows_valid = jnp.clip(lookup(poffsets + counts, block_e) - starts, 0, MOE_BM).astype(jnp.int32)
    used = counts > 0
    last_e = jnp.max(jnp.where(used, jnp.arange(N_EXPERTS, dtype=jnp.int32), 0))
    block_e = jnp.where(rows_valid > 0, block_e, last_e).astype(jnp.int32)
    place = jnp.cumsum(used.astype(jnp.int32)) - 1
    by_place = jnp.sum(jnp.where(used[None, :] & (place[None, :] == experts[:, None]), experts[None, :], 0),
                       axis=1)
    n_used = jnp.sum(used.astype(jnp.int32))

    def at_place(p):
        return jnp.where(p < n_used, lookup(by_place, jnp.minimum(p, N_EXPERTS - 1)), -1).astype(jnp.int32)

    ahead_of = at_place(place + (WEIGHT_SLOTS - 1))
    next_e = lookup(ahead_of, block_e)
    slot = lookup(place % WEIGHT_SLOTS, block_e)
    n_steps_used = -(-(pends[-1] // MOE_BM) // MOE_STEP_BLOCKS)
    first_e = jnp.concatenate([at_place(jnp.arange(WEIGHT_SLOTS - 1, dtype=jnp.int32)),
                               n_steps_used.reshape(1).astype(jnp.int32)])

    def dest_of(route_t):
        return lookup(poffsets, route_t[0:2].astype(jnp.int32)) + route_t[4:6].astype(jnp.int32)

    dest_p, dest_s = dest_of(route_tp), dest_of(route_ts)
    n_sw = ts // SAMPLE_CH
    xd = _sc_dispatch(h2p, h2s, _dest_layout(dest_p, SC_WORKERS, DISP_CH),
                      _dest_layout(dest_s, n_sw, SAMPLE_CH), n_blocks * MOE_BM)
    yd = _moe_experts(l, block_e, rows_valid, next_e, slot, first_e, xd, w_gate, w_up, w_down)
    return _sc_combine_gather(yd, _dest_layout(dest_p, SC_WORKERS, COMB_CH),
                              _dest_layout(dest_s, n_sw, SAMPLE_CH), tp, ts)


def kernel(x_prompt, x_sample, state_pool, cache_k_win, cache_v_win, norm_attn_g, w_in, pool_w, pool_scale, q_norm_g, k_norm_g, attn_sinks, w_out, norm_ffn_g, router_group_w, router_group_b, router_expert_w, router_expert_b, w_gate, w_up, w_down):
    n_p, t_p, d = x_prompt.shape
    n_s, t_s, _ = x_sample.shape
    depth = w_in.shape[0]
    lw_s = cache_k_win.shape[2]
    assert t_s == 1 and lw_s == WINDOW and d == D_MODEL
    assert t_p % TM_PROJ == 0 and t_p >= WINDOW

    seg = jnp.arange(256) // HEAD_DIM
    bd = jnp.where(seg[:, None] == seg[None, :], 1.0 / HEAD_DIM, 0.0).astype(BF16)
    slopes = jnp.exp2(-8.0 * jnp.arange(1, N_HEADS + 1, dtype=F32) / N_HEADS)
    bias_p = _prompt_bias_t()
    dist_s = (WINDOW - 1) - jnp.arange(WINDOW, dtype=F32)
    bias_s = -slopes[:, None] * dist_s[None, :]

    wp = jnp.zeros((depth, 2, 256, 256), F32)
    for p in range(2):
        wp = wp.at[:, p, :POOL_GC, :POOL_GC].set(pool_w[:, 2 * p])
        wp = wp.at[:, p, POOL_GC:, POOL_GC:].set(pool_w[:, 2 * p + 1])
    assert GROUP_LANE0 == N_EXPERTS
    lane_pad = LANES - N_EXPERTS - N_EXPERT_GROUPS
    wr = jnp.concatenate([router_expert_w, router_group_w, jnp.zeros((depth, D_MODEL, lane_pad), F32)], axis=-1)
    br = jnp.concatenate([router_expert_b, router_group_b, jnp.zeros((depth, lane_pad), F32)],
                         axis=-1).reshape(depth, 1, LANES)
    lp = dict(
        w_in=w_in.astype(BF16),
        w_out=w_out.astype(BF16),
        g_attn=norm_attn_g.reshape(depth, 1, D_MODEL),
        g_ffn=norm_ffn_g.reshape(depth, 1, D_MODEL),
        qg=(jnp.tile(q_norm_g, (1, N_HEADS)) * ATTN_SCALE).reshape(depth, 1, Q_W),
        kg=jnp.tile(k_norm_g, (1, N_KV_HEADS)).reshape(depth, 1, KV_W),
        wp=wp.astype(BF16),
        ps=pool_scale.reshape(depth, 1, POOL_W),
        wr=wr.astype(BF16),
        br=br,
        state=state_pool,
        ck=cache_k_win.reshape(depth, n_s, lw_s, KV_W),
        cv=cache_v_win.reshape(depth, n_s, lw_s, KV_W),
    )

    xp = x_prompt.reshape(n_p * t_p, D_MODEL)
    xs = x_sample.reshape(n_s, D_MODEL)
    lw_p = min(WINDOW, t_p)
    pool_p, kp_new, vp_new = [], [], []
    sample_state = [lp["state"], lp["ck"], lp["cv"]]
    zero_cnt = jnp.zeros((N_EXPERTS, 1), F32)
    pending = None
    for l in range(depth):
        sinks = attn_sinks[l]
        outs = _proj_pool_prompt(
            l, xp if pending is None else pending, n_p, t_p,
            lp["g_attn"], lp["w_in"], lp["qg"], lp["kg"], bd, lp["wp"], lp["ps"])
        if pending is not None:
            xp, outs = outs[0], outs[1:]
        pool_o, q, k, vt, utail, ktail, vtail = outs
        attn_o = _attn_prompt(q, k, vt, bias_p, sinks, n_p, t_p)
        x1p, h2p, route_tp, cnt_p = _merge_router(
            l, pool_o, attn_o, xp, lp["w_out"], lp["g_ffn"], lp["wr"], lp["br"], zero_cnt, TM_MERGE)
        pool_p.append(utail[:, 16 - POOL_STATE:, :])
        kp_new.append(ktail)
        vp_new.append(vtail)
        pool_so, attn_so, *sample_state = _sample_mixer(
            l, depth, xs, lp["g_attn"], lp["w_in"], lp["qg"], lp["kg"], bd, lp["wp"], lp["ps"],
            *sample_state, sinks.reshape(N_HEADS, 1), bias_s, PAST_LEN)
        x1s, h2s, route_ts, cnt_all = _merge_router(
            l, pool_so, attn_so, xs, lp["w_out"], lp["g_ffn"], lp["wr"], lp["br"], cnt_p, n_s)
        counts = cnt_all[:, 0].astype(jnp.int32)
        g = _hier_moe(l, h2p, h2s, route_tp, route_ts, counts, w_gate, w_up, w_down)
        xs = _combine(x1s, g, route_ts, n_p * t_p, n_s)
        pending = (x1p, g, route_tp)
    xp = _combine(*pending, 0, TM_MERGE)
    return (xp.reshape(n_p, t_p, D_MODEL), xs.reshape(n_s, t_s, D_MODEL),
            jnp.stack(pool_p),
            jnp.stack(kp_new).reshape(depth, n_p, lw_p, N_KV_HEADS, HEAD_DIM),
            jnp.stack(vp_new).reshape(depth, n_p, lw_p, N_KV_HEADS, HEAD_DIM),
            sample_state[0],
            sample_state[1].reshape(depth, n_s, lw_s, N_KV_HEADS, HEAD_DIM),
            sample_state[2].reshape(depth, n_s, lw_s, N_KV_HEADS, HEAD_DIM))
```

```python
import functools

import jax
import jax.numpy as jnp
from jax import lax
from jax.experimental import pallas as pl
from jax.experimental.pallas import tpu as pltpu
from jax.experimental.pallas import tpu_sc as plsc

D_MODEL = 1024
POOL_W = 512
POOL_WINDOWS = (2, 4, 8, 16)
POOL_GC = 128
POOL_STATE = 15
HEAD_DIM = 64
N_HEADS = 8
N_KV_HEADS = 2
GQA_GROUP = 4
Q_W = 512
KV_W = 128
D_IN = POOL_W + Q_W + 2 * KV_W
WINDOW = 128
ATTN_SCALE = HEAD_DIM ** -0.5
N_EXPERT_GROUPS = 4
EXPERTS_PER_GROUP = 8
N_EXPERTS = 32
EXPERT_FF = 512
EPS = 1e-6
PAST_LEN = 16384

LANES = 128
HALO = 32
TM_PROJ = 1024
TM_MERGE = 1024
MERGE_CHUNKS = 4
ATTN_QB = 16
MOE_BM = 256
MOE_STEP_BLOCKS = 8
WEIGHT_SLOTS = 3
GROUP_LANE0 = 32
ROUTE_FIELDS = 8
SC_CORES = 2
SC_SUBCORES = 16
SC_WORKERS = SC_CORES * SC_SUBCORES
DISP_CH = 64
COMB_CH = 64
SAMPLE_CH = 32
SC_RING = 3
VMEM_LIMIT = 48 * 1024 * 1024

BF16 = jnp.bfloat16
F32 = jnp.float32


def _pack_bf16_pairs(h):
    w = h.shape[1] // 2
    hi = lax.bitcast_convert_type(h[:, :w].astype(F32), jnp.uint32)
    lo = lax.bitcast_convert_type(h[:, w:].astype(F32), jnp.uint32)
    return lax.bitcast_convert_type(hi | (lo >> 16), jnp.int32)


def _unpack_bf16_pairs(words):
    u = lax.bitcast_convert_type(words, jnp.uint32)
    hi = lax.bitcast_convert_type(u & jnp.uint32(0xFFFF0000), F32)
    lo = lax.bitcast_convert_type(u << 16, F32)
    return jnp.concatenate([hi, lo], axis=-1)


def _segment_mean_sq(a, bd):
    w = a.shape[1]
    return jnp.dot((a * a).astype(BF16), bd[:w, :w], preferred_element_type=F32)


def _rms_bf16(x, g):
    ms = jnp.mean(x * x, axis=-1, keepdims=True)
    return (x * lax.rsqrt(ms + EPS) * g).astype(BF16)


def _qk_norm(q, k, qg, kg, bd):
    qn = []
    for c in range(Q_W // 256):
        qc = q[:, c * 256:(c + 1) * 256]
        qn.append(qc * lax.rsqrt(_segment_mean_sq(qc, bd) + EPS))
    qn = jnp.concatenate(qn, axis=-1) * qg
    kn = k * lax.rsqrt(_segment_mean_sq(k, bd) + EPS) * kg
    return qn, kn


def _project(x, g, w_in, qg, kg, bd):
    z = jnp.dot(_rms_bf16(x, g), w_in, preferred_element_type=F32)
    u = z[:, :POOL_W]
    q = z[:, POOL_W:POOL_W + Q_W]
    k = z[:, POOL_W + Q_W:POOL_W + Q_W + KV_W]
    v = z[:, POOL_W + Q_W + KV_W:]
    qn, kn = _qk_norm(q, k, qg, kg, bd)
    return u, qn, kn, v


def _pool_project(d_groups, wp_ref, ps):
    outs = []
    for p in range(2):
        dp = jnp.concatenate([d_groups[2 * p], d_groups[2 * p + 1]], axis=-1).astype(BF16)
        y = jnp.dot(dp, wp_ref[p], preferred_element_type=F32)
        outs.append(y * ps[:, p * 256:(p + 1) * 256])
    return jnp.concatenate(outs, axis=-1)


def _proj_pool_kernel(x_ref, g_ref, win_ref, qg_ref, kg_ref, bd_ref, wp_ref, ps_ref,
                      pool_ref, q_ref, k_ref, vt_ref, utail_ref, ktail_ref, vtail_ref,
                      ext_ref, sa_ref, sb_ref, zq_ref, *, tm, n_j):
    j = pl.program_id(1)

    @pl.when(j == 0)
    def _():
        ext_ref[0:HALO, :] = jnp.zeros((HALO, POOL_W), F32)

    r = tm + HALO
    h = _rms_bf16(x_ref[...], g_ref[...])
    ext_ref[HALO:r, :] = jnp.dot(h, win_ref[:, 0:POOL_W], preferred_element_type=F32)
    zq_ref[...] = jnp.dot(h, win_ref[:, POOL_W:], preferred_element_type=F32)
    u = ext_ref[HALO:r, :]
    sa_ref[8:r, :] = ext_ref[8:r, :] + ext_ref[7:r - 1, :]
    sb_ref[16:r, 128:] = sa_ref[16:r, 128:] + sa_ref[14:r - 2, 128:]
    sa_ref[24:r, 256:] = sb_ref[24:r, 256:] + sb_ref[20:r - 4, 256:]
    sb_ref[32:r, 384:] = sa_ref[32:r, 384:] + sa_ref[24:r - 8, 384:]
    pos1 = j * tm + lax.broadcasted_iota(jnp.int32, (tm, POOL_GC), 0) + 1
    sums = (sa_ref, sb_ref, sa_ref, sb_ref)
    d_groups = []
    for gi, w in enumerate(POOL_WINDOWS):
        sl = slice(gi * POOL_GC, (gi + 1) * POOL_GC)
        cnt = jnp.minimum(pos1, w).astype(F32)
        d_groups.append(sums[gi][HALO:r, sl] / cnt - u[:, sl])
    pool_ref[...] = _pool_project(d_groups, wp_ref, ps_ref[...]).astype(BF16)
    ext_ref[16:HALO, :] = ext_ref[tm + 16:r, :]

    qn, kn = _qk_norm(zq_ref[:, 0:Q_W], zq_ref[:, Q_W:Q_W + KV_W], qg_ref[...], kg_ref[...], bd_ref[...])
    v = zq_ref[:, Q_W + KV_W:]
    q_ref[...] = qn.astype(BF16)
    k_ref[...] = kn.astype(BF16)
    vt_ref[...] = jnp.transpose(v).astype(BF16)

    @pl.when(j == n_j - 1)
    def _():
        utail_ref[...] = u[tm - 16:, :]
        ktail_ref[...] = kn[tm - WINDOW:, :]
        vtail_ref[...] = v[tm - WINDOW:, :]


def _proj_pool_combine_kernel(x1_ref, gath_ref, route_ref, *rest, tm, n_j):
    x2_ref = rest[7]
    _combine_kernel(x1_ref, gath_ref, route_ref, x2_ref)
    _proj_pool_kernel(x2_ref, *rest[:7], *rest[8:], tm=tm, n_j=n_j)


def _proj_pool_prompt(l, x_in, n_seq, seq, g_attn, w_in, qg, kg, bd, wp, ps):
    tm = TM_PROJ
    n_j = seq // tm
    t = n_seq * seq
    row = lambda b, j: (b * n_j + j, 0)
    lay = lambda b, j: (l, 0, 0)
    fused = isinstance(x_in, tuple)
    if fused:
        kern = _proj_pool_combine_kernel
        x_args = list(x_in)
        x_specs = [pl.BlockSpec((tm, D_MODEL), row),
                   pl.BlockSpec((2, tm, D_MODEL // 2), lambda b, j: (0, b * n_j + j, 0)),
                   pl.BlockSpec((ROUTE_FIELDS, tm), lambda b, j: (0, b * n_j + j))]
        x_out_specs = [pl.BlockSpec((tm, D_MODEL), row)]
        x_out_shape = [jax.ShapeDtypeStruct((t, D_MODEL), F32)]
    else:
        kern = _proj_pool_kernel
        x_args = [x_in]
        x_specs = [pl.BlockSpec((tm, D_MODEL), row)]
        x_out_specs, x_out_shape = [], []
    return pl.pallas_call(
        functools.partial(kern, tm=tm, n_j=n_j),
        grid=(n_seq, n_j),
        in_specs=x_specs + [
            pl.BlockSpec((None, 1, D_MODEL), lay),
            pl.BlockSpec((None, D_MODEL, D_IN), lay),
            pl.BlockSpec((None, 1, Q_W), lay),
            pl.BlockSpec((None, 1, KV_W), lay),
            pl.BlockSpec((256, 256), lambda b, j: (0, 0)),
            pl.BlockSpec((None, 2, 256, 256), lambda b, j: (l, 0, 0, 0)),
            pl.BlockSpec((None, 1, POOL_W), lay),
        ],
        out_specs=x_out_specs + [
            pl.BlockSpec((tm, POOL_W), row),
            pl.BlockSpec((tm, Q_W), row),
            pl.BlockSpec((tm, KV_W), row),
            pl.BlockSpec((KV_W, tm), lambda b, j: (0, b * n_j + j)),
            pl.BlockSpec((None, 16, POOL_W), lambda b, j: (b, 0, 0)),
            pl.BlockSpec((None, WINDOW, KV_W), lambda b, j: (b, 0, 0)),
            pl.BlockSpec((None, WINDOW, KV_W), lambda b, j: (b, 0, 0)),
        ],
        out_shape=x_out_shape + [
            jax.ShapeDtypeStruct((t, POOL_W), BF16),
            jax.ShapeDtypeStruct((t, Q_W), BF16),
            jax.ShapeDtypeStruct((t, KV_W), BF16),
            jax.ShapeDtypeStruct((KV_W, t), BF16),
            jax.ShapeDtypeStruct((n_seq, 16, POOL_W), F32),
            jax.ShapeDtypeStruct((n_seq, WINDOW, KV_W), F32),
            jax.ShapeDtypeStruct((n_seq, WINDOW, KV_W), F32),
        ],
        scratch_shapes=[pltpu.VMEM((tm + HALO, POOL_W), F32)] * 3 + [pltpu.VMEM((tm, Q_W + 2 * KV_W), F32)],
        compiler_params=pltpu.CompilerParams(
            dimension_semantics=("arbitrary", "arbitrary"), vmem_limit_bytes=VMEM_LIMIT),
        name="proj_pool_prompt",
    )(*x_args, g_attn, w_in, qg, kg, bd, wp, ps)


def _attn_kernel(sink_ref, q_ref, kp_ref, kc_ref, vtp_ref, vtc_ref, bias_ref, o_ref, s_ref):
    j = pl.program_id(1)
    kk_all = jnp.concatenate([kp_ref[...], kc_ref[...]], axis=0)
    vt_all = jnp.concatenate([vtp_ref[...], vtc_ref[...]], axis=1)
    from_prev = (lax.broadcasted_iota(jnp.int32, (WINDOW, WINDOW), 0)
                 > lax.broadcasted_iota(jnp.int32, (WINDOW, WINDOW), 1))
    units = [(blk, kv) for blk in range(ATTN_QB) for kv in range(N_KV_HEADS)]

    def scores(n):
        blk, kv = units[n]
        q = q_ref[blk * WINDOW:(blk + 1) * WINDOW, :]
        kk = kk_all[blk * WINDOW:(blk + 2) * WINDOW, kv * HEAD_DIM:(kv + 1) * HEAD_DIM]
        heads = range(kv * GQA_GROUP, (kv + 1) * GQA_GROUP)
        q_rows = jnp.concatenate([q[:, h * HEAD_DIM:(h + 1) * HEAD_DIM] for h in heads], axis=0)
        s_ref[n % 2] = lax.dot_general(kk, q_rows, (((1,), (1,)), ((), ())), preferred_element_type=F32)

    scores(0)
    outs = []
    for n, (blk, kv) in enumerate(units):
        if n + 1 < len(units):
            scores(n + 1)
        vt_kv = vt_all[kv * HEAD_DIM:(kv + 1) * HEAD_DIM, blk * WINDOW:(blk + 2) * WINDOW]
        variant = jnp.minimum(j, 1) if blk == 0 else 1
        for g in range(GQA_GROUP):
            h = kv * GQA_GROUP + g
            s = jnp.where(from_prev, s_ref[n % 2, 0:WINDOW, g * WINDOW:(g + 1) * WINDOW],
                          s_ref[n % 2, WINDOW:, g * WINDOW:(g + 1) * WINDOW]) + bias_ref[variant, h]
            sink = sink_ref[h]
            m = jnp.maximum(jnp.max(s, axis=0, keepdims=True), sink)
            p = jnp.exp(s - m)
            denom = jnp.sum(p, axis=0, keepdims=True) + jnp.exp(sink - m)
            p_keys = jnp.concatenate([jnp.where(from_prev, p, 0.0), jnp.where(from_prev, 0.0, p)], axis=0)
            o_t = jnp.dot(vt_kv, p_keys.astype(BF16), preferred_element_type=F32)
            outs.append(o_t / denom)
        if kv == N_KV_HEADS - 1:
            o_ref[blk * WINDOW:(blk + 1) * WINDOW, :] = jnp.transpose(jnp.concatenate(outs, axis=0)).astype(BF16)
            outs = []


def _attn_prompt(q, k, vt, bias_t, sinks, n_seq, seq):
    tq = ATTN_QB * WINDOW
    nj = seq // tq
    t = n_seq * seq
    cur = lambda b, j: (b * nj + j, 0)
    prev = lambda b, j: (jnp.maximum((b * nj + j) * ATTN_QB - 1, 0), 0)
    cur_t = lambda b, j: (0, b * nj + j)
    prev_t = lambda b, j: (0, jnp.maximum((b * nj + j) * ATTN_QB - 1, 0))
    return pl.pallas_call(
        _attn_kernel,
        grid=(n_seq, nj),
        in_specs=[
            pl.BlockSpec(memory_space=pltpu.SMEM),
            pl.BlockSpec((tq, Q_W), cur),
            pl.BlockSpec((WINDOW, KV_W), prev),
            pl.BlockSpec((tq, KV_W), cur),
            pl.BlockSpec((KV_W, WINDOW), prev_t),
            pl.BlockSpec((KV_W, tq), cur_t),
            pl.BlockSpec((2, N_HEADS, WINDOW, WINDOW), lambda b, j: (0, 0, 0, 0)),
        ],
        out_specs=pl.BlockSpec((tq, Q_W), cur),
        out_shape=jax.ShapeDtypeStruct((t, Q_W), BF16),
        scratch_shapes=[pltpu.VMEM((2, 2 * WINDOW, GQA_GROUP * WINDOW), F32)],
        compiler_params=pltpu.CompilerParams(
            dimension_semantics=("arbitrary", "arbitrary"), vmem_limit_bytes=VMEM_LIMIT),
        name="attn_prompt",
    )(sinks, q, k, k, vt, vt, bias_t)


def _prompt_bias_t():
    r = jnp.arange(WINDOW, dtype=jnp.int32)[None, :]
    c = jnp.arange(WINDOW, dtype=jnp.int32)[:, None]
    from_prev = c > r
    dist = r - c + jnp.where(from_prev, WINDOW, 0)
    slopes = jnp.exp2(-8.0 * jnp.arange(1, N_HEADS + 1, dtype=F32) / N_HEADS)
    later = -slopes[:, None, None] * dist.astype(F32)[None]
    first = jnp.where(from_prev[None], -jnp.inf, later)
    return jnp.stack([first, later])


def _sample_kernel(x_ref, g_ref, win_ref, qg_ref, kg_ref, bd_ref, wp_ref, ps_ref,
                   st_ref, ck_ref, cv_ref, sink_ref, bias_ref, perm_ref,
                   pool_ref, attn_ref, pst_ref, kc_ref, vc_ref, *, ns, pos0):
    u, qn, kn, v = _project(x_ref[...], g_ref[...], win_ref[...], qg_ref[...], kg_ref[...], bd_ref[...])
    pst_ref[:, 0:POOL_STATE - 1, :] = st_ref[:, 1:POOL_STATE, :]
    kc_ref[:, 0:WINDOW - 1, :] = ck_ref[:, 1:WINDOW, :]
    vc_ref[:, 0:WINDOW - 1, :] = cv_ref[:, 1:WINDOW, :]
    for n in range(ns):
        pst_ref[n, POOL_STATE - 1:POOL_STATE, :] = u[n:n + 1, :]
        kc_ref[n, WINDOW - 1:WINDOW, :] = kn[n:n + 1, :]
        vc_ref[n, WINDOW - 1:WINDOW, :] = v[n:n + 1, :]

    d_groups = []
    for gi, w in enumerate(POOL_WINDOWS):
        lo = gi * POOL_GC
        acc = u[:, lo:lo + POOL_GC]
        for back in range(1, w):
            acc = acc + st_ref[:, POOL_STATE - back, lo:lo + POOL_GC]
        d_groups.append(acc / float(min(pos0 + 1, w)) - u[:, lo:lo + POOL_GC])
    pool_ref[...] = _pool_project(d_groups, wp_ref, ps_ref[...]).astype(BF16)

    zeros = jnp.zeros((ns, HEAD_DIM), F32)
    stacked = []
    for h in range(N_HEADS):
        piece = qn[:, h * HEAD_DIM:(h + 1) * HEAD_DIM]
        pair = [piece, zeros] if h < GQA_GROUP else [zeros, piece]
        stacked.append(jnp.concatenate(pair, axis=-1))
    q_hn = jnp.concatenate(stacked, axis=0).astype(BF16)
    q_nh = jnp.dot(perm_ref[0], q_hn, preferred_element_type=F32).astype(BF16)

    keys = kc_ref[...].reshape(ns * WINDOW, KV_W).astype(BF16)
    vals = vc_ref[...].reshape(ns * WINDOW, KV_W).astype(BF16)
    s_all = lax.dot_general(q_nh, keys, (((1,), (1,)), ((), ())), preferred_element_type=F32)
    sink = sink_ref[...]
    bias = bias_ref[...]
    zero_blk = jnp.zeros((N_HEADS, WINDOW), F32)
    p_rows = []
    for n in range(ns):
        s = s_all[n * N_HEADS:(n + 1) * N_HEADS, n * WINDOW:(n + 1) * WINDOW] + bias
        m = jnp.maximum(jnp.max(s, axis=-1, keepdims=True), sink)
        p = jnp.exp(s - m)
        denom = jnp.sum(p, axis=-1, keepdims=True) + jnp.exp(sink - m)
        p_rows.append(jnp.concatenate([zero_blk] * n + [p / denom] + [zero_blk] * (ns - 1 - n), axis=-1))
    p_blockdiag = jnp.concatenate(p_rows, axis=0).astype(BF16)
    o_nh = jnp.dot(p_blockdiag, vals, preferred_element_type=F32).astype(BF16)
    o_hn = jnp.dot(perm_ref[1], o_nh, preferred_element_type=F32)
    pieces = []
    for h in range(N_HEADS):
        kv = h // GQA_GROUP
        pieces.append(o_hn[h * ns:(h + 1) * ns, kv * HEAD_DIM:(kv + 1) * HEAD_DIM])
    attn_ref[...] = jnp.concatenate(pieces, axis=-1).astype(BF16)


def _sample_mixer(l, depth, xs, g_attn, w_in, qg, kg, bd, wp, ps, state, ck, cv, sink8, bias_s, pos0):
    n = xs.shape[0]
    ns = 32
    row = lambda i: (i, 0)
    lay = lambda i: (l, 0, 0)
    src = jnp.arange(ns * N_HEADS)
    perm = (((src % N_HEADS) * ns + src // N_HEADS)[:, None] == src[None, :]).astype(BF16)
    perms = jnp.stack([perm, perm.T])
    return pl.pallas_call(
        functools.partial(_sample_kernel, ns=ns, pos0=pos0),
        grid=(n // ns,),
        input_output_aliases={8: 2, 9: 3, 10: 4},
        in_specs=[
            pl.BlockSpec((ns, D_MODEL), row),
            pl.BlockSpec((None, 1, D_MODEL), lay),
            pl.BlockSpec((None, D_MODEL, D_IN), lay),
            pl.BlockSpec((None, 1, Q_W), lay),
            pl.BlockSpec((None, 1, KV_W), lay),
            pl.BlockSpec((256, 256), lambda i: (0, 0)),
            pl.BlockSpec((None, 2, 256, 256), lambda i: (l, 0, 0, 0)),
            pl.BlockSpec((None, 1, POOL_W), lay),
            pl.BlockSpec((None, ns, POOL_STATE, POOL_W), lambda i: (l, i, 0, 0)),
            pl.BlockSpec((None, ns, WINDOW, KV_W), lambda i: (l, i, 0, 0)),
            pl.BlockSpec((None, ns, WINDOW, KV_W), lambda i: (l, i, 0, 0)),
            pl.BlockSpec((N_HEADS, 1), lambda i: (0, 0)),
            pl.BlockSpec((N_HEADS, WINDOW), lambda i: (0, 0)),
            pl.BlockSpec((2, ns * N_HEADS, ns * N_HEADS), lambda i: (0, 0, 0)),
        ],
        out_specs=[
            pl.BlockSpec((ns, POOL_W), row),
            pl.BlockSpec((ns, Q_W), row),
            pl.BlockSpec((None, ns, POOL_STATE, POOL_W), lambda i: (l, i, 0, 0)),
            pl.BlockSpec((None, ns, WINDOW, KV_W), lambda i: (l, i, 0, 0)),
            pl.BlockSpec((None, ns, WINDOW, KV_W), lambda i: (l, i, 0, 0)),
        ],
        out_shape=[
            jax.ShapeDtypeStruct((n, POOL_W), BF16),
            jax.ShapeDtypeStruct((n, Q_W), BF16),
            jax.ShapeDtypeStruct((depth, n, POOL_STATE, POOL_W), F32),
            jax.ShapeDtypeStruct((depth, n, WINDOW, KV_W), F32),
            jax.ShapeDtypeStruct((depth, n, WINDOW, KV_W), F32),
        ],
        compiler_params=pltpu.CompilerParams(
            dimension_semantics=("arbitrary",), vmem_limit_bytes=VMEM_LIMIT),
        name="sample_mixer",
    )(xs, g_attn, w_in, qg, kg, bd, wp, ps, state, ck, cv, sink8, bias_s, perms)


def _merge_router_kernel(pool_ref, attn_ref, x_ref, wout_ref, g_ref, wr_ref, br_ref, utri_ref, cin_ref,
                         x1_ref, h2_ref, route_t_ref, cnt_ref, y_ref, lg_ref):
    i = pl.program_id(0)

    @pl.when(i == 0)
    def _():
        cnt_ref[...] = cin_ref[...]

    tm = x_ref.shape[0]
    rc = tm // MERGE_CHUNKS
    chunks = [slice(ci * rc, (ci + 1) * rc) for ci in range(MERGE_CHUNKS)]
    for rows in chunks:
        y_ref[rows, :] = (jnp.dot(pool_ref[rows, :], wout_ref[0:POOL_W, :], preferred_element_type=F32)
                          + jnp.dot(attn_ref[rows, :], wout_ref[POOL_W:, :], preferred_element_type=F32))
    for rows in chunks:
        x1 = x_ref[rows, :] + y_ref[rows, :]
        x1_ref[rows, :] = x1
        h2 = _rms_bf16(x1, g_ref[...])
        h2_ref[rows, :] = _pack_bf16_pairs(h2)
        lg_ref[rows, :] = jnp.dot(h2, wr_ref[...], preferred_element_type=F32) + br_ref[...]
    logits = lg_ref[...]

    lt = jnp.transpose(logits)
    sub = lax.broadcasted_iota(jnp.int32, (EXPERTS_PER_GROUP, tm), 0)
    neg = -jnp.inf
    big = jnp.int32(EXPERTS_PER_GROUP)
    gl = jnp.where(sub < N_EXPERT_GROUPS, lt[GROUP_LANE0:GROUP_LANE0 + EXPERTS_PER_GROUP, :], neg)
    gmax = jnp.max(gl, axis=0, keepdims=True)
    grp = jnp.min(jnp.where(gl == gmax, sub, big), axis=0, keepdims=True)
    g_w = 1.0 / jnp.sum(jnp.exp(gl - gmax), axis=0, keepdims=True)
    el = lt[(N_EXPERT_GROUPS - 1) * EXPERTS_PER_GROUP:N_EXPERT_GROUPS * EXPERTS_PER_GROUP, :]
    for gi in range(N_EXPERT_GROUPS - 2, -1, -1):
        el = jnp.where(grp == gi, lt[gi * EXPERTS_PER_GROUP:(gi + 1) * EXPERTS_PER_GROUP, :], el)
    v1 = jnp.max(el, axis=0, keepdims=True)
    i1 = jnp.min(jnp.where(el == v1, sub, big), axis=0, keepdims=True)
    el2 = jnp.where(sub == i1, neg, el)
    v2 = jnp.max(el2, axis=0, keepdims=True)
    i2 = jnp.min(jnp.where(el2 == v2, sub, big), axis=0, keepdims=True)
    e21 = jnp.exp(v2 - v1)
    w1 = g_w / (1.0 + e21)
    w2 = g_w * e21 / (1.0 + e21)
    e1 = grp * EXPERTS_PER_GROUP + i1
    e2 = grp * EXPERTS_PER_GROUP + i2

    esub = lax.broadcasted_iota(jnp.int32, (N_EXPERTS, tm), 0)
    oh1 = esub == e1
    oh2 = esub == e2
    c = jnp.where(oh1 | oh2, 1.0, 0.0)
    prefix = jnp.dot(c.astype(BF16), utri_ref[...], preferred_element_type=F32) + cnt_ref[...]
    r1 = jnp.sum(jnp.where(oh1, prefix, 0.0), axis=0, keepdims=True)
    r2 = jnp.sum(jnp.where(oh2, prefix, 0.0), axis=0, keepdims=True)
    cnt_ref[...] = cnt_ref[...] + jnp.sum(c, axis=1, keepdims=True)

    fields = jnp.zeros((ROUTE_FIELDS, tm), F32)
    for idx, val in enumerate((e1.astype(F32), e2.astype(F32), w1, w2, r1, r2)):
        fields = jnp.where(sub == idx, val, fields)
    route_t_ref[...] = fields


def _merge_router(l, pool, attn, x2d, w_out, g_ffn, wr, br, cnt_in, tm):
    t = x2d.shape[0]
    utri = (jnp.arange(tm)[:, None] < jnp.arange(tm)[None, :]).astype(BF16)
    row = lambda i: (i, 0)
    lay = lambda i: (l, 0, 0)
    return pl.pallas_call(
        _merge_router_kernel,
        grid=(t // tm,),
        in_specs=[
            pl.BlockSpec((tm, POOL_W), row),
            pl.BlockSpec((tm, Q_W), row),
            pl.BlockSpec((tm, D_MODEL), row),
            pl.BlockSpec((None, D_MODEL, D_MODEL), lay),
            pl.BlockSpec((None, 1, D_MODEL), lay),
            pl.BlockSpec((None, D_MODEL, LANES), lay),
            pl.BlockSpec((None, 1, LANES), lay),
            pl.BlockSpec((tm, tm), lambda i: (0, 0)),
            pl.BlockSpec((N_EXPERTS, 1), lambda i: (0, 0)),
        ],
        out_specs=[
            pl.BlockSpec((tm, D_MODEL), row),
            pl.BlockSpec((tm, D_MODEL // 2), row),
            pl.BlockSpec((ROUTE_FIELDS, tm), lambda i: (0, i)),
            pl.BlockSpec((N_EXPERTS, 1), lambda i: (0, 0)),
        ],
        out_shape=[
            jax.ShapeDtypeStruct((t, D_MODEL), F32),
            jax.ShapeDtypeStruct((t, D_MODEL // 2), jnp.int32),
            jax.ShapeDtypeStruct((ROUTE_FIELDS, t), F32),
            jax.ShapeDtypeStruct((N_EXPERTS, 1), F32),
        ],
        scratch_shapes=[pltpu.VMEM((tm, D_MODEL), F32), pltpu.VMEM((tm, LANES), F32)],
        compiler_params=pltpu.CompilerParams(
            dimension_semantics=("arbitrary",), vmem_limit_bytes=VMEM_LIMIT),
        name="merge_router",
    )(pool, attn, x2d, w_out, g_ffn, wr, br, utri, cnt_in)


def _moe_kernel(be_ref, rv_ref, nx_ref, sl_ref, first_ref, xd_ref, wg_hbm, wu_hbm, wd_hbm, yd_ref,
                wg_f, wu_f, wd_f, wg_s, wu_s, wd_s, sem, *, layer):
    step = pl.program_id(0)

    def weight_copies(e, s):
        return [pltpu.make_async_copy(w_hbm.at[layer, e], w_f.at[s], sem.at[s, n])
                for n, (w_hbm, w_f) in enumerate(((wg_hbm, wg_f), (wu_hbm, wu_f), (wd_hbm, wd_f)))]

    @pl.when(step == 0)
    def _():
        for s in range(WEIGHT_SLOTS - 1):
            @pl.when(first_ref[s] >= 0)
            def _():
                for c in weight_copies(first_ref[s], s):
                    c.start()

    def enter_expert(i):
        expert, slot = be_ref[i], sl_ref[i]

        @pl.when((i == 0) | (expert != be_ref[jnp.maximum(i - 1, 0)]))
        def _():
            for c in weight_copies(expert, slot):
                c.wait()

            @pl.when(nx_ref[i] >= 0)
            def _():
                for c in weight_copies(nx_ref[i], lax.rem(slot + WEIGHT_SLOTS - 1, WEIGHT_SLOTS)):
                    c.start(priority=1)

            wg_s[...] = wg_f[slot].astype(BF16)
            wu_s[...] = wu_f[slot].astype(BF16)
            wd_s[...] = wd_f[slot].astype(BF16)

    def experts_on(row0, n_rows, rows_valid):
        rows = pl.ds(row0, n_rows)
        row = lax.broadcasted_iota(jnp.int32, (n_rows, D_MODEL // 2), 0)
        x = _unpack_bf16_pairs(jnp.where(row < rows_valid, xd_ref[rows, :], 0)).astype(BF16)
        gate = jnp.dot(x, wg_s[...], preferred_element_type=F32)
        up = jnp.dot(x, wu_s[...], preferred_element_type=F32)
        act = (gate * jax.nn.sigmoid(gate) * up).astype(BF16)
        y = jnp.dot(act, wd_s[...], preferred_element_type=F32)
        yd_ref[rows, :] = _pack_bf16_pairs(y.astype(BF16))

    def single_block(i, row0):
        enter_expert(i)

        @pl.when(rv_ref[i] > 0)
        def _():
            experts_on(row0, MOE_BM, rv_ref[i])

        @pl.when(rv_ref[i] <= 0)
        def _():
            yd_ref[pl.ds(row0, MOE_BM), :] = jnp.zeros((MOE_BM, D_MODEL // 2), jnp.int32)

    @pl.when(rv_ref[step * MOE_STEP_BLOCKS] > 0)
    def _():
        for pair in range(MOE_STEP_BLOCKS // 2):
            ia = step * MOE_STEP_BLOCKS + 2 * pair
            ib = ia + 1
            row0 = 2 * pair * MOE_BM
            same = (be_ref[ib] == be_ref[ia]) & (rv_ref[ib] > 0)

            @pl.when(same)
            def _():
                enter_expert(ia)
                experts_on(row0, 2 * MOE_BM, MOE_BM + rv_ref[ib])

            @pl.when(jnp.logical_not(same))
            def _():
                single_block(ia, row0)
                single_block(ib, row0 + MOE_BM)


def _moe_experts(l, block_e, rows_valid, next_e, slot, first_e, xd, w_gate, w_up, w_down):
    n_blocks = xd.shape[0] // MOE_BM
    step_rows = MOE_STEP_BLOCKS * MOE_BM
    row = lambda i, be, rv, nx, sl, fe: (jnp.minimum(i, fe[WEIGHT_SLOTS - 1] - 1), 0)
    return pl.pallas_call(
        functools.partial(_moe_kernel, layer=l),
        grid_spec=pltpu.PrefetchScalarGridSpec(
            num_scalar_prefetch=5,
            grid=(n_blocks // MOE_STEP_BLOCKS,),
            in_specs=[
                pl.BlockSpec((step_rows, D_MODEL // 2), row),
                pl.BlockSpec(memory_space=pl.ANY),
                pl.BlockSpec(memory_space=pl.ANY),
                pl.BlockSpec(memory_space=pl.ANY),
            ],
            out_specs=pl.BlockSpec((step_rows, D_MODEL // 2), row),
            scratch_shapes=[
                pltpu.VMEM((WEIGHT_SLOTS, D_MODEL, EXPERT_FF), F32),
                pltpu.VMEM((WEIGHT_SLOTS, D_MODEL, EXPERT_FF), F32),
                pltpu.VMEM((WEIGHT_SLOTS, EXPERT_FF, D_MODEL), F32),
                pltpu.VMEM((D_MODEL, EXPERT_FF), BF16),
                pltpu.VMEM((D_MODEL, EXPERT_FF), BF16),
                pltpu.VMEM((EXPERT_FF, D_MODEL), BF16),
                pltpu.SemaphoreType.DMA((WEIGHT_SLOTS, 3)),
            ],
        ),
        out_shape=jax.ShapeDtypeStruct((n_blocks * MOE_BM, D_MODEL // 2), jnp.int32),
        compiler_params=pltpu.CompilerParams(
            dimension_semantics=("arbitrary",), vmem_limit_bytes=VMEM_LIMIT),
        name="moe_experts",
    )(block_e, rows_valid, next_e, slot, first_e, xd, w_gate, w_up, w_down)


def _sc_worker_id():
    return lax.axis_index("s") * SC_CORES + lax.axis_index("c")


def _sc_dispatch(hp, hs, dest_p, dest_s, n_rows):
    tp, width = hp.shape
    per_w = tp // SC_WORKERS
    n_ch = per_w // DISP_CH
    n_sw = hs.shape[0] // SAMPLE_CH
    mesh = plsc.VectorSubcoreMesh(core_axis_name="c", subcore_axis_name="s")

    @functools.partial(
        pl.kernel, mesh=mesh,
        out_type=jax.ShapeDtypeStruct((n_rows, width), jnp.int32),
        scratch_types=[
            pltpu.VMEM((2, n_ch, DISP_CH), jnp.int32),
            pltpu.VMEM((2, 1, SAMPLE_CH), jnp.int32),
            pltpu.VMEM((SC_RING, DISP_CH, width), jnp.int32),
            pltpu.SemaphoreType.DMA((SC_RING,)),
            pltpu.SemaphoreType.DMA((SC_RING, 2)),
        ],
        name="sc_dispatch",
    )
    def k(hp_hbm, hs_hbm, dp_hbm, ds_hbm, xd_hbm, idx_v, idxs_v, bufs, rsem, wsem):
        wid = _sc_worker_id()
        base = wid * per_w
        for kk in range(2):
            pltpu.sync_copy(dp_hbm.at[kk, wid], idx_v.at[kk])
        reads = [pltpu.make_async_copy(hp_hbm.at[pl.ds(base + j * DISP_CH, DISP_CH)],
                                       bufs.at[j % SC_RING], rsem.at[j % SC_RING]) for j in range(n_ch)]
        writes = [[pltpu.make_async_copy(bufs.at[j % SC_RING], xd_hbm.at[idx_v.at[kk, j]],
                                         wsem.at[j % SC_RING, kk]) for kk in range(2)] for j in range(n_ch)]
        for j in range(min(SC_RING - 1, n_ch)):
            reads[j].start()
        for j in range(n_ch):
            reads[j].wait()
            for w in writes[j]:
                w.start()
            if j >= 1:
                for w in writes[j - 1]:
                    w.wait()
            if j + SC_RING - 1 < n_ch:
                reads[j + SC_RING - 1].start()
        for w in writes[n_ch - 1]:
            w.wait()

        @pl.when(wid < n_sw)
        def _():
            rows = bufs.at[0, pl.ds(0, SAMPLE_CH)]
            for kk in range(2):
                pltpu.sync_copy(ds_hbm.at[kk, wid], idxs_v.at[kk])
            pltpu.sync_copy(hs_hbm.at[pl.ds(wid * SAMPLE_CH, SAMPLE_CH)], rows)
            for kk in range(2):
                pltpu.sync_copy(rows, xd_hbm.at[idxs_v.at[kk, 0]])

    return k(hp, hs, dest_p, dest_s)


def _sc_combine_gather(yd, dest_p, dest_s, tp, ts):
    width = yd.shape[1]
    per_w = tp // SC_WORKERS
    n_ch = per_w // COMB_CH
    n_sw = ts // SAMPLE_CH
    mesh = plsc.VectorSubcoreMesh(core_axis_name="c", subcore_axis_name="s")

    @functools.partial(
        pl.kernel, mesh=mesh,
        out_type=jax.ShapeDtypeStruct((2, tp + ts, width), yd.dtype),
        scratch_types=[
            pltpu.VMEM((2, n_ch, COMB_CH), jnp.int32),
            pltpu.VMEM((2, 1, SAMPLE_CH), jnp.int32),
            pltpu.VMEM((SC_RING, COMB_CH, width), yd.dtype),
            pltpu.SemaphoreType.DMA((SC_RING,)),
            pltpu.SemaphoreType.DMA((SC_RING,)),
        ],
        name="sc_combine_gather",
    )
    def k(yd_hbm, dp_hbm, ds_hbm, g_hbm, idx_v, idxs_v, bufs, gsem, wsem):
        wid = _sc_worker_id()
        base = wid * per_w
        for kk in range(2):
            pltpu.sync_copy(dp_hbm.at[kk, wid], idx_v.at[kk])
        items = [(kk, j) for kk in range(2) for j in range(n_ch)]
        n_items = len(items)
        gathers = [pltpu.make_async_copy(yd_hbm.at[idx_v.at[kk, j]], bufs.at[n % SC_RING], gsem.at[n % SC_RING])
                   for n, (kk, j) in enumerate(items)]
        outs = [pltpu.make_async_copy(bufs.at[n % SC_RING], g_hbm.at[kk, pl.ds(base + j * COMB_CH, COMB_CH)],
                                      wsem.at[n % SC_RING]) for n, (kk, j) in enumerate(items)]
        for n in range(min(SC_RING - 1, n_items)):
            gathers[n].start()
        for n in range(n_items):
            gathers[n].wait()
            outs[n].start()
            if n >= 1:
                outs[n - 1].wait()
            if n + SC_RING - 1 < n_items:
                gathers[n + SC_RING - 1].start()
        outs[n_items - 1].wait()

        @pl.when(wid < n_sw)
        def _():
            for kk in range(2):
                pltpu.sync_copy(ds_hbm.at[kk, wid], idxs_v.at[kk])
            for kk in range(2):
                rows = bufs.at[kk, pl.ds(0, SAMPLE_CH)]
                pltpu.sync_copy(yd_hbm.at[idxs_v.at[kk, 0]], rows)
                pltpu.sync_copy(rows, g_hbm.at[kk, pl.ds(tp + wid * SAMPLE_CH, SAMPLE_CH)])

    return k(yd, dest_p, dest_s)


def _combine_kernel(x1_ref, g_ref, route_t_ref, x2_ref):
    fields = route_t_ref[...]
    tm = fields.shape[1]
    cols = jnp.transpose(jnp.concatenate([fields, jnp.zeros((LANES - ROUTE_FIELDS, tm), F32)], axis=0))
    w1 = cols[:, 2:3]
    w2 = cols[:, 3:4]
    x2_ref[...] = x1_ref[...] + _unpack_bf16_pairs(g_ref[0]) * w1 + _unpack_bf16_pairs(g_ref[1]) * w2


def _combine(x1, g, route_t, row0, tm):
    t = x1.shape[0]
    blk0 = row0 // tm
    row = lambda i: (i, 0)
    return pl.pallas_call(
        _combine_kernel,
        grid=(t // tm,),
        in_specs=[
            pl.BlockSpec((tm, D_MODEL), row),
            pl.BlockSpec((2, tm, D_MODEL // 2), lambda i: (0, blk0 + i, 0)),
            pl.BlockSpec((ROUTE_FIELDS, tm), lambda i: (0, i)),
        ],
        out_specs=pl.BlockSpec((tm, D_MODEL), row),
        out_shape=jax.ShapeDtypeStruct((t, D_MODEL), F32),
        compiler_params=pltpu.CompilerParams(
            dimension_semantics=("arbitrary",), vmem_limit_bytes=VMEM_LIMIT),
        name="combine",
    )(x1, g, route_t)


def _dest_layout(dest, workers, chunk):
    t = dest.shape[1]
    return dest.reshape(2, workers, t // (workers * chunk), chunk)


def _hier_moe(l, h2p, h2s, route_tp, route_ts, counts, w_gate, w_up, w_down):
    tp, ts = h2p.shape[0], h2s.shape[0]
    n_assign = 2 * (tp + ts)
    n_blocks = -(-n_assign // MOE_BM) + N_EXPERTS
    n_blocks = -(-n_blocks // MOE_STEP_BLOCKS) * MOE_STEP_BLOCKS
    pcounts = (counts + MOE_BM - 1) // MOE_BM * MOE_BM
    pends = jnp.cumsum(pcounts)
    poffsets = pends - pcounts
    starts = jnp.arange(n_blocks, dtype=jnp.int32) * MOE_BM
    block_e = jnp.minimum(jnp.sum((pends[None, :] <= starts[:, None]).astype(jnp.int32), axis=1),
                          N_EXPERTS - 1)
    experts = jnp.arange(N_EXPERTS, dtype=jnp.int32)

    def lookup(table, idx):
        return jnp.sum(jnp.where(idx[..., None] == experts, table, 0), axis=-1)

    rows_valid = jnp.clip(lookup(poffsets + counts, block_e) - starts, 0, MOE_BM).astype(jnp.int32)
    used = counts > 0
    last_e = jnp.max(jnp.where(used, jnp.arange(N_EXPERTS, dtype=jnp.int32), 0))
    block_e = jnp.where(rows_valid > 0, block_e, last_e).astype(jnp.int32)
    place = jnp.cumsum(used.astype(jnp.int32)) - 1
    by_place = jnp.sum(jnp.where(used[None, :] & (place[None, :] == experts[:, None]), experts[None, :], 0),
                       axis=1)
    n_used = jnp.sum(used.astype(jnp.int32))

    def at_place(p):
        return jnp.where(p < n_used, lookup(by_place, jnp.minimum(p, N_EXPERTS - 1)), -1).astype(jnp.int32)

    ahead_of = at_place(place + (WEIGHT_SLOTS - 1))
    next_e = lookup(ahead_of, block_e)
    slot = lookup(place % WEIGHT_SLOTS, block_e)
    n_steps_used = -(-(pends[-1] // MOE_BM) // MOE_STEP_BLOCKS)
    first_e = jnp.concatenate([at_place(jnp.arange(WEIGHT_SLOTS - 1, dtype=jnp.int32)),
                               n_steps_used.reshape(1).astype(jnp.int32)])

    def dest_of(route_t):
        return lookup(poffsets, route_t[0:2].astype(jnp.int32)) + route_t[4:6].astype(jnp.int32)

    dest_p, dest_s = dest_of(route_tp), dest_of(route_ts)
    n_sw = ts // SAMPLE_CH
    xd = _sc_dispatch(h2p, h2s, _dest_layout(dest_p, SC_WORKERS, DISP_CH),
                      _dest_layout(dest_s, n_sw, SAMPLE_CH), n_blocks * MOE_BM)
    yd = _moe_experts(l, block_e, rows_valid, next_e, slot, first_e, xd, w_gate, w_up, w_down)
    return _sc_combine_gather(yd, _dest_layout(dest_p, SC_WORKERS, COMB_CH),
                              _dest_layout(dest_s, n_sw, SAMPLE_CH), tp, ts)


def kernel(x_prompt, x_sample, state_pool, cache_k_win, cache_v_win, norm_attn_g, w_in, pool_w, pool_scale, q_norm_g, k_norm_g, attn_sinks, w_out, norm_ffn_g, router_group_w, router_group_b, router_expert_w, router_expert_b, w_gate, w_up, w_down):
    n_p, t_p, d = x_prompt.shape
    n_s, t_s, _ = x_sample.shape
    depth = w_in.shape[0]
    lw_s = cache_k_win.shape[2]
    assert t_s == 1 and lw_s == WINDOW and d == D_MODEL
    assert t_p % TM_PROJ == 0 and t_p >= WINDOW

    seg = jnp.arange(256) // HEAD_DIM
    bd = jnp.where(seg[:, None] == seg[None, :], 1.0 / HEAD_DIM, 0.0).astype(BF16)
    slopes = jnp.exp2(-8.0 * jnp.arange(1, N_HEADS + 1, dtype=F32) / N_HEADS)
    bias_p = _prompt_bias_t()
    dist_s = (WINDOW - 1) - jnp.arange(WINDOW, dtype=F32)
    bias_s = -slopes[:, None] * dist_s[None, :]

    wp = jnp.zeros((depth, 2, 256, 256), F32)
    for p in range(2):
        wp = wp.at[:, p, :POOL_GC, :POOL_GC].set(pool_w[:, 2 * p])
        wp = wp.at[:, p, POOL_GC:, POOL_GC:].set(pool_w[:, 2 * p + 1])
    assert GROUP_LANE0 == N_EXPERTS
    lane_pad = LANES - N_EXPERTS - N_EXPERT_GROUPS
    wr = jnp.concatenate([router_expert_w, router_group_w, jnp.zeros((depth, D_MODEL, lane_pad), F32)], axis=-1)
    br = jnp.concatenate([router_expert_b, router_group_b, jnp.zeros((depth, lane_pad), F32)],
                         axis=-1).reshape(depth, 1, LANES)
    lp = dict(
        w_in=w_in.astype(BF16),
        w_out=w_out.astype(BF16),
        g_attn=norm_attn_g.reshape(depth, 1, D_MODEL),
        g_ffn=norm_ffn_g.reshape(depth, 1, D_MODEL),
        qg=(jnp.tile(q_norm_g, (1, N_HEADS)) * ATTN_SCALE).reshape(depth, 1, Q_W),
        kg=jnp.tile(k_norm_g, (1, N_KV_HEADS)).reshape(depth, 1, KV_W),
        wp=wp.astype(BF16),
        ps=pool_scale.reshape(depth, 1, POOL_W),
        wr=wr.astype(BF16),
        br=br,
        state=state_pool,
        ck=cache_k_win.reshape(depth, n_s, lw_s, KV_W),
        cv=cache_v_win.reshape(depth, n_s, lw_s, KV_W),
    )

    xp = x_prompt.reshape(n_p * t_p, D_MODEL)
    xs = x_sample.reshape(n_s, D_MODEL)
    lw_p = min(WINDOW, t_p)
    pool_p, kp_new, vp_new = [], [], []
    sample_state = [lp["state"], lp["ck"], lp["cv"]]
    zero_cnt = jnp.zeros((N_EXPERTS, 1), F32)
    pending = None
    for l in range(depth):
        sinks = attn_sinks[l]
        outs = _proj_pool_prompt(
            l, xp if pending is None else pending, n_p, t_p,
            lp["g_attn"], lp["w_in"], lp["qg"], lp["kg"], bd, lp["wp"], lp["ps"])
        if pending is not None:
            xp, outs = outs[0], outs[1:]
        pool_o, q, k, vt, utail, ktail, vtail = outs
        attn_o = _attn_prompt(q, k, vt, bias_p, sinks, n_p, t_p)
        x1p, h2p, route_tp, cnt_p = _merge_router(
            l, pool_o, attn_o, xp, lp["w_out"], lp["g_ffn"], lp["wr"], lp["br"], zero_cnt, TM_MERGE)
        pool_p.append(utail[:, 16 - POOL_STATE:, :])
        kp_new.append(ktail)
        vp_new.append(vtail)
        pool_so, attn_so, *sample_state = _sample_mixer(
            l, depth, xs, lp["g_attn"], lp["w_in"], lp["qg"], lp["kg"], bd, lp["wp"], lp["ps"],
            *sample_state, sinks.reshape(N_HEADS, 1), bias_s, PAST_LEN)
        x1s, h2s, route_ts, cnt_all = _merge_router(
            l, pool_so, attn_so, xs, lp["w_out"], lp["g_ffn"], lp["wr"], lp["br"], cnt_p, n_s)
        counts = cnt_all[:, 0].astype(jnp.int32)
        g = _hier_moe(l, h2p, h2s, route_tp, route_ts, counts, w_gate, w_up, w_down)
        xs = _combine(x1s, g, route_ts, n_p * t_p, n_s)
        pending = (x1p, g, route_tp)
    xp = _combine(*pending, 0, TM_MERGE)
    return (xp.reshape(n_p, t_p, D_MODEL), xs.reshape(n_s, t_s, D_MODEL),
            jnp.stack(pool_p),
            jnp.stack(kp_new).reshape(depth, n_p, lw_p, N_KV_HEADS, HEAD_DIM),
            jnp.stack(vp_new).reshape(depth, n_p, lw_p, N_KV_HEADS, HEAD_DIM),
            sample_state[0],
            sample_state[1].reshape(depth, n_s, lw_s, N_KV_HEADS, HEAD_DIM),
            sample_state[2].reshape(depth, n_s, lw_s, N_KV_HEADS, HEAD_DIM))
```

```python
import functools

import jax
import jax.numpy as jnp
from jax import lax
from jax.experimental import pallas as pl
from jax.experimental.pallas import tpu as pltpu
from jax.experimental.pallas import tpu_sc as plsc

D_MODEL = 1024
POOL_W = 512
POOL_WINDOWS = (2, 4, 8, 16)
POOL_GC = 128
POOL_STATE = 15
HEAD_DIM = 64
N_HEADS = 8
N_KV_HEADS = 2
GQA_GROUP = 4
Q_W = 512
KV_W = 128
D_IN = POOL_W + Q_W + 2 * KV_W
WINDOW = 128
ATTN_SCALE = HEAD_DIM ** -0.5
LOG2E = 1.4426950408889634
N_EXPERT_GROUPS = 4
EXPERTS_PER_GROUP = 8
N_EXPERTS = 32
EXPERT_FF = 512
EPS = 1e-6
PAST_LEN = 16384

LANES = 128
HALO = 32
TM_PROJ = 1024
TM_MERGE = 1024
MERGE_CHUNKS = 4
ATTN_QB = 16
MOE_BM = 256
MOE_STEP_BLOCKS = 4
WEIGHT_SLOTS = 3
GROUP_LANE0 = 32
ROUTE_FIELDS = 8
SC_CORES = 2
SC_SUBCORES = 16
SC_WORKERS = SC_CORES * SC_SUBCORES
DISP_CH = 64
COMB_CH = 64
SAMPLE_CH = 32
SC_RING = 3
VMEM_LIMIT = 48 * 1024 * 1024

BF16 = jnp.bfloat16
F32 = jnp.float32


def _pack_bf16_pairs(h):
    w = h.shape[1] // 2
    hi = lax.bitcast_convert_type(h[:, :w].astype(F32), jnp.uint32)
    lo = lax.bitcast_convert_type(h[:, w:].astype(F32), jnp.uint32)
    return lax.bitcast_convert_type(hi | (lo >> 16), jnp.int32)


def _unpack_bf16_pairs(words):
    u = lax.bitcast_convert_type(words, jnp.uint32)
    hi = lax.bitcast_convert_type(u & jnp.uint32(0xFFFF0000), F32)
    lo = lax.bitcast_convert_type(u << 16, F32)
    return jnp.concatenate([hi, lo], axis=-1)


def _segment_mean_sq(a, bd):
    w = a.shape[1]
    return jnp.dot((a * a).astype(BF16), bd[:w, :w], preferred_element_type=F32)


def _rms_bf16(x, g):
    ms = jnp.mean(x * x, axis=-1, keepdims=True)
    return (x * lax.rsqrt(ms + EPS) * g).astype(BF16)


def _qk_norm(q, k, qg, kg, bd):
    qn = []
    for c in range(Q_W // 256):
        qc = q[:, c * 256:(c + 1) * 256]
        qn.append(qc * lax.rsqrt(_segment_mean_sq(qc, bd) + EPS))
    qn = jnp.concatenate(qn, axis=-1) * qg
    kn = k * lax.rsqrt(_segment_mean_sq(k, bd) + EPS) * kg
    return qn, kn


def _project(x, g, w_in, qg, kg, bd):
    z = jnp.dot(_rms_bf16(x, g), w_in, preferred_element_type=F32)
    u = z[:, :POOL_W]
    q = z[:, POOL_W:POOL_W + Q_W]
    k = z[:, POOL_W + Q_W:POOL_W + Q_W + KV_W]
    v = z[:, POOL_W + Q_W + KV_W:]
    qn, kn = _qk_norm(q, k, qg, kg, bd)
    return u, qn, kn, v


def _pool_project(d_groups, wp_ref, ps):
    outs = []
    for p in range(2):
        dp = jnp.concatenate([d_groups[2 * p], d_groups[2 * p + 1]], axis=-1).astype(BF16)
        y = jnp.dot(dp, wp_ref[p], preferred_element_type=F32)
        outs.append(y * ps[:, p * 256:(p + 1) * 256])
    return jnp.concatenate(outs, axis=-1)


def _proj_pool_kernel(x_ref, g_ref, win_ref, qg_ref, kg_ref, bd_ref, wp_ref, ps_ref,
                      pool_ref, q_ref, k_ref, vt_ref, utail_ref, ktail_ref, vtail_ref,
                      ext_ref, sa_ref, sb_ref, zq_ref, *, tm, n_j):
    j = pl.program_id(1)

    @pl.when(j == 0)
    def _():
        ext_ref[0:HALO, :] = jnp.zeros((HALO, POOL_W), F32)

    r = tm + HALO
    h = _rms_bf16(x_ref[...], g_ref[...])
    ext_ref[HALO:r, :] = jnp.dot(h, win_ref[:, 0:POOL_W], preferred_element_type=F32)
    zq_ref[...] = jnp.dot(h, win_ref[:, POOL_W:], preferred_element_type=F32)
    u = ext_ref[HALO:r, :]
    sa_ref[8:r, :] = ext_ref[8:r, :] + ext_ref[7:r - 1, :]
    sb_ref[16:r, 128:] = sa_ref[16:r, 128:] + sa_ref[14:r - 2, 128:]
    sa_ref[24:r, 256:] = sb_ref[24:r, 256:] + sb_ref[20:r - 4, 256:]
    sb_ref[32:r, 384:] = sa_ref[32:r, 384:] + sa_ref[24:r - 8, 384:]
    pos1 = j * tm + lax.broadcasted_iota(jnp.int32, (tm, POOL_GC), 0) + 1
    sums = (sa_ref, sb_ref, sa_ref, sb_ref)
    d_groups = []
    for gi, w in enumerate(POOL_WINDOWS):
        sl = slice(gi * POOL_GC, (gi + 1) * POOL_GC)
        cnt = jnp.minimum(pos1, w).astype(F32)
        d_groups.append(sums[gi][HALO:r, sl] / cnt - u[:, sl])
    pool_ref[...] = _pool_project(d_groups, wp_ref, ps_ref[...]).astype(BF16)
    ext_ref[16:HALO, :] = ext_ref[tm + 16:r, :]

    qn, kn = _qk_norm(zq_ref[:, 0:Q_W], zq_ref[:, Q_W:Q_W + KV_W], qg_ref[...], kg_ref[...], bd_ref[...])
    v = zq_ref[:, Q_W + KV_W:]
    q_ref[...] = qn.astype(BF16)
    k_ref[...] = kn.astype(BF16)
    vt_ref[...] = jnp.transpose(v).astype(BF16)

    @pl.when(j == n_j - 1)
    def _():
        utail_ref[...] = u[tm - 16:, :]
        ktail_ref[...] = kn[tm - WINDOW:, :]
        vtail_ref[...] = v[tm - WINDOW:, :]


def _proj_pool_combine_kernel(x1_ref, gath_ref, route_ref, *rest, tm, n_j):
    x2_ref = rest[7]
    _combine_kernel(x1_ref, gath_ref, route_ref, x2_ref)
    _proj_pool_kernel(x2_ref, *rest[:7], *rest[8:], tm=tm, n_j=n_j)


def _proj_pool_prompt(l, x_in, n_seq, seq, g_attn, w_in, qg, kg, bd, wp, ps):
    tm = TM_PROJ
    n_j = seq // tm
    t = n_seq * seq
    row = lambda b, j: (b * n_j + j, 0)
    lay = lambda b, j: (l, 0, 0)
    fused = isinstance(x_in, tuple)
    if fused:
        kern = _proj_pool_combine_kernel
        x_args = list(x_in)
        x_specs = [pl.BlockSpec((tm, D_MODEL), row),
                   pl.BlockSpec((2, tm, D_MODEL // 2), lambda b, j: (0, b * n_j + j, 0)),
                   pl.BlockSpec((ROUTE_FIELDS, tm), lambda b, j: (0, b * n_j + j))]
        x_out_specs = [pl.BlockSpec((tm, D_MODEL), row)]
        x_out_shape = [jax.ShapeDtypeStruct((t, D_MODEL), F32)]
    else:
        kern = _proj_pool_kernel
        x_args = [x_in]
        x_specs = [pl.BlockSpec((tm, D_MODEL), row)]
        x_out_specs, x_out_shape = [], []
    return pl.pallas_call(
        functools.partial(kern, tm=tm, n_j=n_j),
        grid=(n_seq, n_j),
        in_specs=x_specs + [
            pl.BlockSpec((None, 1, D_MODEL), lay),
            pl.BlockSpec((None, D_MODEL, D_IN), lay),
            pl.BlockSpec((None, 1, Q_W), lay),
            pl.BlockSpec((None, 1, KV_W), lay),
            pl.BlockSpec((256, 256), lambda b, j: (0, 0)),
            pl.BlockSpec((None, 2, 256, 256), lambda b, j: (l, 0, 0, 0)),
            pl.BlockSpec((None, 1, POOL_W), lay),
        ],
        out_specs=x_out_specs + [
            pl.BlockSpec((tm, POOL_W), row),
            pl.BlockSpec((tm, Q_W), row),
            pl.BlockSpec((tm, KV_W), row),
            pl.BlockSpec((KV_W, tm), lambda b, j: (0, b * n_j + j)),
            pl.BlockSpec((None, 16, POOL_W), lambda b, j: (b, 0, 0)),
            pl.BlockSpec((None, WINDOW, KV_W), lambda b, j: (b, 0, 0)),
            pl.BlockSpec((None, WINDOW, KV_W), lambda b, j: (b, 0, 0)),
        ],
        out_shape=x_out_shape + [
            jax.ShapeDtypeStruct((t, POOL_W), BF16),
            jax.ShapeDtypeStruct((t, Q_W), BF16),
            jax.ShapeDtypeStruct((t, KV_W), BF16),
            jax.ShapeDtypeStruct((KV_W, t), BF16),
            jax.ShapeDtypeStruct((n_seq, 16, POOL_W), F32),
            jax.ShapeDtypeStruct((n_seq, WINDOW, KV_W), F32),
            jax.ShapeDtypeStruct((n_seq, WINDOW, KV_W), F32),
        ],
        scratch_shapes=[pltpu.VMEM((tm + HALO, POOL_W), F32)] * 3 + [pltpu.VMEM((tm, Q_W + 2 * KV_W), F32)],
        compiler_params=pltpu.CompilerParams(
            dimension_semantics=("arbitrary", "arbitrary"), vmem_limit_bytes=VMEM_LIMIT),
        name="proj_pool_prompt",
    )(*x_args, g_attn, w_in, qg, kg, bd, wp, ps)


def _attn_kernel(sink_ref, q_ref, kp_ref, kc_ref, vtp_ref, vtc_ref, bias_ref, o_ref, s_ref):
    j = pl.program_id(1)
    kk_all = jnp.concatenate([kp_ref[...], kc_ref[...]], axis=0)
    vt_all = jnp.concatenate([vtp_ref[...], vtc_ref[...]], axis=1)
    from_prev = (lax.broadcasted_iota(jnp.int32, (WINDOW, WINDOW), 0)
                 > lax.broadcasted_iota(jnp.int32, (WINDOW, WINDOW), 1))
    units = [(blk, kv) for blk in range(ATTN_QB) for kv in range(N_KV_HEADS)]

    def scores(n):
        blk, kv = units[n]
        q = q_ref[blk * WINDOW:(blk + 1) * WINDOW, :]
        kk = kk_all[blk * WINDOW:(blk + 2) * WINDOW, kv * HEAD_DIM:(kv + 1) * HEAD_DIM]
        heads = range(kv * GQA_GROUP, (kv + 1) * GQA_GROUP)
        q_rows = jnp.concatenate([q[:, h * HEAD_DIM:(h + 1) * HEAD_DIM] for h in heads], axis=0)
        s_ref[n % 3] = lax.dot_general(kk, q_rows, (((1,), (1,)), ((), ())), preferred_element_type=F32)

    scores(0)
    scores(1)
    outs = []
    for n, (blk, kv) in enumerate(units):
        if n + 2 < len(units):
            scores(n + 2)
        vt_kv = vt_all[kv * HEAD_DIM:(kv + 1) * HEAD_DIM, blk * WINDOW:(blk + 2) * WINDOW]
        variant = jnp.minimum(j, 1) if blk == 0 else 1
        for g in range(GQA_GROUP):
            h = kv * GQA_GROUP + g
            s = jnp.where(from_prev, s_ref[n % 3, 0:WINDOW, g * WINDOW:(g + 1) * WINDOW],
                          s_ref[n % 3, WINDOW:, g * WINDOW:(g + 1) * WINDOW]) + bias_ref[variant, h]
            sink = sink_ref[h]
            m = jnp.maximum(jnp.max(s, axis=0, keepdims=True), sink)
            p = jnp.exp2(s - m)
            denom = jnp.sum(p, axis=0, keepdims=True) + jnp.exp2(sink - m)
            p_keys = jnp.concatenate([jnp.where(from_prev, p, 0.0), jnp.where(from_prev, 0.0, p)], axis=0)
            o_t = jnp.dot(vt_kv, p_keys.astype(BF16), preferred_element_type=F32)
            outs.append(o_t / denom)
        if kv == N_KV_HEADS - 1:
            o_ref[blk * WINDOW:(blk + 1) * WINDOW, :] = jnp.transpose(jnp.concatenate(outs, axis=0)).astype(BF16)
            outs = []


def _attn_prompt(q, k, vt, bias_t, sinks, n_seq, seq):
    tq = ATTN_QB * WINDOW
    nj = seq // tq
    t = n_seq * seq
    cur = lambda b, j: (b * nj + j, 0)
    prev = lambda b, j: (jnp.maximum((b * nj + j) * ATTN_QB - 1, 0), 0)
    cur_t = lambda b, j: (0, b * nj + j)
    prev_t = lambda b, j: (0, jnp.maximum((b * nj + j) * ATTN_QB - 1, 0))
    return pl.pallas_call(
        _attn_kernel,
        grid=(n_seq, nj),
        in_specs=[
            pl.BlockSpec(memory_space=pltpu.SMEM),
            pl.BlockSpec((tq, Q_W), cur),
            pl.BlockSpec((WINDOW, KV_W), prev),
            pl.BlockSpec((tq, KV_W), cur),
            pl.BlockSpec((KV_W, WINDOW), prev_t),
            pl.BlockSpec((KV_W, tq), cur_t),
            pl.BlockSpec((2, N_HEADS, WINDOW, WINDOW), lambda b, j: (0, 0, 0, 0)),
        ],
        out_specs=pl.BlockSpec((tq, Q_W), cur),
        out_shape=jax.ShapeDtypeStruct((t, Q_W), BF16),
        scratch_shapes=[pltpu.VMEM((3, 2 * WINDOW, GQA_GROUP * WINDOW), F32)],
        compiler_params=pltpu.CompilerParams(
            dimension_semantics=("arbitrary", "arbitrary"), vmem_limit_bytes=VMEM_LIMIT),
        name="attn_prompt",
    )(sinks, q, k, k, vt, vt, bias_t)


def _prompt_bias_t():
    r = jnp.arange(WINDOW, dtype=jnp.int32)[None, :]
    c = jnp.arange(WINDOW, dtype=jnp.int32)[:, None]
    from_prev = c > r
    dist = r - c + jnp.where(from_prev, WINDOW, 0)
    slopes = jnp.exp2(-8.0 * jnp.arange(1, N_HEADS + 1, dtype=F32) / N_HEADS)
    later = -LOG2E * slopes[:, None, None] * dist.astype(F32)[None]
    first = jnp.where(from_prev[None], -jnp.inf, later)
    return jnp.stack([first, later])


def _sample_kernel(x_ref, g_ref, win_ref, qg_ref, kg_ref, bd_ref, wp_ref, ps_ref,
                   st_ref, ck_ref, cv_ref, sink_ref, bias_ref, perm_ref,
                   pool_ref, attn_ref, pst_ref, kc_ref, vc_ref, *, ns, pos0):
    u, qn, kn, v = _project(x_ref[...], g_ref[...], win_ref[...], qg_ref[...], kg_ref[...], bd_ref[...])
    pst_ref[:, 0:POOL_STATE - 1, :] = st_ref[:, 1:POOL_STATE, :]
    kc_ref[:, 0:WINDOW - 1, :] = ck_ref[:, 1:WINDOW, :]
    vc_ref[:, 0:WINDOW - 1, :] = cv_ref[:, 1:WINDOW, :]
    for n in range(ns):
        pst_ref[n, POOL_STATE - 1:POOL_STATE, :] = u[n:n + 1, :]
        kc_ref[n, WINDOW - 1:WINDOW, :] = kn[n:n + 1, :]
        vc_ref[n, WINDOW - 1:WINDOW, :] = v[n:n + 1, :]

    d_groups = []
    for gi, w in enumerate(POOL_WINDOWS):
        lo = gi * POOL_GC
        acc = u[:, lo:lo + POOL_GC]
        for back in range(1, w):
            acc = acc + st_ref[:, POOL_STATE - back, lo:lo + POOL_GC]
        d_groups.append(acc / float(min(pos0 + 1, w)) - u[:, lo:lo + POOL_GC])
    pool_ref[...] = _pool_project(d_groups, wp_ref, ps_ref[...]).astype(BF16)

    zeros = jnp.zeros((ns, HEAD_DIM), F32)
    stacked = []
    for h in range(N_HEADS):
        piece = qn[:, h * HEAD_DIM:(h + 1) * HEAD_DIM]
        pair = [piece, zeros] if h < GQA_GROUP else [zeros, piece]
        stacked.append(jnp.concatenate(pair, axis=-1))
    q_hn = jnp.concatenate(stacked, axis=0).astype(BF16)
    q_nh = jnp.dot(perm_ref[0], q_hn, preferred_element_type=F32).astype(BF16)

    keys = kc_ref[...].reshape(ns * WINDOW, KV_W).astype(BF16)
    vals = vc_ref[...].reshape(ns * WINDOW, KV_W).astype(BF16)
    s_all = lax.dot_general(q_nh, keys, (((1,), (1,)), ((), ())), preferred_element_type=F32)
    sink = sink_ref[...]
    bias = bias_ref[...]
    zero_blk = jnp.zeros((N_HEADS, WINDOW), F32)
    p_rows = []
    for n in range(ns):
        s = s_all[n * N_HEADS:(n + 1) * N_HEADS, n * WINDOW:(n + 1) * WINDOW] + bias
        m = jnp.maximum(jnp.max(s, axis=-1, keepdims=True), sink)
        p = jnp.exp2(s - m)
        denom = jnp.sum(p, axis=-1, keepdims=True) + jnp.exp2(sink - m)
        p_rows.append(jnp.concatenate([zero_blk] * n + [p / denom] + [zero_blk] * (ns - 1 - n), axis=-1))
    p_blockdiag = jnp.concatenate(p_rows, axis=0).astype(BF16)
    o_nh = jnp.dot(p_blockdiag, vals, preferred_element_type=F32).astype(BF16)
    o_hn = jnp.dot(perm_ref[1], o_nh, preferred_element_type=F32)
    pieces = []
    for h in range(N_HEADS):
        kv = h // GQA_GROUP
        pieces.append(o_hn[h * ns:(h + 1) * ns, kv * HEAD_DIM:(kv + 1) * HEAD_DIM])
    attn_ref[...] = jnp.concatenate(pieces, axis=-1).astype(BF16)


def _sample_mixer(l, depth, xs, g_attn, w_in, qg, kg, bd, wp, ps, state, ck, cv, sink8, bias_s, pos0):
    n = xs.shape[0]
    ns = 32
    row = lambda i: (i, 0)
    lay = lambda i: (l, 0, 0)
    src = jnp.arange(ns * N_HEADS)
    perm = (((src % N_HEADS) * ns + src // N_HEADS)[:, None] == src[None, :]).astype(BF16)
    perms = jnp.stack([perm, perm.T])
    return pl.pallas_call(
        functools.partial(_sample_kernel, ns=ns, pos0=pos0),
        grid=(n // ns,),
        input_output_aliases={8: 2, 9: 3, 10: 4},
        in_specs=[
            pl.BlockSpec((ns, D_MODEL), row),
            pl.BlockSpec((None, 1, D_MODEL), lay),
            pl.BlockSpec((None, D_MODEL, D_IN), lay),
            pl.BlockSpec((None, 1, Q_W), lay),
            pl.BlockSpec((None, 1, KV_W), lay),
            pl.BlockSpec((256, 256), lambda i: (0, 0)),
            pl.BlockSpec((None, 2, 256, 256), lambda i: (l, 0, 0, 0)),
            pl.BlockSpec((None, 1, POOL_W), lay),
            pl.BlockSpec((None, ns, POOL_STATE, POOL_W), lambda i: (l, i, 0, 0)),
            pl.BlockSpec((None, ns, WINDOW, KV_W), lambda i: (l, i, 0, 0)),
            pl.BlockSpec((None, ns, WINDOW, KV_W), lambda i: (l, i, 0, 0)),
            pl.BlockSpec((N_HEADS, 1), lambda i: (0, 0)),
            pl.BlockSpec((N_HEADS, WINDOW), lambda i: (0, 0)),
            pl.BlockSpec((2, ns * N_HEADS, ns * N_HEADS), lambda i: (0, 0, 0)),
        ],
        out_specs=[
            pl.BlockSpec((ns, POOL_W), row),
            pl.BlockSpec((ns, Q_W), row),
            pl.BlockSpec((None, ns, POOL_STATE, POOL_W), lambda i: (l, i, 0, 0)),
            pl.BlockSpec((None, ns, WINDOW, KV_W), lambda i: (l, i, 0, 0)),
            pl.BlockSpec((None, ns, WINDOW, KV_W), lambda i: (l, i, 0, 0)),
        ],
        out_shape=[
            jax.ShapeDtypeStruct((n, POOL_W), BF16),
            jax.ShapeDtypeStruct((n, Q_W), BF16),
            jax.ShapeDtypeStruct((depth, n, POOL_STATE, POOL_W), F32),
            jax.ShapeDtypeStruct((depth, n, WINDOW, KV_W), F32),
            jax.ShapeDtypeStruct((depth, n, WINDOW, KV_W), F32),
        ],
        compiler_params=pltpu.CompilerParams(
            dimension_semantics=("arbitrary",), vmem_limit_bytes=VMEM_LIMIT),
        name="sample_mixer",
    )(xs, g_attn, w_in, qg, kg, bd, wp, ps, state, ck, cv, sink8, bias_s, perms)


def _merge_router_kernel(pool_ref, attn_ref, x_ref, wout_ref, g_ref, wr_ref, br_ref, utri_ref, cin_ref,
                         x1_ref, h2_ref, route_t_ref, cnt_ref, y_ref, lg_ref):
    i = pl.program_id(0)

    @pl.when(i == 0)
    def _():
        cnt_ref[...] = cin_ref[...]

    tm = x_ref.shape[0]
    rc = tm // MERGE_CHUNKS
    chunks = [slice(ci * rc, (ci + 1) * rc) for ci in range(MERGE_CHUNKS)]
    for rows in chunks:
        y_ref[rows, :] = (jnp.dot(pool_ref[rows, :], wout_ref[0:POOL_W, :], preferred_element_type=F32)
                          + jnp.dot(attn_ref[rows, :], wout_ref[POOL_W:, :], preferred_element_type=F32))
    for rows in chunks:
        x1 = x_ref[rows, :] + y_ref[rows, :]
        x1_ref[rows, :] = x1
        h2 = _rms_bf16(x1, g_ref[...])
        h2_ref[rows, :] = _pack_bf16_pairs(h2)
        lg_ref[rows, :] = jnp.dot(h2, wr_ref[...], preferred_element_type=F32) + br_ref[...]
    logits = lg_ref[...]

    lt = jnp.transpose(logits)
    sub = lax.broadcasted_iota(jnp.int32, (EXPERTS_PER_GROUP, tm), 0)
    neg = -jnp.inf
    big = jnp.int32(EXPERTS_PER_GROUP)
    gl = jnp.where(sub < N_EXPERT_GROUPS, lt[GROUP_LANE0:GROUP_LANE0 + EXPERTS_PER_GROUP, :], neg)
    gmax = jnp.max(gl, axis=0, keepdims=True)
    grp = jnp.min(jnp.where(gl == gmax, sub, big), axis=0, keepdims=True)
    g_w = 1.0 / jnp.sum(jnp.exp(gl - gmax), axis=0, keepdims=True)
    el = lt[(N_EXPERT_GROUPS - 1) * EXPERTS_PER_GROUP:N_EXPERT_GROUPS * EXPERTS_PER_GROUP, :]
    for gi in range(N_EXPERT_GROUPS - 2, -1, -1):
        el = jnp.where(grp == gi, lt[gi * EXPERTS_PER_GROUP:(gi + 1) * EXPERTS_PER_GROUP, :], el)
    v1 = jnp.max(el, axis=0, keepdims=True)
    i1 = jnp.min(jnp.where(el == v1, sub, big), axis=0, keepdims=True)
    el2 = jnp.where(sub == i1, neg, el)
    v2 = jnp.max(el2, axis=0, keepdims=True)
    i2 = jnp.min(jnp.where(el2 == v2, sub, big), axis=0, keepdims=True)
    e21 = jnp.exp(v2 - v1)
    w1 = g_w / (1.0 + e21)
    w2 = g_w * e21 / (1.0 + e21)
    e1 = grp * EXPERTS_PER_GROUP + i1
    e2 = grp * EXPERTS_PER_GROUP + i2

    esub = lax.broadcasted_iota(jnp.int32, (N_EXPERTS, tm), 0)
    oh1 = esub == e1
    oh2 = esub == e2
    c = jnp.where(oh1 | oh2, 1.0, 0.0)
    prefix = jnp.dot(c.astype(BF16), utri_ref[...], preferred_element_type=F32) + cnt_ref[...]
    r1 = jnp.sum(jnp.where(oh1, prefix, 0.0), axis=0, keepdims=True)
    r2 = jnp.sum(jnp.where(oh2, prefix, 0.0), axis=0, keepdims=True)
    cnt_ref[...] = cnt_ref[...] + jnp.sum(c, axis=1, keepdims=True)

    fields = jnp.zeros((ROUTE_FIELDS, tm), F32)
    for idx, val in enumerate((e1.astype(F32), e2.astype(F32), w1, w2, r1, r2)):
        fields = jnp.where(sub == idx, val, fields)
    route_t_ref[...] = fields


def _merge_router(l, pool, attn, x2d, w_out, g_ffn, wr, br, cnt_in, tm):
    t = x2d.shape[0]
    utri = (jnp.arange(tm)[:, None] < jnp.arange(tm)[None, :]).astype(BF16)
    row = lambda i: (i, 0)
    lay = lambda i: (l, 0, 0)
    return pl.pallas_call(
        _merge_router_kernel,
        grid=(t // tm,),
        in_specs=[
            pl.BlockSpec((tm, POOL_W), row),
            pl.BlockSpec((tm, Q_W), row),
            pl.BlockSpec((tm, D_MODEL), row),
            pl.BlockSpec((None, D_MODEL, D_MODEL), lay),
            pl.BlockSpec((None, 1, D_MODEL), lay),
            pl.BlockSpec((None, D_MODEL, LANES), lay),
            pl.BlockSpec((None, 1, LANES), lay),
            pl.BlockSpec((tm, tm), lambda i: (0, 0)),
            pl.BlockSpec((N_EXPERTS, 1), lambda i: (0, 0)),
        ],
        out_specs=[
            pl.BlockSpec((tm, D_MODEL), row),
            pl.BlockSpec((tm, D_MODEL // 2), row),
            pl.BlockSpec((ROUTE_FIELDS, tm), lambda i: (0, i)),
            pl.BlockSpec((N_EXPERTS, 1), lambda i: (0, 0)),
        ],
        out_shape=[
            jax.ShapeDtypeStruct((t, D_MODEL), F32),
            jax.ShapeDtypeStruct((t, D_MODEL // 2), jnp.int32),
            jax.ShapeDtypeStruct((ROUTE_FIELDS, t), F32),
            jax.ShapeDtypeStruct((N_EXPERTS, 1), F32),
        ],
        scratch_shapes=[pltpu.VMEM((tm, D_MODEL), F32), pltpu.VMEM((tm, LANES), F32)],
        compiler_params=pltpu.CompilerParams(
            dimension_semantics=("arbitrary",), vmem_limit_bytes=VMEM_LIMIT),
        name="merge_router",
    )(pool, attn, x2d, w_out, g_ffn, wr, br, utri, cnt_in)


def _moe_kernel(be_ref, rv_ref, nx_ref, sl_ref, first_ref, xd_ref, wg_hbm, wu_hbm, wd_hbm, yd_ref,
                wg_f, wu_f, wd_f, wg_s, wu_s, wd_s, sem, *, layer):
    step = pl.program_id(0)

    def weight_copies(e, s):
        return [pltpu.make_async_copy(w_hbm.at[layer, e], w_f.at[s], sem.at[s, n])
                for n, (w_hbm, w_f) in enumerate(((wg_hbm, wg_f), (wu_hbm, wu_f), (wd_hbm, wd_f)))]

    @pl.when(step == 0)
    def _():
        for s in range(WEIGHT_SLOTS - 1):
            @pl.when(first_ref[s] >= 0)
            def _():
                for c in weight_copies(first_ref[s], s):
                    c.start()

    def enter_expert(i):
        expert, slot = be_ref[i], sl_ref[i]

        @pl.when((i == 0) | (expert != be_ref[jnp.maximum(i - 1, 0)]))
        def _():
            for c in weight_copies(expert, slot):
                c.wait()

            @pl.when(nx_ref[i] >= 0)
            def _():
                for c in weight_copies(nx_ref[i], lax.rem(slot + WEIGHT_SLOTS - 1, WEIGHT_SLOTS)):
                    c.start(priority=1)

            wg_s[...] = wg_f[slot].astype(BF16)
            wu_s[...] = wu_f[slot].astype(BF16)
            wd_s[...] = wd_f[slot].astype(BF16)

    def experts_on(row0, n_rows, rows_valid):
        rows = pl.ds(row0, n_rows)
        row = lax.broadcasted_iota(jnp.int32, (n_rows, D_MODEL // 2), 0)
        x = _unpack_bf16_pairs(jnp.where(row < rows_valid, xd_ref[rows, :], 0)).astype(BF16)
        gate = jnp.dot(x, wg_s[...], preferred_element_type=F32)
        up = jnp.dot(x, wu_s[...], preferred_element_type=F32)
        act = (gate * jax.nn.sigmoid(gate) * up).astype(BF16)
        y = jnp.dot(act, wd_s[...], preferred_element_type=F32)
        yd_ref[rows, :] = _pack_bf16_pairs(y.astype(BF16))

    def experts_ragged(row0, lead_rows, rows_last):
        half = MOE_BM // 2

        @pl.when(rows_last > half)
        def _():
            experts_on(row0, lead_rows + MOE_BM, lead_rows + rows_last)

        @pl.when(rows_last <= half)
        def _():
            experts_on(row0, lead_rows + half, lead_rows + rows_last)
            yd_ref[pl.ds(row0 + lead_rows + half, half), :] = jnp.zeros((half, D_MODEL // 2), jnp.int32)

    def single_block(i, row0):
        enter_expert(i)

        @pl.when(rv_ref[i] > 0)
        def _():
            experts_ragged(row0, 0, rv_ref[i])

        @pl.when(rv_ref[i] <= 0)
        def _():
            yd_ref[pl.ds(row0, MOE_BM), :] = jnp.zeros((MOE_BM, D_MODEL // 2), jnp.int32)

    @pl.when(rv_ref[step * MOE_STEP_BLOCKS] > 0)
    def _():
        for pair in range(MOE_STEP_BLOCKS // 2):
            ia = step * MOE_STEP_BLOCKS + 2 * pair
            ib = ia + 1
            row0 = 2 * pair * MOE_BM
            same = (be_ref[ib] == be_ref[ia]) & (rv_ref[ib] > 0)

            @pl.when(same)
            def _():
                enter_expert(ia)
                experts_ragged(row0, MOE_BM, rv_ref[ib])

            @pl.when(jnp.logical_not(same))
            def _():
                single_block(ia, row0)
                single_block(ib, row0 + MOE_BM)


def _moe_experts(l, block_e, rows_valid, next_e, slot, first_e, xd, w_gate, w_up, w_down):
    n_blocks = xd.shape[0] // MOE_BM
    step_rows = MOE_STEP_BLOCKS * MOE_BM
    row = lambda i, be, rv, nx, sl, fe: (jnp.minimum(i, fe[WEIGHT_SLOTS - 1] - 1), 0)
    return pl.pallas_call(
        functools.partial(_moe_kernel, layer=l),
        grid_spec=pltpu.PrefetchScalarGridSpec(
            num_scalar_prefetch=5,
            grid=(n_blocks // MOE_STEP_BLOCKS,),
            in_specs=[
                pl.BlockSpec((step_rows, D_MODEL // 2), row),
                pl.BlockSpec(memory_space=pl.ANY),
                pl.BlockSpec(memory_space=pl.ANY),
                pl.BlockSpec(memory_space=pl.ANY),
            ],
            out_specs=pl.BlockSpec((step_rows, D_MODEL // 2), row),
            scratch_shapes=[
                pltpu.VMEM((WEIGHT_SLOTS, D_MODEL, EXPERT_FF), F32),
                pltpu.VMEM((WEIGHT_SLOTS, D_MODEL, EXPERT_FF), F32),
                pltpu.VMEM((WEIGHT_SLOTS, EXPERT_FF, D_MODEL), F32),
                pltpu.VMEM((D_MODEL, EXPERT_FF), BF16),
                pltpu.VMEM((D_MODEL, EXPERT_FF), BF16),
                pltpu.VMEM((EXPERT_FF, D_MODEL), BF16),
                pltpu.SemaphoreType.DMA((WEIGHT_SLOTS, 3)),
            ],
        ),
        out_shape=jax.ShapeDtypeStruct((n_blocks * MOE_BM, D_MODEL // 2), jnp.int32),
        compiler_params=pltpu.CompilerParams(
            dimension_semantics=("arbitrary",), vmem_limit_bytes=VMEM_LIMIT),
        name="moe_experts",
    )(block_e, rows_valid, next_e, slot, first_e, xd, w_gate, w_up, w_down)


def _sc_worker_id():
    return lax.axis_index("s") * SC_CORES + lax.axis_index("c")


def _sc_dispatch(hp, hs, dest_p, dest_s, n_rows):
    tp, width = hp.shape
    per_w = tp // SC_WORKERS
    n_ch = per_w // DISP_CH
    n_sw = hs.shape[0] // SAMPLE_CH
    mesh = plsc.VectorSubcoreMesh(core_axis_name="c", subcore_axis_name="s")

    @functools.partial(
        pl.kernel, mesh=mesh,
        out_type=jax.ShapeDtypeStruct((n_rows, width), jnp.int32),
        scratch_types=[
            pltpu.VMEM((2, n_ch, DISP_CH), jnp.int32),
            pltpu.VMEM((2, 1, SAMPLE_CH), jnp.int32),
            pltpu.VMEM((SC_RING, DISP_CH, width), jnp.int32),
            pltpu.SemaphoreType.DMA((SC_RING,)),
            pltpu.SemaphoreType.DMA((SC_RING, 2)),
        ],
        name="sc_dispatch",
    )
    def k(hp_hbm, hs_hbm, dp_hbm, ds_hbm, xd_hbm, idx_v, idxs_v, bufs, rsem, wsem):
        wid = _sc_worker_id()
        base = wid * per_w
        for kk in range(2):
            pltpu.sync_copy(dp_hbm.at[kk, wid], idx_v.at[kk])
        reads = [pltpu.make_async_copy(hp_hbm.at[pl.ds(base + j * DISP_CH, DISP_CH)],
                                       bufs.at[j % SC_RING], rsem.at[j % SC_RING]) for j in range(n_ch)]
        writes = [[pltpu.make_async_copy(bufs.at[j % SC_RING], xd_hbm.at[idx_v.at[kk, j]],
                                         wsem.at[j % SC_RING, kk]) for kk in range(2)] for j in range(n_ch)]
        for j in range(min(SC_RING - 1, n_ch)):
            reads[j].start()
        for j in range(n_ch):
            reads[j].wait()
            for w in writes[j]:
                w.start()
            if j >= 1:
                for w in writes[j - 1]:
                    w.wait()
            if j + SC_RING - 1 < n_ch:
                reads[j + SC_RING - 1].start()
        for w in writes[n_ch - 1]:
            w.wait()

        @pl.when(wid < n_sw)
        def _():
            rows = bufs.at[0, pl.ds(0, SAMPLE_CH)]
            for kk in range(2):
                pltpu.sync_copy(ds_hbm.at[kk, wid], idxs_v.at[kk])
            pltpu.sync_copy(hs_hbm.at[pl.ds(wid * SAMPLE_CH, SAMPLE_CH)], rows)
            for kk in range(2):
                pltpu.sync_copy(rows, xd_hbm.at[idxs_v.at[kk, 0]])

    return k(hp, hs, dest_p, dest_s)


def _sc_combine_gather(yd, dest_p, dest_s, tp, ts):
    width = yd.shape[1]
    per_w = tp // SC_WORKERS
    n_ch = per_w // COMB_CH
    n_sw = ts // SAMPLE_CH
    mesh = plsc.VectorSubcoreMesh(core_axis_name="c", subcore_axis_name="s")

    @functools.partial(
        pl.kernel, mesh=mesh,
        out_type=jax.ShapeDtypeStruct((2, tp + ts, width), yd.dtype),
        scratch_types=[
            pltpu.VMEM((2, n_ch, COMB_CH), jnp.int32),
            pltpu.VMEM((2, 1, SAMPLE_CH), jnp.int32),
            pltpu.VMEM((SC_RING, COMB_CH, width), yd.dtype),
            pltpu.SemaphoreType.DMA((SC_RING,)),
            pltpu.SemaphoreType.DMA((SC_RING,)),
        ],
        name="sc_combine_gather",
    )
    def k(yd_hbm, dp_hbm, ds_hbm, g_hbm, idx_v, idxs_v, bufs, gsem, wsem):
        wid = _sc_worker_id()
        base = wid * per_w
        for kk in range(2):
            pltpu.sync_copy(dp_hbm.at[kk, wid], idx_v.at[kk])
        items = [(kk, j) for kk in range(2) for j in range(n_ch)]
        n_items = len(items)
        gathers = [pltpu.make_async_copy(yd_hbm.at[idx_v.at[kk, j]], bufs.at[n % SC_RING], gsem.at[n % SC_RING])
                   for n, (kk, j) in enumerate(items)]
        outs = [pltpu.make_async_copy(bufs.at[n % SC_RING], g_hbm.at[kk, pl.ds(base + j * COMB_CH, COMB_CH)],
                                      wsem.at[n % SC_RING]) for n, (kk, j) in enumerate(items)]
        for n in range(min(SC_RING - 1, n_items)):
            gathers[n].start()
        for n in range(n_items):
            gathers[n].wait()
            outs[n].start()
            if n >= 1:
                outs[n - 1].wait()
            if n + SC_RING - 1 < n_items:
                gathers[n + SC_RING - 1].start()
        outs[n_items - 1].wait()

        @pl.when(wid < n_sw)
        def _():
            for kk in range(2):
                pltpu.sync_copy(ds_hbm.at[kk, wid], idxs_v.at[kk])
            for kk in range(2):
                rows = bufs.at[kk, pl.ds(0, SAMPLE_CH)]
                pltpu.sync_copy(yd_hbm.at[idxs_v.at[kk, 0]], rows)
                pltpu.sync_copy(rows, g_hbm.at[kk, pl.ds(tp + wid * SAMPLE_CH, SAMPLE_CH)])

    return k(yd, dest_p, dest_s)


def _combine_kernel(x1_ref, g_ref, route_t_ref, x2_ref):
    fields = route_t_ref[...]
    tm = fields.shape[1]
    cols = jnp.transpose(jnp.concatenate([fields, jnp.zeros((LANES - ROUTE_FIELDS, tm), F32)], axis=0))
    w1 = cols[:, 2:3]
    w2 = cols[:, 3:4]
    x2_ref[...] = x1_ref[...] + _unpack_bf16_pairs(g_ref[0]) * w1 + _unpack_bf16_pairs(g_ref[1]) * w2


def _combine(x1, g, route_t, row0, tm):
    t = x1.shape[0]
    blk0 = row0 // tm
    row = lambda i: (i, 0)
    return pl.pallas_call(
        _combine_kernel,
        grid=(t // tm,),
        in_specs=[
            pl.BlockSpec((tm, D_MODEL), row),
            pl.BlockSpec((2, tm, D_MODEL // 2), lambda i: (0, blk0 + i, 0)),
            pl.BlockSpec((ROUTE_FIELDS, tm), lambda i: (0, i)),
        ],
        out_specs=pl.BlockSpec((tm, D_MODEL), row),
        out_shape=jax.ShapeDtypeStruct((t, D_MODEL), F32),
        compiler_params=pltpu.CompilerParams(
            dimension_semantics=("arbitrary",), vmem_limit_bytes=VMEM_LIMIT),
        name="combine",
    )(x1, g, route_t)


def _dest_layout(dest, workers, chunk):
    t = dest.shape[1]
    return dest.reshape(2, workers, t // (workers * chunk), chunk)


def _hier_moe(l, h2p, h2s, route_tp, route_ts, counts, w_gate, w_up, w_down):
    tp, ts = h2p.shape[0], h2s.shape[0]
    n_assign = 2 * (tp + ts)
    n_blocks = -(-n_assign // MOE_BM) + N_EXPERTS
    n_blocks = -(-n_blocks // MOE_STEP_BLOCKS) * MOE_STEP_BLOCKS
    pcounts = (counts + MOE_BM - 1) // MOE_BM * MOE_BM
    pends = jnp.cumsum(pcounts)
    poffsets = pends - pcounts
    starts = jnp.arange(n_blocks, dtype=jnp.int32) * MOE_BM
    block_e = jnp.minimum(jnp.sum((pends[None, :] <= starts[:, None]).astype(jnp.int32), axis=1),
                          N_EXPERTS - 1)
    experts = jnp.arange(N_EXPERTS, dtype=jnp.int32)

    def lookup(table, idx):
        return jnp.sum(jnp.where(idx[..., None] == experts, table, 0), axis=-1)

    rows_valid = jnp.clip(lookup(poffsets + counts, block_e) - starts, 0, MOE_BM).astype(jnp.int32)
    used = counts > 0
    last_e = jnp.max(jnp.where(used, jnp.arange(N_EXPERTS, dtype=jnp.int32), 0))
    block_e = jnp.where(rows_valid > 0, block_e, last_e).astype(jnp.int32)
    place = jnp.cumsum(used.astype(jnp.int32)) - 1
    by_place = jnp.sum(jnp.where(used[None, :] & (place[None, :] == experts[:, None]), experts[None, :], 0),
                       axis=1)
    n_used = jnp.sum(used.astype(jnp.int32))

    def at_place(p):
        return jnp.where(p < n_used, lookup(by_place, jnp.minimum(p, N_EXPERTS - 1)), -1).astype(jnp.int32)

    ahead_of = at_place(place + (WEIGHT_SLOTS - 1))
    next_e = lookup(ahead_of, block_e)
    slot = lookup(place % WEIGHT_SLOTS, block_e)
    n_steps_used = -(-(pends[-1] // MOE_BM) // MOE_STEP_BLOCKS)
    first_e = jnp.concatenate([at_place(jnp.arange(WEIGHT_SLOTS - 1, dtype=jnp.int32)),
                               n_steps_used.reshape(1).astype(jnp.int32)])

    def dest_of(route_t):
        return lookup(poffsets, route_t[0:2].astype(jnp.int32)) + route_t[4:6].astype(jnp.int32)

    dest_p, dest_s = dest_of(route_tp), dest_of(route_ts)
    n_sw = ts // SAMPLE_CH
    xd = _sc_dispatch(h2p, h2s, _dest_layout(dest_p, SC_WORKERS, DISP_CH),
                      _dest_layout(dest_s, n_sw, SAMPLE_CH), n_blocks * MOE_BM)
    yd = _moe_experts(l, block_e, rows_valid, next_e, slot, first_e, xd, w_gate, w_up, w_down)
    return _sc_combine_gather(yd, _dest_layout(dest_p, SC_WORKERS, COMB_CH),
                              _dest_layout(dest_s, n_sw, SAMPLE_CH), tp, ts)


def kernel(x_prompt, x_sample, state_pool, cache_k_win, cache_v_win, norm_attn_g, w_in, pool_w, pool_scale, q_norm_g, k_norm_g, attn_sinks, w_out, norm_ffn_g, router_group_w, router_group_b, router_expert_w, router_expert_b, w_gate, w_up, w_down):
    n_p, t_p, d = x_prompt.shape
    n_s, t_s, _ = x_sample.shape
    depth = w_in.shape[0]
    lw_s = cache_k_win.shape[2]
    assert t_s == 1 and lw_s == WINDOW and d == D_MODEL
    assert t_p % TM_PROJ == 0 and t_p >= WINDOW

    seg = jnp.arange(256) // HEAD_DIM
    bd = jnp.where(seg[:, None] == seg[None, :], 1.0 / HEAD_DIM, 0.0).astype(BF16)
    slopes = jnp.exp2(-8.0 * jnp.arange(1, N_HEADS + 1, dtype=F32) / N_HEADS)
    bias_p = _prompt_bias_t()
    dist_s = (WINDOW - 1) - jnp.arange(WINDOW, dtype=F32)
    bias_s = -LOG2E * slopes[:, None] * dist_s[None, :]

    wp = jnp.zeros((depth, 2, 256, 256), F32)
    for p in range(2):
        wp = wp.at[:, p, :POOL_GC, :POOL_GC].set(pool_w[:, 2 * p])
        wp = wp.at[:, p, POOL_GC:, POOL_GC:].set(pool_w[:, 2 * p + 1])
    assert GROUP_LANE0 == N_EXPERTS
    lane_pad = LANES - N_EXPERTS - N_EXPERT_GROUPS
    wr = jnp.concatenate([router_expert_w, router_group_w, jnp.zeros((depth, D_MODEL, lane_pad), F32)], axis=-1)
    br = jnp.concatenate([router_expert_b, router_group_b, jnp.zeros((depth, lane_pad), F32)],
                         axis=-1).reshape(depth, 1, LANES)
    lp = dict(
        w_in=w_in.astype(BF16),
        w_out=w_out.astype(BF16),
        g_attn=norm_attn_g.reshape(depth, 1, D_MODEL),
        g_ffn=norm_ffn_g.reshape(depth, 1, D_MODEL),
        qg=(jnp.tile(q_norm_g, (1, N_HEADS)) * (ATTN_SCALE * LOG2E)).reshape(depth, 1, Q_W),
        kg=jnp.tile(k_norm_g, (1, N_KV_HEADS)).reshape(depth, 1, KV_W),
        wp=wp.astype(BF16),
        ps=pool_scale.reshape(depth, 1, POOL_W),
        wr=wr.astype(BF16),
        br=br,
        state=state_pool,
        ck=cache_k_win.reshape(depth, n_s, lw_s, KV_W),
        cv=cache_v_win.reshape(depth, n_s, lw_s, KV_W),
    )

    xp = x_prompt.reshape(n_p * t_p, D_MODEL)
    xs = x_sample.reshape(n_s, D_MODEL)
    lw_p = min(WINDOW, t_p)
    pool_p, kp_new, vp_new = [], [], []
    sample_state = [lp["state"], lp["ck"], lp["cv"]]
    zero_cnt = jnp.zeros((N_EXPERTS, 1), F32)
    pending = None
    for l in range(depth):
        sinks = attn_sinks[l] * LOG2E
        outs = _proj_pool_prompt(
            l, xp if pending is None else pending, n_p, t_p,
            lp["g_attn"], lp["w_in"], lp["qg"], lp["kg"], bd, lp["wp"], lp["ps"])
        if pending is not None:
            xp, outs = outs[0], outs[1:]
        pool_o, q, k, vt, utail, ktail, vtail = outs
        attn_o = _attn_prompt(q, k, vt, bias_p, sinks, n_p, t_p)
        x1p, h2p, route_tp, cnt_p = _merge_router(
            l, pool_o, attn_o, xp, lp["w_out"], lp["g_ffn"], lp["wr"], lp["br"], zero_cnt, TM_MERGE)
        pool_p.append(utail[:, 16 - POOL_STATE:, :])
        kp_new.append(ktail)
        vp_new.append(vtail)
        pool_so, attn_so, *sample_state = _sample_mixer(
            l, depth, xs, lp["g_attn"], lp["w_in"], lp["qg"], lp["kg"], bd, lp["wp"], lp["ps"],
            *sample_state, sinks.reshape(N_HEADS, 1), bias_s, PAST_LEN)
        x1s, h2s, route_ts, cnt_all = _merge_router(
            l, pool_so, attn_so, xs, lp["w_out"], lp["g_ffn"], lp["wr"], lp["br"], cnt_p, n_s)
        counts = cnt_all[:, 0].astype(jnp.int32)
        g = _hier_moe(l, h2p, h2s, route_tp, route_ts, counts, w_gate, w_up, w_down)
        xs = _combine(x1s, g, route_ts, n_p * t_p, n_s)
        pending = (x1p, g, route_tp)
    xp = _combine(*pending, 0, TM_MERGE)
    return (xp.reshape(n_p, t_p, D_MODEL), xs.reshape(n_s, t_s, D_MODEL),
            jnp.stack(pool_p),
            jnp.stack(kp_new).reshape(depth, n_p, lw_p, N_KV_HEADS, HEAD_DIM),
            jnp.stack(vp_new).reshape(depth, n_p, lw_p, N_KV_HEADS, HEAD_DIM),
            sample_state[0],
            sample_state[1].reshape(depth, n_s, lw_s, N_KV_HEADS, HEAD_DIM),
            sample_state[2].reshape(depth, n_s, lw_s, N_KV_HEADS, HEAD_DIM))
```

```python
import functools

import jax
import jax.numpy as jnp
from jax import lax
from jax.experimental import pallas as pl
from jax.experimental.pallas import tpu as pltpu
from jax.experimental.pallas import tpu_sc as plsc

D_MODEL = 1024
POOL_W = 512
POOL_WINDOWS = (2, 4, 8, 16)
POOL_GC = 128
POOL_STATE = 15
HEAD_DIM = 64
N_HEADS = 8
N_KV_HEADS = 2
GQA_GROUP = 4
Q_W = 512
KV_W = 128
D_IN = POOL_W + Q_W + 2 * KV_W
WINDOW = 128
ATTN_SCALE = HEAD_DIM ** -0.5
LOG2E = 1.4426950408889634
N_EXPERT_GROUPS = 4
EXPERTS_PER_GROUP = 8
N_EXPERTS = 32
EXPERT_FF = 512
EPS = 1e-6
PAST_LEN = 16384

LANES = 128
HALO = 32
TM_PROJ = 1024
TM_MERGE = 1024
MERGE_CHUNKS = 4
ATTN_QB = 16
MOE_BM = 256
MOE_STEP_BLOCKS = 4
WEIGHT_SLOTS = 3
GROUP_LANE0 = 32
ROUTE_FIELDS = 8
SC_CORES = 2
SC_SUBCORES = 16
SC_WORKERS = SC_CORES * SC_SUBCORES
DISP_CH = 64
COMB_CH = 64
SAMPLE_CH = 32
SC_RING = 3
VMEM_LIMIT = 48 * 1024 * 1024

BF16 = jnp.bfloat16
F32 = jnp.float32


def _pack_bf16_pairs(h):
    w = h.shape[1] // 2
    hi = lax.bitcast_convert_type(h[:, :w].astype(F32), jnp.uint32)
    lo = lax.bitcast_convert_type(h[:, w:].astype(F32), jnp.uint32)
    return lax.bitcast_convert_type(hi | (lo >> 16), jnp.int32)


def _unpack_bf16_pairs(words):
    u = lax.bitcast_convert_type(words, jnp.uint32)
    hi = lax.bitcast_convert_type(u & jnp.uint32(0xFFFF0000), F32)
    lo = lax.bitcast_convert_type(u << 16, F32)
    return jnp.concatenate([hi, lo], axis=-1)


def _segment_mean_sq(a, bd):
    w = a.shape[1]
    return jnp.dot((a * a).astype(BF16), bd[:w, :w], preferred_element_type=F32)


def _rms_bf16(x, g):
    ms = jnp.mean(x * x, axis=-1, keepdims=True)
    return (x * lax.rsqrt(ms + EPS) * g).astype(BF16)


def _qk_norm(q, k, qg, kg, bd):
    qn = []
    for c in range(Q_W // 256):
        qc = q[:, c * 256:(c + 1) * 256]
        qn.append(qc * lax.rsqrt(_segment_mean_sq(qc, bd) + EPS))
    qn = jnp.concatenate(qn, axis=-1) * qg
    kn = k * lax.rsqrt(_segment_mean_sq(k, bd) + EPS) * kg
    return qn, kn


def _project(x, g, w_in, qg, kg, bd):
    z = jnp.dot(_rms_bf16(x, g), w_in, preferred_element_type=F32)
    u = z[:, :POOL_W]
    q = z[:, POOL_W:POOL_W + Q_W]
    k = z[:, POOL_W + Q_W:POOL_W + Q_W + KV_W]
    v = z[:, POOL_W + Q_W + KV_W:]
    qn, kn = _qk_norm(q, k, qg, kg, bd)
    return u, qn, kn, v


def _pool_project(d_groups, wp_ref, ps):
    outs = []
    for p in range(2):
        dp = jnp.concatenate([d_groups[2 * p], d_groups[2 * p + 1]], axis=-1).astype(BF16)
        y = jnp.dot(dp, wp_ref[p], preferred_element_type=F32)
        outs.append(y * ps[:, p * 256:(p + 1) * 256])
    return jnp.concatenate(outs, axis=-1)


def _proj_pool_kernel(x_ref, g_ref, win_ref, qg_ref, kg_ref, bd_ref, wp_ref, ps_ref,
                      pool_ref, q_ref, k_ref, vt_ref, utail_ref, ktail_ref, vtail_ref,
                      ext_ref, sa_ref, sb_ref, zq_ref, *, tm, n_j):
    j = pl.program_id(1)

    @pl.when(j == 0)
    def _():
        ext_ref[0:HALO, :] = jnp.zeros((HALO, POOL_W), F32)

    r = tm + HALO
    h = _rms_bf16(x_ref[...], g_ref[...])
    ext_ref[HALO:r, :] = jnp.dot(h, win_ref[:, 0:POOL_W], preferred_element_type=F32)
    zq_ref[...] = jnp.dot(h, win_ref[:, POOL_W:], preferred_element_type=F32)
    u = ext_ref[HALO:r, :]
    sa_ref[8:r, :] = ext_ref[8:r, :] + ext_ref[7:r - 1, :]
    sb_ref[16:r, 128:] = sa_ref[16:r, 128:] + sa_ref[14:r - 2, 128:]
    sa_ref[24:r, 256:] = sb_ref[24:r, 256:] + sb_ref[20:r - 4, 256:]
    sb_ref[32:r, 384:] = sa_ref[32:r, 384:] + sa_ref[24:r - 8, 384:]
    pos1 = j * tm + lax.broadcasted_iota(jnp.int32, (tm, POOL_GC), 0) + 1
    sums = (sa_ref, sb_ref, sa_ref, sb_ref)
    d_groups = []
    for gi, w in enumerate(POOL_WINDOWS):
        sl = slice(gi * POOL_GC, (gi + 1) * POOL_GC)
        cnt = jnp.minimum(pos1, w).astype(F32)
        d_groups.append(sums[gi][HALO:r, sl] / cnt - u[:, sl])
    pool_ref[...] = _pool_project(d_groups, wp_ref, ps_ref[...]).astype(BF16)
    ext_ref[16:HALO, :] = ext_ref[tm + 16:r, :]

    qn, kn = _qk_norm(zq_ref[:, 0:Q_W], zq_ref[:, Q_W:Q_W + KV_W], qg_ref[...], kg_ref[...], bd_ref[...])
    v = zq_ref[:, Q_W + KV_W:]
    q_ref[...] = qn.astype(BF16)
    k_ref[...] = kn.astype(BF16)
    vt_ref[...] = jnp.transpose(v).astype(BF16)

    @pl.when(j == n_j - 1)
    def _():
        utail_ref[...] = u[tm - 16:, :]
        ktail_ref[...] = kn[tm - WINDOW:, :]
        vtail_ref[...] = v[tm - WINDOW:, :]


def _proj_pool_combine_kernel(x1_ref, gath_ref, route_ref, *rest, tm, n_j):
    x2_ref = rest[7]
    _combine_kernel(x1_ref, gath_ref, route_ref, x2_ref)
    _proj_pool_kernel(x2_ref, *rest[:7], *rest[8:], tm=tm, n_j=n_j)


def _proj_pool_prompt(l, x_in, n_seq, seq, g_attn, w_in, qg, kg, bd, wp, ps):
    tm = TM_PROJ
    n_j = seq // tm
    t = n_seq * seq
    row = lambda b, j: (b * n_j + j, 0)
    lay = lambda b, j: (l, 0, 0)
    fused = isinstance(x_in, tuple)
    if fused:
        kern = _proj_pool_combine_kernel
        x_args = list(x_in)
        x_specs = [pl.BlockSpec((tm, D_MODEL), row),
                   pl.BlockSpec((2, tm, D_MODEL // 2), lambda b, j: (0, b * n_j + j, 0)),
                   pl.BlockSpec((ROUTE_FIELDS, tm), lambda b, j: (0, b * n_j + j))]
        x_out_specs = [pl.BlockSpec((tm, D_MODEL), row)]
        x_out_shape = [jax.ShapeDtypeStruct((t, D_MODEL), F32)]
    else:
        kern = _proj_pool_kernel
        x_args = [x_in]
        x_specs = [pl.BlockSpec((tm, D_MODEL), row)]
        x_out_specs, x_out_shape = [], []
    return pl.pallas_call(
        functools.partial(kern, tm=tm, n_j=n_j),
        grid=(n_seq, n_j),
        in_specs=x_specs + [
            pl.BlockSpec((None, 1, D_MODEL), lay),
            pl.BlockSpec((None, D_MODEL, D_IN), lay),
            pl.BlockSpec((None, 1, Q_W), lay),
            pl.BlockSpec((None, 1, KV_W), lay),
            pl.BlockSpec((256, 256), lambda b, j: (0, 0)),
            pl.BlockSpec((None, 2, 256, 256), lambda b, j: (l, 0, 0, 0)),
            pl.BlockSpec((None, 1, POOL_W), lay),
        ],
        out_specs=x_out_specs + [
            pl.BlockSpec((tm, POOL_W), row),
            pl.BlockSpec((tm, Q_W), row),
            pl.BlockSpec((tm, KV_W), row),
            pl.BlockSpec((KV_W, tm), lambda b, j: (0, b * n_j + j)),
            pl.BlockSpec((None, 16, POOL_W), lambda b, j: (b, 0, 0)),
            pl.BlockSpec((None, WINDOW, KV_W), lambda b, j: (b, 0, 0)),
            pl.BlockSpec((None, WINDOW, KV_W), lambda b, j: (b, 0, 0)),
        ],
        out_shape=x_out_shape + [
            jax.ShapeDtypeStruct((t, POOL_W), BF16),
            jax.ShapeDtypeStruct((t, Q_W), BF16),
            jax.ShapeDtypeStruct((t, KV_W), BF16),
            jax.ShapeDtypeStruct((KV_W, t), BF16),
            jax.ShapeDtypeStruct((n_seq, 16, POOL_W), F32),
            jax.ShapeDtypeStruct((n_seq, WINDOW, KV_W), F32),
            jax.ShapeDtypeStruct((n_seq, WINDOW, KV_W), F32),
        ],
        scratch_shapes=[pltpu.VMEM((tm + HALO, POOL_W), F32)] * 3 + [pltpu.VMEM((tm, Q_W + 2 * KV_W), F32)],
        compiler_params=pltpu.CompilerParams(
            dimension_semantics=("arbitrary", "arbitrary"), vmem_limit_bytes=VMEM_LIMIT),
        name="proj_pool_prompt",
    )(*x_args, g_attn, w_in, qg, kg, bd, wp, ps)


def _attn_kernel(sink_ref, q_ref, kp_ref, kc_ref, vtp_ref, vtc_ref, bias_ref, o_ref, s_ref):
    j = pl.program_id(1)
    kk_all = jnp.concatenate([kp_ref[...], kc_ref[...]], axis=0)
    vt_all = jnp.concatenate([vtp_ref[...], vtc_ref[...]], axis=1)
    from_prev = (lax.broadcasted_iota(jnp.int32, (WINDOW, WINDOW), 0)
                 > lax.broadcasted_iota(jnp.int32, (WINDOW, WINDOW), 1))
    units = [(blk, kv) for blk in range(ATTN_QB) for kv in range(N_KV_HEADS)]

    def scores(n):
        blk, kv = units[n]
        q = q_ref[blk * WINDOW:(blk + 1) * WINDOW, :]
        kk = kk_all[blk * WINDOW:(blk + 2) * WINDOW, kv * HEAD_DIM:(kv + 1) * HEAD_DIM]
        heads = range(kv * GQA_GROUP, (kv + 1) * GQA_GROUP)
        q_rows = jnp.concatenate([q[:, h * HEAD_DIM:(h + 1) * HEAD_DIM] for h in heads], axis=0)
        s_ref[n % 3] = lax.dot_general(kk, q_rows, (((1,), (1,)), ((), ())), preferred_element_type=F32)

    scores(0)
    scores(1)
    outs = []
    for n, (blk, kv) in enumerate(units):
        if n + 2 < len(units):
            scores(n + 2)
        vt_kv = vt_all[kv * HEAD_DIM:(kv + 1) * HEAD_DIM, blk * WINDOW:(blk + 2) * WINDOW]
        variant = jnp.minimum(j, 1) if blk == 0 else 1
        for g in range(GQA_GROUP):
            h = kv * GQA_GROUP + g
            s = jnp.where(from_prev, s_ref[n % 3, 0:WINDOW, g * WINDOW:(g + 1) * WINDOW],
                          s_ref[n % 3, WINDOW:, g * WINDOW:(g + 1) * WINDOW]) + bias_ref[variant, h]
            sink = sink_ref[h]
            m = jnp.maximum(jnp.max(s, axis=0, keepdims=True), sink)
            p = jnp.exp2(s - m)
            denom = jnp.sum(p, axis=0, keepdims=True) + jnp.exp2(sink - m)
            p_keys = jnp.concatenate([jnp.where(from_prev, p, 0.0), jnp.where(from_prev, 0.0, p)], axis=0)
            o_t = jnp.dot(vt_kv, p_keys.astype(BF16), preferred_element_type=F32)
            outs.append(o_t / denom)
        if kv == N_KV_HEADS - 1:
            o_ref[blk * WINDOW:(blk + 1) * WINDOW, :] = jnp.transpose(jnp.concatenate(outs, axis=0)).astype(BF16)
            outs = []


def _attn_prompt(q, k, vt, bias_t, sinks, n_seq, seq):
    tq = ATTN_QB * WINDOW
    nj = seq // tq
    t = n_seq * seq
    cur = lambda b, j: (b * nj + j, 0)
    prev = lambda b, j: (jnp.maximum((b * nj + j) * ATTN_QB - 1, 0), 0)
    cur_t = lambda b, j: (0, b * nj + j)
    prev_t = lambda b, j: (0, jnp.maximum((b * nj + j) * ATTN_QB - 1, 0))
    return pl.pallas_call(
        _attn_kernel,
        grid=(n_seq, nj),
        in_specs=[
            pl.BlockSpec(memory_space=pltpu.SMEM),
            pl.BlockSpec((tq, Q_W), cur),
            pl.BlockSpec((WINDOW, KV_W), prev),
            pl.BlockSpec((tq, KV_W), cur),
            pl.BlockSpec((KV_W, WINDOW), prev_t),
            pl.BlockSpec((KV_W, tq), cur_t),
            pl.BlockSpec((2, N_HEADS, WINDOW, WINDOW), lambda b, j: (0, 0, 0, 0)),
        ],
        out_specs=pl.BlockSpec((tq, Q_W), cur),
        out_shape=jax.ShapeDtypeStruct((t, Q_W), BF16),
        scratch_shapes=[pltpu.VMEM((3, 2 * WINDOW, GQA_GROUP * WINDOW), F32)],
        compiler_params=pltpu.CompilerParams(
            dimension_semantics=("arbitrary", "arbitrary"), vmem_limit_bytes=VMEM_LIMIT),
        name="attn_prompt",
    )(sinks, q, k, k, vt, vt, bias_t)


def _prompt_bias_t():
    r = jnp.arange(WINDOW, dtype=jnp.int32)[None, :]
    c = jnp.arange(WINDOW, dtype=jnp.int32)[:, None]
    from_prev = c > r
    dist = r - c + jnp.where(from_prev, WINDOW, 0)
    slopes = jnp.exp2(-8.0 * jnp.arange(1, N_HEADS + 1, dtype=F32) / N_HEADS)
    later = -LOG2E * slopes[:, None, None] * dist.astype(F32)[None]
    first = jnp.where(from_prev[None], -jnp.inf, later)
    return jnp.stack([first, later])


def _sample_kernel(x_ref, g_ref, win_ref, qg_ref, kg_ref, bd_ref, wp_ref, ps_ref,
                   st_ref, ck_ref, cv_ref, sink_ref, bias_ref, perm_ref,
                   pool_ref, attn_ref, pst_ref, kc_ref, vc_ref, *, ns, pos0):
    u, qn, kn, v = _project(x_ref[...], g_ref[...], win_ref[...], qg_ref[...], kg_ref[...], bd_ref[...])
    pst_ref[:, 0:POOL_STATE - 1, :] = st_ref[:, 1:POOL_STATE, :]
    kc_ref[:, 0:WINDOW - 1, :] = ck_ref[:, 1:WINDOW, :]
    vc_ref[:, 0:WINDOW - 1, :] = cv_ref[:, 1:WINDOW, :]
    for n in range(ns):
        pst_ref[n, POOL_STATE - 1:POOL_STATE, :] = u[n:n + 1, :]
        kc_ref[n, WINDOW - 1:WINDOW, :] = kn[n:n + 1, :]
        vc_ref[n, WINDOW - 1:WINDOW, :] = v[n:n + 1, :]

    d_groups = []
    for gi, w in enumerate(POOL_WINDOWS):
        lo = gi * POOL_GC
        acc = u[:, lo:lo + POOL_GC]
        for back in range(1, w):
            acc = acc + st_ref[:, POOL_STATE - back, lo:lo + POOL_GC]
        d_groups.append(acc / float(min(pos0 + 1, w)) - u[:, lo:lo + POOL_GC])
    pool_ref[...] = _pool_project(d_groups, wp_ref, ps_ref[...]).astype(BF16)

    zeros = jnp.zeros((ns, HEAD_DIM), F32)
    stacked = []
    for h in range(N_HEADS):
        piece = qn[:, h * HEAD_DIM:(h + 1) * HEAD_DIM]
        pair = [piece, zeros] if h < GQA_GROUP else [zeros, piece]
        stacked.append(jnp.concatenate(pair, axis=-1))
    q_hn = jnp.concatenate(stacked, axis=0).astype(BF16)
    q_nh = jnp.dot(perm_ref[0], q_hn, preferred_element_type=F32).astype(BF16)

    keys = kc_ref[...].reshape(ns * WINDOW, KV_W).astype(BF16)
    vals = vc_ref[...].reshape(ns * WINDOW, KV_W).astype(BF16)
    s_all = lax.dot_general(q_nh, keys, (((1,), (1,)), ((), ())), preferred_element_type=F32)
    sink = sink_ref[...]
    bias = bias_ref[...]
    zero_blk = jnp.zeros((N_HEADS, WINDOW), F32)
    p_rows = []
    for n in range(ns):
        s = s_all[n * N_HEADS:(n + 1) * N_HEADS, n * WINDOW:(n + 1) * WINDOW] + bias
        m = jnp.maximum(jnp.max(s, axis=-1, keepdims=True), sink)
        p = jnp.exp2(s - m)
        denom = jnp.sum(p, axis=-1, keepdims=True) + jnp.exp2(sink - m)
        p_rows.append(jnp.concatenate([zero_blk] * n + [p / denom] + [zero_blk] * (ns - 1 - n), axis=-1))
    p_blockdiag = jnp.concatenate(p_rows, axis=0).astype(BF16)
    o_nh = jnp.dot(p_blockdiag, vals, preferred_element_type=F32).astype(BF16)
    o_hn = jnp.dot(perm_ref[1], o_nh, preferred_element_type=F32)
    pieces = []
    for h in range(N_HEADS):
        kv = h // GQA_GROUP
        pieces.append(o_hn[h * ns:(h + 1) * ns, kv * HEAD_DIM:(kv + 1) * HEAD_DIM])
    attn_ref[...] = jnp.concatenate(pieces, axis=-1).astype(BF16)


def _sample_mixer(l, depth, xs, g_attn, w_in, qg, kg, bd, wp, ps, state, ck, cv, sink8, bias_s, pos0):
    n = xs.shape[0]
    ns = 32
    row = lambda i: (i, 0)
    lay = lambda i: (l, 0, 0)
    src = jnp.arange(ns * N_HEADS)
    perm = (((src % N_HEADS) * ns + src // N_HEADS)[:, None] == src[None, :]).astype(BF16)
    perms = jnp.stack([perm, perm.T])
    return pl.pallas_call(
        functools.partial(_sample_kernel, ns=ns, pos0=pos0),
        grid=(n // ns,),
        input_output_aliases={8: 2, 9: 3, 10: 4},
        in_specs=[
            pl.BlockSpec((ns, D_MODEL), row),
            pl.BlockSpec((None, 1, D_MODEL), lay),
            pl.BlockSpec((None, D_MODEL, D_IN), lay),
            pl.BlockSpec((None, 1, Q_W), lay),
            pl.BlockSpec((None, 1, KV_W), lay),
            pl.BlockSpec((256, 256), lambda i: (0, 0)),
            pl.BlockSpec((None, 2, 256, 256), lambda i: (l, 0, 0, 0)),
            pl.BlockSpec((None, 1, POOL_W), lay),
            pl.BlockSpec((None, ns, POOL_STATE, POOL_W), lambda i: (l, i, 0, 0)),
            pl.BlockSpec((None, ns, WINDOW, KV_W), lambda i: (l, i, 0, 0)),
            pl.BlockSpec((None, ns, WINDOW, KV_W), lambda i: (l, i, 0, 0)),
            pl.BlockSpec((N_HEADS, 1), lambda i: (0, 0)),
            pl.BlockSpec((N_HEADS, WINDOW), lambda i: (0, 0)),
            pl.BlockSpec((2, ns * N_HEADS, ns * N_HEADS), lambda i: (0, 0, 0)),
        ],
        out_specs=[
            pl.BlockSpec((ns, POOL_W), row),
            pl.BlockSpec((ns, Q_W), row),
            pl.BlockSpec((None, ns, POOL_STATE, POOL_W), lambda i: (l, i, 0, 0)),
            pl.BlockSpec((None, ns, WINDOW, KV_W), lambda i: (l, i, 0, 0)),
            pl.BlockSpec((None, ns, WINDOW, KV_W), lambda i: (l, i, 0, 0)),
        ],
        out_shape=[
            jax.ShapeDtypeStruct((n, POOL_W), BF16),
            jax.ShapeDtypeStruct((n, Q_W), BF16),
            jax.ShapeDtypeStruct((depth, n, POOL_STATE, POOL_W), F32),
            jax.ShapeDtypeStruct((depth, n, WINDOW, KV_W), F32),
            jax.ShapeDtypeStruct((depth, n, WINDOW, KV_W), F32),
        ],
        compiler_params=pltpu.CompilerParams(
            dimension_semantics=("arbitrary",), vmem_limit_bytes=VMEM_LIMIT),
        name="sample_mixer",
    )(xs, g_attn, w_in, qg, kg, bd, wp, ps, state, ck, cv, sink8, bias_s, perms)


def _merge_router_kernel(pool_ref, attn_ref, x_ref, wout_ref, g_ref, wr_ref, br_ref, utri_ref, cin_ref,
                         x1_ref, h2_ref, route_t_ref, cnt_ref, y_ref, lg_ref):
    i = pl.program_id(0)

    @pl.when(i == 0)
    def _():
        cnt_ref[...] = cin_ref[...]

    tm = x_ref.shape[0]
    rc = tm // MERGE_CHUNKS
    chunks = [slice(ci * rc, (ci + 1) * rc) for ci in range(MERGE_CHUNKS)]
    for rows in chunks:
        y_ref[rows, :] = (jnp.dot(pool_ref[rows, :], wout_ref[0:POOL_W, :], preferred_element_type=F32)
                          + jnp.dot(attn_ref[rows, :], wout_ref[POOL_W:, :], preferred_element_type=F32))
    for rows in chunks:
        x1 = x_ref[rows, :] + y_ref[rows, :]
        x1_ref[rows, :] = x1
        h2 = _rms_bf16(x1, g_ref[...])
        h2_ref[rows, :] = _pack_bf16_pairs(h2)
        lg_ref[rows, :] = jnp.dot(h2, wr_ref[...], preferred_element_type=F32) + br_ref[...]
    logits = lg_ref[...]

    lt = jnp.transpose(logits)
    sub = lax.broadcasted_iota(jnp.int32, (EXPERTS_PER_GROUP, tm), 0)
    neg = -jnp.inf
    big = jnp.int32(EXPERTS_PER_GROUP)
    gl = jnp.where(sub < N_EXPERT_GROUPS, lt[GROUP_LANE0:GROUP_LANE0 + EXPERTS_PER_GROUP, :], neg)
    gmax = jnp.max(gl, axis=0, keepdims=True)
    grp = jnp.min(jnp.where(gl == gmax, sub, big), axis=0, keepdims=True)
    g_w = 1.0 / jnp.sum(jnp.exp(gl - gmax), axis=0, keepdims=True)
    el = lt[(N_EXPERT_GROUPS - 1) * EXPERTS_PER_GROUP:N_EXPERT_GROUPS * EXPERTS_PER_GROUP, :]
    for gi in range(N_EXPERT_GROUPS - 2, -1, -1):
        el = jnp.where(grp == gi, lt[gi * EXPERTS_PER_GROUP:(gi + 1) * EXPERTS_PER_GROUP, :], el)
    v1 = jnp.max(el, axis=0, keepdims=True)
    i1 = jnp.min(jnp.where(el == v1, sub, big), axis=0, keepdims=True)
    el2 = jnp.where(sub == i1, neg, el)
    v2 = jnp.max(el2, axis=0, keepdims=True)
    i2 = jnp.min(jnp.where(el2 == v2, sub, big), axis=0, keepdims=True)
    e21 = jnp.exp(v2 - v1)
    w1 = g_w / (1.0 + e21)
    w2 = g_w * e21 / (1.0 + e21)
    e1 = grp * EXPERTS_PER_GROUP + i1
    e2 = grp * EXPERTS_PER_GROUP + i2

    esub = lax.broadcasted_iota(jnp.int32, (N_EXPERTS, tm), 0)
    oh1 = esub == e1
    oh2 = esub == e2
    c = jnp.where(oh1 | oh2, 1.0, 0.0)
    prefix = jnp.dot(c.astype(BF16), utri_ref[...], preferred_element_type=F32) + cnt_ref[...]
    r1 = jnp.sum(jnp.where(oh1, prefix, 0.0), axis=0, keepdims=True)
    r2 = jnp.sum(jnp.where(oh2, prefix, 0.0), axis=0, keepdims=True)
    cnt_ref[...] = cnt_ref[...] + jnp.sum(c, axis=1, keepdims=True)

    fields = jnp.zeros((ROUTE_FIELDS, tm), F32)
    for idx, val in enumerate((e1.astype(F32), e2.astype(F32), w1, w2, r1, r2)):
        fields = jnp.where(sub == idx, val, fields)
    route_t_ref[...] = fields


def _merge_router(l, pool, attn, x2d, w_out, g_ffn, wr, br, cnt_in, tm):
    t = x2d.shape[0]
    utri = (jnp.arange(tm)[:, None] < jnp.arange(tm)[None, :]).astype(BF16)
    row = lambda i: (i, 0)
    lay = lambda i: (l, 0, 0)
    return pl.pallas_call(
        _merge_router_kernel,
        grid=(t // tm,),
        in_specs=[
            pl.BlockSpec((tm, POOL_W), row),
            pl.BlockSpec((tm, Q_W), row),
            pl.BlockSpec((tm, D_MODEL), row),
            pl.BlockSpec((None, D_MODEL, D_MODEL), lay),
            pl.BlockSpec((None, 1, D_MODEL), lay),
            pl.BlockSpec((None, D_MODEL, LANES), lay),
            pl.BlockSpec((None, 1, LANES), lay),
            pl.BlockSpec((tm, tm), lambda i: (0, 0)),
            pl.BlockSpec((N_EXPERTS, 1), lambda i: (0, 0)),
        ],
        out_specs=[
            pl.BlockSpec((tm, D_MODEL), row),
            pl.BlockSpec((tm, D_MODEL // 2), row),
            pl.BlockSpec((ROUTE_FIELDS, tm), lambda i: (0, i)),
            pl.BlockSpec((N_EXPERTS, 1), lambda i: (0, 0)),
        ],
        out_shape=[
            jax.ShapeDtypeStruct((t, D_MODEL), F32),
            jax.ShapeDtypeStruct((t, D_MODEL // 2), jnp.int32),
            jax.ShapeDtypeStruct((ROUTE_FIELDS, t), F32),
            jax.ShapeDtypeStruct((N_EXPERTS, 1), F32),
        ],
        scratch_shapes=[pltpu.VMEM((tm, D_MODEL), F32), pltpu.VMEM((tm, LANES), F32)],
        compiler_params=pltpu.CompilerParams(
            dimension_semantics=("arbitrary",), vmem_limit_bytes=VMEM_LIMIT),
        name="merge_router",
    )(pool, attn, x2d, w_out, g_ffn, wr, br, utri, cnt_in)


def _moe_kernel(be_ref, rv_ref, nx_ref, sl_ref, first_ref, xd_ref, wg_hbm, wu_hbm, wd_hbm, yd_ref,
                wg_f, wu_f, wd_f, wg_s, wu_s, wd_s, sem, *, layer):
    step = pl.program_id(0)

    def weight_copies(e, s):
        return [pltpu.make_async_copy(w_hbm.at[layer, e], w_f.at[s], sem.at[s, n])
                for n, (w_hbm, w_f) in enumerate(((wg_hbm, wg_f), (wu_hbm, wu_f), (wd_hbm, wd_f)))]

    @pl.when(step == 0)
    def _():
        for s in range(WEIGHT_SLOTS - 1):
            @pl.when(first_ref[s] >= 0)
            def _():
                for c in weight_copies(first_ref[s], s):
                    c.start()

    def enter_expert(i):
        expert, slot = be_ref[i], sl_ref[i]

        @pl.when((i == 0) | (expert != be_ref[jnp.maximum(i - 1, 0)]))
        def _():
            for c in weight_copies(expert, slot):
                c.wait()

            @pl.when(nx_ref[i] >= 0)
            def _():
                for c in weight_copies(nx_ref[i], lax.rem(slot + WEIGHT_SLOTS - 1, WEIGHT_SLOTS)):
                    c.start(priority=1)

            wg_s[...] = wg_f[slot].astype(BF16)
            wu_s[...] = wu_f[slot].astype(BF16)
            wd_s[...] = wd_f[slot].astype(BF16)

    def experts_on(row0, n_rows, rows_valid):
        rows = pl.ds(row0, n_rows)
        row = lax.broadcasted_iota(jnp.int32, (n_rows, D_MODEL // 2), 0)
        x = _unpack_bf16_pairs(jnp.where(row < rows_valid, xd_ref[rows, :], 0)).astype(BF16)
        gate = jnp.dot(x, wg_s[...], preferred_element_type=F32)
        up = jnp.dot(x, wu_s[...], preferred_element_type=F32)
        act = (gate * jax.nn.sigmoid(gate) * up).astype(BF16)
        y = jnp.dot(act, wd_s[...], preferred_element_type=F32)
        yd_ref[rows, :] = _pack_bf16_pairs(y.astype(BF16))

    def experts_ragged(row0, lead_rows, rows_last):
        half = MOE_BM // 2

        @pl.when(rows_last > half)
        def _():
            experts_on(row0, lead_rows + MOE_BM, lead_rows + rows_last)

        @pl.when(rows_last <= half)
        def _():
            experts_on(row0, lead_rows + half, lead_rows + rows_last)
            yd_ref[pl.ds(row0 + lead_rows + half, half), :] = jnp.zeros((half, D_MODEL // 2), jnp.int32)

    def single_block(i, row0):
        enter_expert(i)

        @pl.when(rv_ref[i] > 0)
        def _():
            experts_ragged(row0, 0, rv_ref[i])

        @pl.when(rv_ref[i] <= 0)
        def _():
            yd_ref[pl.ds(row0, MOE_BM), :] = jnp.zeros((MOE_BM, D_MODEL // 2), jnp.int32)

    @pl.when(rv_ref[step * MOE_STEP_BLOCKS] > 0)
    def _():
        for pair in range(MOE_STEP_BLOCKS // 2):
            ia = step * MOE_STEP_BLOCKS + 2 * pair
            ib = ia + 1
            row0 = 2 * pair * MOE_BM
            same = (be_ref[ib] == be_ref[ia]) & (rv_ref[ib] > 0)

            @pl.when(same)
            def _():
                enter_expert(ia)
                experts_ragged(row0, MOE_BM, rv_ref[ib])

            @pl.when(jnp.logical_not(same))
            def _():
                single_block(ia, row0)
                single_block(ib, row0 + MOE_BM)


def _moe_experts(l, block_e, rows_valid, next_e, slot, first_e, xd, w_gate, w_up, w_down):
    n_blocks = xd.shape[0] // MOE_BM
    step_rows = MOE_STEP_BLOCKS * MOE_BM
    row = lambda i, be, rv, nx, sl, fe: (jnp.minimum(i, fe[WEIGHT_SLOTS - 1] - 1), 0)
    return pl.pallas_call(
        functools.partial(_moe_kernel, layer=l),
        grid_spec=pltpu.PrefetchScalarGridSpec(
            num_scalar_prefetch=5,
            grid=(n_blocks // MOE_STEP_BLOCKS,),
            in_specs=[
                pl.BlockSpec((step_rows, D_MODEL // 2), row),
                pl.BlockSpec(memory_space=pl.ANY),
                pl.BlockSpec(memory_space=pl.ANY),
                pl.BlockSpec(memory_space=pl.ANY),
            ],
            out_specs=pl.BlockSpec((step_rows, D_MODEL // 2), row),
            scratch_shapes=[
                pltpu.VMEM((WEIGHT_SLOTS, D_MODEL, EXPERT_FF), F32),
                pltpu.VMEM((WEIGHT_SLOTS, D_MODEL, EXPERT_FF), F32),
                pltpu.VMEM((WEIGHT_SLOTS, EXPERT_FF, D_MODEL), F32),
                pltpu.VMEM((D_MODEL, EXPERT_FF), BF16),
                pltpu.VMEM((D_MODEL, EXPERT_FF), BF16),
                pltpu.VMEM((EXPERT_FF, D_MODEL), BF16),
                pltpu.SemaphoreType.DMA((WEIGHT_SLOTS, 3)),
            ],
        ),
        out_shape=jax.ShapeDtypeStruct((n_blocks * MOE_BM, D_MODEL // 2), jnp.int32),
        compiler_params=pltpu.CompilerParams(
            dimension_semantics=("arbitrary",), vmem_limit_bytes=VMEM_LIMIT),
        name="moe_experts",
    )(block_e, rows_valid, next_e, slot, first_e, xd, w_gate, w_up, w_down)


def _sc_worker_id():
    return lax.axis_index("s") * SC_CORES + lax.axis_index("c")


def _sc_dispatch(hp, hs, dest_p, dest_s, n_rows):
    tp, width = hp.shape
    per_w = tp // SC_WORKERS
    n_ch = per_w // DISP_CH
    n_sw = hs.shape[0] // SAMPLE_CH
    mesh = plsc.VectorSubcoreMesh(core_axis_name="c", subcore_axis_name="s")

    @functools.partial(
        pl.kernel, mesh=mesh,
        out_type=jax.ShapeDtypeStruct((n_rows, width), jnp.int32),
        scratch_types=[
            pltpu.VMEM((2, n_ch, DISP_CH), jnp.int32),
            pltpu.VMEM((2, 1, SAMPLE_CH), jnp.int32),
            pltpu.VMEM((SC_RING, DISP_CH, width), jnp.int32),
            pltpu.SemaphoreType.DMA((SC_RING,)),
            pltpu.SemaphoreType.DMA((SC_RING, 2)),
        ],
        name="sc_dispatch",
    )
    def k(hp_hbm, hs_hbm, dp_hbm, ds_hbm, xd_hbm, idx_v, idxs_v, bufs, rsem, wsem):
        wid = _sc_worker_id()
        base = wid * per_w
        for kk in range(2):
            pltpu.sync_copy(dp_hbm.at[kk, wid], idx_v.at[kk])
        reads = [pltpu.make_async_copy(hp_hbm.at[pl.ds(base + j * DISP_CH, DISP_CH)],
                                       bufs.at[j % SC_RING], rsem.at[j % SC_RING]) for j in range(n_ch)]
        writes = [[pltpu.make_async_copy(bufs.at[j % SC_RING], xd_hbm.at[idx_v.at[kk, j]],
                                         wsem.at[j % SC_RING, kk]) for kk in range(2)] for j in range(n_ch)]
        for j in range(min(SC_RING - 1, n_ch)):
            reads[j].start()
        for j in range(n_ch):
            reads[j].wait()
            for w in writes[j]:
                w.start()
            if j >= 1:
                for w in writes[j - 1]:
                    w.wait()
            if j + SC_RING - 1 < n_ch:
                reads[j + SC_RING - 1].start()
        for w in writes[n_ch - 1]:
            w.wait()

        @pl.when(wid < n_sw)
        def _():
            rows = bufs.at[0, pl.ds(0, SAMPLE_CH)]
            for kk in range(2):
                pltpu.sync_copy(ds_hbm.at[kk, wid], idxs_v.at[kk])
            pltpu.sync_copy(hs_hbm.at[pl.ds(wid * SAMPLE_CH, SAMPLE_CH)], rows)
            for kk in range(2):
                pltpu.sync_copy(rows, xd_hbm.at[idxs_v.at[kk, 0]])

    return k(hp, hs, dest_p, dest_s)


def _sc_sample_gather(yd, dest_s, ts):
    width = yd.shape[1]
    n_sw = ts // SAMPLE_CH
    mesh = plsc.VectorSubcoreMesh(core_axis_name="c", subcore_axis_name="s")

    @functools.partial(
        pl.kernel, mesh=mesh,
        out_type=jax.ShapeDtypeStruct((2, ts, width), yd.dtype),
        scratch_types=[
            pltpu.VMEM((2, 1, SAMPLE_CH), jnp.int32),
            pltpu.VMEM((2, SAMPLE_CH, width), yd.dtype),
        ],
        name="sc_sample_gather",
    )
    def k(yd_hbm, ds_hbm, g_hbm, idxs_v, bufs):
        wid = _sc_worker_id()

        @pl.when(wid < n_sw)
        def _():
            for kk in range(2):
                pltpu.sync_copy(ds_hbm.at[kk, wid], idxs_v.at[kk])
            for kk in range(2):
                pltpu.sync_copy(yd_hbm.at[idxs_v.at[kk, 0]], bufs.at[kk])
                pltpu.sync_copy(bufs.at[kk], g_hbm.at[kk, pl.ds(wid * SAMPLE_CH, SAMPLE_CH)])

    return k(yd, dest_s)


def _sc_combine_gather(yd, dest_p, tp):
    width = yd.shape[1]
    per_w = tp // SC_WORKERS
    n_ch = per_w // COMB_CH
    mesh = plsc.VectorSubcoreMesh(core_axis_name="c", subcore_axis_name="s")

    @functools.partial(
        pl.kernel, mesh=mesh,
        out_type=jax.ShapeDtypeStruct((2, tp, width), yd.dtype),
        scratch_types=[
            pltpu.VMEM((2, n_ch, COMB_CH), jnp.int32),
            pltpu.VMEM((SC_RING, COMB_CH, width), yd.dtype),
            pltpu.SemaphoreType.DMA((SC_RING,)),
            pltpu.SemaphoreType.DMA((SC_RING,)),
        ],
        name="sc_combine_gather",
    )
    def k(yd_hbm, dp_hbm, g_hbm, idx_v, bufs, gsem, wsem):
        wid = _sc_worker_id()
        base = wid * per_w
        for kk in range(2):
            pltpu.sync_copy(dp_hbm.at[kk, wid], idx_v.at[kk])
        items = [(kk, j) for kk in range(2) for j in range(n_ch)]
        n_items = len(items)
        gathers = [pltpu.make_async_copy(yd_hbm.at[idx_v.at[kk, j]], bufs.at[n % SC_RING], gsem.at[n % SC_RING])
                   for n, (kk, j) in enumerate(items)]
        outs = [pltpu.make_async_copy(bufs.at[n % SC_RING], g_hbm.at[kk, pl.ds(base + j * COMB_CH, COMB_CH)],
                                      wsem.at[n % SC_RING]) for n, (kk, j) in enumerate(items)]
        for n in range(min(SC_RING - 1, n_items)):
            gathers[n].start()
        for n in range(n_items):
            gathers[n].wait()
            outs[n].start()
            if n >= 1:
                outs[n - 1].wait()
            if n + SC_RING - 1 < n_items:
                gathers[n + SC_RING - 1].start()
        outs[n_items - 1].wait()

    return k(yd, dest_p)


def _combine_kernel(x1_ref, g_ref, route_t_ref, x2_ref):
    fields = route_t_ref[...]
    tm = fields.shape[1]
    cols = jnp.transpose(jnp.concatenate([fields, jnp.zeros((LANES - ROUTE_FIELDS, tm), F32)], axis=0))
    w1 = cols[:, 2:3]
    w2 = cols[:, 3:4]
    x2_ref[...] = x1_ref[...] + _unpack_bf16_pairs(g_ref[0]) * w1 + _unpack_bf16_pairs(g_ref[1]) * w2


def _combine(x1, g, route_t, row0, tm):
    t = x1.shape[0]
    blk0 = row0 // tm
    row = lambda i: (i, 0)
    return pl.pallas_call(
        _combine_kernel,
        grid=(t // tm,),
        in_specs=[
            pl.BlockSpec((tm, D_MODEL), row),
            pl.BlockSpec((2, tm, D_MODEL // 2), lambda i: (0, blk0 + i, 0)),
            pl.BlockSpec((ROUTE_FIELDS, tm), lambda i: (0, i)),
        ],
        out_specs=pl.BlockSpec((tm, D_MODEL), row),
        out_shape=jax.ShapeDtypeStruct((t, D_MODEL), F32),
        compiler_params=pltpu.CompilerParams(
            dimension_semantics=("arbitrary",), vmem_limit_bytes=VMEM_LIMIT),
        name="combine",
    )(x1, g, route_t)


def _dest_layout(dest, workers, chunk):
    t = dest.shape[1]
    return dest.reshape(2, workers, t // (workers * chunk), chunk)


def _hier_moe(l, h2p, h2s, route_tp, route_ts, counts, w_gate, w_up, w_down):
    tp, ts = h2p.shape[0], h2s.shape[0]
    n_assign = 2 * (tp + ts)
    n_blocks = -(-n_assign // MOE_BM) + N_EXPERTS
    n_blocks = -(-n_blocks // MOE_STEP_BLOCKS) * MOE_STEP_BLOCKS
    pcounts = (counts + MOE_BM - 1) // MOE_BM * MOE_BM
    pends = jnp.cumsum(pcounts)
    poffsets = pends - pcounts
    starts = jnp.arange(n_blocks, dtype=jnp.int32) * MOE_BM
    block_e = jnp.minimum(jnp.sum((pends[None, :] <= starts[:, None]).astype(jnp.int32), axis=1),
                          N_EXPERTS - 1)
    experts = jnp.arange(N_EXPERTS, dtype=jnp.int32)

    def lookup(table, idx):
        return jnp.sum(jnp.where(idx[..., None] == experts, table, 0), axis=-1)

    rows_valid = jnp.clip(lookup(poffsets + counts, block_e) - starts, 0, MOE_BM).astype(jnp.int32)
    used = counts > 0
    last_e = jnp.max(jnp.where(used, jnp.arange(N_EXPERTS, dtype=jnp.int32), 0))
    block_e = jnp.where(rows_valid > 0, block_e, last_e).astype(jnp.int32)
    place = jnp.cumsum(used.astype(jnp.int32)) - 1
    by_place = jnp.sum(jnp.where(used[None, :] & (place[None, :] == experts[:, None]), experts[None, :], 0),
                       axis=1)
    n_used = jnp.sum(used.astype(jnp.int32))

    def at_place(p):
        return jnp.where(p < n_used, lookup(by_place, jnp.minimum(p, N_EXPERTS - 1)), -1).astype(jnp.int32)

    ahead_of = at_place(place + (WEIGHT_SLOTS - 1))
    next_e = lookup(ahead_of, block_e)
    slot = lookup(place % WEIGHT_SLOTS, block_e)
    n_steps_used = -(-(pends[-1] // MOE_BM) // MOE_STEP_BLOCKS)
    first_e = jnp.concatenate([at_place(jnp.arange(WEIGHT_SLOTS - 1, dtype=jnp.int32)),
                               n_steps_used.reshape(1).astype(jnp.int32)])

    def dest_of(route_t):
        return lookup(poffsets, route_t[0:2].astype(jnp.int32)) + route_t[4:6].astype(jnp.int32)

    dest_p, dest_s = dest_of(route_tp), dest_of(route_ts)
    n_sw = ts // SAMPLE_CH
    xd = _sc_dispatch(h2p, h2s, _dest_layout(dest_p, SC_WORKERS, DISP_CH),
                      _dest_layout(dest_s, n_sw, SAMPLE_CH), n_blocks * MOE_BM)
    yd = _moe_experts(l, block_e, rows_valid, next_e, slot, first_e, xd, w_gate, w_up, w_down)
    g_s = _sc_sample_gather(yd, _dest_layout(dest_s, n_sw, SAMPLE_CH), ts)
    g_p = _sc_combine_gather(yd, _dest_layout(dest_p, SC_WORKERS, COMB_CH), tp)
    return g_p, g_s


def kernel(x_prompt, x_sample, state_pool, cache_k_win, cache_v_win, norm_attn_g, w_in, pool_w, pool_scale, q_norm_g, k_norm_g, attn_sinks, w_out, norm_ffn_g, router_group_w, router_group_b, router_expert_w, router_expert_b, w_gate, w_up, w_down):
    n_p, t_p, d = x_prompt.shape
    n_s, t_s, _ = x_sample.shape
    depth = w_in.shape[0]
    lw_s = cache_k_win.shape[2]
    assert t_s == 1 and lw_s == WINDOW and d == D_MODEL
    assert t_p % TM_PROJ == 0 and t_p >= WINDOW

    seg = jnp.arange(256) // HEAD_DIM
    bd = jnp.where(seg[:, None] == seg[None, :], 1.0 / HEAD_DIM, 0.0).astype(BF16)
    slopes = jnp.exp2(-8.0 * jnp.arange(1, N_HEADS + 1, dtype=F32) / N_HEADS)
    bias_p = _prompt_bias_t()
    dist_s = (WINDOW - 1) - jnp.arange(WINDOW, dtype=F32)
    bias_s = -LOG2E * slopes[:, None] * dist_s[None, :]

    wp = jnp.zeros((depth, 2, 256, 256), F32)
    for p in range(2):
        wp = wp.at[:, p, :POOL_GC, :POOL_GC].set(pool_w[:, 2 * p])
        wp = wp.at[:, p, POOL_GC:, POOL_GC:].set(pool_w[:, 2 * p + 1])
    assert GROUP_LANE0 == N_EXPERTS
    lane_pad = LANES - N_EXPERTS - N_EXPERT_GROUPS
    wr = jnp.concatenate([router_expert_w, router_group_w, jnp.zeros((depth, D_MODEL, lane_pad), F32)], axis=-1)
    br = jnp.concatenate([router_expert_b, router_group_b, jnp.zeros((depth, lane_pad), F32)],
                         axis=-1).reshape(depth, 1, LANES)
    lp = dict(
        w_in=w_in.astype(BF16),
        w_out=w_out.astype(BF16),
        g_attn=norm_attn_g.reshape(depth, 1, D_MODEL),
        g_ffn=norm_ffn_g.reshape(depth, 1, D_MODEL),
        qg=(jnp.tile(q_norm_g, (1, N_HEADS)) * (ATTN_SCALE * LOG2E)).reshape(depth, 1, Q_W),
        kg=jnp.tile(k_norm_g, (1, N_KV_HEADS)).reshape(depth, 1, KV_W),
        wp=wp.astype(BF16),
        ps=pool_scale.reshape(depth, 1, POOL_W),
        wr=wr.astype(BF16),
        br=br,
        state=state_pool,
        ck=cache_k_win.reshape(depth, n_s, lw_s, KV_W),
        cv=cache_v_win.reshape(depth, n_s, lw_s, KV_W),
    )

    xp = x_prompt.reshape(n_p * t_p, D_MODEL)
    xs = x_sample.reshape(n_s, D_MODEL)
    lw_p = min(WINDOW, t_p)
    pool_p, kp_new, vp_new = [], [], []
    sample_state = [lp["state"], lp["ck"], lp["cv"]]
    zero_cnt = jnp.zeros((N_EXPERTS, 1), F32)
    pending = None
    for l in range(depth):
        sinks = attn_sinks[l] * LOG2E
        pool_so, attn_so, *sample_state = _sample_mixer(
            l, depth, xs, lp["g_attn"], lp["w_in"], lp["qg"], lp["kg"], bd, lp["wp"], lp["ps"],
            *sample_state, sinks.reshape(N_HEADS, 1), bias_s, PAST_LEN)
        x1s, h2s, route_ts, cnt_s = _merge_router(
            l, pool_so, attn_so, xs, lp["w_out"], lp["g_ffn"], lp["wr"], lp["br"], zero_cnt, n_s)
        outs = _proj_pool_prompt(
            l, xp if pending is None else pending, n_p, t_p,
            lp["g_attn"], lp["w_in"], lp["qg"], lp["kg"], bd, lp["wp"], lp["ps"])
        if pending is not None:
            xp, outs = outs[0], outs[1:]
        pool_o, q, k, vt, utail, ktail, vtail = outs
        attn_o = _attn_prompt(q, k, vt, bias_p, sinks, n_p, t_p)
        x1p, h2p, route_tp, cnt_all = _merge_router(
            l, pool_o, attn_o, xp, lp["w_out"], lp["g_ffn"], lp["wr"], lp["br"], cnt_s, TM_MERGE)
        pool_p.append(utail[:, 16 - POOL_STATE:, :])
        kp_new.append(ktail)
        vp_new.append(vtail)
        counts = cnt_all[:, 0].astype(jnp.int32)
        g_p, g_s = _hier_moe(l, h2p, h2s, route_tp, route_ts, counts, w_gate, w_up, w_down)
        xs = _combine(x1s, g_s, route_ts, 0, n_s)
        pending = (x1p, g_p, route_tp)
    xp = _combine(*pending, 0, TM_MERGE)
    return (xp.reshape(n_p, t_p, D_MODEL), xs.reshape(n_s, t_s, D_MODEL),
            jnp.stack(pool_p),
            jnp.stack(kp_new).reshape(depth, n_p, lw_p, N_KV_HEADS, HEAD_DIM),
            jnp.stack(vp_new).reshape(depth, n_p, lw_p, N_KV_HEADS, HEAD_DIM),
            sample_state[0],
            sample_state[1].reshape(depth, n_s, lw_s, N_KV_HEADS, HEAD_DIM),
            sample_state[2].reshape(depth, n_s, lw_s, N_KV_HEADS, HEAD_DIM))
```

```python
import functools

import jax
import jax.numpy as jnp
from jax import lax
from jax.experimental import pallas as pl
from jax.experimental.pallas import tpu as pltpu
from jax.experimental.pallas import tpu_sc as plsc

D_MODEL = 1024
POOL_W = 512
POOL_WINDOWS = (2, 4, 8, 16)
POOL_GC = 128
POOL_STATE = 15
HEAD_DIM = 64
N_HEADS = 8
N_KV_HEADS = 2
GQA_GROUP = 4
Q_W = 512
KV_W = 128
D_IN = POOL_W + Q_W + 2 * KV_W
WINDOW = 128
ATTN_SCALE = HEAD_DIM ** -0.5
LOG2E = 1.4426950408889634
N_EXPERT_GROUPS = 4
EXPERTS_PER_GROUP = 8
N_EXPERTS = 32
EXPERT_FF = 512
EPS = 1e-6
PAST_LEN = 16384

LANES = 128
HALO = 32
TM_PROJ = 1024
TM_MERGE = 1024
MERGE_CHUNKS = 4
ATTN_QB = 16
MOE_BM = 256
MOE_STEP_BLOCKS = 4
WEIGHT_SLOTS = 3
GROUP_LANE0 = 32
ROUTE_FIELDS = 8
SC_CORES = 2
SC_SUBCORES = 16
SC_WORKERS = SC_CORES * SC_SUBCORES
DISP_CH = 64
COMB_CH = 64
SAMPLE_CH = 32
SC_RING = 3
FINAL_GATHER_PARTS = 2
VMEM_LIMIT = 48 * 1024 * 1024

BF16 = jnp.bfloat16
F32 = jnp.float32


def _pack_bf16_pairs(h):
    w = h.shape[1] // 2
    hi = lax.bitcast_convert_type(h[:, :w].astype(F32), jnp.uint32)
    lo = lax.bitcast_convert_type(h[:, w:].astype(F32), jnp.uint32)
    return lax.bitcast_convert_type(hi | (lo >> 16), jnp.int32)


def _unpack_bf16_pairs(words):
    u = lax.bitcast_convert_type(words, jnp.uint32)
    hi = lax.bitcast_convert_type(u & jnp.uint32(0xFFFF0000), F32)
    lo = lax.bitcast_convert_type(u << 16, F32)
    return jnp.concatenate([hi, lo], axis=-1)


def _segment_mean_sq(a, bd):
    w = a.shape[1]
    return jnp.dot((a * a).astype(BF16), bd[:w, :w], preferred_element_type=F32)


def _rms_bf16(x, g):
    ms = jnp.mean(x * x, axis=-1, keepdims=True)
    return (x * lax.rsqrt(ms + EPS) * g).astype(BF16)


def _qk_norm(q, k, qg, kg, bd):
    qn = []
    for c in range(Q_W // 256):
        qc = q[:, c * 256:(c + 1) * 256]
        qn.append(qc * lax.rsqrt(_segment_mean_sq(qc, bd) + EPS))
    qn = jnp.concatenate(qn, axis=-1) * qg
    kn = k * lax.rsqrt(_segment_mean_sq(k, bd) + EPS) * kg
    return qn, kn


def _project(x, g, w_in, qg, kg, bd):
    z = jnp.dot(_rms_bf16(x, g), w_in, preferred_element_type=F32)
    u = z[:, :POOL_W]
    q = z[:, POOL_W:POOL_W + Q_W]
    k = z[:, POOL_W + Q_W:POOL_W + Q_W + KV_W]
    v = z[:, POOL_W + Q_W + KV_W:]
    qn, kn = _qk_norm(q, k, qg, kg, bd)
    return u, qn, kn, v


def _pool_project(d_groups, wp_ref, ps):
    outs = []
    for p in range(2):
        dp = jnp.concatenate([d_groups[2 * p], d_groups[2 * p + 1]], axis=-1).astype(BF16)
        y = jnp.dot(dp, wp_ref[p], preferred_element_type=F32)
        outs.append(y * ps[:, p * 256:(p + 1) * 256])
    return jnp.concatenate(outs, axis=-1)


def _proj_pool_kernel(x_ref, g_ref, win_ref, qg_ref, kg_ref, bd_ref, wp_ref, ps_ref,
                      pool_ref, q_ref, k_ref, vt_ref, utail_ref, ktail_ref, vtail_ref,
                      ext_ref, sa_ref, sb_ref, zq_ref, *, tm, n_j):
    j = pl.program_id(1)

    @pl.when(j == 0)
    def _():
        ext_ref[0:HALO, :] = jnp.zeros((HALO, POOL_W), F32)

    r = tm + HALO
    h = _rms_bf16(x_ref[...], g_ref[...])
    ext_ref[HALO:r, :] = jnp.dot(h, win_ref[:, 0:POOL_W], preferred_element_type=F32)
    zq_ref[...] = jnp.dot(h, win_ref[:, POOL_W:], preferred_element_type=F32)
    u = ext_ref[HALO:r, :]
    sa_ref[8:r, :] = ext_ref[8:r, :] + ext_ref[7:r - 1, :]
    sb_ref[16:r, 128:] = sa_ref[16:r, 128:] + sa_ref[14:r - 2, 128:]
    sa_ref[24:r, 256:] = sb_ref[24:r, 256:] + sb_ref[20:r - 4, 256:]
    sb_ref[32:r, 384:] = sa_ref[32:r, 384:] + sa_ref[24:r - 8, 384:]
    pos1 = j * tm + lax.broadcasted_iota(jnp.int32, (tm, POOL_GC), 0) + 1
    sums = (sa_ref, sb_ref, sa_ref, sb_ref)
    d_groups = []
    for gi, w in enumerate(POOL_WINDOWS):
        sl = slice(gi * POOL_GC, (gi + 1) * POOL_GC)
        cnt = jnp.minimum(pos1, w).astype(F32)
        d_groups.append(sums[gi][HALO:r, sl] / cnt - u[:, sl])
    pool_ref[...] = _pool_project(d_groups, wp_ref, ps_ref[...]).astype(BF16)
    ext_ref[16:HALO, :] = ext_ref[tm + 16:r, :]

    qn, kn = _qk_norm(zq_ref[:, 0:Q_W], zq_ref[:, Q_W:Q_W + KV_W], qg_ref[...], kg_ref[...], bd_ref[...])
    v = zq_ref[:, Q_W + KV_W:]
    q_ref[...] = qn.astype(BF16)
    k_ref[...] = kn.astype(BF16)
    vt_ref[...] = jnp.transpose(v).astype(BF16)

    @pl.when(j == n_j - 1)
    def _():
        utail_ref[...] = u[tm - 16:, :]
        ktail_ref[...] = kn[tm - WINDOW:, :]
        vtail_ref[...] = v[tm - WINDOW:, :]


def _proj_pool_combine_kernel(x1_ref, gath_ref, route_ref, *rest, tm, n_j):
    x2_ref = rest[7]
    _combine_kernel(x1_ref, gath_ref, route_ref, x2_ref)
    _proj_pool_kernel(x2_ref, *rest[:7], *rest[8:], tm=tm, n_j=n_j)


def _proj_pool_prompt(l, x_in, n_seq, seq, g_attn, w_in, qg, kg, bd, wp, ps):
    tm = TM_PROJ
    n_j = seq // tm
    t = n_seq * seq
    row = lambda b, j: (b * n_j + j, 0)
    lay = lambda b, j: (l, 0, 0)
    fused = isinstance(x_in, tuple)
    if fused:
        kern = _proj_pool_combine_kernel
        x_args = list(x_in)
        x_specs = [pl.BlockSpec((tm, D_MODEL), row),
                   pl.BlockSpec((2, tm, D_MODEL // 2), lambda b, j: (0, b * n_j + j, 0)),
                   pl.BlockSpec((ROUTE_FIELDS, tm), lambda b, j: (0, b * n_j + j))]
        x_out_specs = [pl.BlockSpec((tm, D_MODEL), row)]
        x_out_shape = [jax.ShapeDtypeStruct((t, D_MODEL), F32)]
    else:
        kern = _proj_pool_kernel
        x_args = [x_in]
        x_specs = [pl.BlockSpec((tm, D_MODEL), row)]
        x_out_specs, x_out_shape = [], []
    return pl.pallas_call(
        functools.partial(kern, tm=tm, n_j=n_j),
        grid=(n_seq, n_j),
        in_specs=x_specs + [
            pl.BlockSpec((None, 1, D_MODEL), lay),
            pl.BlockSpec((None, D_MODEL, D_IN), lay),
            pl.BlockSpec((None, 1, Q_W), lay),
            pl.BlockSpec((None, 1, KV_W), lay),
            pl.BlockSpec((256, 256), lambda b, j: (0, 0)),
            pl.BlockSpec((None, 2, 256, 256), lambda b, j: (l, 0, 0, 0)),
            pl.BlockSpec((None, 1, POOL_W), lay),
        ],
        out_specs=x_out_specs + [
            pl.BlockSpec((tm, POOL_W), row),
            pl.BlockSpec((tm, Q_W), row),
            pl.BlockSpec((tm, KV_W), row),
            pl.BlockSpec((KV_W, tm), lambda b, j: (0, b * n_j + j)),
            pl.BlockSpec((None, 16, POOL_W), lambda b, j: (b, 0, 0)),
            pl.BlockSpec((None, WINDOW, KV_W), lambda b, j: (b, 0, 0)),
            pl.BlockSpec((None, WINDOW, KV_W), lambda b, j: (b, 0, 0)),
        ],
        out_shape=x_out_shape + [
            jax.ShapeDtypeStruct((t, POOL_W), BF16),
            jax.ShapeDtypeStruct((t, Q_W), BF16),
            jax.ShapeDtypeStruct((t, KV_W), BF16),
            jax.ShapeDtypeStruct((KV_W, t), BF16),
            jax.ShapeDtypeStruct((n_seq, 16, POOL_W), F32),
            jax.ShapeDtypeStruct((n_seq, WINDOW, KV_W), F32),
            jax.ShapeDtypeStruct((n_seq, WINDOW, KV_W), F32),
        ],
        scratch_shapes=[pltpu.VMEM((tm + HALO, POOL_W), F32)] * 3 + [pltpu.VMEM((tm, Q_W + 2 * KV_W), F32)],
        compiler_params=pltpu.CompilerParams(
            dimension_semantics=("arbitrary", "arbitrary"), vmem_limit_bytes=VMEM_LIMIT),
        name="proj_pool_prompt",
    )(*x_args, g_attn, w_in, qg, kg, bd, wp, ps)


def _attn_kernel(sink_ref, q_ref, kp_ref, kc_ref, vtp_ref, vtc_ref, bias_ref, o_ref, s_ref):
    j = pl.program_id(1)
    kk_all = jnp.concatenate([kp_ref[...], kc_ref[...]], axis=0)
    vt_all = jnp.concatenate([vtp_ref[...], vtc_ref[...]], axis=1)
    from_prev = (lax.broadcasted_iota(jnp.int32, (WINDOW, WINDOW), 0)
                 > lax.broadcasted_iota(jnp.int32, (WINDOW, WINDOW), 1))
    units = [(blk, kv) for blk in range(ATTN_QB) for kv in range(N_KV_HEADS)]

    def scores(n):
        blk, kv = units[n]
        q = q_ref[blk * WINDOW:(blk + 1) * WINDOW, :]
        kk = kk_all[blk * WINDOW:(blk + 2) * WINDOW, kv * HEAD_DIM:(kv + 1) * HEAD_DIM]
        heads = range(kv * GQA_GROUP, (kv + 1) * GQA_GROUP)
        q_rows = jnp.concatenate([q[:, h * HEAD_DIM:(h + 1) * HEAD_DIM] for h in heads], axis=0)
        s_ref[n % 3] = lax.dot_general(kk, q_rows, (((1,), (1,)), ((), ())), preferred_element_type=F32)

    scores(0)
    scores(1)
    outs = []
    for n, (blk, kv) in enumerate(units):
        if n + 2 < len(units):
            scores(n + 2)
        vt_kv = vt_all[kv * HEAD_DIM:(kv + 1) * HEAD_DIM, blk * WINDOW:(blk + 2) * WINDOW]
        variant = jnp.minimum(j, 1) if blk == 0 else 1
        for g in range(GQA_GROUP):
            h = kv * GQA_GROUP + g
            s = jnp.where(from_prev, s_ref[n % 3, 0:WINDOW, g * WINDOW:(g + 1) * WINDOW],
                          s_ref[n % 3, WINDOW:, g * WINDOW:(g + 1) * WINDOW]) + bias_ref[variant, h]
            sink = sink_ref[h]
            m = jnp.maximum(jnp.max(s, axis=0, keepdims=True), sink)
            p = jnp.exp2(s - m)
            denom = jnp.sum(p, axis=0, keepdims=True) + jnp.exp2(sink - m)
            p_keys = jnp.concatenate([jnp.where(from_prev, p, 0.0), jnp.where(from_prev, 0.0, p)], axis=0)
            o_t = jnp.dot(vt_kv, p_keys.astype(BF16), preferred_element_type=F32)
            outs.append(o_t / denom)
        if kv == N_KV_HEADS - 1:
            o_ref[blk * WINDOW:(blk + 1) * WINDOW, :] = jnp.transpose(jnp.concatenate(outs, axis=0)).astype(BF16)
            outs = []


def _attn_prompt(q, k, vt, bias_t, sinks, n_seq, seq):
    tq = ATTN_QB * WINDOW
    nj = seq // tq
    t = n_seq * seq
    cur = lambda b, j: (b * nj + j, 0)
    prev = lambda b, j: (jnp.maximum((b * nj + j) * ATTN_QB - 1, 0), 0)
    cur_t = lambda b, j: (0, b * nj + j)
    prev_t = lambda b, j: (0, jnp.maximum((b * nj + j) * ATTN_QB - 1, 0))
    return pl.pallas_call(
        _attn_kernel,
        grid=(n_seq, nj),
        in_specs=[
            pl.BlockSpec(memory_space=pltpu.SMEM),
            pl.BlockSpec((tq, Q_W), cur),
            pl.BlockSpec((WINDOW, KV_W), prev),
            pl.BlockSpec((tq, KV_W), cur),
            pl.BlockSpec((KV_W, WINDOW), prev_t),
            pl.BlockSpec((KV_W, tq), cur_t),
            pl.BlockSpec((2, N_HEADS, WINDOW, WINDOW), lambda b, j: (0, 0, 0, 0)),
        ],
        out_specs=pl.BlockSpec((tq, Q_W), cur),
        out_shape=jax.ShapeDtypeStruct((t, Q_W), BF16),
        scratch_shapes=[pltpu.VMEM((3, 2 * WINDOW, GQA_GROUP * WINDOW), F32)],
        compiler_params=pltpu.CompilerParams(
            dimension_semantics=("arbitrary", "arbitrary"), vmem_limit_bytes=VMEM_LIMIT),
        name="attn_prompt",
    )(sinks, q, k, k, vt, vt, bias_t)


def _prompt_bias_t():
    r = jnp.arange(WINDOW, dtype=jnp.int32)[None, :]
    c = jnp.arange(WINDOW, dtype=jnp.int32)[:, None]
    from_prev = c > r
    dist = r - c + jnp.where(from_prev, WINDOW, 0)
    slopes = jnp.exp2(-8.0 * jnp.arange(1, N_HEADS + 1, dtype=F32) / N_HEADS)
    later = -LOG2E * slopes[:, None, None] * dist.astype(F32)[None]
    first = jnp.where(from_prev[None], -jnp.inf, later)
    return jnp.stack([first, later])


def _sample_kernel(x_ref, g_ref, win_ref, qg_ref, kg_ref, bd_ref, wp_ref, ps_ref,
                   st_ref, ck_ref, cv_ref, sink_ref, bias_ref, perm_ref,
                   pool_ref, attn_ref, pst_ref, kc_ref, vc_ref, *, ns, pos0):
    u, qn, kn, v = _project(x_ref[...], g_ref[...], win_ref[...], qg_ref[...], kg_ref[...], bd_ref[...])
    pst_ref[:, 0:POOL_STATE - 1, :] = st_ref[:, 1:POOL_STATE, :]
    kc_ref[:, 0:WINDOW - 1, :] = ck_ref[:, 1:WINDOW, :]
    vc_ref[:, 0:WINDOW - 1, :] = cv_ref[:, 1:WINDOW, :]
    for n in range(ns):
        pst_ref[n, POOL_STATE - 1:POOL_STATE, :] = u[n:n + 1, :]
        kc_ref[n, WINDOW - 1:WINDOW, :] = kn[n:n + 1, :]
        vc_ref[n, WINDOW - 1:WINDOW, :] = v[n:n + 1, :]

    d_groups = []
    for gi, w in enumerate(POOL_WINDOWS):
        lo = gi * POOL_GC
        acc = u[:, lo:lo + POOL_GC]
        for back in range(1, w):
            acc = acc + st_ref[:, POOL_STATE - back, lo:lo + POOL_GC]
        d_groups.append(acc / float(min(pos0 + 1, w)) - u[:, lo:lo + POOL_GC])
    pool_ref[...] = _pool_project(d_groups, wp_ref, ps_ref[...]).astype(BF16)

    zeros = jnp.zeros((ns, HEAD_DIM), F32)
    stacked = []
    for h in range(N_HEADS):
        piece = qn[:, h * HEAD_DIM:(h + 1) * HEAD_DIM]
        pair = [piece, zeros] if h < GQA_GROUP else [zeros, piece]
        stacked.append(jnp.concatenate(pair, axis=-1))
    q_hn = jnp.concatenate(stacked, axis=0).astype(BF16)
    q_nh = jnp.dot(perm_ref[0], q_hn, preferred_element_type=F32).astype(BF16)

    keys = kc_ref[...].reshape(ns * WINDOW, KV_W).astype(BF16)
    vals = vc_ref[...].reshape(ns * WINDOW, KV_W).astype(BF16)
    s_all = lax.dot_general(q_nh, keys, (((1,), (1,)), ((), ())), preferred_element_type=F32)
    sink = sink_ref[...]
    bias = bias_ref[...]
    zero_blk = jnp.zeros((N_HEADS, WINDOW), F32)
    p_rows = []
    for n in range(ns):
        s = s_all[n * N_HEADS:(n + 1) * N_HEADS, n * WINDOW:(n + 1) * WINDOW] + bias
        m = jnp.maximum(jnp.max(s, axis=-1, keepdims=True), sink)
        p = jnp.exp2(s - m)
        denom = jnp.sum(p, axis=-1, keepdims=True) + jnp.exp2(sink - m)
        p_rows.append(jnp.concatenate([zero_blk] * n + [p / denom] + [zero_blk] * (ns - 1 - n), axis=-1))
    p_blockdiag = jnp.concatenate(p_rows, axis=0).astype(BF16)
    o_nh = jnp.dot(p_blockdiag, vals, preferred_element_type=F32).astype(BF16)
    o_hn = jnp.dot(perm_ref[1], o_nh, preferred_element_type=F32)
    pieces = []
    for h in range(N_HEADS):
        kv = h // GQA_GROUP
        pieces.append(o_hn[h * ns:(h + 1) * ns, kv * HEAD_DIM:(kv + 1) * HEAD_DIM])
    attn_ref[...] = jnp.concatenate(pieces, axis=-1).astype(BF16)


def _sample_mixer(l, depth, xs, g_attn, w_in, qg, kg, bd, wp, ps, state, ck, cv, sink8, bias_s, pos0):
    n = xs.shape[0]
    ns = 32
    row = lambda i: (i, 0)
    lay = lambda i: (l, 0, 0)
    src = jnp.arange(ns * N_HEADS)
    perm = (((src % N_HEADS) * ns + src // N_HEADS)[:, None] == src[None, :]).astype(BF16)
    perms = jnp.stack([perm, perm.T])
    return pl.pallas_call(
        functools.partial(_sample_kernel, ns=ns, pos0=pos0),
        grid=(n // ns,),
        input_output_aliases={8: 2, 9: 3, 10: 4},
        in_specs=[
            pl.BlockSpec((ns, D_MODEL), row),
            pl.BlockSpec((None, 1, D_MODEL), lay),
            pl.BlockSpec((None, D_MODEL, D_IN), lay),
            pl.BlockSpec((None, 1, Q_W), lay),
            pl.BlockSpec((None, 1, KV_W), lay),
            pl.BlockSpec((256, 256), lambda i: (0, 0)),
            pl.BlockSpec((None, 2, 256, 256), lambda i: (l, 0, 0, 0)),
            pl.BlockSpec((None, 1, POOL_W), lay),
            pl.BlockSpec((None, ns, POOL_STATE, POOL_W), lambda i: (l, i, 0, 0)),
            pl.BlockSpec((None, ns, WINDOW, KV_W), lambda i: (l, i, 0, 0)),
            pl.BlockSpec((None, ns, WINDOW, KV_W), lambda i: (l, i, 0, 0)),
            pl.BlockSpec((N_HEADS, 1), lambda i: (0, 0)),
            pl.BlockSpec((N_HEADS, WINDOW), lambda i: (0, 0)),
            pl.BlockSpec((2, ns * N_HEADS, ns * N_HEADS), lambda i: (0, 0, 0)),
        ],
        out_specs=[
            pl.BlockSpec((ns, POOL_W), row),
            pl.BlockSpec((ns, Q_W), row),
            pl.BlockSpec((None, ns, POOL_STATE, POOL_W), lambda i: (l, i, 0, 0)),
            pl.BlockSpec((None, ns, WINDOW, KV_W), lambda i: (l, i, 0, 0)),
            pl.BlockSpec((None, ns, WINDOW, KV_W), lambda i: (l, i, 0, 0)),
        ],
        out_shape=[
            jax.ShapeDtypeStruct((n, POOL_W), BF16),
            jax.ShapeDtypeStruct((n, Q_W), BF16),
            jax.ShapeDtypeStruct((depth, n, POOL_STATE, POOL_W), F32),
            jax.ShapeDtypeStruct((depth, n, WINDOW, KV_W), F32),
            jax.ShapeDtypeStruct((depth, n, WINDOW, KV_W), F32),
        ],
        compiler_params=pltpu.CompilerParams(
            dimension_semantics=("arbitrary",), vmem_limit_bytes=VMEM_LIMIT),
        name="sample_mixer",
    )(xs, g_attn, w_in, qg, kg, bd, wp, ps, state, ck, cv, sink8, bias_s, perms)


def _merge_router_kernel(pool_ref, attn_ref, x_ref, wout_ref, g_ref, wr_ref, br_ref, utri_ref, cin_ref,
                         x1_ref, h2_ref, route_t_ref, cnt_ref, y_ref, lg_ref):
    i = pl.program_id(0)

    @pl.when(i == 0)
    def _():
        cnt_ref[...] = cin_ref[...]

    tm = x_ref.shape[0]
    rc = tm // MERGE_CHUNKS
    chunks = [slice(ci * rc, (ci + 1) * rc) for ci in range(MERGE_CHUNKS)]
    for rows in chunks:
        y_ref[rows, :] = (jnp.dot(pool_ref[rows, :], wout_ref[0:POOL_W, :], preferred_element_type=F32)
                          + jnp.dot(attn_ref[rows, :], wout_ref[POOL_W:, :], preferred_element_type=F32))
    for rows in chunks:
        x1 = x_ref[rows, :] + y_ref[rows, :]
        x1_ref[rows, :] = x1
        h2 = _rms_bf16(x1, g_ref[...])
        h2_ref[rows, :] = _pack_bf16_pairs(h2)
        lg_ref[rows, :] = jnp.dot(h2, wr_ref[...], preferred_element_type=F32) + br_ref[...]
    logits = lg_ref[...]

    lt = jnp.transpose(logits)
    sub = lax.broadcasted_iota(jnp.int32, (EXPERTS_PER_GROUP, tm), 0)
    neg = -jnp.inf
    big = jnp.int32(EXPERTS_PER_GROUP)
    gl = jnp.where(sub < N_EXPERT_GROUPS, lt[GROUP_LANE0:GROUP_LANE0 + EXPERTS_PER_GROUP, :], neg)
    gmax = jnp.max(gl, axis=0, keepdims=True)
    grp = jnp.min(jnp.where(gl == gmax, sub, big), axis=0, keepdims=True)
    g_w = 1.0 / jnp.sum(jnp.exp(gl - gmax), axis=0, keepdims=True)
    el = lt[(N_EXPERT_GROUPS - 1) * EXPERTS_PER_GROUP:N_EXPERT_GROUPS * EXPERTS_PER_GROUP, :]
    for gi in range(N_EXPERT_GROUPS - 2, -1, -1):
        el = jnp.where(grp == gi, lt[gi * EXPERTS_PER_GROUP:(gi + 1) * EXPERTS_PER_GROUP, :], el)
    v1 = jnp.max(el, axis=0, keepdims=True)
    i1 = jnp.min(jnp.where(el == v1, sub, big), axis=0, keepdims=True)
    el2 = jnp.where(sub == i1, neg, el)
    v2 = jnp.max(el2, axis=0, keepdims=True)
    i2 = jnp.min(jnp.where(el2 == v2, sub, big), axis=0, keepdims=True)
    e21 = jnp.exp(v2 - v1)
    w1 = g_w / (1.0 + e21)
    w2 = g_w * e21 / (1.0 + e21)
    e1 = grp * EXPERTS_PER_GROUP + i1
    e2 = grp * EXPERTS_PER_GROUP + i2

    esub = lax.broadcasted_iota(jnp.int32, (N_EXPERTS, tm), 0)
    oh1 = esub == e1
    oh2 = esub == e2
    c = jnp.where(oh1 | oh2, 1.0, 0.0)
    prefix = jnp.dot(c.astype(BF16), utri_ref[...], preferred_element_type=F32) + cnt_ref[...]
    r1 = jnp.sum(jnp.where(oh1, prefix, 0.0), axis=0, keepdims=True)
    r2 = jnp.sum(jnp.where(oh2, prefix, 0.0), axis=0, keepdims=True)
    cnt_ref[...] = cnt_ref[...] + jnp.sum(c, axis=1, keepdims=True)

    fields = jnp.zeros((ROUTE_FIELDS, tm), F32)
    for idx, val in enumerate((e1.astype(F32), e2.astype(F32), w1, w2, r1, r2)):
        fields = jnp.where(sub == idx, val, fields)
    route_t_ref[...] = fields


def _merge_router(l, pool, attn, x2d, w_out, g_ffn, wr, br, cnt_in, tm):
    t = x2d.shape[0]
    utri = (jnp.arange(tm)[:, None] < jnp.arange(tm)[None, :]).astype(BF16)
    row = lambda i: (i, 0)
    lay = lambda i: (l, 0, 0)
    return pl.pallas_call(
        _merge_router_kernel,
        grid=(t // tm,),
        in_specs=[
            pl.BlockSpec((tm, POOL_W), row),
            pl.BlockSpec((tm, Q_W), row),
            pl.BlockSpec((tm, D_MODEL), row),
            pl.BlockSpec((None, D_MODEL, D_MODEL), lay),
            pl.BlockSpec((None, 1, D_MODEL), lay),
            pl.BlockSpec((None, D_MODEL, LANES), lay),
            pl.BlockSpec((None, 1, LANES), lay),
            pl.BlockSpec((tm, tm), lambda i: (0, 0)),
            pl.BlockSpec((N_EXPERTS, 1), lambda i: (0, 0)),
        ],
        out_specs=[
            pl.BlockSpec((tm, D_MODEL), row),
            pl.BlockSpec((tm, D_MODEL // 2), row),
            pl.BlockSpec((ROUTE_FIELDS, tm), lambda i: (0, i)),
            pl.BlockSpec((N_EXPERTS, 1), lambda i: (0, 0)),
        ],
        out_shape=[
            jax.ShapeDtypeStruct((t, D_MODEL), F32),
            jax.ShapeDtypeStruct((t, D_MODEL // 2), jnp.int32),
            jax.ShapeDtypeStruct((ROUTE_FIELDS, t), F32),
            jax.ShapeDtypeStruct((N_EXPERTS, 1), F32),
        ],
        scratch_shapes=[pltpu.VMEM((tm, D_MODEL), F32), pltpu.VMEM((tm, LANES), F32)],
        compiler_params=pltpu.CompilerParams(
            dimension_semantics=("arbitrary",), vmem_limit_bytes=VMEM_LIMIT),
        name="merge_router",
    )(pool, attn, x2d, w_out, g_ffn, wr, br, utri, cnt_in)


def _moe_kernel(be_ref, rv_ref, nx_ref, sl_ref, first_ref, xd_ref, wg_hbm, wu_hbm, wd_hbm, yd_ref,
                wg_f, wu_f, wd_f, wg_s, wu_s, wd_s, sem, *, layer):
    step = pl.program_id(0)

    def weight_copies(e, s):
        return [pltpu.make_async_copy(w_hbm.at[layer, e], w_f.at[s], sem.at[s, n])
                for n, (w_hbm, w_f) in enumerate(((wg_hbm, wg_f), (wu_hbm, wu_f), (wd_hbm, wd_f)))]

    @pl.when(step == 0)
    def _():
        for s in range(WEIGHT_SLOTS - 1):
            @pl.when(first_ref[s] >= 0)
            def _():
                for c in weight_copies(first_ref[s], s):
                    c.start()

    def enter_expert(i):
        expert, slot = be_ref[i], sl_ref[i]

        @pl.when((i == 0) | (expert != be_ref[jnp.maximum(i - 1, 0)]))
        def _():
            for c in weight_copies(expert, slot):
                c.wait()

            @pl.when(nx_ref[i] >= 0)
            def _():
                for c in weight_copies(nx_ref[i], lax.rem(slot + WEIGHT_SLOTS - 1, WEIGHT_SLOTS)):
                    c.start(priority=1)

            wg_s[...] = wg_f[slot].astype(BF16)
            wu_s[...] = wu_f[slot].astype(BF16)
            wd_s[...] = wd_f[slot].astype(BF16)

    def experts_on(row0, n_rows, rows_valid):
        rows = pl.ds(row0, n_rows)
        row = lax.broadcasted_iota(jnp.int32, (n_rows, D_MODEL // 2), 0)
        x = _unpack_bf16_pairs(jnp.where(row < rows_valid, xd_ref[rows, :], 0)).astype(BF16)
        gate = jnp.dot(x, wg_s[...], preferred_element_type=F32)
        up = jnp.dot(x, wu_s[...], preferred_element_type=F32)
        act = (gate * jax.nn.sigmoid(gate) * up).astype(BF16)
        y = jnp.dot(act, wd_s[...], preferred_element_type=F32)
        yd_ref[rows, :] = _pack_bf16_pairs(y.astype(BF16))

    def experts_ragged(row0, lead_rows, rows_last):
        half = MOE_BM // 2

        @pl.when(rows_last > half)
        def _():
            experts_on(row0, lead_rows + MOE_BM, lead_rows + rows_last)

        @pl.when(rows_last <= half)
        def _():
            experts_on(row0, lead_rows + half, lead_rows + rows_last)
            yd_ref[pl.ds(row0 + lead_rows + half, half), :] = jnp.zeros((half, D_MODEL // 2), jnp.int32)

    def single_block(i, row0):
        enter_expert(i)

        @pl.when(rv_ref[i] > 0)
        def _():
            experts_ragged(row0, 0, rv_ref[i])

        @pl.when(rv_ref[i] <= 0)
        def _():
            yd_ref[pl.ds(row0, MOE_BM), :] = jnp.zeros((MOE_BM, D_MODEL // 2), jnp.int32)

    @pl.when(rv_ref[step * MOE_STEP_BLOCKS] > 0)
    def _():
        for pair in range(MOE_STEP_BLOCKS // 2):
            ia = step * MOE_STEP_BLOCKS + 2 * pair
            ib = ia + 1
            row0 = 2 * pair * MOE_BM
            same = (be_ref[ib] == be_ref[ia]) & (rv_ref[ib] > 0)

            @pl.when(same)
            def _():
                enter_expert(ia)
                experts_ragged(row0, MOE_BM, rv_ref[ib])

            @pl.when(jnp.logical_not(same))
            def _():
                single_block(ia, row0)
                single_block(ib, row0 + MOE_BM)


def _moe_experts(l, block_e, rows_valid, next_e, slot, first_e, xd, w_gate, w_up, w_down):
    n_blocks = xd.shape[0] // MOE_BM
    step_rows = MOE_STEP_BLOCKS * MOE_BM
    row = lambda i, be, rv, nx, sl, fe: (jnp.minimum(i, fe[WEIGHT_SLOTS - 1] - 1), 0)
    return pl.pallas_call(
        functools.partial(_moe_kernel, layer=l),
        grid_spec=pltpu.PrefetchScalarGridSpec(
            num_scalar_prefetch=5,
            grid=(n_blocks // MOE_STEP_BLOCKS,),
            in_specs=[
                pl.BlockSpec((step_rows, D_MODEL // 2), row),
                pl.BlockSpec(memory_space=pl.ANY),
                pl.BlockSpec(memory_space=pl.ANY),
                pl.BlockSpec(memory_space=pl.ANY),
            ],
            out_specs=pl.BlockSpec((step_rows, D_MODEL // 2), row),
            scratch_shapes=[
                pltpu.VMEM((WEIGHT_SLOTS, D_MODEL, EXPERT_FF), F32),
                pltpu.VMEM((WEIGHT_SLOTS, D_MODEL, EXPERT_FF), F32),
                pltpu.VMEM((WEIGHT_SLOTS, EXPERT_FF, D_MODEL), F32),
                pltpu.VMEM((D_MODEL, EXPERT_FF), BF16),
                pltpu.VMEM((D_MODEL, EXPERT_FF), BF16),
                pltpu.VMEM((EXPERT_FF, D_MODEL), BF16),
                pltpu.SemaphoreType.DMA((WEIGHT_SLOTS, 3)),
            ],
        ),
        out_shape=jax.ShapeDtypeStruct((n_blocks * MOE_BM, D_MODEL // 2), jnp.int32),
        compiler_params=pltpu.CompilerParams(
            dimension_semantics=("arbitrary",), vmem_limit_bytes=VMEM_LIMIT),
        name="moe_experts",
    )(block_e, rows_valid, next_e, slot, first_e, xd, w_gate, w_up, w_down)


def _sc_worker_id():
    return lax.axis_index("s") * SC_CORES + lax.axis_index("c")


def _sc_dispatch(hp, hs, dest_p, dest_s, n_rows):
    tp, width = hp.shape
    per_w = tp // SC_WORKERS
    n_ch = per_w // DISP_CH
    n_sw = hs.shape[0] // SAMPLE_CH
    mesh = plsc.VectorSubcoreMesh(core_axis_name="c", subcore_axis_name="s")

    @functools.partial(
        pl.kernel, mesh=mesh,
        out_type=jax.ShapeDtypeStruct((n_rows, width), jnp.int32),
        scratch_types=[
            pltpu.VMEM((2, n_ch, DISP_CH), jnp.int32),
            pltpu.VMEM((2, 1, SAMPLE_CH), jnp.int32),
            pltpu.VMEM((SC_RING, DISP_CH, width), jnp.int32),
            pltpu.SemaphoreType.DMA((SC_RING,)),
            pltpu.SemaphoreType.DMA((SC_RING, 2)),
        ],
        name="sc_dispatch",
    )
    def k(hp_hbm, hs_hbm, dp_hbm, ds_hbm, xd_hbm, idx_v, idxs_v, bufs, rsem, wsem):
        wid = _sc_worker_id()
        base = wid * per_w
        for kk in range(2):
            pltpu.sync_copy(dp_hbm.at[kk, wid], idx_v.at[kk])
        reads = [pltpu.make_async_copy(hp_hbm.at[pl.ds(base + j * DISP_CH, DISP_CH)],
                                       bufs.at[j % SC_RING], rsem.at[j % SC_RING]) for j in range(n_ch)]
        writes = [[pltpu.make_async_copy(bufs.at[j % SC_RING], xd_hbm.at[idx_v.at[kk, j]],
                                         wsem.at[j % SC_RING, kk]) for kk in range(2)] for j in range(n_ch)]
        for j in range(min(SC_RING - 1, n_ch)):
            reads[j].start()
        for j in range(n_ch):
            reads[j].wait()
            for w in writes[j]:
                w.start()
            if j >= 1:
                for w in writes[j - 1]:
                    w.wait()
            if j + SC_RING - 1 < n_ch:
                reads[j + SC_RING - 1].start()
        for w in writes[n_ch - 1]:
            w.wait()

        @pl.when(wid < n_sw)
        def _():
            rows = bufs.at[0, pl.ds(0, SAMPLE_CH)]
            for kk in range(2):
                pltpu.sync_copy(ds_hbm.at[kk, wid], idxs_v.at[kk])
            pltpu.sync_copy(hs_hbm.at[pl.ds(wid * SAMPLE_CH, SAMPLE_CH)], rows)
            for kk in range(2):
                pltpu.sync_copy(rows, xd_hbm.at[idxs_v.at[kk, 0]])

    return k(hp, hs, dest_p, dest_s)


def _sc_sample_gather(yd, dest_s, ts):
    width = yd.shape[1]
    n_sw = ts // SAMPLE_CH
    mesh = plsc.VectorSubcoreMesh(core_axis_name="c", subcore_axis_name="s")

    @functools.partial(
        pl.kernel, mesh=mesh,
        out_type=jax.ShapeDtypeStruct((2, ts, width), yd.dtype),
        scratch_types=[
            pltpu.VMEM((2, 1, SAMPLE_CH), jnp.int32),
            pltpu.VMEM((2, SAMPLE_CH, width), yd.dtype),
        ],
        name="sc_sample_gather",
    )
    def k(yd_hbm, ds_hbm, g_hbm, idxs_v, bufs):
        wid = _sc_worker_id()

        @pl.when(wid < n_sw)
        def _():
            for kk in range(2):
                pltpu.sync_copy(ds_hbm.at[kk, wid], idxs_v.at[kk])
            for kk in range(2):
                pltpu.sync_copy(yd_hbm.at[idxs_v.at[kk, 0]], bufs.at[kk])
                pltpu.sync_copy(bufs.at[kk], g_hbm.at[kk, pl.ds(wid * SAMPLE_CH, SAMPLE_CH)])

    return k(yd, dest_s)


def _sc_combine_gather(yd, dest_p, tp):
    width = yd.shape[1]
    per_w = tp // SC_WORKERS
    n_ch = per_w // COMB_CH
    mesh = plsc.VectorSubcoreMesh(core_axis_name="c", subcore_axis_name="s")

    @functools.partial(
        pl.kernel, mesh=mesh,
        out_type=jax.ShapeDtypeStruct((2, tp, width), yd.dtype),
        scratch_types=[
            pltpu.VMEM((2, n_ch, COMB_CH), jnp.int32),
            pltpu.VMEM((SC_RING, COMB_CH, width), yd.dtype),
            pltpu.SemaphoreType.DMA((SC_RING,)),
            pltpu.SemaphoreType.DMA((SC_RING,)),
        ],
        name="sc_combine_gather",
    )
    def k(yd_hbm, dp_hbm, g_hbm, idx_v, bufs, gsem, wsem):
        wid = _sc_worker_id()
        base = wid * per_w
        for kk in range(2):
            pltpu.sync_copy(dp_hbm.at[kk, wid], idx_v.at[kk])
        items = [(kk, j) for kk in range(2) for j in range(n_ch)]
        n_items = len(items)
        gathers = [pltpu.make_async_copy(yd_hbm.at[idx_v.at[kk, j]], bufs.at[n % SC_RING], gsem.at[n % SC_RING])
                   for n, (kk, j) in enumerate(items)]
        outs = [pltpu.make_async_copy(bufs.at[n % SC_RING], g_hbm.at[kk, pl.ds(base + j * COMB_CH, COMB_CH)],
                                      wsem.at[n % SC_RING]) for n, (kk, j) in enumerate(items)]
        for n in range(min(SC_RING - 1, n_items)):
            gathers[n].start()
        for n in range(n_items):
            gathers[n].wait()
            outs[n].start()
            if n >= 1:
                outs[n - 1].wait()
            if n + SC_RING - 1 < n_items:
                gathers[n + SC_RING - 1].start()
        outs[n_items - 1].wait()

    return k(yd, dest_p)


def _combine_kernel(x1_ref, g_ref, route_t_ref, x2_ref):
    fields = route_t_ref[...]
    tm = fields.shape[1]
    cols = jnp.transpose(jnp.concatenate([fields, jnp.zeros((LANES - ROUTE_FIELDS, tm), F32)], axis=0))
    w1 = cols[:, 2:3]
    w2 = cols[:, 3:4]
    x2_ref[...] = x1_ref[...] + _unpack_bf16_pairs(g_ref[0]) * w1 + _unpack_bf16_pairs(g_ref[1]) * w2


def _combine(x1, g, route_t, tm, rows=None, row0=0):
    t = x1.shape[0]
    in_place = rows is not None
    n = (rows if in_place else t) // tm
    blk0 = row0 // tm
    row = lambda i: (blk0 + i, 0)
    return pl.pallas_call(
        _combine_kernel,
        grid=(n,),
        in_specs=[
            pl.BlockSpec((tm, D_MODEL), row),
            pl.BlockSpec((2, tm, D_MODEL // 2), lambda i: (0, i, 0)),
            pl.BlockSpec((ROUTE_FIELDS, tm), lambda i: (0, blk0 + i)),
        ],
        out_specs=pl.BlockSpec((tm, D_MODEL), row),
        out_shape=jax.ShapeDtypeStruct((t, D_MODEL), F32),
        input_output_aliases={0: 0} if in_place else {},
        compiler_params=pltpu.CompilerParams(
            dimension_semantics=("arbitrary",), vmem_limit_bytes=VMEM_LIMIT),
        name="combine",
    )(x1, g, route_t)


def _dest_layout(dest, workers, chunk):
    t = dest.shape[1]
    return dest.reshape(2, workers, t // (workers * chunk), chunk)


def _hier_moe(l, h2p, h2s, route_tp, route_ts, counts, w_gate, w_up, w_down, gather_parts):
    tp, ts = h2p.shape[0], h2s.shape[0]
    n_assign = 2 * (tp + ts)
    n_blocks = -(-n_assign // MOE_BM) + N_EXPERTS
    n_blocks = -(-n_blocks // MOE_STEP_BLOCKS) * MOE_STEP_BLOCKS
    pcounts = (counts + MOE_BM - 1) // MOE_BM * MOE_BM
    pends = jnp.cumsum(pcounts)
    poffsets = pends - pcounts
    starts = jnp.arange(n_blocks, dtype=jnp.int32) * MOE_BM
    block_e = jnp.minimum(jnp.sum((pends[None, :] <= starts[:, None]).astype(jnp.int32), axis=1),
                          N_EXPERTS - 1)
    experts = jnp.arange(N_EXPERTS, dtype=jnp.int32)

    def lookup(table, idx):
        return jnp.sum(jnp.where(idx[..., None] == experts, table, 0), axis=-1)

    rows_valid = jnp.clip(lookup(poffsets + counts, block_e) - starts, 0, MOE_BM).astype(jnp.int32)
    used = counts > 0
    last_e = jnp.max(jnp.where(used, jnp.arange(N_EXPERTS, dtype=jnp.int32), 0))
    block_e = jnp.where(rows_valid > 0, block_e, last_e).astype(jnp.int32)
    place = jnp.cumsum(used.astype(jnp.int32)) - 1
    by_place = jnp.sum(jnp.where(used[None, :] & (place[None, :] == experts[:, None]), experts[None, :], 0),
                       axis=1)
    n_used = jnp.sum(used.astype(jnp.int32))

    def at_place(p):
        return jnp.where(p < n_used, lookup(by_place, jnp.minimum(p, N_EXPERTS - 1)), -1).astype(jnp.int32)

    ahead_of = at_place(place + (WEIGHT_SLOTS - 1))
    next_e = lookup(ahead_of, block_e)
    slot = lookup(place % WEIGHT_SLOTS, block_e)
    n_steps_used = -(-(pends[-1] // MOE_BM) // MOE_STEP_BLOCKS)
    first_e = jnp.concatenate([at_place(jnp.arange(WEIGHT_SLOTS - 1, dtype=jnp.int32)),
                               n_steps_used.reshape(1).astype(jnp.int32)])

    def dest_of(route_t):
        return lookup(poffsets, route_t[0:2].astype(jnp.int32)) + route_t[4:6].astype(jnp.int32)

    dest_p, dest_s = dest_of(route_tp), dest_of(route_ts)
    n_sw = ts // SAMPLE_CH
    xd = _sc_dispatch(h2p, h2s, _dest_layout(dest_p, SC_WORKERS, DISP_CH),
                      _dest_layout(dest_s, n_sw, SAMPLE_CH), n_blocks * MOE_BM)
    yd = _moe_experts(l, block_e, rows_valid, next_e, slot, first_e, xd, w_gate, w_up, w_down)
    g_s = _sc_sample_gather(yd, _dest_layout(dest_s, n_sw, SAMPLE_CH), ts)
    part = tp // gather_parts
    g_p = [_sc_combine_gather(yd, _dest_layout(dest_p[:, n * part:(n + 1) * part], SC_WORKERS, COMB_CH), part)
           for n in range(gather_parts)]
    return g_p, g_s


def kernel(x_prompt, x_sample, state_pool, cache_k_win, cache_v_win, norm_attn_g, w_in, pool_w, pool_scale, q_norm_g, k_norm_g, attn_sinks, w_out, norm_ffn_g, router_group_w, router_group_b, router_expert_w, router_expert_b, w_gate, w_up, w_down):
    n_p, t_p, d = x_prompt.shape
    n_s, t_s, _ = x_sample.shape
    depth = w_in.shape[0]
    lw_s = cache_k_win.shape[2]
    assert t_s == 1 and lw_s == WINDOW and d == D_MODEL
    assert t_p % TM_PROJ == 0 and t_p >= WINDOW

    seg = jnp.arange(256) // HEAD_DIM
    bd = jnp.where(seg[:, None] == seg[None, :], 1.0 / HEAD_DIM, 0.0).astype(BF16)
    slopes = jnp.exp2(-8.0 * jnp.arange(1, N_HEADS + 1, dtype=F32) / N_HEADS)
    bias_p = _prompt_bias_t()
    dist_s = (WINDOW - 1) - jnp.arange(WINDOW, dtype=F32)
    bias_s = -LOG2E * slopes[:, None] * dist_s[None, :]

    wp = jnp.zeros((depth, 2, 256, 256), F32)
    for p in range(2):
        wp = wp.at[:, p, :POOL_GC, :POOL_GC].set(pool_w[:, 2 * p])
        wp = wp.at[:, p, POOL_GC:, POOL_GC:].set(pool_w[:, 2 * p + 1])
    assert GROUP_LANE0 == N_EXPERTS
    lane_pad = LANES - N_EXPERTS - N_EXPERT_GROUPS
    wr = jnp.concatenate([router_expert_w, router_group_w, jnp.zeros((depth, D_MODEL, lane_pad), F32)], axis=-1)
    br = jnp.concatenate([router_expert_b, router_group_b, jnp.zeros((depth, lane_pad), F32)],
                         axis=-1).reshape(depth, 1, LANES)
    lp = dict(
        w_in=w_in.astype(BF16),
        w_out=w_out.astype(BF16),
        g_attn=norm_attn_g.reshape(depth, 1, D_MODEL),
        g_ffn=norm_ffn_g.reshape(depth, 1, D_MODEL),
        qg=(jnp.tile(q_norm_g, (1, N_HEADS)) * (ATTN_SCALE * LOG2E)).reshape(depth, 1, Q_W),
        kg=jnp.tile(k_norm_g, (1, N_KV_HEADS)).reshape(depth, 1, KV_W),
        wp=wp.astype(BF16),
        ps=pool_scale.reshape(depth, 1, POOL_W),
        wr=wr.astype(BF16),
        br=br,
        state=state_pool,
        ck=cache_k_win.reshape(depth, n_s, lw_s, KV_W),
        cv=cache_v_win.reshape(depth, n_s, lw_s, KV_W),
    )

    xp = x_prompt.reshape(n_p * t_p, D_MODEL)
    xs = x_sample.reshape(n_s, D_MODEL)
    lw_p = min(WINDOW, t_p)
    pool_p, kp_new, vp_new = [], [], []
    sample_state = [lp["state"], lp["ck"], lp["cv"]]
    zero_cnt = jnp.zeros((N_EXPERTS, 1), F32)
    pending = None
    for l in range(depth):
        sinks = attn_sinks[l] * LOG2E
        pool_so, attn_so, *sample_state = _sample_mixer(
            l, depth, xs, lp["g_attn"], lp["w_in"], lp["qg"], lp["kg"], bd, lp["wp"], lp["ps"],
            *sample_state, sinks.reshape(N_HEADS, 1), bias_s, PAST_LEN)
        x1s, h2s, route_ts, cnt_s = _merge_router(
            l, pool_so, attn_so, xs, lp["w_out"], lp["g_ffn"], lp["wr"], lp["br"], zero_cnt, n_s)
        outs = _proj_pool_prompt(
            l, xp if pending is None else pending, n_p, t_p,
            lp["g_attn"], lp["w_in"], lp["qg"], lp["kg"], bd, lp["wp"], lp["ps"])
        if pending is not None:
            xp, outs = outs[0], outs[1:]
        pool_o, q, k, vt, utail, ktail, vtail = outs
        attn_o = _attn_prompt(q, k, vt, bias_p, sinks, n_p, t_p)
        x1p, h2p, route_tp, cnt_all = _merge_router(
            l, pool_o, attn_o, xp, lp["w_out"], lp["g_ffn"], lp["wr"], lp["br"], cnt_s, TM_MERGE)
        pool_p.append(utail[:, 16 - POOL_STATE:, :])
        kp_new.append(ktail)
        vp_new.append(vtail)
        counts = cnt_all[:, 0].astype(jnp.int32)
        last = l == depth - 1
        g_p, g_s = _hier_moe(l, h2p, h2s, route_tp, route_ts, counts, w_gate, w_up, w_down,
                             FINAL_GATHER_PARTS if last else 1)
        xs = _combine(x1s, g_s, route_ts, n_s)
        pending = (x1p, g_p[0], route_tp)
    xp = x1p
    part = n_p * t_p // FINAL_GATHER_PARTS
    for n, g_part in enumerate(g_p):
        xp = _combine(xp, g_part, route_tp, TM_MERGE, rows=part, row0=n * part)
    return (xp.reshape(n_p, t_p, D_MODEL), xs.reshape(n_s, t_s, D_MODEL),
            jnp.stack(pool_p),
            jnp.stack(kp_new).reshape(depth, n_p, lw_p, N_KV_HEADS, HEAD_DIM),
            jnp.stack(vp_new).reshape(depth, n_p, lw_p, N_KV_HEADS, HEAD_DIM),
            sample_state[0],
            sample_state[1].reshape(depth, n_s, lw_s, N_KV_HEADS, HEAD_DIM),
            sample_state[2].reshape(depth, n_s, lw_s, N_KV_HEADS, HEAD_DIM))
```

```python
import functools

import jax
import jax.numpy as jnp
from jax import lax
from jax.experimental import pallas as pl
from jax.experimental.pallas import tpu as pltpu
from jax.experimental.pallas import tpu_sc as plsc

D_MODEL = 1024
POOL_W = 512
POOL_WINDOWS = (2, 4, 8, 16)
POOL_GC = 128
POOL_STATE = 15
HEAD_DIM = 64
N_HEADS = 8
N_KV_HEADS = 2
GQA_GROUP = 4
Q_W = 512
KV_W = 128
D_IN = POOL_W + Q_W + 2 * KV_W
WINDOW = 128
ATTN_SCALE = HEAD_DIM ** -0.5
LOG2E = 1.4426950408889634
N_EXPERT_GROUPS = 4
EXPERTS_PER_GROUP = 8
N_EXPERTS = 32
EXPERT_FF = 512
EPS = 1e-6
PAST_LEN = 16384

LANES = 128
HALO = 32
TM_PROJ = 1024
TM_MERGE = 1024
MERGE_CHUNKS = 4
ATTN_QB = 16
MOE_BM = 256
MOE_STEP_BLOCKS = 4
WEIGHT_SLOTS = 3
GROUP_LANE0 = 32
ROUTE_FIELDS = 8
SC_CORES = 2
SC_SUBCORES = 16
SC_WORKERS = SC_CORES * SC_SUBCORES
DISP_CH = 64
COMB_CH = 64
SAMPLE_CH = 32
SC_RING = 3
VMEM_LIMIT = 48 * 1024 * 1024

BF16 = jnp.bfloat16
F32 = jnp.float32


def _pack_bf16_pairs(h):
    w = h.shape[1] // 2
    hi = lax.bitcast_convert_type(h[:, :w].astype(F32), jnp.uint32)
    lo = lax.bitcast_convert_type(h[:, w:].astype(F32), jnp.uint32)
    return lax.bitcast_convert_type(hi | (lo >> 16), jnp.int32)


def _unpack_bf16_pairs(words):
    u = lax.bitcast_convert_type(words, jnp.uint32)
    hi = lax.bitcast_convert_type(u & jnp.uint32(0xFFFF0000), F32)
    lo = lax.bitcast_convert_type(u << 16, F32)
    return jnp.concatenate([hi, lo], axis=-1)


def _segment_mean_sq(a, bd):
    w = a.shape[1]
    return jnp.dot((a * a).astype(BF16), bd[:w, :w], preferred_element_type=F32)


def _rms_bf16(x, g):
    ms = jnp.mean(x * x, axis=-1, keepdims=True)
    return (x * lax.rsqrt(ms + EPS) * g).astype(BF16)


def _qk_norm(q, k, qg, kg, bd):
    qn = []
    for c in range(Q_W // 256):
        qc = q[:, c * 256:(c + 1) * 256]
        qn.append(qc * lax.rsqrt(_segment_mean_sq(qc, bd) + EPS))
    qn = jnp.concatenate(qn, axis=-1) * qg
    kn = k * lax.rsqrt(_segment_mean_sq(k, bd) + EPS) * kg
    return qn, kn


def _project(x, g, w_in, qg, kg, bd):
    z = jnp.dot(_rms_bf16(x, g), w_in, preferred_element_type=F32)
    u = z[:, :POOL_W]
    q = z[:, POOL_W:POOL_W + Q_W]
    k = z[:, POOL_W + Q_W:POOL_W + Q_W + KV_W]
    v = z[:, POOL_W + Q_W + KV_W:]
    qn, kn = _qk_norm(q, k, qg, kg, bd)
    return u, qn, kn, v


def _pool_project(d_groups, wp_ref, ps):
    outs = []
    for p in range(2):
        dp = jnp.concatenate([d_groups[2 * p], d_groups[2 * p + 1]], axis=-1).astype(BF16)
        y = jnp.dot(dp, wp_ref[p], preferred_element_type=F32)
        outs.append(y * ps[:, p * 256:(p + 1) * 256])
    return jnp.concatenate(outs, axis=-1)


def _proj_pool_kernel(x_ref, g_ref, win_ref, qg_ref, kg_ref, bd_ref, wp_ref, ps_ref,
                      pool_ref, q_ref, k_ref, vt_ref, utail_ref, ktail_ref, vtail_ref,
                      ext_ref, sa_ref, sb_ref, zq_ref, *, tm, n_j):
    j = pl.program_id(1)

    @pl.when(j == 0)
    def _():
        ext_ref[0:HALO, :] = jnp.zeros((HALO, POOL_W), F32)

    r = tm + HALO
    h = _rms_bf16(x_ref[...], g_ref[...])
    ext_ref[HALO:r, :] = jnp.dot(h, win_ref[:, 0:POOL_W], preferred_element_type=F32)
    zq_ref[...] = jnp.dot(h, win_ref[:, POOL_W:], preferred_element_type=F32)
    u = ext_ref[HALO:r, :]
    sa_ref[8:r, :] = ext_ref[8:r, :] + ext_ref[7:r - 1, :]
    sb_ref[16:r, 128:] = sa_ref[16:r, 128:] + sa_ref[14:r - 2, 128:]
    sa_ref[24:r, 256:] = sb_ref[24:r, 256:] + sb_ref[20:r - 4, 256:]
    sb_ref[32:r, 384:] = sa_ref[32:r, 384:] + sa_ref[24:r - 8, 384:]
    pos1 = j * tm + lax.broadcasted_iota(jnp.int32, (tm, POOL_GC), 0) + 1
    sums = (sa_ref, sb_ref, sa_ref, sb_ref)
    d_groups = []
    for gi, w in enumerate(POOL_WINDOWS):
        sl = slice(gi * POOL_GC, (gi + 1) * POOL_GC)
        cnt = jnp.minimum(pos1, w).astype(F32)
        d_groups.append(sums[gi][HALO:r, sl] / cnt - u[:, sl])
    pool_ref[...] = _pool_project(d_groups, wp_ref, ps_ref[...]).astype(BF16)
    ext_ref[16:HALO, :] = ext_ref[tm + 16:r, :]

    qn, kn = _qk_norm(zq_ref[:, 0:Q_W], zq_ref[:, Q_W:Q_W + KV_W], qg_ref[...], kg_ref[...], bd_ref[...])
    v = zq_ref[:, Q_W + KV_W:]
    q_ref[...] = qn.astype(BF16)
    k_ref[...] = kn.astype(BF16)
    vt_ref[...] = jnp.transpose(v).astype(BF16)

    @pl.when(j == n_j - 1)
    def _():
        utail_ref[...] = u[tm - 16:, :]
        ktail_ref[...] = kn[tm - WINDOW:, :]
        vtail_ref[...] = v[tm - WINDOW:, :]


def _proj_pool_combine_kernel(x1_ref, gath_ref, route_ref, *rest, tm, n_j):
    x2_ref = rest[7]
    _combine_kernel(x1_ref, gath_ref, route_ref, x2_ref)
    _proj_pool_kernel(x2_ref, *rest[:7], *rest[8:], tm=tm, n_j=n_j)


def _proj_pool_prompt(l, x_in, n_seq, seq, g_attn, w_in, qg, kg, bd, wp, ps):
    tm = TM_PROJ
    n_j = seq // tm
    t = n_seq * seq
    row = lambda b, j: (b * n_j + j, 0)
    lay = lambda b, j: (l, 0, 0)
    fused = isinstance(x_in, tuple)
    if fused:
        kern = _proj_pool_combine_kernel
        x_args = list(x_in)
        x_specs = [pl.BlockSpec((tm, D_MODEL), row),
                   pl.BlockSpec((2, tm, D_MODEL // 2), lambda b, j: (0, b * n_j + j, 0)),
                   pl.BlockSpec((ROUTE_FIELDS, tm), lambda b, j: (0, b * n_j + j))]
        x_out_specs = [pl.BlockSpec((tm, D_MODEL), row)]
        x_out_shape = [jax.ShapeDtypeStruct((t, D_MODEL), F32)]
    else:
        kern = _proj_pool_kernel
        x_args = [x_in]
        x_specs = [pl.BlockSpec((tm, D_MODEL), row)]
        x_out_specs, x_out_shape = [], []
    return pl.pallas_call(
        functools.partial(kern, tm=tm, n_j=n_j),
        grid=(n_seq, n_j),
        in_specs=x_specs + [
            pl.BlockSpec((None, 1, D_MODEL), lay),
            pl.BlockSpec((None, D_MODEL, D_IN), lay),
            pl.BlockSpec((None, 1, Q_W), lay),
            pl.BlockSpec((None, 1, KV_W), lay),
            pl.BlockSpec((256, 256), lambda b, j: (0, 0)),
            pl.BlockSpec((None, 2, 256, 256), lambda b, j: (l, 0, 0, 0)),
            pl.BlockSpec((None, 1, POOL_W), lay),
        ],
        out_specs=x_out_specs + [
            pl.BlockSpec((tm, POOL_W), row),
            pl.BlockSpec((tm, Q_W), row),
            pl.BlockSpec((tm, KV_W), row),
            pl.BlockSpec((KV_W, tm), lambda b, j: (0, b * n_j + j)),
            pl.BlockSpec((None, 16, POOL_W), lambda b, j: (b, 0, 0)),
            pl.BlockSpec((None, WINDOW, KV_W), lambda b, j: (b, 0, 0)),
            pl.BlockSpec((None, WINDOW, KV_W), lambda b, j: (b, 0, 0)),
        ],
        out_shape=x_out_shape + [
            jax.ShapeDtypeStruct((t, POOL_W), BF16),
            jax.ShapeDtypeStruct((t, Q_W), BF16),
            jax.ShapeDtypeStruct((t, KV_W), BF16),
            jax.ShapeDtypeStruct((KV_W, t), BF16),
            jax.ShapeDtypeStruct((n_seq, 16, POOL_W), F32),
            jax.ShapeDtypeStruct((n_seq, WINDOW, KV_W), F32),
            jax.ShapeDtypeStruct((n_seq, WINDOW, KV_W), F32),
        ],
        scratch_shapes=[pltpu.VMEM((tm + HALO, POOL_W), F32)] * 3 + [pltpu.VMEM((tm, Q_W + 2 * KV_W), F32)],
        compiler_params=pltpu.CompilerParams(
            dimension_semantics=("arbitrary", "arbitrary"), vmem_limit_bytes=VMEM_LIMIT),
        name="proj_pool_prompt",
    )(*x_args, g_attn, w_in, qg, kg, bd, wp, ps)


def _attn_kernel(sink_ref, q_ref, kp_ref, kc_ref, vtp_ref, vtc_ref, bias_ref, o_ref, s_ref):
    j = pl.program_id(1)
    kk_all = jnp.concatenate([kp_ref[...], kc_ref[...]], axis=0)
    vt_all = jnp.concatenate([vtp_ref[...], vtc_ref[...]], axis=1)
    from_prev = (lax.broadcasted_iota(jnp.int32, (WINDOW, WINDOW), 0)
                 > lax.broadcasted_iota(jnp.int32, (WINDOW, WINDOW), 1))
    units = [(blk, kv) for blk in range(ATTN_QB) for kv in range(N_KV_HEADS)]

    def scores(n):
        blk, kv = units[n]
        q = q_ref[blk * WINDOW:(blk + 1) * WINDOW, :]
        kk = kk_all[blk * WINDOW:(blk + 2) * WINDOW, kv * HEAD_DIM:(kv + 1) * HEAD_DIM]
        heads = range(kv * GQA_GROUP, (kv + 1) * GQA_GROUP)
        q_rows = jnp.concatenate([q[:, h * HEAD_DIM:(h + 1) * HEAD_DIM] for h in heads], axis=0)
        s_ref[n % 3] = lax.dot_general(kk, q_rows, (((1,), (1,)), ((), ())), preferred_element_type=F32)

    scores(0)
    scores(1)
    outs = []
    for n, (blk, kv) in enumerate(units):
        if n + 2 < len(units):
            scores(n + 2)
        vt_kv = vt_all[kv * HEAD_DIM:(kv + 1) * HEAD_DIM, blk * WINDOW:(blk + 2) * WINDOW]
        variant = jnp.minimum(j, 1) if blk == 0 else 1
        for g in range(GQA_GROUP):
            h = kv * GQA_GROUP + g
            s = jnp.where(from_prev, s_ref[n % 3, 0:WINDOW, g * WINDOW:(g + 1) * WINDOW],
                          s_ref[n % 3, WINDOW:, g * WINDOW:(g + 1) * WINDOW]) + bias_ref[variant, h]
            sink = sink_ref[h]
            m = jnp.maximum(jnp.max(s, axis=0, keepdims=True), sink)
            p = jnp.exp2(s - m)
            denom = jnp.sum(p, axis=0, keepdims=True) + jnp.exp2(sink - m)
            p_keys = jnp.concatenate([jnp.where(from_prev, p, 0.0), jnp.where(from_prev, 0.0, p)], axis=0)
            o_t = jnp.dot(vt_kv, p_keys.astype(BF16), preferred_element_type=F32)
            outs.append(o_t / denom)
        if kv == N_KV_HEADS - 1:
            o_ref[blk * WINDOW:(blk + 1) * WINDOW, :] = jnp.transpose(jnp.concatenate(outs, axis=0)).astype(BF16)
            outs = []


def _attn_prompt(q, k, vt, bias_t, sinks, n_seq, seq):
    tq = ATTN_QB * WINDOW
    nj = seq // tq
    t = n_seq * seq
    cur = lambda b, j: (b * nj + j, 0)
    prev = lambda b, j: (jnp.maximum((b * nj + j) * ATTN_QB - 1, 0), 0)
    cur_t = lambda b, j: (0, b * nj + j)
    prev_t = lambda b, j: (0, jnp.maximum((b * nj + j) * ATTN_QB - 1, 0))
    return pl.pallas_call(
        _attn_kernel,
        grid=(n_seq, nj),
        in_specs=[
            pl.BlockSpec(memory_space=pltpu.SMEM),
            pl.BlockSpec((tq, Q_W), cur),
            pl.BlockSpec((WINDOW, KV_W), prev),
            pl.BlockSpec((tq, KV_W), cur),
            pl.BlockSpec((KV_W, WINDOW), prev_t),
            pl.BlockSpec((KV_W, tq), cur_t),
            pl.BlockSpec((2, N_HEADS, WINDOW, WINDOW), lambda b, j: (0, 0, 0, 0)),
        ],
        out_specs=pl.BlockSpec((tq, Q_W), cur),
        out_shape=jax.ShapeDtypeStruct((t, Q_W), BF16),
        scratch_shapes=[pltpu.VMEM((3, 2 * WINDOW, GQA_GROUP * WINDOW), F32)],
        compiler_params=pltpu.CompilerParams(
            dimension_semantics=("arbitrary", "arbitrary"), vmem_limit_bytes=VMEM_LIMIT),
        name="attn_prompt",
    )(sinks, q, k, k, vt, vt, bias_t)


def _prompt_bias_t():
    r = jnp.arange(WINDOW, dtype=jnp.int32)[None, :]
    c = jnp.arange(WINDOW, dtype=jnp.int32)[:, None]
    from_prev = c > r
    dist = r - c + jnp.where(from_prev, WINDOW, 0)
    slopes = jnp.exp2(-8.0 * jnp.arange(1, N_HEADS + 1, dtype=F32) / N_HEADS)
    later = -LOG2E * slopes[:, None, None] * dist.astype(F32)[None]
    first = jnp.where(from_prev[None], -jnp.inf, later)
    return jnp.stack([first, later])


def _sample_kernel(x_ref, g_ref, win_ref, qg_ref, kg_ref, bd_ref, wp_ref, ps_ref,
                   st_ref, ck_ref, cv_ref, sink_ref, bias_ref, perm_ref,
                   pool_ref, attn_ref, pst_ref, kc_ref, vc_ref, *, ns, pos0):
    u, qn, kn, v = _project(x_ref[...], g_ref[...], win_ref[...], qg_ref[...], kg_ref[...], bd_ref[...])
    pst_ref[:, 0:POOL_STATE - 1, :] = st_ref[:, 1:POOL_STATE, :]
    kc_ref[:, 0:WINDOW - 1, :] = ck_ref[:, 1:WINDOW, :]
    vc_ref[:, 0:WINDOW - 1, :] = cv_ref[:, 1:WINDOW, :]
    for n in range(ns):
        pst_ref[n, POOL_STATE - 1:POOL_STATE, :] = u[n:n + 1, :]
        kc_ref[n, WINDOW - 1:WINDOW, :] = kn[n:n + 1, :]
        vc_ref[n, WINDOW - 1:WINDOW, :] = v[n:n + 1, :]

    d_groups = []
    for gi, w in enumerate(POOL_WINDOWS):
        lo = gi * POOL_GC
        acc = u[:, lo:lo + POOL_GC]
        for back in range(1, w):
            acc = acc + st_ref[:, POOL_STATE - back, lo:lo + POOL_GC]
        d_groups.append(acc / float(min(pos0 + 1, w)) - u[:, lo:lo + POOL_GC])
    pool_ref[...] = _pool_project(d_groups, wp_ref, ps_ref[...]).astype(BF16)

    zeros = jnp.zeros((ns, HEAD_DIM), F32)
    stacked = []
    for h in range(N_HEADS):
        piece = qn[:, h * HEAD_DIM:(h + 1) * HEAD_DIM]
        pair = [piece, zeros] if h < GQA_GROUP else [zeros, piece]
        stacked.append(jnp.concatenate(pair, axis=-1))
    q_hn = jnp.concatenate(stacked, axis=0).astype(BF16)
    q_nh = jnp.dot(perm_ref[0], q_hn, preferred_element_type=F32).astype(BF16)

    keys = kc_ref[...].reshape(ns * WINDOW, KV_W).astype(BF16)
    vals = vc_ref[...].reshape(ns * WINDOW, KV_W).astype(BF16)
    s_all = lax.dot_general(q_nh, keys, (((1,), (1,)), ((), ())), preferred_element_type=F32)
    sink = sink_ref[...]
    bias = bias_ref[...]
    zero_blk = jnp.zeros((N_HEADS, WINDOW), F32)
    p_rows = []
    for n in range(ns):
        s = s_all[n * N_HEADS:(n + 1) * N_HEADS, n * WINDOW:(n + 1) * WINDOW] + bias
        m = jnp.maximum(jnp.max(s, axis=-1, keepdims=True), sink)
        p = jnp.exp2(s - m)
        denom = jnp.sum(p, axis=-1, keepdims=True) + jnp.exp2(sink - m)
        p_rows.append(jnp.concatenate([zero_blk] * n + [p / denom] + [zero_blk] * (ns - 1 - n), axis=-1))
    p_blockdiag = jnp.concatenate(p_rows, axis=0).astype(BF16)
    o_nh = jnp.dot(p_blockdiag, vals, preferred_element_type=F32).astype(BF16)
    o_hn = jnp.dot(perm_ref[1], o_nh, preferred_element_type=F32)
    pieces = []
    for h in range(N_HEADS):
        kv = h // GQA_GROUP
        pieces.append(o_hn[h * ns:(h + 1) * ns, kv * HEAD_DIM:(kv + 1) * HEAD_DIM])
    attn_ref[...] = jnp.concatenate(pieces, axis=-1).astype(BF16)


def _sample_mixer(l, xs, g_attn, w_in, qg, kg, bd, wp, ps, state, ck, cv, sink8, bias_s, pos0):
    n = xs.shape[0]
    ns = 32
    row = lambda i: (i, 0)
    lay = lambda i: (l, 0, 0)
    src = jnp.arange(ns * N_HEADS)
    perm = (((src % N_HEADS) * ns + src // N_HEADS)[:, None] == src[None, :]).astype(BF16)
    perms = jnp.stack([perm, perm.T])
    return pl.pallas_call(
        functools.partial(_sample_kernel, ns=ns, pos0=pos0),
        grid=(n // ns,),
        in_specs=[
            pl.BlockSpec((ns, D_MODEL), row),
            pl.BlockSpec((None, 1, D_MODEL), lay),
            pl.BlockSpec((None, D_MODEL, D_IN), lay),
            pl.BlockSpec((None, 1, Q_W), lay),
            pl.BlockSpec((None, 1, KV_W), lay),
            pl.BlockSpec((256, 256), lambda i: (0, 0)),
            pl.BlockSpec((None, 2, 256, 256), lambda i: (l, 0, 0, 0)),
            pl.BlockSpec((None, 1, POOL_W), lay),
            pl.BlockSpec((ns, POOL_STATE, POOL_W), lambda i: (i, 0, 0)),
            pl.BlockSpec((ns, WINDOW, KV_W), lambda i: (i, 0, 0)),
            pl.BlockSpec((ns, WINDOW, KV_W), lambda i: (i, 0, 0)),
            pl.BlockSpec((N_HEADS, 1), lambda i: (0, 0)),
            pl.BlockSpec((N_HEADS, WINDOW), lambda i: (0, 0)),
            pl.BlockSpec((2, ns * N_HEADS, ns * N_HEADS), lambda i: (0, 0, 0)),
        ],
        out_specs=[
            pl.BlockSpec((ns, POOL_W), row),
            pl.BlockSpec((ns, Q_W), row),
            pl.BlockSpec((ns, POOL_STATE, POOL_W), lambda i: (i, 0, 0)),
            pl.BlockSpec((ns, WINDOW, KV_W), lambda i: (i, 0, 0)),
            pl.BlockSpec((ns, WINDOW, KV_W), lambda i: (i, 0, 0)),
        ],
        out_shape=[
            jax.ShapeDtypeStruct((n, POOL_W), BF16),
            jax.ShapeDtypeStruct((n, Q_W), BF16),
            jax.ShapeDtypeStruct((n, POOL_STATE, POOL_W), F32),
            jax.ShapeDtypeStruct((n, WINDOW, KV_W), F32),
            jax.ShapeDtypeStruct((n, WINDOW, KV_W), F32),
        ],
        compiler_params=pltpu.CompilerParams(
            dimension_semantics=("arbitrary",), vmem_limit_bytes=VMEM_LIMIT),
        name="sample_mixer",
    )(xs, g_attn, w_in, qg, kg, bd, wp, ps, state, ck, cv, sink8, bias_s, perms)


def _merge_router_kernel(pool_ref, attn_ref, x_ref, wout_ref, g_ref, wr_ref, br_ref, utri_ref, cin_ref,
                         x1_ref, h2_ref, route_t_ref, cnt_ref, y_ref, lg_ref):
    i = pl.program_id(0)

    @pl.when(i == 0)
    def _():
        cnt_ref[...] = cin_ref[...]

    tm = x_ref.shape[0]
    rc = tm // MERGE_CHUNKS
    chunks = [slice(ci * rc, (ci + 1) * rc) for ci in range(MERGE_CHUNKS)]
    for rows in chunks:
        y_ref[rows, :] = (jnp.dot(pool_ref[rows, :], wout_ref[0:POOL_W, :], preferred_element_type=F32)
                          + jnp.dot(attn_ref[rows, :], wout_ref[POOL_W:, :], preferred_element_type=F32))
    for rows in chunks:
        x1 = x_ref[rows, :] + y_ref[rows, :]
        x1_ref[rows, :] = x1
        h2 = _rms_bf16(x1, g_ref[...])
        h2_ref[rows, :] = _pack_bf16_pairs(h2)
        lg_ref[rows, :] = jnp.dot(h2, wr_ref[...], preferred_element_type=F32) + br_ref[...]
    logits = lg_ref[...]

    lt = jnp.transpose(logits)
    sub = lax.broadcasted_iota(jnp.int32, (EXPERTS_PER_GROUP, tm), 0)
    neg = -jnp.inf
    big = jnp.int32(EXPERTS_PER_GROUP)
    gl = jnp.where(sub < N_EXPERT_GROUPS, lt[GROUP_LANE0:GROUP_LANE0 + EXPERTS_PER_GROUP, :], neg)
    gmax = jnp.max(gl, axis=0, keepdims=True)
    grp = jnp.min(jnp.where(gl == gmax, sub, big), axis=0, keepdims=True)
    g_w = 1.0 / jnp.sum(jnp.exp(gl - gmax), axis=0, keepdims=True)
    el = lt[(N_EXPERT_GROUPS - 1) * EXPERTS_PER_GROUP:N_EXPERT_GROUPS * EXPERTS_PER_GROUP, :]
    for gi in range(N_EXPERT_GROUPS - 2, -1, -1):
        el = jnp.where(grp == gi, lt[gi * EXPERTS_PER_GROUP:(gi + 1) * EXPERTS_PER_GROUP, :], el)
    v1 = jnp.max(el, axis=0, keepdims=True)
    i1 = jnp.min(jnp.where(el == v1, sub, big), axis=0, keepdims=True)
    el2 = jnp.where(sub == i1, neg, el)
    v2 = jnp.max(el2, axis=0, keepdims=True)
    i2 = jnp.min(jnp.where(el2 == v2, sub, big), axis=0, keepdims=True)
    e21 = jnp.exp(v2 - v1)
    w1 = g_w / (1.0 + e21)
    w2 = g_w * e21 / (1.0 + e21)
    e1 = grp * EXPERTS_PER_GROUP + i1
    e2 = grp * EXPERTS_PER_GROUP + i2

    esub = lax.broadcasted_iota(jnp.int32, (N_EXPERTS, tm), 0)
    oh1 = esub == e1
    oh2 = esub == e2
    c = jnp.where(oh1 | oh2, 1.0, 0.0)
    prefix = jnp.dot(c.astype(BF16), utri_ref[...], preferred_element_type=F32) + cnt_ref[...]
    r1 = jnp.sum(jnp.where(oh1, prefix, 0.0), axis=0, keepdims=True)
    r2 = jnp.sum(jnp.where(oh2, prefix, 0.0), axis=0, keepdims=True)
    cnt_ref[...] = cnt_ref[...] + jnp.sum(c, axis=1, keepdims=True)

    fields = jnp.zeros((ROUTE_FIELDS, tm), F32)
    for idx, val in enumerate((e1.astype(F32), e2.astype(F32), w1, w2, r1, r2)):
        fields = jnp.where(sub == idx, val, fields)
    route_t_ref[...] = fields


def _merge_router(l, pool, attn, x2d, w_out, g_ffn, wr, br, cnt_in, tm):
    t = x2d.shape[0]
    utri = (jnp.arange(tm)[:, None] < jnp.arange(tm)[None, :]).astype(BF16)
    row = lambda i: (i, 0)
    lay = lambda i: (l, 0, 0)
    return pl.pallas_call(
        _merge_router_kernel,
        grid=(t // tm,),
        in_specs=[
            pl.BlockSpec((tm, POOL_W), row),
            pl.BlockSpec((tm, Q_W), row),
            pl.BlockSpec((tm, D_MODEL), row),
            pl.BlockSpec((None, D_MODEL, D_MODEL), lay),
            pl.BlockSpec((None, 1, D_MODEL), lay),
            pl.BlockSpec((None, D_MODEL, LANES), lay),
            pl.BlockSpec((None, 1, LANES), lay),
            pl.BlockSpec((tm, tm), lambda i: (0, 0)),
            pl.BlockSpec((N_EXPERTS, 1), lambda i: (0, 0)),
        ],
        out_specs=[
            pl.BlockSpec((tm, D_MODEL), row),
            pl.BlockSpec((tm, D_MODEL // 2), row),
            pl.BlockSpec((ROUTE_FIELDS, tm), lambda i: (0, i)),
            pl.BlockSpec((N_EXPERTS, 1), lambda i: (0, 0)),
        ],
        out_shape=[
            jax.ShapeDtypeStruct((t, D_MODEL), F32),
            jax.ShapeDtypeStruct((t, D_MODEL // 2), jnp.int32),
            jax.ShapeDtypeStruct((ROUTE_FIELDS, t), F32),
            jax.ShapeDtypeStruct((N_EXPERTS, 1), F32),
        ],
        scratch_shapes=[pltpu.VMEM((tm, D_MODEL), F32), pltpu.VMEM((tm, LANES), F32)],
        compiler_params=pltpu.CompilerParams(
            dimension_semantics=("arbitrary",), vmem_limit_bytes=VMEM_LIMIT),
        name="merge_router",
    )(pool, attn, x2d, w_out, g_ffn, wr, br, utri, cnt_in)


def _moe_kernel(be_ref, rv_ref, nx_ref, sl_ref, first_ref, xd_ref, wg_hbm, wu_hbm, wd_hbm, yd_ref,
                wg_f, wu_f, wd_f, wg_s, wu_s, wd_s, sem, *, layer):
    step = pl.program_id(0)

    def weight_copies(e, s):
        return [pltpu.make_async_copy(w_hbm.at[layer, e], w_f.at[s], sem.at[s, n])
                for n, (w_hbm, w_f) in enumerate(((wg_hbm, wg_f), (wu_hbm, wu_f), (wd_hbm, wd_f)))]

    @pl.when(step == 0)
    def _():
        for s in range(WEIGHT_SLOTS - 1):
            @pl.when(first_ref[s] >= 0)
            def _():
                for c in weight_copies(first_ref[s], s):
                    c.start()

    def enter_expert(i):
        expert, slot = be_ref[i], sl_ref[i]

        @pl.when((i == 0) | (expert != be_ref[jnp.maximum(i - 1, 0)]))
        def _():
            for c in weight_copies(expert, slot):
                c.wait()

            @pl.when(nx_ref[i] >= 0)
            def _():
                for c in weight_copies(nx_ref[i], lax.rem(slot + WEIGHT_SLOTS - 1, WEIGHT_SLOTS)):
                    c.start(priority=1)

            wg_s[...] = wg_f[slot].astype(BF16)
            wu_s[...] = wu_f[slot].astype(BF16)
            wd_s[...] = wd_f[slot].astype(BF16)

    def experts_on(row0, n_rows, rows_valid):
        rows = pl.ds(row0, n_rows)
        row = lax.broadcasted_iota(jnp.int32, (n_rows, D_MODEL // 2), 0)
        x = _unpack_bf16_pairs(jnp.where(row < rows_valid, xd_ref[rows, :], 0)).astype(BF16)
        gate = jnp.dot(x, wg_s[...], preferred_element_type=F32)
        up = jnp.dot(x, wu_s[...], preferred_element_type=F32)
        act = (gate * jax.nn.sigmoid(gate) * up).astype(BF16)
        y = jnp.dot(act, wd_s[...], preferred_element_type=F32)
        yd_ref[rows, :] = _pack_bf16_pairs(y.astype(BF16))

    def experts_ragged(row0, lead_rows, rows_last):
        half = MOE_BM // 2

        @pl.when(rows_last > half)
        def _():
            experts_on(row0, lead_rows + MOE_BM, lead_rows + rows_last)

        @pl.when(rows_last <= half)
        def _():
            experts_on(row0, lead_rows + half, lead_rows + rows_last)
            yd_ref[pl.ds(row0 + lead_rows + half, half), :] = jnp.zeros((half, D_MODEL // 2), jnp.int32)

    def single_block(i, row0):
        enter_expert(i)

        @pl.when(rv_ref[i] > 0)
        def _():
            experts_ragged(row0, 0, rv_ref[i])

        @pl.when(rv_ref[i] <= 0)
        def _():
            yd_ref[pl.ds(row0, MOE_BM), :] = jnp.zeros((MOE_BM, D_MODEL // 2), jnp.int32)

    @pl.when(rv_ref[step * MOE_STEP_BLOCKS] > 0)
    def _():
        for pair in range(MOE_STEP_BLOCKS // 2):
            ia = step * MOE_STEP_BLOCKS + 2 * pair
            ib = ia + 1
            row0 = 2 * pair * MOE_BM
            same = (be_ref[ib] == be_ref[ia]) & (rv_ref[ib] > 0)

            @pl.when(same)
            def _():
                enter_expert(ia)
                experts_ragged(row0, MOE_BM, rv_ref[ib])

            @pl.when(jnp.logical_not(same))
            def _():
                single_block(ia, row0)
                single_block(ib, row0 + MOE_BM)


def _moe_experts(l, block_e, rows_valid, next_e, slot, first_e, xd, w_gate, w_up, w_down):
    n_blocks = xd.shape[0] // MOE_BM
    step_rows = MOE_STEP_BLOCKS * MOE_BM
    row = lambda i, be, rv, nx, sl, fe: (jnp.minimum(i, fe[WEIGHT_SLOTS - 1] - 1), 0)
    return pl.pallas_call(
        functools.partial(_moe_kernel, layer=l),
        grid_spec=pltpu.PrefetchScalarGridSpec(
            num_scalar_prefetch=5,
            grid=(n_blocks // MOE_STEP_BLOCKS,),
            in_specs=[
                pl.BlockSpec((step_rows, D_MODEL // 2), row),
                pl.BlockSpec(memory_space=pl.ANY),
                pl.BlockSpec(memory_space=pl.ANY),
                pl.BlockSpec(memory_space=pl.ANY),
            ],
            out_specs=pl.BlockSpec((step_rows, D_MODEL // 2), row),
            scratch_shapes=[
                pltpu.VMEM((WEIGHT_SLOTS, D_MODEL, EXPERT_FF), F32),
                pltpu.VMEM((WEIGHT_SLOTS, D_MODEL, EXPERT_FF), F32),
                pltpu.VMEM((WEIGHT_SLOTS, EXPERT_FF, D_MODEL), F32),
                pltpu.VMEM((D_MODEL, EXPERT_FF), BF16),
                pltpu.VMEM((D_MODEL, EXPERT_FF), BF16),
                pltpu.VMEM((EXPERT_FF, D_MODEL), BF16),
                pltpu.SemaphoreType.DMA((WEIGHT_SLOTS, 3)),
            ],
        ),
        out_shape=jax.ShapeDtypeStruct((n_blocks * MOE_BM, D_MODEL // 2), jnp.int32),
        compiler_params=pltpu.CompilerParams(
            dimension_semantics=("arbitrary",), vmem_limit_bytes=VMEM_LIMIT),
        name="moe_experts",
    )(block_e, rows_valid, next_e, slot, first_e, xd, w_gate, w_up, w_down)


def _sc_worker_id():
    return lax.axis_index("s") * SC_CORES + lax.axis_index("c")


def _sc_dispatch(hp, hs, dest_p, dest_s, n_rows):
    tp, width = hp.shape
    per_w = tp // SC_WORKERS
    n_ch = per_w // DISP_CH
    n_sw = hs.shape[0] // SAMPLE_CH
    mesh = plsc.VectorSubcoreMesh(core_axis_name="c", subcore_axis_name="s")

    @functools.partial(
        pl.kernel, mesh=mesh,
        out_type=jax.ShapeDtypeStruct((n_rows, width), jnp.int32),
        scratch_types=[
            pltpu.VMEM((2, n_ch, DISP_CH), jnp.int32),
            pltpu.VMEM((2, 1, SAMPLE_CH), jnp.int32),
            pltpu.VMEM((SC_RING, DISP_CH, width), jnp.int32),
            pltpu.SemaphoreType.DMA((SC_RING,)),
            pltpu.SemaphoreType.DMA((SC_RING, 2)),
        ],
        name="sc_dispatch",
    )
    def k(hp_hbm, hs_hbm, dp_hbm, ds_hbm, xd_hbm, idx_v, idxs_v, bufs, rsem, wsem):
        wid = _sc_worker_id()
        base = wid * per_w
        for kk in range(2):
            pltpu.sync_copy(dp_hbm.at[kk, wid], idx_v.at[kk])
        reads = [pltpu.make_async_copy(hp_hbm.at[pl.ds(base + j * DISP_CH, DISP_CH)],
                                       bufs.at[j % SC_RING], rsem.at[j % SC_RING]) for j in range(n_ch)]
        writes = [[pltpu.make_async_copy(bufs.at[j % SC_RING], xd_hbm.at[idx_v.at[kk, j]],
                                         wsem.at[j % SC_RING, kk]) for kk in range(2)] for j in range(n_ch)]
        for j in range(min(SC_RING - 1, n_ch)):
            reads[j].start()
        for j in range(n_ch):
            reads[j].wait()
            for w in writes[j]:
                w.start()
            if j >= 1:
                for w in writes[j - 1]:
                    w.wait()
            if j + SC_RING - 1 < n_ch:
                reads[j + SC_RING - 1].start()
        for w in writes[n_ch - 1]:
            w.wait()

        @pl.when(wid < n_sw)
        def _():
            rows = bufs.at[0, pl.ds(0, SAMPLE_CH)]
            for kk in range(2):
                pltpu.sync_copy(ds_hbm.at[kk, wid], idxs_v.at[kk])
            pltpu.sync_copy(hs_hbm.at[pl.ds(wid * SAMPLE_CH, SAMPLE_CH)], rows)
            for kk in range(2):
                pltpu.sync_copy(rows, xd_hbm.at[idxs_v.at[kk, 0]])

    return k(hp, hs, dest_p, dest_s)


def _sc_sample_gather(yd, dest_s, ts):
    width = yd.shape[1]
    n_sw = ts // SAMPLE_CH
    mesh = plsc.VectorSubcoreMesh(core_axis_name="c", subcore_axis_name="s")

    @functools.partial(
        pl.kernel, mesh=mesh,
        out_type=jax.ShapeDtypeStruct((2, ts, width), yd.dtype),
        scratch_types=[
            pltpu.VMEM((2, 1, SAMPLE_CH), jnp.int32),
            pltpu.VMEM((2, SAMPLE_CH, width), yd.dtype),
        ],
        name="sc_sample_gather",
    )
    def k(yd_hbm, ds_hbm, g_hbm, idxs_v, bufs):
        wid = _sc_worker_id()

        @pl.when(wid < n_sw)
        def _():
            for kk in range(2):
                pltpu.sync_copy(ds_hbm.at[kk, wid], idxs_v.at[kk])
            for kk in range(2):
                pltpu.sync_copy(yd_hbm.at[idxs_v.at[kk, 0]], bufs.at[kk])
                pltpu.sync_copy(bufs.at[kk], g_hbm.at[kk, pl.ds(wid * SAMPLE_CH, SAMPLE_CH)])

    return k(yd, dest_s)


def _sc_combine_gather(yd, dest_p, tp):
    width = yd.shape[1]
    per_w = tp // SC_WORKERS
    n_ch = per_w // COMB_CH
    mesh = plsc.VectorSubcoreMesh(core_axis_name="c", subcore_axis_name="s")

    @functools.partial(
        pl.kernel, mesh=mesh,
        out_type=jax.ShapeDtypeStruct((2, tp, width), yd.dtype),
        scratch_types=[
            pltpu.VMEM((2, n_ch, COMB_CH), jnp.int32),
            pltpu.VMEM((SC_RING, COMB_CH, width), yd.dtype),
            pltpu.SemaphoreType.DMA((SC_RING,)),
            pltpu.SemaphoreType.DMA((SC_RING,)),
        ],
        name="sc_combine_gather",
    )
    def k(yd_hbm, dp_hbm, g_hbm, idx_v, bufs, gsem, wsem):
        wid = _sc_worker_id()
        base = wid * per_w
        for kk in range(2):
            pltpu.sync_copy(dp_hbm.at[kk, wid], idx_v.at[kk])
        items = [(kk, j) for kk in range(2) for j in range(n_ch)]
        n_items = len(items)
        gathers = [pltpu.make_async_copy(yd_hbm.at[idx_v.at[kk, j]], bufs.at[n % SC_RING], gsem.at[n % SC_RING])
                   for n, (kk, j) in enumerate(items)]
        outs = [pltpu.make_async_copy(bufs.at[n % SC_RING], g_hbm.at[kk, pl.ds(base + j * COMB_CH, COMB_CH)],
                                      wsem.at[n % SC_RING]) for n, (kk, j) in enumerate(items)]
        for n in range(min(SC_RING - 1, n_items)):
            gathers[n].start()
        for n in range(n_items):
            gathers[n].wait()
            outs[n].start()
            if n >= 1:
                outs[n - 1].wait()
            if n + SC_RING - 1 < n_items:
                gathers[n + SC_RING - 1].start()
        outs[n_items - 1].wait()

    return k(yd, dest_p)


def _combine_kernel(x1_ref, g_ref, route_t_ref, x2_ref):
    fields = route_t_ref[...]
    tm = fields.shape[1]
    cols = jnp.transpose(jnp.concatenate([fields, jnp.zeros((LANES - ROUTE_FIELDS, tm), F32)], axis=0))
    w1 = cols[:, 2:3]
    w2 = cols[:, 3:4]
    x2_ref[...] = x1_ref[...] + _unpack_bf16_pairs(g_ref[0]) * w1 + _unpack_bf16_pairs(g_ref[1]) * w2


def _combine(x1, g, route_t, row0, tm):
    t = x1.shape[0]
    blk0 = row0 // tm
    row = lambda i: (i, 0)
    return pl.pallas_call(
        _combine_kernel,
        grid=(t // tm,),
        in_specs=[
            pl.BlockSpec((tm, D_MODEL), row),
            pl.BlockSpec((2, tm, D_MODEL // 2), lambda i: (0, blk0 + i, 0)),
            pl.BlockSpec((ROUTE_FIELDS, tm), lambda i: (0, i)),
        ],
        out_specs=pl.BlockSpec((tm, D_MODEL), row),
        out_shape=jax.ShapeDtypeStruct((t, D_MODEL), F32),
        compiler_params=pltpu.CompilerParams(
            dimension_semantics=("arbitrary",), vmem_limit_bytes=VMEM_LIMIT),
        name="combine",
    )(x1, g, route_t)


def _dest_layout(dest, workers, chunk):
    t = dest.shape[1]
    return dest.reshape(2, workers, t // (workers * chunk), chunk)


def _hier_moe(l, h2p, h2s, route_tp, route_ts, counts, w_gate, w_up, w_down):
    tp, ts = h2p.shape[0], h2s.shape[0]
    n_assign = 2 * (tp + ts)
    n_blocks = -(-n_assign // MOE_BM) + N_EXPERTS
    n_blocks = -(-n_blocks // MOE_STEP_BLOCKS) * MOE_STEP_BLOCKS
    pcounts = (counts + MOE_BM - 1) // MOE_BM * MOE_BM
    pends = jnp.cumsum(pcounts)
    poffsets = pends - pcounts
    starts = jnp.arange(n_blocks, dtype=jnp.int32) * MOE_BM
    block_e = jnp.minimum(jnp.sum((pends[None, :] <= starts[:, None]).astype(jnp.int32), axis=1),
                          N_EXPERTS - 1)
    experts = jnp.arange(N_EXPERTS, dtype=jnp.int32)

    def lookup(table, idx):
        return jnp.sum(jnp.where(idx[..., None] == experts, table, 0), axis=-1)

    rows_valid = jnp.clip(lookup(poffsets + counts, block_e) - starts, 0, MOE_BM).astype(jnp.int32)
    used = counts > 0
    last_e = jnp.max(jnp.where(used, jnp.arange(N_EXPERTS, dtype=jnp.int32), 0))
    block_e = jnp.where(rows_valid > 0, block_e, last_e).astype(jnp.int32)
    place = jnp.cumsum(used.astype(jnp.int32)) - 1
    by_place = jnp.sum(jnp.where(used[None, :] & (place[None, :] == experts[:, None]), experts[None, :], 0),
                       axis=1)
    n_used = jnp.sum(used.astype(jnp.int32))

    def at_place(p):
        return jnp.where(p < n_used, lookup(by_place, jnp.minimum(p, N_EXPERTS - 1)), -1).astype(jnp.int32)

    ahead_of = at_place(place + (WEIGHT_SLOTS - 1))
    next_e = lookup(ahead_of, block_e)
    slot = lookup(place % WEIGHT_SLOTS, block_e)
    n_steps_used = -(-(pends[-1] // MOE_BM) // MOE_STEP_BLOCKS)
    first_e = jnp.concatenate([at_place(jnp.arange(WEIGHT_SLOTS - 1, dtype=jnp.int32)),
                               n_steps_used.reshape(1).astype(jnp.int32)])

    def dest_of(route_t):
        return lookup(poffsets, route_t[0:2].astype(jnp.int32)) + route_t[4:6].astype(jnp.int32)

    dest_p, dest_s = dest_of(route_tp), dest_of(route_ts)
    n_sw = ts // SAMPLE_CH
    xd = _sc_dispatch(h2p, h2s, _dest_layout(dest_p, SC_WORKERS, DISP_CH),
                      _dest_layout(dest_s, n_sw, SAMPLE_CH), n_blocks * MOE_BM)
    yd = _moe_experts(l, block_e, rows_valid, next_e, slot, first_e, xd, w_gate, w_up, w_down)
    g_s = _sc_sample_gather(yd, _dest_layout(dest_s, n_sw, SAMPLE_CH), ts)
    g_p = _sc_combine_gather(yd, _dest_layout(dest_p, SC_WORKERS, COMB_CH), tp)
    return g_p, g_s


def kernel(x_prompt, x_sample, state_pool, cache_k_win, cache_v_win, norm_attn_g, w_in, pool_w, pool_scale, q_norm_g, k_norm_g, attn_sinks, w_out, norm_ffn_g, router_group_w, router_group_b, router_expert_w, router_expert_b, w_gate, w_up, w_down):
    n_p, t_p, d = x_prompt.shape
    n_s, t_s, _ = x_sample.shape
    depth = w_in.shape[0]
    lw_s = cache_k_win.shape[2]
    assert t_s == 1 and lw_s == WINDOW and d == D_MODEL
    assert t_p % TM_PROJ == 0 and t_p >= WINDOW

    seg = jnp.arange(256) // HEAD_DIM
    bd = jnp.where(seg[:, None] == seg[None, :], 1.0 / HEAD_DIM, 0.0).astype(BF16)
    slopes = jnp.exp2(-8.0 * jnp.arange(1, N_HEADS + 1, dtype=F32) / N_HEADS)
    bias_p = _prompt_bias_t()
    dist_s = (WINDOW - 1) - jnp.arange(WINDOW, dtype=F32)
    bias_s = -LOG2E * slopes[:, None] * dist_s[None, :]

    wp = jnp.zeros((depth, 2, 256, 256), F32)
    for p in range(2):
        wp = wp.at[:, p, :POOL_GC, :POOL_GC].set(pool_w[:, 2 * p])
        wp = wp.at[:, p, POOL_GC:, POOL_GC:].set(pool_w[:, 2 * p + 1])
    assert GROUP_LANE0 == N_EXPERTS
    lane_pad = LANES - N_EXPERTS - N_EXPERT_GROUPS
    wr = jnp.concatenate([router_expert_w, router_group_w, jnp.zeros((depth, D_MODEL, lane_pad), F32)], axis=-1)
    br = jnp.concatenate([router_expert_b, router_group_b, jnp.zeros((depth, lane_pad), F32)],
                         axis=-1).reshape(depth, 1, LANES)
    lp = dict(
        w_in=w_in.astype(BF16),
        w_out=w_out.astype(BF16),
        g_attn=norm_attn_g.reshape(depth, 1, D_MODEL),
        g_ffn=norm_ffn_g.reshape(depth, 1, D_MODEL),
        qg=(jnp.tile(q_norm_g, (1, N_HEADS)) * (ATTN_SCALE * LOG2E)).reshape(depth, 1, Q_W),
        kg=jnp.tile(k_norm_g, (1, N_KV_HEADS)).reshape(depth, 1, KV_W),
        wp=wp.astype(BF16),
        ps=pool_scale.reshape(depth, 1, POOL_W),
        wr=wr.astype(BF16),
        br=br,
    )

    xp = x_prompt.reshape(n_p * t_p, D_MODEL)
    xs = x_sample.reshape(n_s, D_MODEL)
    lw_p = min(WINDOW, t_p)
    pool_p, kp_new, vp_new = [], [], []
    state_s, ks_new, vs_new = [], [], []
    zero_cnt = jnp.zeros((N_EXPERTS, 1), F32)
    pending = None
    for l in range(depth):
        sinks = attn_sinks[l] * LOG2E
        pool_so, attn_so, state_l, ks_l, vs_l = _sample_mixer(
            l, xs, lp["g_attn"], lp["w_in"], lp["qg"], lp["kg"], bd, lp["wp"], lp["ps"],
            state_pool[l], cache_k_win[l].reshape(n_s, lw_s, KV_W), cache_v_win[l].reshape(n_s, lw_s, KV_W),
            sinks.reshape(N_HEADS, 1), bias_s, PAST_LEN)
        state_s.append(state_l)
        ks_new.append(ks_l)
        vs_new.append(vs_l)
        x1s, h2s, route_ts, cnt_s = _merge_router(
            l, pool_so, attn_so, xs, lp["w_out"], lp["g_ffn"], lp["wr"], lp["br"], zero_cnt, n_s)
        outs = _proj_pool_prompt(
            l, xp if pending is None else pending, n_p, t_p,
            lp["g_attn"], lp["w_in"], lp["qg"], lp["kg"], bd, lp["wp"], lp["ps"])
        if pending is not None:
            xp, outs = outs[0], outs[1:]
        pool_o, q, k, vt, utail, ktail, vtail = outs
        attn_o = _attn_prompt(q, k, vt, bias_p, sinks, n_p, t_p)
        x1p, h2p, route_tp, cnt_all = _merge_router(
            l, pool_o, attn_o, xp, lp["w_out"], lp["g_ffn"], lp["wr"], lp["br"], cnt_s, TM_MERGE)
        pool_p.append(utail[:, 16 - POOL_STATE:, :])
        kp_new.append(ktail)
        vp_new.append(vtail)
        counts = cnt_all[:, 0].astype(jnp.int32)
        g_p, g_s = _hier_moe(l, h2p, h2s, route_tp, route_ts, counts, w_gate, w_up, w_down)
        xs = _combine(x1s, g_s, route_ts, 0, n_s)
        pending = (x1p, g_p, route_tp)
    xp = _combine(*pending, 0, TM_MERGE)
    return (xp.reshape(n_p, t_p, D_MODEL), xs.reshape(n_s, t_s, D_MODEL),
            jnp.stack(pool_p),
            jnp.stack(kp_new).reshape(depth, n_p, lw_p, N_KV_HEADS, HEAD_DIM),
            jnp.stack(vp_new).reshape(depth, n_p, lw_p, N_KV_HEADS, HEAD_DIM),
            jnp.stack(state_s),
            jnp.stack(ks_new).reshape(depth, n_s, lw_s, N_KV_HEADS, HEAD_DIM),
            jnp.stack(vs_new).reshape(depth, n_s, lw_s, N_KV_HEADS, HEAD_DIM))
```

```python
import functools

import jax
import jax.numpy as jnp
from jax import lax
from jax.experimental import pallas as pl
from jax.experimental.pallas import tpu as pltpu
from jax.experimental.pallas import tpu_sc as plsc

D_MODEL = 1024
POOL_W = 512
POOL_WINDOWS = (2, 4, 8, 16)
POOL_GC = 128
POOL_STATE = 15
HEAD_DIM = 64
N_HEADS = 8
N_KV_HEADS = 2
GQA_GROUP = 4
Q_W = 512
KV_W = 128
D_IN = POOL_W + Q_W + 2 * KV_W
WINDOW = 128
ATTN_SCALE = HEAD_DIM ** -0.5
LOG2E = 1.4426950408889634
N_EXPERT_GROUPS = 4
EXPERTS_PER_GROUP = 8
N_EXPERTS = 32
EXPERT_FF = 512
EPS = 1e-6
PAST_LEN = 16384

LANES = 128
HALO = 32
TM_PROJ = 1024
TM_MERGE = 1024
MERGE_CHUNKS = 4
ATTN_QB = 16
MOE_BM = 256
MOE_STEP_BLOCKS = 4
WEIGHT_SLOTS = 3
GROUP_LANE0 = 32
ROUTE_FIELDS = 8
SC_CORES = 2
SC_SUBCORES = 16
SC_WORKERS = SC_CORES * SC_SUBCORES
DISP_CH = 64
COMB_CH = 64
SAMPLE_CH = 32
SC_RING = 3
VMEM_LIMIT = 48 * 1024 * 1024

BF16 = jnp.bfloat16
F32 = jnp.float32


def _pack_bf16_pairs(h):
    w = h.shape[1] // 2
    hi = lax.bitcast_convert_type(h[:, :w].astype(F32), jnp.uint32)
    lo = lax.bitcast_convert_type(h[:, w:].astype(F32), jnp.uint32)
    return lax.bitcast_convert_type(hi | (lo >> 16), jnp.int32)


def _unpack_bf16_pairs(words):
    u = lax.bitcast_convert_type(words, jnp.uint32)
    hi = lax.bitcast_convert_type(u & jnp.uint32(0xFFFF0000), F32)
    lo = lax.bitcast_convert_type(u << 16, F32)
    return jnp.concatenate([hi, lo], axis=-1)


def _segment_mean_sq(a, bd):
    w = a.shape[1]
    return jnp.dot((a * a).astype(BF16), bd[:w, :w], preferred_element_type=F32)


def _rms_bf16(x, g):
    ms = jnp.mean(x * x, axis=-1, keepdims=True)
    return (x * lax.rsqrt(ms + EPS) * g).astype(BF16)


def _qk_norm(q, k, qg, kg, bd):
    qn = []
    for c in range(Q_W // 256):
        qc = q[:, c * 256:(c + 1) * 256]
        qn.append(qc * lax.rsqrt(_segment_mean_sq(qc, bd) + EPS))
    qn = jnp.concatenate(qn, axis=-1) * qg
    kn = k * lax.rsqrt(_segment_mean_sq(k, bd) + EPS) * kg
    return qn, kn


def _project(x, g, w_in, qg, kg, bd):
    z = jnp.dot(_rms_bf16(x, g), w_in, preferred_element_type=F32)
    u = z[:, :POOL_W]
    q = z[:, POOL_W:POOL_W + Q_W]
    k = z[:, POOL_W + Q_W:POOL_W + Q_W + KV_W]
    v = z[:, POOL_W + Q_W + KV_W:]
    qn, kn = _qk_norm(q, k, qg, kg, bd)
    return u, qn, kn, v


def _pool_project(d_groups, wp_ref, ps):
    outs = []
    for p in range(2):
        dp = jnp.concatenate([d_groups[2 * p], d_groups[2 * p + 1]], axis=-1).astype(BF16)
        y = jnp.dot(dp, wp_ref[p], preferred_element_type=F32)
        outs.append(y * ps[:, p * 256:(p + 1) * 256])
    return jnp.concatenate(outs, axis=-1)


def _proj_pool_kernel(x_ref, g_ref, win_ref, qg_ref, kg_ref, bd_ref, wp_ref, ps_ref,
                      pool_ref, q_ref, k_ref, vt_ref, utail_ref, ktail_ref, vtail_ref,
                      ext_ref, sa_ref, sb_ref, zq_ref, *, tm, n_j):
    j = pl.program_id(1)

    @pl.when(j == 0)
    def _():
        ext_ref[0:HALO, :] = jnp.zeros((HALO, POOL_W), F32)

    r = tm + HALO
    h = _rms_bf16(x_ref[...], g_ref[...])
    ext_ref[HALO:r, :] = jnp.dot(h, win_ref[:, 0:POOL_W], preferred_element_type=F32)
    zq_ref[...] = jnp.dot(h, win_ref[:, POOL_W:], preferred_element_type=F32)
    u = ext_ref[HALO:r, :]
    sa_ref[8:r, :] = ext_ref[8:r, :] + ext_ref[7:r - 1, :]
    sb_ref[16:r, 128:] = sa_ref[16:r, 128:] + sa_ref[14:r - 2, 128:]
    sa_ref[24:r, 256:] = sb_ref[24:r, 256:] + sb_ref[20:r - 4, 256:]
    sb_ref[32:r, 384:] = sa_ref[32:r, 384:] + sa_ref[24:r - 8, 384:]
    pos1 = j * tm + lax.broadcasted_iota(jnp.int32, (tm, POOL_GC), 0) + 1
    sums = (sa_ref, sb_ref, sa_ref, sb_ref)
    d_groups = []
    for gi, w in enumerate(POOL_WINDOWS):
        sl = slice(gi * POOL_GC, (gi + 1) * POOL_GC)
        cnt = jnp.minimum(pos1, w).astype(F32)
        d_groups.append(sums[gi][HALO:r, sl] / cnt - u[:, sl])
    pool_ref[...] = _pool_project(d_groups, wp_ref, ps_ref[...]).astype(BF16)
    ext_ref[16:HALO, :] = ext_ref[tm + 16:r, :]

    qn, kn = _qk_norm(zq_ref[:, 0:Q_W], zq_ref[:, Q_W:Q_W + KV_W], qg_ref[...], kg_ref[...], bd_ref[...])
    v = zq_ref[:, Q_W + KV_W:]
    q_ref[...] = qn.astype(BF16)
    k_ref[...] = kn.astype(BF16)
    vt_ref[...] = jnp.transpose(v).astype(BF16)

    @pl.when(j == n_j - 1)
    def _():
        utail_ref[...] = u[tm - 16:, :]
        ktail_ref[...] = kn[tm - WINDOW:, :]
        vtail_ref[...] = v[tm - WINDOW:, :]


def _proj_pool_combine_kernel(x1_ref, gath_ref, route_ref, *rest, tm, n_j):
    x2_ref = rest[7]
    _combine_kernel(x1_ref, gath_ref, route_ref, x2_ref)
    _proj_pool_kernel(x2_ref, *rest[:7], *rest[8:], tm=tm, n_j=n_j)


def _proj_pool_prompt(l, x_in, n_seq, seq, g_attn, w_in, qg, kg, bd, wp, ps):
    tm = TM_PROJ
    n_j = seq // tm
    t = n_seq * seq
    row = lambda b, j: (b * n_j + j, 0)
    lay = lambda b, j: (l, 0, 0)
    fused = isinstance(x_in, tuple)
    if fused:
        kern = _proj_pool_combine_kernel
        x_args = list(x_in)
        x_specs = [pl.BlockSpec((tm, D_MODEL), row),
                   pl.BlockSpec((2, tm, D_MODEL // 2), lambda b, j: (0, b * n_j + j, 0)),
                   pl.BlockSpec((ROUTE_FIELDS, tm), lambda b, j: (0, b * n_j + j))]
        x_out_specs = [pl.BlockSpec((tm, D_MODEL), row)]
        x_out_shape = [jax.ShapeDtypeStruct((t, D_MODEL), F32)]
    else:
        kern = _proj_pool_kernel
        x_args = [x_in]
        x_specs = [pl.BlockSpec((tm, D_MODEL), row)]
        x_out_specs, x_out_shape = [], []
    return pl.pallas_call(
        functools.partial(kern, tm=tm, n_j=n_j),
        grid=(n_seq, n_j),
        in_specs=x_specs + [
            pl.BlockSpec((None, 1, D_MODEL), lay),
            pl.BlockSpec((None, D_MODEL, D_IN), lay),
            pl.BlockSpec((None, 1, Q_W), lay),
            pl.BlockSpec((None, 1, KV_W), lay),
            pl.BlockSpec((256, 256), lambda b, j: (0, 0)),
            pl.BlockSpec((None, 2, 256, 256), lambda b, j: (l, 0, 0, 0)),
            pl.BlockSpec((None, 1, POOL_W), lay),
        ],
        out_specs=x_out_specs + [
            pl.BlockSpec((tm, POOL_W), row),
            pl.BlockSpec((tm, Q_W), row),
            pl.BlockSpec((tm, KV_W), row),
            pl.BlockSpec((KV_W, tm), lambda b, j: (0, b * n_j + j)),
            pl.BlockSpec((None, 16, POOL_W), lambda b, j: (b, 0, 0)),
            pl.BlockSpec((None, WINDOW, KV_W), lambda b, j: (b, 0, 0)),
            pl.BlockSpec((None, WINDOW, KV_W), lambda b, j: (b, 0, 0)),
        ],
        out_shape=x_out_shape + [
            jax.ShapeDtypeStruct((t, POOL_W), BF16),
            jax.ShapeDtypeStruct((t, Q_W), BF16),
            jax.ShapeDtypeStruct((t, KV_W), BF16),
            jax.ShapeDtypeStruct((KV_W, t), BF16),
            jax.ShapeDtypeStruct((n_seq, 16, POOL_W), F32),
            jax.ShapeDtypeStruct((n_seq, WINDOW, KV_W), F32),
            jax.ShapeDtypeStruct((n_seq, WINDOW, KV_W), F32),
        ],
        scratch_shapes=[pltpu.VMEM((tm + HALO, POOL_W), F32)] * 3 + [pltpu.VMEM((tm, Q_W + 2 * KV_W), F32)],
        compiler_params=pltpu.CompilerParams(
            dimension_semantics=("arbitrary", "arbitrary"), vmem_limit_bytes=VMEM_LIMIT),
        name="proj_pool_prompt",
    )(*x_args, g_attn, w_in, qg, kg, bd, wp, ps)


def _attn_kernel(sink_ref, q_ref, kp_ref, kc_ref, vtp_ref, vtc_ref, bias_ref, o_ref, s_ref):
    j = pl.program_id(1)
    kk_all = jnp.concatenate([kp_ref[...], kc_ref[...]], axis=0)
    vt_all = jnp.concatenate([vtp_ref[...], vtc_ref[...]], axis=1)
    from_prev = (lax.broadcasted_iota(jnp.int32, (WINDOW, WINDOW), 0)
                 > lax.broadcasted_iota(jnp.int32, (WINDOW, WINDOW), 1))
    units = [(blk, kv) for blk in range(ATTN_QB) for kv in range(N_KV_HEADS)]

    def scores(n):
        blk, kv = units[n]
        q = q_ref[blk * WINDOW:(blk + 1) * WINDOW, :]
        kk = kk_all[blk * WINDOW:(blk + 2) * WINDOW, kv * HEAD_DIM:(kv + 1) * HEAD_DIM]
        heads = range(kv * GQA_GROUP, (kv + 1) * GQA_GROUP)
        q_rows = jnp.concatenate([q[:, h * HEAD_DIM:(h + 1) * HEAD_DIM] for h in heads], axis=0)
        s_ref[n % 3] = lax.dot_general(kk, q_rows, (((1,), (1,)), ((), ())), preferred_element_type=F32)

    scores(0)
    scores(1)
    outs = []
    for n, (blk, kv) in enumerate(units):
        if n + 2 < len(units):
            scores(n + 2)
        vt_kv = vt_all[kv * HEAD_DIM:(kv + 1) * HEAD_DIM, blk * WINDOW:(blk + 2) * WINDOW]
        variant = jnp.minimum(j, 1) if blk == 0 else 1
        for g in range(GQA_GROUP):
            h = kv * GQA_GROUP + g
            s = jnp.where(from_prev, s_ref[n % 3, 0:WINDOW, g * WINDOW:(g + 1) * WINDOW],
                          s_ref[n % 3, WINDOW:, g * WINDOW:(g + 1) * WINDOW]) + bias_ref[variant, h]
            sink = sink_ref[h]
            m = jnp.maximum(jnp.max(s, axis=0, keepdims=True), sink)
            p = jnp.exp2(s - m)
            denom = jnp.sum(p, axis=0, keepdims=True) + jnp.exp2(sink - m)
            p_keys = jnp.concatenate([jnp.where(from_prev, p, 0.0), jnp.where(from_prev, 0.0, p)], axis=0)
            o_t = jnp.dot(vt_kv, p_keys.astype(BF16), preferred_element_type=F32)
            outs.append(o_t / denom)
        if kv == N_KV_HEADS - 1:
            o_ref[blk * WINDOW:(blk + 1) * WINDOW, :] = jnp.transpose(jnp.concatenate(outs, axis=0)).astype(BF16)
            outs = []


def _attn_prompt(q, k, vt, bias_t, sinks, n_seq, seq):
    tq = ATTN_QB * WINDOW
    nj = seq // tq
    t = n_seq * seq
    cur = lambda b, j: (b * nj + j, 0)
    prev = lambda b, j: (jnp.maximum((b * nj + j) * ATTN_QB - 1, 0), 0)
    cur_t = lambda b, j: (0, b * nj + j)
    prev_t = lambda b, j: (0, jnp.maximum((b * nj + j) * ATTN_QB - 1, 0))
    return pl.pallas_call(
        _attn_kernel,
        grid=(n_seq, nj),
        in_specs=[
            pl.BlockSpec(memory_space=pltpu.SMEM),
            pl.BlockSpec((tq, Q_W), cur),
            pl.BlockSpec((WINDOW, KV_W), prev),
            pl.BlockSpec((tq, KV_W), cur),
            pl.BlockSpec((KV_W, WINDOW), prev_t),
            pl.BlockSpec((KV_W, tq), cur_t),
            pl.BlockSpec((2, N_HEADS, WINDOW, WINDOW), lambda b, j: (0, 0, 0, 0)),
        ],
        out_specs=pl.BlockSpec((tq, Q_W), cur),
        out_shape=jax.ShapeDtypeStruct((t, Q_W), BF16),
        scratch_shapes=[pltpu.VMEM((3, 2 * WINDOW, GQA_GROUP * WINDOW), F32)],
        compiler_params=pltpu.CompilerParams(
            dimension_semantics=("arbitrary", "arbitrary"), vmem_limit_bytes=VMEM_LIMIT),
        name="attn_prompt",
    )(sinks, q, k, k, vt, vt, bias_t)


def _prompt_bias_t():
    r = jnp.arange(WINDOW, dtype=jnp.int32)[None, :]
    c = jnp.arange(WINDOW, dtype=jnp.int32)[:, None]
    from_prev = c > r
    dist = r - c + jnp.where(from_prev, WINDOW, 0)
    slopes = jnp.exp2(-8.0 * jnp.arange(1, N_HEADS + 1, dtype=F32) / N_HEADS)
    later = -LOG2E * slopes[:, None, None] * dist.astype(F32)[None]
    first = jnp.where(from_prev[None], -jnp.inf, later)
    return jnp.stack([first, later])


def _sample_kernel(x_ref, g_ref, win_ref, qg_ref, kg_ref, bd_ref, wp_ref, ps_ref,
                   st_ref, ck_ref, cv_ref, sink_ref, bias_ref, perm_ref,
                   pool_ref, attn_ref, pst_ref, kc_ref, vc_ref, *, ns, pos0):
    u, qn, kn, v = _project(x_ref[...], g_ref[...], win_ref[...], qg_ref[...], kg_ref[...], bd_ref[...])
    pst_ref[:, 0:POOL_STATE - 1, :] = st_ref[:, 1:POOL_STATE, :]
    kc_ref[:, 0:WINDOW - 1, :] = ck_ref[:, 1:WINDOW, :]
    vc_ref[:, 0:WINDOW - 1, :] = cv_ref[:, 1:WINDOW, :]
    for n in range(ns):
        pst_ref[n, POOL_STATE - 1:POOL_STATE, :] = u[n:n + 1, :]
        kc_ref[n, WINDOW - 1:WINDOW, :] = kn[n:n + 1, :]
        vc_ref[n, WINDOW - 1:WINDOW, :] = v[n:n + 1, :]

    d_groups = []
    for gi, w in enumerate(POOL_WINDOWS):
        lo = gi * POOL_GC
        acc = u[:, lo:lo + POOL_GC]
        for back in range(1, w):
            acc = acc + st_ref[:, POOL_STATE - back, lo:lo + POOL_GC]
        d_groups.append(acc / float(min(pos0 + 1, w)) - u[:, lo:lo + POOL_GC])
    pool_ref[...] = _pool_project(d_groups, wp_ref, ps_ref[...]).astype(BF16)

    zeros = jnp.zeros((ns, HEAD_DIM), F32)
    stacked = []
    for h in range(N_HEADS):
        piece = qn[:, h * HEAD_DIM:(h + 1) * HEAD_DIM]
        pair = [piece, zeros] if h < GQA_GROUP else [zeros, piece]
        stacked.append(jnp.concatenate(pair, axis=-1))
    q_hn = jnp.concatenate(stacked, axis=0).astype(BF16)
    q_nh = jnp.dot(perm_ref[0], q_hn, preferred_element_type=F32).astype(BF16)

    keys = kc_ref[...].reshape(ns * WINDOW, KV_W).astype(BF16)
    vals = vc_ref[...].reshape(ns * WINDOW, KV_W).astype(BF16)
    s_all = lax.dot_general(q_nh, keys, (((1,), (1,)), ((), ())), preferred_element_type=F32)
    sink = sink_ref[...]
    bias = bias_ref[...]
    zero_blk = jnp.zeros((N_HEADS, WINDOW), F32)
    p_rows = []
    for n in range(ns):
        s = s_all[n * N_HEADS:(n + 1) * N_HEADS, n * WINDOW:(n + 1) * WINDOW] + bias
        m = jnp.maximum(jnp.max(s, axis=-1, keepdims=True), sink)
        p = jnp.exp2(s - m)
        denom = jnp.sum(p, axis=-1, keepdims=True) + jnp.exp2(sink - m)
        p_rows.append(jnp.concatenate([zero_blk] * n + [p / denom] + [zero_blk] * (ns - 1 - n), axis=-1))
    p_blockdiag = jnp.concatenate(p_rows, axis=0).astype(BF16)
    o_nh = jnp.dot(p_blockdiag, vals, preferred_element_type=F32).astype(BF16)
    o_hn = jnp.dot(perm_ref[1], o_nh, preferred_element_type=F32)
    pieces = []
    for h in range(N_HEADS):
        kv = h // GQA_GROUP
        pieces.append(o_hn[h * ns:(h + 1) * ns, kv * HEAD_DIM:(kv + 1) * HEAD_DIM])
    attn_ref[...] = jnp.concatenate(pieces, axis=-1).astype(BF16)


def _sample_mixer(l, depth, xs, g_attn, w_in, qg, kg, bd, wp, ps, state, ck, cv, sink8, bias_s, pos0):
    n = xs.shape[0]
    ns = 32
    row = lambda i: (i, 0)
    lay = lambda i: (l, 0, 0)
    src = jnp.arange(ns * N_HEADS)
    perm = (((src % N_HEADS) * ns + src // N_HEADS)[:, None] == src[None, :]).astype(BF16)
    perms = jnp.stack([perm, perm.T])
    return pl.pallas_call(
        functools.partial(_sample_kernel, ns=ns, pos0=pos0),
        grid=(n // ns,),
        input_output_aliases={8: 2, 9: 3, 10: 4},
        in_specs=[
            pl.BlockSpec((ns, D_MODEL), row),
            pl.BlockSpec((None, 1, D_MODEL), lay),
            pl.BlockSpec((None, D_MODEL, D_IN), lay),
            pl.BlockSpec((None, 1, Q_W), lay),
            pl.BlockSpec((None, 1, KV_W), lay),
            pl.BlockSpec((256, 256), lambda i: (0, 0)),
            pl.BlockSpec((None, 2, 256, 256), lambda i: (l, 0, 0, 0)),
            pl.BlockSpec((None, 1, POOL_W), lay),
            pl.BlockSpec((None, ns, POOL_STATE, POOL_W), lambda i: (l, i, 0, 0)),
            pl.BlockSpec((None, ns, WINDOW, KV_W), lambda i: (l, i, 0, 0)),
            pl.BlockSpec((None, ns, WINDOW, KV_W), lambda i: (l, i, 0, 0)),
            pl.BlockSpec((N_HEADS, 1), lambda i: (0, 0)),
            pl.BlockSpec((N_HEADS, WINDOW), lambda i: (0, 0)),
            pl.BlockSpec((2, ns * N_HEADS, ns * N_HEADS), lambda i: (0, 0, 0)),
        ],
        out_specs=[
            pl.BlockSpec((ns, POOL_W), row),
            pl.BlockSpec((ns, Q_W), row),
            pl.BlockSpec((None, ns, POOL_STATE, POOL_W), lambda i: (l, i, 0, 0)),
            pl.BlockSpec((None, ns, WINDOW, KV_W), lambda i: (l, i, 0, 0)),
            pl.BlockSpec((None, ns, WINDOW, KV_W), lambda i: (l, i, 0, 0)),
        ],
        out_shape=[
            jax.ShapeDtypeStruct((n, POOL_W), BF16),
            jax.ShapeDtypeStruct((n, Q_W), BF16),
            jax.ShapeDtypeStruct((depth, n, POOL_STATE, POOL_W), F32),
            jax.ShapeDtypeStruct((depth, n, WINDOW, KV_W), F32),
            jax.ShapeDtypeStruct((depth, n, WINDOW, KV_W), F32),
        ],
        compiler_params=pltpu.CompilerParams(
            dimension_semantics=("arbitrary",), vmem_limit_bytes=VMEM_LIMIT),
        name="sample_mixer",
    )(xs, g_attn, w_in, qg, kg, bd, wp, ps, state, ck, cv, sink8, bias_s, perms)


def _merge_router_kernel(pool_ref, attn_ref, x_ref, wout_ref, g_ref, wr_ref, br_ref, utri_ref, cin_ref,
                         x1_ref, h2_ref, route_t_ref, cnt_ref, y_ref, lg_ref):
    i = pl.program_id(0)

    @pl.when(i == 0)
    def _():
        cnt_ref[...] = cin_ref[...]

    tm = x_ref.shape[0]
    rc = tm // MERGE_CHUNKS
    chunks = [slice(ci * rc, (ci + 1) * rc) for ci in range(MERGE_CHUNKS)]
    for rows in chunks:
        y_ref[rows, :] = (jnp.dot(pool_ref[rows, :], wout_ref[0:POOL_W, :], preferred_element_type=F32)
                          + jnp.dot(attn_ref[rows, :], wout_ref[POOL_W:, :], preferred_element_type=F32))
    for rows in chunks:
        x1 = x_ref[rows, :] + y_ref[rows, :]
        x1_ref[rows, :] = x1
        h2 = _rms_bf16(x1, g_ref[...])
        h2_ref[rows, :] = _pack_bf16_pairs(h2)
        lg_ref[rows, :] = jnp.dot(h2, wr_ref[...], preferred_element_type=F32) + br_ref[...]
    logits = lg_ref[...]

    lt = jnp.transpose(logits)
    sub = lax.broadcasted_iota(jnp.int32, (EXPERTS_PER_GROUP, tm), 0)
    neg = -jnp.inf
    big = jnp.int32(EXPERTS_PER_GROUP)
    gl = jnp.where(sub < N_EXPERT_GROUPS, lt[GROUP_LANE0:GROUP_LANE0 + EXPERTS_PER_GROUP, :], neg)
    gmax = jnp.max(gl, axis=0, keepdims=True)
    grp = jnp.min(jnp.where(gl == gmax, sub, big), axis=0, keepdims=True)
    g_w = 1.0 / jnp.sum(jnp.exp(gl - gmax), axis=0, keepdims=True)
    el = lt[(N_EXPERT_GROUPS - 1) * EXPERTS_PER_GROUP:N_EXPERT_GROUPS * EXPERTS_PER_GROUP, :]
    for gi in range(N_EXPERT_GROUPS - 2, -1, -1):
        el = jnp.where(grp == gi, lt[gi * EXPERTS_PER_GROUP:(gi + 1) * EXPERTS_PER_GROUP, :], el)
    v1 = jnp.max(el, axis=0, keepdims=True)
    i1 = jnp.min(jnp.where(el == v1, sub, big), axis=0, keepdims=True)
    el2 = jnp.where(sub == i1, neg, el)
    v2 = jnp.max(el2, axis=0, keepdims=True)
    i2 = jnp.min(jnp.where(el2 == v2, sub, big), axis=0, keepdims=True)
    e21 = jnp.exp(v2 - v1)
    w1 = g_w / (1.0 + e21)
    w2 = g_w * e21 / (1.0 + e21)
    e1 = grp * EXPERTS_PER_GROUP + i1
    e2 = grp * EXPERTS_PER_GROUP + i2

    esub = lax.broadcasted_iota(jnp.int32, (N_EXPERTS, tm), 0)
    oh1 = esub == e1
    oh2 = esub == e2
    c = jnp.where(oh1 | oh2, 1.0, 0.0)
    prefix = jnp.dot(c.astype(BF16), utri_ref[...], preferred_element_type=F32) + cnt_ref[...]
    r1 = jnp.sum(jnp.where(oh1, prefix, 0.0), axis=0, keepdims=True)
    r2 = jnp.sum(jnp.where(oh2, prefix, 0.0), axis=0, keepdims=True)
    cnt_ref[...] = cnt_ref[...] + jnp.sum(c, axis=1, keepdims=True)

    fields = jnp.zeros((ROUTE_FIELDS, tm), F32)
    for idx, val in enumerate((e1.astype(F32), e2.astype(F32), w1, w2, r1, r2)):
        fields = jnp.where(sub == idx, val, fields)
    route_t_ref[...] = fields


def _merge_router(l, pool, attn, x2d, w_out, g_ffn, wr, br, cnt_in, tm):
    t = x2d.shape[0]
    utri = (jnp.arange(tm)[:, None] < jnp.arange(tm)[None, :]).astype(BF16)
    row = lambda i: (i, 0)
    lay = lambda i: (l, 0, 0)
    return pl.pallas_call(
        _merge_router_kernel,
        grid=(t // tm,),
        in_specs=[
            pl.BlockSpec((tm, POOL_W), row),
            pl.BlockSpec((tm, Q_W), row),
            pl.BlockSpec((tm, D_MODEL), row),
            pl.BlockSpec((None, D_MODEL, D_MODEL), lay),
            pl.BlockSpec((None, 1, D_MODEL), lay),
            pl.BlockSpec((None, D_MODEL, LANES), lay),
            pl.BlockSpec((None, 1, LANES), lay),
            pl.BlockSpec((tm, tm), lambda i: (0, 0)),
            pl.BlockSpec((N_EXPERTS, 1), lambda i: (0, 0)),
        ],
        out_specs=[
            pl.BlockSpec((tm, D_MODEL), row),
            pl.BlockSpec((tm, D_MODEL // 2), row),
            pl.BlockSpec((ROUTE_FIELDS, tm), lambda i: (0, i)),
            pl.BlockSpec((N_EXPERTS, 1), lambda i: (0, 0)),
        ],
        out_shape=[
            jax.ShapeDtypeStruct((t, D_MODEL), F32),
            jax.ShapeDtypeStruct((t, D_MODEL // 2), jnp.int32),
            jax.ShapeDtypeStruct((ROUTE_FIELDS, t), F32),
            jax.ShapeDtypeStruct((N_EXPERTS, 1), F32),
        ],
        scratch_shapes=[pltpu.VMEM((tm, D_MODEL), F32), pltpu.VMEM((tm, LANES), F32)],
        compiler_params=pltpu.CompilerParams(
            dimension_semantics=("arbitrary",), vmem_limit_bytes=VMEM_LIMIT),
        name="merge_router",
    )(pool, attn, x2d, w_out, g_ffn, wr, br, utri, cnt_in)


def _moe_kernel(be_ref, rv_ref, nx_ref, sl_ref, first_ref, xd_ref, wg_hbm, wu_hbm, wd_hbm, yd_ref,
                wg_f, wu_f, wd_f, wg_s, wu_s, wd_s, sem, *, layer):
    step = pl.program_id(0)

    def weight_copies(e, s):
        return [pltpu.make_async_copy(w_hbm.at[layer, e], w_f.at[s], sem.at[s, n])
                for n, (w_hbm, w_f) in enumerate(((wg_hbm, wg_f), (wu_hbm, wu_f), (wd_hbm, wd_f)))]

    @pl.when(step == 0)
    def _():
        for s in range(WEIGHT_SLOTS - 1):
            @pl.when(first_ref[s] >= 0)
            def _():
                for c in weight_copies(first_ref[s], s):
                    c.start()

    def enter_expert(i):
        expert, slot = be_ref[i], sl_ref[i]

        @pl.when((i == 0) | (expert != be_ref[jnp.maximum(i - 1, 0)]))
        def _():
            for c in weight_copies(expert, slot):
                c.wait()

            @pl.when(nx_ref[i] >= 0)
            def _():
                for c in weight_copies(nx_ref[i], lax.rem(slot + WEIGHT_SLOTS - 1, WEIGHT_SLOTS)):
                    c.start(priority=1)

            wg_s[...] = wg_f[slot].astype(BF16)
            wu_s[...] = wu_f[slot].astype(BF16)
            wd_s[...] = wd_f[slot].astype(BF16)

    def experts_on(row0, n_rows, rows_valid):
        rows = pl.ds(row0, n_rows)
        row = lax.broadcasted_iota(jnp.int32, (n_rows, D_MODEL // 2), 0)
        x = _unpack_bf16_pairs(jnp.where(row < rows_valid, xd_ref[rows, :], 0)).astype(BF16)
        gate = jnp.dot(x, wg_s[...], preferred_element_type=F32)
        up = jnp.dot(x, wu_s[...], preferred_element_type=F32)
        act = (gate * jax.nn.sigmoid(gate) * up).astype(BF16)
        y = jnp.dot(act, wd_s[...], preferred_element_type=F32)
        yd_ref[rows, :] = _pack_bf16_pairs(y.astype(BF16))

    def experts_ragged(row0, lead_rows, rows_last):
        half = MOE_BM // 2

        @pl.when(rows_last > half)
        def _():
            experts_on(row0, lead_rows + MOE_BM, lead_rows + rows_last)

        @pl.when(rows_last <= half)
        def _():
            experts_on(row0, lead_rows + half, lead_rows + rows_last)
            yd_ref[pl.ds(row0 + lead_rows + half, half), :] = jnp.zeros((half, D_MODEL // 2), jnp.int32)

    def single_block(i, row0):
        enter_expert(i)

        @pl.when(rv_ref[i] > 0)
        def _():
            experts_ragged(row0, 0, rv_ref[i])

        @pl.when(rv_ref[i] <= 0)
        def _():
            yd_ref[pl.ds(row0, MOE_BM), :] = jnp.zeros((MOE_BM, D_MODEL // 2), jnp.int32)

    @pl.when(rv_ref[step * MOE_STEP_BLOCKS] > 0)
    def _():
        for pair in range(MOE_STEP_BLOCKS // 2):
            ia = step * MOE_STEP_BLOCKS + 2 * pair
            ib = ia + 1
            row0 = 2 * pair * MOE_BM
            same = (be_ref[ib] == be_ref[ia]) & (rv_ref[ib] > 0)

            @pl.when(same)
            def _():
                enter_expert(ia)
                experts_ragged(row0, MOE_BM, rv_ref[ib])

            @pl.when(jnp.logical_not(same))
            def _():
                single_block(ia, row0)
                single_block(ib, row0 + MOE_BM)


def _moe_experts(l, block_e, rows_valid, next_e, slot, first_e, xd, w_gate, w_up, w_down):
    n_blocks = xd.shape[0] // MOE_BM
    step_rows = MOE_STEP_BLOCKS * MOE_BM
    row = lambda i, be, rv, nx, sl, fe: (jnp.minimum(i, fe[WEIGHT_SLOTS - 1] - 1), 0)
    return pl.pallas_call(
        functools.partial(_moe_kernel, layer=l),
        grid_spec=pltpu.PrefetchScalarGridSpec(
            num_scalar_prefetch=5,
            grid=(n_blocks // MOE_STEP_BLOCKS,),
            in_specs=[
                pl.BlockSpec((step_rows, D_MODEL // 2), row),
                pl.BlockSpec(memory_space=pl.ANY),
                pl.BlockSpec(memory_space=pl.ANY),
                pl.BlockSpec(memory_space=pl.ANY),
            ],
            out_specs=pl.BlockSpec((step_rows, D_MODEL // 2), row),
            scratch_shapes=[
                pltpu.VMEM((WEIGHT_SLOTS, D_MODEL, EXPERT_FF), F32),
                pltpu.VMEM((WEIGHT_SLOTS, D_MODEL, EXPERT_FF), F32),
                pltpu.VMEM((WEIGHT_SLOTS, EXPERT_FF, D_MODEL), F32),
                pltpu.VMEM((D_MODEL, EXPERT_FF), BF16),
                pltpu.VMEM((D_MODEL, EXPERT_FF), BF16),
                pltpu.VMEM((EXPERT_FF, D_MODEL), BF16),
                pltpu.SemaphoreType.DMA((WEIGHT_SLOTS, 3)),
            ],
        ),
        out_shape=jax.ShapeDtypeStruct((n_blocks * MOE_BM, D_MODEL // 2), jnp.int32),
        compiler_params=pltpu.CompilerParams(
            dimension_semantics=("arbitrary",), vmem_limit_bytes=VMEM_LIMIT),
        name="moe_experts",
    )(block_e, rows_valid, next_e, slot, first_e, xd, w_gate, w_up, w_down)


def _sc_worker_id():
    return lax.axis_index("s") * SC_CORES + lax.axis_index("c")


def _sc_dispatch(hp, hs, dest_p, dest_s, n_rows):
    tp, width = hp.shape
    per_w = tp // SC_WORKERS
    n_ch = per_w // DISP_CH
    n_sw = hs.shape[0] // SAMPLE_CH
    mesh = plsc.VectorSubcoreMesh(core_axis_name="c", subcore_axis_name="s")

    @functools.partial(
        pl.kernel, mesh=mesh,
        out_type=jax.ShapeDtypeStruct((n_rows, width), jnp.int32),
        scratch_types=[
            pltpu.VMEM((2, n_ch, DISP_CH), jnp.int32),
            pltpu.VMEM((2, 1, SAMPLE_CH), jnp.int32),
            pltpu.VMEM((SC_RING, DISP_CH, width), jnp.int32),
            pltpu.SemaphoreType.DMA((SC_RING,)),
            pltpu.SemaphoreType.DMA((SC_RING, 2)),
        ],
        name="sc_dispatch",
    )
    def k(hp_hbm, hs_hbm, dp_hbm, ds_hbm, xd_hbm, idx_v, idxs_v, bufs, rsem, wsem):
        wid = _sc_worker_id()
        base = wid * per_w
        for kk in range(2):
            pltpu.sync_copy(dp_hbm.at[kk, wid], idx_v.at[kk])
        reads = [pltpu.make_async_copy(hp_hbm.at[pl.ds(base + j * DISP_CH, DISP_CH)],
                                       bufs.at[j % SC_RING], rsem.at[j % SC_RING]) for j in range(n_ch)]
        writes = [[pltpu.make_async_copy(bufs.at[j % SC_RING], xd_hbm.at[idx_v.at[kk, j]],
                                         wsem.at[j % SC_RING, kk]) for kk in range(2)] for j in range(n_ch)]
        for j in range(min(SC_RING - 1, n_ch)):
            reads[j].start()
        for j in range(n_ch):
            reads[j].wait()
            for w in writes[j]:
                w.start()
            if j >= 1:
                for w in writes[j - 1]:
                    w.wait()
            if j + SC_RING - 1 < n_ch:
                reads[j + SC_RING - 1].start()
        for w in writes[n_ch - 1]:
            w.wait()

        @pl.when(wid < n_sw)
        def _():
            rows = bufs.at[0, pl.ds(0, SAMPLE_CH)]
            for kk in range(2):
                pltpu.sync_copy(ds_hbm.at[kk, wid], idxs_v.at[kk])
            pltpu.sync_copy(hs_hbm.at[pl.ds(wid * SAMPLE_CH, SAMPLE_CH)], rows)
            for kk in range(2):
                pltpu.sync_copy(rows, xd_hbm.at[idxs_v.at[kk, 0]])

    return k(hp, hs, dest_p, dest_s)


def _sc_sample_gather(yd, dest_s, ts):
    width = yd.shape[1]
    n_sw = ts // SAMPLE_CH
    mesh = plsc.VectorSubcoreMesh(core_axis_name="c", subcore_axis_name="s")

    @functools.partial(
        pl.kernel, mesh=mesh,
        out_type=jax.ShapeDtypeStruct((2, ts, width), yd.dtype),
        scratch_types=[
            pltpu.VMEM((2, 1, SAMPLE_CH), jnp.int32),
            pltpu.VMEM((2, SAMPLE_CH, width), yd.dtype),
        ],
        name="sc_sample_gather",
    )
    def k(yd_hbm, ds_hbm, g_hbm, idxs_v, bufs):
        wid = _sc_worker_id()

        @pl.when(wid < n_sw)
        def _():
            for kk in range(2):
                pltpu.sync_copy(ds_hbm.at[kk, wid], idxs_v.at[kk])
            for kk in range(2):
                pltpu.sync_copy(yd_hbm.at[idxs_v.at[kk, 0]], bufs.at[kk])
                pltpu.sync_copy(bufs.at[kk], g_hbm.at[kk, pl.ds(wid * SAMPLE_CH, SAMPLE_CH)])

    return k(yd, dest_s)


def _sc_combine_gather(yd, dest_p, tp):
    width = yd.shape[1]
    per_w = tp // SC_WORKERS
    n_ch = per_w // COMB_CH
    mesh = plsc.VectorSubcoreMesh(core_axis_name="c", subcore_axis_name="s")

    @functools.partial(
        pl.kernel, mesh=mesh,
        out_type=jax.ShapeDtypeStruct((2, tp, width), yd.dtype),
        scratch_types=[
            pltpu.VMEM((2, n_ch, COMB_CH), jnp.int32),
            pltpu.VMEM((SC_RING, COMB_CH, width), yd.dtype),
            pltpu.SemaphoreType.DMA((SC_RING,)),
            pltpu.SemaphoreType.DMA((SC_RING,)),
        ],
        name="sc_combine_gather",
    )
    def k(yd_hbm, dp_hbm, g_hbm, idx_v, bufs, gsem, wsem):
        wid = _sc_worker_id()
        base = wid * per_w
        for kk in range(2):
            pltpu.sync_copy(dp_hbm.at[kk, wid], idx_v.at[kk])
        items = [(kk, j) for kk in range(2) for j in range(n_ch)]
        n_items = len(items)
        gathers = [pltpu.make_async_copy(yd_hbm.at[idx_v.at[kk, j]], bufs.at[n % SC_RING], gsem.at[n % SC_RING])
                   for n, (kk, j) in enumerate(items)]
        outs = [pltpu.make_async_copy(bufs.at[n % SC_RING], g_hbm.at[kk, pl.ds(base + j * COMB_CH, COMB_CH)],
                                      wsem.at[n % SC_RING]) for n, (kk, j) in enumerate(items)]
        for n in range(min(SC_RING - 1, n_items)):
            gathers[n].start()
        for n in range(n_items):
            gathers[n].wait()
            outs[n].start()
            if n >= 1:
                outs[n - 1].wait()
            if n + SC_RING - 1 < n_items:
                gathers[n + SC_RING - 1].start()
        outs[n_items - 1].wait()

    return k(yd, dest_p)


def _combine_kernel(x1_ref, g_ref, route_t_ref, x2_ref):
    fields = route_t_ref[...]
    tm = fields.shape[1]
    cols = jnp.transpose(jnp.concatenate([fields, jnp.zeros((LANES - ROUTE_FIELDS, tm), F32)], axis=0))
    w1 = cols[:, 2:3]
    w2 = cols[:, 3:4]
    x2_ref[...] = x1_ref[...] + _unpack_bf16_pairs(g_ref[0]) * w1 + _unpack_bf16_pairs(g_ref[1]) * w2


def _combine(x1, g, route_t, row0, tm):
    t = x1.shape[0]
    blk0 = row0 // tm
    row = lambda i: (i, 0)
    return pl.pallas_call(
        _combine_kernel,
        grid=(t // tm,),
        in_specs=[
            pl.BlockSpec((tm, D_MODEL), row),
            pl.BlockSpec((2, tm, D_MODEL // 2), lambda i: (0, blk0 + i, 0)),
            pl.BlockSpec((ROUTE_FIELDS, tm), lambda i: (0, i)),
        ],
        out_specs=pl.BlockSpec((tm, D_MODEL), row),
        out_shape=jax.ShapeDtypeStruct((t, D_MODEL), F32),
        compiler_params=pltpu.CompilerParams(
            dimension_semantics=("arbitrary",), vmem_limit_bytes=VMEM_LIMIT),
        name="combine",
    )(x1, g, route_t)


def _dest_layout(dest, workers, chunk):
    t = dest.shape[1]
    return dest.reshape(2, workers, t // (workers * chunk), chunk)


def _hier_moe(l, h2p, h2s, route_tp, route_ts, counts, w_gate, w_up, w_down):
    tp, ts = h2p.shape[0], h2s.shape[0]
    n_assign = 2 * (tp + ts)
    n_blocks = -(-n_assign // MOE_BM) + N_EXPERTS
    n_blocks = -(-n_blocks // MOE_STEP_BLOCKS) * MOE_STEP_BLOCKS
    pcounts = (counts + MOE_BM - 1) // MOE_BM * MOE_BM
    pends = jnp.cumsum(pcounts)
    poffsets = pends - pcounts
    starts = jnp.arange(n_blocks, dtype=jnp.int32) * MOE_BM
    block_e = jnp.minimum(jnp.sum((pends[None, :] <= starts[:, None]).astype(jnp.int32), axis=1),
                          N_EXPERTS - 1)
    experts = jnp.arange(N_EXPERTS, dtype=jnp.int32)

    def lookup(table, idx):
        return jnp.sum(jnp.where(idx[..., None] == experts, table, 0), axis=-1)

    rows_valid = jnp.clip(lookup(poffsets + counts, block_e) - starts, 0, MOE_BM).astype(jnp.int32)
    used = counts > 0
    last_e = jnp.max(jnp.where(used, jnp.arange(N_EXPERTS, dtype=jnp.int32), 0))
    block_e = jnp.where(rows_valid > 0, block_e, last_e).astype(jnp.int32)
    place = jnp.cumsum(used.astype(jnp.int32)) - 1
    by_place = jnp.sum(jnp.where(used[None, :] & (place[None, :] == experts[:, None]), experts[None, :], 0),
                       axis=1)
    n_used = jnp.sum(used.astype(jnp.int32))

    def at_place(p):
        return jnp.where(p < n_used, lookup(by_place, jnp.minimum(p, N_EXPERTS - 1)), -1).astype(jnp.int32)

    ahead_of = at_place(place + (WEIGHT_SLOTS - 1))
    next_e = lookup(ahead_of, block_e)
    slot = lookup(place % WEIGHT_SLOTS, block_e)
    n_steps_used = -(-(pends[-1] // MOE_BM) // MOE_STEP_BLOCKS)
    first_e = jnp.concatenate([at_place(jnp.arange(WEIGHT_SLOTS - 1, dtype=jnp.int32)),
                               n_steps_used.reshape(1).astype(jnp.int32)])

    def dest_of(route_t):
        return lookup(poffsets, route_t[0:2].astype(jnp.int32)) + route_t[4:6].astype(jnp.int32)

    dest_p, dest_s = dest_of(route_tp), dest_of(route_ts)
    n_sw = ts // SAMPLE_CH
    xd = _sc_dispatch(h2p, h2s, _dest_layout(dest_p, SC_WORKERS, DISP_CH),
                      _dest_layout(dest_s, n_sw, SAMPLE_CH), n_blocks * MOE_BM)
    yd = _moe_experts(l, block_e, rows_valid, next_e, slot, first_e, xd, w_gate, w_up, w_down)
    g_s = _sc_sample_gather(yd, _dest_layout(dest_s, n_sw, SAMPLE_CH), ts)
    g_p = _sc_combine_gather(yd, _dest_layout(dest_p, SC_WORKERS, COMB_CH), tp)
    return g_p, g_s


def kernel(x_prompt, x_sample, state_pool, cache_k_win, cache_v_win, norm_attn_g, w_in, pool_w, pool_scale, q_norm_g, k_norm_g, attn_sinks, w_out, norm_ffn_g, router_group_w, router_group_b, router_expert_w, router_expert_b, w_gate, w_up, w_down):
    n_p, t_p, d = x_prompt.shape
    n_s, t_s, _ = x_sample.shape
    depth = w_in.shape[0]
    lw_s = cache_k_win.shape[2]
    assert t_s == 1 and lw_s == WINDOW and d == D_MODEL
    assert t_p % TM_PROJ == 0 and t_p >= WINDOW

    seg = jnp.arange(256) // HEAD_DIM
    bd = jnp.where(seg[:, None] == seg[None, :], 1.0 / HEAD_DIM, 0.0).astype(BF16)
    slopes = jnp.exp2(-8.0 * jnp.arange(1, N_HEADS + 1, dtype=F32) / N_HEADS)
    bias_p = _prompt_bias_t()
    dist_s = (WINDOW - 1) - jnp.arange(WINDOW, dtype=F32)
    bias_s = -LOG2E * slopes[:, None] * dist_s[None, :]

    wp = jnp.zeros((depth, 2, 256, 256), F32)
    for p in range(2):
        wp = wp.at[:, p, :POOL_GC, :POOL_GC].set(pool_w[:, 2 * p])
        wp = wp.at[:, p, POOL_GC:, POOL_GC:].set(pool_w[:, 2 * p + 1])
    assert GROUP_LANE0 == N_EXPERTS
    lane_pad = LANES - N_EXPERTS - N_EXPERT_GROUPS
    wr = jnp.concatenate([router_expert_w, router_group_w, jnp.zeros((depth, D_MODEL, lane_pad), F32)], axis=-1)
    br = jnp.concatenate([router_expert_b, router_group_b, jnp.zeros((depth, lane_pad), F32)],
                         axis=-1).reshape(depth, 1, LANES)
    lp = dict(
        w_in=w_in.astype(BF16),
        w_out=w_out.astype(BF16),
        g_attn=norm_attn_g.reshape(depth, 1, D_MODEL),
        g_ffn=norm_ffn_g.reshape(depth, 1, D_MODEL),
        qg=(jnp.tile(q_norm_g, (1, N_HEADS)) * (ATTN_SCALE * LOG2E)).reshape(depth, 1, Q_W),
        kg=jnp.tile(k_norm_g, (1, N_KV_HEADS)).reshape(depth, 1, KV_W),
        wp=wp.astype(BF16),
        ps=pool_scale.reshape(depth, 1, POOL_W),
        wr=wr.astype(BF16),
        br=br,
        state=state_pool,
        ck=cache_k_win.reshape(depth, n_s, lw_s, KV_W),
        cv=cache_v_win.reshape(depth, n_s, lw_s, KV_W),
    )

    xp = x_prompt.reshape(n_p * t_p, D_MODEL)
    xs = x_sample.reshape(n_s, D_MODEL)
    lw_p = min(WINDOW, t_p)
    pool_p, kp_new, vp_new = [], [], []
    sample_state = [lp["state"], lp["ck"], lp["cv"]]
    zero_cnt = jnp.zeros((N_EXPERTS, 1), F32)
    pending = None
    for l in range(depth):
        sinks = attn_sinks[l] * LOG2E
        pool_so, attn_so, *sample_state = _sample_mixer(
            l, depth, xs, lp["g_attn"], lp["w_in"], lp["qg"], lp["kg"], bd, lp["wp"], lp["ps"],
            *sample_state, sinks.reshape(N_HEADS, 1), bias_s, PAST_LEN)
        x1s, h2s, route_ts, cnt_s = _merge_router(
            l, pool_so, attn_so, xs, lp["w_out"], lp["g_ffn"], lp["wr"], lp["br"], zero_cnt, n_s)
        outs = _proj_pool_prompt(
            l, xp if pending is None else pending, n_p, t_p,
            lp["g_attn"], lp["w_in"], lp["qg"], lp["kg"], bd, lp["wp"], lp["ps"])
        if pending is not None:
            xp, outs = outs[0], outs[1:]
        pool_o, q, k, vt, utail, ktail, vtail = outs
        attn_o = _attn_prompt(q, k, vt, bias_p, sinks, n_p, t_p)
        if l == depth - 1:
            ks_out = sample_state[1].reshape(depth, n_s, lw_s, N_KV_HEADS, HEAD_DIM)
            vs_out = sample_state[2].reshape(depth, n_s, lw_s, N_KV_HEADS, HEAD_DIM)
            ks_out, vs_out, attn_o = lax.optimization_barrier((ks_out, vs_out, attn_o))
        x1p, h2p, route_tp, cnt_all = _merge_router(
            l, pool_o, attn_o, xp, lp["w_out"], lp["g_ffn"], lp["wr"], lp["br"], cnt_s, TM_MERGE)
        pool_p.append(utail[:, 16 - POOL_STATE:, :])
        kp_new.append(ktail)
        vp_new.append(vtail)
        counts = cnt_all[:, 0].astype(jnp.int32)
        g_p, g_s = _hier_moe(l, h2p, h2s, route_tp, route_ts, counts, w_gate, w_up, w_down)
        xs = _combine(x1s, g_s, route_ts, 0, n_s)
        pending = (x1p, g_p, route_tp)
    xp = _combine(*pending, 0, TM_MERGE)
    return (xp.reshape(n_p, t_p, D_MODEL), xs.reshape(n_s, t_s, D_MODEL),
            jnp.stack(pool_p),
            jnp.stack(kp_new).reshape(depth, n_p, lw_p, N_KV_HEADS, HEAD_DIM),
            jnp.stack(vp_new).reshape(depth, n_p, lw_p, N_KV_HEADS, HEAD_DIM),
            sample_state[0], ks_out, vs_out)
```

```python
import functools

import jax
import jax.numpy as jnp
from jax import lax
from jax.experimental import pallas as pl
from jax.experimental.pallas import tpu as pltpu
from jax.experimental.pallas import tpu_sc as plsc

D_MODEL = 1024
POOL_W = 512
POOL_WINDOWS = (2, 4, 8, 16)
POOL_GC = 128
POOL_STATE = 15
HEAD_DIM = 64
N_HEADS = 8
N_KV_HEADS = 2
GQA_GROUP = 4
Q_W = 512
KV_W = 128
D_IN = POOL_W + Q_W + 2 * KV_W
WINDOW = 128
ATTN_SCALE = HEAD_DIM ** -0.5
LOG2E = 1.4426950408889634
N_EXPERT_GROUPS = 4
EXPERTS_PER_GROUP = 8
N_EXPERTS = 32
EXPERT_FF = 512
EPS = 1e-6
PAST_LEN = 16384

LANES = 128
HALO = 32
TM_PROJ = 1024
TM_MERGE = 1024
MERGE_CHUNKS = 4
ATTN_QB = 16
MOE_BM = 256
MOE_STEP_BLOCKS = 4
WEIGHT_SLOTS = 3
GROUP_LANE0 = 32
ROUTE_FIELDS = 8
SC_CORES = 2
SC_SUBCORES = 16
SC_WORKERS = SC_CORES * SC_SUBCORES
DISP_CH = 64
COMB_CH = 64
SAMPLE_CH = 32
SC_RING = 3
VMEM_LIMIT = 48 * 1024 * 1024

BF16 = jnp.bfloat16
F32 = jnp.float32


def _pack_bf16_pairs(h):
    w = h.shape[1] // 2
    hi = lax.bitcast_convert_type(h[:, :w].astype(F32), jnp.uint32)
    lo = lax.bitcast_convert_type(h[:, w:].astype(F32), jnp.uint32)
    return lax.bitcast_convert_type(hi | (lo >> 16), jnp.int32)


def _unpack_bf16_pairs(words):
    u = lax.bitcast_convert_type(words, jnp.uint32)
    hi = lax.bitcast_convert_type(u & jnp.uint32(0xFFFF0000), F32)
    lo = lax.bitcast_convert_type(u << 16, F32)
    return jnp.concatenate([hi, lo], axis=-1)


def _segment_mean_sq(a, bd):
    w = a.shape[1]
    return jnp.dot((a * a).astype(BF16), bd[:w, :w], preferred_element_type=F32)


def _rms_bf16(x, g):
    ms = jnp.mean(x * x, axis=-1, keepdims=True)
    return (x * lax.rsqrt(ms + EPS) * g).astype(BF16)


def _qk_norm(q, k, qg, kg, bd):
    qn = []
    for c in range(Q_W // 256):
        qc = q[:, c * 256:(c + 1) * 256]
        qn.append(qc * lax.rsqrt(_segment_mean_sq(qc, bd) + EPS))
    qn = jnp.concatenate(qn, axis=-1) * qg
    kn = k * lax.rsqrt(_segment_mean_sq(k, bd) + EPS) * kg
    return qn, kn


def _project(x, g, w_in, qg, kg, bd):
    z = jnp.dot(_rms_bf16(x, g), w_in, preferred_element_type=F32)
    u = z[:, :POOL_W]
    q = z[:, POOL_W:POOL_W + Q_W]
    k = z[:, POOL_W + Q_W:POOL_W + Q_W + KV_W]
    v = z[:, POOL_W + Q_W + KV_W:]
    qn, kn = _qk_norm(q, k, qg, kg, bd)
    return u, qn, kn, v


def _pool_project(d_groups, wp_ref, ps):
    outs = []
    for p in range(2):
        dp = jnp.concatenate([d_groups[2 * p], d_groups[2 * p + 1]], axis=-1).astype(BF16)
        y = jnp.dot(dp, wp_ref[p], preferred_element_type=F32)
        outs.append(y * ps[:, p * 256:(p + 1) * 256])
    return jnp.concatenate(outs, axis=-1)


def _proj_pool_kernel(x_ref, g_ref, win_ref, qg_ref, kg_ref, bd_ref, wp_ref, ps_ref,
                      pool_ref, q_ref, k_ref, vt_ref, utail_ref, ktail_ref, vtail_ref,
                      ext_ref, sa_ref, sb_ref, zq_ref, *, tm, n_j):
    j = pl.program_id(1)

    @pl.when(j == 0)
    def _():
        ext_ref[0:HALO, :] = jnp.zeros((HALO, POOL_W), F32)

    r = tm + HALO
    h = _rms_bf16(x_ref[...], g_ref[...])
    ext_ref[HALO:r, :] = jnp.dot(h, win_ref[:, 0:POOL_W], preferred_element_type=F32)
    zq_ref[...] = jnp.dot(h, win_ref[:, POOL_W:], preferred_element_type=F32)
    u = ext_ref[HALO:r, :]
    sa_ref[8:r, :] = ext_ref[8:r, :] + ext_ref[7:r - 1, :]
    sb_ref[16:r, 128:] = sa_ref[16:r, 128:] + sa_ref[14:r - 2, 128:]
    sa_ref[24:r, 256:] = sb_ref[24:r, 256:] + sb_ref[20:r - 4, 256:]
    sb_ref[32:r, 384:] = sa_ref[32:r, 384:] + sa_ref[24:r - 8, 384:]
    pos1 = j * tm + lax.broadcasted_iota(jnp.int32, (tm, POOL_GC), 0) + 1
    sums = (sa_ref, sb_ref, sa_ref, sb_ref)
    d_groups = []
    for gi, w in enumerate(POOL_WINDOWS):
        sl = slice(gi * POOL_GC, (gi + 1) * POOL_GC)
        cnt = jnp.minimum(pos1, w).astype(F32)
        d_groups.append(sums[gi][HALO:r, sl] / cnt - u[:, sl])
    pool_ref[...] = _pool_project(d_groups, wp_ref, ps_ref[...]).astype(BF16)
    ext_ref[16:HALO, :] = ext_ref[tm + 16:r, :]

    qn, kn = _qk_norm(zq_ref[:, 0:Q_W], zq_ref[:, Q_W:Q_W + KV_W], qg_ref[...], kg_ref[...], bd_ref[...])
    v = zq_ref[:, Q_W + KV_W:]
    q_ref[...] = qn.astype(BF16)
    k_ref[...] = kn.astype(BF16)
    vt_ref[...] = jnp.transpose(v).astype(BF16)

    @pl.when(j == n_j - 1)
    def _():
        utail_ref[...] = u[tm - 16:, :]
        ktail_ref[...] = kn[tm - WINDOW:, :]
        vtail_ref[...] = v[tm - WINDOW:, :]


def _proj_pool_combine_kernel(x1_ref, gath_ref, route_ref, *rest, tm, n_j):
    x2_ref = rest[7]
    _combine_kernel(x1_ref, gath_ref, route_ref, x2_ref)
    _proj_pool_kernel(x2_ref, *rest[:7], *rest[8:], tm=tm, n_j=n_j)


def _proj_pool_prompt(l, x_in, n_seq, seq, g_attn, w_in, qg, kg, bd, wp, ps):
    tm = TM_PROJ
    n_j = seq // tm
    t = n_seq * seq
    row = lambda b, j: (b * n_j + j, 0)
    lay = lambda b, j: (l, 0, 0)
    fused = isinstance(x_in, tuple)
    if fused:
        kern = _proj_pool_combine_kernel
        x_args = list(x_in)
        x_specs = [pl.BlockSpec((tm, D_MODEL), row),
                   pl.BlockSpec((2, tm, D_MODEL // 2), lambda b, j: (0, b * n_j + j, 0)),
                   pl.BlockSpec((ROUTE_FIELDS, tm), lambda b, j: (0, b * n_j + j))]
        x_out_specs = [pl.BlockSpec((tm, D_MODEL), row)]
        x_out_shape = [jax.ShapeDtypeStruct((t, D_MODEL), F32)]
    else:
        kern = _proj_pool_kernel
        x_args = [x_in]
        x_specs = [pl.BlockSpec((tm, D_MODEL), row)]
        x_out_specs, x_out_shape = [], []
    return pl.pallas_call(
        functools.partial(kern, tm=tm, n_j=n_j),
        grid=(n_seq, n_j),
        in_specs=x_specs + [
            pl.BlockSpec((None, 1, D_MODEL), lay),
            pl.BlockSpec((None, D_MODEL, D_IN), lay),
            pl.BlockSpec((None, 1, Q_W), lay),
            pl.BlockSpec((None, 1, KV_W), lay),
            pl.BlockSpec((256, 256), lambda b, j: (0, 0)),
            pl.BlockSpec((None, 2, 256, 256), lambda b, j: (l, 0, 0, 0)),
            pl.BlockSpec((None, 1, POOL_W), lay),
        ],
        out_specs=x_out_specs + [
            pl.BlockSpec((tm, POOL_W), row),
            pl.BlockSpec((tm, Q_W), row),
            pl.BlockSpec((tm, KV_W), row),
            pl.BlockSpec((KV_W, tm), lambda b, j: (0, b * n_j + j)),
            pl.BlockSpec((None, 16, POOL_W), lambda b, j: (b, 0, 0)),
            pl.BlockSpec((None, WINDOW, KV_W), lambda b, j: (b, 0, 0)),
            pl.BlockSpec((None, WINDOW, KV_W), lambda b, j: (b, 0, 0)),
        ],
        out_shape=x_out_shape + [
            jax.ShapeDtypeStruct((t, POOL_W), BF16),
            jax.ShapeDtypeStruct((t, Q_W), BF16),
            jax.ShapeDtypeStruct((t, KV_W), BF16),
            jax.ShapeDtypeStruct((KV_W, t), BF16),
            jax.ShapeDtypeStruct((n_seq, 16, POOL_W), F32),
            jax.ShapeDtypeStruct((n_seq, WINDOW, KV_W), F32),
            jax.ShapeDtypeStruct((n_seq, WINDOW, KV_W), F32),
        ],
        scratch_shapes=[pltpu.VMEM((tm + HALO, POOL_W), F32)] * 3 + [pltpu.VMEM((tm, Q_W + 2 * KV_W), F32)],
        compiler_params=pltpu.CompilerParams(
            dimension_semantics=("arbitrary", "arbitrary"), vmem_limit_bytes=VMEM_LIMIT),
        name="proj_pool_prompt",
    )(*x_args, g_attn, w_in, qg, kg, bd, wp, ps)


def _attn_kernel(sink_ref, q_ref, kp_ref, kc_ref, vtp_ref, vtc_ref, bias_ref, o_ref, s_ref):
    j = pl.program_id(1)
    kk_all = jnp.concatenate([kp_ref[...], kc_ref[...]], axis=0)
    vt_all = jnp.concatenate([vtp_ref[...], vtc_ref[...]], axis=1)
    from_prev = (lax.broadcasted_iota(jnp.int32, (WINDOW, WINDOW), 0)
                 > lax.broadcasted_iota(jnp.int32, (WINDOW, WINDOW), 1))
    units = [(blk, kv) for blk in range(ATTN_QB) for kv in range(N_KV_HEADS)]

    def scores(n):
        blk, kv = units[n]
        q = q_ref[blk * WINDOW:(blk + 1) * WINDOW, :]
        kk = kk_all[blk * WINDOW:(blk + 2) * WINDOW, kv * HEAD_DIM:(kv + 1) * HEAD_DIM]
        heads = range(kv * GQA_GROUP, (kv + 1) * GQA_GROUP)
        q_rows = jnp.concatenate([q[:, h * HEAD_DIM:(h + 1) * HEAD_DIM] for h in heads], axis=0)
        s_ref[n % 3] = lax.dot_general(kk, q_rows, (((1,), (1,)), ((), ())), preferred_element_type=F32)

    scores(0)
    scores(1)
    outs = []
    for n, (blk, kv) in enumerate(units):
        if n + 2 < len(units):
            scores(n + 2)
        vt_kv = vt_all[kv * HEAD_DIM:(kv + 1) * HEAD_DIM, blk * WINDOW:(blk + 2) * WINDOW]
        variant = jnp.minimum(j, 1) if blk == 0 else 1
        for g in range(GQA_GROUP):
            h = kv * GQA_GROUP + g
            s = jnp.where(from_prev, s_ref[n % 3, 0:WINDOW, g * WINDOW:(g + 1) * WINDOW],
                          s_ref[n % 3, WINDOW:, g * WINDOW:(g + 1) * WINDOW]) + bias_ref[variant, h]
            sink = sink_ref[h]
            m = jnp.maximum(jnp.max(s, axis=0, keepdims=True), sink)
            p = jnp.exp2(s - m)
            denom = jnp.sum(p, axis=0, keepdims=True) + jnp.exp2(sink - m)
            p_keys = jnp.concatenate([jnp.where(from_prev, p, 0.0), jnp.where(from_prev, 0.0, p)], axis=0)
            o_t = jnp.dot(vt_kv, p_keys.astype(BF16), preferred_element_type=F32)
            outs.append(o_t / denom)
        if kv == N_KV_HEADS - 1:
            o_ref[blk * WINDOW:(blk + 1) * WINDOW, :] = jnp.transpose(jnp.concatenate(outs, axis=0)).astype(BF16)
            outs = []


def _attn_prompt(q, k, vt, bias_t, sinks, n_seq, seq):
    tq = ATTN_QB * WINDOW
    nj = seq // tq
    t = n_seq * seq
    cur = lambda b, j: (b * nj + j, 0)
    prev = lambda b, j: (jnp.maximum((b * nj + j) * ATTN_QB - 1, 0), 0)
    cur_t = lambda b, j: (0, b * nj + j)
    prev_t = lambda b, j: (0, jnp.maximum((b * nj + j) * ATTN_QB - 1, 0))
    return pl.pallas_call(
        _attn_kernel,
        grid=(n_seq, nj),
        in_specs=[
            pl.BlockSpec(memory_space=pltpu.SMEM),
            pl.BlockSpec((tq, Q_W), cur),
            pl.BlockSpec((WINDOW, KV_W), prev),
            pl.BlockSpec((tq, KV_W), cur),
            pl.BlockSpec((KV_W, WINDOW), prev_t),
            pl.BlockSpec((KV_W, tq), cur_t),
            pl.BlockSpec((2, N_HEADS, WINDOW, WINDOW), lambda b, j: (0, 0, 0, 0)),
        ],
        out_specs=pl.BlockSpec((tq, Q_W), cur),
        out_shape=jax.ShapeDtypeStruct((t, Q_W), BF16),
        scratch_shapes=[pltpu.VMEM((3, 2 * WINDOW, GQA_GROUP * WINDOW), F32)],
        compiler_params=pltpu.CompilerParams(
            dimension_semantics=("arbitrary", "arbitrary"), vmem_limit_bytes=VMEM_LIMIT),
        name="attn_prompt",
    )(sinks, q, k, k, vt, vt, bias_t)


def _prompt_bias_t():
    r = jnp.arange(WINDOW, dtype=jnp.int32)[None, :]
    c = jnp.arange(WINDOW, dtype=jnp.int32)[:, None]
    from_prev = c > r
    dist = r - c + jnp.where(from_prev, WINDOW, 0)
    slopes = jnp.exp2(-8.0 * jnp.arange(1, N_HEADS + 1, dtype=F32) / N_HEADS)
    later = -LOG2E * slopes[:, None, None] * dist.astype(F32)[None]
    first = jnp.where(from_prev[None], -jnp.inf, later)
    return jnp.stack([first, later])


def _sample_kernel(x_ref, g_ref, win_ref, qg_ref, kg_ref, bd_ref, wp_ref, ps_ref,
                   st_ref, ck_ref, cv_ref, sink_ref, bias_ref, perm_ref,
                   pool_ref, attn_ref, pst_ref, kc_ref, vc_ref, *, ns, pos0):
    u, qn, kn, v = _project(x_ref[...], g_ref[...], win_ref[...], qg_ref[...], kg_ref[...], bd_ref[...])
    pst_ref[:, 0:POOL_STATE - 1, :] = st_ref[:, 1:POOL_STATE, :]
    kc_ref[:, 0:WINDOW - 1, :] = ck_ref[:, 1:WINDOW, :]
    vc_ref[:, 0:WINDOW - 1, :] = cv_ref[:, 1:WINDOW, :]
    for n in range(ns):
        pst_ref[n, POOL_STATE - 1:POOL_STATE, :] = u[n:n + 1, :]
        kc_ref[n, WINDOW - 1:WINDOW, :] = kn[n:n + 1, :]
        vc_ref[n, WINDOW - 1:WINDOW, :] = v[n:n + 1, :]

    d_groups = []
    for gi, w in enumerate(POOL_WINDOWS):
        lo = gi * POOL_GC
        acc = u[:, lo:lo + POOL_GC]
        for back in range(1, w):
            acc = acc + st_ref[:, POOL_STATE - back, lo:lo + POOL_GC]
        d_groups.append(acc / float(min(pos0 + 1, w)) - u[:, lo:lo + POOL_GC])
    pool_ref[...] = _pool_project(d_groups, wp_ref, ps_ref[...]).astype(BF16)

    zeros = jnp.zeros((ns, HEAD_DIM), F32)
    stacked = []
    for h in range(N_HEADS):
        piece = qn[:, h * HEAD_DIM:(h + 1) * HEAD_DIM]
        pair = [piece, zeros] if h < GQA_GROUP else [zeros, piece]
        stacked.append(jnp.concatenate(pair, axis=-1))
    q_hn = jnp.concatenate(stacked, axis=0).astype(BF16)
    q_nh = jnp.dot(perm_ref[0], q_hn, preferred_element_type=F32).astype(BF16)

    keys = kc_ref[...].reshape(ns * WINDOW, KV_W).astype(BF16)
    vals = vc_ref[...].reshape(ns * WINDOW, KV_W).astype(BF16)
    s_all = lax.dot_general(q_nh, keys, (((1,), (1,)), ((), ())), preferred_element_type=F32)
    sink = sink_ref[...]
    bias = bias_ref[...]
    zero_blk = jnp.zeros((N_HEADS, WINDOW), F32)
    p_rows = []
    for n in range(ns):
        s = s_all[n * N_HEADS:(n + 1) * N_HEADS, n * WINDOW:(n + 1) * WINDOW] + bias
        m = jnp.maximum(jnp.max(s, axis=-1, keepdims=True), sink)
        p = jnp.exp2(s - m)
        denom = jnp.sum(p, axis=-1, keepdims=True) + jnp.exp2(sink - m)
        p_rows.append(jnp.concatenate([zero_blk] * n + [p / denom] + [zero_blk] * (ns - 1 - n), axis=-1))
    p_blockdiag = jnp.concatenate(p_rows, axis=0).astype(BF16)
    o_nh = jnp.dot(p_blockdiag, vals, preferred_element_type=F32).astype(BF16)
    o_hn = jnp.dot(perm_ref[1], o_nh, preferred_element_type=F32)
    pieces = []
    for h in range(N_HEADS):
        kv = h // GQA_GROUP
        pieces.append(o_hn[h * ns:(h + 1) * ns, kv * HEAD_DIM:(kv + 1) * HEAD_DIM])
    attn_ref[...] = jnp.concatenate(pieces, axis=-1).astype(BF16)


def _sample_mixer(l, depth, xs, g_attn, w_in, qg, kg, bd, wp, ps, state, ck, cv, sink8, bias_s, pos0):
    n = xs.shape[0]
    ns = 32
    row = lambda i: (i, 0)
    lay = lambda i: (l, 0, 0)
    src = jnp.arange(ns * N_HEADS)
    perm = (((src % N_HEADS) * ns + src // N_HEADS)[:, None] == src[None, :]).astype(BF16)
    perms = jnp.stack([perm, perm.T])
    return pl.pallas_call(
        functools.partial(_sample_kernel, ns=ns, pos0=pos0),
        grid=(n // ns,),
        input_output_aliases={8: 2, 9: 3, 10: 4},
        in_specs=[
            pl.BlockSpec((ns, D_MODEL), row),
            pl.BlockSpec((None, 1, D_MODEL), lay),
            pl.BlockSpec((None, D_MODEL, D_IN), lay),
            pl.BlockSpec((None, 1, Q_W), lay),
            pl.BlockSpec((None, 1, KV_W), lay),
            pl.BlockSpec((256, 256), lambda i: (0, 0)),
            pl.BlockSpec((None, 2, 256, 256), lambda i: (l, 0, 0, 0)),
            pl.BlockSpec((None, 1, POOL_W), lay),
            pl.BlockSpec((None, ns, POOL_STATE, POOL_W), lambda i: (l, i, 0, 0)),
            pl.BlockSpec((None, ns, WINDOW, KV_W), lambda i: (l, i, 0, 0)),
            pl.BlockSpec((None, ns, WINDOW, KV_W), lambda i: (l, i, 0, 0)),
            pl.BlockSpec((N_HEADS, 1), lambda i: (0, 0)),
            pl.BlockSpec((N_HEADS, WINDOW), lambda i: (0, 0)),
            pl.BlockSpec((2, ns * N_HEADS, ns * N_HEADS), lambda i: (0, 0, 0)),
        ],
        out_specs=[
            pl.BlockSpec((ns, POOL_W), row),
            pl.BlockSpec((ns, Q_W), row),
            pl.BlockSpec((None, ns, POOL_STATE, POOL_W), lambda i: (l, i, 0, 0)),
            pl.BlockSpec((None, ns, WINDOW, KV_W), lambda i: (l, i, 0, 0)),
            pl.BlockSpec((None, ns, WINDOW, KV_W), lambda i: (l, i, 0, 0)),
        ],
        out_shape=[
            jax.ShapeDtypeStruct((n, POOL_W), BF16),
            jax.ShapeDtypeStruct((n, Q_W), BF16),
            jax.ShapeDtypeStruct((depth, n, POOL_STATE, POOL_W), F32),
            jax.ShapeDtypeStruct((depth, n, WINDOW, KV_W), F32),
            jax.ShapeDtypeStruct((depth, n, WINDOW, KV_W), F32),
        ],
        compiler_params=pltpu.CompilerParams(
            dimension_semantics=("arbitrary",), vmem_limit_bytes=VMEM_LIMIT),
        name="sample_mixer",
    )(xs, g_attn, w_in, qg, kg, bd, wp, ps, state, ck, cv, sink8, bias_s, perms)


def _merge_router_kernel(pool_ref, attn_ref, x_ref, wout_ref, g_ref, wr_ref, br_ref, utri_ref, cin_ref,
                         x1_ref, h2_ref, route_t_ref, cnt_ref, y_ref, lg_ref):
    i = pl.program_id(0)

    @pl.when(i == 0)
    def _():
        cnt_ref[...] = cin_ref[...]

    tm = x_ref.shape[0]
    rc = tm // MERGE_CHUNKS
    chunks = [slice(ci * rc, (ci + 1) * rc) for ci in range(MERGE_CHUNKS)]
    for rows in chunks:
        y_ref[rows, :] = (jnp.dot(pool_ref[rows, :], wout_ref[0:POOL_W, :], preferred_element_type=F32)
                          + jnp.dot(attn_ref[rows, :], wout_ref[POOL_W:, :], preferred_element_type=F32))
    for rows in chunks:
        x1 = x_ref[rows, :] + y_ref[rows, :]
        x1_ref[rows, :] = x1
        h2 = _rms_bf16(x1, g_ref[...])
        h2_ref[rows, :] = _pack_bf16_pairs(h2)
        lg_ref[rows, :] = jnp.dot(h2, wr_ref[...], preferred_element_type=F32) + br_ref[...]
    logits = lg_ref[...]

    lt = jnp.transpose(logits)
    sub = lax.broadcasted_iota(jnp.int32, (EXPERTS_PER_GROUP, tm), 0)
    neg = -jnp.inf
    big = jnp.int32(EXPERTS_PER_GROUP)
    gl = jnp.where(sub < N_EXPERT_GROUPS, lt[GROUP_LANE0:GROUP_LANE0 + EXPERTS_PER_GROUP, :], neg)
    gmax = jnp.max(gl, axis=0, keepdims=True)
    grp = jnp.min(jnp.where(gl == gmax, sub, big), axis=0, keepdims=True)
    g_w = 1.0 / jnp.sum(jnp.exp(gl - gmax), axis=0, keepdims=True)
    el = lt[(N_EXPERT_GROUPS - 1) * EXPERTS_PER_GROUP:N_EXPERT_GROUPS * EXPERTS_PER_GROUP, :]
    for gi in range(N_EXPERT_GROUPS - 2, -1, -1):
        el = jnp.where(grp == gi, lt[gi * EXPERTS_PER_GROUP:(gi + 1) * EXPERTS_PER_GROUP, :], el)
    v1 = jnp.max(el, axis=0, keepdims=True)
    i1 = jnp.min(jnp.where(el == v1, sub, big), axis=0, keepdims=True)
    el2 = jnp.where(sub == i1, neg, el)
    v2 = jnp.max(el2, axis=0, keepdims=True)
    i2 = jnp.min(jnp.where(el2 == v2, sub, big), axis=0, keepdims=True)
    e21 = jnp.exp(v2 - v1)
    w1 = g_w / (1.0 + e21)
    w2 = g_w * e21 / (1.0 + e21)
    e1 = grp * EXPERTS_PER_GROUP + i1
    e2 = grp * EXPERTS_PER_GROUP + i2

    esub = lax.broadcasted_iota(jnp.int32, (N_EXPERTS, tm), 0)
    oh1 = esub == e1
    oh2 = esub == e2
    c = jnp.where(oh1 | oh2, 1.0, 0.0)
    prefix = jnp.dot(c.astype(BF16), utri_ref[...], preferred_element_type=F32) + cnt_ref[...]
    r1 = jnp.sum(jnp.where(oh1, prefix, 0.0), axis=0, keepdims=True)
    r2 = jnp.sum(jnp.where(oh2, prefix, 0.0), axis=0, keepdims=True)
    cnt_ref[...] = cnt_ref[...] + jnp.sum(c, axis=1, keepdims=True)

    fields = jnp.zeros((ROUTE_FIELDS, tm), F32)
    for idx, val in enumerate((e1.astype(F32), e2.astype(F32), w1, w2, r1, r2)):
        fields = jnp.where(sub == idx, val, fields)
    route_t_ref[...] = fields


def _merge_router(l, pool, attn, x2d, w_out, g_ffn, wr, br, cnt_in, tm):
    t = x2d.shape[0]
    utri = (jnp.arange(tm)[:, None] < jnp.arange(tm)[None, :]).astype(BF16)
    row = lambda i: (i, 0)
    lay = lambda i: (l, 0, 0)
    return pl.pallas_call(
        _merge_router_kernel,
        grid=(t // tm,),
        in_specs=[
            pl.BlockSpec((tm, POOL_W), row),
            pl.BlockSpec((tm, Q_W), row),
            pl.BlockSpec((tm, D_MODEL), row),
            pl.BlockSpec((None, D_MODEL, D_MODEL), lay),
            pl.BlockSpec((None, 1, D_MODEL), lay),
            pl.BlockSpec((None, D_MODEL, LANES), lay),
            pl.BlockSpec((None, 1, LANES), lay),
            pl.BlockSpec((tm, tm), lambda i: (0, 0)),
            pl.BlockSpec((N_EXPERTS, 1), lambda i: (0, 0)),
        ],
        out_specs=[
            pl.BlockSpec((tm, D_MODEL), row),
            pl.BlockSpec((tm, D_MODEL // 2), row),
            pl.BlockSpec((ROUTE_FIELDS, tm), lambda i: (0, i)),
            pl.BlockSpec((N_EXPERTS, 1), lambda i: (0, 0)),
        ],
        out_shape=[
            jax.ShapeDtypeStruct((t, D_MODEL), F32),
            jax.ShapeDtypeStruct((t, D_MODEL // 2), jnp.int32),
            jax.ShapeDtypeStruct((ROUTE_FIELDS, t), F32),
            jax.ShapeDtypeStruct((N_EXPERTS, 1), F32),
        ],
        scratch_shapes=[pltpu.VMEM((tm, D_MODEL), F32), pltpu.VMEM((tm, LANES), F32)],
        compiler_params=pltpu.CompilerParams(
            dimension_semantics=("arbitrary",), vmem_limit_bytes=VMEM_LIMIT),
        name="merge_router",
    )(pool, attn, x2d, w_out, g_ffn, wr, br, utri, cnt_in)


def _moe_kernel(be_ref, rv_ref, nx_ref, sl_ref, first_ref, xd_ref, wg_hbm, wu_hbm, wd_hbm, yd_ref,
                wg_f, wu_f, wd_f, wg_s, wu_s, wd_s, sem, *, layer):
    step = pl.program_id(0)

    def weight_copies(e, s):
        return [pltpu.make_async_copy(w_hbm.at[layer, e], w_f.at[s], sem.at[s, n])
                for n, (w_hbm, w_f) in enumerate(((wg_hbm, wg_f), (wu_hbm, wu_f), (wd_hbm, wd_f)))]

    @pl.when(step == 0)
    def _():
        for s in range(WEIGHT_SLOTS - 1):
            @pl.when(first_ref[s] >= 0)
            def _():
                for c in weight_copies(first_ref[s], s):
                    c.start()

    def enter_expert(i):
        expert, slot = be_ref[i], sl_ref[i]

        @pl.when((i == 0) | (expert != be_ref[jnp.maximum(i - 1, 0)]))
        def _():
            for c in weight_copies(expert, slot):
                c.wait()

            @pl.when(nx_ref[i] >= 0)
            def _():
                for c in weight_copies(nx_ref[i], lax.rem(slot + WEIGHT_SLOTS - 1, WEIGHT_SLOTS)):
                    c.start(priority=1)

            wg_s[...] = wg_f[slot].astype(BF16)
            wu_s[...] = wu_f[slot].astype(BF16)
            wd_s[...] = wd_f[slot].astype(BF16)

    def experts_on(row0, n_rows, rows_valid):
        rows = pl.ds(row0, n_rows)
        row = lax.broadcasted_iota(jnp.int32, (n_rows, D_MODEL // 2), 0)
        x = _unpack_bf16_pairs(jnp.where(row < rows_valid, xd_ref[rows, :], 0)).astype(BF16)
        gate = jnp.dot(x, wg_s[...], preferred_element_type=F32)
        up = jnp.dot(x, wu_s[...], preferred_element_type=F32)
        act = (gate * jax.nn.sigmoid(gate) * up).astype(BF16)
        y = jnp.dot(act, wd_s[...], preferred_element_type=F32)
        yd_ref[rows, :] = _pack_bf16_pairs(y.astype(BF16))

    def experts_ragged(row0, lead_rows, rows_last):
        half = MOE_BM // 2

        @pl.when(rows_last > half)
        def _():
            experts_on(row0, lead_rows + MOE_BM, lead_rows + rows_last)

        @pl.when(rows_last <= half)
        def _():
            experts_on(row0, lead_rows + half, lead_rows + rows_last)
            yd_ref[pl.ds(row0 + lead_rows + half, half), :] = jnp.zeros((half, D_MODEL // 2), jnp.int32)

    def single_block(i, row0):
        enter_expert(i)

        @pl.when(rv_ref[i] > 0)
        def _():
            experts_ragged(row0, 0, rv_ref[i])

        @pl.when(rv_ref[i] <= 0)
        def _():
            yd_ref[pl.ds(row0, MOE_BM), :] = jnp.zeros((MOE_BM, D_MODEL // 2), jnp.int32)

    @pl.when(rv_ref[step * MOE_STEP_BLOCKS] > 0)
    def _():
        for pair in range(MOE_STEP_BLOCKS // 2):
            ia = step * MOE_STEP_BLOCKS + 2 * pair
            ib = ia + 1
            row0 = 2 * pair * MOE_BM
            same = (be_ref[ib] == be_ref[ia]) & (rv_ref[ib] > 0)

            @pl.when(same)
            def _():
                enter_expert(ia)
                experts_ragged(row0, MOE_BM, rv_ref[ib])

            @pl.when(jnp.logical_not(same))
            def _():
                single_block(ia, row0)
                single_block(ib, row0 + MOE_BM)


def _moe_experts(l, block_e, rows_valid, next_e, slot, first_e, xd, w_gate, w_up, w_down):
    n_blocks = xd.shape[0] // MOE_BM
    step_rows = MOE_STEP_BLOCKS * MOE_BM
    row = lambda i, be, rv, nx, sl, fe: (jnp.minimum(i, fe[WEIGHT_SLOTS - 1] - 1), 0)
    return pl.pallas_call(
        functools.partial(_moe_kernel, layer=l),
        grid_spec=pltpu.PrefetchScalarGridSpec(
            num_scalar_prefetch=5,
            grid=(n_blocks // MOE_STEP_BLOCKS,),
            in_specs=[
                pl.BlockSpec((step_rows, D_MODEL // 2), row),
                pl.BlockSpec(memory_space=pl.ANY),
                pl.BlockSpec(memory_space=pl.ANY),
                pl.BlockSpec(memory_space=pl.ANY),
            ],
            out_specs=pl.BlockSpec((step_rows, D_MODEL // 2), row),
            scratch_shapes=[
                pltpu.VMEM((WEIGHT_SLOTS, D_MODEL, EXPERT_FF), F32),
                pltpu.VMEM((WEIGHT_SLOTS, D_MODEL, EXPERT_FF), F32),
                pltpu.VMEM((WEIGHT_SLOTS, EXPERT_FF, D_MODEL), F32),
                pltpu.VMEM((D_MODEL, EXPERT_FF), BF16),
                pltpu.VMEM((D_MODEL, EXPERT_FF), BF16),
                pltpu.VMEM((EXPERT_FF, D_MODEL), BF16),
                pltpu.SemaphoreType.DMA((WEIGHT_SLOTS, 3)),
            ],
        ),
        out_shape=jax.ShapeDtypeStruct((n_blocks * MOE_BM, D_MODEL // 2), jnp.int32),
        compiler_params=pltpu.CompilerParams(
            dimension_semantics=("arbitrary",), vmem_limit_bytes=VMEM_LIMIT),
        name="moe_experts",
    )(block_e, rows_valid, next_e, slot, first_e, xd, w_gate, w_up, w_down)


def _sc_worker_id():
    return lax.axis_index("s") * SC_CORES + lax.axis_index("c")


def _sc_dispatch(hp, hs, dest_p, dest_s, n_rows):
    tp, width = hp.shape
    per_w = tp // SC_WORKERS
    n_ch = per_w // DISP_CH
    n_sw = hs.shape[0] // SAMPLE_CH
    mesh = plsc.VectorSubcoreMesh(core_axis_name="c", subcore_axis_name="s")

    @functools.partial(
        pl.kernel, mesh=mesh,
        out_type=jax.ShapeDtypeStruct((n_rows, width), jnp.int32),
        scratch_types=[
            pltpu.VMEM((2, n_ch, DISP_CH), jnp.int32),
            pltpu.VMEM((2, 1, SAMPLE_CH), jnp.int32),
            pltpu.VMEM((SC_RING, DISP_CH, width), jnp.int32),
            pltpu.SemaphoreType.DMA((SC_RING,)),
            pltpu.SemaphoreType.DMA((SC_RING, 2)),
        ],
        name="sc_dispatch",
    )
    def k(hp_hbm, hs_hbm, dp_hbm, ds_hbm, xd_hbm, idx_v, idxs_v, bufs, rsem, wsem):
        wid = _sc_worker_id()
        base = wid * per_w
        for kk in range(2):
            pltpu.sync_copy(dp_hbm.at[kk, wid], idx_v.at[kk])
        reads = [pltpu.make_async_copy(hp_hbm.at[pl.ds(base + j * DISP_CH, DISP_CH)],
                                       bufs.at[j % SC_RING], rsem.at[j % SC_RING]) for j in range(n_ch)]
        writes = [[pltpu.make_async_copy(bufs.at[j % SC_RING], xd_hbm.at[idx_v.at[kk, j]],
                                         wsem.at[j % SC_RING, kk]) for kk in range(2)] for j in range(n_ch)]
        for j in range(min(SC_RING - 1, n_ch)):
            reads[j].start()
        for j in range(n_ch):
            reads[j].wait()
            for w in writes[j]:
                w.start()
            if j >= 1:
                for w in writes[j - 1]:
                    w.wait()
            if j + SC_RING - 1 < n_ch:
                reads[j + SC_RING - 1].start()
        for w in writes[n_ch - 1]:
            w.wait()

        @pl.when(wid < n_sw)
        def _():
            rows = bufs.at[0, pl.ds(0, SAMPLE_CH)]
            for kk in range(2):
                pltpu.sync_copy(ds_hbm.at[kk, wid], idxs_v.at[kk])
            pltpu.sync_copy(hs_hbm.at[pl.ds(wid * SAMPLE_CH, SAMPLE_CH)], rows)
            for kk in range(2):
                pltpu.sync_copy(rows, xd_hbm.at[idxs_v.at[kk, 0]])

    return k(hp, hs, dest_p, dest_s)


def _sc_sample_gather(yd, dest_s, ts):
    width = yd.shape[1]
    n_sw = ts // SAMPLE_CH
    mesh = plsc.VectorSubcoreMesh(core_axis_name="c", subcore_axis_name="s")

    @functools.partial(
        pl.kernel, mesh=mesh,
        out_type=jax.ShapeDtypeStruct((2, ts, width), yd.dtype),
        scratch_types=[
            pltpu.VMEM((2, 1, SAMPLE_CH), jnp.int32),
            pltpu.VMEM((2, SAMPLE_CH, width), yd.dtype),
        ],
        name="sc_sample_gather",
    )
    def k(yd_hbm, ds_hbm, g_hbm, idxs_v, bufs):
        wid = _sc_worker_id()

        @pl.when(wid < n_sw)
        def _():
            for kk in range(2):
                pltpu.sync_copy(ds_hbm.at[kk, wid], idxs_v.at[kk])
            for kk in range(2):
                pltpu.sync_copy(yd_hbm.at[idxs_v.at[kk, 0]], bufs.at[kk])
                pltpu.sync_copy(bufs.at[kk], g_hbm.at[kk, pl.ds(wid * SAMPLE_CH, SAMPLE_CH)])

    return k(yd, dest_s)


def _sc_combine_gather(yd, dest_p, tp):
    width = yd.shape[1]
    per_w = tp // SC_WORKERS
    n_ch = per_w // COMB_CH
    mesh = plsc.VectorSubcoreMesh(core_axis_name="c", subcore_axis_name="s")

    @functools.partial(
        pl.kernel, mesh=mesh,
        out_type=jax.ShapeDtypeStruct((2, tp, width), yd.dtype),
        scratch_types=[
            pltpu.VMEM((2, n_ch, COMB_CH), jnp.int32),
            pltpu.VMEM((SC_RING, COMB_CH, width), yd.dtype),
            pltpu.SemaphoreType.DMA((SC_RING,)),
            pltpu.SemaphoreType.DMA((SC_RING,)),
        ],
        name="sc_combine_gather",
    )
    def k(yd_hbm, dp_hbm, g_hbm, idx_v, bufs, gsem, wsem):
        wid = _sc_worker_id()
        base = wid * per_w
        for kk in range(2):
            pltpu.sync_copy(dp_hbm.at[kk, wid], idx_v.at[kk])
        items = [(kk, j) for kk in range(2) for j in range(n_ch)]
        n_items = len(items)
        gathers = [pltpu.make_async_copy(yd_hbm.at[idx_v.at[kk, j]], bufs.at[n % SC_RING], gsem.at[n % SC_RING])
                   for n, (kk, j) in enumerate(items)]
        outs = [pltpu.make_async_copy(bufs.at[n % SC_RING], g_hbm.at[kk, pl.ds(base + j * COMB_CH, COMB_CH)],
                                      wsem.at[n % SC_RING]) for n, (kk, j) in enumerate(items)]
        for n in range(min(SC_RING - 1, n_items)):
            gathers[n].start()
        for n in range(n_items):
            gathers[n].wait()
            outs[n].start()
            if n >= 1:
                outs[n - 1].wait()
            if n + SC_RING - 1 < n_items:
                gathers[n + SC_RING - 1].start()
        outs[n_items - 1].wait()

    return k(yd, dest_p)


def _combine_kernel(x1_ref, g_ref, route_t_ref, x2_ref):
    fields = route_t_ref[...]
    tm = fields.shape[1]
    cols = jnp.transpose(jnp.concatenate([fields, jnp.zeros((LANES - ROUTE_FIELDS, tm), F32)], axis=0))
    w1 = cols[:, 2:3]
    w2 = cols[:, 3:4]
    x2_ref[...] = x1_ref[...] + _unpack_bf16_pairs(g_ref[0]) * w1 + _unpack_bf16_pairs(g_ref[1]) * w2


def _combine(x1, g, route_t, row0, tm):
    t = x1.shape[0]
    blk0 = row0 // tm
    row = lambda i: (i, 0)
    return pl.pallas_call(
        _combine_kernel,
        grid=(t // tm,),
        in_specs=[
            pl.BlockSpec((tm, D_MODEL), row),
            pl.BlockSpec((2, tm, D_MODEL // 2), lambda i: (0, blk0 + i, 0)),
            pl.BlockSpec((ROUTE_FIELDS, tm), lambda i: (0, i)),
        ],
        out_specs=pl.BlockSpec((tm, D_MODEL), row),
        out_shape=jax.ShapeDtypeStruct((t, D_MODEL), F32),
        compiler_params=pltpu.CompilerParams(
            dimension_semantics=("arbitrary",), vmem_limit_bytes=VMEM_LIMIT),
        name="combine",
    )(x1, g, route_t)


def _dest_layout(dest, workers, chunk):
    t = dest.shape[1]
    return dest.reshape(2, workers, t // (workers * chunk), chunk)


def _hier_moe(l, h2p, h2s, route_tp, route_ts, counts, w_gate, w_up, w_down):
    tp, ts = h2p.shape[0], h2s.shape[0]
    n_assign = 2 * (tp + ts)
    n_blocks = -(-n_assign // MOE_BM) + N_EXPERTS
    n_blocks = -(-n_blocks // MOE_STEP_BLOCKS) * MOE_STEP_BLOCKS
    pcounts = (counts + MOE_BM - 1) // MOE_BM * MOE_BM
    pends = jnp.cumsum(pcounts)
    poffsets = pends - pcounts
    starts = jnp.arange(n_blocks, dtype=jnp.int32) * MOE_BM
    block_e = jnp.minimum(jnp.sum((pends[None, :] <= starts[:, None]).astype(jnp.int32), axis=1),
                          N_EXPERTS - 1)
    experts = jnp.arange(N_EXPERTS, dtype=jnp.int32)

    def lookup(table, idx):
        return jnp.sum(jnp.where(idx[..., None] == experts, table, 0), axis=-1)

    rows_valid = jnp.clip(lookup(poffsets + counts, block_e) - starts, 0, MOE_BM).astype(jnp.int32)
    used = counts > 0
    last_e = jnp.max(jnp.where(used, jnp.arange(N_EXPERTS, dtype=jnp.int32), 0))
    block_e = jnp.where(rows_valid > 0, block_e, last_e).astype(jnp.int32)
    place = jnp.cumsum(used.astype(jnp.int32)) - 1
    by_place = jnp.sum(jnp.where(used[None, :] & (place[None, :] == experts[:, None]), experts[None, :], 0),
                       axis=1)
    n_used = jnp.sum(used.astype(jnp.int32))

    def at_place(p):
        return jnp.where(p < n_used, lookup(by_place, jnp.minimum(p, N_EXPERTS - 1)), -1).astype(jnp.int32)

    ahead_of = at_place(place + (WEIGHT_SLOTS - 1))
    next_e = lookup(ahead_of, block_e)
    slot = lookup(place % WEIGHT_SLOTS, block_e)
    n_steps_used = -(-(pends[-1] // MOE_BM) // MOE_STEP_BLOCKS)
    first_e = jnp.concatenate([at_place(jnp.arange(WEIGHT_SLOTS - 1, dtype=jnp.int32)),
                               n_steps_used.reshape(1).astype(jnp.int32)])

    def dest_of(route_t):
        return lookup(poffsets, route_t[0:2].astype(jnp.int32)) + route_t[4:6].astype(jnp.int32)

    dest_p, dest_s = dest_of(route_tp), dest_of(route_ts)
    n_sw = ts // SAMPLE_CH
    xd = _sc_dispatch(h2p, h2s, _dest_layout(dest_p, SC_WORKERS, DISP_CH),
                      _dest_layout(dest_s, n_sw, SAMPLE_CH), n_blocks * MOE_BM)
    yd = _moe_experts(l, block_e, rows_valid, next_e, slot, first_e, xd, w_gate, w_up, w_down)
    g_s = _sc_sample_gather(yd, _dest_layout(dest_s, n_sw, SAMPLE_CH), ts)
    g_p = _sc_combine_gather(yd, _dest_layout(dest_p, SC_WORKERS, COMB_CH), tp)
    return g_p, g_s


def kernel(x_prompt, x_sample, state_pool, cache_k_win, cache_v_win, norm_attn_g, w_in, pool_w, pool_scale, q_norm_g, k_norm_g, attn_sinks, w_out, norm_ffn_g, router_group_w, router_group_b, router_expert_w, router_expert_b, w_gate, w_up, w_down):
    n_p, t_p, d = x_prompt.shape
    n_s, t_s, _ = x_sample.shape
    depth = w_in.shape[0]
    lw_s = cache_k_win.shape[2]
    assert t_s == 1 and lw_s == WINDOW and d == D_MODEL
    assert t_p % TM_PROJ == 0 and t_p >= WINDOW

    seg = jnp.arange(256) // HEAD_DIM
    bd = jnp.where(seg[:, None] == seg[None, :], 1.0 / HEAD_DIM, 0.0).astype(BF16)
    slopes = jnp.exp2(-8.0 * jnp.arange(1, N_HEADS + 1, dtype=F32) / N_HEADS)
    bias_p = _prompt_bias_t()
    dist_s = (WINDOW - 1) - jnp.arange(WINDOW, dtype=F32)
    bias_s = -LOG2E * slopes[:, None] * dist_s[None, :]

    wp = jnp.zeros((depth, 2, 256, 256), F32)
    for p in range(2):
        wp = wp.at[:, p, :POOL_GC, :POOL_GC].set(pool_w[:, 2 * p])
        wp = wp.at[:, p, POOL_GC:, POOL_GC:].set(pool_w[:, 2 * p + 1])
    assert GROUP_LANE0 == N_EXPERTS
    lane_pad = LANES - N_EXPERTS - N_EXPERT_GROUPS
    wr = jnp.concatenate([router_expert_w, router_group_w, jnp.zeros((depth, D_MODEL, lane_pad), F32)], axis=-1)
    br = jnp.concatenate([router_expert_b, router_group_b, jnp.zeros((depth, lane_pad), F32)],
                         axis=-1).reshape(depth, 1, LANES)
    lp = dict(
        w_in=w_in.astype(BF16),
        w_out=w_out.astype(BF16),
        g_attn=norm_attn_g.reshape(depth, 1, D_MODEL),
        g_ffn=norm_ffn_g.reshape(depth, 1, D_MODEL),
        qg=(jnp.tile(q_norm_g, (1, N_HEADS)) * (ATTN_SCALE * LOG2E)).reshape(depth, 1, Q_W),
        kg=jnp.tile(k_norm_g, (1, N_KV_HEADS)).reshape(depth, 1, KV_W),
        wp=wp.astype(BF16),
        ps=pool_scale.reshape(depth, 1, POOL_W),
        wr=wr.astype(BF16),
        br=br,
        state=state_pool,
        ck=cache_k_win.reshape(depth, n_s, lw_s, KV_W),
        cv=cache_v_win.reshape(depth, n_s, lw_s, KV_W),
    )

    xp = x_prompt.reshape(n_p * t_p, D_MODEL)
    xs = x_sample.reshape(n_s, D_MODEL)
    lw_p = min(WINDOW, t_p)
    pool_p, kp_new, vp_new = [], [], []
    sample_state = [lp["state"], lp["ck"], lp["cv"]]
    zero_cnt = jnp.zeros((N_EXPERTS, 1), F32)
    pending = None
    for l in range(depth):
        sinks = attn_sinks[l] * LOG2E
        pool_so, attn_so, *sample_state = _sample_mixer(
            l, depth, xs, lp["g_attn"], lp["w_in"], lp["qg"], lp["kg"], bd, lp["wp"], lp["ps"],
            *sample_state, sinks.reshape(N_HEADS, 1), bias_s, PAST_LEN)
        x1s, h2s, route_ts, cnt_s = _merge_router(
            l, pool_so, attn_so, xs, lp["w_out"], lp["g_ffn"], lp["wr"], lp["br"], zero_cnt, n_s)
        outs = _proj_pool_prompt(
            l, xp if pending is None else pending, n_p, t_p,
            lp["g_attn"], lp["w_in"], lp["qg"], lp["kg"], bd, lp["wp"], lp["ps"])
        if pending is not None:
            xp, outs = outs[0], outs[1:]
        pool_o, q, k, vt, utail, ktail, vtail = outs
        if l == depth - 1:
            ks_in, vs_in, q = lax.optimization_barrier((sample_state[1], sample_state[2], q))
            ks_out = ks_in.reshape(depth, n_s, lw_s, N_KV_HEADS, HEAD_DIM)
            vs_out = vs_in.reshape(depth, n_s, lw_s, N_KV_HEADS, HEAD_DIM)
        attn_o = _attn_prompt(q, k, vt, bias_p, sinks, n_p, t_p)
        if l == depth - 1:
            ks_out, vs_out, attn_o = lax.optimization_barrier((ks_out, vs_out, attn_o))
        x1p, h2p, route_tp, cnt_all = _merge_router(
            l, pool_o, attn_o, xp, lp["w_out"], lp["g_ffn"], lp["wr"], lp["br"], cnt_s, TM_MERGE)
        pool_p.append(utail[:, 16 - POOL_STATE:, :])
        kp_new.append(ktail)
        vp_new.append(vtail)
        counts = cnt_all[:, 0].astype(jnp.int32)
        g_p, g_s = _hier_moe(l, h2p, h2s, route_tp, route_ts, counts, w_gate, w_up, w_down)
        xs = _combine(x1s, g_s, route_ts, 0, n_s)
        pending = (x1p, g_p, route_tp)
    xp = _combine(*pending, 0, TM_MERGE)
    return (xp.reshape(n_p, t_p, D_MODEL), xs.reshape(n_s, t_s, D_MODEL),
            jnp.stack(pool_p),
            jnp.stack(kp_new).reshape(depth, n_p, lw_p, N_KV_HEADS, HEAD_DIM),
            jnp.stack(vp_new).reshape(depth, n_p, lw_p, N_KV_HEADS, HEAD_DIM),
            sample_state[0], ks_out, vs_out)
```

```python
import functools

import jax
import jax.numpy as jnp
from jax import lax
from jax.experimental import pallas as pl
from jax.experimental.pallas import tpu as pltpu
from jax.experimental.pallas import tpu_sc as plsc

D_MODEL = 1024
POOL_W = 512
POOL_WINDOWS = (2, 4, 8, 16)
POOL_GC = 128
POOL_STATE = 15
HEAD_DIM = 64
N_HEADS = 8
N_KV_HEADS = 2
GQA_GROUP = 4
Q_W = 512
KV_W = 128
D_IN = POOL_W + Q_W + 2 * KV_W
WINDOW = 128
ATTN_SCALE = HEAD_DIM ** -0.5
LOG2E = 1.4426950408889634
N_EXPERT_GROUPS = 4
EXPERTS_PER_GROUP = 8
N_EXPERTS = 32
EXPERT_FF = 512
EPS = 1e-6
PAST_LEN = 16384

LANES = 128
HALO = 32
TM_PROJ = 1024
TM_MERGE = 1024
MERGE_CHUNKS = 4
ATTN_QB = 16
MOE_BM = 256
MOE_STEP_BLOCKS = 4
WEIGHT_SLOTS = 3
GROUP_LANE0 = 32
ROUTE_FIELDS = 8
SC_CORES = 2
SC_SUBCORES = 16
SC_WORKERS = SC_CORES * SC_SUBCORES
DISP_CH = 64
COMB_CH = 64
SAMPLE_CH = 32
SC_RING = 3
VMEM_LIMIT = 48 * 1024 * 1024

BF16 = jnp.bfloat16
F32 = jnp.float32


def _pack_bf16_pairs(h):
    w = h.shape[1] // 2
    hi = lax.bitcast_convert_type(h[:, :w].astype(F32), jnp.uint32)
    lo = lax.bitcast_convert_type(h[:, w:].astype(F32), jnp.uint32)
    return lax.bitcast_convert_type(hi | (lo >> 16), jnp.int32)


def _unpack_bf16_pairs(words):
    u = lax.bitcast_convert_type(words, jnp.uint32)
    hi = lax.bitcast_convert_type(u & jnp.uint32(0xFFFF0000), F32)
    lo = lax.bitcast_convert_type(u << 16, F32)
    return jnp.concatenate([hi, lo], axis=-1)


def _segment_mean_sq(a, bd):
    w = a.shape[1]
    return jnp.dot((a * a).astype(BF16), bd[:w, :w], preferred_element_type=F32)


def _rms_bf16(x, g):
    ms = jnp.mean(x * x, axis=-1, keepdims=True)
    return (x * lax.rsqrt(ms + EPS) * g).astype(BF16)


def _qk_norm(q, k, qg, kg, bd):
    qn = []
    for c in range(Q_W // 256):
        qc = q[:, c * 256:(c + 1) * 256]
        qn.append(qc * lax.rsqrt(_segment_mean_sq(qc, bd) + EPS))
    qn = jnp.concatenate(qn, axis=-1) * qg
    kn = k * lax.rsqrt(_segment_mean_sq(k, bd) + EPS) * kg
    return qn, kn


def _project(x, g, w_in, qg, kg, bd):
    z = jnp.dot(_rms_bf16(x, g), w_in, preferred_element_type=F32)
    u = z[:, :POOL_W]
    q = z[:, POOL_W:POOL_W + Q_W]
    k = z[:, POOL_W + Q_W:POOL_W + Q_W + KV_W]
    v = z[:, POOL_W + Q_W + KV_W:]
    qn, kn = _qk_norm(q, k, qg, kg, bd)
    return u, qn, kn, v


def _pool_project(d_groups, wp_ref, ps):
    outs = []
    for p in range(2):
        dp = jnp.concatenate([d_groups[2 * p], d_groups[2 * p + 1]], axis=-1).astype(BF16)
        y = jnp.dot(dp, wp_ref[p], preferred_element_type=F32)
        outs.append(y * ps[:, p * 256:(p + 1) * 256])
    return jnp.concatenate(outs, axis=-1)


def _proj_pool_kernel(x_ref, g_ref, win_ref, qg_ref, kg_ref, bd_ref, wp_ref, ps_ref,
                      pool_ref, q_ref, k_ref, vt_ref, utail_ref, ktail_ref, vtail_ref,
                      ext_ref, sa_ref, sb_ref, zq_ref, *, tm, n_j):
    j = pl.program_id(1)

    @pl.when(j == 0)
    def _():
        ext_ref[0:HALO, :] = jnp.zeros((HALO, POOL_W), F32)

    r = tm + HALO
    h = _rms_bf16(x_ref[...], g_ref[...])
    ext_ref[HALO:r, :] = jnp.dot(h, win_ref[:, 0:POOL_W], preferred_element_type=F32)
    zq_ref[...] = jnp.dot(h, win_ref[:, POOL_W:], preferred_element_type=F32)
    u = ext_ref[HALO:r, :]
    sa_ref[8:r, :] = ext_ref[8:r, :] + ext_ref[7:r - 1, :]
    sb_ref[16:r, 128:] = sa_ref[16:r, 128:] + sa_ref[14:r - 2, 128:]
    sa_ref[24:r, 256:] = sb_ref[24:r, 256:] + sb_ref[20:r - 4, 256:]
    sb_ref[32:r, 384:] = sa_ref[32:r, 384:] + sa_ref[24:r - 8, 384:]
    pos1 = j * tm + lax.broadcasted_iota(jnp.int32, (tm, POOL_GC), 0) + 1
    sums = (sa_ref, sb_ref, sa_ref, sb_ref)
    d_groups = []
    for gi, w in enumerate(POOL_WINDOWS):
        sl = slice(gi * POOL_GC, (gi + 1) * POOL_GC)
        cnt = jnp.minimum(pos1, w).astype(F32)
        d_groups.append(sums[gi][HALO:r, sl] / cnt - u[:, sl])
    pool_ref[...] = _pool_project(d_groups, wp_ref, ps_ref[...]).astype(BF16)
    ext_ref[16:HALO, :] = ext_ref[tm + 16:r, :]

    qn, kn = _qk_norm(zq_ref[:, 0:Q_W], zq_ref[:, Q_W:Q_W + KV_W], qg_ref[...], kg_ref[...], bd_ref[...])
    v = zq_ref[:, Q_W + KV_W:]
    q_ref[...] = qn.astype(BF16)
    k_ref[...] = kn.astype(BF16)
    vt_ref[...] = jnp.transpose(v).astype(BF16)

    @pl.when(j == n_j - 1)
    def _():
        utail_ref[...] = u[tm - 16:, :]
        ktail_ref[...] = kn[tm - WINDOW:, :]
        vtail_ref[...] = v[tm - WINDOW:, :]


def _proj_pool_combine_kernel(x1_ref, gath_ref, route_ref, *rest, tm, n_j):
    x2_ref = rest[7]
    _combine_kernel(x1_ref, gath_ref, route_ref, x2_ref)
    _proj_pool_kernel(x2_ref, *rest[:7], *rest[8:], tm=tm, n_j=n_j)


def _proj_pool_prompt(l, x_in, n_seq, seq, g_attn, w_in, qg, kg, bd, wp, ps):
    tm = TM_PROJ
    n_j = seq // tm
    t = n_seq * seq
    row = lambda b, j: (b * n_j + j, 0)
    lay = lambda b, j: (l, 0, 0)
    fused = isinstance(x_in, tuple)
    if fused:
        kern = _proj_pool_combine_kernel
        x_args = list(x_in)
        x_specs = [pl.BlockSpec((tm, D_MODEL), row),
                   pl.BlockSpec((2, tm, D_MODEL // 2), lambda b, j: (0, b * n_j + j, 0)),
                   pl.BlockSpec((ROUTE_FIELDS, tm), lambda b, j: (0, b * n_j + j))]
        x_out_specs = [pl.BlockSpec((tm, D_MODEL), row)]
        x_out_shape = [jax.ShapeDtypeStruct((t, D_MODEL), F32)]
    else:
        kern = _proj_pool_kernel
        x_args = [x_in]
        x_specs = [pl.BlockSpec((tm, D_MODEL), row)]
        x_out_specs, x_out_shape = [], []
    return pl.pallas_call(
        functools.partial(kern, tm=tm, n_j=n_j),
        grid=(n_seq, n_j),
        in_specs=x_specs + [
            pl.BlockSpec((None, 1, D_MODEL), lay),
            pl.BlockSpec((None, D_MODEL, D_IN), lay),
            pl.BlockSpec((None, 1, Q_W), lay),
            pl.BlockSpec((None, 1, KV_W), lay),
            pl.BlockSpec((256, 256), lambda b, j: (0, 0)),
            pl.BlockSpec((None, 2, 256, 256), lambda b, j: (l, 0, 0, 0)),
            pl.BlockSpec((None, 1, POOL_W), lay),
        ],
        out_specs=x_out_specs + [
            pl.BlockSpec((tm, POOL_W), row),
            pl.BlockSpec((tm, Q_W), row),
            pl.BlockSpec((tm, KV_W), row),
            pl.BlockSpec((KV_W, tm), lambda b, j: (0, b * n_j + j)),
            pl.BlockSpec((None, 16, POOL_W), lambda b, j: (b, 0, 0)),
            pl.BlockSpec((None, WINDOW, KV_W), lambda b, j: (b, 0, 0)),
            pl.BlockSpec((None, WINDOW, KV_W), lambda b, j: (b, 0, 0)),
        ],
        out_shape=x_out_shape + [
            jax.ShapeDtypeStruct((t, POOL_W), BF16),
            jax.ShapeDtypeStruct((t, Q_W), BF16),
            jax.ShapeDtypeStruct((t, KV_W), BF16),
            jax.ShapeDtypeStruct((KV_W, t), BF16),
            jax.ShapeDtypeStruct((n_seq, 16, POOL_W), F32),
            jax.ShapeDtypeStruct((n_seq, WINDOW, KV_W), F32),
            jax.ShapeDtypeStruct((n_seq, WINDOW, KV_W), F32),
        ],
        scratch_shapes=[pltpu.VMEM((tm + HALO, POOL_W), F32)] * 3 + [pltpu.VMEM((tm, Q_W + 2 * KV_W), F32)],
        compiler_params=pltpu.CompilerParams(
            dimension_semantics=("arbitrary", "arbitrary"), vmem_limit_bytes=VMEM_LIMIT),
        name="proj_pool_prompt",
    )(*x_args, g_attn, w_in, qg, kg, bd, wp, ps)


def _attn_kernel(sink_ref, q_ref, kp_ref, kc_ref, vtp_ref, vtc_ref, bias_ref, o_ref, s_ref):
    j = pl.program_id(1)
    kk_all = jnp.concatenate([kp_ref[...], kc_ref[...]], axis=0)
    vt_all = jnp.concatenate([vtp_ref[...], vtc_ref[...]], axis=1)
    from_prev = (lax.broadcasted_iota(jnp.int32, (WINDOW, WINDOW), 0)
                 > lax.broadcasted_iota(jnp.int32, (WINDOW, WINDOW), 1))
    units = [(blk, kv) for blk in range(ATTN_QB) for kv in range(N_KV_HEADS)]

    def scores(n):
        blk, kv = units[n]
        q = q_ref[blk * WINDOW:(blk + 1) * WINDOW, :]
        kk = kk_all[blk * WINDOW:(blk + 2) * WINDOW, kv * HEAD_DIM:(kv + 1) * HEAD_DIM]
        heads = range(kv * GQA_GROUP, (kv + 1) * GQA_GROUP)
        q_rows = jnp.concatenate([q[:, h * HEAD_DIM:(h + 1) * HEAD_DIM] for h in heads], axis=0)
        s_ref[n % 3] = lax.dot_general(kk, q_rows, (((1,), (1,)), ((), ())), preferred_element_type=F32)

    scores(0)
    scores(1)
    outs = []
    for n, (blk, kv) in enumerate(units):
        if n + 2 < len(units):
            scores(n + 2)
        vt_kv = vt_all[kv * HEAD_DIM:(kv + 1) * HEAD_DIM, blk * WINDOW:(blk + 2) * WINDOW]
        variant = jnp.minimum(j, 1) if blk == 0 else 1
        for g in range(GQA_GROUP):
            h = kv * GQA_GROUP + g
            s = jnp.where(from_prev, s_ref[n % 3, 0:WINDOW, g * WINDOW:(g + 1) * WINDOW],
                          s_ref[n % 3, WINDOW:, g * WINDOW:(g + 1) * WINDOW]) + bias_ref[variant, h]
            sink = sink_ref[h]
            m = jnp.maximum(jnp.max(s, axis=0, keepdims=True), sink)
            p = jnp.exp2(s - m)
            denom = jnp.sum(p, axis=0, keepdims=True) + jnp.exp2(sink - m)
            p_keys = jnp.concatenate([jnp.where(from_prev, p, 0.0), jnp.where(from_prev, 0.0, p)], axis=0)
            o_t = jnp.dot(vt_kv, p_keys.astype(BF16), preferred_element_type=F32)
            outs.append(o_t / denom)
        if kv == N_KV_HEADS - 1:
            o_ref[blk * WINDOW:(blk + 1) * WINDOW, :] = jnp.transpose(jnp.concatenate(outs, axis=0)).astype(BF16)
            outs = []


def _attn_prompt(q, k, vt, bias_t, sinks, n_seq, seq):
    tq = ATTN_QB * WINDOW
    nj = seq // tq
    t = n_seq * seq
    cur = lambda b, j: (b * nj + j, 0)
    prev = lambda b, j: (jnp.maximum((b * nj + j) * ATTN_QB - 1, 0), 0)
    cur_t = lambda b, j: (0, b * nj + j)
    prev_t = lambda b, j: (0, jnp.maximum((b * nj + j) * ATTN_QB - 1, 0))
    return pl.pallas_call(
        _attn_kernel,
        grid=(n_seq, nj),
        in_specs=[
            pl.BlockSpec(memory_space=pltpu.SMEM),
            pl.BlockSpec((tq, Q_W), cur),
            pl.BlockSpec((WINDOW, KV_W), prev),
            pl.BlockSpec((tq, KV_W), cur),
            pl.BlockSpec((KV_W, WINDOW), prev_t),
            pl.BlockSpec((KV_W, tq), cur_t),
            pl.BlockSpec((2, N_HEADS, WINDOW, WINDOW), lambda b, j: (0, 0, 0, 0)),
        ],
        out_specs=pl.BlockSpec((tq, Q_W), cur),
        out_shape=jax.ShapeDtypeStruct((t, Q_W), BF16),
        scratch_shapes=[pltpu.VMEM((3, 2 * WINDOW, GQA_GROUP * WINDOW), F32)],
        compiler_params=pltpu.CompilerParams(
            dimension_semantics=("arbitrary", "arbitrary"), vmem_limit_bytes=VMEM_LIMIT),
        name="attn_prompt",
    )(sinks, q, k, k, vt, vt, bias_t)


def _prompt_bias_t():
    r = jnp.arange(WINDOW, dtype=jnp.int32)[None, :]
    c = jnp.arange(WINDOW, dtype=jnp.int32)[:, None]
    from_prev = c > r
    dist = r - c + jnp.where(from_prev, WINDOW, 0)
    slopes = jnp.exp2(-8.0 * jnp.arange(1, N_HEADS + 1, dtype=F32) / N_HEADS)
    later = -LOG2E * slopes[:, None, None] * dist.astype(F32)[None]
    first = jnp.where(from_prev[None], -jnp.inf, later)
    return jnp.stack([first, later])


def _sample_kernel(x_ref, g_ref, win_ref, qg_ref, kg_ref, bd_ref, wp_ref, ps_ref,
                   st_ref, ck_ref, cv_ref, sink_ref, bias_ref, perm_ref,
                   pool_ref, attn_ref, pst_ref, kc_ref, vc_ref, *, ns, pos0):
    u, qn, kn, v = _project(x_ref[...], g_ref[...], win_ref[...], qg_ref[...], kg_ref[...], bd_ref[...])
    pst_ref[:, 0:POOL_STATE - 1, :] = st_ref[:, 1:POOL_STATE, :]
    kc_ref[:, 0:WINDOW - 1, :] = ck_ref[:, 1:WINDOW, :]
    vc_ref[:, 0:WINDOW - 1, :] = cv_ref[:, 1:WINDOW, :]
    for n in range(ns):
        pst_ref[n, POOL_STATE - 1:POOL_STATE, :] = u[n:n + 1, :]
        kc_ref[n, WINDOW - 1:WINDOW, :] = kn[n:n + 1, :]
        vc_ref[n, WINDOW - 1:WINDOW, :] = v[n:n + 1, :]

    d_groups = []
    for gi, w in enumerate(POOL_WINDOWS):
        lo = gi * POOL_GC
        acc = u[:, lo:lo + POOL_GC]
        for back in range(1, w):
            acc = acc + st_ref[:, POOL_STATE - back, lo:lo + POOL_GC]
        d_groups.append(acc / float(min(pos0 + 1, w)) - u[:, lo:lo + POOL_GC])
    pool_ref[...] = _pool_project(d_groups, wp_ref, ps_ref[...]).astype(BF16)

    zeros = jnp.zeros((ns, HEAD_DIM), F32)
    stacked = []
    for h in range(N_HEADS):
        piece = qn[:, h * HEAD_DIM:(h + 1) * HEAD_DIM]
        pair = [piece, zeros] if h < GQA_GROUP else [zeros, piece]
        stacked.append(jnp.concatenate(pair, axis=-1))
    q_hn = jnp.concatenate(stacked, axis=0).astype(BF16)
    q_nh = jnp.dot(perm_ref[0], q_hn, preferred_element_type=F32).astype(BF16)

    keys = kc_ref[...].reshape(ns * WINDOW, KV_W).astype(BF16)
    vals = vc_ref[...].reshape(ns * WINDOW, KV_W).astype(BF16)
    s_all = lax.dot_general(q_nh, keys, (((1,), (1,)), ((), ())), preferred_element_type=F32)
    sink = sink_ref[...]
    bias = bias_ref[...]
    zero_blk = jnp.zeros((N_HEADS, WINDOW), F32)
    p_rows = []
    for n in range(ns):
        s = s_all[n * N_HEADS:(n + 1) * N_HEADS, n * WINDOW:(n + 1) * WINDOW] + bias
        m = jnp.maximum(jnp.max(s, axis=-1, keepdims=True), sink)
        p = jnp.exp2(s - m)
        denom = jnp.sum(p, axis=-1, keepdims=True) + jnp.exp2(sink - m)
        p_rows.append(jnp.concatenate([zero_blk] * n + [p / denom] + [zero_blk] * (ns - 1 - n), axis=-1))
    p_blockdiag = jnp.concatenate(p_rows, axis=0).astype(BF16)
    o_nh = jnp.dot(p_blockdiag, vals, preferred_element_type=F32).astype(BF16)
    o_hn = jnp.dot(perm_ref[1], o_nh, preferred_element_type=F32)
    pieces = []
    for h in range(N_HEADS):
        kv = h // GQA_GROUP
        pieces.append(o_hn[h * ns:(h + 1) * ns, kv * HEAD_DIM:(kv + 1) * HEAD_DIM])
    attn_ref[...] = jnp.concatenate(pieces, axis=-1).astype(BF16)


def _sample_mixer(l, depth, xs, g_attn, w_in, qg, kg, bd, wp, ps, state, ck, cv, sink8, bias_s, pos0):
    n = xs.shape[0]
    ns = 32
    row = lambda i: (i, 0)
    lay = lambda i: (l, 0, 0)
    src = jnp.arange(ns * N_HEADS)
    perm = (((src % N_HEADS) * ns + src // N_HEADS)[:, None] == src[None, :]).astype(BF16)
    perms = jnp.stack([perm, perm.T])
    return pl.pallas_call(
        functools.partial(_sample_kernel, ns=ns, pos0=pos0),
        grid=(n // ns,),
        input_output_aliases={8: 2, 9: 3, 10: 4},
        in_specs=[
            pl.BlockSpec((ns, D_MODEL), row),
            pl.BlockSpec((None, 1, D_MODEL), lay),
            pl.BlockSpec((None, D_MODEL, D_IN), lay),
            pl.BlockSpec((None, 1, Q_W), lay),
            pl.BlockSpec((None, 1, KV_W), lay),
            pl.BlockSpec((256, 256), lambda i: (0, 0)),
            pl.BlockSpec((None, 2, 256, 256), lambda i: (l, 0, 0, 0)),
            pl.BlockSpec((None, 1, POOL_W), lay),
            pl.BlockSpec((None, ns, POOL_STATE, POOL_W), lambda i: (l, i, 0, 0)),
            pl.BlockSpec((None, ns, WINDOW, KV_W), lambda i: (l, i, 0, 0)),
            pl.BlockSpec((None, ns, WINDOW, KV_W), lambda i: (l, i, 0, 0)),
            pl.BlockSpec((N_HEADS, 1), lambda i: (0, 0)),
            pl.BlockSpec((N_HEADS, WINDOW), lambda i: (0, 0)),
            pl.BlockSpec((2, ns * N_HEADS, ns * N_HEADS), lambda i: (0, 0, 0)),
        ],
        out_specs=[
            pl.BlockSpec((ns, POOL_W), row),
            pl.BlockSpec((ns, Q_W), row),
            pl.BlockSpec((None, ns, POOL_STATE, POOL_W), lambda i: (l, i, 0, 0)),
            pl.BlockSpec((None, ns, WINDOW, KV_W), lambda i: (l, i, 0, 0)),
            pl.BlockSpec((None, ns, WINDOW, KV_W), lambda i: (l, i, 0, 0)),
        ],
        out_shape=[
            jax.ShapeDtypeStruct((n, POOL_W), BF16),
            jax.ShapeDtypeStruct((n, Q_W), BF16),
            jax.ShapeDtypeStruct((depth, n, POOL_STATE, POOL_W), F32),
            jax.ShapeDtypeStruct((depth, n, WINDOW, KV_W), F32),
            jax.ShapeDtypeStruct((depth, n, WINDOW, KV_W), F32),
        ],
        compiler_params=pltpu.CompilerParams(
            dimension_semantics=("arbitrary",), vmem_limit_bytes=VMEM_LIMIT),
        name="sample_mixer",
    )(xs, g_attn, w_in, qg, kg, bd, wp, ps, state, ck, cv, sink8, bias_s, perms)


def _merge_router_kernel(pool_ref, attn_ref, x_ref, wout_ref, g_ref, wr_ref, br_ref, utri_ref, cin_ref,
                         x1_ref, h2_ref, route_t_ref, cnt_ref, y_ref, lg_ref):
    i = pl.program_id(0)

    @pl.when(i == 0)
    def _():
        cnt_ref[...] = cin_ref[...]

    tm = x_ref.shape[0]
    rc = tm // MERGE_CHUNKS
    chunks = [slice(ci * rc, (ci + 1) * rc) for ci in range(MERGE_CHUNKS)]
    for rows in chunks:
        y_ref[rows, :] = (jnp.dot(pool_ref[rows, :], wout_ref[0:POOL_W, :], preferred_element_type=F32)
                          + jnp.dot(attn_ref[rows, :], wout_ref[POOL_W:, :], preferred_element_type=F32))
    for rows in chunks:
        x1 = x_ref[rows, :] + y_ref[rows, :]
        x1_ref[rows, :] = x1
        h2 = _rms_bf16(x1, g_ref[...])
        h2_ref[rows, :] = _pack_bf16_pairs(h2)
        lg_ref[rows, :] = jnp.dot(h2, wr_ref[...], preferred_element_type=F32) + br_ref[...]
    logits = lg_ref[...]

    lt = jnp.transpose(logits)
    sub = lax.broadcasted_iota(jnp.int32, (EXPERTS_PER_GROUP, tm), 0)
    neg = -jnp.inf
    big = jnp.int32(EXPERTS_PER_GROUP)
    gl = jnp.where(sub < N_EXPERT_GROUPS, lt[GROUP_LANE0:GROUP_LANE0 + EXPERTS_PER_GROUP, :], neg)
    gmax = jnp.max(gl, axis=0, keepdims=True)
    grp = jnp.min(jnp.where(gl == gmax, sub, big), axis=0, keepdims=True)
    g_w = 1.0 / jnp.sum(jnp.exp(gl - gmax), axis=0, keepdims=True)
    el = lt[(N_EXPERT_GROUPS - 1) * EXPERTS_PER_GROUP:N_EXPERT_GROUPS * EXPERTS_PER_GROUP, :]
    for gi in range(N_EXPERT_GROUPS - 2, -1, -1):
        el = jnp.where(grp == gi, lt[gi * EXPERTS_PER_GROUP:(gi + 1) * EXPERTS_PER_GROUP, :], el)
    v1 = jnp.max(el, axis=0, keepdims=True)
    i1 = jnp.min(jnp.where(el == v1, sub, big), axis=0, keepdims=True)
    el2 = jnp.where(sub == i1, neg, el)
    v2 = jnp.max(el2, axis=0, keepdims=True)
    i2 = jnp.min(jnp.where(el2 == v2, sub, big), axis=0, keepdims=True)
    e21 = jnp.exp(v2 - v1)
    w1 = g_w / (1.0 + e21)
    w2 = g_w * e21 / (1.0 + e21)
    e1 = grp * EXPERTS_PER_GROUP + i1
    e2 = grp * EXPERTS_PER_GROUP + i2

    esub = lax.broadcasted_iota(jnp.int32, (N_EXPERTS, tm), 0)
    oh1 = esub == e1
    oh2 = esub == e2
    c = jnp.where(oh1 | oh2, 1.0, 0.0)
    prefix = jnp.dot(c.astype(BF16), utri_ref[...], preferred_element_type=F32) + cnt_ref[...]
    r1 = jnp.sum(jnp.where(oh1, prefix, 0.0), axis=0, keepdims=True)
    r2 = jnp.sum(jnp.where(oh2, prefix, 0.0), axis=0, keepdims=True)
    cnt_ref[...] = cnt_ref[...] + jnp.sum(c, axis=1, keepdims=True)

    fields = jnp.zeros((ROUTE_FIELDS, tm), F32)
    for idx, val in enumerate((e1.astype(F32), e2.astype(F32), w1, w2, r1, r2)):
        fields = jnp.where(sub == idx, val, fields)
    route_t_ref[...] = fields


def _merge_router(l, pool, attn, x2d, w_out, g_ffn, wr, br, cnt_in, tm):
    t = x2d.shape[0]
    utri = (jnp.arange(tm)[:, None] < jnp.arange(tm)[None, :]).astype(BF16)
    row = lambda i: (i, 0)
    lay = lambda i: (l, 0, 0)
    return pl.pallas_call(
        _merge_router_kernel,
        grid=(t // tm,),
        in_specs=[
            pl.BlockSpec((tm, POOL_W), row),
            pl.BlockSpec((tm, Q_W), row),
            pl.BlockSpec((tm, D_MODEL), row),
            pl.BlockSpec((None, D_MODEL, D_MODEL), lay),
            pl.BlockSpec((None, 1, D_MODEL), lay),
            pl.BlockSpec((None, D_MODEL, LANES), lay),
            pl.BlockSpec((None, 1, LANES), lay),
            pl.BlockSpec((tm, tm), lambda i: (0, 0)),
            pl.BlockSpec((N_EXPERTS, 1), lambda i: (0, 0)),
        ],
        out_specs=[
            pl.BlockSpec((tm, D_MODEL), row),
            pl.BlockSpec((tm, D_MODEL // 2), row),
            pl.BlockSpec((ROUTE_FIELDS, tm), lambda i: (0, i)),
            pl.BlockSpec((N_EXPERTS, 1), lambda i: (0, 0)),
        ],
        out_shape=[
            jax.ShapeDtypeStruct((t, D_MODEL), F32),
            jax.ShapeDtypeStruct((t, D_MODEL // 2), jnp.int32),
            jax.ShapeDtypeStruct((ROUTE_FIELDS, t), F32),
            jax.ShapeDtypeStruct((N_EXPERTS, 1), F32),
        ],
        scratch_shapes=[pltpu.VMEM((tm, D_MODEL), F32), pltpu.VMEM((tm, LANES), F32)],
        compiler_params=pltpu.CompilerParams(
            dimension_semantics=("arbitrary",), vmem_limit_bytes=VMEM_LIMIT),
        name="merge_router",
    )(pool, attn, x2d, w_out, g_ffn, wr, br, utri, cnt_in)


def _moe_kernel(be_ref, rv_ref, nx_ref, sl_ref, first_ref, xd_ref, wg_hbm, wu_hbm, wd_hbm, yd_ref,
                wg_f, wu_f, wd_f, wg_s, wu_s, wd_s, sem, *, layer):
    step = pl.program_id(0)

    def weight_copies(e, s):
        return [pltpu.make_async_copy(w_hbm.at[layer, e], w_f.at[s], sem.at[s, n])
                for n, (w_hbm, w_f) in enumerate(((wg_hbm, wg_f), (wu_hbm, wu_f), (wd_hbm, wd_f)))]

    @pl.when(step == 0)
    def _():
        for s in range(WEIGHT_SLOTS - 1):
            @pl.when(first_ref[s] >= 0)
            def _():
                for c in weight_copies(first_ref[s], s):
                    c.start()

    def enter_expert(i):
        expert, slot = be_ref[i], sl_ref[i]
        entering = (i == 0) | (expert != be_ref[jnp.maximum(i - 1, 0)])

        @pl.when(entering)
        def _():
            for c in weight_copies(expert, slot):
                c.wait()

            @pl.when(nx_ref[i] >= 0)
            def _():
                for c in weight_copies(nx_ref[i], lax.rem(slot + WEIGHT_SLOTS - 1, WEIGHT_SLOTS)):
                    c.start(priority=1)

        return entering, slot

    def experts_on(row0, n_rows, rows_valid, cast_slot):
        def cast(w_s, w_f):
            if cast_slot is not None:
                w_s[...] = w_f[cast_slot].astype(BF16)

        rows = pl.ds(row0, n_rows)
        row = lax.broadcasted_iota(jnp.int32, (n_rows, D_MODEL // 2), 0)
        cast(wg_s, wg_f)
        x = _unpack_bf16_pairs(jnp.where(row < rows_valid, xd_ref[rows, :], 0)).astype(BF16)
        cast(wu_s, wu_f)
        gate = jnp.dot(x, wg_s[...], preferred_element_type=F32)
        cast(wd_s, wd_f)
        up = jnp.dot(x, wu_s[...], preferred_element_type=F32)
        act = (gate * jax.nn.sigmoid(gate) * up).astype(BF16)
        y = jnp.dot(act, wd_s[...], preferred_element_type=F32)
        yd_ref[rows, :] = _pack_bf16_pairs(y.astype(BF16))

    def experts_ragged(i, row0, lead_rows, rows_last):
        half = MOE_BM // 2
        entering, slot = enter_expert(i)

        def run(cast_slot):
            @pl.when(rows_last > half)
            def _():
                experts_on(row0, lead_rows + MOE_BM, lead_rows + rows_last, cast_slot)

            @pl.when(rows_last <= half)
            def _():
                experts_on(row0, lead_rows + half, lead_rows + rows_last, cast_slot)
                yd_ref[pl.ds(row0 + lead_rows + half, half), :] = jnp.zeros((half, D_MODEL // 2), jnp.int32)

        pl.when(entering)(lambda: run(slot))
        pl.when(jnp.logical_not(entering))(lambda: run(None))

    def single_block(i, row0):
        @pl.when(rv_ref[i] > 0)
        def _():
            experts_ragged(i, row0, 0, rv_ref[i])

        @pl.when(rv_ref[i] <= 0)
        def _():
            entering, slot = enter_expert(i)

            @pl.when(entering)
            def _():
                wg_s[...] = wg_f[slot].astype(BF16)
                wu_s[...] = wu_f[slot].astype(BF16)
                wd_s[...] = wd_f[slot].astype(BF16)

            yd_ref[pl.ds(row0, MOE_BM), :] = jnp.zeros((MOE_BM, D_MODEL // 2), jnp.int32)

    @pl.when(rv_ref[step * MOE_STEP_BLOCKS] > 0)
    def _():
        for pair in range(MOE_STEP_BLOCKS // 2):
            ia = step * MOE_STEP_BLOCKS + 2 * pair
            ib = ia + 1
            row0 = 2 * pair * MOE_BM
            same = (be_ref[ib] == be_ref[ia]) & (rv_ref[ib] > 0)

            @pl.when(same)
            def _():
                experts_ragged(ia, row0, MOE_BM, rv_ref[ib])

            @pl.when(jnp.logical_not(same))
            def _():
                single_block(ia, row0)
                single_block(ib, row0 + MOE_BM)


def _moe_experts(l, block_e, rows_valid, next_e, slot, first_e, xd, w_gate, w_up, w_down):
    n_blocks = xd.shape[0] // MOE_BM
    step_rows = MOE_STEP_BLOCKS * MOE_BM
    row = lambda i, be, rv, nx, sl, fe: (jnp.minimum(i, fe[WEIGHT_SLOTS - 1] - 1), 0)
    return pl.pallas_call(
        functools.partial(_moe_kernel, layer=l),
        grid_spec=pltpu.PrefetchScalarGridSpec(
            num_scalar_prefetch=5,
            grid=(n_blocks // MOE_STEP_BLOCKS,),
            in_specs=[
                pl.BlockSpec((step_rows, D_MODEL // 2), row),
                pl.BlockSpec(memory_space=pl.ANY),
                pl.BlockSpec(memory_space=pl.ANY),
                pl.BlockSpec(memory_space=pl.ANY),
            ],
            out_specs=pl.BlockSpec((step_rows, D_MODEL // 2), row),
            scratch_shapes=[
                pltpu.VMEM((WEIGHT_SLOTS, D_MODEL, EXPERT_FF), F32),
                pltpu.VMEM((WEIGHT_SLOTS, D_MODEL, EXPERT_FF), F32),
                pltpu.VMEM((WEIGHT_SLOTS, EXPERT_FF, D_MODEL), F32),
                pltpu.VMEM((D_MODEL, EXPERT_FF), BF16),
                pltpu.VMEM((D_MODEL, EXPERT_FF), BF16),
                pltpu.VMEM((EXPERT_FF, D_MODEL), BF16),
                pltpu.SemaphoreType.DMA((WEIGHT_SLOTS, 3)),
            ],
        ),
        out_shape=jax.ShapeDtypeStruct((n_blocks * MOE_BM, D_MODEL // 2), jnp.int32),
        compiler_params=pltpu.CompilerParams(
            dimension_semantics=("arbitrary",), vmem_limit_bytes=VMEM_LIMIT),
        name="moe_experts",
    )(block_e, rows_valid, next_e, slot, first_e, xd, w_gate, w_up, w_down)


def _sc_worker_id():
    return lax.axis_index("s") * SC_CORES + lax.axis_index("c")


def _sc_dispatch(hp, hs, dest_p, dest_s, n_rows):
    tp, width = hp.shape
    per_w = tp // SC_WORKERS
    n_ch = per_w // DISP_CH
    n_sw = hs.shape[0] // SAMPLE_CH
    mesh = plsc.VectorSubcoreMesh(core_axis_name="c", subcore_axis_name="s")

    @functools.partial(
        pl.kernel, mesh=mesh,
        out_type=jax.ShapeDtypeStruct((n_rows, width), jnp.int32),
        scratch_types=[
            pltpu.VMEM((2, n_ch, DISP_CH), jnp.int32),
            pltpu.VMEM((2, 1, SAMPLE_CH), jnp.int32),
            pltpu.VMEM((SC_RING, DISP_CH, width), jnp.int32),
            pltpu.SemaphoreType.DMA((SC_RING,)),
            pltpu.SemaphoreType.DMA((SC_RING, 2)),
        ],
        name="sc_dispatch",
    )
    def k(hp_hbm, hs_hbm, dp_hbm, ds_hbm, xd_hbm, idx_v, idxs_v, bufs, rsem, wsem):
        wid = _sc_worker_id()
        base = wid * per_w
        for kk in range(2):
            pltpu.sync_copy(dp_hbm.at[kk, wid], idx_v.at[kk])
        reads = [pltpu.make_async_copy(hp_hbm.at[pl.ds(base + j * DISP_CH, DISP_CH)],
                                       bufs.at[j % SC_RING], rsem.at[j % SC_RING]) for j in range(n_ch)]
        writes = [[pltpu.make_async_copy(bufs.at[j % SC_RING], xd_hbm.at[idx_v.at[kk, j]],
                                         wsem.at[j % SC_RING, kk]) for kk in range(2)] for j in range(n_ch)]
        for j in range(min(SC_RING - 1, n_ch)):
            reads[j].start()
        for j in range(n_ch):
            reads[j].wait()
            for w in writes[j]:
                w.start()
            if j >= 1:
                for w in writes[j - 1]:
                    w.wait()
            if j + SC_RING - 1 < n_ch:
                reads[j + SC_RING - 1].start()
        for w in writes[n_ch - 1]:
            w.wait()

        @pl.when(wid < n_sw)
        def _():
            rows = bufs.at[0, pl.ds(0, SAMPLE_CH)]
            for kk in range(2):
                pltpu.sync_copy(ds_hbm.at[kk, wid], idxs_v.at[kk])
            pltpu.sync_copy(hs_hbm.at[pl.ds(wid * SAMPLE_CH, SAMPLE_CH)], rows)
            for kk in range(2):
                pltpu.sync_copy(rows, xd_hbm.at[idxs_v.at[kk, 0]])

    return k(hp, hs, dest_p, dest_s)


def _sc_sample_gather(yd, dest_s, ts):
    width = yd.shape[1]
    n_sw = ts // SAMPLE_CH
    mesh = plsc.VectorSubcoreMesh(core_axis_name="c", subcore_axis_name="s")

    @functools.partial(
        pl.kernel, mesh=mesh,
        out_type=jax.ShapeDtypeStruct((2, ts, width), yd.dtype),
        scratch_types=[
            pltpu.VMEM((2, 1, SAMPLE_CH), jnp.int32),
            pltpu.VMEM((2, SAMPLE_CH, width), yd.dtype),
        ],
        name="sc_sample_gather",
    )
    def k(yd_hbm, ds_hbm, g_hbm, idxs_v, bufs):
        wid = _sc_worker_id()

        @pl.when(wid < n_sw)
        def _():
            for kk in range(2):
                pltpu.sync_copy(ds_hbm.at[kk, wid], idxs_v.at[kk])
            for kk in range(2):
                pltpu.sync_copy(yd_hbm.at[idxs_v.at[kk, 0]], bufs.at[kk])
                pltpu.sync_copy(bufs.at[kk], g_hbm.at[kk, pl.ds(wid * SAMPLE_CH, SAMPLE_CH)])

    return k(yd, dest_s)


def _sc_combine_gather(yd, dest_p, tp):
    width = yd.shape[1]
    per_w = tp // SC_WORKERS
    n_ch = per_w // COMB_CH
    mesh = plsc.VectorSubcoreMesh(core_axis_name="c", subcore_axis_name="s")

    @functools.partial(
        pl.kernel, mesh=mesh,
        out_type=jax.ShapeDtypeStruct((2, tp, width), yd.dtype),
        scratch_types=[
            pltpu.VMEM((2, n_ch, COMB_CH), jnp.int32),
            pltpu.VMEM((SC_RING, COMB_CH, width), yd.dtype),
            pltpu.SemaphoreType.DMA((SC_RING,)),
            pltpu.SemaphoreType.DMA((SC_RING,)),
        ],
        name="sc_combine_gather",
    )
    def k(yd_hbm, dp_hbm, g_hbm, idx_v, bufs, gsem, wsem):
        wid = _sc_worker_id()
        base = wid * per_w
        for kk in range(2):
            pltpu.sync_copy(dp_hbm.at[kk, wid], idx_v.at[kk])
        items = [(kk, j) for kk in range(2) for j in range(n_ch)]
        n_items = len(items)
        gathers = [pltpu.make_async_copy(yd_hbm.at[idx_v.at[kk, j]], bufs.at[n % SC_RING], gsem.at[n % SC_RING])
                   for n, (kk, j) in enumerate(items)]
        outs = [pltpu.make_async_copy(bufs.at[n % SC_RING], g_hbm.at[kk, pl.ds(base + j * COMB_CH, COMB_CH)],
                                      wsem.at[n % SC_RING]) for n, (kk, j) in enumerate(items)]
        for n in range(min(SC_RING - 1, n_items)):
            gathers[n].start()
        for n in range(n_items):
            gathers[n].wait()
            outs[n].start()
            if n >= 1:
                outs[n - 1].wait()
            if n + SC_RING - 1 < n_items:
                gathers[n + SC_RING - 1].start()
        outs[n_items - 1].wait()

    return k(yd, dest_p)


def _combine_kernel(x1_ref, g_ref, route_t_ref, x2_ref):
    fields = route_t_ref[...]
    tm = fields.shape[1]
    cols = jnp.transpose(jnp.concatenate([fields, jnp.zeros((LANES - ROUTE_FIELDS, tm), F32)], axis=0))
    w1 = cols[:, 2:3]
    w2 = cols[:, 3:4]
    x2_ref[...] = x1_ref[...] + _unpack_bf16_pairs(g_ref[0]) * w1 + _unpack_bf16_pairs(g_ref[1]) * w2


def _combine(x1, g, route_t, row0, tm):
    t = x1.shape[0]
    blk0 = row0 // tm
    row = lambda i: (i, 0)
    return pl.pallas_call(
        _combine_kernel,
        grid=(t // tm,),
        in_specs=[
            pl.BlockSpec((tm, D_MODEL), row),
            pl.BlockSpec((2, tm, D_MODEL // 2), lambda i: (0, blk0 + i, 0)),
            pl.BlockSpec((ROUTE_FIELDS, tm), lambda i: (0, i)),
        ],
        out_specs=pl.BlockSpec((tm, D_MODEL), row),
        out_shape=jax.ShapeDtypeStruct((t, D_MODEL), F32),
        compiler_params=pltpu.CompilerParams(
            dimension_semantics=("arbitrary",), vmem_limit_bytes=VMEM_LIMIT),
        name="combine",
    )(x1, g, route_t)


def _dest_layout(dest, workers, chunk):
    t = dest.shape[1]
    return dest.reshape(2, workers, t // (workers * chunk), chunk)


def _hier_moe(l, h2p, h2s, route_tp, route_ts, counts, w_gate, w_up, w_down):
    tp, ts = h2p.shape[0], h2s.shape[0]
    n_assign = 2 * (tp + ts)
    n_blocks = -(-n_assign // MOE_BM) + N_EXPERTS
    n_blocks = -(-n_blocks // MOE_STEP_BLOCKS) * MOE_STEP_BLOCKS
    pcounts = (counts + MOE_BM - 1) // MOE_BM * MOE_BM
    pends = jnp.cumsum(pcounts)
    poffsets = pends - pcounts
    starts = jnp.arange(n_blocks, dtype=jnp.int32) * MOE_BM
    block_e = jnp.minimum(jnp.sum((pends[None, :] <= starts[:, None]).astype(jnp.int32), axis=1),
                          N_EXPERTS - 1)
    experts = jnp.arange(N_EXPERTS, dtype=jnp.int32)

    def lookup(table, idx):
        return jnp.sum(jnp.where(idx[..., None] == experts, table, 0), axis=-1)

    rows_valid = jnp.clip(lookup(poffsets + counts, block_e) - starts, 0, MOE_BM).astype(jnp.int32)
    used = counts > 0
    last_e = jnp.max(jnp.where(used, jnp.arange(N_EXPERTS, dtype=jnp.int32), 0))
    block_e = jnp.where(rows_valid > 0, block_e, last_e).astype(jnp.int32)
    place = jnp.cumsum(used.astype(jnp.int32)) - 1
    by_place = jnp.sum(jnp.where(used[None, :] & (place[None, :] == experts[:, None]), experts[None, :], 0),
                       axis=1)
    n_used = jnp.sum(used.astype(jnp.int32))

    def at_place(p):
        return jnp.where(p < n_used, lookup(by_place, jnp.minimum(p, N_EXPERTS - 1)), -1).astype(jnp.int32)

    ahead_of = at_place(place + (WEIGHT_SLOTS - 1))
    next_e = lookup(ahead_of, block_e)
    slot = lookup(place % WEIGHT_SLOTS, block_e)
    n_steps_used = -(-(pends[-1] // MOE_BM) // MOE_STEP_BLOCKS)
    first_e = jnp.concatenate([at_place(jnp.arange(WEIGHT_SLOTS - 1, dtype=jnp.int32)),
                               n_steps_used.reshape(1).astype(jnp.int32)])

    def dest_of(route_t):
        return lookup(poffsets, route_t[0:2].astype(jnp.int32)) + route_t[4:6].astype(jnp.int32)

    dest_p, dest_s = dest_of(route_tp), dest_of(route_ts)
    n_sw = ts // SAMPLE_CH
    xd = _sc_dispatch(h2p, h2s, _dest_layout(dest_p, SC_WORKERS, DISP_CH),
                      _dest_layout(dest_s, n_sw, SAMPLE_CH), n_blocks * MOE_BM)
    yd = _moe_experts(l, block_e, rows_valid, next_e, slot, first_e, xd, w_gate, w_up, w_down)
    g_s = _sc_sample_gather(yd, _dest_layout(dest_s, n_sw, SAMPLE_CH), ts)
    g_p = _sc_combine_gather(yd, _dest_layout(dest_p, SC_WORKERS, COMB_CH), tp)
    return g_p, g_s


def kernel(x_prompt, x_sample, state_pool, cache_k_win, cache_v_win, norm_attn_g, w_in, pool_w, pool_scale, q_norm_g, k_norm_g, attn_sinks, w_out, norm_ffn_g, router_group_w, router_group_b, router_expert_w, router_expert_b, w_gate, w_up, w_down):
    n_p, t_p, d = x_prompt.shape
    n_s, t_s, _ = x_sample.shape
    depth = w_in.shape[0]
    lw_s = cache_k_win.shape[2]
    assert t_s == 1 and lw_s == WINDOW and d == D_MODEL
    assert t_p % TM_PROJ == 0 and t_p >= WINDOW

    seg = jnp.arange(256) // HEAD_DIM
    bd = jnp.where(seg[:, None] == seg[None, :], 1.0 / HEAD_DIM, 0.0).astype(BF16)
    slopes = jnp.exp2(-8.0 * jnp.arange(1, N_HEADS + 1, dtype=F32) / N_HEADS)
    bias_p = _prompt_bias_t()
    dist_s = (WINDOW - 1) - jnp.arange(WINDOW, dtype=F32)
    bias_s = -LOG2E * slopes[:, None] * dist_s[None, :]

    wp = jnp.zeros((depth, 2, 256, 256), F32)
    for p in range(2):
        wp = wp.at[:, p, :POOL_GC, :POOL_GC].set(pool_w[:, 2 * p])
        wp = wp.at[:, p, POOL_GC:, POOL_GC:].set(pool_w[:, 2 * p + 1])
    assert GROUP_LANE0 == N_EXPERTS
    lane_pad = LANES - N_EXPERTS - N_EXPERT_GROUPS
    wr = jnp.concatenate([router_expert_w, router_group_w, jnp.zeros((depth, D_MODEL, lane_pad), F32)], axis=-1)
    br = jnp.concatenate([router_expert_b, router_group_b, jnp.zeros((depth, lane_pad), F32)],
                         axis=-1).reshape(depth, 1, LANES)
    lp = dict(
        w_in=w_in.astype(BF16),
        w_out=w_out.astype(BF16),
        g_attn=norm_attn_g.reshape(depth, 1, D_MODEL),
        g_ffn=norm_ffn_g.reshape(depth, 1, D_MODEL),
        qg=(jnp.tile(q_norm_g, (1, N_HEADS)) * (ATTN_SCALE * LOG2E)).reshape(depth, 1, Q_W),
        kg=jnp.tile(k_norm_g, (1, N_KV_HEADS)).reshape(depth, 1, KV_W),
        wp=wp.astype(BF16),
        ps=pool_scale.reshape(depth, 1, POOL_W),
        wr=wr.astype(BF16),
        br=br,
        state=state_pool,
        ck=cache_k_win.reshape(depth, n_s, lw_s, KV_W),
        cv=cache_v_win.reshape(depth, n_s, lw_s, KV_W),
    )

    xp = x_prompt.reshape(n_p * t_p, D_MODEL)
    xs = x_sample.reshape(n_s, D_MODEL)
    lw_p = min(WINDOW, t_p)
    pool_p, kp_new, vp_new = [], [], []
    sample_state = [lp["state"], lp["ck"], lp["cv"]]
    zero_cnt = jnp.zeros((N_EXPERTS, 1), F32)
    pending = None
    for l in range(depth):
        sinks = attn_sinks[l] * LOG2E
        pool_so, attn_so, *sample_state = _sample_mixer(
            l, depth, xs, lp["g_attn"], lp["w_in"], lp["qg"], lp["kg"], bd, lp["wp"], lp["ps"],
            *sample_state, sinks.reshape(N_HEADS, 1), bias_s, PAST_LEN)
        x1s, h2s, route_ts, cnt_s = _merge_router(
            l, pool_so, attn_so, xs, lp["w_out"], lp["g_ffn"], lp["wr"], lp["br"], zero_cnt, n_s)
        outs = _proj_pool_prompt(
            l, xp if pending is None else pending, n_p, t_p,
            lp["g_attn"], lp["w_in"], lp["qg"], lp["kg"], bd, lp["wp"], lp["ps"])
        if pending is not None:
            xp, outs = outs[0], outs[1:]
        pool_o, q, k, vt, utail, ktail, vtail = outs
        attn_o = _attn_prompt(q, k, vt, bias_p, sinks, n_p, t_p)
        x1p, h2p, route_tp, cnt_all = _merge_router(
            l, pool_o, attn_o, xp, lp["w_out"], lp["g_ffn"], lp["wr"], lp["br"], cnt_s, TM_MERGE)
        pool_p.append(utail[:, 16 - POOL_STATE:, :])
        kp_new.append(ktail)
        vp_new.append(vtail)
        counts = cnt_all[:, 0].astype(jnp.int32)
        g_p, g_s = _hier_moe(l, h2p, h2s, route_tp, route_ts, counts, w_gate, w_up, w_down)
        xs = _combine(x1s, g_s, route_ts, 0, n_s)
        pending = (x1p, g_p, route_tp)
    xp = _combine(*pending, 0, TM_MERGE)
    return (xp.reshape(n_p, t_p, D_MODEL), xs.reshape(n_s, t_s, D_MODEL),
            jnp.stack(pool_p),
            jnp.stack(kp_new).reshape(depth, n_p, lw_p, N_KV_HEADS, HEAD_DIM),
            jnp.stack(vp_new).reshape(depth, n_p, lw_p, N_KV_HEADS, HEAD_DIM),
            sample_state[0],
            sample_state[1].reshape(depth, n_s, lw_s, N_KV_HEADS, HEAD_DIM),
            sample_state[2].reshape(depth, n_s, lw_s, N_KV_HEADS, HEAD_DIM))
```

```python
import functools

import jax
import jax.numpy as jnp
from jax import lax
from jax.experimental import pallas as pl
from jax.experimental.pallas import tpu as pltpu
from jax.experimental.pallas import tpu_sc as plsc

D_MODEL = 1024
POOL_W = 512
POOL_WINDOWS = (2, 4, 8, 16)
POOL_GC = 128
POOL_STATE = 15
HEAD_DIM = 64
N_HEADS = 8
N_KV_HEADS = 2
GQA_GROUP = 4
Q_W = 512
KV_W = 128
D_IN = POOL_W + Q_W + 2 * KV_W
WINDOW = 128
ATTN_SCALE = HEAD_DIM ** -0.5
LOG2E = 1.4426950408889634
N_EXPERT_GROUPS = 4
EXPERTS_PER_GROUP = 8
N_EXPERTS = 32
EXPERT_FF = 512
EPS = 1e-6
PAST_LEN = 16384

LANES = 128
HALO = 32
TM_PROJ = 1024
TM_MERGE = 1024
MERGE_CHUNKS = 4
ATTN_QB = 16
MOE_BM = 256
MOE_STEP_BLOCKS = 4
WEIGHT_SLOTS = 3
GROUP_LANE0 = 32
ROUTE_FIELDS = 8
SC_CORES = 2
SC_SUBCORES = 16
SC_WORKERS = SC_CORES * SC_SUBCORES
DISP_CH = 64
COMB_CH = 64
SAMPLE_CH = 32
SC_RING = 3
VMEM_LIMIT = 48 * 1024 * 1024

BF16 = jnp.bfloat16
F32 = jnp.float32


def _pack_bf16_pairs(h):
    w = h.shape[1] // 2
    hi = lax.bitcast_convert_type(h[:, :w].astype(F32), jnp.uint32)
    lo = lax.bitcast_convert_type(h[:, w:].astype(F32), jnp.uint32)
    return lax.bitcast_convert_type(hi | (lo >> 16), jnp.int32)


def _unpack_bf16_pairs(words):
    u = lax.bitcast_convert_type(words, jnp.uint32)
    hi = lax.bitcast_convert_type(u & jnp.uint32(0xFFFF0000), F32)
    lo = lax.bitcast_convert_type(u << 16, F32)
    return jnp.concatenate([hi, lo], axis=-1)


def _segment_mean_sq(a, bd):
    w = a.shape[1]
    return jnp.dot((a * a).astype(BF16), bd[:w, :w], preferred_element_type=F32)


def _rms_bf16(x, g):
    ms = jnp.mean(x * x, axis=-1, keepdims=True)
    return (x * lax.rsqrt(ms + EPS) * g).astype(BF16)


def _qk_norm(q, k, qg, kg, bd):
    qn = []
    for c in range(Q_W // 256):
        qc = q[:, c * 256:(c + 1) * 256]
        qn.append(qc * lax.rsqrt(_segment_mean_sq(qc, bd) + EPS))
    qn = jnp.concatenate(qn, axis=-1) * qg
    kn = k * lax.rsqrt(_segment_mean_sq(k, bd) + EPS) * kg
    return qn, kn


def _project(x, g, w_in, qg, kg, bd):
    z = jnp.dot(_rms_bf16(x, g), w_in, preferred_element_type=F32)
    u = z[:, :POOL_W]
    q = z[:, POOL_W:POOL_W + Q_W]
    k = z[:, POOL_W + Q_W:POOL_W + Q_W + KV_W]
    v = z[:, POOL_W + Q_W + KV_W:]
    qn, kn = _qk_norm(q, k, qg, kg, bd)
    return u, qn, kn, v


def _pool_project(d_groups, wp_ref, ps):
    outs = []
    for p in range(2):
        dp = jnp.concatenate([d_groups[2 * p], d_groups[2 * p + 1]], axis=-1).astype(BF16)
        y = jnp.dot(dp, wp_ref[p], preferred_element_type=F32)
        outs.append(y * ps[:, p * 256:(p + 1) * 256])
    return jnp.concatenate(outs, axis=-1)


def _proj_pool_kernel(x_ref, g_ref, win_ref, qg_ref, kg_ref, bd_ref, wp_ref, ps_ref,
                      pool_ref, q_ref, k_ref, vt_ref, utail_ref, ktail_ref, vtail_ref,
                      ext_ref, sa_ref, sb_ref, zq_ref, *, tm, n_j):
    j = pl.program_id(1)

    @pl.when(j == 0)
    def _():
        ext_ref[0:HALO, :] = jnp.zeros((HALO, POOL_W), F32)

    r = tm + HALO
    h = _rms_bf16(x_ref[...], g_ref[...])
    ext_ref[HALO:r, :] = jnp.dot(h, win_ref[:, 0:POOL_W], preferred_element_type=F32)
    zq_ref[...] = jnp.dot(h, win_ref[:, POOL_W:], preferred_element_type=F32)
    u = ext_ref[HALO:r, :]
    sa_ref[8:r, :] = ext_ref[8:r, :] + ext_ref[7:r - 1, :]
    sb_ref[16:r, 128:] = sa_ref[16:r, 128:] + sa_ref[14:r - 2, 128:]
    sa_ref[24:r, 256:] = sb_ref[24:r, 256:] + sb_ref[20:r - 4, 256:]
    sb_ref[32:r, 384:] = sa_ref[32:r, 384:] + sa_ref[24:r - 8, 384:]
    pos1 = j * tm + lax.broadcasted_iota(jnp.int32, (tm, POOL_GC), 0) + 1
    sums = (sa_ref, sb_ref, sa_ref, sb_ref)
    d_groups = []
    for gi, w in enumerate(POOL_WINDOWS):
        sl = slice(gi * POOL_GC, (gi + 1) * POOL_GC)
        cnt = jnp.minimum(pos1, w).astype(F32)
        d_groups.append(sums[gi][HALO:r, sl] / cnt - u[:, sl])
    pool_ref[...] = _pool_project(d_groups, wp_ref, ps_ref[...]).astype(BF16)
    ext_ref[16:HALO, :] = ext_ref[tm + 16:r, :]

    qn, kn = _qk_norm(zq_ref[:, 0:Q_W], zq_ref[:, Q_W:Q_W + KV_W], qg_ref[...], kg_ref[...], bd_ref[...])
    v = zq_ref[:, Q_W + KV_W:]
    q_ref[...] = qn.astype(BF16)
    k_ref[...] = kn.astype(BF16)
    vt_ref[...] = jnp.transpose(v).astype(BF16)

    @pl.when(j == n_j - 1)
    def _():
        utail_ref[...] = u[tm - 16:, :]
        ktail_ref[...] = kn[tm - WINDOW:, :]
        vtail_ref[...] = v[tm - WINDOW:, :]


def _proj_pool_combine_kernel(x1_ref, gath_ref, route_ref, *rest, tm, n_j):
    x2_ref = rest[7]
    _combine_kernel(x1_ref, gath_ref, route_ref, x2_ref)
    _proj_pool_kernel(x2_ref, *rest[:7], *rest[8:], tm=tm, n_j=n_j)


def _proj_pool_prompt(l, x_in, n_seq, seq, g_attn, w_in, qg, kg, bd, wp, ps):
    tm = TM_PROJ
    n_j = seq // tm
    t = n_seq * seq
    row = lambda b, j: (b * n_j + j, 0)
    lay = lambda b, j: (l, 0, 0)
    fused = isinstance(x_in, tuple)
    if fused:
        kern = _proj_pool_combine_kernel
        x_args = list(x_in)
        x_specs = [pl.BlockSpec((tm, D_MODEL), row),
                   pl.BlockSpec((2, tm, D_MODEL // 2), lambda b, j: (0, b * n_j + j, 0)),
                   pl.BlockSpec((ROUTE_FIELDS, tm), lambda b, j: (0, b * n_j + j))]
        x_out_specs = [pl.BlockSpec((tm, D_MODEL), row)]
        x_out_shape = [jax.ShapeDtypeStruct((t, D_MODEL), F32)]
    else:
        kern = _proj_pool_kernel
        x_args = [x_in]
        x_specs = [pl.BlockSpec((tm, D_MODEL), row)]
        x_out_specs, x_out_shape = [], []
    return pl.pallas_call(
        functools.partial(kern, tm=tm, n_j=n_j),
        grid=(n_seq, n_j),
        in_specs=x_specs + [
            pl.BlockSpec((None, 1, D_MODEL), lay),
            pl.BlockSpec((None, D_MODEL, D_IN), lay),
            pl.BlockSpec((None, 1, Q_W), lay),
            pl.BlockSpec((None, 1, KV_W), lay),
            pl.BlockSpec((256, 256), lambda b, j: (0, 0)),
            pl.BlockSpec((None, 2, 256, 256), lambda b, j: (l, 0, 0, 0)),
            pl.BlockSpec((None, 1, POOL_W), lay),
        ],
        out_specs=x_out_specs + [
            pl.BlockSpec((tm, POOL_W), row),
            pl.BlockSpec((tm, Q_W), row),
            pl.BlockSpec((tm, KV_W), row),
            pl.BlockSpec((KV_W, tm), lambda b, j: (0, b * n_j + j)),
            pl.BlockSpec((None, 16, POOL_W), lambda b, j: (b, 0, 0)),
            pl.BlockSpec((None, WINDOW, KV_W), lambda b, j: (b, 0, 0)),
            pl.BlockSpec((None, WINDOW, KV_W), lambda b, j: (b, 0, 0)),
        ],
        out_shape=x_out_shape + [
            jax.ShapeDtypeStruct((t, POOL_W), BF16),
            jax.ShapeDtypeStruct((t, Q_W), BF16),
            jax.ShapeDtypeStruct((t, KV_W), BF16),
            jax.ShapeDtypeStruct((KV_W, t), BF16),
            jax.ShapeDtypeStruct((n_seq, 16, POOL_W), F32),
            jax.ShapeDtypeStruct((n_seq, WINDOW, KV_W), F32),
            jax.ShapeDtypeStruct((n_seq, WINDOW, KV_W), F32),
        ],
        scratch_shapes=[pltpu.VMEM((tm + HALO, POOL_W), F32)] * 3 + [pltpu.VMEM((tm, Q_W + 2 * KV_W), F32)],
        compiler_params=pltpu.CompilerParams(
            dimension_semantics=("arbitrary", "arbitrary"), vmem_limit_bytes=VMEM_LIMIT,
            allow_input_fusion=[False] * len(x_args) + [False, True, False, False, False, True, False]),
        name="proj_pool_prompt",
    )(*x_args, g_attn, w_in, qg, kg, bd, wp, ps)


def _attn_kernel(sink_ref, q_ref, kp_ref, kc_ref, vtp_ref, vtc_ref, bias_ref, o_ref, s_ref):
    j = pl.program_id(1)
    kk_all = jnp.concatenate([kp_ref[...], kc_ref[...]], axis=0)
    vt_all = jnp.concatenate([vtp_ref[...], vtc_ref[...]], axis=1)
    from_prev = (lax.broadcasted_iota(jnp.int32, (WINDOW, WINDOW), 0)
                 > lax.broadcasted_iota(jnp.int32, (WINDOW, WINDOW), 1))
    units = [(blk, kv) for blk in range(ATTN_QB) for kv in range(N_KV_HEADS)]

    def scores(n):
        blk, kv = units[n]
        q = q_ref[blk * WINDOW:(blk + 1) * WINDOW, :]
        kk = kk_all[blk * WINDOW:(blk + 2) * WINDOW, kv * HEAD_DIM:(kv + 1) * HEAD_DIM]
        heads = range(kv * GQA_GROUP, (kv + 1) * GQA_GROUP)
        q_rows = jnp.concatenate([q[:, h * HEAD_DIM:(h + 1) * HEAD_DIM] for h in heads], axis=0)
        s_ref[n % 3] = lax.dot_general(kk, q_rows, (((1,), (1,)), ((), ())), preferred_element_type=F32)

    scores(0)
    scores(1)
    outs = []
    for n, (blk, kv) in enumerate(units):
        if n + 2 < len(units):
            scores(n + 2)
        vt_kv = vt_all[kv * HEAD_DIM:(kv + 1) * HEAD_DIM, blk * WINDOW:(blk + 2) * WINDOW]
        variant = jnp.minimum(j, 1) if blk == 0 else 1
        for g in range(GQA_GROUP):
            h = kv * GQA_GROUP + g
            s = jnp.where(from_prev, s_ref[n % 3, 0:WINDOW, g * WINDOW:(g + 1) * WINDOW],
                          s_ref[n % 3, WINDOW:, g * WINDOW:(g + 1) * WINDOW]) + bias_ref[variant, h]
            sink = sink_ref[h]
            m = jnp.maximum(jnp.max(s, axis=0, keepdims=True), sink)
            p = jnp.exp2(s - m)
            denom = jnp.sum(p, axis=0, keepdims=True) + jnp.exp2(sink - m)
            p_keys = jnp.concatenate([jnp.where(from_prev, p, 0.0), jnp.where(from_prev, 0.0, p)], axis=0)
            o_t = jnp.dot(vt_kv, p_keys.astype(BF16), preferred_element_type=F32)
            outs.append(o_t / denom)
        if kv == N_KV_HEADS - 1:
            o_ref[blk * WINDOW:(blk + 1) * WINDOW, :] = jnp.transpose(jnp.concatenate(outs, axis=0)).astype(BF16)
            outs = []


def _attn_prompt(q, k, vt, bias_t, sinks, n_seq, seq):
    tq = ATTN_QB * WINDOW
    nj = seq // tq
    t = n_seq * seq
    cur = lambda b, j: (b * nj + j, 0)
    prev = lambda b, j: (jnp.maximum((b * nj + j) * ATTN_QB - 1, 0), 0)
    cur_t = lambda b, j: (0, b * nj + j)
    prev_t = lambda b, j: (0, jnp.maximum((b * nj + j) * ATTN_QB - 1, 0))
    return pl.pallas_call(
        _attn_kernel,
        grid=(n_seq, nj),
        in_specs=[
            pl.BlockSpec(memory_space=pltpu.SMEM),
            pl.BlockSpec((tq, Q_W), cur),
            pl.BlockSpec((WINDOW, KV_W), prev),
            pl.BlockSpec((tq, KV_W), cur),
            pl.BlockSpec((KV_W, WINDOW), prev_t),
            pl.BlockSpec((KV_W, tq), cur_t),
            pl.BlockSpec((2, N_HEADS, WINDOW, WINDOW), lambda b, j: (0, 0, 0, 0)),
        ],
        out_specs=pl.BlockSpec((tq, Q_W), cur),
        out_shape=jax.ShapeDtypeStruct((t, Q_W), BF16),
        scratch_shapes=[pltpu.VMEM((3, 2 * WINDOW, GQA_GROUP * WINDOW), F32)],
        compiler_params=pltpu.CompilerParams(
            dimension_semantics=("arbitrary", "arbitrary"), vmem_limit_bytes=VMEM_LIMIT),
        name="attn_prompt",
    )(sinks, q, k, k, vt, vt, bias_t)


def _prompt_bias_t():
    r = jnp.arange(WINDOW, dtype=jnp.int32)[None, :]
    c = jnp.arange(WINDOW, dtype=jnp.int32)[:, None]
    from_prev = c > r
    dist = r - c + jnp.where(from_prev, WINDOW, 0)
    slopes = jnp.exp2(-8.0 * jnp.arange(1, N_HEADS + 1, dtype=F32) / N_HEADS)
    later = -LOG2E * slopes[:, None, None] * dist.astype(F32)[None]
    first = jnp.where(from_prev[None], -jnp.inf, later)
    return jnp.stack([first, later])


def _sample_kernel(x_ref, g_ref, win_ref, qg_ref, kg_ref, bd_ref, wp_ref, ps_ref,
                   st_ref, ck_ref, cv_ref, sink_ref, bias_ref, perm_ref,
                   pool_ref, attn_ref, pst_ref, kc_ref, vc_ref, *, ns, pos0):
    u, qn, kn, v = _project(x_ref[...], g_ref[...], win_ref[...], qg_ref[...], kg_ref[...], bd_ref[...])
    pst_ref[:, 0:POOL_STATE - 1, :] = st_ref[:, 1:POOL_STATE, :]
    kc_ref[:, 0:WINDOW - 1, :] = ck_ref[:, 1:WINDOW, :]
    vc_ref[:, 0:WINDOW - 1, :] = cv_ref[:, 1:WINDOW, :]
    for n in range(ns):
        pst_ref[n, POOL_STATE - 1:POOL_STATE, :] = u[n:n + 1, :]
        kc_ref[n, WINDOW - 1:WINDOW, :] = kn[n:n + 1, :]
        vc_ref[n, WINDOW - 1:WINDOW, :] = v[n:n + 1, :]

    d_groups = []
    for gi, w in enumerate(POOL_WINDOWS):
        lo = gi * POOL_GC
        acc = u[:, lo:lo + POOL_GC]
        for back in range(1, w):
            acc = acc + st_ref[:, POOL_STATE - back, lo:lo + POOL_GC]
        d_groups.append(acc / float(min(pos0 + 1, w)) - u[:, lo:lo + POOL_GC])
    pool_ref[...] = _pool_project(d_groups, wp_ref, ps_ref[...]).astype(BF16)

    zeros = jnp.zeros((ns, HEAD_DIM), F32)
    stacked = []
    for h in range(N_HEADS):
        piece = qn[:, h * HEAD_DIM:(h + 1) * HEAD_DIM]
        pair = [piece, zeros] if h < GQA_GROUP else [zeros, piece]
        stacked.append(jnp.concatenate(pair, axis=-1))
    q_hn = jnp.concatenate(stacked, axis=0).astype(BF16)
    q_nh = jnp.dot(perm_ref[0], q_hn, preferred_element_type=F32).astype(BF16)

    keys = kc_ref[...].reshape(ns * WINDOW, KV_W).astype(BF16)
    vals = vc_ref[...].reshape(ns * WINDOW, KV_W).astype(BF16)
    s_all = lax.dot_general(q_nh, keys, (((1,), (1,)), ((), ())), preferred_element_type=F32)
    sink = sink_ref[...]
    bias = bias_ref[...]
    zero_blk = jnp.zeros((N_HEADS, WINDOW), F32)
    p_rows = []
    for n in range(ns):
        s = s_all[n * N_HEADS:(n + 1) * N_HEADS, n * WINDOW:(n + 1) * WINDOW] + bias
        m = jnp.maximum(jnp.max(s, axis=-1, keepdims=True), sink)
        p = jnp.exp2(s - m)
        denom = jnp.sum(p, axis=-1, keepdims=True) + jnp.exp2(sink - m)
        p_rows.append(jnp.concatenate([zero_blk] * n + [p / denom] + [zero_blk] * (ns - 1 - n), axis=-1))
    p_blockdiag = jnp.concatenate(p_rows, axis=0).astype(BF16)
    o_nh = jnp.dot(p_blockdiag, vals, preferred_element_type=F32).astype(BF16)
    o_hn = jnp.dot(perm_ref[1], o_nh, preferred_element_type=F32)
    pieces = []
    for h in range(N_HEADS):
        kv = h // GQA_GROUP
        pieces.append(o_hn[h * ns:(h + 1) * ns, kv * HEAD_DIM:(kv + 1) * HEAD_DIM])
    attn_ref[...] = jnp.concatenate(pieces, axis=-1).astype(BF16)


def _sample_mixer(l, depth, xs, g_attn, w_in, qg, kg, bd, wp, ps, state, ck, cv, sink8, bias_s, pos0):
    n = xs.shape[0]
    ns = 32
    row = lambda i: (i, 0)
    lay = lambda i: (l, 0, 0)
    src = jnp.arange(ns * N_HEADS)
    perm = (((src % N_HEADS) * ns + src // N_HEADS)[:, None] == src[None, :]).astype(BF16)
    perms = jnp.stack([perm, perm.T])
    return pl.pallas_call(
        functools.partial(_sample_kernel, ns=ns, pos0=pos0),
        grid=(n // ns,),
        input_output_aliases={8: 2, 9: 3, 10: 4},
        in_specs=[
            pl.BlockSpec((ns, D_MODEL), row),
            pl.BlockSpec((None, 1, D_MODEL), lay),
            pl.BlockSpec((None, D_MODEL, D_IN), lay),
            pl.BlockSpec((None, 1, Q_W), lay),
            pl.BlockSpec((None, 1, KV_W), lay),
            pl.BlockSpec((256, 256), lambda i: (0, 0)),
            pl.BlockSpec((None, 2, 256, 256), lambda i: (l, 0, 0, 0)),
            pl.BlockSpec((None, 1, POOL_W), lay),
            pl.BlockSpec((None, ns, POOL_STATE, POOL_W), lambda i: (l, i, 0, 0)),
            pl.BlockSpec((None, ns, WINDOW, KV_W), lambda i: (l, i, 0, 0)),
            pl.BlockSpec((None, ns, WINDOW, KV_W), lambda i: (l, i, 0, 0)),
            pl.BlockSpec((N_HEADS, 1), lambda i: (0, 0)),
            pl.BlockSpec((N_HEADS, WINDOW), lambda i: (0, 0)),
            pl.BlockSpec((2, ns * N_HEADS, ns * N_HEADS), lambda i: (0, 0, 0)),
        ],
        out_specs=[
            pl.BlockSpec((ns, POOL_W), row),
            pl.BlockSpec((ns, Q_W), row),
            pl.BlockSpec((None, ns, POOL_STATE, POOL_W), lambda i: (l, i, 0, 0)),
            pl.BlockSpec((None, ns, WINDOW, KV_W), lambda i: (l, i, 0, 0)),
            pl.BlockSpec((None, ns, WINDOW, KV_W), lambda i: (l, i, 0, 0)),
        ],
        out_shape=[
            jax.ShapeDtypeStruct((n, POOL_W), BF16),
            jax.ShapeDtypeStruct((n, Q_W), BF16),
            jax.ShapeDtypeStruct((depth, n, POOL_STATE, POOL_W), F32),
            jax.ShapeDtypeStruct((depth, n, WINDOW, KV_W), F32),
            jax.ShapeDtypeStruct((depth, n, WINDOW, KV_W), F32),
        ],
        compiler_params=pltpu.CompilerParams(
            dimension_semantics=("arbitrary",), vmem_limit_bytes=VMEM_LIMIT,
            allow_input_fusion=[i in (2, 6) for i in range(14)]),
        name="sample_mixer",
    )(xs, g_attn, w_in, qg, kg, bd, wp, ps, state, ck, cv, sink8, bias_s, perms)


def _merge_router_kernel(pool_ref, attn_ref, x_ref, wout_ref, g_ref, wr_ref, br_ref, utri_ref, cin_ref,
                         x1_ref, h2_ref, route_t_ref, cnt_ref, y_ref, lg_ref):
    i = pl.program_id(0)

    @pl.when(i == 0)
    def _():
        cnt_ref[...] = cin_ref[...]

    tm = x_ref.shape[0]
    rc = tm // MERGE_CHUNKS
    chunks = [slice(ci * rc, (ci + 1) * rc) for ci in range(MERGE_CHUNKS)]
    for rows in chunks:
        y_ref[rows, :] = (jnp.dot(pool_ref[rows, :], wout_ref[0:POOL_W, :], preferred_element_type=F32)
                          + jnp.dot(attn_ref[rows, :], wout_ref[POOL_W:, :], preferred_element_type=F32))
    for rows in chunks:
        x1 = x_ref[rows, :] + y_ref[rows, :]
        x1_ref[rows, :] = x1
        h2 = _rms_bf16(x1, g_ref[...])
        h2_ref[rows, :] = _pack_bf16_pairs(h2)
        lg_ref[rows, :] = jnp.dot(h2, wr_ref[...], preferred_element_type=F32) + br_ref[...]
    logits = lg_ref[...]

    lt = jnp.transpose(logits)
    sub = lax.broadcasted_iota(jnp.int32, (EXPERTS_PER_GROUP, tm), 0)
    neg = -jnp.inf
    big = jnp.int32(EXPERTS_PER_GROUP)
    gl = jnp.where(sub < N_EXPERT_GROUPS, lt[GROUP_LANE0:GROUP_LANE0 + EXPERTS_PER_GROUP, :], neg)
    gmax = jnp.max(gl, axis=0, keepdims=True)
    grp = jnp.min(jnp.where(gl == gmax, sub, big), axis=0, keepdims=True)
    g_w = 1.0 / jnp.sum(jnp.exp(gl - gmax), axis=0, keepdims=True)
    el = lt[(N_EXPERT_GROUPS - 1) * EXPERTS_PER_GROUP:N_EXPERT_GROUPS * EXPERTS_PER_GROUP, :]
    for gi in range(N_EXPERT_GROUPS - 2, -1, -1):
        el = jnp.where(grp == gi, lt[gi * EXPERTS_PER_GROUP:(gi + 1) * EXPERTS_PER_GROUP, :], el)
    v1 = jnp.max(el, axis=0, keepdims=True)
    i1 = jnp.min(jnp.where(el == v1, sub, big), axis=0, keepdims=True)
    el2 = jnp.where(sub == i1, neg, el)
    v2 = jnp.max(el2, axis=0, keepdims=True)
    i2 = jnp.min(jnp.where(el2 == v2, sub, big), axis=0, keepdims=True)
    e21 = jnp.exp(v2 - v1)
    w1 = g_w / (1.0 + e21)
    w2 = g_w * e21 / (1.0 + e21)
    e1 = grp * EXPERTS_PER_GROUP + i1
    e2 = grp * EXPERTS_PER_GROUP + i2

    esub = lax.broadcasted_iota(jnp.int32, (N_EXPERTS, tm), 0)
    oh1 = esub == e1
    oh2 = esub == e2
    c = jnp.where(oh1 | oh2, 1.0, 0.0)
    prefix = jnp.dot(c.astype(BF16), utri_ref[...], preferred_element_type=F32) + cnt_ref[...]
    r1 = jnp.sum(jnp.where(oh1, prefix, 0.0), axis=0, keepdims=True)
    r2 = jnp.sum(jnp.where(oh2, prefix, 0.0), axis=0, keepdims=True)
    cnt_ref[...] = cnt_ref[...] + jnp.sum(c, axis=1, keepdims=True)

    fields = jnp.zeros((ROUTE_FIELDS, tm), F32)
    for idx, val in enumerate((e1.astype(F32), e2.astype(F32), w1, w2, r1, r2)):
        fields = jnp.where(sub == idx, val, fields)
    route_t_ref[...] = fields


def _merge_router(l, pool, attn, x2d, w_out, g_ffn, wr, br, cnt_in, tm):
    t = x2d.shape[0]
    utri = (jnp.arange(tm)[:, None] < jnp.arange(tm)[None, :]).astype(BF16)
    row = lambda i: (i, 0)
    lay = lambda i: (l, 0, 0)
    return pl.pallas_call(
        _merge_router_kernel,
        grid=(t // tm,),
        in_specs=[
            pl.BlockSpec((tm, POOL_W), row),
            pl.BlockSpec((tm, Q_W), row),
            pl.BlockSpec((tm, D_MODEL), row),
            pl.BlockSpec((None, D_MODEL, D_MODEL), lay),
            pl.BlockSpec((None, 1, D_MODEL), lay),
            pl.BlockSpec((None, D_MODEL, LANES), lay),
            pl.BlockSpec((None, 1, LANES), lay),
            pl.BlockSpec((tm, tm), lambda i: (0, 0)),
            pl.BlockSpec((N_EXPERTS, 1), lambda i: (0, 0)),
        ],
        out_specs=[
            pl.BlockSpec((tm, D_MODEL), row),
            pl.BlockSpec((tm, D_MODEL // 2), row),
            pl.BlockSpec((ROUTE_FIELDS, tm), lambda i: (0, i)),
            pl.BlockSpec((N_EXPERTS, 1), lambda i: (0, 0)),
        ],
        out_shape=[
            jax.ShapeDtypeStruct((t, D_MODEL), F32),
            jax.ShapeDtypeStruct((t, D_MODEL // 2), jnp.int32),
            jax.ShapeDtypeStruct((ROUTE_FIELDS, t), F32),
            jax.ShapeDtypeStruct((N_EXPERTS, 1), F32),
        ],
        scratch_shapes=[pltpu.VMEM((tm, D_MODEL), F32), pltpu.VMEM((tm, LANES), F32)],
        compiler_params=pltpu.CompilerParams(
            dimension_semantics=("arbitrary",), vmem_limit_bytes=VMEM_LIMIT,
            allow_input_fusion=[False, False, False, True, False, True, False, False, False]),
        name="merge_router",
    )(pool, attn, x2d, w_out, g_ffn, wr, br, utri, cnt_in)


def _moe_kernel(be_ref, rv_ref, nx_ref, sl_ref, first_ref, xd_ref, wg_hbm, wu_hbm, wd_hbm, yd_ref,
                wg_f, wu_f, wd_f, wg_s, wu_s, wd_s, sem, *, layer):
    step = pl.program_id(0)

    def weight_copies(e, s):
        return [pltpu.make_async_copy(w_hbm.at[layer, e], w_f.at[s], sem.at[s, n])
                for n, (w_hbm, w_f) in enumerate(((wg_hbm, wg_f), (wu_hbm, wu_f), (wd_hbm, wd_f)))]

    @pl.when(step == 0)
    def _():
        for s in range(WEIGHT_SLOTS - 1):
            @pl.when(first_ref[s] >= 0)
            def _():
                for c in weight_copies(first_ref[s], s):
                    c.start()

    def enter_expert(i):
        expert, slot = be_ref[i], sl_ref[i]

        @pl.when((i == 0) | (expert != be_ref[jnp.maximum(i - 1, 0)]))
        def _():
            for c in weight_copies(expert, slot):
                c.wait()

            @pl.when(nx_ref[i] >= 0)
            def _():
                for c in weight_copies(nx_ref[i], lax.rem(slot + WEIGHT_SLOTS - 1, WEIGHT_SLOTS)):
                    c.start(priority=1)

            wg_s[...] = wg_f[slot].astype(BF16)
            wu_s[...] = wu_f[slot].astype(BF16)
            wd_s[...] = wd_f[slot].astype(BF16)

    def experts_on(row0, n_rows, rows_valid):
        rows = pl.ds(row0, n_rows)
        row = lax.broadcasted_iota(jnp.int32, (n_rows, D_MODEL // 2), 0)
        x = _unpack_bf16_pairs(jnp.where(row < rows_valid, xd_ref[rows, :], 0)).astype(BF16)
        gate = jnp.dot(x, wg_s[...], preferred_element_type=F32)
        up = jnp.dot(x, wu_s[...], preferred_element_type=F32)
        act = (gate * jax.nn.sigmoid(gate) * up).astype(BF16)
        y = jnp.dot(act, wd_s[...], preferred_element_type=F32)
        yd_ref[rows, :] = _pack_bf16_pairs(y.astype(BF16))

    def experts_ragged(row0, lead_rows, rows_last):
        half = MOE_BM // 2

        @pl.when(rows_last > half)
        def _():
            experts_on(row0, lead_rows + MOE_BM, lead_rows + rows_last)

        @pl.when(rows_last <= half)
        def _():
            experts_on(row0, lead_rows + half, lead_rows + rows_last)
            yd_ref[pl.ds(row0 + lead_rows + half, half), :] = jnp.zeros((half, D_MODEL // 2), jnp.int32)

    def single_block(i, row0):
        enter_expert(i)

        @pl.when(rv_ref[i] > 0)
        def _():
            experts_ragged(row0, 0, rv_ref[i])

        @pl.when(rv_ref[i] <= 0)
        def _():
            yd_ref[pl.ds(row0, MOE_BM), :] = jnp.zeros((MOE_BM, D_MODEL // 2), jnp.int32)

    @pl.when(rv_ref[step * MOE_STEP_BLOCKS] > 0)
    def _():
        for pair in range(MOE_STEP_BLOCKS // 2):
            ia = step * MOE_STEP_BLOCKS + 2 * pair
            ib = ia + 1
            row0 = 2 * pair * MOE_BM
            same = (be_ref[ib] == be_ref[ia]) & (rv_ref[ib] > 0)

            @pl.when(same)
            def _():
                enter_expert(ia)
                experts_ragged(row0, MOE_BM, rv_ref[ib])

            @pl.when(jnp.logical_not(same))
            def _():
                single_block(ia, row0)
                single_block(ib, row0 + MOE_BM)


def _moe_experts(l, block_e, rows_valid, next_e, slot, first_e, xd, w_gate, w_up, w_down):
    n_blocks = xd.shape[0] // MOE_BM
    step_rows = MOE_STEP_BLOCKS * MOE_BM
    row = lambda i, be, rv, nx, sl, fe: (jnp.minimum(i, fe[WEIGHT_SLOTS - 1] - 1), 0)
    return pl.pallas_call(
        functools.partial(_moe_kernel, layer=l),
        grid_spec=pltpu.PrefetchScalarGridSpec(
            num_scalar_prefetch=5,
            grid=(n_blocks // MOE_STEP_BLOCKS,),
            in_specs=[
                pl.BlockSpec((step_rows, D_MODEL // 2), row),
                pl.BlockSpec(memory_space=pl.ANY),
                pl.BlockSpec(memory_space=pl.ANY),
                pl.BlockSpec(memory_space=pl.ANY),
            ],
            out_specs=pl.BlockSpec((step_rows, D_MODEL // 2), row),
            scratch_shapes=[
                pltpu.VMEM((WEIGHT_SLOTS, D_MODEL, EXPERT_FF), F32),
                pltpu.VMEM((WEIGHT_SLOTS, D_MODEL, EXPERT_FF), F32),
                pltpu.VMEM((WEIGHT_SLOTS, EXPERT_FF, D_MODEL), F32),
                pltpu.VMEM((D_MODEL, EXPERT_FF), BF16),
                pltpu.VMEM((D_MODEL, EXPERT_FF), BF16),
                pltpu.VMEM((EXPERT_FF, D_MODEL), BF16),
                pltpu.SemaphoreType.DMA((WEIGHT_SLOTS, 3)),
            ],
        ),
        out_shape=jax.ShapeDtypeStruct((n_blocks * MOE_BM, D_MODEL // 2), jnp.int32),
        compiler_params=pltpu.CompilerParams(
            dimension_semantics=("arbitrary",), vmem_limit_bytes=VMEM_LIMIT),
        name="moe_experts",
    )(block_e, rows_valid, next_e, slot, first_e, xd, w_gate, w_up, w_down)


def _sc_worker_id():
    return lax.axis_index("s") * SC_CORES + lax.axis_index("c")


def _sc_dispatch(hp, hs, dest_p, dest_s, n_rows):
    tp, width = hp.shape
    per_w = tp // SC_WORKERS
    n_ch = per_w // DISP_CH
    n_sw = hs.shape[0] // SAMPLE_CH
    mesh = plsc.VectorSubcoreMesh(core_axis_name="c", subcore_axis_name="s")

    @functools.partial(
        pl.kernel, mesh=mesh,
        out_type=jax.ShapeDtypeStruct((n_rows, width), jnp.int32),
        scratch_types=[
            pltpu.VMEM((2, n_ch, DISP_CH), jnp.int32),
            pltpu.VMEM((2, 1, SAMPLE_CH), jnp.int32),
            pltpu.VMEM((SC_RING, DISP_CH, width), jnp.int32),
            pltpu.SemaphoreType.DMA((SC_RING,)),
            pltpu.SemaphoreType.DMA((SC_RING, 2)),
        ],
        name="sc_dispatch",
    )
    def k(hp_hbm, hs_hbm, dp_hbm, ds_hbm, xd_hbm, idx_v, idxs_v, bufs, rsem, wsem):
        wid = _sc_worker_id()
        base = wid * per_w
        for kk in range(2):
            pltpu.sync_copy(dp_hbm.at[kk, wid], idx_v.at[kk])
        reads = [pltpu.make_async_copy(hp_hbm.at[pl.ds(base + j * DISP_CH, DISP_CH)],
                                       bufs.at[j % SC_RING], rsem.at[j % SC_RING]) for j in range(n_ch)]
        writes = [[pltpu.make_async_copy(bufs.at[j % SC_RING], xd_hbm.at[idx_v.at[kk, j]],
                                         wsem.at[j % SC_RING, kk]) for kk in range(2)] for j in range(n_ch)]
        for j in range(min(SC_RING - 1, n_ch)):
            reads[j].start()
        for j in range(n_ch):
            reads[j].wait()
            for w in writes[j]:
                w.start()
            if j >= 1:
                for w in writes[j - 1]:
                    w.wait()
            if j + SC_RING - 1 < n_ch:
                reads[j + SC_RING - 1].start()
        for w in writes[n_ch - 1]:
            w.wait()

        @pl.when(wid < n_sw)
        def _():
            rows = bufs.at[0, pl.ds(0, SAMPLE_CH)]
            for kk in range(2):
                pltpu.sync_copy(ds_hbm.at[kk, wid], idxs_v.at[kk])
            pltpu.sync_copy(hs_hbm.at[pl.ds(wid * SAMPLE_CH, SAMPLE_CH)], rows)
            for kk in range(2):
                pltpu.sync_copy(rows, xd_hbm.at[idxs_v.at[kk, 0]])

    return k(hp, hs, dest_p, dest_s)


def _sc_sample_gather(yd, dest_s, ts):
    width = yd.shape[1]
    n_sw = ts // SAMPLE_CH
    mesh = plsc.VectorSubcoreMesh(core_axis_name="c", subcore_axis_name="s")

    @functools.partial(
        pl.kernel, mesh=mesh,
        out_type=jax.ShapeDtypeStruct((2, ts, width), yd.dtype),
        scratch_types=[
            pltpu.VMEM((2, 1, SAMPLE_CH), jnp.int32),
            pltpu.VMEM((2, SAMPLE_CH, width), yd.dtype),
        ],
        name="sc_sample_gather",
    )
    def k(yd_hbm, ds_hbm, g_hbm, idxs_v, bufs):
        wid = _sc_worker_id()

        @pl.when(wid < n_sw)
        def _():
            for kk in range(2):
                pltpu.sync_copy(ds_hbm.at[kk, wid], idxs_v.at[kk])
            for kk in range(2):
                pltpu.sync_copy(yd_hbm.at[idxs_v.at[kk, 0]], bufs.at[kk])
                pltpu.sync_copy(bufs.at[kk], g_hbm.at[kk, pl.ds(wid * SAMPLE_CH, SAMPLE_CH)])

    return k(yd, dest_s)


def _sc_combine_gather(yd, dest_p, tp):
    width = yd.shape[1]
    per_w = tp // SC_WORKERS
    n_ch = per_w // COMB_CH
    mesh = plsc.VectorSubcoreMesh(core_axis_name="c", subcore_axis_name="s")

    @functools.partial(
        pl.kernel, mesh=mesh,
        out_type=jax.ShapeDtypeStruct((2, tp, width), yd.dtype),
        scratch_types=[
            pltpu.VMEM((2, n_ch, COMB_CH), jnp.int32),
            pltpu.VMEM((SC_RING, COMB_CH, width), yd.dtype),
            pltpu.SemaphoreType.DMA((SC_RING,)),
            pltpu.SemaphoreType.DMA((SC_RING,)),
        ],
        name="sc_combine_gather",
    )
    def k(yd_hbm, dp_hbm, g_hbm, idx_v, bufs, gsem, wsem):
        wid = _sc_worker_id()
        base = wid * per_w
        for kk in range(2):
            pltpu.sync_copy(dp_hbm.at[kk, wid], idx_v.at[kk])
        items = [(kk, j) for kk in range(2) for j in range(n_ch)]
        n_items = len(items)
        gathers = [pltpu.make_async_copy(yd_hbm.at[idx_v.at[kk, j]], bufs.at[n % SC_RING], gsem.at[n % SC_RING])
                   for n, (kk, j) in enumerate(items)]
        outs = [pltpu.make_async_copy(bufs.at[n % SC_RING], g_hbm.at[kk, pl.ds(base + j * COMB_CH, COMB_CH)],
                                      wsem.at[n % SC_RING]) for n, (kk, j) in enumerate(items)]
        for n in range(min(SC_RING - 1, n_items)):
            gathers[n].start()
        for n in range(n_items):
            gathers[n].wait()
            outs[n].start()
            if n >= 1:
                outs[n - 1].wait()
            if n + SC_RING - 1 < n_items:
                gathers[n + SC_RING - 1].start()
        outs[n_items - 1].wait()

    return k(yd, dest_p)


def _combine_kernel(x1_ref, g_ref, route_t_ref, x2_ref):
    fields = route_t_ref[...]
    tm = fields.shape[1]
    cols = jnp.transpose(jnp.concatenate([fields, jnp.zeros((LANES - ROUTE_FIELDS, tm), F32)], axis=0))
    w1 = cols[:, 2:3]
    w2 = cols[:, 3:4]
    x2_ref[...] = x1_ref[...] + _unpack_bf16_pairs(g_ref[0]) * w1 + _unpack_bf16_pairs(g_ref[1]) * w2


def _combine(x1, g, route_t, row0, tm):
    t = x1.shape[0]
    blk0 = row0 // tm
    row = lambda i: (i, 0)
    return pl.pallas_call(
        _combine_kernel,
        grid=(t // tm,),
        in_specs=[
            pl.BlockSpec((tm, D_MODEL), row),
            pl.BlockSpec((2, tm, D_MODEL // 2), lambda i: (0, blk0 + i, 0)),
            pl.BlockSpec((ROUTE_FIELDS, tm), lambda i: (0, i)),
        ],
        out_specs=pl.BlockSpec((tm, D_MODEL), row),
        out_shape=jax.ShapeDtypeStruct((t, D_MODEL), F32),
        compiler_params=pltpu.CompilerParams(
            dimension_semantics=("arbitrary",), vmem_limit_bytes=VMEM_LIMIT),
        name="combine",
    )(x1, g, route_t)


def _dest_layout(dest, workers, chunk):
    t = dest.shape[1]
    return dest.reshape(2, workers, t // (workers * chunk), chunk)


def _hier_moe(l, h2p, h2s, route_tp, route_ts, counts, w_gate, w_up, w_down):
    tp, ts = h2p.shape[0], h2s.shape[0]
    n_assign = 2 * (tp + ts)
    n_blocks = -(-n_assign // MOE_BM) + N_EXPERTS
    n_blocks = -(-n_blocks // MOE_STEP_BLOCKS) * MOE_STEP_BLOCKS
    pcounts = (counts + MOE_BM - 1) // MOE_BM * MOE_BM
    pends = jnp.cumsum(pcounts)
    poffsets = pends - pcounts
    starts = jnp.arange(n_blocks, dtype=jnp.int32) * MOE_BM
    block_e = jnp.minimum(jnp.sum((pends[None, :] <= starts[:, None]).astype(jnp.int32), axis=1),
                          N_EXPERTS - 1)
    experts = jnp.arange(N_EXPERTS, dtype=jnp.int32)

    def lookup(table, idx):
        return jnp.sum(jnp.where(idx[..., None] == experts, table, 0), axis=-1)

    rows_valid = jnp.clip(lookup(poffsets + counts, block_e) - starts, 0, MOE_BM).astype(jnp.int32)
    used = counts > 0
    last_e = jnp.max(jnp.where(used, jnp.arange(N_EXPERTS, dtype=jnp.int32), 0))
    block_e = jnp.where(rows_valid > 0, block_e, last_e).astype(jnp.int32)
    place = jnp.cumsum(used.astype(jnp.int32)) - 1
    by_place = jnp.sum(jnp.where(used[None, :] & (place[None, :] == experts[:, None]), experts[None, :], 0),
                       axis=1)
    n_used = jnp.sum(used.astype(jnp.int32))

    def at_place(p):
        return jnp.where(p < n_used, lookup(by_place, jnp.minimum(p, N_EXPERTS - 1)), -1).astype(jnp.int32)

    ahead_of = at_place(place + (WEIGHT_SLOTS - 1))
    next_e = lookup(ahead_of, block_e)
    slot = lookup(place % WEIGHT_SLOTS, block_e)
    n_steps_used = -(-(pends[-1] // MOE_BM) // MOE_STEP_BLOCKS)
    first_e = jnp.concatenate([at_place(jnp.arange(WEIGHT_SLOTS - 1, dtype=jnp.int32)),
                               n_steps_used.reshape(1).astype(jnp.int32)])

    def dest_of(route_t):
        return lookup(poffsets, route_t[0:2].astype(jnp.int32)) + route_t[4:6].astype(jnp.int32)

    dest_p, dest_s = dest_of(route_tp), dest_of(route_ts)
    n_sw = ts // SAMPLE_CH
    xd = _sc_dispatch(h2p, h2s, _dest_layout(dest_p, SC_WORKERS, DISP_CH),
                      _dest_layout(dest_s, n_sw, SAMPLE_CH), n_blocks * MOE_BM)
    yd = _moe_experts(l, block_e, rows_valid, next_e, slot, first_e, xd, w_gate, w_up, w_down)
    g_s = _sc_sample_gather(yd, _dest_layout(dest_s, n_sw, SAMPLE_CH), ts)
    g_p = _sc_combine_gather(yd, _dest_layout(dest_p, SC_WORKERS, COMB_CH), tp)
    return g_p, g_s


def kernel(x_prompt, x_sample, state_pool, cache_k_win, cache_v_win, norm_attn_g, w_in, pool_w, pool_scale, q_norm_g, k_norm_g, attn_sinks, w_out, norm_ffn_g, router_group_w, router_group_b, router_expert_w, router_expert_b, w_gate, w_up, w_down):
    n_p, t_p, d = x_prompt.shape
    n_s, t_s, _ = x_sample.shape
    depth = w_in.shape[0]
    lw_s = cache_k_win.shape[2]
    assert t_s == 1 and lw_s == WINDOW and d == D_MODEL
    assert t_p % TM_PROJ == 0 and t_p >= WINDOW

    seg = jnp.arange(256) // HEAD_DIM
    bd = jnp.where(seg[:, None] == seg[None, :], 1.0 / HEAD_DIM, 0.0).astype(BF16)
    slopes = jnp.exp2(-8.0 * jnp.arange(1, N_HEADS + 1, dtype=F32) / N_HEADS)
    bias_p = _prompt_bias_t()
    dist_s = (WINDOW - 1) - jnp.arange(WINDOW, dtype=F32)
    bias_s = -LOG2E * slopes[:, None] * dist_s[None, :]

    wp = jnp.zeros((depth, 2, 256, 256), F32)
    for p in range(2):
        wp = wp.at[:, p, :POOL_GC, :POOL_GC].set(pool_w[:, 2 * p])
        wp = wp.at[:, p, POOL_GC:, POOL_GC:].set(pool_w[:, 2 * p + 1])
    assert GROUP_LANE0 == N_EXPERTS
    lane_pad = LANES - N_EXPERTS - N_EXPERT_GROUPS
    wr = jnp.concatenate([router_expert_w, router_group_w, jnp.zeros((depth, D_MODEL, lane_pad), F32)], axis=-1)
    br = jnp.concatenate([router_expert_b, router_group_b, jnp.zeros((depth, lane_pad), F32)],
                         axis=-1).reshape(depth, 1, LANES)
    lp = dict(
        w_in=w_in.astype(BF16),
        w_out=w_out.astype(BF16),
        g_attn=norm_attn_g.reshape(depth, 1, D_MODEL),
        g_ffn=norm_ffn_g.reshape(depth, 1, D_MODEL),
        qg=(jnp.tile(q_norm_g, (1, N_HEADS)) * (ATTN_SCALE * LOG2E)).reshape(depth, 1, Q_W),
        kg=jnp.tile(k_norm_g, (1, N_KV_HEADS)).reshape(depth, 1, KV_W),
        wp=wp.astype(BF16),
        ps=pool_scale.reshape(depth, 1, POOL_W),
        wr=wr.astype(BF16),
        br=br,
        state=state_pool,
        ck=cache_k_win.reshape(depth, n_s, lw_s, KV_W),
        cv=cache_v_win.reshape(depth, n_s, lw_s, KV_W),
    )

    xp = x_prompt.reshape(n_p * t_p, D_MODEL)
    xs = x_sample.reshape(n_s, D_MODEL)
    lw_p = min(WINDOW, t_p)
    pool_p, kp_new, vp_new = [], [], []
    sample_state = [lp["state"], lp["ck"], lp["cv"]]
    zero_cnt = jnp.zeros((N_EXPERTS, 1), F32)
    pending = None
    for l in range(depth):
        sinks = attn_sinks[l] * LOG2E
        pool_so, attn_so, *sample_state = _sample_mixer(
            l, depth, xs, lp["g_attn"], lp["w_in"], lp["qg"], lp["kg"], bd, lp["wp"], lp["ps"],
            *sample_state, sinks.reshape(N_HEADS, 1), bias_s, PAST_LEN)
        x1s, h2s, route_ts, cnt_s = _merge_router(
            l, pool_so, attn_so, xs, lp["w_out"], lp["g_ffn"], lp["wr"], lp["br"], zero_cnt, n_s)
        outs = _proj_pool_prompt(
            l, xp if pending is None else pending, n_p, t_p,
            lp["g_attn"], lp["w_in"], lp["qg"], lp["kg"], bd, lp["wp"], lp["ps"])
        if pending is not None:
            xp, outs = outs[0], outs[1:]
        pool_o, q, k, vt, utail, ktail, vtail = outs
        attn_o = _attn_prompt(q, k, vt, bias_p, sinks, n_p, t_p)
        x1p, h2p, route_tp, cnt_all = _merge_router(
            l, pool_o, attn_o, xp, lp["w_out"], lp["g_ffn"], lp["wr"], lp["br"], cnt_s, TM_MERGE)
        pool_p.append(utail[:, 16 - POOL_STATE:, :])
        kp_new.append(ktail)
        vp_new.append(vtail)
        counts = cnt_all[:, 0].astype(jnp.int32)
        g_p, g_s = _hier_moe(l, h2p, h2s, route_tp, route_ts, counts, w_gate, w_up, w_down)
        xs = _combine(x1s, g_s, route_ts, 0, n_s)
        pending = (x1p, g_p, route_tp)
    xp = _combine(*pending, 0, TM_MERGE)
    return (xp.reshape(n_p, t_p, D_MODEL), xs.reshape(n_s, t_s, D_MODEL),
            jnp.stack(pool_p),
            jnp.stack(kp_new).reshape(depth, n_p, lw_p, N_KV_HEADS, HEAD_DIM),
            jnp.stack(vp_new).reshape(depth, n_p, lw_p, N_KV_HEADS, HEAD_DIM),
            sample_state[0],
            sample_state[1].reshape(depth, n_s, lw_s, N_KV_HEADS, HEAD_DIM),
            sample_state[2].reshape(depth, n_s, lw_s, N_KV_HEADS, HEAD_DIM))
```

```python
import functools

import jax
import jax.numpy as jnp
from jax import lax
from jax.experimental import pallas as pl
from jax.experimental.pallas import tpu as pltpu
from jax.experimental.pallas import tpu_sc as plsc

D_MODEL = 1024
POOL_W = 512
POOL_WINDOWS = (2, 4, 8, 16)
POOL_GC = 128
POOL_STATE = 15
HEAD_DIM = 64
N_HEADS = 8
N_KV_HEADS = 2
GQA_GROUP = 4
Q_W = 512
KV_W = 128
D_IN = POOL_W + Q_W + 2 * KV_W
WINDOW = 128
ATTN_SCALE = HEAD_DIM ** -0.5
LOG2E = 1.4426950408889634
N_EXPERT_GROUPS = 4
EXPERTS_PER_GROUP = 8
N_EXPERTS = 32
EXPERT_FF = 512
EPS = 1e-6
PAST_LEN = 16384

LANES = 128
HALO = 32
TM_PROJ = 1024
TM_MERGE = 1024
MERGE_CHUNKS = 4
ATTN_QB = 16
MOE_BM = 256
MOE_STEP_BLOCKS = 4
WEIGHT_SLOTS = 3
GROUP_LANE0 = 32
ROUTE_FIELDS = 8
SC_CORES = 2
SC_SUBCORES = 16
SC_WORKERS = SC_CORES * SC_SUBCORES
DISP_CH = 64
COMB_CH = 64
SAMPLE_CH = 32
SC_RING = 3
VMEM_LIMIT = 48 * 1024 * 1024

BF16 = jnp.bfloat16
F32 = jnp.float32


def _pack_bf16_pairs(h):
    w = h.shape[1] // 2
    hi = lax.bitcast_convert_type(h[:, :w].astype(F32), jnp.uint32)
    lo = lax.bitcast_convert_type(h[:, w:].astype(F32), jnp.uint32)
    return lax.bitcast_convert_type(hi | (lo >> 16), jnp.int32)


def _unpack_bf16_pairs(words):
    u = lax.bitcast_convert_type(words, jnp.uint32)
    hi = lax.bitcast_convert_type(u & jnp.uint32(0xFFFF0000), F32)
    lo = lax.bitcast_convert_type(u << 16, F32)
    return jnp.concatenate([hi, lo], axis=-1)


def _segment_mean_sq(a, bd):
    w = a.shape[1]
    return jnp.dot((a * a).astype(BF16), bd[:w, :w], preferred_element_type=F32)


def _rms_bf16(x, g):
    ms = jnp.mean(x * x, axis=-1, keepdims=True)
    return (x * lax.rsqrt(ms + EPS) * g).astype(BF16)


def _qk_norm(q, k, qg, kg, bd):
    qn = []
    for c in range(Q_W // 256):
        qc = q[:, c * 256:(c + 1) * 256]
        qn.append(qc * lax.rsqrt(_segment_mean_sq(qc, bd) + EPS))
    qn = jnp.concatenate(qn, axis=-1) * qg
    kn = k * lax.rsqrt(_segment_mean_sq(k, bd) + EPS) * kg
    return qn, kn


def _project(x, g, w_in, qg, kg, bd):
    z = jnp.dot(_rms_bf16(x, g), w_in, preferred_element_type=F32)
    u = z[:, :POOL_W]
    q = z[:, POOL_W:POOL_W + Q_W]
    k = z[:, POOL_W + Q_W:POOL_W + Q_W + KV_W]
    v = z[:, POOL_W + Q_W + KV_W:]
    qn, kn = _qk_norm(q, k, qg, kg, bd)
    return u, qn, kn, v


def _pool_project(d_groups, wp_ref, ps):
    outs = []
    for p in range(2):
        dp = jnp.concatenate([d_groups[2 * p], d_groups[2 * p + 1]], axis=-1).astype(BF16)
        y = jnp.dot(dp, wp_ref[p], preferred_element_type=F32)
        outs.append(y * ps[:, p * 256:(p + 1) * 256])
    return jnp.concatenate(outs, axis=-1)


def _proj_pool_kernel(x_ref, g_ref, win_ref, qg_ref, kg_ref, bd_ref, wp_ref, ps_ref,
                      pool_ref, q_ref, k_ref, vt_ref, utail_ref, ktail_ref, vtail_ref,
                      ext_ref, sa_ref, sb_ref, zq_ref, *, tm, n_j):
    j = pl.program_id(1)

    @pl.when(j == 0)
    def _():
        ext_ref[0:HALO, :] = jnp.zeros((HALO, POOL_W), F32)

    r = tm + HALO
    h = _rms_bf16(x_ref[...], g_ref[...])
    ext_ref[HALO:r, :] = jnp.dot(h, win_ref[:, 0:POOL_W], preferred_element_type=F32)
    zq_ref[...] = jnp.dot(h, win_ref[:, POOL_W:], preferred_element_type=F32)
    u = ext_ref[HALO:r, :]
    sa_ref[8:r, :] = ext_ref[8:r, :] + ext_ref[7:r - 1, :]
    sb_ref[16:r, 128:] = sa_ref[16:r, 128:] + sa_ref[14:r - 2, 128:]
    sa_ref[24:r, 256:] = sb_ref[24:r, 256:] + sb_ref[20:r - 4, 256:]
    sb_ref[32:r, 384:] = sa_ref[32:r, 384:] + sa_ref[24:r - 8, 384:]
    pos1 = j * tm + lax.broadcasted_iota(jnp.int32, (tm, POOL_GC), 0) + 1
    sums = (sa_ref, sb_ref, sa_ref, sb_ref)
    d_groups = []
    for gi, w in enumerate(POOL_WINDOWS):
        sl = slice(gi * POOL_GC, (gi + 1) * POOL_GC)
        cnt = jnp.minimum(pos1, w).astype(F32)
        d_groups.append(sums[gi][HALO:r, sl] / cnt - u[:, sl])
    pool_ref[...] = _pool_project(d_groups, wp_ref, ps_ref[...]).astype(BF16)
    ext_ref[16:HALO, :] = ext_ref[tm + 16:r, :]

    qn, kn = _qk_norm(zq_ref[:, 0:Q_W], zq_ref[:, Q_W:Q_W + KV_W], qg_ref[...], kg_ref[...], bd_ref[...])
    v = zq_ref[:, Q_W + KV_W:]
    q_ref[...] = qn.astype(BF16)
    k_ref[...] = kn.astype(BF16)
    vt_ref[...] = jnp.transpose(v).astype(BF16)

    @pl.when(j == n_j - 1)
    def _():
        utail_ref[...] = u[tm - 16:, :]
        ktail_ref[...] = kn[tm - WINDOW:, :]
        vtail_ref[...] = v[tm - WINDOW:, :]


def _proj_pool_combine_kernel(x1_ref, gath_ref, route_ref, *rest, tm, n_j):
    x2_ref = rest[7]
    _combine_kernel(x1_ref, gath_ref, route_ref, x2_ref)
    _proj_pool_kernel(x2_ref, *rest[:7], *rest[8:], tm=tm, n_j=n_j)


def _proj_pool_prompt(l, x_in, n_seq, seq, g_attn, w_in, qg, kg, bd, wp, ps):
    tm = TM_PROJ
    n_j = seq // tm
    t = n_seq * seq
    row = lambda b, j: (b * n_j + j, 0)
    lay = lambda b, j: (l, 0, 0)
    fused = isinstance(x_in, tuple)
    if fused:
        kern = _proj_pool_combine_kernel
        x_args = list(x_in)
        x_specs = [pl.BlockSpec((tm, D_MODEL), row),
                   pl.BlockSpec((2, tm, D_MODEL // 2), lambda b, j: (0, b * n_j + j, 0)),
                   pl.BlockSpec((ROUTE_FIELDS, tm), lambda b, j: (0, b * n_j + j))]
        x_out_specs = [pl.BlockSpec((tm, D_MODEL), row)]
        x_out_shape = [jax.ShapeDtypeStruct((t, D_MODEL), F32)]
    else:
        kern = _proj_pool_kernel
        x_args = [x_in]
        x_specs = [pl.BlockSpec((tm, D_MODEL), row)]
        x_out_specs, x_out_shape = [], []
    return pl.pallas_call(
        functools.partial(kern, tm=tm, n_j=n_j),
        grid=(n_seq, n_j),
        in_specs=x_specs + [
            pl.BlockSpec((None, 1, D_MODEL), lay),
            pl.BlockSpec((None, D_MODEL, D_IN), lay),
            pl.BlockSpec((None, 1, Q_W), lay),
            pl.BlockSpec((None, 1, KV_W), lay),
            pl.BlockSpec((256, 256), lambda b, j: (0, 0)),
            pl.BlockSpec((None, 2, 256, 256), lambda b, j: (l, 0, 0, 0)),
            pl.BlockSpec((None, 1, POOL_W), lay),
        ],
        out_specs=x_out_specs + [
            pl.BlockSpec((tm, POOL_W), row),
            pl.BlockSpec((tm, Q_W), row),
            pl.BlockSpec((tm, KV_W), row),
            pl.BlockSpec((KV_W, tm), lambda b, j: (0, b * n_j + j)),
            pl.BlockSpec((None, 16, POOL_W), lambda b, j: (b, 0, 0)),
            pl.BlockSpec((None, WINDOW, KV_W), lambda b, j: (b, 0, 0)),
            pl.BlockSpec((None, WINDOW, KV_W), lambda b, j: (b, 0, 0)),
        ],
        out_shape=x_out_shape + [
            jax.ShapeDtypeStruct((t, POOL_W), BF16),
            jax.ShapeDtypeStruct((t, Q_W), BF16),
            jax.ShapeDtypeStruct((t, KV_W), BF16),
            jax.ShapeDtypeStruct((KV_W, t), BF16),
            jax.ShapeDtypeStruct((n_seq, 16, POOL_W), F32),
            jax.ShapeDtypeStruct((n_seq, WINDOW, KV_W), F32),
            jax.ShapeDtypeStruct((n_seq, WINDOW, KV_W), F32),
        ],
        scratch_shapes=[pltpu.VMEM((tm + HALO, POOL_W), F32)] * 3 + [pltpu.VMEM((tm, Q_W + 2 * KV_W), F32)],
        compiler_params=pltpu.CompilerParams(
            dimension_semantics=("arbitrary", "arbitrary"), vmem_limit_bytes=VMEM_LIMIT),
        name="proj_pool_prompt",
    )(*x_args, g_attn, w_in, qg, kg, bd, wp, ps)


def _attn_kernel(sink_ref, q_ref, kp_ref, kc_ref, vtp_ref, vtc_ref, bias_ref, o_ref, s_ref):
    j = pl.program_id(1)
    kk_all = jnp.concatenate([kp_ref[...], kc_ref[...]], axis=0)
    vt_all = jnp.concatenate([vtp_ref[...], vtc_ref[...]], axis=1)
    from_prev = (lax.broadcasted_iota(jnp.int32, (WINDOW, WINDOW), 0)
                 > lax.broadcasted_iota(jnp.int32, (WINDOW, WINDOW), 1))
    units = [(blk, kv) for blk in range(ATTN_QB) for kv in range(N_KV_HEADS)]

    def scores(n):
        blk, kv = units[n]
        q = q_ref[blk * WINDOW:(blk + 1) * WINDOW, :]
        kk = kk_all[blk * WINDOW:(blk + 2) * WINDOW, kv * HEAD_DIM:(kv + 1) * HEAD_DIM]
        heads = range(kv * GQA_GROUP, (kv + 1) * GQA_GROUP)
        q_rows = jnp.concatenate([q[:, h * HEAD_DIM:(h + 1) * HEAD_DIM] for h in heads], axis=0)
        s_ref[n % 3] = lax.dot_general(kk, q_rows, (((1,), (1,)), ((), ())), preferred_element_type=F32)

    scores(0)
    scores(1)
    outs = []
    for n, (blk, kv) in enumerate(units):
        if n + 2 < len(units):
            scores(n + 2)
        vt_kv = vt_all[kv * HEAD_DIM:(kv + 1) * HEAD_DIM, blk * WINDOW:(blk + 2) * WINDOW]
        variant = jnp.minimum(j, 1) if blk == 0 else 1
        for g in range(GQA_GROUP):
            h = kv * GQA_GROUP + g
            s = jnp.where(from_prev, s_ref[n % 3, 0:WINDOW, g * WINDOW:(g + 1) * WINDOW],
                          s_ref[n % 3, WINDOW:, g * WINDOW:(g + 1) * WINDOW]) + bias_ref[variant, h]
            sink = sink_ref[h]
            m = jnp.maximum(jnp.max(s, axis=0, keepdims=True), sink)
            p = jnp.exp2(s - m)
            denom = jnp.sum(p, axis=0, keepdims=True) + jnp.exp2(sink - m)
            p_keys = jnp.concatenate([jnp.where(from_prev, p, 0.0), jnp.where(from_prev, 0.0, p)], axis=0)
            o_t = jnp.dot(vt_kv, p_keys.astype(BF16), preferred_element_type=F32)
            outs.append(o_t / denom)
        if kv == N_KV_HEADS - 1:
            o_ref[blk * WINDOW:(blk + 1) * WINDOW, :] = jnp.transpose(jnp.concatenate(outs, axis=0)).astype(BF16)
            outs = []


def _attn_prompt(q, k, vt, bias_t, sinks, n_seq, seq):
    tq = ATTN_QB * WINDOW
    nj = seq // tq
    t = n_seq * seq
    cur = lambda b, j: (b * nj + j, 0)
    prev = lambda b, j: (jnp.maximum((b * nj + j) * ATTN_QB - 1, 0), 0)
    cur_t = lambda b, j: (0, b * nj + j)
    prev_t = lambda b, j: (0, jnp.maximum((b * nj + j) * ATTN_QB - 1, 0))
    return pl.pallas_call(
        _attn_kernel,
        grid=(n_seq, nj),
        in_specs=[
            pl.BlockSpec(memory_space=pltpu.SMEM),
            pl.BlockSpec((tq, Q_W), cur),
            pl.BlockSpec((WINDOW, KV_W), prev),
            pl.BlockSpec((tq, KV_W), cur),
            pl.BlockSpec((KV_W, WINDOW), prev_t),
            pl.BlockSpec((KV_W, tq), cur_t),
            pl.BlockSpec((2, N_HEADS, WINDOW, WINDOW), lambda b, j: (0, 0, 0, 0)),
        ],
        out_specs=pl.BlockSpec((tq, Q_W), cur),
        out_shape=jax.ShapeDtypeStruct((t, Q_W), BF16),
        scratch_shapes=[pltpu.VMEM((3, 2 * WINDOW, GQA_GROUP * WINDOW), F32)],
        compiler_params=pltpu.CompilerParams(
            dimension_semantics=("arbitrary", "arbitrary"), vmem_limit_bytes=VMEM_LIMIT),
        name="attn_prompt",
    )(sinks, q, k, k, vt, vt, bias_t)


def _prompt_bias_t():
    r = jnp.arange(WINDOW, dtype=jnp.int32)[None, :]
    c = jnp.arange(WINDOW, dtype=jnp.int32)[:, None]
    from_prev = c > r
    dist = r - c + jnp.where(from_prev, WINDOW, 0)
    slopes = jnp.exp2(-8.0 * jnp.arange(1, N_HEADS + 1, dtype=F32) / N_HEADS)
    later = -LOG2E * slopes[:, None, None] * dist.astype(F32)[None]
    first = jnp.where(from_prev[None], -jnp.inf, later)
    return jnp.stack([first, later])


def _sample_kernel(x_ref, g_ref, win_ref, qg_ref, kg_ref, bd_ref, wp_ref, ps_ref,
                   st_ref, ck_ref, cv_ref, sink_ref, bias_ref, perm_ref,
                   pool_ref, attn_ref, pst_ref, kc_ref, vc_ref, *, ns, pos0):
    u, qn, kn, v = _project(x_ref[...], g_ref[...], win_ref[...], qg_ref[...], kg_ref[...], bd_ref[...])
    pst_ref[:, 0:POOL_STATE - 1, :] = st_ref[:, 1:POOL_STATE, :]
    kc_ref[:, 0:WINDOW - 1, :] = ck_ref[:, 1:WINDOW, :]
    vc_ref[:, 0:WINDOW - 1, :] = cv_ref[:, 1:WINDOW, :]
    for n in range(ns):
        pst_ref[n, POOL_STATE - 1:POOL_STATE, :] = u[n:n + 1, :]
        kc_ref[n, WINDOW - 1:WINDOW, :] = kn[n:n + 1, :]
        vc_ref[n, WINDOW - 1:WINDOW, :] = v[n:n + 1, :]

    d_groups = []
    for gi, w in enumerate(POOL_WINDOWS):
        lo = gi * POOL_GC
        acc = u[:, lo:lo + POOL_GC]
        for back in range(1, w):
            acc = acc + st_ref[:, POOL_STATE - back, lo:lo + POOL_GC]
        d_groups.append(acc / float(min(pos0 + 1, w)) - u[:, lo:lo + POOL_GC])
    pool_ref[...] = _pool_project(d_groups, wp_ref, ps_ref[...]).astype(BF16)

    zeros = jnp.zeros((ns, HEAD_DIM), F32)
    stacked = []
    for h in range(N_HEADS):
        piece = qn[:, h * HEAD_DIM:(h + 1) * HEAD_DIM]
        pair = [piece, zeros] if h < GQA_GROUP else [zeros, piece]
        stacked.append(jnp.concatenate(pair, axis=-1))
    q_hn = jnp.concatenate(stacked, axis=0).astype(BF16)
    q_nh = jnp.dot(perm_ref[0], q_hn, preferred_element_type=F32).astype(BF16)

    keys = kc_ref[...].reshape(ns * WINDOW, KV_W).astype(BF16)
    vals = vc_ref[...].reshape(ns * WINDOW, KV_W).astype(BF16)
    s_all = lax.dot_general(q_nh, keys, (((1,), (1,)), ((), ())), preferred_element_type=F32)
    sink = sink_ref[...]
    bias = bias_ref[...]
    zero_blk = jnp.zeros((N_HEADS, WINDOW), F32)
    p_rows = []
    for n in range(ns):
        s = s_all[n * N_HEADS:(n + 1) * N_HEADS, n * WINDOW:(n + 1) * WINDOW] + bias
        m = jnp.maximum(jnp.max(s, axis=-1, keepdims=True), sink)
        p = jnp.exp2(s - m)
        denom = jnp.sum(p, axis=-1, keepdims=True) + jnp.exp2(sink - m)
        p_rows.append(jnp.concatenate([zero_blk] * n + [p / denom] + [zero_blk] * (ns - 1 - n), axis=-1))
    p_blockdiag = jnp.concatenate(p_rows, axis=0).astype(BF16)
    o_nh = jnp.dot(p_blockdiag, vals, preferred_element_type=F32).astype(BF16)
    o_hn = jnp.dot(perm_ref[1], o_nh, preferred_element_type=F32)
    pieces = []
    for h in range(N_HEADS):
        kv = h // GQA_GROUP
        pieces.append(o_hn[h * ns:(h + 1) * ns, kv * HEAD_DIM:(kv + 1) * HEAD_DIM])
    attn_ref[...] = jnp.concatenate(pieces, axis=-1).astype(BF16)


def _sample_mixer(l, depth, xs, g_attn, w_in, qg, kg, bd, wp, ps, state, ck, cv, sink8, bias_s, pos0):
    n = xs.shape[0]
    ns = 32
    row = lambda i: (i, 0)
    lay = lambda i: (l, 0, 0)
    src = jnp.arange(ns * N_HEADS)
    perm = (((src % N_HEADS) * ns + src // N_HEADS)[:, None] == src[None, :]).astype(BF16)
    perms = jnp.stack([perm, perm.T])
    return pl.pallas_call(
        functools.partial(_sample_kernel, ns=ns, pos0=pos0),
        grid=(n // ns,),
        input_output_aliases={8: 2, 9: 3, 10: 4},
        in_specs=[
            pl.BlockSpec((ns, D_MODEL), row),
            pl.BlockSpec((None, 1, D_MODEL), lay),
            pl.BlockSpec((None, D_MODEL, D_IN), lay),
            pl.BlockSpec((None, 1, Q_W), lay),
            pl.BlockSpec((None, 1, KV_W), lay),
            pl.BlockSpec((256, 256), lambda i: (0, 0)),
            pl.BlockSpec((None, 2, 256, 256), lambda i: (l, 0, 0, 0)),
            pl.BlockSpec((None, 1, POOL_W), lay),
            pl.BlockSpec((None, ns, POOL_STATE, POOL_W), lambda i: (l, i, 0, 0)),
            pl.BlockSpec((None, ns, WINDOW, KV_W), lambda i: (l, i, 0, 0)),
            pl.BlockSpec((None, ns, WINDOW, KV_W), lambda i: (l, i, 0, 0)),
            pl.BlockSpec((N_HEADS, 1), lambda i: (0, 0)),
            pl.BlockSpec((N_HEADS, WINDOW), lambda i: (0, 0)),
            pl.BlockSpec((2, ns * N_HEADS, ns * N_HEADS), lambda i: (0, 0, 0)),
        ],
        out_specs=[
            pl.BlockSpec((ns, POOL_W), row),
            pl.BlockSpec((ns, Q_W), row),
            pl.BlockSpec((None, ns, POOL_STATE, POOL_W), lambda i: (l, i, 0, 0)),
            pl.BlockSpec((None, ns, WINDOW, KV_W), lambda i: (l, i, 0, 0)),
            pl.BlockSpec((None, ns, WINDOW, KV_W), lambda i: (l, i, 0, 0)),
        ],
        out_shape=[
            jax.ShapeDtypeStruct((n, POOL_W), BF16),
            jax.ShapeDtypeStruct((n, Q_W), BF16),
            jax.ShapeDtypeStruct((depth, n, POOL_STATE, POOL_W), F32),
            jax.ShapeDtypeStruct((depth, n, WINDOW, KV_W), F32),
            jax.ShapeDtypeStruct((depth, n, WINDOW, KV_W), F32),
        ],
        compiler_params=pltpu.CompilerParams(
            dimension_semantics=("arbitrary",), vmem_limit_bytes=VMEM_LIMIT),
        name="sample_mixer",
    )(xs, g_attn, w_in, qg, kg, bd, wp, ps, state, ck, cv, sink8, bias_s, perms)


def _merge_router_kernel(pool_ref, attn_ref, x_ref, wout_ref, g_ref, wr_ref, br_ref, utri_ref, cin_ref,
                         x1_ref, h2_ref, route_t_ref, cnt_ref, y_ref, lg_ref):
    i = pl.program_id(0)

    @pl.when(i == 0)
    def _():
        cnt_ref[...] = cin_ref[...]

    tm = x_ref.shape[0]
    rc = tm // MERGE_CHUNKS
    chunks = [slice(ci * rc, (ci + 1) * rc) for ci in range(MERGE_CHUNKS)]
    for rows in chunks:
        y_ref[rows, :] = (jnp.dot(pool_ref[rows, :], wout_ref[0:POOL_W, :], preferred_element_type=F32)
                          + jnp.dot(attn_ref[rows, :], wout_ref[POOL_W:, :], preferred_element_type=F32))
    for rows in chunks:
        x1 = x_ref[rows, :] + y_ref[rows, :]
        x1_ref[rows, :] = x1
        h2 = _rms_bf16(x1, g_ref[...])
        h2_ref[rows, :] = _pack_bf16_pairs(h2)
        lg_ref[rows, :] = jnp.dot(h2, wr_ref[...], preferred_element_type=F32) + br_ref[...]
    logits = lg_ref[...]

    lt = jnp.transpose(logits)
    sub = lax.broadcasted_iota(jnp.int32, (EXPERTS_PER_GROUP, tm), 0)
    neg = -jnp.inf
    big = jnp.int32(EXPERTS_PER_GROUP)
    gl = jnp.where(sub < N_EXPERT_GROUPS, lt[GROUP_LANE0:GROUP_LANE0 + EXPERTS_PER_GROUP, :], neg)
    gmax = jnp.max(gl, axis=0, keepdims=True)
    grp = jnp.min(jnp.where(gl == gmax, sub, big), axis=0, keepdims=True)
    g_w = 1.0 / jnp.sum(jnp.exp(gl - gmax), axis=0, keepdims=True)
    el = lt[(N_EXPERT_GROUPS - 1) * EXPERTS_PER_GROUP:N_EXPERT_GROUPS * EXPERTS_PER_GROUP, :]
    for gi in range(N_EXPERT_GROUPS - 2, -1, -1):
        el = jnp.where(grp == gi, lt[gi * EXPERTS_PER_GROUP:(gi + 1) * EXPERTS_PER_GROUP, :], el)
    v1 = jnp.max(el, axis=0, keepdims=True)
    i1 = jnp.min(jnp.where(el == v1, sub, big), axis=0, keepdims=True)
    el2 = jnp.where(sub == i1, neg, el)
    v2 = jnp.max(el2, axis=0, keepdims=True)
    i2 = jnp.min(jnp.where(el2 == v2, sub, big), axis=0, keepdims=True)
    e21 = jnp.exp(v2 - v1)
    w1 = g_w / (1.0 + e21)
    w2 = g_w * e21 / (1.0 + e21)
    e1 = grp * EXPERTS_PER_GROUP + i1
    e2 = grp * EXPERTS_PER_GROUP + i2

    esub = lax.broadcasted_iota(jnp.int32, (N_EXPERTS, tm), 0)
    oh1 = esub == e1
    oh2 = esub == e2
    c = jnp.where(oh1 | oh2, 1.0, 0.0)
    prefix = jnp.dot(c.astype(BF16), utri_ref[...], preferred_element_type=F32) + cnt_ref[...]
    r1 = jnp.sum(jnp.where(oh1, prefix, 0.0), axis=0, keepdims=True)
    r2 = jnp.sum(jnp.where(oh2, prefix, 0.0), axis=0, keepdims=True)
    cnt_ref[...] = cnt_ref[...] + jnp.sum(c, axis=1, keepdims=True)

    fields = jnp.zeros((ROUTE_FIELDS, tm), F32)
    for idx, val in enumerate((e1.astype(F32), e2.astype(F32), w1, w2, r1, r2)):
        fields = jnp.where(sub == idx, val, fields)
    route_t_ref[...] = fields


def _merge_router(l, pool, attn, x2d, w_out, g_ffn, wr, br, cnt_in, tm):
    t = x2d.shape[0]
    utri = (jnp.arange(tm)[:, None] < jnp.arange(tm)[None, :]).astype(BF16)
    row = lambda i: (i, 0)
    lay = lambda i: (l, 0, 0)
    return pl.pallas_call(
        _merge_router_kernel,
        grid=(t // tm,),
        in_specs=[
            pl.BlockSpec((tm, POOL_W), row),
            pl.BlockSpec((tm, Q_W), row),
            pl.BlockSpec((tm, D_MODEL), row),
            pl.BlockSpec((None, D_MODEL, D_MODEL), lay),
            pl.BlockSpec((None, 1, D_MODEL), lay),
            pl.BlockSpec((None, D_MODEL, LANES), lay),
            pl.BlockSpec((None, 1, LANES), lay),
            pl.BlockSpec((tm, tm), lambda i: (0, 0)),
            pl.BlockSpec((N_EXPERTS, 1), lambda i: (0, 0)),
        ],
        out_specs=[
            pl.BlockSpec((tm, D_MODEL), row),
            pl.BlockSpec((tm, D_MODEL // 2), row),
            pl.BlockSpec((ROUTE_FIELDS, tm), lambda i: (0, i)),
            pl.BlockSpec((N_EXPERTS, 1), lambda i: (0, 0)),
        ],
        out_shape=[
            jax.ShapeDtypeStruct((t, D_MODEL), F32),
            jax.ShapeDtypeStruct((t, D_MODEL // 2), jnp.int32),
            jax.ShapeDtypeStruct((ROUTE_FIELDS, t), F32),
            jax.ShapeDtypeStruct((N_EXPERTS, 1), F32),
        ],
        scratch_shapes=[pltpu.VMEM((tm, D_MODEL), F32), pltpu.VMEM((tm, LANES), F32)],
        compiler_params=pltpu.CompilerParams(
            dimension_semantics=("arbitrary",), vmem_limit_bytes=VMEM_LIMIT),
        name="merge_router",
    )(pool, attn, x2d, w_out, g_ffn, wr, br, utri, cnt_in)


def _moe_kernel(be_ref, rv_ref, nx_ref, sl_ref, first_ref, xd_ref, wg_hbm, wu_hbm, wd_hbm, yd_ref,
                wg_f, wu_f, wd_f, wg_s, wu_s, wd_s, sem, *, layer):
    step = pl.program_id(0)

    def weight_copies(e, s):
        return [pltpu.make_async_copy(w_hbm.at[layer, e], w_f.at[s], sem.at[s, n])
                for n, (w_hbm, w_f) in enumerate(((wg_hbm, wg_f), (wu_hbm, wu_f), (wd_hbm, wd_f)))]

    @pl.when(step == 0)
    def _():
        for s in range(WEIGHT_SLOTS - 1):
            @pl.when(first_ref[s] >= 0)
            def _():
                for c in weight_copies(first_ref[s], s):
                    c.start()

    def enter_expert(i):
        expert, slot = be_ref[i], sl_ref[i]

        @pl.when((i == 0) | (expert != be_ref[jnp.maximum(i - 1, 0)]))
        def _():
            for c in weight_copies(expert, slot):
                c.wait()

            @pl.when(nx_ref[i] >= 0)
            def _():
                for c in weight_copies(nx_ref[i], lax.rem(slot + WEIGHT_SLOTS - 1, WEIGHT_SLOTS)):
                    c.start(priority=1)

            wg_s[...] = wg_f[slot].astype(BF16)
            wu_s[...] = wu_f[slot].astype(BF16)
            wd_s[...] = wd_f[slot].astype(BF16)

    def experts_on(row0, n_rows, rows_valid):
        rows = pl.ds(row0, n_rows)
        row = lax.broadcasted_iota(jnp.int32, (n_rows, D_MODEL // 2), 0)
        x = _unpack_bf16_pairs(jnp.where(row < rows_valid, xd_ref[rows, :], 0)).astype(BF16)
        gate = jnp.dot(x, wg_s[...], preferred_element_type=F32)
        up = jnp.dot(x, wu_s[...], preferred_element_type=F32)
        act = (gate * jax.nn.sigmoid(gate) * up).astype(BF16)
        y = jnp.dot(act, wd_s[...], preferred_element_type=F32)
        yd_ref[rows, :] = _pack_bf16_pairs(y.astype(BF16))

    def experts_ragged(row0, lead_rows, rows_last):
        half = MOE_BM // 2

        @pl.when(rows_last > half)
        def _():
            experts_on(row0, lead_rows + MOE_BM, lead_rows + rows_last)

        @pl.when(rows_last <= half)
        def _():
            experts_on(row0, lead_rows + half, lead_rows + rows_last)
            yd_ref[pl.ds(row0 + lead_rows + half, half), :] = jnp.zeros((half, D_MODEL // 2), jnp.int32)

    def single_block(i, row0):
        enter_expert(i)

        @pl.when(rv_ref[i] > 0)
        def _():
            experts_ragged(row0, 0, rv_ref[i])

        @pl.when(rv_ref[i] <= 0)
        def _():
            yd_ref[pl.ds(row0, MOE_BM), :] = jnp.zeros((MOE_BM, D_MODEL // 2), jnp.int32)

    @pl.when(rv_ref[step * MOE_STEP_BLOCKS] > 0)
    def _():
        for pair in range(MOE_STEP_BLOCKS // 2):
            ia = step * MOE_STEP_BLOCKS + 2 * pair
            ib = ia + 1
            row0 = 2 * pair * MOE_BM
            same = (be_ref[ib] == be_ref[ia]) & (rv_ref[ib] > 0)

            @pl.when(same)
            def _():
                enter_expert(ia)
                experts_ragged(row0, MOE_BM, rv_ref[ib])

            @pl.when(jnp.logical_not(same))
            def _():
                single_block(ia, row0)
                single_block(ib, row0 + MOE_BM)


def _moe_experts(l, block_e, rows_valid, next_e, slot, first_e, xd, w_gate, w_up, w_down):
    n_blocks = xd.shape[0] // MOE_BM
    step_rows = MOE_STEP_BLOCKS * MOE_BM
    row = lambda i, be, rv, nx, sl, fe: (jnp.minimum(i, fe[WEIGHT_SLOTS - 1] - 1), 0)
    return pl.pallas_call(
        functools.partial(_moe_kernel, layer=l),
        grid_spec=pltpu.PrefetchScalarGridSpec(
            num_scalar_prefetch=5,
            grid=(n_blocks // MOE_STEP_BLOCKS,),
            in_specs=[
                pl.BlockSpec((step_rows, D_MODEL // 2), row),
                pl.BlockSpec(memory_space=pl.ANY),
                pl.BlockSpec(memory_space=pl.ANY),
                pl.BlockSpec(memory_space=pl.ANY),
            ],
            out_specs=pl.BlockSpec((step_rows, D_MODEL // 2), row),
            scratch_shapes=[
                pltpu.VMEM((WEIGHT_SLOTS, D_MODEL, EXPERT_FF), F32),
                pltpu.VMEM((WEIGHT_SLOTS, D_MODEL, EXPERT_FF), F32),
                pltpu.VMEM((WEIGHT_SLOTS, EXPERT_FF, D_MODEL), F32),
                pltpu.VMEM((D_MODEL, EXPERT_FF), BF16),
                pltpu.VMEM((D_MODEL, EXPERT_FF), BF16),
                pltpu.VMEM((EXPERT_FF, D_MODEL), BF16),
                pltpu.SemaphoreType.DMA((WEIGHT_SLOTS, 3)),
            ],
        ),
        out_shape=jax.ShapeDtypeStruct((n_blocks * MOE_BM, D_MODEL // 2), jnp.int32),
        compiler_params=pltpu.CompilerParams(
            dimension_semantics=("arbitrary",), vmem_limit_bytes=VMEM_LIMIT),
        name="moe_experts",
    )(block_e, rows_valid, next_e, slot, first_e, xd, w_gate, w_up, w_down)


def _sc_worker_id():
    return lax.axis_index("s") * SC_CORES + lax.axis_index("c")


def _sc_dispatch(hp, hs, dest_p, dest_s, n_rows):
    tp, width = hp.shape
    per_w = tp // SC_WORKERS
    n_ch = per_w // DISP_CH
    n_sw = hs.shape[0] // SAMPLE_CH
    mesh = plsc.VectorSubcoreMesh(core_axis_name="c", subcore_axis_name="s")

    @functools.partial(
        pl.kernel, mesh=mesh,
        out_type=jax.ShapeDtypeStruct((n_rows, width), jnp.int32),
        scratch_types=[
            pltpu.VMEM((2, n_ch, DISP_CH), jnp.int32),
            pltpu.VMEM((2, 1, SAMPLE_CH), jnp.int32),
            pltpu.VMEM((SC_RING, DISP_CH, width), jnp.int32),
            pltpu.SemaphoreType.DMA((SC_RING,)),
            pltpu.SemaphoreType.DMA((SC_RING, 2)),
        ],
        name="sc_dispatch",
    )
    def k(hp_hbm, hs_hbm, dp_hbm, ds_hbm, xd_hbm, idx_v, idxs_v, bufs, rsem, wsem):
        wid = _sc_worker_id()
        base = wid * per_w
        for kk in range(2):
            pltpu.sync_copy(dp_hbm.at[kk, wid], idx_v.at[kk])
        reads = [pltpu.make_async_copy(hp_hbm.at[pl.ds(base + j * DISP_CH, DISP_CH)],
                                       bufs.at[j % SC_RING], rsem.at[j % SC_RING]) for j in range(n_ch)]
        writes = [[pltpu.make_async_copy(bufs.at[j % SC_RING], xd_hbm.at[idx_v.at[kk, j]],
                                         wsem.at[j % SC_RING, kk]) for kk in range(2)] for j in range(n_ch)]
        for j in range(min(SC_RING - 1, n_ch)):
            reads[j].start()
        for j in range(n_ch):
            reads[j].wait()
            for w in writes[j]:
                w.start()
            if j >= 1:
                for w in writes[j - 1]:
                    w.wait()
            if j + SC_RING - 1 < n_ch:
                reads[j + SC_RING - 1].start()
        for w in writes[n_ch - 1]:
            w.wait()

        @pl.when(wid < n_sw)
        def _():
            rows = bufs.at[0, pl.ds(0, SAMPLE_CH)]
            for kk in range(2):
                pltpu.sync_copy(ds_hbm.at[kk, wid], idxs_v.at[kk])
            pltpu.sync_copy(hs_hbm.at[pl.ds(wid * SAMPLE_CH, SAMPLE_CH)], rows)
            for kk in range(2):
                pltpu.sync_copy(rows, xd_hbm.at[idxs_v.at[kk, 0]])

    return k(hp, hs, dest_p, dest_s)


def _sc_sample_gather(yd, dest_s, ts):
    width = yd.shape[1]
    n_sw = ts // SAMPLE_CH
    mesh = plsc.VectorSubcoreMesh(core_axis_name="c", subcore_axis_name="s")

    @functools.partial(
        pl.kernel, mesh=mesh,
        out_type=jax.ShapeDtypeStruct((2, ts, width), yd.dtype),
        scratch_types=[
            pltpu.VMEM((2, 1, SAMPLE_CH), jnp.int32),
            pltpu.VMEM((2, SAMPLE_CH, width), yd.dtype),
        ],
        name="sc_sample_gather",
    )
    def k(yd_hbm, ds_hbm, g_hbm, idxs_v, bufs):
        wid = _sc_worker_id()

        @pl.when(wid < n_sw)
        def _():
            for kk in range(2):
                pltpu.sync_copy(ds_hbm.at[kk, wid], idxs_v.at[kk])
            for kk in range(2):
                pltpu.sync_copy(yd_hbm.at[idxs_v.at[kk, 0]], bufs.at[kk])
                pltpu.sync_copy(bufs.at[kk], g_hbm.at[kk, pl.ds(wid * SAMPLE_CH, SAMPLE_CH)])

    return k(yd, dest_s)


def _sc_combine_gather(yd, dest_p, tp):
    width = yd.shape[1]
    per_w = tp // SC_WORKERS
    n_ch = per_w // COMB_CH
    mesh = plsc.VectorSubcoreMesh(core_axis_name="c", subcore_axis_name="s")

    @functools.partial(
        pl.kernel, mesh=mesh,
        out_type=jax.ShapeDtypeStruct((2, tp, width), yd.dtype),
        scratch_types=[
            pltpu.VMEM((2, n_ch, COMB_CH), jnp.int32),
            pltpu.VMEM((SC_RING, COMB_CH, width), yd.dtype),
            pltpu.SemaphoreType.DMA((SC_RING,)),
            pltpu.SemaphoreType.DMA((SC_RING,)),
        ],
        name="sc_combine_gather",
    )
    def k(yd_hbm, dp_hbm, g_hbm, idx_v, bufs, gsem, wsem):
        wid = _sc_worker_id()
        base = wid * per_w
        for kk in range(2):
            pltpu.sync_copy(dp_hbm.at[kk, wid], idx_v.at[kk])
        items = [(kk, j) for kk in range(2) for j in range(n_ch)]
        n_items = len(items)
        gathers = [pltpu.make_async_copy(yd_hbm.at[idx_v.at[kk, j]], bufs.at[n % SC_RING], gsem.at[n % SC_RING])
                   for n, (kk, j) in enumerate(items)]
        outs = [pltpu.make_async_copy(bufs.at[n % SC_RING], g_hbm.at[kk, pl.ds(base + j * COMB_CH, COMB_CH)],
                                      wsem.at[n % SC_RING]) for n, (kk, j) in enumerate(items)]
        for n in range(min(SC_RING - 1, n_items)):
            gathers[n].start()
        for n in range(n_items):
            gathers[n].wait()
            outs[n].start()
            if n >= 1:
                outs[n - 1].wait()
            if n + SC_RING - 1 < n_items:
                gathers[n + SC_RING - 1].start()
        outs[n_items - 1].wait()

    return k(yd, dest_p)


def _combine_kernel(x1_ref, g_ref, route_t_ref, x2_ref):
    fields = route_t_ref[...]
    tm = fields.shape[1]
    cols = jnp.transpose(jnp.concatenate([fields, jnp.zeros((LANES - ROUTE_FIELDS, tm), F32)], axis=0))
    w1 = cols[:, 2:3]
    w2 = cols[:, 3:4]
    x2_ref[...] = x1_ref[...] + _unpack_bf16_pairs(g_ref[0]) * w1 + _unpack_bf16_pairs(g_ref[1]) * w2


def _combine(x1, g, route_t, row0, tm):
    t = x1.shape[0]
    blk0 = row0 // tm
    row = lambda i: (i, 0)
    return pl.pallas_call(
        _combine_kernel,
        grid=(t // tm,),
        in_specs=[
            pl.BlockSpec((tm, D_MODEL), row),
            pl.BlockSpec((2, tm, D_MODEL // 2), lambda i: (0, blk0 + i, 0)),
            pl.BlockSpec((ROUTE_FIELDS, tm), lambda i: (0, i)),
        ],
        out_specs=pl.BlockSpec((tm, D_MODEL), row),
        out_shape=jax.ShapeDtypeStruct((t, D_MODEL), F32),
        compiler_params=pltpu.CompilerParams(
            dimension_semantics=("arbitrary",), vmem_limit_bytes=VMEM_LIMIT),
        name="combine",
    )(x1, g, route_t)


def _dest_layout(dest, workers, chunk):
    t = dest.shape[1]
    return dest.reshape(2, workers, t // (workers * chunk), chunk)


def _hier_moe(l, h2p, h2s, route_tp, route_ts, counts, w_gate, w_up, w_down):
    tp, ts = h2p.shape[0], h2s.shape[0]
    n_assign = 2 * (tp + ts)
    n_blocks = -(-n_assign // MOE_BM) + N_EXPERTS
    n_blocks = -(-n_blocks // MOE_STEP_BLOCKS) * MOE_STEP_BLOCKS
    pcounts = (counts + MOE_BM - 1) // MOE_BM * MOE_BM
    pends = jnp.cumsum(pcounts)
    poffsets = pends - pcounts
    starts = jnp.arange(n_blocks, dtype=jnp.int32) * MOE_BM
    block_e = jnp.minimum(jnp.sum((pends[None, :] <= starts[:, None]).astype(jnp.int32), axis=1),
                          N_EXPERTS - 1)
    experts = jnp.arange(N_EXPERTS, dtype=jnp.int32)

    def lookup(table, idx):
        return jnp.sum(jnp.where(idx[..., None] == experts, table, 0), axis=-1)

    rows_valid = jnp.clip(lookup(poffsets + counts, block_e) - starts, 0, MOE_BM).astype(jnp.int32)
    used = counts > 0
    last_e = jnp.max(jnp.where(used, jnp.arange(N_EXPERTS, dtype=jnp.int32), 0))
    block_e = jnp.where(rows_valid > 0, block_e, last_e).astype(jnp.int32)
    place = jnp.cumsum(used.astype(jnp.int32)) - 1
    by_place = jnp.sum(jnp.where(used[None, :] & (place[None, :] == experts[:, None]), experts[None, :], 0),
                       axis=1)
    n_used = jnp.sum(used.astype(jnp.int32))

    def at_place(p):
        return jnp.where(p < n_used, lookup(by_place, jnp.minimum(p, N_EXPERTS - 1)), -1).astype(jnp.int32)

    ahead_of = at_place(place + (WEIGHT_SLOTS - 1))
    next_e = lookup(ahead_of, block_e)
    slot = lookup(place % WEIGHT_SLOTS, block_e)
    n_steps_used = -(-(pends[-1] // MOE_BM) // MOE_STEP_BLOCKS)
    first_e = jnp.concatenate([at_place(jnp.arange(WEIGHT_SLOTS - 1, dtype=jnp.int32)),
                               n_steps_used.reshape(1).astype(jnp.int32)])

    def dest_of(route_t):
        return lookup(poffsets, route_t[0:2].astype(jnp.int32)) + route_t[4:6].astype(jnp.int32)

    dest_p, dest_s = dest_of(route_tp), dest_of(route_ts)
    n_sw = ts // SAMPLE_CH
    xd = _sc_dispatch(h2p, h2s, _dest_layout(dest_p, SC_WORKERS, DISP_CH),
                      _dest_layout(dest_s, n_sw, SAMPLE_CH), n_blocks * MOE_BM)
    yd = _moe_experts(l, block_e, rows_valid, next_e, slot, first_e, xd, w_gate, w_up, w_down)
    g_s = _sc_sample_gather(yd, _dest_layout(dest_s, n_sw, SAMPLE_CH), ts)
    g_p = _sc_combine_gather(yd, _dest_layout(dest_p, SC_WORKERS, COMB_CH), tp)
    return g_p, g_s


def kernel(x_prompt, x_sample, state_pool, cache_k_win, cache_v_win, norm_attn_g, w_in, pool_w, pool_scale, q_norm_g, k_norm_g, attn_sinks, w_out, norm_ffn_g, router_group_w, router_group_b, router_expert_w, router_expert_b, w_gate, w_up, w_down):
    n_p, t_p, d = x_prompt.shape
    n_s, t_s, _ = x_sample.shape
    depth = w_in.shape[0]
    lw_s = cache_k_win.shape[2]
    assert t_s == 1 and lw_s == WINDOW and d == D_MODEL
    assert t_p % TM_PROJ == 0 and t_p >= WINDOW

    seg = jnp.arange(256) // HEAD_DIM
    bd = jnp.where(seg[:, None] == seg[None, :], 1.0 / HEAD_DIM, 0.0).astype(BF16)
    slopes = jnp.exp2(-8.0 * jnp.arange(1, N_HEADS + 1, dtype=F32) / N_HEADS)
    bias_p = _prompt_bias_t()
    dist_s = (WINDOW - 1) - jnp.arange(WINDOW, dtype=F32)
    bias_s = -LOG2E * slopes[:, None] * dist_s[None, :]

    zero_gc = jnp.zeros((depth, POOL_GC, POOL_GC), F32)
    wp = jnp.stack([jnp.concatenate([jnp.concatenate([pool_w[:, 2 * p], zero_gc], axis=-1),
                                     jnp.concatenate([zero_gc, pool_w[:, 2 * p + 1]], axis=-1)], axis=-2)
                    for p in range(2)], axis=1)
    assert GROUP_LANE0 == N_EXPERTS
    lane_pad = LANES - N_EXPERTS - N_EXPERT_GROUPS
    wr = jnp.concatenate([router_expert_w, router_group_w, jnp.zeros((depth, D_MODEL, lane_pad), F32)], axis=-1)
    br = jnp.concatenate([router_expert_b, router_group_b, jnp.zeros((depth, lane_pad), F32)],
                         axis=-1).reshape(depth, 1, LANES)
    lp = dict(
        w_in=w_in.astype(BF16),
        w_out=w_out.astype(BF16),
        g_attn=norm_attn_g.reshape(depth, 1, D_MODEL),
        g_ffn=norm_ffn_g.reshape(depth, 1, D_MODEL),
        qg=(jnp.tile(q_norm_g, (1, N_HEADS)) * (ATTN_SCALE * LOG2E)).reshape(depth, 1, Q_W),
        kg=jnp.tile(k_norm_g, (1, N_KV_HEADS)).reshape(depth, 1, KV_W),
        wp=wp.astype(BF16),
        ps=pool_scale.reshape(depth, 1, POOL_W),
        wr=wr.astype(BF16),
        br=br,
        state=state_pool,
        ck=cache_k_win.reshape(depth, n_s, lw_s, KV_W),
        cv=cache_v_win.reshape(depth, n_s, lw_s, KV_W),
    )

    xp = x_prompt.reshape(n_p * t_p, D_MODEL)
    xs = x_sample.reshape(n_s, D_MODEL)
    lw_p = min(WINDOW, t_p)
    pool_p, kp_new, vp_new = [], [], []
    sample_state = [lp["state"], lp["ck"], lp["cv"]]
    zero_cnt = jnp.zeros((N_EXPERTS, 1), F32)
    pending = None
    for l in range(depth):
        sinks = attn_sinks[l] * LOG2E
        pool_so, attn_so, *sample_state = _sample_mixer(
            l, depth, xs, lp["g_attn"], lp["w_in"], lp["qg"], lp["kg"], bd, lp["wp"], lp["ps"],
            *sample_state, sinks.reshape(N_HEADS, 1), bias_s, PAST_LEN)
        x1s, h2s, route_ts, cnt_s = _merge_router(
            l, pool_so, attn_so, xs, lp["w_out"], lp["g_ffn"], lp["wr"], lp["br"], zero_cnt, n_s)
        outs = _proj_pool_prompt(
            l, xp if pending is None else pending, n_p, t_p,
            lp["g_attn"], lp["w_in"], lp["qg"], lp["kg"], bd, lp["wp"], lp["ps"])
        if pending is not None:
            xp, outs = outs[0], outs[1:]
        pool_o, q, k, vt, utail, ktail, vtail = outs
        attn_o = _attn_prompt(q, k, vt, bias_p, sinks, n_p, t_p)
        x1p, h2p, route_tp, cnt_all = _merge_router(
            l, pool_o, attn_o, xp, lp["w_out"], lp["g_ffn"], lp["wr"], lp["br"], cnt_s, TM_MERGE)
        pool_p.append(utail[:, 16 - POOL_STATE:, :])
        kp_new.append(ktail)
        vp_new.append(vtail)
        counts = cnt_all[:, 0].astype(jnp.int32)
        g_p, g_s = _hier_moe(l, h2p, h2s, route_tp, route_ts, counts, w_gate, w_up, w_down)
        xs = _combine(x1s, g_s, route_ts, 0, n_s)
        pending = (x1p, g_p, route_tp)
    xp = _combine(*pending, 0, TM_MERGE)
    return (xp.reshape(n_p, t_p, D_MODEL), xs.reshape(n_s, t_s, D_MODEL),
            jnp.stack(pool_p),
            jnp.stack(kp_new).reshape(depth, n_p, lw_p, N_KV_HEADS, HEAD_DIM),
            jnp.stack(vp_new).reshape(depth, n_p, lw_p, N_KV_HEADS, HEAD_DIM),
            sample_state[0],
            sample_state[1].reshape(depth, n_s, lw_s, N_KV_HEADS, HEAD_DIM),
            sample_state[2].reshape(depth, n_s, lw_s, N_KV_HEADS, HEAD_DIM))
```

```python
import functools

import jax
import jax.numpy as jnp
from jax import lax
from jax.experimental import pallas as pl
from jax.experimental.pallas import tpu as pltpu
from jax.experimental.pallas import tpu_sc as plsc

D_MODEL = 1024
POOL_W = 512
POOL_WINDOWS = (2, 4, 8, 16)
POOL_GC = 128
POOL_STATE = 15
HEAD_DIM = 64
N_HEADS = 8
N_KV_HEADS = 2
GQA_GROUP = 4
Q_W = 512
KV_W = 128
D_IN = POOL_W + Q_W + 2 * KV_W
WINDOW = 128
ATTN_SCALE = HEAD_DIM ** -0.5
LOG2E = 1.4426950408889634
N_EXPERT_GROUPS = 4
EXPERTS_PER_GROUP = 8
N_EXPERTS = 32
EXPERT_FF = 512
EPS = 1e-6
PAST_LEN = 16384

LANES = 128
HALO = 32
TM_PROJ = 1024
TM_MERGE = 1024
MERGE_CHUNKS = 4
ATTN_QB = 16
MOE_BM = 256
MOE_STEP_BLOCKS = 4
WEIGHT_SLOTS = 3
GROUP_LANE0 = 32
ROUTE_FIELDS = 8
SC_CORES = 2
SC_SUBCORES = 16
SC_WORKERS = SC_CORES * SC_SUBCORES
DISP_CH = 64
COMB_CH = 64
SAMPLE_CH = 32
SC_RING = 3
VMEM_LIMIT = 48 * 1024 * 1024

BF16 = jnp.bfloat16
F32 = jnp.float32


def _pack_bf16_pairs(h):
    w = h.shape[1] // 2
    hi = lax.bitcast_convert_type(h[:, :w].astype(F32), jnp.uint32)
    lo = lax.bitcast_convert_type(h[:, w:].astype(F32), jnp.uint32)
    return lax.bitcast_convert_type(hi | (lo >> 16), jnp.int32)


def _unpack_bf16_pairs(words):
    u = lax.bitcast_convert_type(words, jnp.uint32)
    hi = lax.bitcast_convert_type(u & jnp.uint32(0xFFFF0000), F32)
    lo = lax.bitcast_convert_type(u << 16, F32)
    return jnp.concatenate([hi, lo], axis=-1)


def _segment_mean_sq(a, bd):
    w = a.shape[1]
    return jnp.dot((a * a).astype(BF16), bd[:w, :w], preferred_element_type=F32)


def _rms_bf16(x, g):
    ms = jnp.mean(x * x, axis=-1, keepdims=True)
    return (x * lax.rsqrt(ms + EPS) * g).astype(BF16)


def _qk_norm(q, k, qg, kg, bd):
    qn = []
    for c in range(Q_W // 256):
        qc = q[:, c * 256:(c + 1) * 256]
        qn.append(qc * lax.rsqrt(_segment_mean_sq(qc, bd) + EPS))
    qn = jnp.concatenate(qn, axis=-1) * qg
    kn = k * lax.rsqrt(_segment_mean_sq(k, bd) + EPS) * kg
    return qn, kn


def _project(x, g, w_in, qg, kg, bd):
    z = jnp.dot(_rms_bf16(x, g), w_in, preferred_element_type=F32)
    u = z[:, :POOL_W]
    q = z[:, POOL_W:POOL_W + Q_W]
    k = z[:, POOL_W + Q_W:POOL_W + Q_W + KV_W]
    v = z[:, POOL_W + Q_W + KV_W:]
    qn, kn = _qk_norm(q, k, qg, kg, bd)
    return u, qn, kn, v


def _pool_project(d_groups, wp_ref, ps):
    outs = []
    for p in range(2):
        dp = jnp.concatenate([d_groups[2 * p], d_groups[2 * p + 1]], axis=-1).astype(BF16)
        y = jnp.dot(dp, wp_ref[p], preferred_element_type=F32)
        outs.append(y * ps[:, p * 256:(p + 1) * 256])
    return jnp.concatenate(outs, axis=-1)


def _proj_pool_kernel(x_ref, g_ref, win_ref, qg_ref, kg_ref, bd_ref, wp_ref, ps_ref,
                      pool_ref, q_ref, k_ref, vt_ref, utail_ref, ktail_ref, vtail_ref,
                      ext_ref, sa_ref, sb_ref, zq_ref, *, tm, n_j):
    j = pl.program_id(1)

    @pl.when(j == 0)
    def _():
        ext_ref[0:HALO, :] = jnp.zeros((HALO, POOL_W), F32)

    r = tm + HALO
    h = _rms_bf16(x_ref[...], g_ref[...])
    ext_ref[HALO:r, :] = jnp.dot(h, win_ref[:, 0:POOL_W], preferred_element_type=F32)
    zq_ref[...] = jnp.dot(h, win_ref[:, POOL_W:], preferred_element_type=F32)
    u = ext_ref[HALO:r, :]
    sa_ref[8:r, :] = ext_ref[8:r, :] + ext_ref[7:r - 1, :]
    sb_ref[16:r, 128:] = sa_ref[16:r, 128:] + sa_ref[14:r - 2, 128:]
    sa_ref[24:r, 256:] = sb_ref[24:r, 256:] + sb_ref[20:r - 4, 256:]
    sb_ref[32:r, 384:] = sa_ref[32:r, 384:] + sa_ref[24:r - 8, 384:]
    pos1 = j * tm + lax.broadcasted_iota(jnp.int32, (tm, POOL_GC), 0) + 1
    sums = (sa_ref, sb_ref, sa_ref, sb_ref)
    d_groups = []
    for gi, w in enumerate(POOL_WINDOWS):
        sl = slice(gi * POOL_GC, (gi + 1) * POOL_GC)
        cnt = jnp.minimum(pos1, w).astype(F32)
        d_groups.append(sums[gi][HALO:r, sl] / cnt - u[:, sl])
    pool_ref[...] = _pool_project(d_groups, wp_ref, ps_ref[...]).astype(BF16)
    ext_ref[16:HALO, :] = ext_ref[tm + 16:r, :]

    qn, kn = _qk_norm(zq_ref[:, 0:Q_W], zq_ref[:, Q_W:Q_W + KV_W], qg_ref[...], kg_ref[...], bd_ref[...])
    v = zq_ref[:, Q_W + KV_W:]
    q_ref[...] = qn.astype(BF16)
    k_ref[...] = kn.astype(BF16)
    vt_ref[...] = jnp.transpose(v).astype(BF16)

    @pl.when(j == n_j - 1)
    def _():
        utail_ref[...] = u[tm - 16:, :]
        ktail_ref[...] = kn[tm - WINDOW:, :]
        vtail_ref[...] = v[tm - WINDOW:, :]


def _proj_pool_combine_kernel(x1_ref, gath_ref, route_ref, *rest, tm, n_j):
    x2_ref = rest[7]
    _combine_kernel(x1_ref, gath_ref, route_ref, x2_ref)
    _proj_pool_kernel(x2_ref, *rest[:7], *rest[8:], tm=tm, n_j=n_j)


def _proj_pool_prompt(l, x_in, n_seq, seq, g_attn, w_in, qg, kg, bd, wp, ps):
    tm = TM_PROJ
    n_j = seq // tm
    t = n_seq * seq
    row = lambda b, j: (b * n_j + j, 0)
    lay = lambda b, j: (l, 0, 0)
    fused = isinstance(x_in, tuple)
    if fused:
        kern = _proj_pool_combine_kernel
        x_args = list(x_in)
        x_specs = [pl.BlockSpec((tm, D_MODEL), row),
                   pl.BlockSpec((2, tm, D_MODEL // 2), lambda b, j: (0, b * n_j + j, 0)),
                   pl.BlockSpec((ROUTE_FIELDS, tm), lambda b, j: (0, b * n_j + j))]
        x_out_specs = [pl.BlockSpec((tm, D_MODEL), row)]
        x_out_shape = [jax.ShapeDtypeStruct((t, D_MODEL), F32)]
    else:
        kern = _proj_pool_kernel
        x_args = [x_in]
        x_specs = [pl.BlockSpec((tm, D_MODEL), row)]
        x_out_specs, x_out_shape = [], []
    return pl.pallas_call(
        functools.partial(kern, tm=tm, n_j=n_j),
        grid=(n_seq, n_j),
        in_specs=x_specs + [
            pl.BlockSpec((None, 1, D_MODEL), lay),
            pl.BlockSpec((None, D_MODEL, D_IN), lay),
            pl.BlockSpec((None, 1, Q_W), lay),
            pl.BlockSpec((None, 1, KV_W), lay),
            pl.BlockSpec((256, 256), lambda b, j: (0, 0)),
            pl.BlockSpec((None, 2, 256, 256), lambda b, j: (l, 0, 0, 0)),
            pl.BlockSpec((None, 1, POOL_W), lay),
        ],
        out_specs=x_out_specs + [
            pl.BlockSpec((tm, POOL_W), row),
            pl.BlockSpec((tm, Q_W), row),
            pl.BlockSpec((tm, KV_W), row),
            pl.BlockSpec((KV_W, tm), lambda b, j: (0, b * n_j + j)),
            pl.BlockSpec((None, 16, POOL_W), lambda b, j: (b, 0, 0)),
            pl.BlockSpec((None, WINDOW, KV_W), lambda b, j: (b, 0, 0)),
            pl.BlockSpec((None, WINDOW, KV_W), lambda b, j: (b, 0, 0)),
        ],
        out_shape=x_out_shape + [
            jax.ShapeDtypeStruct((t, POOL_W), BF16),
            jax.ShapeDtypeStruct((t, Q_W), BF16),
            jax.ShapeDtypeStruct((t, KV_W), BF16),
            jax.ShapeDtypeStruct((KV_W, t), BF16),
            jax.ShapeDtypeStruct((n_seq, 16, POOL_W), F32),
            jax.ShapeDtypeStruct((n_seq, WINDOW, KV_W), F32),
            jax.ShapeDtypeStruct((n_seq, WINDOW, KV_W), F32),
        ],
        scratch_shapes=[pltpu.VMEM((tm + HALO, POOL_W), F32)] * 3 + [pltpu.VMEM((tm, Q_W + 2 * KV_W), F32)],
        compiler_params=pltpu.CompilerParams(
            dimension_semantics=("arbitrary", "arbitrary"), vmem_limit_bytes=VMEM_LIMIT),
        name="proj_pool_prompt",
    )(*x_args, g_attn, w_in, qg, kg, bd, wp, ps)


def _attn_kernel(sink_ref, q_ref, kp_ref, kc_ref, vtp_ref, vtc_ref, bias_ref, o_ref, s_ref):
    j = pl.program_id(1)
    kk_all = jnp.concatenate([kp_ref[...], kc_ref[...]], axis=0)
    vt_all = jnp.concatenate([vtp_ref[...], vtc_ref[...]], axis=1)
    from_prev = (lax.broadcasted_iota(jnp.int32, (WINDOW, WINDOW), 0)
                 > lax.broadcasted_iota(jnp.int32, (WINDOW, WINDOW), 1))
    units = [(blk, kv) for blk in range(ATTN_QB) for kv in range(N_KV_HEADS)]

    def scores(n):
        blk, kv = units[n]
        q = q_ref[blk * WINDOW:(blk + 1) * WINDOW, :]
        kk = kk_all[blk * WINDOW:(blk + 2) * WINDOW, kv * HEAD_DIM:(kv + 1) * HEAD_DIM]
        heads = range(kv * GQA_GROUP, (kv + 1) * GQA_GROUP)
        q_rows = jnp.concatenate([q[:, h * HEAD_DIM:(h + 1) * HEAD_DIM] for h in heads], axis=0)
        s_ref[n % 3] = lax.dot_general(kk, q_rows, (((1,), (1,)), ((), ())), preferred_element_type=F32)

    scores(0)
    scores(1)
    outs = []
    for n, (blk, kv) in enumerate(units):
        if n + 2 < len(units):
            scores(n + 2)
        vt_kv = vt_all[kv * HEAD_DIM:(kv + 1) * HEAD_DIM, blk * WINDOW:(blk + 2) * WINDOW]
        variant = jnp.minimum(j, 1) if blk == 0 else 1
        for g in range(GQA_GROUP):
            h = kv * GQA_GROUP + g
            s = jnp.where(from_prev, s_ref[n % 3, 0:WINDOW, g * WINDOW:(g + 1) * WINDOW],
                          s_ref[n % 3, WINDOW:, g * WINDOW:(g + 1) * WINDOW]) + bias_ref[variant, h]
            sink = sink_ref[h]
            m = jnp.maximum(jnp.max(s, axis=0, keepdims=True), sink)
            p = jnp.exp2(s - m)
            denom = jnp.sum(p, axis=0, keepdims=True) + jnp.exp2(sink - m)
            p_keys = jnp.concatenate([jnp.where(from_prev, p, 0.0), jnp.where(from_prev, 0.0, p)], axis=0)
            o_t = jnp.dot(vt_kv, p_keys.astype(BF16), preferred_element_type=F32)
            outs.append(o_t / denom)
        if kv == N_KV_HEADS - 1:
            o_ref[blk * WINDOW:(blk + 1) * WINDOW, :] = jnp.transpose(jnp.concatenate(outs, axis=0)).astype(BF16)
            outs = []


def _attn_prompt(q, k, vt, bias_t, sinks, n_seq, seq):
    tq = ATTN_QB * WINDOW
    nj = seq // tq
    t = n_seq * seq
    cur = lambda b, j: (b * nj + j, 0)
    prev = lambda b, j: (jnp.maximum((b * nj + j) * ATTN_QB - 1, 0), 0)
    cur_t = lambda b, j: (0, b * nj + j)
    prev_t = lambda b, j: (0, jnp.maximum((b * nj + j) * ATTN_QB - 1, 0))
    return pl.pallas_call(
        _attn_kernel,
        grid=(n_seq, nj),
        in_specs=[
            pl.BlockSpec(memory_space=pltpu.SMEM),
            pl.BlockSpec((tq, Q_W), cur),
            pl.BlockSpec((WINDOW, KV_W), prev),
            pl.BlockSpec((tq, KV_W), cur),
            pl.BlockSpec((KV_W, WINDOW), prev_t),
            pl.BlockSpec((KV_W, tq), cur_t),
            pl.BlockSpec((2, N_HEADS, WINDOW, WINDOW), lambda b, j: (0, 0, 0, 0)),
        ],
        out_specs=pl.BlockSpec((tq, Q_W), cur),
        out_shape=jax.ShapeDtypeStruct((t, Q_W), BF16),
        scratch_shapes=[pltpu.VMEM((3, 2 * WINDOW, GQA_GROUP * WINDOW), F32)],
        compiler_params=pltpu.CompilerParams(
            dimension_semantics=("arbitrary", "arbitrary"), vmem_limit_bytes=VMEM_LIMIT),
        name="attn_prompt",
    )(sinks, q, k, k, vt, vt, bias_t)


def _prompt_bias_t():
    r = jnp.arange(WINDOW, dtype=jnp.int32)[None, :]
    c = jnp.arange(WINDOW, dtype=jnp.int32)[:, None]
    from_prev = c > r
    dist = r - c + jnp.where(from_prev, WINDOW, 0)
    slopes = jnp.exp2(-8.0 * jnp.arange(1, N_HEADS + 1, dtype=F32) / N_HEADS)
    later = -LOG2E * slopes[:, None, None] * dist.astype(F32)[None]
    first = jnp.where(from_prev[None], -jnp.inf, later)
    return jnp.stack([first, later])


def _sample_kernel(x_ref, g_ref, win_ref, qg_ref, kg_ref, bd_ref, wp_ref, ps_ref,
                   st_ref, ck_ref, cv_ref, sink_ref, bias_ref, perm_ref,
                   pool_ref, attn_ref, pst_ref, kc_ref, vc_ref, *, ns, pos0):
    u, qn, kn, v = _project(x_ref[...], g_ref[...], win_ref[...], qg_ref[...], kg_ref[...], bd_ref[...])
    pst_ref[:, 0:POOL_STATE - 1, :] = st_ref[:, 1:POOL_STATE, :]
    kc_ref[:, 0:WINDOW - 1, :] = ck_ref[:, 1:WINDOW, :]
    vc_ref[:, 0:WINDOW - 1, :] = cv_ref[:, 1:WINDOW, :]
    for n in range(ns):
        pst_ref[n, POOL_STATE - 1:POOL_STATE, :] = u[n:n + 1, :]
        kc_ref[n, WINDOW - 1:WINDOW, :] = kn[n:n + 1, :]
        vc_ref[n, WINDOW - 1:WINDOW, :] = v[n:n + 1, :]

    d_groups = []
    for gi, w in enumerate(POOL_WINDOWS):
        lo = gi * POOL_GC
        acc = u[:, lo:lo + POOL_GC]
        for back in range(1, w):
            acc = acc + st_ref[:, POOL_STATE - back, lo:lo + POOL_GC]
        d_groups.append(acc / float(min(pos0 + 1, w)) - u[:, lo:lo + POOL_GC])
    pool_ref[...] = _pool_project(d_groups, wp_ref, ps_ref[...]).astype(BF16)

    zeros = jnp.zeros((ns, HEAD_DIM), F32)
    stacked = []
    for h in range(N_HEADS):
        piece = qn[:, h * HEAD_DIM:(h + 1) * HEAD_DIM]
        pair = [piece, zeros] if h < GQA_GROUP else [zeros, piece]
        stacked.append(jnp.concatenate(pair, axis=-1))
    q_hn = jnp.concatenate(stacked, axis=0).astype(BF16)
    q_nh = jnp.dot(perm_ref[0], q_hn, preferred_element_type=F32).astype(BF16)

    keys = kc_ref[...].reshape(ns * WINDOW, KV_W).astype(BF16)
    vals = vc_ref[...].reshape(ns * WINDOW, KV_W).astype(BF16)
    s_all = lax.dot_general(q_nh, keys, (((1,), (1,)), ((), ())), preferred_element_type=F32)
    sink = sink_ref[...]
    bias = bias_ref[...]
    zero_blk = jnp.zeros((N_HEADS, WINDOW), F32)
    p_rows = []
    for n in range(ns):
        s = s_all[n * N_HEADS:(n + 1) * N_HEADS, n * WINDOW:(n + 1) * WINDOW] + bias
        m = jnp.maximum(jnp.max(s, axis=-1, keepdims=True), sink)
        p = jnp.exp2(s - m)
        denom = jnp.sum(p, axis=-1, keepdims=True) + jnp.exp2(sink - m)
        p_rows.append(jnp.concatenate([zero_blk] * n + [p / denom] + [zero_blk] * (ns - 1 - n), axis=-1))
    p_blockdiag = jnp.concatenate(p_rows, axis=0).astype(BF16)
    o_nh = jnp.dot(p_blockdiag, vals, preferred_element_type=F32).astype(BF16)
    o_hn = jnp.dot(perm_ref[1], o_nh, preferred_element_type=F32)
    pieces = []
    for h in range(N_HEADS):
        kv = h // GQA_GROUP
        pieces.append(o_hn[h * ns:(h + 1) * ns, kv * HEAD_DIM:(kv + 1) * HEAD_DIM])
    attn_ref[...] = jnp.concatenate(pieces, axis=-1).astype(BF16)


def _sample_mixer(l, depth, xs, g_attn, w_in, qg, kg, bd, wp, ps, state, ck, cv, sink8, bias_s, pos0):
    n = xs.shape[0]
    ns = 32
    row = lambda i: (i, 0)
    lay = lambda i: (l, 0, 0)
    src = jnp.arange(ns * N_HEADS)
    perm = (((src % N_HEADS) * ns + src // N_HEADS)[:, None] == src[None, :]).astype(BF16)
    perms = jnp.stack([perm, perm.T])
    return pl.pallas_call(
        functools.partial(_sample_kernel, ns=ns, pos0=pos0),
        grid=(n // ns,),
        input_output_aliases={8: 2, 9: 3, 10: 4},
        in_specs=[
            pl.BlockSpec((ns, D_MODEL), row),
            pl.BlockSpec((None, 1, D_MODEL), lay),
            pl.BlockSpec((None, D_MODEL, D_IN), lay),
            pl.BlockSpec((None, 1, Q_W), lay),
            pl.BlockSpec((None, 1, KV_W), lay),
            pl.BlockSpec((256, 256), lambda i: (0, 0)),
            pl.BlockSpec((None, 2, 256, 256), lambda i: (l, 0, 0, 0)),
            pl.BlockSpec((None, 1, POOL_W), lay),
            pl.BlockSpec((None, ns, POOL_STATE, POOL_W), lambda i: (l, i, 0, 0)),
            pl.BlockSpec((None, ns, WINDOW, KV_W), lambda i: (l, i, 0, 0)),
            pl.BlockSpec((None, ns, WINDOW, KV_W), lambda i: (l, i, 0, 0)),
            pl.BlockSpec((N_HEADS, 1), lambda i: (0, 0)),
            pl.BlockSpec((N_HEADS, WINDOW), lambda i: (0, 0)),
            pl.BlockSpec((2, ns * N_HEADS, ns * N_HEADS), lambda i: (0, 0, 0)),
        ],
        out_specs=[
            pl.BlockSpec((ns, POOL_W), row),
            pl.BlockSpec((ns, Q_W), row),
            pl.BlockSpec((None, ns, POOL_STATE, POOL_W), lambda i: (l, i, 0, 0)),
            pl.BlockSpec((None, ns, WINDOW, KV_W), lambda i: (l, i, 0, 0)),
            pl.BlockSpec((None, ns, WINDOW, KV_W), lambda i: (l, i, 0, 0)),
        ],
        out_shape=[
            jax.ShapeDtypeStruct((n, POOL_W), BF16),
            jax.ShapeDtypeStruct((n, Q_W), BF16),
            jax.ShapeDtypeStruct((depth, n, POOL_STATE, POOL_W), F32),
            jax.ShapeDtypeStruct((depth, n, WINDOW, KV_W), F32),
            jax.ShapeDtypeStruct((depth, n, WINDOW, KV_W), F32),
        ],
        compiler_params=pltpu.CompilerParams(
            dimension_semantics=("arbitrary",), vmem_limit_bytes=VMEM_LIMIT),
        name="sample_mixer",
    )(xs, g_attn, w_in, qg, kg, bd, wp, ps, state, ck, cv, sink8, bias_s, perms)


def _merge_router_kernel(pool_ref, attn_ref, x_ref, wout_ref, g_ref, wr_ref, br_ref, utri_ref, cin_ref,
                         x1_ref, h2_ref, route_t_ref, cnt_ref, y_ref, lg_ref):
    i = pl.program_id(0)

    @pl.when(i == 0)
    def _():
        cnt_ref[...] = cin_ref[...]

    tm = x_ref.shape[0]
    rc = tm // MERGE_CHUNKS
    chunks = [slice(ci * rc, (ci + 1) * rc) for ci in range(MERGE_CHUNKS)]
    for rows in chunks:
        y_ref[rows, :] = (jnp.dot(pool_ref[rows, :], wout_ref[0:POOL_W, :], preferred_element_type=F32)
                          + jnp.dot(attn_ref[rows, :], wout_ref[POOL_W:, :], preferred_element_type=F32))
    for rows in chunks:
        x1 = x_ref[rows, :] + y_ref[rows, :]
        x1_ref[rows, :] = x1
        h2 = _rms_bf16(x1, g_ref[...])
        h2_ref[rows, :] = _pack_bf16_pairs(h2)
        lg_ref[rows, :] = jnp.dot(h2, wr_ref[...], preferred_element_type=F32) + br_ref[...]
    logits = lg_ref[...]

    lt = jnp.transpose(logits)
    sub = lax.broadcasted_iota(jnp.int32, (EXPERTS_PER_GROUP, tm), 0)
    neg = -jnp.inf
    big = jnp.int32(EXPERTS_PER_GROUP)
    gl = jnp.where(sub < N_EXPERT_GROUPS, lt[GROUP_LANE0:GROUP_LANE0 + EXPERTS_PER_GROUP, :], neg)
    gmax = jnp.max(gl, axis=0, keepdims=True)
    grp = jnp.min(jnp.where(gl == gmax, sub, big), axis=0, keepdims=True)
    g_w = 1.0 / jnp.sum(jnp.exp(gl - gmax), axis=0, keepdims=True)
    el = lt[(N_EXPERT_GROUPS - 1) * EXPERTS_PER_GROUP:N_EXPERT_GROUPS * EXPERTS_PER_GROUP, :]
    for gi in range(N_EXPERT_GROUPS - 2, -1, -1):
        el = jnp.where(grp == gi, lt[gi * EXPERTS_PER_GROUP:(gi + 1) * EXPERTS_PER_GROUP, :], el)
    v1 = jnp.max(el, axis=0, keepdims=True)
    i1 = jnp.min(jnp.where(el == v1, sub, big), axis=0, keepdims=True)
    el2 = jnp.where(sub == i1, neg, el)
    v2 = jnp.max(el2, axis=0, keepdims=True)
    i2 = jnp.min(jnp.where(el2 == v2, sub, big), axis=0, keepdims=True)
    e21 = jnp.exp(v2 - v1)
    w1 = g_w / (1.0 + e21)
    w2 = g_w * e21 / (1.0 + e21)
    e1 = grp * EXPERTS_PER_GROUP + i1
    e2 = grp * EXPERTS_PER_GROUP + i2

    esub = lax.broadcasted_iota(jnp.int32, (N_EXPERTS, tm), 0)
    oh1 = esub == e1
    oh2 = esub == e2
    c = jnp.where(oh1 | oh2, 1.0, 0.0)
    prefix = jnp.dot(c.astype(BF16), utri_ref[...], preferred_element_type=F32) + cnt_ref[...]
    r1 = jnp.sum(jnp.where(oh1, prefix, 0.0), axis=0, keepdims=True)
    r2 = jnp.sum(jnp.where(oh2, prefix, 0.0), axis=0, keepdims=True)
    cnt_ref[...] = cnt_ref[...] + jnp.sum(c, axis=1, keepdims=True)

    fields = jnp.zeros((ROUTE_FIELDS, tm), F32)
    for idx, val in enumerate((e1.astype(F32), e2.astype(F32), w1, w2, r1, r2)):
        fields = jnp.where(sub == idx, val, fields)
    route_t_ref[...] = fields


def _merge_router(l, pool, attn, x2d, w_out, g_ffn, wr, br, cnt_in, tm):
    t = x2d.shape[0]
    utri = (jnp.arange(tm)[:, None] < jnp.arange(tm)[None, :]).astype(BF16)
    row = lambda i: (i, 0)
    lay = lambda i: (l, 0, 0)
    return pl.pallas_call(
        _merge_router_kernel,
        grid=(t // tm,),
        in_specs=[
            pl.BlockSpec((tm, POOL_W), row),
            pl.BlockSpec((tm, Q_W), row),
            pl.BlockSpec((tm, D_MODEL), row),
            pl.BlockSpec((None, D_MODEL, D_MODEL), lay),
            pl.BlockSpec((None, 1, D_MODEL), lay),
            pl.BlockSpec((None, D_MODEL, LANES), lay),
            pl.BlockSpec((None, 1, LANES), lay),
            pl.BlockSpec((tm, tm), lambda i: (0, 0)),
            pl.BlockSpec((N_EXPERTS, 1), lambda i: (0, 0)),
        ],
        out_specs=[
            pl.BlockSpec((tm, D_MODEL), row),
            pl.BlockSpec((tm, D_MODEL // 2), row),
            pl.BlockSpec((ROUTE_FIELDS, tm), lambda i: (0, i)),
            pl.BlockSpec((N_EXPERTS, 1), lambda i: (0, 0)),
        ],
        out_shape=[
            jax.ShapeDtypeStruct((t, D_MODEL), F32),
            jax.ShapeDtypeStruct((t, D_MODEL // 2), jnp.int32),
            jax.ShapeDtypeStruct((ROUTE_FIELDS, t), F32),
            jax.ShapeDtypeStruct((N_EXPERTS, 1), F32),
        ],
        scratch_shapes=[pltpu.VMEM((tm, D_MODEL), F32), pltpu.VMEM((tm, LANES), F32)],
        compiler_params=pltpu.CompilerParams(
            dimension_semantics=("arbitrary",), vmem_limit_bytes=VMEM_LIMIT),
        name="merge_router",
    )(pool, attn, x2d, w_out, g_ffn, wr, br, utri, cnt_in)


def _moe_kernel(be_ref, rv_ref, nx_ref, sl_ref, first_ref, xd_ref, wg_hbm, wu_hbm, wd_hbm, yd_ref,
                wg_f, wu_f, wd_f, wg_s, wu_s, wd_s, sem, *, layer):
    step = pl.program_id(0)

    def weight_copies(e, s):
        return [pltpu.make_async_copy(w_hbm.at[layer, e], w_f.at[s], sem.at[s, n])
                for n, (w_hbm, w_f) in enumerate(((wg_hbm, wg_f), (wu_hbm, wu_f), (wd_hbm, wd_f)))]

    @pl.when(step == 0)
    def _():
        for s in range(WEIGHT_SLOTS - 1):
            @pl.when(first_ref[s] >= 0)
            def _():
                for c in weight_copies(first_ref[s], s):
                    c.start()

    def enter_expert(i):
        expert, slot = be_ref[i], sl_ref[i]

        @pl.when((i == 0) | (expert != be_ref[jnp.maximum(i - 1, 0)]))
        def _():
            for c in weight_copies(expert, slot):
                c.wait()

            @pl.when(nx_ref[i] >= 0)
            def _():
                for c in weight_copies(nx_ref[i], lax.rem(slot + WEIGHT_SLOTS - 1, WEIGHT_SLOTS)):
                    c.start(priority=1)

            wg_s[...] = wg_f[slot].astype(BF16)
            wu_s[...] = wu_f[slot].astype(BF16)
            wd_s[...] = wd_f[slot].astype(BF16)

    def experts_on(row0, n_rows, rows_valid):
        rows = pl.ds(row0, n_rows)
        row = lax.broadcasted_iota(jnp.int32, (n_rows, D_MODEL // 2), 0)
        x = _unpack_bf16_pairs(jnp.where(row < rows_valid, xd_ref[rows, :], 0)).astype(BF16)
        gate = jnp.dot(x, wg_s[...], preferred_element_type=F32)
        up = jnp.dot(x, wu_s[...], preferred_element_type=F32)
        act = (gate * jax.nn.sigmoid(gate) * up).astype(BF16)
        y = jnp.dot(act, wd_s[...], preferred_element_type=F32)
        yd_ref[rows, :] = _pack_bf16_pairs(y.astype(BF16))

    def experts_ragged(row0, lead_rows, rows_last):
        half = MOE_BM // 2

        @pl.when(rows_last > half)
        def _():
            experts_on(row0, lead_rows + MOE_BM, lead_rows + rows_last)

        @pl.when(rows_last <= half)
        def _():
            experts_on(row0, lead_rows + half, lead_rows + rows_last)
            yd_ref[pl.ds(row0 + lead_rows + half, half), :] = jnp.zeros((half, D_MODEL // 2), jnp.int32)

    def single_block(i, row0):
        enter_expert(i)

        @pl.when(rv_ref[i] > 0)
        def _():
            experts_ragged(row0, 0, rv_ref[i])

        @pl.when(rv_ref[i] <= 0)
        def _():
            yd_ref[pl.ds(row0, MOE_BM), :] = jnp.zeros((MOE_BM, D_MODEL // 2), jnp.int32)

    @pl.when(rv_ref[step * MOE_STEP_BLOCKS] > 0)
    def _():
        for pair in range(MOE_STEP_BLOCKS // 2):
            ia = step * MOE_STEP_BLOCKS + 2 * pair
            ib = ia + 1
            row0 = 2 * pair * MOE_BM
            same = (be_ref[ib] == be_ref[ia]) & (rv_ref[ib] > 0)

            @pl.when(same)
            def _():
                enter_expert(ia)
                experts_ragged(row0, MOE_BM, rv_ref[ib])

            @pl.when(jnp.logical_not(same))
            def _():
                single_block(ia, row0)
                single_block(ib, row0 + MOE_BM)


def _moe_experts(l, block_e, rows_valid, next_e, slot, first_e, xd, w_gate, w_up, w_down):
    n_blocks = xd.shape[0] // MOE_BM
    step_rows = MOE_STEP_BLOCKS * MOE_BM
    row = lambda i, be, rv, nx, sl, fe: (jnp.minimum(i, fe[WEIGHT_SLOTS - 1] - 1), 0)
    return pl.pallas_call(
        functools.partial(_moe_kernel, layer=l),
        grid_spec=pltpu.PrefetchScalarGridSpec(
            num_scalar_prefetch=5,
            grid=(n_blocks // MOE_STEP_BLOCKS,),
            in_specs=[
                pl.BlockSpec((step_rows, D_MODEL // 2), row),
                pl.BlockSpec(memory_space=pl.ANY),
                pl.BlockSpec(memory_space=pl.ANY),
                pl.BlockSpec(memory_space=pl.ANY),
            ],
            out_specs=pl.BlockSpec((step_rows, D_MODEL // 2), row),
            scratch_shapes=[
                pltpu.VMEM((WEIGHT_SLOTS, D_MODEL, EXPERT_FF), F32),
                pltpu.VMEM((WEIGHT_SLOTS, D_MODEL, EXPERT_FF), F32),
                pltpu.VMEM((WEIGHT_SLOTS, EXPERT_FF, D_MODEL), F32),
                pltpu.VMEM((D_MODEL, EXPERT_FF), BF16),
                pltpu.VMEM((D_MODEL, EXPERT_FF), BF16),
                pltpu.VMEM((EXPERT_FF, D_MODEL), BF16),
                pltpu.SemaphoreType.DMA((WEIGHT_SLOTS, 3)),
            ],
        ),
        out_shape=jax.ShapeDtypeStruct((n_blocks * MOE_BM, D_MODEL // 2), jnp.int32),
        compiler_params=pltpu.CompilerParams(
            dimension_semantics=("arbitrary",), vmem_limit_bytes=VMEM_LIMIT),
        name="moe_experts",
    )(block_e, rows_valid, next_e, slot, first_e, xd, w_gate, w_up, w_down)


def _sc_worker_id():
    return lax.axis_index("s") * SC_CORES + lax.axis_index("c")


def _sc_dispatch(hp, hs, dest_p, dest_s, n_rows):
    tp, width = hp.shape
    per_w = tp // SC_WORKERS
    n_ch = per_w // DISP_CH
    n_sw = hs.shape[0] // SAMPLE_CH
    mesh = plsc.VectorSubcoreMesh(core_axis_name="c", subcore_axis_name="s")

    @functools.partial(
        pl.kernel, mesh=mesh,
        out_type=jax.ShapeDtypeStruct((n_rows, width), jnp.int32),
        scratch_types=[
            pltpu.VMEM((2, n_ch, DISP_CH), jnp.int32),
            pltpu.VMEM((2, 1, SAMPLE_CH), jnp.int32),
            pltpu.VMEM((SC_RING, DISP_CH, width), jnp.int32),
            pltpu.SemaphoreType.DMA((SC_RING,)),
            pltpu.SemaphoreType.DMA((SC_RING, 2)),
        ],
        name="sc_dispatch",
    )
    def k(hp_hbm, hs_hbm, dp_hbm, ds_hbm, xd_hbm, idx_v, idxs_v, bufs, rsem, wsem):
        wid = _sc_worker_id()
        base = wid * per_w
        for kk in range(2):
            pltpu.sync_copy(dp_hbm.at[kk, wid], idx_v.at[kk])
        reads = [pltpu.make_async_copy(hp_hbm.at[pl.ds(base + j * DISP_CH, DISP_CH)],
                                       bufs.at[j % SC_RING], rsem.at[j % SC_RING]) for j in range(n_ch)]
        writes = [[pltpu.make_async_copy(bufs.at[j % SC_RING], xd_hbm.at[idx_v.at[kk, j]],
                                         wsem.at[j % SC_RING, kk]) for kk in range(2)] for j in range(n_ch)]
        for j in range(min(SC_RING - 1, n_ch)):
            reads[j].start()
        for j in range(n_ch):
            reads[j].wait()
            for w in writes[j]:
                w.start()
            if j >= 1:
                for w in writes[j - 1]:
                    w.wait()
            if j + SC_RING - 1 < n_ch:
                reads[j + SC_RING - 1].start()
        for w in writes[n_ch - 1]:
            w.wait()

        @pl.when(wid < n_sw)
        def _():
            rows = bufs.at[0, pl.ds(0, SAMPLE_CH)]
            for kk in range(2):
                pltpu.sync_copy(ds_hbm.at[kk, wid], idxs_v.at[kk])
            pltpu.sync_copy(hs_hbm.at[pl.ds(wid * SAMPLE_CH, SAMPLE_CH)], rows)
            for kk in range(2):
                pltpu.sync_copy(rows, xd_hbm.at[idxs_v.at[kk, 0]])

    return k(hp, hs, dest_p, dest_s)


def _sc_sample_gather(yd, dest_s, ts):
    width = yd.shape[1]
    n_sw = ts // SAMPLE_CH
    mesh = plsc.VectorSubcoreMesh(core_axis_name="c", subcore_axis_name="s")

    @functools.partial(
        pl.kernel, mesh=mesh,
        out_type=jax.ShapeDtypeStruct((2, ts, width), yd.dtype),
        scratch_types=[
            pltpu.VMEM((2, 1, SAMPLE_CH), jnp.int32),
            pltpu.VMEM((2, SAMPLE_CH, width), yd.dtype),
        ],
        name="sc_sample_gather",
    )
    def k(yd_hbm, ds_hbm, g_hbm, idxs_v, bufs):
        wid = _sc_worker_id()

        @pl.when(wid < n_sw)
        def _():
            for kk in range(2):
                pltpu.sync_copy(ds_hbm.at[kk, wid], idxs_v.at[kk])
            for kk in range(2):
                pltpu.sync_copy(yd_hbm.at[idxs_v.at[kk, 0]], bufs.at[kk])
                pltpu.sync_copy(bufs.at[kk], g_hbm.at[kk, pl.ds(wid * SAMPLE_CH, SAMPLE_CH)])

    return k(yd, dest_s)


def _sc_combine_gather(yd, dest_p, tp):
    width = yd.shape[1]
    per_w = tp // SC_WORKERS
    n_ch = per_w // COMB_CH
    mesh = plsc.VectorSubcoreMesh(core_axis_name="c", subcore_axis_name="s")

    @functools.partial(
        pl.kernel, mesh=mesh,
        out_type=jax.ShapeDtypeStruct((2, tp, width), yd.dtype),
        scratch_types=[
            pltpu.VMEM((2, n_ch, COMB_CH), jnp.int32),
            pltpu.VMEM((SC_RING, COMB_CH, width), yd.dtype),
            pltpu.SemaphoreType.DMA((SC_RING,)),
            pltpu.SemaphoreType.DMA((SC_RING,)),
        ],
        name="sc_combine_gather",
    )
    def k(yd_hbm, dp_hbm, g_hbm, idx_v, bufs, gsem, wsem):
        wid = _sc_worker_id()
        base = wid * per_w
        for kk in range(2):
            pltpu.sync_copy(dp_hbm.at[kk, wid], idx_v.at[kk])
        items = [(kk, j) for kk in range(2) for j in range(n_ch)]
        n_items = len(items)
        gathers = [pltpu.make_async_copy(yd_hbm.at[idx_v.at[kk, j]], bufs.at[n % SC_RING], gsem.at[n % SC_RING])
                   for n, (kk, j) in enumerate(items)]
        outs = [pltpu.make_async_copy(bufs.at[n % SC_RING], g_hbm.at[kk, pl.ds(base + j * COMB_CH, COMB_CH)],
                                      wsem.at[n % SC_RING]) for n, (kk, j) in enumerate(items)]
        for n in range(min(SC_RING - 1, n_items)):
            gathers[n].start()
        for n in range(n_items):
            gathers[n].wait()
            outs[n].start()
            if n >= 1:
                outs[n - 1].wait()
            if n + SC_RING - 1 < n_items:
                gathers[n + SC_RING - 1].start()
        outs[n_items - 1].wait()

    return k(yd, dest_p)


def _combine_kernel(x1_ref, g_ref, route_t_ref, x2_ref):
    fields = route_t_ref[...]
    tm = fields.shape[1]
    cols = jnp.transpose(jnp.concatenate([fields, jnp.zeros((LANES - ROUTE_FIELDS, tm), F32)], axis=0))
    w1 = cols[:, 2:3]
    w2 = cols[:, 3:4]
    x2_ref[...] = x1_ref[...] + _unpack_bf16_pairs(g_ref[0]) * w1 + _unpack_bf16_pairs(g_ref[1]) * w2


def _combine(x1, g, route_t, row0, tm):
    t = x1.shape[0]
    blk0 = row0 // tm
    row = lambda i: (i, 0)
    return pl.pallas_call(
        _combine_kernel,
        grid=(t // tm,),
        in_specs=[
            pl.BlockSpec((tm, D_MODEL), row),
            pl.BlockSpec((2, tm, D_MODEL // 2), lambda i: (0, blk0 + i, 0)),
            pl.BlockSpec((ROUTE_FIELDS, tm), lambda i: (0, i)),
        ],
        out_specs=pl.BlockSpec((tm, D_MODEL), row),
        out_shape=jax.ShapeDtypeStruct((t, D_MODEL), F32),
        compiler_params=pltpu.CompilerParams(
            dimension_semantics=("arbitrary",), vmem_limit_bytes=VMEM_LIMIT),
        name="combine",
    )(x1, g, route_t)


def _dest_layout(dest, workers, chunk):
    t = dest.shape[1]
    return dest.reshape(2, workers, t // (workers * chunk), chunk)


def _hier_moe(l, h2p, h2s, route_tp, route_ts, counts, w_gate, w_up, w_down):
    tp, ts = h2p.shape[0], h2s.shape[0]
    n_assign = 2 * (tp + ts)
    n_blocks = -(-n_assign // MOE_BM) + N_EXPERTS
    n_blocks = -(-n_blocks // MOE_STEP_BLOCKS) * MOE_STEP_BLOCKS
    pcounts = (counts + MOE_BM - 1) // MOE_BM * MOE_BM
    pends = jnp.cumsum(pcounts)
    poffsets = pends - pcounts
    starts = jnp.arange(n_blocks, dtype=jnp.int32) * MOE_BM
    block_e = jnp.minimum(jnp.sum((pends[None, :] <= starts[:, None]).astype(jnp.int32), axis=1),
                          N_EXPERTS - 1)
    experts = jnp.arange(N_EXPERTS, dtype=jnp.int32)

    def lookup(table, idx):
        return jnp.sum(jnp.where(idx[..., None] == experts, table, 0), axis=-1)

    rows_valid = jnp.clip(lookup(poffsets + counts, block_e) - starts, 0, MOE_BM).astype(jnp.int32)
    used = counts > 0
    last_e = jnp.max(jnp.where(used, jnp.arange(N_EXPERTS, dtype=jnp.int32), 0))
    block_e = jnp.where(rows_valid > 0, block_e, last_e).astype(jnp.int32)
    place = jnp.cumsum(used.astype(jnp.int32)) - 1
    by_place = jnp.sum(jnp.where(used[None, :] & (place[None, :] == experts[:, None]), experts[None, :], 0),
                       axis=1)
    n_used = jnp.sum(used.astype(jnp.int32))

    def at_place(p):
        return jnp.where(p < n_used, lookup(by_place, jnp.minimum(p, N_EXPERTS - 1)), -1).astype(jnp.int32)

    ahead_of = at_place(place + (WEIGHT_SLOTS - 1))
    next_e = lookup(ahead_of, block_e)
    slot = lookup(place % WEIGHT_SLOTS, block_e)
    n_steps_used = -(-(pends[-1] // MOE_BM) // MOE_STEP_BLOCKS)
    first_e = jnp.concatenate([at_place(jnp.arange(WEIGHT_SLOTS - 1, dtype=jnp.int32)),
                               n_steps_used.reshape(1).astype(jnp.int32)])

    def dest_of(route_t):
        return lookup(poffsets, route_t[0:2].astype(jnp.int32)) + route_t[4:6].astype(jnp.int32)

    dest_p, dest_s = dest_of(route_tp), dest_of(route_ts)
    n_sw = ts // SAMPLE_CH
    xd = _sc_dispatch(h2p, h2s, _dest_layout(dest_p, SC_WORKERS, DISP_CH),
                      _dest_layout(dest_s, n_sw, SAMPLE_CH), n_blocks * MOE_BM)
    yd = _moe_experts(l, block_e, rows_valid, next_e, slot, first_e, xd, w_gate, w_up, w_down)
    g_s = _sc_sample_gather(yd, _dest_layout(dest_s, n_sw, SAMPLE_CH), ts)
    g_p = _sc_combine_gather(yd, _dest_layout(dest_p, SC_WORKERS, COMB_CH), tp)
    return g_p, g_s


def kernel(x_prompt, x_sample, state_pool, cache_k_win, cache_v_win, norm_attn_g, w_in, pool_w, pool_scale, q_norm_g, k_norm_g, attn_sinks, w_out, norm_ffn_g, router_group_w, router_group_b, router_expert_w, router_expert_b, w_gate, w_up, w_down):
    n_p, t_p, d = x_prompt.shape
    n_s, t_s, _ = x_sample.shape
    depth = w_in.shape[0]
    lw_s = cache_k_win.shape[2]
    assert t_s == 1 and lw_s == WINDOW and d == D_MODEL
    assert t_p % TM_PROJ == 0 and t_p >= WINDOW

    seg = jnp.arange(256) // HEAD_DIM
    bd = jnp.where(seg[:, None] == seg[None, :], 1.0 / HEAD_DIM, 0.0).astype(BF16)
    slopes = jnp.exp2(-8.0 * jnp.arange(1, N_HEADS + 1, dtype=F32) / N_HEADS)
    bias_p = _prompt_bias_t()
    dist_s = (WINDOW - 1) - jnp.arange(WINDOW, dtype=F32)
    bias_s = -LOG2E * slopes[:, None] * dist_s[None, :]

    wp = jnp.zeros((depth, 2, 256, 256), F32)
    for p in range(2):
        wp = wp.at[:, p, :POOL_GC, :POOL_GC].set(pool_w[:, 2 * p])
        wp = wp.at[:, p, POOL_GC:, POOL_GC:].set(pool_w[:, 2 * p + 1])
    assert GROUP_LANE0 == N_EXPERTS
    lane_pad = LANES - N_EXPERTS - N_EXPERT_GROUPS
    wr = jnp.concatenate([router_expert_w, router_group_w, jnp.zeros((depth, D_MODEL, lane_pad), F32)], axis=-1)
    br = jnp.concatenate([router_expert_b, router_group_b, jnp.zeros((depth, lane_pad), F32)],
                         axis=-1).reshape(depth, 1, LANES)
    lp = dict(
        w_in=w_in.astype(BF16),
        w_out=w_out.astype(BF16),
        g_attn=norm_attn_g.reshape(depth, 1, D_MODEL),
        g_ffn=norm_ffn_g.reshape(depth, 1, D_MODEL),
        qg=(jnp.tile(q_norm_g, (1, N_HEADS)) * (ATTN_SCALE * LOG2E)).reshape(depth, 1, Q_W),
        kg=jnp.tile(k_norm_g, (1, N_KV_HEADS)).reshape(depth, 1, KV_W),
        wp=wp.astype(BF16),
        ps=pool_scale.reshape(depth, 1, POOL_W),
        wr=wr.astype(BF16),
        br=br,
        state=state_pool,
        ck=cache_k_win.reshape(depth, n_s, lw_s, KV_W),
        cv=cache_v_win.reshape(depth, n_s, lw_s, KV_W),
    )

    xp = x_prompt.reshape(n_p * t_p, D_MODEL)
    xs = x_sample.reshape(n_s, D_MODEL)
    lw_p = min(WINDOW, t_p)
    pool_p, kp_new, vp_new = [], [], []
    sample_state = [lp["state"], lp["ck"], lp["cv"]]
    zero_cnt = jnp.zeros((N_EXPERTS, 1), F32)
    pending = None
    for l in range(depth):
        sinks = attn_sinks[l] * LOG2E
        pool_so, attn_so, *sample_state = _sample_mixer(
            l, depth, xs, lp["g_attn"], lp["w_in"], lp["qg"], lp["kg"], bd, lp["wp"], lp["ps"],
            *sample_state, sinks.reshape(N_HEADS, 1), bias_s, PAST_LEN)
        x1s, h2s, route_ts, cnt_s = _merge_router(
            l, pool_so, attn_so, xs, lp["w_out"], lp["g_ffn"], lp["wr"], lp["br"], zero_cnt, n_s)
        outs = _proj_pool_prompt(
            l, xp if pending is None else pending, n_p, t_p,
            lp["g_attn"], lp["w_in"], lp["qg"], lp["kg"], bd, lp["wp"], lp["ps"])
        if pending is not None:
            xp, outs = outs[0], outs[1:]
        pool_o, q, k, vt, utail, ktail, vtail = outs
        if l == depth - 1:
            ks_out = sample_state[1].reshape(depth, n_s, lw_s, N_KV_HEADS, HEAD_DIM)
            ks_out, vs_in, q = lax.optimization_barrier((ks_out, sample_state[2], q))
            vs_out = vs_in.reshape(depth, n_s, lw_s, N_KV_HEADS, HEAD_DIM)
        attn_o = _attn_prompt(q, k, vt, bias_p, sinks, n_p, t_p)
        if l == depth - 1:
            vs_out, attn_o = lax.optimization_barrier((vs_out, attn_o))
        x1p, h2p, route_tp, cnt_all = _merge_router(
            l, pool_o, attn_o, xp, lp["w_out"], lp["g_ffn"], lp["wr"], lp["br"], cnt_s, TM_MERGE)
        pool_p.append(utail[:, 16 - POOL_STATE:, :])
        kp_new.append(ktail)
        vp_new.append(vtail)
        counts = cnt_all[:, 0].astype(jnp.int32)
        g_p, g_s = _hier_moe(l, h2p, h2s, route_tp, route_ts, counts, w_gate, w_up, w_down)
        xs = _combine(x1s, g_s, route_ts, 0, n_s)
        pending = (x1p, g_p, route_tp)
    xp = _combine(*pending, 0, TM_MERGE)
    return (xp.reshape(n_p, t_p, D_MODEL), xs.reshape(n_s, t_s, D_MODEL),
            jnp.stack(pool_p),
            jnp.stack(kp_new).reshape(depth, n_p, lw_p, N_KV_HEADS, HEAD_DIM),
            jnp.stack(vp_new).reshape(depth, n_p, lw_p, N_KV_HEADS, HEAD_DIM),
            sample_state[0], ks_out, vs_out)
```

```python
import functools

import jax
import jax.numpy as jnp
from jax import lax
from jax.experimental import pallas as pl
from jax.experimental.pallas import tpu as pltpu
from jax.experimental.pallas import tpu_sc as plsc

D_MODEL = 1024
POOL_W = 512
POOL_WINDOWS = (2, 4, 8, 16)
POOL_GC = 128
POOL_STATE = 15
HEAD_DIM = 64
N_HEADS = 8
N_KV_HEADS = 2
GQA_GROUP = 4
Q_W = 512
KV_W = 128
D_IN = POOL_W + Q_W + 2 * KV_W
WINDOW = 128
ATTN_SCALE = HEAD_DIM ** -0.5
LOG2E = 1.4426950408889634
N_EXPERT_GROUPS = 4
EXPERTS_PER_GROUP = 8
N_EXPERTS = 32
EXPERT_FF = 512
EPS = 1e-6
PAST_LEN = 16384

LANES = 128
HALO = 32
TM_PROJ = 1024
TM_MERGE = 1024
MERGE_CHUNKS = 4
X_RING = 3
ATTN_QB = 16
MOE_BM = 256
MOE_STEP_BLOCKS = 4
WEIGHT_SLOTS = 3
GROUP_LANE0 = 32
ROUTE_FIELDS = 8
SC_CORES = 2
SC_SUBCORES = 16
SC_WORKERS = SC_CORES * SC_SUBCORES
DISP_CH = 64
COMB_CH = 64
SAMPLE_CH = 32
SC_RING = 3
VMEM_LIMIT = 48 * 1024 * 1024

BF16 = jnp.bfloat16
F32 = jnp.float32


def _pack_bf16_pairs(h):
    w = h.shape[1] // 2
    hi = lax.bitcast_convert_type(h[:, :w].astype(F32), jnp.uint32)
    lo = lax.bitcast_convert_type(h[:, w:].astype(F32), jnp.uint32)
    return lax.bitcast_convert_type(hi | (lo >> 16), jnp.int32)


def _unpack_bf16_pairs(words):
    u = lax.bitcast_convert_type(words, jnp.uint32)
    hi = lax.bitcast_convert_type(u & jnp.uint32(0xFFFF0000), F32)
    lo = lax.bitcast_convert_type(u << 16, F32)
    return jnp.concatenate([hi, lo], axis=-1)


def _segment_mean_sq(a, bd):
    w = a.shape[1]
    return jnp.dot((a * a).astype(BF16), bd[:w, :w], preferred_element_type=F32)


def _rms_bf16(x, g):
    ms = jnp.mean(x * x, axis=-1, keepdims=True)
    return (x * lax.rsqrt(ms + EPS) * g).astype(BF16)


def _qk_norm(q, k, qg, kg, bd):
    qn = []
    for c in range(Q_W // 256):
        qc = q[:, c * 256:(c + 1) * 256]
        qn.append(qc * lax.rsqrt(_segment_mean_sq(qc, bd) + EPS))
    qn = jnp.concatenate(qn, axis=-1) * qg
    kn = k * lax.rsqrt(_segment_mean_sq(k, bd) + EPS) * kg
    return qn, kn


def _project(x, g, w_in, qg, kg, bd):
    z = jnp.dot(_rms_bf16(x, g), w_in, preferred_element_type=F32)
    u = z[:, :POOL_W]
    q = z[:, POOL_W:POOL_W + Q_W]
    k = z[:, POOL_W + Q_W:POOL_W + Q_W + KV_W]
    v = z[:, POOL_W + Q_W + KV_W:]
    qn, kn = _qk_norm(q, k, qg, kg, bd)
    return u, qn, kn, v


def _pool_project(d_groups, wp_ref, ps):
    outs = []
    for p in range(2):
        dp = jnp.concatenate([d_groups[2 * p], d_groups[2 * p + 1]], axis=-1).astype(BF16)
        y = jnp.dot(dp, wp_ref[p], preferred_element_type=F32)
        outs.append(y * ps[:, p * 256:(p + 1) * 256])
    return jnp.concatenate(outs, axis=-1)


def _proj_pool_kernel(x_ref, g_ref, win_ref, qg_ref, kg_ref, bd_ref, wp_ref, ps_ref,
                      pool_ref, q_ref, k_ref, vt_ref, utail_ref, ktail_ref, vtail_ref,
                      ext_ref, sa_ref, sb_ref, zq_ref, *, tm, n_j):
    j = pl.program_id(1)

    @pl.when(j == 0)
    def _():
        ext_ref[0:HALO, :] = jnp.zeros((HALO, POOL_W), F32)

    r = tm + HALO
    h = _rms_bf16(x_ref[...], g_ref[...])
    ext_ref[HALO:r, :] = jnp.dot(h, win_ref[:, 0:POOL_W], preferred_element_type=F32)
    zq_ref[...] = jnp.dot(h, win_ref[:, POOL_W:], preferred_element_type=F32)
    u = ext_ref[HALO:r, :]
    sa_ref[8:r, :] = ext_ref[8:r, :] + ext_ref[7:r - 1, :]
    sb_ref[16:r, 128:] = sa_ref[16:r, 128:] + sa_ref[14:r - 2, 128:]
    sa_ref[24:r, 256:] = sb_ref[24:r, 256:] + sb_ref[20:r - 4, 256:]
    sb_ref[32:r, 384:] = sa_ref[32:r, 384:] + sa_ref[24:r - 8, 384:]
    pos1 = j * tm + lax.broadcasted_iota(jnp.int32, (tm, POOL_GC), 0) + 1
    sums = (sa_ref, sb_ref, sa_ref, sb_ref)
    d_groups = []
    for gi, w in enumerate(POOL_WINDOWS):
        sl = slice(gi * POOL_GC, (gi + 1) * POOL_GC)
        cnt = jnp.minimum(pos1, w).astype(F32)
        d_groups.append(sums[gi][HALO:r, sl] / cnt - u[:, sl])
    pool_ref[...] = _pool_project(d_groups, wp_ref, ps_ref[...]).astype(BF16)
    ext_ref[16:HALO, :] = ext_ref[tm + 16:r, :]

    qn, kn = _qk_norm(zq_ref[:, 0:Q_W], zq_ref[:, Q_W:Q_W + KV_W], qg_ref[...], kg_ref[...], bd_ref[...])
    v = zq_ref[:, Q_W + KV_W:]
    q_ref[...] = qn.astype(BF16)
    k_ref[...] = kn.astype(BF16)
    vt_ref[...] = jnp.transpose(v).astype(BF16)

    @pl.when(j == n_j - 1)
    def _():
        utail_ref[...] = u[tm - 16:, :]
        ktail_ref[...] = kn[tm - WINDOW:, :]
        vtail_ref[...] = v[tm - WINDOW:, :]


def _proj_pool_combine_kernel(x1_ref, gath_ref, route_ref, *rest, tm, n_j):
    x2_ref = rest[7]
    _combine_kernel(x1_ref, gath_ref, route_ref, x2_ref)
    _proj_pool_kernel(x2_ref, *rest[:7], *rest[8:], tm=tm, n_j=n_j)


def _proj_pool_prompt(l, x_in, n_seq, seq, g_attn, w_in, qg, kg, bd, wp, ps):
    tm = TM_PROJ
    n_j = seq // tm
    t = n_seq * seq
    row = lambda b, j: (b * n_j + j, 0)
    lay = lambda b, j: (l, 0, 0)
    fused = isinstance(x_in, tuple)
    if fused:
        kern = _proj_pool_combine_kernel
        x_args = list(x_in)
        x_specs = [pl.BlockSpec((tm, D_MODEL), row),
                   pl.BlockSpec((2, tm, D_MODEL // 2), lambda b, j: (0, b * n_j + j, 0)),
                   pl.BlockSpec((ROUTE_FIELDS, tm), lambda b, j: (0, b * n_j + j))]
        x_out_specs = [pl.BlockSpec((tm, D_MODEL), row)]
        x_out_shape = [jax.ShapeDtypeStruct((t, D_MODEL), F32)]
    else:
        kern = _proj_pool_kernel
        x_args = [x_in]
        x_specs = [pl.BlockSpec((tm, D_MODEL), row)]
        x_out_specs, x_out_shape = [], []
    return pl.pallas_call(
        functools.partial(kern, tm=tm, n_j=n_j),
        grid=(n_seq, n_j),
        in_specs=x_specs + [
            pl.BlockSpec((None, 1, D_MODEL), lay),
            pl.BlockSpec((None, D_MODEL, D_IN), lay),
            pl.BlockSpec((None, 1, Q_W), lay),
            pl.BlockSpec((None, 1, KV_W), lay),
            pl.BlockSpec((256, 256), lambda b, j: (0, 0)),
            pl.BlockSpec((None, 2, 256, 256), lambda b, j: (l, 0, 0, 0)),
            pl.BlockSpec((None, 1, POOL_W), lay),
        ],
        out_specs=x_out_specs + [
            pl.BlockSpec((tm, POOL_W), row),
            pl.BlockSpec((tm, Q_W), row),
            pl.BlockSpec((tm, KV_W), row),
            pl.BlockSpec((KV_W, tm), lambda b, j: (0, b * n_j + j)),
            pl.BlockSpec((None, 16, POOL_W), lambda b, j: (b, 0, 0)),
            pl.BlockSpec((None, WINDOW, KV_W), lambda b, j: (b, 0, 0)),
            pl.BlockSpec((None, WINDOW, KV_W), lambda b, j: (b, 0, 0)),
        ],
        out_shape=x_out_shape + [
            jax.ShapeDtypeStruct((t, POOL_W), BF16),
            jax.ShapeDtypeStruct((t, Q_W), BF16),
            jax.ShapeDtypeStruct((t, KV_W), BF16),
            jax.ShapeDtypeStruct((KV_W, t), BF16),
            jax.ShapeDtypeStruct((n_seq, 16, POOL_W), F32),
            jax.ShapeDtypeStruct((n_seq, WINDOW, KV_W), F32),
            jax.ShapeDtypeStruct((n_seq, WINDOW, KV_W), F32),
        ],
        scratch_shapes=[pltpu.VMEM((tm + HALO, POOL_W), F32)] * 3 + [pltpu.VMEM((tm, Q_W + 2 * KV_W), F32)],
        compiler_params=pltpu.CompilerParams(
            dimension_semantics=("arbitrary", "arbitrary"), vmem_limit_bytes=VMEM_LIMIT),
        name="proj_pool_prompt",
    )(*x_args, g_attn, w_in, qg, kg, bd, wp, ps)


def _attn_kernel(sink_ref, q_ref, kp_ref, kc_ref, vtp_ref, vtc_ref, bias_ref, o_ref, s_ref):
    j = pl.program_id(1)
    kk_all = jnp.concatenate([kp_ref[...], kc_ref[...]], axis=0)
    vt_all = jnp.concatenate([vtp_ref[...], vtc_ref[...]], axis=1)
    from_prev = (lax.broadcasted_iota(jnp.int32, (WINDOW, WINDOW), 0)
                 > lax.broadcasted_iota(jnp.int32, (WINDOW, WINDOW), 1))
    units = [(blk, kv) for blk in range(ATTN_QB) for kv in range(N_KV_HEADS)]

    def scores(n):
        blk, kv = units[n]
        q = q_ref[blk * WINDOW:(blk + 1) * WINDOW, :]
        kk = kk_all[blk * WINDOW:(blk + 2) * WINDOW, kv * HEAD_DIM:(kv + 1) * HEAD_DIM]
        heads = range(kv * GQA_GROUP, (kv + 1) * GQA_GROUP)
        q_rows = jnp.concatenate([q[:, h * HEAD_DIM:(h + 1) * HEAD_DIM] for h in heads], axis=0)
        s_ref[n % 3] = lax.dot_general(kk, q_rows, (((1,), (1,)), ((), ())), preferred_element_type=F32)

    scores(0)
    scores(1)
    outs = []
    for n, (blk, kv) in enumerate(units):
        if n + 2 < len(units):
            scores(n + 2)
        vt_kv = vt_all[kv * HEAD_DIM:(kv + 1) * HEAD_DIM, blk * WINDOW:(blk + 2) * WINDOW]
        variant = jnp.minimum(j, 1) if blk == 0 else 1
        for g in range(GQA_GROUP):
            h = kv * GQA_GROUP + g
            s = jnp.where(from_prev, s_ref[n % 3, 0:WINDOW, g * WINDOW:(g + 1) * WINDOW],
                          s_ref[n % 3, WINDOW:, g * WINDOW:(g + 1) * WINDOW]) + bias_ref[variant, h]
            sink = sink_ref[h]
            m = jnp.maximum(jnp.max(s, axis=0, keepdims=True), sink)
            p = jnp.exp2(s - m)
            denom = jnp.sum(p, axis=0, keepdims=True) + jnp.exp2(sink - m)
            p_keys = jnp.concatenate([jnp.where(from_prev, p, 0.0), jnp.where(from_prev, 0.0, p)], axis=0)
            o_t = jnp.dot(vt_kv, p_keys.astype(BF16), preferred_element_type=F32)
            outs.append(o_t / denom)
        if kv == N_KV_HEADS - 1:
            o_ref[blk * WINDOW:(blk + 1) * WINDOW, :] = jnp.transpose(jnp.concatenate(outs, axis=0)).astype(BF16)
            outs = []


def _attn_prompt(q, k, vt, bias_t, sinks, n_seq, seq):
    tq = ATTN_QB * WINDOW
    nj = seq // tq
    t = n_seq * seq
    cur = lambda b, j: (b * nj + j, 0)
    prev = lambda b, j: (jnp.maximum((b * nj + j) * ATTN_QB - 1, 0), 0)
    cur_t = lambda b, j: (0, b * nj + j)
    prev_t = lambda b, j: (0, jnp.maximum((b * nj + j) * ATTN_QB - 1, 0))
    return pl.pallas_call(
        _attn_kernel,
        grid=(n_seq, nj),
        in_specs=[
            pl.BlockSpec(memory_space=pltpu.SMEM),
            pl.BlockSpec((tq, Q_W), cur),
            pl.BlockSpec((WINDOW, KV_W), prev),
            pl.BlockSpec((tq, KV_W), cur),
            pl.BlockSpec((KV_W, WINDOW), prev_t),
            pl.BlockSpec((KV_W, tq), cur_t),
            pl.BlockSpec((2, N_HEADS, WINDOW, WINDOW), lambda b, j: (0, 0, 0, 0)),
        ],
        out_specs=pl.BlockSpec((tq, Q_W), cur),
        out_shape=jax.ShapeDtypeStruct((t, Q_W), BF16),
        scratch_shapes=[pltpu.VMEM((3, 2 * WINDOW, GQA_GROUP * WINDOW), F32)],
        compiler_params=pltpu.CompilerParams(
            dimension_semantics=("arbitrary", "arbitrary"), vmem_limit_bytes=VMEM_LIMIT),
        name="attn_prompt",
    )(sinks, q, k, k, vt, vt, bias_t)


def _prompt_bias_t():
    r = jnp.arange(WINDOW, dtype=jnp.int32)[None, :]
    c = jnp.arange(WINDOW, dtype=jnp.int32)[:, None]
    from_prev = c > r
    dist = r - c + jnp.where(from_prev, WINDOW, 0)
    slopes = jnp.exp2(-8.0 * jnp.arange(1, N_HEADS + 1, dtype=F32) / N_HEADS)
    later = -LOG2E * slopes[:, None, None] * dist.astype(F32)[None]
    first = jnp.where(from_prev[None], -jnp.inf, later)
    return jnp.stack([first, later])


def _sample_kernel(x_ref, g_ref, win_ref, qg_ref, kg_ref, bd_ref, wp_ref, ps_ref,
                   st_ref, ck_ref, cv_ref, sink_ref, bias_ref, perm_ref,
                   pool_ref, attn_ref, pst_ref, kc_ref, vc_ref, *, ns, pos0):
    u, qn, kn, v = _project(x_ref[...], g_ref[...], win_ref[...], qg_ref[...], kg_ref[...], bd_ref[...])
    pst_ref[:, 0:POOL_STATE - 1, :] = st_ref[:, 1:POOL_STATE, :]
    kc_ref[:, 0:WINDOW - 1, :] = ck_ref[:, 1:WINDOW, :]
    vc_ref[:, 0:WINDOW - 1, :] = cv_ref[:, 1:WINDOW, :]
    for n in range(ns):
        pst_ref[n, POOL_STATE - 1:POOL_STATE, :] = u[n:n + 1, :]
        kc_ref[n, WINDOW - 1:WINDOW, :] = kn[n:n + 1, :]
        vc_ref[n, WINDOW - 1:WINDOW, :] = v[n:n + 1, :]

    d_groups = []
    for gi, w in enumerate(POOL_WINDOWS):
        lo = gi * POOL_GC
        acc = u[:, lo:lo + POOL_GC]
        for back in range(1, w):
            acc = acc + st_ref[:, POOL_STATE - back, lo:lo + POOL_GC]
        d_groups.append(acc / float(min(pos0 + 1, w)) - u[:, lo:lo + POOL_GC])
    pool_ref[...] = _pool_project(d_groups, wp_ref, ps_ref[...]).astype(BF16)

    zeros = jnp.zeros((ns, HEAD_DIM), F32)
    stacked = []
    for h in range(N_HEADS):
        piece = qn[:, h * HEAD_DIM:(h + 1) * HEAD_DIM]
        pair = [piece, zeros] if h < GQA_GROUP else [zeros, piece]
        stacked.append(jnp.concatenate(pair, axis=-1))
    q_hn = jnp.concatenate(stacked, axis=0).astype(BF16)
    q_nh = jnp.dot(perm_ref[0], q_hn, preferred_element_type=F32).astype(BF16)

    keys = kc_ref[...].reshape(ns * WINDOW, KV_W).astype(BF16)
    vals = vc_ref[...].reshape(ns * WINDOW, KV_W).astype(BF16)
    s_all = lax.dot_general(q_nh, keys, (((1,), (1,)), ((), ())), preferred_element_type=F32)
    sink = sink_ref[...]
    bias = bias_ref[...]
    zero_blk = jnp.zeros((N_HEADS, WINDOW), F32)
    p_rows = []
    for n in range(ns):
        s = s_all[n * N_HEADS:(n + 1) * N_HEADS, n * WINDOW:(n + 1) * WINDOW] + bias
        m = jnp.maximum(jnp.max(s, axis=-1, keepdims=True), sink)
        p = jnp.exp2(s - m)
        denom = jnp.sum(p, axis=-1, keepdims=True) + jnp.exp2(sink - m)
        p_rows.append(jnp.concatenate([zero_blk] * n + [p / denom] + [zero_blk] * (ns - 1 - n), axis=-1))
    p_blockdiag = jnp.concatenate(p_rows, axis=0).astype(BF16)
    o_nh = jnp.dot(p_blockdiag, vals, preferred_element_type=F32).astype(BF16)
    o_hn = jnp.dot(perm_ref[1], o_nh, preferred_element_type=F32)
    pieces = []
    for h in range(N_HEADS):
        kv = h // GQA_GROUP
        pieces.append(o_hn[h * ns:(h + 1) * ns, kv * HEAD_DIM:(kv + 1) * HEAD_DIM])
    attn_ref[...] = jnp.concatenate(pieces, axis=-1).astype(BF16)


def _sample_mixer(l, depth, xs, g_attn, w_in, qg, kg, bd, wp, ps, state, ck, cv, sink8, bias_s, pos0):
    n = xs.shape[0]
    ns = 32
    row = lambda i: (i, 0)
    lay = lambda i: (l, 0, 0)
    src = jnp.arange(ns * N_HEADS)
    perm = (((src % N_HEADS) * ns + src // N_HEADS)[:, None] == src[None, :]).astype(BF16)
    perms = jnp.stack([perm, perm.T])
    return pl.pallas_call(
        functools.partial(_sample_kernel, ns=ns, pos0=pos0),
        grid=(n // ns,),
        input_output_aliases={8: 2, 9: 3, 10: 4},
        in_specs=[
            pl.BlockSpec((ns, D_MODEL), row),
            pl.BlockSpec((None, 1, D_MODEL), lay),
            pl.BlockSpec((None, D_MODEL, D_IN), lay),
            pl.BlockSpec((None, 1, Q_W), lay),
            pl.BlockSpec((None, 1, KV_W), lay),
            pl.BlockSpec((256, 256), lambda i: (0, 0)),
            pl.BlockSpec((None, 2, 256, 256), lambda i: (l, 0, 0, 0)),
            pl.BlockSpec((None, 1, POOL_W), lay),
            pl.BlockSpec((None, ns, POOL_STATE, POOL_W), lambda i: (l, i, 0, 0)),
            pl.BlockSpec((None, ns, WINDOW, KV_W), lambda i: (l, i, 0, 0)),
            pl.BlockSpec((None, ns, WINDOW, KV_W), lambda i: (l, i, 0, 0)),
            pl.BlockSpec((N_HEADS, 1), lambda i: (0, 0)),
            pl.BlockSpec((N_HEADS, WINDOW), lambda i: (0, 0)),
            pl.BlockSpec((2, ns * N_HEADS, ns * N_HEADS), lambda i: (0, 0, 0)),
        ],
        out_specs=[
            pl.BlockSpec((ns, POOL_W), row),
            pl.BlockSpec((ns, Q_W), row),
            pl.BlockSpec((None, ns, POOL_STATE, POOL_W), lambda i: (l, i, 0, 0)),
            pl.BlockSpec((None, ns, WINDOW, KV_W), lambda i: (l, i, 0, 0)),
            pl.BlockSpec((None, ns, WINDOW, KV_W), lambda i: (l, i, 0, 0)),
        ],
        out_shape=[
            jax.ShapeDtypeStruct((n, POOL_W), BF16),
            jax.ShapeDtypeStruct((n, Q_W), BF16),
            jax.ShapeDtypeStruct((depth, n, POOL_STATE, POOL_W), F32),
            jax.ShapeDtypeStruct((depth, n, WINDOW, KV_W), F32),
            jax.ShapeDtypeStruct((depth, n, WINDOW, KV_W), F32),
        ],
        compiler_params=pltpu.CompilerParams(
            dimension_semantics=("arbitrary",), vmem_limit_bytes=VMEM_LIMIT),
        name="sample_mixer",
    )(xs, g_attn, w_in, qg, kg, bd, wp, ps, state, ck, cv, sink8, bias_s, perms)


def _merge_router_kernel(pool_ref, attn_ref, x_hbm, wout_ref, g_ref, wr_ref, br_ref, utri_ref, cin_ref,
                         x1_ref, h2_ref, route_t_ref, cnt_ref, y_ref, lg_ref, xbuf, xsem, *, n_steps):
    i = pl.program_id(0)
    tm = pool_ref.shape[0]

    def x_copy(step, slot):
        return pltpu.make_async_copy(x_hbm.at[pl.ds(step * tm, tm), :], xbuf.at[slot], xsem.at[slot])

    @pl.when(i == 0)
    def _():
        cnt_ref[...] = cin_ref[...]
        for s in range(min(X_RING - 1, n_steps)):
            x_copy(s, s).start()

    @pl.when(i + X_RING - 1 < n_steps)
    def _():
        x_copy(i + X_RING - 1, lax.rem(i + X_RING - 1, X_RING)).start()

    rc = tm // MERGE_CHUNKS
    chunks = [slice(ci * rc, (ci + 1) * rc) for ci in range(MERGE_CHUNKS)]
    for rows in chunks:
        y_ref[rows, :] = (jnp.dot(pool_ref[rows, :], wout_ref[0:POOL_W, :], preferred_element_type=F32)
                          + jnp.dot(attn_ref[rows, :], wout_ref[POOL_W:, :], preferred_element_type=F32))
    slot = lax.rem(i, X_RING)
    x_copy(i, slot).wait()
    for rows in chunks:
        x1 = xbuf[slot, rows, :] + y_ref[rows, :]
        x1_ref[rows, :] = x1
        h2 = _rms_bf16(x1, g_ref[...])
        h2_ref[rows, :] = _pack_bf16_pairs(h2)
        lg_ref[rows, :] = jnp.dot(h2, wr_ref[...], preferred_element_type=F32) + br_ref[...]
    logits = lg_ref[...]

    lt = jnp.transpose(logits)
    sub = lax.broadcasted_iota(jnp.int32, (EXPERTS_PER_GROUP, tm), 0)
    neg = -jnp.inf
    big = jnp.int32(EXPERTS_PER_GROUP)
    gl = jnp.where(sub < N_EXPERT_GROUPS, lt[GROUP_LANE0:GROUP_LANE0 + EXPERTS_PER_GROUP, :], neg)
    gmax = jnp.max(gl, axis=0, keepdims=True)
    grp = jnp.min(jnp.where(gl == gmax, sub, big), axis=0, keepdims=True)
    g_w = 1.0 / jnp.sum(jnp.exp(gl - gmax), axis=0, keepdims=True)
    el = lt[(N_EXPERT_GROUPS - 1) * EXPERTS_PER_GROUP:N_EXPERT_GROUPS * EXPERTS_PER_GROUP, :]
    for gi in range(N_EXPERT_GROUPS - 2, -1, -1):
        el = jnp.where(grp == gi, lt[gi * EXPERTS_PER_GROUP:(gi + 1) * EXPERTS_PER_GROUP, :], el)
    v1 = jnp.max(el, axis=0, keepdims=True)
    i1 = jnp.min(jnp.where(el == v1, sub, big), axis=0, keepdims=True)
    el2 = jnp.where(sub == i1, neg, el)
    v2 = jnp.max(el2, axis=0, keepdims=True)
    i2 = jnp.min(jnp.where(el2 == v2, sub, big), axis=0, keepdims=True)
    e21 = jnp.exp(v2 - v1)
    w1 = g_w / (1.0 + e21)
    w2 = g_w * e21 / (1.0 + e21)
    e1 = grp * EXPERTS_PER_GROUP + i1
    e2 = grp * EXPERTS_PER_GROUP + i2

    esub = lax.broadcasted_iota(jnp.int32, (N_EXPERTS, tm), 0)
    oh1 = esub == e1
    oh2 = esub == e2
    c = jnp.where(oh1 | oh2, 1.0, 0.0)
    prefix = jnp.dot(c.astype(BF16), utri_ref[...], preferred_element_type=F32) + cnt_ref[...]
    r1 = jnp.sum(jnp.where(oh1, prefix, 0.0), axis=0, keepdims=True)
    r2 = jnp.sum(jnp.where(oh2, prefix, 0.0), axis=0, keepdims=True)
    cnt_ref[...] = cnt_ref[...] + jnp.sum(c, axis=1, keepdims=True)

    fields = jnp.zeros((ROUTE_FIELDS, tm), F32)
    for idx, val in enumerate((e1.astype(F32), e2.astype(F32), w1, w2, r1, r2)):
        fields = jnp.where(sub == idx, val, fields)
    route_t_ref[...] = fields


def _merge_router(l, pool, attn, x2d, w_out, g_ffn, wr, br, cnt_in, tm):
    t = x2d.shape[0]
    utri = (jnp.arange(tm)[:, None] < jnp.arange(tm)[None, :]).astype(BF16)
    row = lambda i: (i, 0)
    lay = lambda i: (l, 0, 0)
    return pl.pallas_call(
        functools.partial(_merge_router_kernel, n_steps=t // tm),
        grid=(t // tm,),
        in_specs=[
            pl.BlockSpec((tm, POOL_W), row),
            pl.BlockSpec((tm, Q_W), row),
            pl.BlockSpec(memory_space=pl.ANY),
            pl.BlockSpec((None, D_MODEL, D_MODEL), lay),
            pl.BlockSpec((None, 1, D_MODEL), lay),
            pl.BlockSpec((None, D_MODEL, LANES), lay),
            pl.BlockSpec((None, 1, LANES), lay),
            pl.BlockSpec((tm, tm), lambda i: (0, 0)),
            pl.BlockSpec((N_EXPERTS, 1), lambda i: (0, 0)),
        ],
        out_specs=[
            pl.BlockSpec((tm, D_MODEL), row),
            pl.BlockSpec((tm, D_MODEL // 2), row),
            pl.BlockSpec((ROUTE_FIELDS, tm), lambda i: (0, i)),
            pl.BlockSpec((N_EXPERTS, 1), lambda i: (0, 0)),
        ],
        out_shape=[
            jax.ShapeDtypeStruct((t, D_MODEL), F32),
            jax.ShapeDtypeStruct((t, D_MODEL // 2), jnp.int32),
            jax.ShapeDtypeStruct((ROUTE_FIELDS, t), F32),
            jax.ShapeDtypeStruct((N_EXPERTS, 1), F32),
        ],
        scratch_shapes=[pltpu.VMEM((tm, D_MODEL), F32), pltpu.VMEM((tm, LANES), F32),
                        pltpu.VMEM((X_RING, tm, D_MODEL), F32), pltpu.SemaphoreType.DMA((X_RING,))],
        compiler_params=pltpu.CompilerParams(
            dimension_semantics=("arbitrary",), vmem_limit_bytes=VMEM_LIMIT),
        name="merge_router",
    )(pool, attn, x2d, w_out, g_ffn, wr, br, utri, cnt_in)


def _moe_kernel(be_ref, rv_ref, nx_ref, sl_ref, first_ref, xd_ref, wg_hbm, wu_hbm, wd_hbm, yd_ref,
                wg_f, wu_f, wd_f, wg_s, wu_s, wd_s, sem, *, layer):
    step = pl.program_id(0)

    def weight_copies(e, s):
        return [pltpu.make_async_copy(w_hbm.at[layer, e], w_f.at[s], sem.at[s, n])
                for n, (w_hbm, w_f) in enumerate(((wg_hbm, wg_f), (wu_hbm, wu_f), (wd_hbm, wd_f)))]

    @pl.when(step == 0)
    def _():
        for s in range(WEIGHT_SLOTS - 1):
            @pl.when(first_ref[s] >= 0)
            def _():
                for c in weight_copies(first_ref[s], s):
                    c.start()

    def enter_expert(i):
        expert, slot = be_ref[i], sl_ref[i]

        @pl.when((i == 0) | (expert != be_ref[jnp.maximum(i - 1, 0)]))
        def _():
            for c in weight_copies(expert, slot):
                c.wait()

            @pl.when(nx_ref[i] >= 0)
            def _():
                for c in weight_copies(nx_ref[i], lax.rem(slot + WEIGHT_SLOTS - 1, WEIGHT_SLOTS)):
                    c.start(priority=1)

            wg_s[...] = wg_f[slot].astype(BF16)
            wu_s[...] = wu_f[slot].astype(BF16)
            wd_s[...] = wd_f[slot].astype(BF16)

    def experts_on(row0, n_rows, rows_valid):
        rows = pl.ds(row0, n_rows)
        row = lax.broadcasted_iota(jnp.int32, (n_rows, D_MODEL // 2), 0)
        x = _unpack_bf16_pairs(jnp.where(row < rows_valid, xd_ref[rows, :], 0)).astype(BF16)
        gate = jnp.dot(x, wg_s[...], preferred_element_type=F32)
        up = jnp.dot(x, wu_s[...], preferred_element_type=F32)
        act = (gate * jax.nn.sigmoid(gate) * up).astype(BF16)
        y = jnp.dot(act, wd_s[...], preferred_element_type=F32)
        yd_ref[rows, :] = _pack_bf16_pairs(y.astype(BF16))

    def experts_ragged(row0, lead_rows, rows_last):
        half = MOE_BM // 2

        @pl.when(rows_last > half)
        def _():
            experts_on(row0, lead_rows + MOE_BM, lead_rows + rows_last)

        @pl.when(rows_last <= half)
        def _():
            experts_on(row0, lead_rows + half, lead_rows + rows_last)
            yd_ref[pl.ds(row0 + lead_rows + half, half), :] = jnp.zeros((half, D_MODEL // 2), jnp.int32)

    def single_block(i, row0):
        enter_expert(i)

        @pl.when(rv_ref[i] > 0)
        def _():
            experts_ragged(row0, 0, rv_ref[i])

        @pl.when(rv_ref[i] <= 0)
        def _():
            yd_ref[pl.ds(row0, MOE_BM), :] = jnp.zeros((MOE_BM, D_MODEL // 2), jnp.int32)

    @pl.when(rv_ref[step * MOE_STEP_BLOCKS] > 0)
    def _():
        for pair in range(MOE_STEP_BLOCKS // 2):
            ia = step * MOE_STEP_BLOCKS + 2 * pair
            ib = ia + 1
            row0 = 2 * pair * MOE_BM
            same = (be_ref[ib] == be_ref[ia]) & (rv_ref[ib] > 0)

            @pl.when(same)
            def _():
                enter_expert(ia)
                experts_ragged(row0, MOE_BM, rv_ref[ib])

            @pl.when(jnp.logical_not(same))
            def _():
                single_block(ia, row0)
                single_block(ib, row0 + MOE_BM)


def _moe_experts(l, block_e, rows_valid, next_e, slot, first_e, xd, w_gate, w_up, w_down):
    n_blocks = xd.shape[0] // MOE_BM
    step_rows = MOE_STEP_BLOCKS * MOE_BM
    row = lambda i, be, rv, nx, sl, fe: (jnp.minimum(i, fe[WEIGHT_SLOTS - 1] - 1), 0)
    return pl.pallas_call(
        functools.partial(_moe_kernel, layer=l),
        grid_spec=pltpu.PrefetchScalarGridSpec(
            num_scalar_prefetch=5,
            grid=(n_blocks // MOE_STEP_BLOCKS,),
            in_specs=[
                pl.BlockSpec((step_rows, D_MODEL // 2), row),
                pl.BlockSpec(memory_space=pl.ANY),
                pl.BlockSpec(memory_space=pl.ANY),
                pl.BlockSpec(memory_space=pl.ANY),
            ],
            out_specs=pl.BlockSpec((step_rows, D_MODEL // 2), row),
            scratch_shapes=[
                pltpu.VMEM((WEIGHT_SLOTS, D_MODEL, EXPERT_FF), F32),
                pltpu.VMEM((WEIGHT_SLOTS, D_MODEL, EXPERT_FF), F32),
                pltpu.VMEM((WEIGHT_SLOTS, EXPERT_FF, D_MODEL), F32),
                pltpu.VMEM((D_MODEL, EXPERT_FF), BF16),
                pltpu.VMEM((D_MODEL, EXPERT_FF), BF16),
                pltpu.VMEM((EXPERT_FF, D_MODEL), BF16),
                pltpu.SemaphoreType.DMA((WEIGHT_SLOTS, 3)),
            ],
        ),
        out_shape=jax.ShapeDtypeStruct((n_blocks * MOE_BM, D_MODEL // 2), jnp.int32),
        compiler_params=pltpu.CompilerParams(
            dimension_semantics=("arbitrary",), vmem_limit_bytes=VMEM_LIMIT),
        name="moe_experts",
    )(block_e, rows_valid, next_e, slot, first_e, xd, w_gate, w_up, w_down)


def _sc_worker_id():
    return lax.axis_index("s") * SC_CORES + lax.axis_index("c")


def _sc_dispatch(hp, hs, dest_p, dest_s, n_rows):
    tp, width = hp.shape
    per_w = tp // SC_WORKERS
    n_ch = per_w // DISP_CH
    n_sw = hs.shape[0] // SAMPLE_CH
    mesh = plsc.VectorSubcoreMesh(core_axis_name="c", subcore_axis_name="s")

    @functools.partial(
        pl.kernel, mesh=mesh,
        out_type=jax.ShapeDtypeStruct((n_rows, width), jnp.int32),
        scratch_types=[
            pltpu.VMEM((2, n_ch, DISP_CH), jnp.int32),
            pltpu.VMEM((2, 1, SAMPLE_CH), jnp.int32),
            pltpu.VMEM((SC_RING, DISP_CH, width), jnp.int32),
            pltpu.SemaphoreType.DMA((SC_RING,)),
            pltpu.SemaphoreType.DMA((SC_RING, 2)),
        ],
        name="sc_dispatch",
    )
    def k(hp_hbm, hs_hbm, dp_hbm, ds_hbm, xd_hbm, idx_v, idxs_v, bufs, rsem, wsem):
        wid = _sc_worker_id()
        base = wid * per_w
        for kk in range(2):
            pltpu.sync_copy(dp_hbm.at[kk, wid], idx_v.at[kk])
        reads = [pltpu.make_async_copy(hp_hbm.at[pl.ds(base + j * DISP_CH, DISP_CH)],
                                       bufs.at[j % SC_RING], rsem.at[j % SC_RING]) for j in range(n_ch)]
        writes = [[pltpu.make_async_copy(bufs.at[j % SC_RING], xd_hbm.at[idx_v.at[kk, j]],
                                         wsem.at[j % SC_RING, kk]) for kk in range(2)] for j in range(n_ch)]
        for j in range(min(SC_RING - 1, n_ch)):
            reads[j].start()
        for j in range(n_ch):
            reads[j].wait()
            for w in writes[j]:
                w.start()
            if j >= 1:
                for w in writes[j - 1]:
                    w.wait()
            if j + SC_RING - 1 < n_ch:
                reads[j + SC_RING - 1].start()
        for w in writes[n_ch - 1]:
            w.wait()

        @pl.when(wid < n_sw)
        def _():
            rows = bufs.at[0, pl.ds(0, SAMPLE_CH)]
            for kk in range(2):
                pltpu.sync_copy(ds_hbm.at[kk, wid], idxs_v.at[kk])
            pltpu.sync_copy(hs_hbm.at[pl.ds(wid * SAMPLE_CH, SAMPLE_CH)], rows)
            for kk in range(2):
                pltpu.sync_copy(rows, xd_hbm.at[idxs_v.at[kk, 0]])

    return k(hp, hs, dest_p, dest_s)


def _sc_sample_gather(yd, dest_s, ts):
    width = yd.shape[1]
    n_sw = ts // SAMPLE_CH
    mesh = plsc.VectorSubcoreMesh(core_axis_name="c", subcore_axis_name="s")

    @functools.partial(
        pl.kernel, mesh=mesh,
        out_type=jax.ShapeDtypeStruct((2, ts, width), yd.dtype),
        scratch_types=[
            pltpu.VMEM((2, 1, SAMPLE_CH), jnp.int32),
            pltpu.VMEM((2, SAMPLE_CH, width), yd.dtype),
        ],
        name="sc_sample_gather",
    )
    def k(yd_hbm, ds_hbm, g_hbm, idxs_v, bufs):
        wid = _sc_worker_id()

        @pl.when(wid < n_sw)
        def _():
            for kk in range(2):
                pltpu.sync_copy(ds_hbm.at[kk, wid], idxs_v.at[kk])
            for kk in range(2):
                pltpu.sync_copy(yd_hbm.at[idxs_v.at[kk, 0]], bufs.at[kk])
                pltpu.sync_copy(bufs.at[kk], g_hbm.at[kk, pl.ds(wid * SAMPLE_CH, SAMPLE_CH)])

    return k(yd, dest_s)


def _sc_combine_gather(yd, dest_p, tp):
    width = yd.shape[1]
    per_w = tp // SC_WORKERS
    n_ch = per_w // COMB_CH
    mesh = plsc.VectorSubcoreMesh(core_axis_name="c", subcore_axis_name="s")

    @functools.partial(
        pl.kernel, mesh=mesh,
        out_type=jax.ShapeDtypeStruct((2, tp, width), yd.dtype),
        scratch_types=[
            pltpu.VMEM((2, n_ch, COMB_CH), jnp.int32),
            pltpu.VMEM((SC_RING, COMB_CH, width), yd.dtype),
            pltpu.SemaphoreType.DMA((SC_RING,)),
            pltpu.SemaphoreType.DMA((SC_RING,)),
        ],
        name="sc_combine_gather",
    )
    def k(yd_hbm, dp_hbm, g_hbm, idx_v, bufs, gsem, wsem):
        wid = _sc_worker_id()
        base = wid * per_w
        for kk in range(2):
            pltpu.sync_copy(dp_hbm.at[kk, wid], idx_v.at[kk])
        items = [(kk, j) for kk in range(2) for j in range(n_ch)]
        n_items = len(items)
        gathers = [pltpu.make_async_copy(yd_hbm.at[idx_v.at[kk, j]], bufs.at[n % SC_RING], gsem.at[n % SC_RING])
                   for n, (kk, j) in enumerate(items)]
        outs = [pltpu.make_async_copy(bufs.at[n % SC_RING], g_hbm.at[kk, pl.ds(base + j * COMB_CH, COMB_CH)],
                                      wsem.at[n % SC_RING]) for n, (kk, j) in enumerate(items)]
        for n in range(min(SC_RING - 1, n_items)):
            gathers[n].start()
        for n in range(n_items):
            gathers[n].wait()
            outs[n].start()
            if n >= 1:
                outs[n - 1].wait()
            if n + SC_RING - 1 < n_items:
                gathers[n + SC_RING - 1].start()
        outs[n_items - 1].wait()

    return k(yd, dest_p)


def _combine_kernel(x1_ref, g_ref, route_t_ref, x2_ref):
    fields = route_t_ref[...]
    tm = fields.shape[1]
    cols = jnp.transpose(jnp.concatenate([fields, jnp.zeros((LANES - ROUTE_FIELDS, tm), F32)], axis=0))
    w1 = cols[:, 2:3]
    w2 = cols[:, 3:4]
    x2_ref[...] = x1_ref[...] + _unpack_bf16_pairs(g_ref[0]) * w1 + _unpack_bf16_pairs(g_ref[1]) * w2


def _combine(x1, g, route_t, row0, tm):
    t = x1.shape[0]
    blk0 = row0 // tm
    row = lambda i: (i, 0)
    return pl.pallas_call(
        _combine_kernel,
        grid=(t // tm,),
        in_specs=[
            pl.BlockSpec((tm, D_MODEL), row),
            pl.BlockSpec((2, tm, D_MODEL // 2), lambda i: (0, blk0 + i, 0)),
            pl.BlockSpec((ROUTE_FIELDS, tm), lambda i: (0, i)),
        ],
        out_specs=pl.BlockSpec((tm, D_MODEL), row),
        out_shape=jax.ShapeDtypeStruct((t, D_MODEL), F32),
        compiler_params=pltpu.CompilerParams(
            dimension_semantics=("arbitrary",), vmem_limit_bytes=VMEM_LIMIT),
        name="combine",
    )(x1, g, route_t)


def _dest_layout(dest, workers, chunk):
    t = dest.shape[1]
    return dest.reshape(2, workers, t // (workers * chunk), chunk)


def _hier_moe(l, h2p, h2s, route_tp, route_ts, counts, w_gate, w_up, w_down):
    tp, ts = h2p.shape[0], h2s.shape[0]
    n_assign = 2 * (tp + ts)
    n_blocks = -(-n_assign // MOE_BM) + N_EXPERTS
    n_blocks = -(-n_blocks // MOE_STEP_BLOCKS) * MOE_STEP_BLOCKS
    pcounts = (counts + MOE_BM - 1) // MOE_BM * MOE_BM
    pends = jnp.cumsum(pcounts)
    poffsets = pends - pcounts
    starts = jnp.arange(n_blocks, dtype=jnp.int32) * MOE_BM
    block_e = jnp.minimum(jnp.sum((pends[None, :] <= starts[:, None]).astype(jnp.int32), axis=1),
                          N_EXPERTS - 1)
    experts = jnp.arange(N_EXPERTS, dtype=jnp.int32)

    def lookup(table, idx):
        return jnp.sum(jnp.where(idx[..., None] == experts, table, 0), axis=-1)

    rows_valid = jnp.clip(lookup(poffsets + counts, block_e) - starts, 0, MOE_BM).astype(jnp.int32)
    used = counts > 0
    last_e = jnp.max(jnp.where(used, jnp.arange(N_EXPERTS, dtype=jnp.int32), 0))
    block_e = jnp.where(rows_valid > 0, block_e, last_e).astype(jnp.int32)
    place = jnp.cumsum(used.astype(jnp.int32)) - 1
    by_place = jnp.sum(jnp.where(used[None, :] & (place[None, :] == experts[:, None]), experts[None, :], 0),
                       axis=1)
    n_used = jnp.sum(used.astype(jnp.int32))

    def at_place(p):
        return jnp.where(p < n_used, lookup(by_place, jnp.minimum(p, N_EXPERTS - 1)), -1).astype(jnp.int32)

    ahead_of = at_place(place + (WEIGHT_SLOTS - 1))
    next_e = lookup(ahead_of, block_e)
    slot = lookup(place % WEIGHT_SLOTS, block_e)
    n_steps_used = -(-(pends[-1] // MOE_BM) // MOE_STEP_BLOCKS)
    first_e = jnp.concatenate([at_place(jnp.arange(WEIGHT_SLOTS - 1, dtype=jnp.int32)),
                               n_steps_used.reshape(1).astype(jnp.int32)])

    def dest_of(route_t):
        return lookup(poffsets, route_t[0:2].astype(jnp.int32)) + route_t[4:6].astype(jnp.int32)

    dest_p, dest_s = dest_of(route_tp), dest_of(route_ts)
    n_sw = ts // SAMPLE_CH
    xd = _sc_dispatch(h2p, h2s, _dest_layout(dest_p, SC_WORKERS, DISP_CH),
                      _dest_layout(dest_s, n_sw, SAMPLE_CH), n_blocks * MOE_BM)
    yd = _moe_experts(l, block_e, rows_valid, next_e, slot, first_e, xd, w_gate, w_up, w_down)
    g_s = _sc_sample_gather(yd, _dest_layout(dest_s, n_sw, SAMPLE_CH), ts)
    g_p = _sc_combine_gather(yd, _dest_layout(dest_p, SC_WORKERS, COMB_CH), tp)
    return g_p, g_s


def kernel(x_prompt, x_sample, state_pool, cache_k_win, cache_v_win, norm_attn_g, w_in, pool_w, pool_scale, q_norm_g, k_norm_g, attn_sinks, w_out, norm_ffn_g, router_group_w, router_group_b, router_expert_w, router_expert_b, w_gate, w_up, w_down):
    n_p, t_p, d = x_prompt.shape
    n_s, t_s, _ = x_sample.shape
    depth = w_in.shape[0]
    lw_s = cache_k_win.shape[2]
    assert t_s == 1 and lw_s == WINDOW and d == D_MODEL
    assert t_p % TM_PROJ == 0 and t_p >= WINDOW

    seg = jnp.arange(256) // HEAD_DIM
    bd = jnp.where(seg[:, None] == seg[None, :], 1.0 / HEAD_DIM, 0.0).astype(BF16)
    slopes = jnp.exp2(-8.0 * jnp.arange(1, N_HEADS + 1, dtype=F32) / N_HEADS)
    bias_p = _prompt_bias_t()
    dist_s = (WINDOW - 1) - jnp.arange(WINDOW, dtype=F32)
    bias_s = -LOG2E * slopes[:, None] * dist_s[None, :]

    wp = jnp.zeros((depth, 2, 256, 256), F32)
    for p in range(2):
        wp = wp.at[:, p, :POOL_GC, :POOL_GC].set(pool_w[:, 2 * p])
        wp = wp.at[:, p, POOL_GC:, POOL_GC:].set(pool_w[:, 2 * p + 1])
    assert GROUP_LANE0 == N_EXPERTS
    lane_pad = LANES - N_EXPERTS - N_EXPERT_GROUPS
    wr = jnp.concatenate([router_expert_w, router_group_w, jnp.zeros((depth, D_MODEL, lane_pad), F32)], axis=-1)
    br = jnp.concatenate([router_expert_b, router_group_b, jnp.zeros((depth, lane_pad), F32)],
                         axis=-1).reshape(depth, 1, LANES)
    lp = dict(
        w_in=w_in.astype(BF16),
        w_out=w_out.astype(BF16),
        g_attn=norm_attn_g.reshape(depth, 1, D_MODEL),
        g_ffn=norm_ffn_g.reshape(depth, 1, D_MODEL),
        qg=(jnp.tile(q_norm_g, (1, N_HEADS)) * (ATTN_SCALE * LOG2E)).reshape(depth, 1, Q_W),
        kg=jnp.tile(k_norm_g, (1, N_KV_HEADS)).reshape(depth, 1, KV_W),
        wp=wp.astype(BF16),
        ps=pool_scale.reshape(depth, 1, POOL_W),
        wr=wr.astype(BF16),
        br=br,
        state=state_pool,
        ck=cache_k_win.reshape(depth, n_s, lw_s, KV_W),
        cv=cache_v_win.reshape(depth, n_s, lw_s, KV_W),
    )

    xp = x_prompt.reshape(n_p * t_p, D_MODEL)
    xs = x_sample.reshape(n_s, D_MODEL)
    lw_p = min(WINDOW, t_p)
    pool_p, kp_new, vp_new = [], [], []
    sample_state = [lp["state"], lp["ck"], lp["cv"]]
    zero_cnt = jnp.zeros((N_EXPERTS, 1), F32)
    pending = None
    for l in range(depth):
        sinks = attn_sinks[l] * LOG2E
        pool_so, attn_so, *sample_state = _sample_mixer(
            l, depth, xs, lp["g_attn"], lp["w_in"], lp["qg"], lp["kg"], bd, lp["wp"], lp["ps"],
            *sample_state, sinks.reshape(N_HEADS, 1), bias_s, PAST_LEN)
        x1s, h2s, route_ts, cnt_s = _merge_router(
            l, pool_so, attn_so, xs, lp["w_out"], lp["g_ffn"], lp["wr"], lp["br"], zero_cnt, n_s)
        outs = _proj_pool_prompt(
            l, xp if pending is None else pending, n_p, t_p,
            lp["g_attn"], lp["w_in"], lp["qg"], lp["kg"], bd, lp["wp"], lp["ps"])
        if pending is not None:
            xp, outs = outs[0], outs[1:]
        pool_o, q, k, vt, utail, ktail, vtail = outs
        attn_o = _attn_prompt(q, k, vt, bias_p, sinks, n_p, t_p)
        x1p, h2p, route_tp, cnt_all = _merge_router(
            l, pool_o, attn_o, xp, lp["w_out"], lp["g_ffn"], lp["wr"], lp["br"], cnt_s, TM_MERGE)
        pool_p.append(utail[:, 16 - POOL_STATE:, :])
        kp_new.append(ktail)
        vp_new.append(vtail)
        counts = cnt_all[:, 0].astype(jnp.int32)
        g_p, g_s = _hier_moe(l, h2p, h2s, route_tp, route_ts, counts, w_gate, w_up, w_down)
        xs = _combine(x1s, g_s, route_ts, 0, n_s)
        pending = (x1p, g_p, route_tp)
    xp = _combine(*pending, 0, TM_MERGE)
    return (xp.reshape(n_p, t_p, D_MODEL), xs.reshape(n_s, t_s, D_MODEL),
            jnp.stack(pool_p),
            jnp.stack(kp_new).reshape(depth, n_p, lw_p, N_KV_HEADS, HEAD_DIM),
            jnp.stack(vp_new).reshape(depth, n_p, lw_p, N_KV_HEADS, HEAD_DIM),
            sample_state[0],
            sample_state[1].reshape(depth, n_s, lw_s, N_KV_HEADS, HEAD_DIM),
            sample_state[2].reshape(depth, n_s, lw_s, N_KV_HEADS, HEAD_DIM))
```
